```python
import jax, jax.numpy as jnp
from jax import lax
import numpy as np

D_MODEL = 1024
BATCH = 8
SEQ = 4096
DEPTH = 2

CHUNK = 128
RET_HEADS = 4
RET_HEAD_DIM = 128
RET_W = RET_HEADS * RET_HEAD_DIM
SB_HEADS = 8
SB_HEAD_DIM = 64
SB_W = SB_HEADS * SB_HEAD_DIM
SGU_GROUPS = 4
SGU_GROUP_DIM = 128
SGU_W = SGU_GROUPS * SGU_GROUP_DIM
D_FF = 4 * D_MODEL
ROPE_BASE = 10000.0
LN_EPS = 1e-5
DEEPNORM_ALPHA = (2 * DEPTH) ** 0.25
DEEPNORM_BETA = (8 * DEPTH) ** -0.25
SPLITS = (RET_W, RET_W, RET_W, RET_W, SB_W, SB_W, SB_W, SGU_W, SGU_W, D_MODEL, D_MODEL, D_MODEL)
N_IN = sum(SPLITS)

kernel_name = "hybrid_ret_sb_sgu_deepnorm"


def layer_norm(x, g, b):
    xf = x.astype(jnp.float32)
    mu = jnp.mean(xf, axis=-1, keepdims=True)
    var = jnp.mean(jnp.square(xf - mu), axis=-1, keepdims=True)
    y = (xf - mu) * lax.rsqrt(var + LN_EPS)
    return (y * g.astype(jnp.float32) + b.astype(jnp.float32)).astype(x.dtype)


def head_group_norm(o, g, b):
    B, S, H, D = o.shape
    of = o.astype(jnp.float32)
    mu = jnp.mean(of, axis=-1, keepdims=True)
    var = jnp.mean(jnp.square(of - mu), axis=-1, keepdims=True)
    y = ((of - mu) * lax.rsqrt(var + LN_EPS)).reshape(B, S, H * D)
    return (y * g.astype(jnp.float32) + b.astype(jnp.float32)).astype(o.dtype)


def rotary(x, pos):
    half = x.shape[-1] // 2
    inv_freq = ROPE_BASE ** (-jnp.arange(half, dtype=jnp.float32) / half)
    ang = pos.astype(jnp.float32)[:, None] * inv_freq[None, :]
    cos = jnp.cos(ang)[None, :, None, :].astype(x.dtype)
    sin = jnp.sin(ang)[None, :, None, :].astype(x.dtype)
    x1, x2 = x[..., :half], x[..., half:]
    return jnp.concatenate([x1 * cos - x2 * sin, x2 * cos + x1 * sin], axis=-1)


def retention(q, k, v):
    B, S, H, D = q.shape
    N = S // CHUNK
    dt = q.dtype
    log_g = jnp.log(1.0 - 2.0 ** (-5.0 - jnp.arange(H, dtype=jnp.float32)))
    idx = jnp.arange(CHUNK, dtype=jnp.float32)
    diff = idx[:, None] - idx[None, :]
    intra_decay = jnp.where(diff[None] >= 0, jnp.exp(log_g[:, None, None] * diff[None]), 0.0).astype(dt)
    k_decay = jnp.exp(log_g[:, None] * (CHUNK - 1 - idx)[None, :]).T.astype(dt)
    q_decay = jnp.exp(log_g[:, None] * (idx + 1.0)[None, :]).T.astype(dt)
    chunk_decay = jnp.exp(log_g * CHUNK).astype(dt)

    qc = q.reshape(B, N, CHUNK, H, D)
    kc = k.reshape(B, N, CHUNK, H, D)
    vc = v.reshape(B, N, CHUNK, H, D)

    scores = jnp.einsum('bnihd,bnjhd->bnhij', qc, kc) * intra_decay
    intra = jnp.einsum('bnhij,bnjhe->bnihe', scores, vc)

    kv = jnp.einsum('bnjhd,bnjhe->nbhde', kc * k_decay[:, :, None], vc)

    def step(state, kv_n):
        new_state = state * chunk_decay[None, :, None, None] + kv_n
        return new_state, state

    state0 = jnp.zeros((B, H, D, D), dt)
    _, prev_states = lax.scan(step, state0, kv)
    inter = jnp.einsum('bnihd,nbhde->bnihe', qc * q_decay[:, :, None], prev_states)
    return (intra + inter).reshape(B, S, H, D)


def stick_breaking(q, k, v):
    B, S, H, D = q.shape
    NB = S // CHUNK
    scale = D ** -0.5
    qb = q.reshape(B, NB, CHUNK, H, D).transpose(1, 0, 2, 3, 4)
    s_pos = jnp.arange(S)

    def block(args):
        qn, n = args
        z = jnp.einsum('bihd,bshd->bhis', qn, k).astype(jnp.float32) * scale
        t_pos = n * CHUNK + jnp.arange(CHUNK)
        mask = s_pos[None, :] < t_pos[:, None]
        log_1m_beta = jnp.where(mask, jax.nn.log_sigmoid(-z), 0.0)
        later = lax.cumsum(log_1m_beta, axis=3, reverse=True) - log_1m_beta
        a = jnp.where(mask, jnp.exp(jax.nn.log_sigmoid(z) + later), 0.0)
        return jnp.einsum('bhis,bshd->bihd', a.astype(v.dtype), v)

    out = lax.map(block, (qb, jnp.arange(NB)))
    return out.transpose(1, 0, 2, 3, 4).reshape(B, S, H * D)


def chunked_sgu(u, v, ln_g, ln_b, w_s, b_s):
    B, S, _ = v.shape
    N = S // CHUNK
    v = layer_norm(v, ln_g, ln_b)
    vg = v.reshape(B, N, CHUNK, SGU_GROUPS, SGU_GROUP_DIM)
    causal = jnp.tril(jnp.ones((CHUNK, CHUNK), dtype=w_s.dtype))
    w = w_s * causal[None]
    sv = jnp.einsum('gij,bnjgc->bnigc', w, vg) + b_s.T[None, None, :, :, None]
    return u * sv.reshape(B, S, SGU_W)


def mixer(x, w_in, ret_gn_g, ret_gn_b, sgu_ln_g, sgu_ln_b, sgu_w, sgu_b, p_ret, p_sb, p_sgu, w_out):
    B, S, _ = x.shape
    proj = x @ w_in
    points = [int(p) for p in np.cumsum(SPLITS)[:-1]]
    (rq, rk, rv, rg, sq, sk, sv, gu, gv, gate_ret, gate_sb, gate_sgu) = jnp.split(proj, points, axis=-1)
    pos = jnp.arange(S, dtype=jnp.int32)

    rq = rotary(rq.reshape(B, S, RET_HEADS, RET_HEAD_DIM), pos)
    rk = rotary(rk.reshape(B, S, RET_HEADS, RET_HEAD_DIM), pos) * (RET_HEAD_DIM ** -0.5)
    ret = retention(rq, rk, rv.reshape(B, S, RET_HEADS, RET_HEAD_DIM))
    ret = jax.nn.silu(rg) * head_group_norm(ret, ret_gn_g, ret_gn_b)

    sb = stick_breaking(sq.reshape(B, S, SB_HEADS, SB_HEAD_DIM),
                        sk.reshape(B, S, SB_HEADS, SB_HEAD_DIM),
                        sv.reshape(B, S, SB_HEADS, SB_HEAD_DIM))

    sg = chunked_sgu(jax.nn.gelu(gu), jax.nn.gelu(gv), sgu_ln_g, sgu_ln_b, sgu_w, sgu_b)

    merged = (jax.nn.sigmoid(gate_ret) * (ret @ p_ret)
              + jax.nn.sigmoid(gate_sb) * (sb @ p_sb)
              + jax.nn.sigmoid(gate_sgu) * (sg @ p_sgu))
    return merged @ w_out


def _fwd_setup_inputs(seed: int = 0) -> dict:
    key = jax.random.key(seed)
    ks = jax.random.split(key, 20)
    L = DEPTH
    f32 = jnp.float32

    def nrm(k, shape, scale):
        return jax.random.normal(k, shape, f32) * scale

    return {
        "x": jax.random.normal(ks[0], (BATCH, SEQ, D_MODEL), f32),
        "w_in": nrm(ks[1], (L, D_MODEL, N_IN), D_MODEL ** -0.5),
        "ret_gn_g": 1.0 + nrm(ks[2], (L, RET_W), 0.02),
        "ret_gn_b": nrm(ks[3], (L, RET_W), 0.02),
        "sgu_ln_g": 1.0 + nrm(ks[4], (L, SGU_W), 0.02),
        "sgu_ln_b": nrm(ks[5], (L, SGU_W), 0.02),
        "sgu_w": nrm(ks[6], (L, SGU_GROUPS, CHUNK, CHUNK), CHUNK ** -0.5),
        "sgu_b": 1.0 + nrm(ks[7], (L, SGU_GROUPS, CHUNK), 0.01),
        "p_ret": nrm(ks[8], (L, RET_W, D_MODEL), RET_W ** -0.5 * DEEPNORM_BETA),
        "p_sb": nrm(ks[9], (L, SB_W, D_MODEL), SB_W ** -0.5 * DEEPNORM_BETA),
        "p_sgu": nrm(ks[10], (L, SGU_W, D_MODEL), SGU_W ** -0.5 * DEEPNORM_BETA),
        "w_out": nrm(ks[11], (L, D_MODEL, D_MODEL), D_MODEL ** -0.5 * DEEPNORM_BETA),
        "ln1_g": 1.0 + nrm(ks[12], (L, D_MODEL), 0.02),
        "ln1_b": nrm(ks[13], (L, D_MODEL), 0.02),
        "w_up": nrm(ks[14], (L, D_MODEL, D_FF), D_MODEL ** -0.5 * DEEPNORM_BETA),
        "w_down": nrm(ks[15], (L, D_FF, D_MODEL), D_FF ** -0.5 * DEEPNORM_BETA),
        "ln2_g": 1.0 + nrm(ks[16], (L, D_MODEL), 0.02),
        "ln2_b": nrm(ks[17], (L, D_MODEL), 0.02),
    }


def _fwd_reference(x, w_in, ret_gn_g, ret_gn_b, sgu_ln_g, sgu_ln_b, sgu_w, sgu_b, p_ret, p_sb, p_sgu,
              w_out, ln1_g, ln1_b, w_up, w_down, ln2_g, ln2_b):
    for l in range(DEPTH):
        y = mixer(x, w_in[l], ret_gn_g[l], ret_gn_b[l], sgu_ln_g[l], sgu_ln_b[l], sgu_w[l], sgu_b[l],
                  p_ret[l], p_sb[l], p_sgu[l], w_out[l])
        x = layer_norm(DEEPNORM_ALPHA * x + y, ln1_g[l], ln1_b[l])
        h = jnp.square(jax.nn.relu(x @ w_up[l])) @ w_down[l]
        x = layer_norm(DEEPNORM_ALPHA * x + h, ln2_g[l], ln2_b[l])
    return x


import jax as _jax
import jax.numpy as _jnp

TWIN_FORMAT = 'train_step'
FWD_PARAMS = ['x', 'w_in', 'ret_gn_g', 'ret_gn_b', 'sgu_ln_g', 'sgu_ln_b', 'sgu_w', 'sgu_b', 'p_ret', 'p_sb', 'p_sgu', 'w_out', 'ln1_g', 'ln1_b', 'w_up', 'w_down', 'ln2_g', 'ln2_b']
TWIN_WEIGHTS = ['w_in', 'ret_gn_g', 'ret_gn_b', 'sgu_ln_g', 'sgu_ln_b', 'sgu_w', 'sgu_b', 'p_ret', 'p_sb', 'p_sgu', 'w_out', 'ln1_g', 'ln1_b', 'w_up', 'w_down', 'ln2_g', 'ln2_b']
TWIN_DIFF_INPUT = 'x'
TWIN_INPUTS = ['x', 'w_in', 'ret_gn_g', 'ret_gn_b', 'sgu_ln_g', 'sgu_ln_b', 'sgu_w', 'sgu_b', 'p_ret', 'p_sb', 'p_sgu', 'w_out', 'ln1_g', 'ln1_b', 'w_up', 'w_down', 'ln2_g', 'ln2_b', 'loss_target', 'm_w_in', 'm_ret_gn_g', 'm_ret_gn_b', 'm_sgu_ln_g', 'm_sgu_ln_b', 'm_sgu_w', 'm_sgu_b', 'm_p_ret', 'm_p_sb', 'm_p_sgu', 'm_w_out', 'm_ln1_g', 'm_ln1_b', 'm_w_up', 'm_w_down', 'm_ln2_g', 'm_ln2_b', 'v_w_in', 'v_ret_gn_g', 'v_ret_gn_b', 'v_sgu_ln_g', 'v_sgu_ln_b', 'v_sgu_w', 'v_sgu_b', 'v_p_ret', 'v_p_sb', 'v_p_sgu', 'v_w_out', 'v_ln1_g', 'v_ln1_b', 'v_w_up', 'v_w_down', 'v_ln2_g', 'v_ln2_b']
TWIN_OUTPUTS = ['loss', 'grad_x', 'grad_w_in', 'grad_ret_gn_g', 'grad_ret_gn_b', 'grad_sgu_ln_g', 'grad_sgu_ln_b', 'grad_sgu_w', 'grad_sgu_b', 'grad_p_ret', 'grad_p_sb', 'grad_p_sgu', 'grad_w_out', 'grad_ln1_g', 'grad_ln1_b', 'grad_w_up', 'grad_w_down', 'grad_ln2_g', 'grad_ln2_b', 'delta_w_in', 'delta_ret_gn_g', 'delta_ret_gn_b', 'delta_sgu_ln_g', 'delta_sgu_ln_b', 'delta_sgu_w', 'delta_sgu_b', 'delta_p_ret', 'delta_p_sb', 'delta_p_sgu', 'delta_w_out', 'delta_ln1_g', 'delta_ln1_b', 'delta_w_up', 'delta_w_down', 'delta_ln2_g', 'delta_ln2_b', 'new_m_w_in', 'new_m_ret_gn_g', 'new_m_ret_gn_b', 'new_m_sgu_ln_g', 'new_m_sgu_ln_b', 'new_m_sgu_w', 'new_m_sgu_b', 'new_m_p_ret', 'new_m_p_sb', 'new_m_p_sgu', 'new_m_w_out', 'new_m_ln1_g', 'new_m_ln1_b', 'new_m_w_up', 'new_m_w_down', 'new_m_ln2_g', 'new_m_ln2_b', 'new_v_w_in', 'new_v_ret_gn_g', 'new_v_ret_gn_b', 'new_v_sgu_ln_g', 'new_v_sgu_ln_b', 'new_v_sgu_w', 'new_v_sgu_b', 'new_v_p_ret', 'new_v_p_sb', 'new_v_p_sgu', 'new_v_w_out', 'new_v_ln1_g', 'new_v_ln1_b', 'new_v_w_up', 'new_v_w_down', 'new_v_ln2_g', 'new_v_ln2_b']
TWIN_LEAF_KINDS = {'loss': 'loss', 'grad_x': 'grad_x', 'grad_w_in': 'grad_w', 'grad_ret_gn_g': 'grad_w', 'grad_ret_gn_b': 'grad_w', 'grad_sgu_ln_g': 'grad_w', 'grad_sgu_ln_b': 'grad_w', 'grad_sgu_w': 'grad_w', 'grad_sgu_b': 'grad_w', 'grad_p_ret': 'grad_w', 'grad_p_sb': 'grad_w', 'grad_p_sgu': 'grad_w', 'grad_w_out': 'grad_w', 'grad_ln1_g': 'grad_w', 'grad_ln1_b': 'grad_w', 'grad_w_up': 'grad_w', 'grad_w_down': 'grad_w', 'grad_ln2_g': 'grad_w', 'grad_ln2_b': 'grad_w', 'delta_w_in': 'delta_w', 'delta_ret_gn_g': 'delta_w', 'delta_ret_gn_b': 'delta_w', 'delta_sgu_ln_g': 'delta_w', 'delta_sgu_ln_b': 'delta_w', 'delta_sgu_w': 'delta_w', 'delta_sgu_b': 'delta_w', 'delta_p_ret': 'delta_w', 'delta_p_sb': 'delta_w', 'delta_p_sgu': 'delta_w', 'delta_w_out': 'delta_w', 'delta_ln1_g': 'delta_w', 'delta_ln1_b': 'delta_w', 'delta_w_up': 'delta_w', 'delta_w_down': 'delta_w', 'delta_ln2_g': 'delta_w', 'delta_ln2_b': 'delta_w', 'new_m_w_in': 'new_m', 'new_m_ret_gn_g': 'new_m', 'new_m_ret_gn_b': 'new_m', 'new_m_sgu_ln_g': 'new_m', 'new_m_sgu_ln_b': 'new_m', 'new_m_sgu_w': 'new_m', 'new_m_sgu_b': 'new_m', 'new_m_p_ret': 'new_m', 'new_m_p_sb': 'new_m', 'new_m_p_sgu': 'new_m', 'new_m_w_out': 'new_m', 'new_m_ln1_g': 'new_m', 'new_m_ln1_b': 'new_m', 'new_m_w_up': 'new_m', 'new_m_w_down': 'new_m', 'new_m_ln2_g': 'new_m', 'new_m_ln2_b': 'new_m', 'new_v_w_in': 'new_v', 'new_v_ret_gn_g': 'new_v', 'new_v_ret_gn_b': 'new_v', 'new_v_sgu_ln_g': 'new_v', 'new_v_sgu_ln_b': 'new_v', 'new_v_sgu_w': 'new_v', 'new_v_sgu_b': 'new_v', 'new_v_p_ret': 'new_v', 'new_v_p_sb': 'new_v', 'new_v_p_sgu': 'new_v', 'new_v_w_out': 'new_v', 'new_v_ln1_g': 'new_v', 'new_v_ln1_b': 'new_v', 'new_v_w_up': 'new_v', 'new_v_w_down': 'new_v', 'new_v_ln2_g': 'new_v', 'new_v_ln2_b': 'new_v'}


def _forward(args):
    return _fwd_reference(*[args[k] for k in FWD_PARAMS])


def _output_shape():
    def fwd():
        inp = _fwd_setup_inputs(0)
        return _fwd_reference(*[inp[k] for k in FWD_PARAMS])
    out = _jax.eval_shape(fwd)
    return out.shape, out.dtype

N_MICROBATCH = 1
ADAM_LR = 0.001
ADAM_B1 = 0.9
ADAM_B2 = 0.999
ADAM_EPS = 1e-08
ADAM_WD = 0.01
ADAM_STEP = 10
PER_EXAMPLE_BATCH_AXIS = {'x': 0, 'loss_target': 0}
SHARED_INPUTS = []
_WEIGHT_DTYPES = {'w_in': _jnp.float32, 'ret_gn_g': _jnp.float32, 'ret_gn_b': _jnp.float32, 'sgu_ln_g': _jnp.float32, 'sgu_ln_b': _jnp.float32, 'sgu_w': _jnp.float32, 'sgu_b': _jnp.float32, 'p_ret': _jnp.float32, 'p_sb': _jnp.float32, 'p_sgu': _jnp.float32, 'w_out': _jnp.float32, 'ln1_g': _jnp.float32, 'ln1_b': _jnp.float32, 'w_up': _jnp.float32, 'w_down': _jnp.float32, 'ln2_g': _jnp.float32, 'ln2_b': _jnp.float32}
MOMENT_SCALE = {'w_in': 1.442615e-02, 'ret_gn_g': 1.446406e-02, 'ret_gn_b': 1.719321e-02, 'sgu_ln_g': 1.181612e-02, 'sgu_ln_b': 1.196979e-02, 'sgu_w': 1.107541e-02, 'sgu_b': 1.620257e-02, 'p_ret': 2.044836e-02, 'p_sb': 2.201698e-02, 'p_sgu': 3.545497e-02, 'w_out': 4.590889e-02, 'ln1_g': 1.151714e+00, 'ln1_b': 5.813955e-01, 'w_up': 2.247608e-02, 'w_down': 5.689542e-02, 'ln2_g': 2.267865e+01, 'ln2_b': 1.794330e+00}


def _to_microbatches(a, axis):
    t = _jnp.moveaxis(a, axis, 0)
    t = t.reshape((N_MICROBATCH, t.shape[0] // N_MICROBATCH) + t.shape[1:])
    return _jnp.moveaxis(t, 1, axis + 1)


def setup_inputs(seed: int = 0) -> dict:
    inp = _fwd_setup_inputs(seed)
    key = _jax.random.fold_in(_jax.random.key(seed), 7919)
    shape, _ = _output_shape()
    out = dict(inp)
    out["loss_target"] = _jax.random.normal(_jax.random.fold_in(key, 0), shape, _jnp.float32)
    for i, name in enumerate(TWIN_WEIGHTS):
        w = inp[name].astype(_jnp.float32)
        if MOMENT_SCALE is None:
            s = _jnp.sqrt(_jnp.mean(_jnp.square(w)) + 1e-30)
        else:
            s = MOMENT_SCALE[name]
        km, kv = _jax.random.split(_jax.random.fold_in(key, i + 1))
        out[name] = w
        out["m_" + name] = s * _jax.random.normal(km, w.shape, _jnp.float32)
        out["v_" + name] = (s * s) * _jax.random.uniform(kv, w.shape, _jnp.float32, 0.5, 1.5)
    if N_MICROBATCH > 1:
        for name, axis in PER_EXAMPLE_BATCH_AXIS.items():
            out[name] = _to_microbatches(out[name], axis)
    return {'x': out['x'], 'w_in': out['w_in'], 'ret_gn_g': out['ret_gn_g'], 'ret_gn_b': out['ret_gn_b'], 'sgu_ln_g': out['sgu_ln_g'], 'sgu_ln_b': out['sgu_ln_b'], 'sgu_w': out['sgu_w'], 'sgu_b': out['sgu_b'], 'p_ret': out['p_ret'], 'p_sb': out['p_sb'], 'p_sgu': out['p_sgu'], 'w_out': out['w_out'], 'ln1_g': out['ln1_g'], 'ln1_b': out['ln1_b'], 'w_up': out['w_up'], 'w_down': out['w_down'], 'ln2_g': out['ln2_g'], 'ln2_b': out['ln2_b'], 'loss_target': out['loss_target'], 'm_w_in': out['m_w_in'], 'm_ret_gn_g': out['m_ret_gn_g'], 'm_ret_gn_b': out['m_ret_gn_b'], 'm_sgu_ln_g': out['m_sgu_ln_g'], 'm_sgu_ln_b': out['m_sgu_ln_b'], 'm_sgu_w': out['m_sgu_w'], 'm_sgu_b': out['m_sgu_b'], 'm_p_ret': out['m_p_ret'], 'm_p_sb': out['m_p_sb'], 'm_p_sgu': out['m_p_sgu'], 'm_w_out': out['m_w_out'], 'm_ln1_g': out['m_ln1_g'], 'm_ln1_b': out['m_ln1_b'], 'm_w_up': out['m_w_up'], 'm_w_down': out['m_w_down'], 'm_ln2_g': out['m_ln2_g'], 'm_ln2_b': out['m_ln2_b'], 'v_w_in': out['v_w_in'], 'v_ret_gn_g': out['v_ret_gn_g'], 'v_ret_gn_b': out['v_ret_gn_b'], 'v_sgu_ln_g': out['v_sgu_ln_g'], 'v_sgu_ln_b': out['v_sgu_ln_b'], 'v_sgu_w': out['v_sgu_w'], 'v_sgu_b': out['v_sgu_b'], 'v_p_ret': out['v_p_ret'], 'v_p_sb': out['v_p_sb'], 'v_p_sgu': out['v_p_sgu'], 'v_w_out': out['v_w_out'], 'v_ln1_g': out['v_ln1_g'], 'v_ln1_b': out['v_ln1_b'], 'v_w_up': out['v_w_up'], 'v_w_down': out['v_w_down'], 'v_ln2_g': out['v_ln2_g'], 'v_ln2_b': out['v_ln2_b']}


def _loss(weights, diff, rest, loss_target):
    with _jax.named_scope("forward"):
        args = {**rest, TWIN_DIFF_INPUT: diff, **{k: w.astype(_WEIGHT_DTYPES[k]) for k, w in weights.items()}}
        y = _forward(args)
    with _jax.named_scope("loss_head"):
        err = _jnp.square(y.astype(_jnp.float32) - loss_target)
        return 0.5 * _jnp.sum(_jnp.mean(err, axis=-1)) if err.ndim else 0.5 * err


def _adamw(w, g, m, v):
    m = ADAM_B1 * m + (1.0 - ADAM_B1) * g
    v = ADAM_B2 * v + (1.0 - ADAM_B2) * _jnp.square(g)
    m_hat = m / (1.0 - ADAM_B1 ** ADAM_STEP)
    v_hat = v / (1.0 - ADAM_B2 ** ADAM_STEP)
    delta = -ADAM_LR * (m_hat / (_jnp.sqrt(v_hat) + ADAM_EPS) + ADAM_WD * w)
    return delta, m, v


def reference(x, w_in, ret_gn_g, ret_gn_b, sgu_ln_g, sgu_ln_b, sgu_w, sgu_b, p_ret, p_sb, p_sgu, w_out, ln1_g, ln1_b, w_up, w_down, ln2_g, ln2_b, loss_target, m_w_in, m_ret_gn_g, m_ret_gn_b, m_sgu_ln_g, m_sgu_ln_b, m_sgu_w, m_sgu_b, m_p_ret, m_p_sb, m_p_sgu, m_w_out, m_ln1_g, m_ln1_b, m_w_up, m_w_down, m_ln2_g, m_ln2_b, v_w_in, v_ret_gn_g, v_ret_gn_b, v_sgu_ln_g, v_sgu_ln_b, v_sgu_w, v_sgu_b, v_p_ret, v_p_sb, v_p_sgu, v_w_out, v_ln1_g, v_ln1_b, v_w_up, v_w_down, v_ln2_g, v_ln2_b):
    given = dict(x=x, w_in=w_in, ret_gn_g=ret_gn_g, ret_gn_b=ret_gn_b, sgu_ln_g=sgu_ln_g, sgu_ln_b=sgu_ln_b, sgu_w=sgu_w, sgu_b=sgu_b, p_ret=p_ret, p_sb=p_sb, p_sgu=p_sgu, w_out=w_out, ln1_g=ln1_g, ln1_b=ln1_b, w_up=w_up, w_down=w_down, ln2_g=ln2_g, ln2_b=ln2_b, loss_target=loss_target, m_w_in=m_w_in, m_ret_gn_g=m_ret_gn_g, m_ret_gn_b=m_ret_gn_b, m_sgu_ln_g=m_sgu_ln_g, m_sgu_ln_b=m_sgu_ln_b, m_sgu_w=m_sgu_w, m_sgu_b=m_sgu_b, m_p_ret=m_p_ret, m_p_sb=m_p_sb, m_p_sgu=m_p_sgu, m_w_out=m_w_out, m_ln1_g=m_ln1_g, m_ln1_b=m_ln1_b, m_w_up=m_w_up, m_w_down=m_w_down, m_ln2_g=m_ln2_g, m_ln2_b=m_ln2_b, v_w_in=v_w_in, v_ret_gn_g=v_ret_gn_g, v_ret_gn_b=v_ret_gn_b, v_sgu_ln_g=v_sgu_ln_g, v_sgu_ln_b=v_sgu_ln_b, v_sgu_w=v_sgu_w, v_sgu_b=v_sgu_b, v_p_ret=v_p_ret, v_p_sb=v_p_sb, v_p_sgu=v_p_sgu, v_w_out=v_w_out, v_ln1_g=v_ln1_g, v_ln1_b=v_ln1_b, v_w_up=v_w_up, v_w_down=v_w_down, v_ln2_g=v_ln2_g, v_ln2_b=v_ln2_b)
    weights = {n: given[n] for n in TWIN_WEIGHTS}
    shared = {n: given[n] for n in SHARED_INPUTS}
    per_example = {n: given[n] for n in ['x']}
    grad_fn = _jax.value_and_grad(_loss, argnums=(0, 1))

    def one_microbatch(ex, loss_target):
        ex = dict(ex)
        diff = ex.pop(TWIN_DIFF_INPUT)
        return grad_fn(weights, diff, {**shared, **ex}, loss_target)

    if N_MICROBATCH == 1:
        loss, (grad_w, grad_x) = one_microbatch(per_example, given["loss_target"])
    else:
        def body(carry, xs):
            loss_sum, grad_sum = carry
            l_k, (gw_k, gx_k) = one_microbatch(xs[0], xs[1])
            with _jax.named_scope("update"):
                return (loss_sum + l_k, _jax.tree.map(_jnp.add, grad_sum, gw_k)), gx_k

        init = (_jnp.zeros((), _jnp.float32), _jax.tree.map(_jnp.zeros_like, weights))
        (loss, grad_w), grad_x = _jax.lax.scan(body, init, (per_example, given["loss_target"]))
    with _jax.named_scope("update"):
        delta_w, new_m, new_v = {}, {}, {}
        for n in TWIN_WEIGHTS:
            delta_w[n], new_m[n], new_v[n] = _adamw(weights[n], grad_w[n], given["m_" + n], given["v_" + n])
    return (loss, grad_x, *[grad_w[n] for n in TWIN_WEIGHTS], *[delta_w[n] for n in TWIN_WEIGHTS],
            *[new_m[n] for n in TWIN_WEIGHTS], *[new_v[n] for n in TWIN_WEIGHTS])
```

```python
import functools
import math

import numpy as np
import jax
import jax.numpy as jnp
from jax import lax
from jax.experimental import pallas as pl
from jax.experimental.pallas import tpu as pltpu

F32 = jnp.float32
BF16 = jnp.bfloat16

N_DEV = 8
DEPTH = 2
D_MODEL = 1024
CHUNK = 128
RET_W = 512
SB_W = 512
SGU_W = 512
N_IN = 7680
LN_EPS = 1e-5
ALPHA = (2 * DEPTH) ** 0.25
ROPE_BASE = 10000.0
ADAM_LR, ADAM_B1, ADAM_B2, ADAM_EPS, ADAM_WD, ADAM_STEP = 0.001, 0.9, 0.999, 1e-08, 0.01, 10
VMEM_LIMIT = 56 * 1024 * 1024

_GELU_K = math.sqrt(2.0 / math.pi)
_GELU_C = 0.044715


def _cparams(sem=None):
    return pltpu.CompilerParams(dimension_semantics=sem, vmem_limit_bytes=VMEM_LIMIT)


def _dg(a, b, ca, cb):
    return lax.dot_general(a, b, (((ca,), (cb,)), ((), ())), preferred_element_type=F32)


def _bf(x):
    return x.astype(BF16)


def _sigmoid(x):
    return 1.0 / (1.0 + jnp.exp(-x))


def _gelu(x):
    t = jnp.tanh(_GELU_K * (x + _GELU_C * (x * x * x)))
    return x * (0.5 * (1.0 + t))


def _gelu_grad(x):
    t = jnp.tanh(_GELU_K * (x + _GELU_C * (x * x * x)))
    return 0.5 * (1.0 + t) + 0.5 * x * (1.0 - t * t) * (_GELU_K * (1.0 + 3.0 * _GELU_C * x * x))


def _norm_stats(u):
    mu = jnp.mean(u, axis=-1, keepdims=True)
    d = u - mu
    var = jnp.mean(d * d, axis=-1, keepdims=True)
    rstd = lax.rsqrt(var + LN_EPS)
    return d * rstd, rstd


def _norm_bwd(dxh, xh, rstd):
    return rstd * (dxh - jnp.mean(dxh, axis=-1, keepdims=True) - xh * jnp.mean(dxh * xh, axis=-1, keepdims=True))


def _matmul(a, b, mode, *, name, epi=None, extra=(), out_dtype=F32, tm=512, tn=512, tk=1024):
    if mode == "nn":
        (M, K), N = a.shape, b.shape[1]
    elif mode == "nt":
        (M, K), N = a.shape, b.shape[0]
    else:
        (K, M), N = a.shape, b.shape[1]
    tm, tk = min(tm, M), min(tk, K)
    if K % tk:
        tk = next(c for c in (1536, 1280, 768, 512, 256, 128) if K % c == 0)
    tn = N if epi == "ln" else min(tn, N)
    if epi == "ln":
        tm = min(tm, 256)
    assert M % tm == 0 and N % tn == 0 and K % tk == 0, (name, M, N, K)
    nk = K // tk
    a_spec = {"nn": pl.BlockSpec((tm, tk), lambda i, j, k: (i, k)),
              "nt": pl.BlockSpec((tm, tk), lambda i, j, k: (i, k)),
              "tn": pl.BlockSpec((tk, tm), lambda i, j, k: (k, i))}[mode]
    b_spec = {"nn": pl.BlockSpec((tk, tn), lambda i, j, k: (k, j)),
              "nt": pl.BlockSpec((tn, tk), lambda i, j, k: (j, k)),
              "tn": pl.BlockSpec((tk, tn), lambda i, j, k: (k, j))}[mode]
    ca, cb = {"nn": (1, 0), "nt": (1, 1), "tn": (0, 0)}[mode]
    tile = pl.BlockSpec((tm, tn), lambda i, j, k: (i, j))
    row = pl.BlockSpec((1, tn), lambda i, j, k: (0, j))
    n_extra = {None: 0, "add": 1, "relu2": 0, "drelu2": 1, "ln": 3}[epi]
    assert len(extra) == n_extra
    extra_specs = {None: [], "add": [tile], "relu2": [], "drelu2": [tile], "ln": [tile, row, row]}[epi]
    if epi == "relu2":
        out_shape = (jax.ShapeDtypeStruct((M, N), F32), jax.ShapeDtypeStruct((M, N), BF16))
        out_specs = (tile, tile)
    elif epi == "ln":
        out_shape = (jax.ShapeDtypeStruct((M, N), F32), jax.ShapeDtypeStruct((M, N), F32))
        out_specs = (tile, tile)
    else:
        out_shape = jax.ShapeDtypeStruct((M, N), out_dtype)
        out_specs = tile
    n_out = 2 if epi in ("relu2", "ln") else 1

    def body(*refs):
        a_ref, b_ref = refs[:2]
        ex = refs[2:2 + n_extra]
        outs = refs[2 + n_extra:2 + n_extra + n_out]
        acc_ref = refs[-1]
        k = pl.program_id(2)
        part = _dg(_bf(a_ref[...]), _bf(b_ref[...]), ca, cb)

        def finish(acc):
            if epi is None:
                outs[0][...] = acc.astype(out_dtype)
            elif epi == "add":
                outs[0][...] = (acc + ALPHA * ex[0][...]).astype(out_dtype)
            elif epi == "relu2":
                r = jnp.maximum(acc, 0.0)
                outs[0][...] = acc
                outs[1][...] = _bf(r * r)
            elif epi == "drelu2":
                outs[0][...] = (acc * (2.0 * jnp.maximum(ex[0][...], 0.0))).astype(out_dtype)
            else:
                u = ALPHA * ex[0][...] + acc
                xh, _ = _norm_stats(u)
                outs[0][...] = u
                outs[1][...] = xh * ex[1][...] + ex[2][...]

        if nk == 1:
            finish(part)
        else:
            @pl.when(k == 0)
            def _():
                acc_ref[...] = part

            @pl.when(jnp.logical_and(k > 0, k < nk - 1))
            def _():
                acc_ref[...] += part

            @pl.when(k == nk - 1)
            def _():
                finish(acc_ref[...] + part)

    return pl.pallas_call(
        body, name=name, out_shape=out_shape, grid=(M // tm, N // tn, nk),
        in_specs=[a_spec, b_spec] + extra_specs, out_specs=out_specs,
        scratch_shapes=[pltpu.VMEM((tm, tn), F32)],
        compiler_params=_cparams(("parallel", "parallel", "arbitrary")),
    )(a, b, *extra)


def _ret_tables(S):
    half = 64
    inv_freq = ROPE_BASE ** (-jnp.arange(half, dtype=F32) / half)
    ang = jnp.arange(S, dtype=jnp.int32).astype(F32)[:, None] * inv_freq[None, :]
    cos, sin = jnp.cos(ang), jnp.sin(ang)
    cosf = jnp.concatenate([cos, cos], axis=1)
    sinf = jnp.concatenate([-sin, sin], axis=1)
    log_g = jnp.log(1.0 - 2.0 ** (-5.0 - jnp.arange(4, dtype=F32)))
    idx = jnp.arange(CHUNK, dtype=F32)
    diff = idx[:, None] - idx[None, :]
    md = jnp.where(diff[None] >= 0, jnp.exp(log_g[:, None, None] * diff[None]), 0.0)
    kd = jnp.exp(log_g[:, None] * (CHUNK - 1 - idx)[None, :])
    qd = jnp.exp(log_g[:, None] * (idx + 1.0)[None, :])
    cd = jnp.exp(log_g * CHUNK)
    bc = lambda t: jnp.broadcast_to(t[:, :, None], (4, CHUNK, CHUNK))
    return cosf, sinf, md, bc(qd), bc(kd), jnp.broadcast_to(cd[:, None, None], (4, 8, CHUNK))


def _rot(x, cosf, sinf):
    return x * cosf + pltpu.roll(x, 64, 1) * sinf


def _rot_t(dx, cosf, sinf):
    return dx * cosf - pltpu.roll(dx, 64, 1) * sinf


def _ret_specs(rev, N):
    rn = (lambda n: N - 1 - n) if rev else (lambda n: n)
    col = lambda c: pl.BlockSpec((CHUNK, 512), lambda n, c=c: (rn(n), c))
    tab = pl.BlockSpec((CHUNK, CHUNK), lambda n: (rn(n), 0))
    dec = pl.BlockSpec((4, CHUNK, CHUNK), lambda n: (0, 0, 0))
    cdec = pl.BlockSpec((4, 8, CHUNK), lambda n: (0, 0, 0))
    vec = pl.BlockSpec((1, 512), lambda n: (0, 0))
    st = pl.BlockSpec((1, 4, CHUNK, CHUNK), lambda n: (rn(n), 0, 0, 0))
    return col, tab, dec, cdec, vec, st


def _ret_fwd(proj, tables, gn_g, gn_b):
    S = proj.shape[0]
    N = S // CHUNK
    col, tab, dec, cdec, vec, st = _ret_specs(False, N)

    def body(q_ref, k_ref, v_ref, g_ref, cos_ref, sin_ref, md_ref, qd_ref, kd_ref, cd_ref, gng_ref, gnb_ref,
             out_ref, st_ref, state):
        @pl.when(pl.program_id(0) == 0)
        def _():
            state[...] = jnp.zeros_like(state)

        cosf, sinf = cos_ref[...], sin_ref[...]
        for h in range(4):
            sl = slice(h * 128, (h + 1) * 128)
            qr = _rot(q_ref[:, sl], cosf, sinf)
            kr = _rot(k_ref[:, sl], cosf, sinf) * (128 ** -0.5)
            vb = _bf(v_ref[:, sl])
            s0 = state[h]
            st_ref[0, h] = s0
            sc = _dg(_bf(qr), _bf(kr), 1, 1) * md_ref[h]
            r = _dg(_bf(sc), vb, 1, 0) + _dg(_bf(qr * qd_ref[h]), _bf(s0), 1, 0)
            state[h] = s0 * cd_ref[h, 0:1, :] + _dg(_bf(kr * kd_ref[h]), vb, 0, 0)
            y, _ = _norm_stats(r)
            rg = g_ref[:, sl]
            out_ref[:, sl] = rg * _sigmoid(rg) * (y * gng_ref[:, sl] + gnb_ref[:, sl])

    return pl.pallas_call(
        body, name="ret_fwd", grid=(N,),
        out_shape=(jax.ShapeDtypeStruct((S, RET_W), F32), jax.ShapeDtypeStruct((N, 4, CHUNK, CHUNK), F32)),
        in_specs=[col(0), col(1), col(2), col(3), tab, tab, dec, dec, dec, cdec, vec, vec],
        out_specs=(pl.BlockSpec((CHUNK, 512), lambda n: (n, 0)), st),
        scratch_shapes=[pltpu.VMEM((4, CHUNK, CHUNK), F32)],
        compiler_params=_cparams(("arbitrary",)),
    )(proj, proj, proj, proj, *tables, gn_g, gn_b)


def _ret_bwd(proj, tables, gn_g, gn_b, states, d_out):
    S = proj.shape[0]
    N = S // CHUNK
    col, tab, dec, cdec, vec, st = _ret_specs(True, N)

    def kernel_body(q_ref, k_ref, v_ref, g_ref, cos_ref, sin_ref, md_ref, qd_ref, kd_ref, cd_ref, gng_ref, gnb_ref,
                    st_ref, do_ref, dp_ref, dg_ref, db_ref, gstate):
        @pl.when(pl.program_id(0) == 0)
        def _():
            gstate[...] = jnp.zeros_like(gstate)
            dg_ref[...] = jnp.zeros_like(dg_ref)
            db_ref[...] = jnp.zeros_like(db_ref)

        cosf, sinf = cos_ref[...], sin_ref[...]
        for h in range(4):
            sl = slice(h * 128, (h + 1) * 128)
            qr = _rot(q_ref[:, sl], cosf, sinf)
            kr = _rot(k_ref[:, sl], cosf, sinf) * (128 ** -0.5)
            qb, kb, vb = _bf(qr), _bf(kr), _bf(v_ref[:, sl])
            s0b = _bf(st_ref[0, h])
            md, qd, kd = md_ref[h], qd_ref[h], kd_ref[h]
            scb = _bf(_dg(qb, kb, 1, 1) * md)
            qdb = _bf(qr * qd)
            kdb = _bf(kr * kd)
            r = _dg(scb, vb, 1, 0) + _dg(qdb, s0b, 1, 0)
            y, rstd = _norm_stats(r)
            gng = gng_ref[:, sl]
            gn = y * gng + gnb_ref[:, sl]
            rg = g_ref[:, sl]
            sg = _sigmoid(rg)
            d_o = do_ref[:, sl]
            d_gn = d_o * (rg * sg)
            dg_ref[:, sl] += jnp.sum(d_gn * y, axis=0, keepdims=True)
            db_ref[:, sl] += jnp.sum(d_gn, axis=0, keepdims=True)
            drb = _bf(_norm_bwd(d_gn * gng, y, rstd))
            g0 = gstate[h]
            gb = _bf(g0)
            dscb = _bf(_dg(drb, vb, 1, 1) * md)
            dqr = _dg(dscb, kb, 1, 0) + _dg(drb, s0b, 1, 1) * qd
            dkr = _dg(dscb, qb, 0, 0) + _dg(vb, gb, 1, 1) * kd
            dv = _dg(scb, drb, 0, 0) + _dg(kdb, gb, 1, 0)
            gstate[h] = g0 * cd_ref[h, 0:1, :] + _dg(qdb, drb, 0, 0)
            dp_ref[:, 0 * 512 + h * 128:0 * 512 + (h + 1) * 128] = _bf(_rot_t(dqr, cosf, sinf))
            dp_ref[:, 1 * 512 + h * 128:1 * 512 + (h + 1) * 128] = _bf(_rot_t(dkr, cosf, sinf) * (128 ** -0.5))
            dp_ref[:, 2 * 512 + h * 128:2 * 512 + (h + 1) * 128] = _bf(dv)
            dp_ref[:, 3 * 512 + h * 128:3 * 512 + (h + 1) * 128] = _bf(d_o * gn * (sg * (1.0 + rg * (1.0 - sg))))

    acc = pl.BlockSpec((1, 512), lambda n: (0, 0))
    return pl.pallas_call(
        kernel_body, name="ret_bwd", grid=(N,),
        out_shape=(jax.ShapeDtypeStruct((S, 2048), BF16), jax.ShapeDtypeStruct((1, 512), F32),
                   jax.ShapeDtypeStruct((1, 512), F32)),
        in_specs=[col(0), col(1), col(2), col(3), tab, tab, dec, dec, dec, cdec, vec, vec, st,
                  pl.BlockSpec((CHUNK, 512), lambda n: (N - 1 - n, 0))],
        out_specs=(pl.BlockSpec((CHUNK, 2048), lambda n: (N - 1 - n, 0)), acc, acc),
        scratch_shapes=[pltpu.VMEM((4, CHUNK, CHUNK), F32)],
        compiler_params=_cparams(("arbitrary",)),
    )(proj, proj, proj, proj, *tables, gn_g, gn_b, states, d_out)


SB_TQ = 256
SB_KB = 256
SB_SCALE = 64 ** -0.5
SB_Q_COL, SB_K_COL, SB_V_COL = 2048 // 128, 2560 // 128, 3072 // 128


def _head_masks():
    lane = lax.broadcasted_iota(jnp.int32, (1, 128), 1)
    m0 = (lane < 64).astype(F32)
    return m0, 1.0 - m0


def _tri(kb, cmp):
    r = lax.broadcasted_iota(jnp.int32, (kb, kb), 0)
    c = lax.broadcasted_iota(jnp.int32, (kb, kb), 1)
    return cmp(r, c).astype(BF16)


def _tri_sum(x, tri):
    hi = _bf(x)
    lo = _bf(x - hi.astype(F32))
    return _dg(hi, tri, 1, 0) + _dg(lo, tri, 1, 0)


def _sb_scores(qm, kblk, valid):
    z = _dg(qm, kblk, 1, 1) * SB_SCALE
    en = jnp.exp(-jnp.abs(z))
    lg = jnp.where(valid, -(jnp.maximum(z, 0.0) + jnp.log1p(en)), 0.0)
    return z, lg, en


def _sb_fwd(proj):
    S = proj.shape[0]
    tq, kb = min(SB_TQ, S), min(SB_KB, S)
    nq = S // tq

    def body(q_ref, k_ref, v_ref, o_ref, kb_ref, vm_ref, acc_ref):
        i = pl.program_id(1)
        m0, m1 = _head_masks()

        @pl.when(i == 0)
        def _():
            v = v_ref[...]
            kb_ref[...] = _bf(k_ref[...])
            vm_ref[0] = _bf(v * m0)
            vm_ref[1] = _bf(v * m1)

        q = q_ref[...]
        qm = (_bf(q * m0), _bf(q * m1))
        acc_ref[...] = jnp.zeros_like(acc_ref)
        upper = _tri(kb, lambda r, c: r > c)
        t_idx = i * tq + lax.broadcasted_iota(jnp.int32, (tq, kb), 0)
        col = lax.broadcasted_iota(jnp.int32, (tq, kb), 1)
        nkb = ((i + 1) * tq) // kb

        def step(jj, carry):
            ks = pl.multiple_of((nkb - 1 - jj) * kb, kb)
            kblk = kb_ref[pl.ds(ks, kb), :]
            valid = (ks + col) < t_idx
            out = []
            for h in range(2):
                z, lg, _ = _sb_scores(qm[h], kblk, valid)
                later = _tri_sum(lg, upper) + carry[h]
                a = jnp.where(valid, jnp.exp(lg + z + later), 0.0)
                acc_ref[...] += _dg(_bf(a), vm_ref[h, pl.ds(ks, kb), :], 1, 0)
                out.append(carry[h] + jnp.sum(lg, axis=1, keepdims=True))
            return tuple(out)

        zero = jnp.zeros((tq, 1), F32)
        lax.fori_loop(0, nkb, step, (zero, zero))
        o_ref[...] = acc_ref[...]

    full = lambda c: pl.BlockSpec((S, 128), lambda p, i, c=c: (0, c + p))
    return pl.pallas_call(
        body, name="sb_fwd", grid=(4, nq),
        out_shape=jax.ShapeDtypeStruct((S, SB_W), F32),
        in_specs=[pl.BlockSpec((tq, 128), lambda p, i: (i, SB_Q_COL + p)), full(SB_K_COL), full(SB_V_COL)],
        out_specs=pl.BlockSpec((tq, 128), lambda p, i: (i, p)),
        scratch_shapes=[pltpu.VMEM((S, 128), BF16), pltpu.VMEM((2, S, 128), BF16), pltpu.VMEM((tq, 128), F32)],
        compiler_params=_cparams(("arbitrary", "arbitrary")),
    )(proj, proj, proj)


def _sb_bwd(proj, d_o):
    S = proj.shape[0]
    tq, kb = min(SB_TQ, S), min(SB_KB, S)
    nq = S // tq

    def body(q_ref, k_ref, v_ref, do_ref, dq_ref, dk_ref, dv_ref, kb_ref, kbm_ref, vb_ref, e_ref, dq_acc):
        i = pl.program_id(1)
        m0, m1 = _head_masks()

        @pl.when(i == 0)
        def _():
            k = k_ref[...]
            kb_ref[...] = _bf(k)
            kbm_ref[0] = _bf(k * m0)
            kbm_ref[1] = _bf(k * m1)
            vb_ref[...] = _bf(v_ref[...])
            dk_ref[...] = jnp.zeros_like(dk_ref)
            dv_ref[...] = jnp.zeros_like(dv_ref)

        q, d_out = q_ref[...], do_ref[...]
        qm = (_bf(q * m0), _bf(q * m1))
        dom = (_bf(d_out * m0), _bf(d_out * m1))
        dq_acc[...] = jnp.zeros_like(dq_acc)
        upper = _tri(kb, lambda r, c: r > c)
        lower = _tri(kb, lambda r, c: r < c)
        t_idx = i * tq + lax.broadcasted_iota(jnp.int32, (tq, kb), 0)
        col = lax.broadcasted_iota(jnp.int32, (tq, kb), 1)
        nkb = ((i + 1) * tq) // kb

        def down(jj, carry):
            j = nkb - 1 - jj
            ks = pl.multiple_of(j * kb, kb)
            kblk, vblk = kb_ref[pl.ds(ks, kb), :], vb_ref[pl.ds(ks, kb), :]
            valid = (ks + col) < t_idx
            out = []
            for h in range(2):
                z, lg, _ = _sb_scores(qm[h], kblk, valid)
                later = _tri_sum(lg, upper) + carry[h]
                a = jnp.where(valid, jnp.exp(lg + z + later), 0.0)
                e_ref[h, j] = a * _dg(dom[h], vblk, 1, 1)
                dv_ref[pl.ds(ks, kb), :] += _dg(_bf(a), dom[h], 0, 0)
                out.append(carry[h] + jnp.sum(lg, axis=1, keepdims=True))
            return tuple(out)

        def up(j, carry):
            ks = pl.multiple_of(j * kb, kb)
            kblk = kb_ref[pl.ds(ks, kb), :]
            valid = (ks + col) < t_idx
            out = []
            for h in range(2):
                z = _dg(qm[h], kblk, 1, 1) * SB_SCALE
                en = jnp.exp(-jnp.abs(z))
                inv = 1.0 / (1.0 + en)
                pos = z >= 0.0
                beta = jnp.where(pos, 1.0, en) * inv
                one_m_beta = jnp.where(pos, en, 1.0) * inv
                e = e_ref[h, j]
                d_lg = _tri_sum(e, lower) + carry[h]
                dz = _bf(jnp.where(valid, e * one_m_beta - d_lg * beta, 0.0) * SB_SCALE)
                dq_acc[...] += _dg(dz, kbm_ref[h, pl.ds(ks, kb), :], 1, 0)
                dk_ref[pl.ds(ks, kb), :] += _dg(dz, qm[h], 0, 0)
                out.append(carry[h] + jnp.sum(e, axis=1, keepdims=True))
            return tuple(out)

        zero = jnp.zeros((tq, 1), F32)
        lax.fori_loop(0, nkb, down, (zero, zero))
        lax.fori_loop(0, nkb, up, (zero, zero))
        dq_ref[...] = _bf(dq_acc[...])

    full = lambda c: pl.BlockSpec((S, 128), lambda p, i, c=c: (0, c + p))
    tile = pl.BlockSpec((tq, 128), lambda p, i: (i, p))
    acc = pl.BlockSpec((S, 128), lambda p, i: (0, p))
    dq, dk, dv = pl.pallas_call(
        body, name="sb_bwd", grid=(4, nq),
        out_shape=(jax.ShapeDtypeStruct((S, SB_W), BF16), jax.ShapeDtypeStruct((S, SB_W), F32),
                   jax.ShapeDtypeStruct((S, SB_W), F32)),
        in_specs=[pl.BlockSpec((tq, 128), lambda p, i: (i, SB_Q_COL + p)), full(SB_K_COL), full(SB_V_COL), tile],
        out_specs=(tile, acc, acc),
        scratch_shapes=[pltpu.VMEM((S, 128), BF16), pltpu.VMEM((2, S, 128), BF16), pltpu.VMEM((S, 128), BF16),
                        pltpu.VMEM((2, S // kb, tq, kb), F32), pltpu.VMEM((tq, 128), F32)],
        compiler_params=_cparams(("arbitrary", "arbitrary")),
    )(proj, proj, proj, d_o)
    return dq, _bf(dk), _bf(dv)


SGU_U_COL, SGU_V_COL = 3584 // 512, 4096 // 512


def _causal(w):
    r = lax.broadcasted_iota(jnp.int32, (CHUNK, CHUNK), 0)
    c = lax.broadcasted_iota(jnp.int32, (CHUNK, CHUNK), 1)
    return jnp.where(r >= c, w, 0.0)


def _sgu_fwd(proj, ln_g, ln_b, w, b):
    S = proj.shape[0]
    N = S // CHUNK

    def body(u_ref, v_ref, g_ref, b_ref, w_ref, bias_ref, out_ref):
        u = _gelu(u_ref[...])
        xh, _ = _norm_stats(_gelu(v_ref[...]))
        vn = _bf(xh * g_ref[...] + b_ref[...])
        for g in range(4):
            sl = slice(g * 128, (g + 1) * 128)
            sv = _dg(_bf(_causal(w_ref[g])), vn[:, sl], 1, 0) + bias_ref[g]
            out_ref[:, sl] = u[:, sl] * sv

    vec = pl.BlockSpec((1, 512), lambda n: (0, 0))
    return pl.pallas_call(
        body, name="sgu_fwd", grid=(N,),
        out_shape=jax.ShapeDtypeStruct((S, SGU_W), F32),
        in_specs=[pl.BlockSpec((CHUNK, 512), lambda n: (n, SGU_U_COL)),
                  pl.BlockSpec((CHUNK, 512), lambda n: (n, SGU_V_COL)), vec, vec,
                  pl.BlockSpec((4, CHUNK, CHUNK), lambda n: (0, 0, 0)), pl.BlockSpec((4, CHUNK, 1), lambda n: (0, 0, 0))],
        out_specs=pl.BlockSpec((CHUNK, 512), lambda n: (n, 0)),
        compiler_params=_cparams(("parallel",)),
    )(proj, proj, ln_g, ln_b, w, b)


def _sgu_bwd(proj, ln_g, ln_b, w, b, d_out):
    S = proj.shape[0]
    N = S // CHUNK

    def body(u_ref, v_ref, g_ref, b_ref, w_ref, bias_ref, do_ref, dp_ref, dg_ref, db_ref, dw_ref, dbias_ref):
        @pl.when(pl.program_id(0) == 0)
        def _():
            dg_ref[...] = jnp.zeros_like(dg_ref)
            db_ref[...] = jnp.zeros_like(db_ref)
            dw_ref[...] = jnp.zeros_like(dw_ref)
            dbias_ref[...] = jnp.zeros_like(dbias_ref)

        gu, gv = u_ref[...], v_ref[...]
        u = _gelu(gu)
        xh, rstd = _norm_stats(_gelu(gv))
        ln_gain = g_ref[...]
        vn = _bf(xh * ln_gain + b_ref[...])
        d_o = do_ref[...]
        d_vn = []
        for g in range(4):
            sl = slice(g * 128, (g + 1) * 128)
            wc = _bf(_causal(w_ref[g]))
            sv = _dg(wc, vn[:, sl], 1, 0) + bias_ref[g]
            dp_ref[:, sl] = _bf(d_o[:, sl] * sv * _gelu_grad(gu[:, sl]))
            d_sv = d_o[:, sl] * u[:, sl]
            dbias_ref[g] += jnp.sum(d_sv, axis=1, keepdims=True)
            d_svb = _bf(d_sv)
            dw_ref[g] += _causal(_dg(d_svb, vn[:, sl], 1, 1))
            d_vn.append(_dg(wc, d_svb, 0, 0))
        d_vn = jnp.concatenate(d_vn, axis=1)
        dg_ref[...] += jnp.sum(d_vn * xh, axis=0, keepdims=True)
        db_ref[...] += jnp.sum(d_vn, axis=0, keepdims=True)
        dp_ref[:, 512:1024] = _bf(_norm_bwd(d_vn * ln_gain, xh, rstd) * _gelu_grad(gv))

    vec = pl.BlockSpec((1, 512), lambda n: (0, 0))
    wspec = pl.BlockSpec((4, CHUNK, CHUNK), lambda n: (0, 0, 0))
    bspec = pl.BlockSpec((4, CHUNK, 1), lambda n: (0, 0, 0))
    return pl.pallas_call(
        body, name="sgu_bwd", grid=(N,),
        out_shape=(jax.ShapeDtypeStruct((S, 1024), BF16), jax.ShapeDtypeStruct((1, 512), F32),
                   jax.ShapeDtypeStruct((1, 512), F32), jax.ShapeDtypeStruct((4, CHUNK, CHUNK), F32),
                   jax.ShapeDtypeStruct((4, CHUNK, 1), F32)),
        in_specs=[pl.BlockSpec((CHUNK, 512), lambda n: (n, SGU_U_COL)),
                  pl.BlockSpec((CHUNK, 512), lambda n: (n, SGU_V_COL)), vec, vec, wspec, bspec,
                  pl.BlockSpec((CHUNK, 512), lambda n: (n, 0))],
        out_specs=(pl.BlockSpec((CHUNK, 1024), lambda n: (n, 0)), vec, vec, wspec, bspec),
        compiler_params=_cparams(("arbitrary",)),
    )(proj, proj, ln_g, ln_b, w, b, d_out)


GATE_COL = 4608 // 512


def _merge_fwd(proj, branches, p_list, tm=512):
    S = proj.shape[0]
    tm = min(tm, S)

    def body(r_ref, s_ref, g_ref, pr_ref, ps_ref, pg_ref, gr_ref, gs_ref, gg_ref, m_ref, br_ref):
        acc = None
        for k, (x_ref, p_ref, gate_ref) in enumerate(((r_ref, pr_ref, gr_ref), (s_ref, ps_ref, gs_ref),
                                                      (g_ref, pg_ref, gg_ref))):
            br = _dg(_bf(x_ref[...]), _bf(p_ref[...]), 1, 0)
            br_ref[k] = br
            term = _sigmoid(gate_ref[...]) * br
            acc = term if acc is None else acc + term
        m_ref[...] = _bf(acc)

    xs = pl.BlockSpec((tm, 512), lambda i, n: (i, 0))
    ps = pl.BlockSpec((512, 512), lambda i, n: (0, n))
    gate = lambda k: pl.BlockSpec((tm, 512), lambda i, n, k=k: (i, GATE_COL + 2 * k + n))
    return pl.pallas_call(
        body, name="merge_fwd", grid=(S // tm, 2),
        out_shape=(jax.ShapeDtypeStruct((S, D_MODEL), BF16), jax.ShapeDtypeStruct((3, S, D_MODEL), F32)),
        in_specs=[xs, xs, xs, ps, ps, ps, gate(0), gate(1), gate(2)],
        out_specs=(pl.BlockSpec((tm, 512), lambda i, n: (i, n)), pl.BlockSpec((3, tm, 512), lambda i, n: (0, i, n))),
        compiler_params=_cparams(("parallel", "parallel")),
    )(*branches, *p_list, proj, proj, proj)


def _gate_bwd(proj, br, d_merged, tm=512):
    S = proj.shape[0]
    tm = min(tm, S)

    def body(dm_ref, br_ref, gr_ref, gs_ref, gg_ref, dbr_ref, dgate_ref):
        dm = dm_ref[...]
        for k, gate_ref in enumerate((gr_ref, gs_ref, gg_ref)):
            s = _sigmoid(gate_ref[...])
            dbr_ref[k] = _bf(dm * s)
            dgate_ref[k] = _bf(dm * br_ref[k] * (s * (1.0 - s)))

    gate = lambda k: pl.BlockSpec((tm, 512), lambda i, n, k=k: (i, GATE_COL + 2 * k + n))
    three = pl.BlockSpec((3, tm, 512), lambda i, n: (0, i, n))
    return pl.pallas_call(
        body, name="gate_bwd", grid=(S // tm, 2),
        out_shape=(jax.ShapeDtypeStruct((3, S, D_MODEL), BF16), jax.ShapeDtypeStruct((3, S, D_MODEL), BF16)),
        in_specs=[pl.BlockSpec((tm, 512), lambda i, n: (i, n)), three, gate(0), gate(1), gate(2)],
        out_specs=(three, three),
        compiler_params=_cparams(("parallel", "parallel")),
    )(d_merged, br, proj, proj, proj)


def _ln_bwd(dy, u, g, tm=256):
    S, D = u.shape
    tm = min(tm, S)

    def body(dy_ref, u_ref, g_ref, du_ref, dg_ref, db_ref):
        @pl.when(pl.program_id(0) == 0)
        def _():
            dg_ref[...] = jnp.zeros_like(dg_ref)
            db_ref[...] = jnp.zeros_like(db_ref)

        dy_t = dy_ref[...]
        xh, rstd = _norm_stats(u_ref[...])
        dg_ref[...] += jnp.sum(dy_t * xh, axis=0, keepdims=True)
        db_ref[...] += jnp.sum(dy_t, axis=0, keepdims=True)
        du_ref[...] = _norm_bwd(dy_t * g_ref[...], xh, rstd)

    tile = pl.BlockSpec((tm, D), lambda i: (i, 0))
    vec = pl.BlockSpec((1, D), lambda i: (0, 0))
    return pl.pallas_call(
        body, name="ln_bwd", grid=(S // tm,),
        out_shape=(jax.ShapeDtypeStruct((S, D), F32), jax.ShapeDtypeStruct((1, D), F32),
                   jax.ShapeDtypeStruct((1, D), F32)),
        in_specs=[tile, tile, vec], out_specs=(tile, vec, vec),
        compiler_params=_cparams(("arbitrary",)),
    )(dy, u, g)


def _loss_grad(y, target, tm=256):
    S, D = y.shape
    tm = min(tm, S)

    def body(y_ref, t_ref, dy_ref, sq_ref):
        @pl.when(pl.program_id(0) == 0)
        def _():
            sq_ref[...] = jnp.zeros_like(sq_ref)

        err = y_ref[...] - t_ref[...]
        dy_ref[...] = err * (1.0 / D)
        sq_ref[...] += jnp.sum(err * err, axis=0, keepdims=True)

    tile = pl.BlockSpec((tm, D), lambda i: (i, 0))
    vec = pl.BlockSpec((1, D), lambda i: (0, 0))
    return pl.pallas_call(
        body, name="loss_grad", grid=(S // tm,),
        out_shape=(jax.ShapeDtypeStruct((S, D), F32), jax.ShapeDtypeStruct((1, D), F32)),
        in_specs=[tile, tile], out_specs=(tile, vec),
        compiler_params=_cparams(("arbitrary",)),
    )(y, target)


def _layer_fwd(x, W, tables):
    proj = _matmul(x, W["w_in"], "nn", name="proj")
    retg, states = _ret_fwd(proj, tables, W["ret_gn_g"], W["ret_gn_b"])
    sb = _sb_fwd(proj)
    sg = _sgu_fwd(proj, W["sgu_ln_g"], W["sgu_ln_b"], W["sgu_w"], W["sgu_b"])
    merged, br = _merge_fwd(proj, (retg, sb, sg), (W["p_ret"], W["p_sb"], W["p_sgu"]))
    u1, x1 = _matmul(merged, W["w_out"], "nn", name="out_ln", epi="ln", extra=(x, W["ln1_g"], W["ln1_b"]))
    hpre, act = _matmul(x1, W["w_up"], "nn", name="up", epi="relu2")
    u2, x2 = _matmul(act, W["w_down"], "nn", name="down_ln", epi="ln", extra=(x1, W["ln2_g"], W["ln2_b"]))
    saved = dict(x=x, proj=proj, retg=retg, states=states, sb=sb, sg=sg, merged=merged, br=br, u1=u1, x1=x1,
                 hpre=hpre, act=act, u2=u2)
    return x2, saved


def _layer_bwd(d_x2, W, tables, sv):
    g = {}
    du2, g["ln2_g"], g["ln2_b"] = _ln_bwd(d_x2, sv["u2"], W["ln2_g"])
    d_hpre = _matmul(du2, W["w_down"], "nt", name="d_act", epi="drelu2", extra=(sv["hpre"],), out_dtype=BF16)
    g["w_down"] = _matmul(sv["act"], du2, "tn", name="dw_down")
    g["w_up"] = _matmul(sv["x1"], d_hpre, "tn", name="dw_up")
    d_x1 = _matmul(d_hpre, W["w_up"], "nt", name="d_x1", epi="add", extra=(du2,))
    du1, g["ln1_g"], g["ln1_b"] = _ln_bwd(d_x1, sv["u1"], W["ln1_g"])
    d_merged = _matmul(du1, W["w_out"], "nt", name="d_merged")
    g["w_out"] = _matmul(sv["merged"], du1, "tn", name="dw_out")
    d_br, d_gate = _gate_bwd(sv["proj"], sv["br"], d_merged)
    d_branch = []
    for k, (nm, act) in enumerate((("p_ret", sv["retg"]), ("p_sb", sv["sb"]), ("p_sgu", sv["sg"]))):
        d_branch.append(_matmul(d_br[k], W[nm], "nt", name="d_" + nm[2:]))
        g[nm] = _matmul(act, d_br[k], "tn", name="dw_" + nm[2:])
    d_ret, g["ret_gn_g"], g["ret_gn_b"] = _ret_bwd(sv["proj"], tables, W["ret_gn_g"], W["ret_gn_b"], sv["states"],
                                                   d_branch[0])
    d_sq, d_sk, d_sv = _sb_bwd(sv["proj"], d_branch[1])
    d_sgu, g["sgu_ln_g"], g["sgu_ln_b"], g["sgu_w"], g["sgu_b"] = _sgu_bwd(
        sv["proj"], W["sgu_ln_g"], W["sgu_ln_b"], W["sgu_w"], W["sgu_b"], d_branch[2])
    d_proj = jnp.concatenate([d_ret, d_sq, d_sk, d_sv, d_sgu, d_gate[0], d_gate[1], d_gate[2]], axis=1)
    g["w_in"] = _matmul(sv["x"], d_proj, "tn", name="dw_in")
    d_x = _matmul(d_proj, W["w_in"], "nt", name="d_x", epi="add", extra=(du1,))
    return d_x, g


BIG = ("w_in", "p_ret", "p_sb", "p_sgu", "w_out", "w_up", "w_down")
SMALL = ("ret_gn_g", "ret_gn_b", "sgu_ln_g", "sgu_ln_b", "sgu_w", "sgu_b", "ln1_g", "ln1_b", "ln2_g", "ln2_b")


def _layer_weights(full, small, l):
    W = {n: full[n][l] for n in BIG}
    for n in SMALL:
        if n == "sgu_w":
            W[n] = small[n][l]
        elif n == "sgu_b":
            W[n] = small[n][l].reshape(4, CHUNK, 1)
        else:
            W[n] = small[n][l].reshape(1, -1)
    return W


def _local_step(x, target, full, small):
    tables = _ret_tables(x.shape[0])
    Ws = [_layer_weights(full, small, l) for l in range(DEPTH)]
    saved = []
    h = x
    for l in range(DEPTH):
        h, sv = _layer_fwd(h, Ws[l], tables)
        saved.append(sv)
    d_h, sq = _loss_grad(h, target)
    grads = [None] * DEPTH
    for l in reversed(range(DEPTH)):
        d_h, grads[l] = _layer_bwd(d_h, Ws[l], tables, saved[l])
    return sq, d_h, grads


def _exchange(arrs, bcast, name):
    n = len(arrs)
    out_shape = [jax.ShapeDtypeStruct(((N_DEV,) + a.shape) if bcast else a.shape, a.dtype) for a in arrs]

    def body(*refs):
        ins, outs = refs[:n], refs[n:2 * n]
        send_sems, recv_sems, local_sems = refs[2 * n:]
        x, y, c = lax.axis_index("x"), lax.axis_index("y"), lax.axis_index("c")
        me = 4 * x + 2 * y + c
        own = []
        for t in range(n):
            cp = pltpu.make_async_copy(ins[t] if bcast else ins[t].at[me], outs[t].at[me], local_sems.at[t])
            cp.start()
            own.append(cp)
        sent = []
        for d in range(1, N_DEV):
            px = 1 - x if d & 4 else x
            py = 1 - y if d & 2 else y
            pc = 1 - c if d & 1 else c
            peer = 4 * px + 2 * py + pc
            for t in range(n):
                cp = pltpu.make_async_remote_copy(
                    src_ref=ins[t] if bcast else ins[t].at[peer], dst_ref=outs[t].at[me],
                    send_sem=send_sems.at[t * (N_DEV - 1) + d - 1], recv_sem=recv_sems.at[t * (N_DEV - 1) + d - 1],
                    device_id=(px, py, pc), device_id_type=pl.DeviceIdType.MESH)
                cp.start()
                sent.append(cp)
        for cp in sent:
            cp.wait()
        for cp in own:
            cp.wait()

    hbm = pl.BlockSpec(memory_space=pl.ANY)
    return pl.pallas_call(
        body, name=name, out_shape=out_shape, in_specs=[hbm] * n, out_specs=[hbm] * n,
        scratch_shapes=[pltpu.SemaphoreType.DMA((n * (N_DEV - 1),)), pltpu.SemaphoreType.DMA((n * (N_DEV - 1),)),
                        pltpu.SemaphoreType.DMA((n,))],
    )(*arrs)


COL_SHARDED = ("w_in", "p_ret", "p_sb", "p_sgu", "w_up")


def _unshard(name, g):
    _, L, r, c = g.shape
    if name in COL_SHARDED:
        return g.transpose(1, 2, 0, 3).reshape(L, r, N_DEV * c)
    return g.transpose(1, 0, 2, 3).reshape(L, N_DEV * r, c)


def _to_chunks(name, w):
    L, R, C = w.shape
    if name in COL_SHARDED:
        return w.reshape(L, R, N_DEV, C // N_DEV).transpose(2, 0, 1, 3)
    return w.reshape(L, N_DEV, R // N_DEV, C).transpose(1, 0, 2, 3)


def _adam(w, parts, m, v, name, tr=256):
    R, C = w.shape
    tr = min(tr, R)
    assert R % tr == 0

    def body(w_ref, p_ref, m_ref, v_ref, g_ref, d_ref, nm_ref, nv_ref):
        g = p_ref[0]
        for j in range(1, N_DEV):
            g = g + p_ref[j]
        m2 = ADAM_B1 * m_ref[...] + (1.0 - ADAM_B1) * g
        v2 = ADAM_B2 * v_ref[...] + (1.0 - ADAM_B2) * (g * g)
        m_hat = m2 / (1.0 - ADAM_B1 ** ADAM_STEP)
        v_hat = v2 / (1.0 - ADAM_B2 ** ADAM_STEP)
        g_ref[...] = g
        d_ref[...] = -ADAM_LR * (m_hat / (jnp.sqrt(v_hat) + ADAM_EPS) + ADAM_WD * w_ref[...])
        nm_ref[...] = m2
        nv_ref[...] = v2

    tile = pl.BlockSpec((tr, C), lambda i: (i, 0))
    out = jax.ShapeDtypeStruct((R, C), F32)
    return pl.pallas_call(
        body, name=name, grid=(R // tr,), out_shape=(out, out, out, out),
        in_specs=[tile, pl.BlockSpec((N_DEV, tr, C), lambda i: (0, i, 0)), tile, tile],
        out_specs=(tile, tile, tile, tile),
        compiler_params=_cparams(("parallel",)),
    )(w, parts, m, v)


def _pack_small(tree):
    return jnp.concatenate([tree[n].reshape(-1, 128) for n in SMALL], axis=0)


def _unpack_small(packed, like):
    out, r = {}, 0
    for n in SMALL:
        rows = like[n].size // 128
        out[n] = packed[r:r + rows].reshape(like[n].shape)
        r += rows
    return out


WEIGHTS = ("w_in", "ret_gn_g", "ret_gn_b", "sgu_ln_g", "sgu_ln_b", "sgu_w", "sgu_b", "p_ret", "p_sb", "p_sgu", "w_out",
           "ln1_g", "ln1_b", "w_up", "w_down", "ln2_g", "ln2_b")


def kernel(x, w_in, ret_gn_g, ret_gn_b, sgu_ln_g, sgu_ln_b, sgu_w, sgu_b, p_ret, p_sb, p_sgu, w_out, ln1_g, ln1_b, w_up, w_down, ln2_g, ln2_b, loss_target, m_w_in, m_ret_gn_g, m_ret_gn_b, m_sgu_ln_g, m_sgu_ln_b, m_sgu_w, m_sgu_b, m_p_ret, m_p_sb, m_p_sgu, m_w_out, m_ln1_g, m_ln1_b, m_w_up, m_w_down, m_ln2_g, m_ln2_b, v_w_in, v_ret_gn_g, v_ret_gn_b, v_sgu_ln_g, v_sgu_ln_b, v_sgu_w, v_sgu_b, v_p_ret, v_p_sb, v_p_sgu, v_w_out, v_ln1_g, v_ln1_b, v_w_up, v_w_down, v_ln2_g, v_ln2_b):
    w = dict(zip(WEIGHTS, (w_in, ret_gn_g, ret_gn_b, sgu_ln_g, sgu_ln_b, sgu_w, sgu_b, p_ret, p_sb, p_sgu, w_out,
                           ln1_g, ln1_b, w_up, w_down, ln2_g, ln2_b)))
    m = dict(zip(WEIGHTS, (m_w_in, m_ret_gn_g, m_ret_gn_b, m_sgu_ln_g, m_sgu_ln_b, m_sgu_w, m_sgu_b, m_p_ret, m_p_sb,
                           m_p_sgu, m_w_out, m_ln1_g, m_ln1_b, m_w_up, m_w_down, m_ln2_g, m_ln2_b)))
    v = dict(zip(WEIGHTS, (v_w_in, v_ret_gn_g, v_ret_gn_b, v_sgu_ln_g, v_sgu_ln_b, v_sgu_w, v_sgu_b, v_p_ret, v_p_sb,
                           v_p_sgu, v_w_out, v_ln1_g, v_ln1_b, v_w_up, v_w_down, v_ln2_g, v_ln2_b)))

    gathered = _exchange([_bf(w[n]) for n in BIG], True, "gather_weights")
    full = {n: _unshard(n, g) for n, g in zip(BIG, gathered)}
    small = {n: w[n] for n in SMALL}
    sq, d_x, grads = _local_step(x[0], loss_target[0], full, small)
    loss = lax.psum(0.5 * jnp.sum(sq) / D_MODEL, ("x", "y", "c"))

    chunks = [_to_chunks(n, jnp.stack([grads[l][n] for l in range(DEPTH)])) for n in BIG]
    parts = _exchange(chunks, False, "scatter_grads")
    grad, delta, new_m, new_v = {}, {}, {}, {}
    for n, p in zip(BIG, parts):
        shape = w[n].shape
        two_d = (shape[0] * shape[1], shape[2])
        res = _adam(w[n].reshape(two_d), p.reshape((N_DEV,) + two_d), m[n].reshape(two_d), v[n].reshape(two_d),
                    "adam_" + n)
        grad[n], delta[n], new_m[n], new_v[n] = (r.reshape(shape) for r in res)

    my_small = {n: jnp.stack([grads[l][n] for l in range(DEPTH)]) for n in SMALL}
    (small_parts,) = _exchange([_pack_small(my_small)], True, "gather_small_grads")
    res = _adam(_pack_small(small), small_parts, _pack_small({n: m[n] for n in SMALL}),
                _pack_small({n: v[n] for n in SMALL}), "adam_small", tr=small_parts.shape[1])
    for tree, packed in zip((grad, delta, new_m, new_v), res):
        tree.update(_unpack_small(packed, small))

    return (loss, d_x[None], *[grad[n] for n in WEIGHTS], *[delta[n] for n in WEIGHTS],
            *[new_m[n] for n in WEIGHTS], *[new_v[n] for n in WEIGHTS])
```

```python
import functools
import math

import numpy as np
import jax
import jax.numpy as jnp
from jax import lax
from jax.experimental import pallas as pl
from jax.experimental.pallas import tpu as pltpu

F32 = jnp.float32
BF16 = jnp.bfloat16

N_DEV = 8
DEPTH = 2
D_MODEL = 1024
CHUNK = 128
RET_W = 512
SB_W = 512
SGU_W = 512
N_IN = 7680
LN_EPS = 1e-5
ALPHA = (2 * DEPTH) ** 0.25
ROPE_BASE = 10000.0
ADAM_LR, ADAM_B1, ADAM_B2, ADAM_EPS, ADAM_WD, ADAM_STEP = 0.001, 0.9, 0.999, 1e-08, 0.01, 10
VMEM_LIMIT = 56 * 1024 * 1024

_GELU_K = math.sqrt(2.0 / math.pi)
_GELU_C = 0.044715


def _cparams(sem=None):
    return pltpu.CompilerParams(dimension_semantics=sem, vmem_limit_bytes=VMEM_LIMIT)


def _dg(a, b, ca, cb):
    return lax.dot_general(a, b, (((ca,), (cb,)), ((), ())), preferred_element_type=F32)


def _bf(x):
    return x.astype(BF16)


def _sigmoid(x):
    return 1.0 / (1.0 + jnp.exp(-x))


def _gelu(x):
    t = jnp.tanh(_GELU_K * (x + _GELU_C * (x * x * x)))
    return x * (0.5 * (1.0 + t))


def _gelu_grad(x):
    t = jnp.tanh(_GELU_K * (x + _GELU_C * (x * x * x)))
    return 0.5 * (1.0 + t) + 0.5 * x * (1.0 - t * t) * (_GELU_K * (1.0 + 3.0 * _GELU_C * x * x))


def _norm_stats(u):
    mu = jnp.mean(u, axis=-1, keepdims=True)
    d = u - mu
    var = jnp.mean(d * d, axis=-1, keepdims=True)
    rstd = lax.rsqrt(var + LN_EPS)
    return d * rstd, rstd


def _norm_bwd(dxh, xh, rstd):
    return rstd * (dxh - jnp.mean(dxh, axis=-1, keepdims=True) - xh * jnp.mean(dxh * xh, axis=-1, keepdims=True))


class _Transfer:
    def __init__(self, src, dst_shape, src_at, dst_at):
        self.src, self.dst_shape, self.src_at, self.dst_at = src, tuple(dst_shape), src_at, dst_at


def _gather_transfer(shard, l, kind):
    _, r, c = shard.shape
    src_at = lambda ref, p: ref.at[l]
    if kind == "slab":
        return _Transfer(shard, (N_DEV, r, c), src_at, lambda ref, s: ref.at[s])
    if kind == "rows":
        return _Transfer(shard, (N_DEV * r, c), src_at, lambda ref, s: ref.at[pl.ds(pl.multiple_of(s * r, r), r), :])
    return _Transfer(shard, (r, N_DEV * c), src_at, lambda ref, s: ref.at[:, pl.ds(pl.multiple_of(s * c, c), c)])


def _scatter_transfer(chunks):
    return _Transfer(chunks, chunks.shape, lambda ref, p: ref.at[p], lambda ref, s: ref.at[s])


def _slab_transfer(arr):
    return _Transfer(arr, (N_DEV,) + arr.shape, lambda ref, p: ref, lambda ref, s: ref.at[s])


class _Comm:
    def __init__(self, transfers):
        self.transfers = list(transfers)
        self.n = len(self.transfers)
        self.arrays = [t.src for t in self.transfers]
        self.out_shape = [jax.ShapeDtypeStruct(t.dst_shape, t.src.dtype) for t in self.transfers]
        self.scratch = [pltpu.SemaphoreType.DMA((self.n * (N_DEV - 1),)), pltpu.SemaphoreType.DMA((self.n * (N_DEV - 1),)),
                        pltpu.SemaphoreType.DMA((self.n,))]

    def _copies(self, srcs, dsts, send_sems, recv_sems, local_sems):
        x, y, c = lax.axis_index("x"), lax.axis_index("y"), lax.axis_index("c")
        me = 4 * x + 2 * y + c
        copies = []
        for d in range(1, N_DEV):
            px = 1 - x if d & 4 else x
            py = 1 - y if d & 2 else y
            pc = 1 - c if d & 1 else c
            peer = 4 * px + 2 * py + pc
            for t, tr in enumerate(self.transfers):
                k = t * (N_DEV - 1) + d - 1
                copies.append(pltpu.make_async_remote_copy(
                    src_ref=tr.src_at(srcs[t], peer), dst_ref=tr.dst_at(dsts[t], me),
                    send_sem=send_sems.at[k], recv_sem=recv_sems.at[k],
                    device_id=(px, py, pc), device_id_type=pl.DeviceIdType.MESH))
        own = [pltpu.make_async_copy(tr.src_at(srcs[t], me), tr.dst_at(dsts[t], me), local_sems.at[t])
               for t, tr in enumerate(self.transfers)]
        return copies, own

    def start(self, srcs, dsts, *sems):
        copies, own = self._copies(srcs, dsts, *sems)
        for cp in own + copies:
            cp.start()

    def wait(self, srcs, dsts, *sems):
        copies, own = self._copies(srcs, dsts, *sems)
        for cp in copies + own:
            cp.wait()


def _pcall(body, *, name, grid, in_specs, out_specs, out_shape, scratch_shapes, sem, args, comm=None):
    in_specs, out_specs, out_shape = list(in_specs), list(out_specs), list(out_shape)
    if comm is None:
        outs = pl.pallas_call(body, name=name, grid=grid, in_specs=in_specs, out_specs=out_specs, out_shape=out_shape,
                              scratch_shapes=list(scratch_shapes), compiler_params=_cparams(sem))(*args)
        return list(outs), []
    n_in, n_out, n_scr, k = len(in_specs), len(out_specs), len(scratch_shapes), comm.n

    def carrier(*refs):
        ins, cin = refs[:n_in], refs[n_in:n_in + k]
        outs, cout = refs[n_in + k:n_in + k + n_out], refs[n_in + k + n_out:n_in + 2 * k + n_out]
        scr, sems = refs[n_in + 2 * k + n_out:n_in + 2 * k + n_out + n_scr], refs[n_in + 2 * k + n_out + n_scr:]
        ids = [pl.program_id(d) for d in range(len(grid))]
        first = functools.reduce(jnp.logical_and, [i == 0 for i in ids])
        last = functools.reduce(jnp.logical_and, [i == g - 1 for i, g in zip(ids, grid)])

        @pl.when(first)
        def _():
            comm.start(cin, cout, *sems)

        body(*ins, *outs, *scr)

        @pl.when(last)
        def _():
            comm.wait(cin, cout, *sems)

    hbm = pl.BlockSpec(memory_space=pl.ANY)
    outs = pl.pallas_call(
        carrier, name=name, grid=grid, in_specs=in_specs + [hbm] * k, out_specs=out_specs + [hbm] * k,
        out_shape=out_shape + comm.out_shape, scratch_shapes=list(scratch_shapes) + comm.scratch,
        compiler_params=_cparams(tuple("arbitrary" for _ in grid)),
    )(*args, *comm.arrays)
    return list(outs[:n_out]), list(outs[n_out:])


def _exchange(transfers, name):
    comm = _Comm(transfers)

    def body(*refs):
        k = comm.n
        comm.start(refs[:k], refs[k:2 * k], *refs[2 * k:])
        comm.wait(refs[:k], refs[k:2 * k], *refs[2 * k:])

    hbm = pl.BlockSpec(memory_space=pl.ANY)
    return pl.pallas_call(body, name=name, out_shape=comm.out_shape, in_specs=[hbm] * comm.n, out_specs=[hbm] * comm.n,
                          scratch_shapes=comm.scratch)(*comm.arrays)


def _matmul(a, b, mode, *, name, epi=None, extra=(), out_dtype=F32, tm=512, tn=512, tk=1024, chunks=None, comm=None):
    if mode == "nn":
        (M, K), N = a.shape, b.shape[1]
    elif mode == "nt":
        (M, K), N = a.shape, b.shape[0]
    else:
        (K, M), N = a.shape, b.shape[1]
    tm, tk = min(tm, M), min(tk, K)
    if K % tk:
        tk = next(c for c in (1536, 1280, 768, 512, 256, 128) if K % c == 0)
    tn = N if epi == "ln" else min(tn, N)
    if epi == "ln":
        tm = min(tm, 256)
    if chunks == "rows":
        tm = min(tm, M // N_DEV)
    if chunks == "cols":
        tn = min(tn, N // N_DEV)
    assert M % tm == 0 and N % tn == 0 and K % tk == 0, (name, M, N, K)
    nk = K // tk
    a_spec = {"nn": pl.BlockSpec((tm, tk), lambda i, j, k: (i, k)),
              "nt": pl.BlockSpec((tm, tk), lambda i, j, k: (i, k)),
              "tn": pl.BlockSpec((tk, tm), lambda i, j, k: (k, i))}[mode]
    b_spec = {"nn": pl.BlockSpec((tk, tn), lambda i, j, k: (k, j)),
              "nt": pl.BlockSpec((tn, tk), lambda i, j, k: (j, k)),
              "tn": pl.BlockSpec((tk, tn), lambda i, j, k: (k, j))}[mode]
    ca, cb = {"nn": (1, 0), "nt": (1, 1), "tn": (0, 0)}[mode]
    tile = pl.BlockSpec((tm, tn), lambda i, j, k: (i, j))
    row = pl.BlockSpec((1, tn), lambda i, j, k: (0, j))
    n_extra = {None: 0, "add": 1, "relu2": 0, "drelu2": 1, "ln": 3}[epi]
    assert len(extra) == n_extra
    extra_specs = {None: [], "add": [tile], "relu2": [], "drelu2": [tile], "ln": [tile, row, row]}[epi]
    if epi == "relu2":
        out_shape = (jax.ShapeDtypeStruct((M, N), F32), jax.ShapeDtypeStruct((M, N), BF16))
        out_specs = (tile, tile)
    elif epi == "ln":
        out_shape = (jax.ShapeDtypeStruct((M, N), F32), jax.ShapeDtypeStruct((M, N), F32))
        out_specs = (tile, tile)
    elif chunks == "rows":
        per = M // N_DEV // tm
        out_shape = (jax.ShapeDtypeStruct((N_DEV, M // N_DEV, N), out_dtype),)
        out_specs = (pl.BlockSpec((None, tm, tn), lambda i, j, k: (i // per, i % per, j)),)
    elif chunks == "cols":
        per = N // N_DEV // tn
        out_shape = (jax.ShapeDtypeStruct((N_DEV, M, N // N_DEV), out_dtype),)
        out_specs = (pl.BlockSpec((None, tm, tn), lambda i, j, k: (j // per, i, j % per)),)
    else:
        out_shape = (jax.ShapeDtypeStruct((M, N), out_dtype),)
        out_specs = (tile,)
    n_out = 2 if epi in ("relu2", "ln") else 1

    def body(*refs):
        a_ref, b_ref = refs[:2]
        ex = refs[2:2 + n_extra]
        outs = refs[2 + n_extra:2 + n_extra + n_out]
        acc_ref = refs[-1]
        k = pl.program_id(2)
        part = _dg(_bf(a_ref[...]), _bf(b_ref[...]), ca, cb)

        def finish(acc):
            if epi is None:
                outs[0][...] = acc.astype(out_dtype)
            elif epi == "add":
                outs[0][...] = (acc + ALPHA * ex[0][...]).astype(out_dtype)
            elif epi == "relu2":
                r = jnp.maximum(acc, 0.0)
                outs[0][...] = acc
                outs[1][...] = _bf(r * r)
            elif epi == "drelu2":
                outs[0][...] = (acc * (2.0 * jnp.maximum(ex[0][...], 0.0))).astype(out_dtype)
            else:
                u = ALPHA * ex[0][...] + acc
                xh, _ = _norm_stats(u)
                outs[0][...] = u
                outs[1][...] = xh * ex[1][...] + ex[2][...]

        if nk == 1:
            finish(part)
        else:
            @pl.when(k == 0)
            def _():
                acc_ref[...] = part

            @pl.when(jnp.logical_and(k > 0, k < nk - 1))
            def _():
                acc_ref[...] += part

            @pl.when(k == nk - 1)
            def _():
                finish(acc_ref[...] + part)

    outs, landed = _pcall(
        body, name=name, out_shape=out_shape, grid=(M // tm, N // tn, nk),
        in_specs=[a_spec, b_spec] + extra_specs, out_specs=out_specs,
        scratch_shapes=[pltpu.VMEM((tm, tn), F32)], sem=("parallel", "parallel", "arbitrary"),
        args=(a, b, *extra), comm=comm)
    res = outs[0] if n_out == 1 else tuple(outs)
    return res if comm is None else (res, landed)


def _ret_tables(S):
    half = 64
    inv_freq = ROPE_BASE ** (-jnp.arange(half, dtype=F32) / half)
    ang = jnp.arange(S, dtype=jnp.int32).astype(F32)[:, None] * inv_freq[None, :]
    cos, sin = jnp.cos(ang), jnp.sin(ang)
    cosf = jnp.concatenate([cos, cos], axis=1)
    sinf = jnp.concatenate([-sin, sin], axis=1)
    log_g = jnp.log(1.0 - 2.0 ** (-5.0 - jnp.arange(4, dtype=F32)))
    idx = jnp.arange(CHUNK, dtype=F32)
    diff = idx[:, None] - idx[None, :]
    md = jnp.where(diff[None] >= 0, jnp.exp(log_g[:, None, None] * diff[None]), 0.0)
    kd = jnp.exp(log_g[:, None] * (CHUNK - 1 - idx)[None, :])
    qd = jnp.exp(log_g[:, None] * (idx + 1.0)[None, :])
    cd = jnp.exp(log_g * CHUNK)
    bc = lambda t: jnp.broadcast_to(t[:, :, None], (4, CHUNK, CHUNK))
    return cosf, sinf, md, bc(qd), bc(kd), jnp.broadcast_to(cd[:, None, None], (4, 8, CHUNK))


def _rot(x, cosf, sinf):
    return x * cosf + pltpu.roll(x, 64, 1) * sinf


def _rot_t(dx, cosf, sinf):
    return dx * cosf - pltpu.roll(dx, 64, 1) * sinf


def _ret_specs(rev, N):
    rn = (lambda n: N - 1 - n) if rev else (lambda n: n)
    col = lambda c: pl.BlockSpec((CHUNK, 512), lambda n, c=c: (rn(n), c))
    tab = pl.BlockSpec((CHUNK, CHUNK), lambda n: (rn(n), 0))
    dec = pl.BlockSpec((4, CHUNK, CHUNK), lambda n: (0, 0, 0))
    cdec = pl.BlockSpec((4, 8, CHUNK), lambda n: (0, 0, 0))
    vec = pl.BlockSpec((1, 512), lambda n: (0, 0))
    st = pl.BlockSpec((1, 4, CHUNK, CHUNK), lambda n: (rn(n), 0, 0, 0))
    return col, tab, dec, cdec, vec, st


def _ret_fwd(proj, tables, gn_g, gn_b):
    S = proj.shape[0]
    N = S // CHUNK
    col, tab, dec, cdec, vec, st = _ret_specs(False, N)

    def body(q_ref, k_ref, v_ref, g_ref, cos_ref, sin_ref, md_ref, qd_ref, kd_ref, cd_ref, gng_ref, gnb_ref,
             out_ref, st_ref, state):
        @pl.when(pl.program_id(0) == 0)
        def _():
            state[...] = jnp.zeros_like(state)

        cosf, sinf = cos_ref[...], sin_ref[...]
        for h in range(4):
            sl = slice(h * 128, (h + 1) * 128)
            qr = _rot(q_ref[:, sl], cosf, sinf)
            kr = _rot(k_ref[:, sl], cosf, sinf) * (128 ** -0.5)
            vb = _bf(v_ref[:, sl])
            s0 = state[h]
            st_ref[0, h] = s0
            sc = _dg(_bf(qr), _bf(kr), 1, 1) * md_ref[h]
            r = _dg(_bf(sc), vb, 1, 0) + _dg(_bf(qr * qd_ref[h]), _bf(s0), 1, 0)
            state[h] = s0 * cd_ref[h, 0:1, :] + _dg(_bf(kr * kd_ref[h]), vb, 0, 0)
            y, _ = _norm_stats(r)
            rg = g_ref[:, sl]
            out_ref[:, sl] = rg * _sigmoid(rg) * (y * gng_ref[:, sl] + gnb_ref[:, sl])

    return pl.pallas_call(
        body, name="ret_fwd", grid=(N,),
        out_shape=(jax.ShapeDtypeStruct((S, RET_W), F32), jax.ShapeDtypeStruct((N, 4, CHUNK, CHUNK), F32)),
        in_specs=[col(0), col(1), col(2), col(3), tab, tab, dec, dec, dec, cdec, vec, vec],
        out_specs=(pl.BlockSpec((CHUNK, 512), lambda n: (n, 0)), st),
        scratch_shapes=[pltpu.VMEM((4, CHUNK, CHUNK), F32)],
        compiler_params=_cparams(("arbitrary",)),
    )(proj, proj, proj, proj, *tables, gn_g, gn_b)


def _ret_bwd(proj, tables, gn_g, gn_b, states, d_out):
    S = proj.shape[0]
    N = S // CHUNK
    col, tab, dec, cdec, vec, st = _ret_specs(True, N)

    def kernel_body(q_ref, k_ref, v_ref, g_ref, cos_ref, sin_ref, md_ref, qd_ref, kd_ref, cd_ref, gng_ref, gnb_ref,
                    st_ref, do_ref, dp_ref, dg_ref, db_ref, gstate):
        @pl.when(pl.program_id(0) == 0)
        def _():
            gstate[...] = jnp.zeros_like(gstate)
            dg_ref[...] = jnp.zeros_like(dg_ref)
            db_ref[...] = jnp.zeros_like(db_ref)

        cosf, sinf = cos_ref[...], sin_ref[...]
        for h in range(4):
            sl = slice(h * 128, (h + 1) * 128)
            qr = _rot(q_ref[:, sl], cosf, sinf)
            kr = _rot(k_ref[:, sl], cosf, sinf) * (128 ** -0.5)
            qb, kb, vb = _bf(qr), _bf(kr), _bf(v_ref[:, sl])
            s0b = _bf(st_ref[0, h])
            md, qd, kd = md_ref[h], qd_ref[h], kd_ref[h]
            scb = _bf(_dg(qb, kb, 1, 1) * md)
            qdb = _bf(qr * qd)
            kdb = _bf(kr * kd)
            r = _dg(scb, vb, 1, 0) + _dg(qdb, s0b, 1, 0)
            y, rstd = _norm_stats(r)
            gng = gng_ref[:, sl]
            gn = y * gng + gnb_ref[:, sl]
            rg = g_ref[:, sl]
            sg = _sigmoid(rg)
            d_o = do_ref[:, sl]
            d_gn = d_o * (rg * sg)
            dg_ref[:, sl] += jnp.sum(d_gn * y, axis=0, keepdims=True)
            db_ref[:, sl] += jnp.sum(d_gn, axis=0, keepdims=True)
            drb = _bf(_norm_bwd(d_gn * gng, y, rstd))
            g0 = gstate[h]
            gb = _bf(g0)
            dscb = _bf(_dg(drb, vb, 1, 1) * md)
            dqr = _dg(dscb, kb, 1, 0) + _dg(drb, s0b, 1, 1) * qd
            dkr = _dg(dscb, qb, 0, 0) + _dg(vb, gb, 1, 1) * kd
            dv = _dg(scb, drb, 0, 0) + _dg(kdb, gb, 1, 0)
            gstate[h] = g0 * cd_ref[h, 0:1, :] + _dg(qdb, drb, 0, 0)
            dp_ref[:, 0 * 512 + h * 128:0 * 512 + (h + 1) * 128] = _bf(_rot_t(dqr, cosf, sinf))
            dp_ref[:, 1 * 512 + h * 128:1 * 512 + (h + 1) * 128] = _bf(_rot_t(dkr, cosf, sinf) * (128 ** -0.5))
            dp_ref[:, 2 * 512 + h * 128:2 * 512 + (h + 1) * 128] = _bf(dv)
            dp_ref[:, 3 * 512 + h * 128:3 * 512 + (h + 1) * 128] = _bf(d_o * gn * (sg * (1.0 + rg * (1.0 - sg))))

    acc = pl.BlockSpec((1, 512), lambda n: (0, 0))
    return pl.pallas_call(
        kernel_body, name="ret_bwd", grid=(N,),
        out_shape=(jax.ShapeDtypeStruct((S, 2048), BF16), jax.ShapeDtypeStruct((1, 512), F32),
                   jax.ShapeDtypeStruct((1, 512), F32)),
        in_specs=[col(0), col(1), col(2), col(3), tab, tab, dec, dec, dec, cdec, vec, vec, st,
                  pl.BlockSpec((CHUNK, 512), lambda n: (N - 1 - n, 0))],
        out_specs=(pl.BlockSpec((CHUNK, 2048), lambda n: (N - 1 - n, 0)), acc, acc),
        scratch_shapes=[pltpu.VMEM((4, CHUNK, CHUNK), F32)],
        compiler_params=_cparams(("arbitrary",)),
    )(proj, proj, proj, proj, *tables, gn_g, gn_b, states, d_out)


SB_T = 256
SB_SCALE = 64 ** -0.5
SB_Q_COL, SB_K_COL, SB_V_COL = 2048 // 128, 2560 // 128, 3072 // 128


def _head_masks():
    lane = lax.broadcasted_iota(jnp.int32, (1, 128), 1)
    m0 = (lane < 64).astype(F32)
    return m0, 1.0 - m0


def _tri(n, cmp):
    r = lax.broadcasted_iota(jnp.int32, (n, n), 0)
    c = lax.broadcasted_iota(jnp.int32, (n, n), 1)
    return cmp(r, c)


def _tri_sum(x, tri):
    hi = _bf(x)
    lo = _bf(x - hi.astype(F32))
    return _dg(hi, tri, 1, 0) + _dg(lo, tri, 1, 0)


def _sb_weights(qm, kblk, upper, carry, causal):
    z = _dg(qm, kblk, 1, 1)
    lg = -(jnp.maximum(z, 0.0) + jnp.log(1.0 + jnp.exp(-jnp.abs(z))))
    if causal is not None:
        lg = jnp.where(causal, lg, 0.0)
    a = jnp.exp(lg + z + (_tri_sum(lg, upper) + carry))
    if causal is not None:
        a = jnp.where(causal, a, 0.0)
    return a, carry + jnp.sum(lg, axis=1, keepdims=True)


def _sb_fwd(proj, comm=None):
    S = proj.shape[0]
    T = min(SB_T, S)
    nq = S // T

    def body(q_ref, k_ref, v_ref, o_ref, kb_ref, vm_ref, acc_ref):
        i = pl.program_id(1)
        m0, m1 = _head_masks()

        @pl.when(i == 0)
        def _():
            v = v_ref[...]
            kb_ref[...] = _bf(k_ref[...])
            vm_ref[0] = _bf(v * m0)
            vm_ref[1] = _bf(v * m1)

        q = q_ref[...]
        qm = (_bf(q * (m0 * SB_SCALE)), _bf(q * (m1 * SB_SCALE)))
        upper = _tri(T, lambda r, c: r > c).astype(BF16)
        causal = _tri(T, lambda r, c: c < r)

        def tile(j, carry, mask, first):
            ks = pl.multiple_of(j * T, T)
            kblk = kb_ref[pl.ds(ks, T), :]
            out = []
            for h in range(2):
                a, c = _sb_weights(qm[h], kblk, upper, carry[h], mask)
                part = _dg(_bf(a), vm_ref[h, pl.ds(ks, T), :], 1, 0)
                if first and h == 0:
                    acc_ref[...] = part
                else:
                    acc_ref[...] += part
                out.append(c)
            return tuple(out)

        zero = jnp.zeros((T, 1), F32)
        carry = tile(i, (zero, zero), causal, True)
        lax.fori_loop(0, i, lambda jj, c: tile(i - 1 - jj, c, None, False), carry)
        o_ref[...] = acc_ref[...]

    full = lambda c: pl.BlockSpec((S, 128), lambda p, i, c=c: (0, c + p))
    outs, landed = _pcall(
        body, name="sb_fwd", grid=(4, nq), out_shape=[jax.ShapeDtypeStruct((S, SB_W), F32)],
        in_specs=[pl.BlockSpec((T, 128), lambda p, i: (i, SB_Q_COL + p)), full(SB_K_COL), full(SB_V_COL)],
        out_specs=[pl.BlockSpec((T, 128), lambda p, i: (i, p))],
        scratch_shapes=[pltpu.VMEM((S, 128), BF16), pltpu.VMEM((2, S, 128), BF16), pltpu.VMEM((T, 128), F32)],
        sem=("arbitrary", "arbitrary"), args=(proj, proj, proj), comm=comm)
    return outs[0] if comm is None else (outs[0], landed)


def _sb_bwd(proj, d_o, comm=None):
    S = proj.shape[0]
    T = min(SB_T, S)
    nq = S // T

    def body(q_ref, k_ref, v_ref, do_ref, dq_ref, dk_ref, dv_ref, kb_ref, kbm_ref, vb_ref, e_ref, dq_acc, dk_acc, dv_acc):
        i = pl.program_id(1)
        m0, m1 = _head_masks()

        @pl.when(i == 0)
        def _():
            k = k_ref[...]
            kb_ref[...] = _bf(k)
            kbm_ref[0] = _bf(k * m0)
            kbm_ref[1] = _bf(k * m1)
            vb_ref[...] = _bf(v_ref[...])
            dk_acc[...] = jnp.zeros_like(dk_acc)
            dv_acc[...] = jnp.zeros_like(dv_acc)

        q, d_out = q_ref[...], do_ref[...]
        qm = (_bf(q * (m0 * SB_SCALE)), _bf(q * (m1 * SB_SCALE)))
        dom = (_bf(d_out * m0), _bf(d_out * m1))
        upper = _tri(T, lambda r, c: r > c).astype(BF16)
        lower = _tri(T, lambda r, c: r < c).astype(BF16)
        causal = _tri(T, lambda r, c: c < r)

        def down(j, carry, mask):
            ks = pl.multiple_of(j * T, T)
            kblk, vblk = kb_ref[pl.ds(ks, T), :], vb_ref[pl.ds(ks, T), :]
            out = []
            for h in range(2):
                a, c = _sb_weights(qm[h], kblk, upper, carry[h], mask)
                e_ref[h, j] = a * _dg(dom[h], vblk, 1, 1)
                dv_acc[pl.ds(ks, T), :] += _dg(_bf(a), dom[h], 0, 0)
                out.append(c)
            return tuple(out)

        def up(j, carry, mask, first):
            ks = pl.multiple_of(j * T, T)
            kblk = kb_ref[pl.ds(ks, T), :]
            out = []
            for h in range(2):
                z = _dg(qm[h], kblk, 1, 1)
                en = jnp.exp(-jnp.abs(z))
                inv = 1.0 / (1.0 + en)
                beta = jnp.where(z >= 0.0, inv, en * inv)
                e = e_ref[h, j]
                d_lg = _tri_sum(e, lower) + carry[h]
                dz = e * (1.0 - beta) - d_lg * beta
                if mask is not None:
                    dz = jnp.where(mask, dz, 0.0)
                dz = _bf(dz)
                part = _dg(dz, kbm_ref[h, pl.ds(ks, T), :], 1, 0)
                if first and h == 0:
                    dq_acc[...] = part
                else:
                    dq_acc[...] += part
                dk_acc[pl.ds(ks, T), :] += _dg(dz, qm[h], 0, 0)
                out.append(carry[h] + jnp.sum(e, axis=1, keepdims=True))
            return tuple(out)

        zero = jnp.zeros((T, 1), F32)
        carry = down(i, (zero, zero), causal)
        lax.fori_loop(0, i, lambda jj, c: down(i - 1 - jj, c, None), carry)

        @pl.when(i == 0)
        def _():
            up(0, (zero, zero), causal, True)

        @pl.when(i > 0)
        def _():
            carry = up(0, (zero, zero), None, True)
            carry = lax.fori_loop(1, i, lambda j, c: up(j, c, None, False), carry)
            up(i, carry, causal, False)

        dq_ref[...] = _bf(dq_acc[...] * SB_SCALE)

        @pl.when(i == nq - 1)
        def _():
            dk_ref[...] = _bf(dk_acc[...])
            dv_ref[...] = _bf(dv_acc[...])

    full = lambda c: pl.BlockSpec((S, 128), lambda p, i, c=c: (0, c + p))
    tile = pl.BlockSpec((T, 128), lambda p, i: (i, p))
    acc = pl.BlockSpec((S, 128), lambda p, i: (0, p))
    out = jax.ShapeDtypeStruct((S, SB_W), BF16)
    outs, landed = _pcall(
        body, name="sb_bwd", grid=(4, nq), out_shape=[out, out, out],
        in_specs=[pl.BlockSpec((T, 128), lambda p, i: (i, SB_Q_COL + p)), full(SB_K_COL), full(SB_V_COL), tile],
        out_specs=[tile, acc, acc],
        scratch_shapes=[pltpu.VMEM((S, 128), BF16), pltpu.VMEM((2, S, 128), BF16), pltpu.VMEM((S, 128), BF16),
                        pltpu.VMEM((2, nq, T, T), F32), pltpu.VMEM((T, 128), F32), pltpu.VMEM((S, 128), F32),
                        pltpu.VMEM((S, 128), F32)],
        sem=("arbitrary", "arbitrary"), args=(proj, proj, proj, d_o), comm=comm)
    return tuple(outs) if comm is None else (tuple(outs), landed)


SGU_U_COL, SGU_V_COL = 3584 // 512, 4096 // 512


def _causal(w):
    r = lax.broadcasted_iota(jnp.int32, (CHUNK, CHUNK), 0)
    c = lax.broadcasted_iota(jnp.int32, (CHUNK, CHUNK), 1)
    return jnp.where(r >= c, w, 0.0)


def _sgu_fwd(proj, ln_g, ln_b, w, b):
    S = proj.shape[0]
    N = S // CHUNK

    def body(u_ref, v_ref, g_ref, b_ref, w_ref, bias_ref, out_ref):
        u = _gelu(u_ref[...])
        xh, _ = _norm_stats(_gelu(v_ref[...]))
        vn = _bf(xh * g_ref[...] + b_ref[...])
        for g in range(4):
            sl = slice(g * 128, (g + 1) * 128)
            sv = _dg(_bf(_causal(w_ref[g])), vn[:, sl], 1, 0) + bias_ref[g]
            out_ref[:, sl] = u[:, sl] * sv

    vec = pl.BlockSpec((1, 512), lambda n: (0, 0))
    return pl.pallas_call(
        body, name="sgu_fwd", grid=(N,),
        out_shape=jax.ShapeDtypeStruct((S, SGU_W), F32),
        in_specs=[pl.BlockSpec((CHUNK, 512), lambda n: (n, SGU_U_COL)),
                  pl.BlockSpec((CHUNK, 512), lambda n: (n, SGU_V_COL)), vec, vec,
                  pl.BlockSpec((4, CHUNK, CHUNK), lambda n: (0, 0, 0)), pl.BlockSpec((4, CHUNK, 1), lambda n: (0, 0, 0))],
        out_specs=pl.BlockSpec((CHUNK, 512), lambda n: (n, 0)),
        compiler_params=_cparams(("parallel",)),
    )(proj, proj, ln_g, ln_b, w, b)


def _sgu_bwd(proj, ln_g, ln_b, w, b, d_out):
    S = proj.shape[0]
    N = S // CHUNK

    def body(u_ref, v_ref, g_ref, b_ref, w_ref, bias_ref, do_ref, dp_ref, dg_ref, db_ref, dw_ref, dbias_ref):
        @pl.when(pl.program_id(0) == 0)
        def _():
            dg_ref[...] = jnp.zeros_like(dg_ref)
            db_ref[...] = jnp.zeros_like(db_ref)
            dw_ref[...] = jnp.zeros_like(dw_ref)
            dbias_ref[...] = jnp.zeros_like(dbias_ref)

        gu, gv = u_ref[...], v_ref[...]
        u = _gelu(gu)
        xh, rstd = _norm_stats(_gelu(gv))
        ln_gain = g_ref[...]
        vn = _bf(xh * ln_gain + b_ref[...])
        d_o = do_ref[...]
        d_vn = []
        for g in range(4):
            sl = slice(g * 128, (g + 1) * 128)
            wc = _bf(_causal(w_ref[g]))
            sv = _dg(wc, vn[:, sl], 1, 0) + bias_ref[g]
            dp_ref[:, sl] = _bf(d_o[:, sl] * sv * _gelu_grad(gu[:, sl]))
            d_sv = d_o[:, sl] * u[:, sl]
            dbias_ref[g] += jnp.sum(d_sv, axis=1, keepdims=True)
            d_svb = _bf(d_sv)
            dw_ref[g] += _causal(_dg(d_svb, vn[:, sl], 1, 1))
            d_vn.append(_dg(wc, d_svb, 0, 0))
        d_vn = jnp.concatenate(d_vn, axis=1)
        dg_ref[...] += jnp.sum(d_vn * xh, axis=0, keepdims=True)
        db_ref[...] += jnp.sum(d_vn, axis=0, keepdims=True)
        dp_ref[:, 512:1024] = _bf(_norm_bwd(d_vn * ln_gain, xh, rstd) * _gelu_grad(gv))

    vec = pl.BlockSpec((1, 512), lambda n: (0, 0))
    wspec = pl.BlockSpec((4, CHUNK, CHUNK), lambda n: (0, 0, 0))
    bspec = pl.BlockSpec((4, CHUNK, 1), lambda n: (0, 0, 0))
    return pl.pallas_call(
        body, name="sgu_bwd", grid=(N,),
        out_shape=(jax.ShapeDtypeStruct((S, 1024), BF16), jax.ShapeDtypeStruct((1, 512), F32),
                   jax.ShapeDtypeStruct((1, 512), F32), jax.ShapeDtypeStruct((4, CHUNK, CHUNK), F32),
                   jax.ShapeDtypeStruct((4, CHUNK, 1), F32)),
        in_specs=[pl.BlockSpec((CHUNK, 512), lambda n: (n, SGU_U_COL)),
                  pl.BlockSpec((CHUNK, 512), lambda n: (n, SGU_V_COL)), vec, vec, wspec, bspec,
                  pl.BlockSpec((CHUNK, 512), lambda n: (n, 0))],
        out_specs=(pl.BlockSpec((CHUNK, 1024), lambda n: (n, 0)), vec, vec, wspec, bspec),
        compiler_params=_cparams(("arbitrary",)),
    )(proj, proj, ln_g, ln_b, w, b, d_out)


GATE_COL = 4608 // 512


def _merge_fwd(proj, branches, p_list, tm=512):
    S = proj.shape[0]
    tm = min(tm, S)

    def body(r_ref, s_ref, g_ref, pr_ref, ps_ref, pg_ref, gr_ref, gs_ref, gg_ref, m_ref, br_ref):
        acc = None
        for k, (x_ref, p_ref, gate_ref) in enumerate(((r_ref, pr_ref, gr_ref), (s_ref, ps_ref, gs_ref),
                                                      (g_ref, pg_ref, gg_ref))):
            br = _dg(_bf(x_ref[...]), _bf(p_ref[...]), 1, 0)
            br_ref[k] = br
            term = _sigmoid(gate_ref[...]) * br
            acc = term if acc is None else acc + term
        m_ref[...] = _bf(acc)

    xs = pl.BlockSpec((tm, 512), lambda i, n: (i, 0))
    ps = pl.BlockSpec((512, 512), lambda i, n: (0, n))
    gate = lambda k: pl.BlockSpec((tm, 512), lambda i, n, k=k: (i, GATE_COL + 2 * k + n))
    return pl.pallas_call(
        body, name="merge_fwd", grid=(S // tm, 2),
        out_shape=(jax.ShapeDtypeStruct((S, D_MODEL), BF16), jax.ShapeDtypeStruct((3, S, D_MODEL), F32)),
        in_specs=[xs, xs, xs, ps, ps, ps, gate(0), gate(1), gate(2)],
        out_specs=(pl.BlockSpec((tm, 512), lambda i, n: (i, n)), pl.BlockSpec((3, tm, 512), lambda i, n: (0, i, n))),
        compiler_params=_cparams(("parallel", "parallel")),
    )(*branches, *p_list, proj, proj, proj)


def _gate_bwd(proj, br, d_merged, tm=512):
    S = proj.shape[0]
    tm = min(tm, S)

    def body(dm_ref, br_ref, gr_ref, gs_ref, gg_ref, dbr_ref, dgate_ref):
        dm = dm_ref[...]
        for k, gate_ref in enumerate((gr_ref, gs_ref, gg_ref)):
            s = _sigmoid(gate_ref[...])
            dbr_ref[k] = _bf(dm * s)
            dgate_ref[k] = _bf(dm * br_ref[k] * (s * (1.0 - s)))

    gate = lambda k: pl.BlockSpec((tm, 512), lambda i, n, k=k: (i, GATE_COL + 2 * k + n))
    three = pl.BlockSpec((3, tm, 512), lambda i, n: (0, i, n))
    return pl.pallas_call(
        body, name="gate_bwd", grid=(S // tm, 2),
        out_shape=(jax.ShapeDtypeStruct((3, S, D_MODEL), BF16), jax.ShapeDtypeStruct((3, S, D_MODEL), BF16)),
        in_specs=[pl.BlockSpec((tm, 512), lambda i, n: (i, n)), three, gate(0), gate(1), gate(2)],
        out_specs=(three, three),
        compiler_params=_cparams(("parallel", "parallel")),
    )(d_merged, br, proj, proj, proj)


def _ln_bwd(dy, u, g, tm=256):
    S, D = u.shape
    tm = min(tm, S)

    def body(dy_ref, u_ref, g_ref, du_ref, dg_ref, db_ref):
        @pl.when(pl.program_id(0) == 0)
        def _():
            dg_ref[...] = jnp.zeros_like(dg_ref)
            db_ref[...] = jnp.zeros_like(db_ref)

        dy_t = dy_ref[...]
        xh, rstd = _norm_stats(u_ref[...])
        dg_ref[...] += jnp.sum(dy_t * xh, axis=0, keepdims=True)
        db_ref[...] += jnp.sum(dy_t, axis=0, keepdims=True)
        du_ref[...] = _norm_bwd(dy_t * g_ref[...], xh, rstd)

    tile = pl.BlockSpec((tm, D), lambda i: (i, 0))
    vec = pl.BlockSpec((1, D), lambda i: (0, 0))
    return pl.pallas_call(
        body, name="ln_bwd", grid=(S // tm,),
        out_shape=(jax.ShapeDtypeStruct((S, D), F32), jax.ShapeDtypeStruct((1, D), F32),
                   jax.ShapeDtypeStruct((1, D), F32)),
        in_specs=[tile, tile, vec], out_specs=(tile, vec, vec),
        compiler_params=_cparams(("arbitrary",)),
    )(dy, u, g)


def _loss_grad(y, target, tm=256):
    S, D = y.shape
    tm = min(tm, S)

    def body(y_ref, t_ref, dy_ref, sq_ref):
        @pl.when(pl.program_id(0) == 0)
        def _():
            sq_ref[...] = jnp.zeros_like(sq_ref)

        err = y_ref[...] - t_ref[...]
        dy_ref[...] = err * (1.0 / D)
        sq_ref[...] += jnp.sum(err * err, axis=0, keepdims=True)

    tile = pl.BlockSpec((tm, D), lambda i: (i, 0))
    vec = pl.BlockSpec((1, D), lambda i: (0, 0))
    return pl.pallas_call(
        body, name="loss_grad", grid=(S // tm,),
        out_shape=(jax.ShapeDtypeStruct((S, D), F32), jax.ShapeDtypeStruct((1, D), F32)),
        in_specs=[tile, tile], out_specs=(tile, vec),
        compiler_params=_cparams(("arbitrary",)),
    )(y, target)


def _layer_fwd(x, W, tables, sb_comm=None):
    proj = _matmul(x, W["w_in"], "nn", name="proj")
    retg, states = _ret_fwd(proj, tables, W["ret_gn_g"], W["ret_gn_b"])
    if sb_comm is None:
        sb = _sb_fwd(proj)
    else:
        sb, landed = _sb_fwd(proj, comm=sb_comm[0])
        sb_comm[1](landed)
    sg = _sgu_fwd(proj, W["sgu_ln_g"], W["sgu_ln_b"], W["sgu_w"], W["sgu_b"])
    merged, br = _merge_fwd(proj, (retg, sb, sg), (W["p_ret"], W["p_sb"], W["p_sgu"]))
    u1, x1 = _matmul(merged, W["w_out"], "nn", name="out_ln", epi="ln", extra=(x, W["ln1_g"], W["ln1_b"]))
    hpre, act = _matmul(x1, W["w_up"], "nn", name="up", epi="relu2")
    u2, x2 = _matmul(act, W["w_down"], "nn", name="down_ln", epi="ln", extra=(x1, W["ln2_g"], W["ln2_b"]))
    saved = dict(x=x, proj=proj, retg=retg, states=states, sb=sb, sg=sg, merged=merged, br=br, u1=u1, x1=x1,
                 hpre=hpre, act=act, u2=u2)
    return x2, saved


def _layer_bwd(d_x2, W, tables, sv, chunk_dtype=None, sb_comm_fn=None, dx_comm_fn=None):
    dt = F32 if chunk_dtype is None else chunk_dtype
    rows, cols = (None, None) if chunk_dtype is None else ("rows", "cols")
    g, landed = {}, {}
    du2, g["ln2_g"], g["ln2_b"] = _ln_bwd(d_x2, sv["u2"], W["ln2_g"])
    d_hpre = _matmul(du2, W["w_down"], "nt", name="d_act", epi="drelu2", extra=(sv["hpre"],), out_dtype=BF16)
    g["w_down"] = _matmul(sv["act"], du2, "tn", name="dw_down", out_dtype=dt, chunks=rows)
    g["w_up"] = _matmul(sv["x1"], d_hpre, "tn", name="dw_up", out_dtype=dt, chunks=cols)
    d_x1 = _matmul(d_hpre, W["w_up"], "nt", name="d_x1", epi="add", extra=(du2,))
    du1, g["ln1_g"], g["ln1_b"] = _ln_bwd(d_x1, sv["u1"], W["ln1_g"])
    d_merged = _matmul(du1, W["w_out"], "nt", name="d_merged")
    g["w_out"] = _matmul(sv["merged"], du1, "tn", name="dw_out", out_dtype=dt, chunks=rows)
    d_br, d_gate = _gate_bwd(sv["proj"], sv["br"], d_merged)
    d_branch = []
    for k, (nm, act) in enumerate((("p_ret", sv["retg"]), ("p_sb", sv["sb"]), ("p_sgu", sv["sg"]))):
        d_branch.append(_matmul(d_br[k], W[nm], "nt", name="d_" + nm[2:]))
        g[nm] = _matmul(act, d_br[k], "tn", name="dw_" + nm[2:], out_dtype=dt, chunks=cols)
    d_ret, g["ret_gn_g"], g["ret_gn_b"] = _ret_bwd(sv["proj"], tables, W["ret_gn_g"], W["ret_gn_b"], sv["states"],
                                                   d_branch[0])
    if sb_comm_fn is None:
        d_sq, d_sk, d_sv = _sb_bwd(sv["proj"], d_branch[1])
    else:
        (d_sq, d_sk, d_sv), landed["sb"] = _sb_bwd(sv["proj"], d_branch[1], comm=sb_comm_fn(g))
    d_sgu, g["sgu_ln_g"], g["sgu_ln_b"], g["sgu_w"], g["sgu_b"] = _sgu_bwd(
        sv["proj"], W["sgu_ln_g"], W["sgu_ln_b"], W["sgu_w"], W["sgu_b"], d_branch[2])
    d_proj = jnp.concatenate([d_ret, d_sq, d_sk, d_sv, d_sgu, d_gate[0], d_gate[1], d_gate[2]], axis=1)
    g["w_in"] = _matmul(sv["x"], d_proj, "tn", name="dw_in", out_dtype=dt)
    if chunk_dtype is not None:
        g["w_in"] = g["w_in"].reshape(D_MODEL, N_DEV, N_IN // N_DEV).transpose(1, 0, 2)
    if dx_comm_fn is None:
        d_x = _matmul(d_proj, W["w_in"], "nt", name="d_x", epi="add", extra=(du1,))
    else:
        d_x, landed["dx"] = _matmul(d_proj, W["w_in"], "nt", name="d_x", epi="add", extra=(du1,), comm=dx_comm_fn(g))
    return d_x, g, landed


BIG = ("w_in", "p_ret", "p_sb", "p_sgu", "w_out", "w_up", "w_down")
SMALL = ("ret_gn_g", "ret_gn_b", "sgu_ln_g", "sgu_ln_b", "sgu_w", "sgu_b", "ln1_g", "ln1_b", "ln2_g", "ln2_b")
GATHER_KIND = {"w_in": "slab", "p_ret": "cols", "p_sb": "cols", "p_sgu": "cols", "w_out": "rows", "w_up": "cols",
               "w_down": "rows"}


def _small_weights(small, l):
    W = {}
    for n in SMALL:
        if n == "sgu_w":
            W[n] = small[n][l]
        elif n == "sgu_b":
            W[n] = small[n][l].reshape(4, CHUNK, 1)
        else:
            W[n] = small[n][l].reshape(1, -1)
    return W


def _local_step(x, target, full, small):
    tables = _ret_tables(x.shape[0])
    Ws = [{**{n: full[n][l] for n in BIG}, **_small_weights(small, l)} for l in range(DEPTH)]
    saved = []
    h = x
    for l in range(DEPTH):
        h, sv = _layer_fwd(h, Ws[l], tables)
        saved.append(sv)
    d_h, sq = _loss_grad(h, target)
    grads = [None] * DEPTH
    for l in reversed(range(DEPTH)):
        d_h, grads[l], _ = _layer_bwd(d_h, Ws[l], tables, saved[l])
    return sq, d_h, grads


def _adam(w, parts, m, v, name, tr=256):
    L, R, C = w.shape
    tr = min(tr, R)
    assert R % tr == 0 and len(parts) == L

    def body(*refs):
        w_ref, p_refs, (m_ref, v_ref, g_ref, d_ref, nm_ref, nv_ref) = refs[0], refs[1:1 + L], refs[1 + L:]
        layer = pl.program_id(0)
        g = None
        for li, p_ref in enumerate(p_refs):
            s = p_ref[0].astype(F32)
            for j in range(1, N_DEV):
                s = s + p_ref[j].astype(F32)
            g = s if g is None else jnp.where(layer == li, s, g)
        m2 = ADAM_B1 * m_ref[...] + (1.0 - ADAM_B1) * g
        v2 = ADAM_B2 * v_ref[...] + (1.0 - ADAM_B2) * (g * g)
        m_hat = m2 / (1.0 - ADAM_B1 ** ADAM_STEP)
        v_hat = v2 / (1.0 - ADAM_B2 ** ADAM_STEP)
        g_ref[...] = g
        d_ref[...] = -ADAM_LR * (m_hat / (jnp.sqrt(v_hat) + ADAM_EPS) + ADAM_WD * w_ref[...])
        nm_ref[...] = m2
        nv_ref[...] = v2

    tile = pl.BlockSpec((None, tr, C), lambda l, i: (l, i, 0))
    part = lambda li: pl.BlockSpec((N_DEV, tr, C), lambda l, i, li=li: (0, jnp.where(l == li, i, 0), 0))
    out = jax.ShapeDtypeStruct((L, R, C), F32)
    return pl.pallas_call(
        body, name=name, grid=(L, R // tr), out_shape=(out, out, out, out),
        in_specs=[tile] + [part(li) for li in range(L)] + [tile, tile],
        out_specs=(tile, tile, tile, tile),
        compiler_params=_cparams(("parallel", "parallel")),
    )(w, *parts, m, v)


def _pack_small(tree):
    return jnp.concatenate([tree[n].reshape(-1, 128) for n in SMALL], axis=0)


def _unpack_small(packed, like):
    out, r = {}, 0
    for n in SMALL:
        rows = like[n].size // 128
        out[n] = packed[r:r + rows].reshape(like[n].shape)
        r += rows
    return out


WEIGHTS = ("w_in", "ret_gn_g", "ret_gn_b", "sgu_ln_g", "sgu_ln_b", "sgu_w", "sgu_b", "p_ret", "p_sb", "p_sgu", "w_out",
           "ln1_g", "ln1_b", "w_up", "w_down", "ln2_g", "ln2_b")


def kernel(x, w_in, ret_gn_g, ret_gn_b, sgu_ln_g, sgu_ln_b, sgu_w, sgu_b, p_ret, p_sb, p_sgu, w_out, ln1_g, ln1_b, w_up, w_down, ln2_g, ln2_b, loss_target, m_w_in, m_ret_gn_g, m_ret_gn_b, m_sgu_ln_g, m_sgu_ln_b, m_sgu_w, m_sgu_b, m_p_ret, m_p_sb, m_p_sgu, m_w_out, m_ln1_g, m_ln1_b, m_w_up, m_w_down, m_ln2_g, m_ln2_b, v_w_in, v_ret_gn_g, v_ret_gn_b, v_sgu_ln_g, v_sgu_ln_b, v_sgu_w, v_sgu_b, v_p_ret, v_p_sb, v_p_sgu, v_w_out, v_ln1_g, v_ln1_b, v_w_up, v_w_down, v_ln2_g, v_ln2_b):
    w = dict(zip(WEIGHTS, (w_in, ret_gn_g, ret_gn_b, sgu_ln_g, sgu_ln_b, sgu_w, sgu_b, p_ret, p_sb, p_sgu, w_out,
                           ln1_g, ln1_b, w_up, w_down, ln2_g, ln2_b)))
    m = dict(zip(WEIGHTS, (m_w_in, m_ret_gn_g, m_ret_gn_b, m_sgu_ln_g, m_sgu_ln_b, m_sgu_w, m_sgu_b, m_p_ret, m_p_sb,
                           m_p_sgu, m_w_out, m_ln1_g, m_ln1_b, m_w_up, m_w_down, m_ln2_g, m_ln2_b)))
    v = dict(zip(WEIGHTS, (v_w_in, v_ret_gn_g, v_ret_gn_b, v_sgu_ln_g, v_sgu_ln_b, v_sgu_w, v_sgu_b, v_p_ret, v_p_sb,
                           v_p_sgu, v_w_out, v_ln1_g, v_ln1_b, v_w_up, v_w_down, v_ln2_g, v_ln2_b)))

    small = {n: w[n] for n in SMALL}
    shard = {n: _bf(w[n]) for n in BIG}
    tables = _ret_tables(x.shape[1])
    Ws = [_small_weights(small, l) for l in range(DEPTH)]
    unslab = lambda z: z.transpose(1, 0, 2).reshape(D_MODEL, N_IN)

    (slabs,) = _exchange([_gather_transfer(shard["w_in"], 0, "slab")], "gather_w_in0")
    Ws[0]["w_in"] = unslab(slabs)
    later = [(n, 0) for n in BIG[1:]] + [(n, 1) for n in BIG]

    def weights_landed(landed):
        for (n, l), z in zip(later, landed):
            Ws[l][n] = unslab(z) if n == "w_in" else z

    gather = _Comm([_gather_transfer(shard[n], l, GATHER_KIND[n]) for n, l in later])
    h, saved0 = _layer_fwd(x[0], Ws[0], tables, sb_comm=(gather, weights_landed))
    h, saved1 = _layer_fwd(h, Ws[1], tables)
    d_h, sq = _loss_grad(h, loss_target[0])
    loss = lax.psum(0.5 * jnp.sum(sq) / D_MODEL, ("x", "y", "c"))

    d_h, g1, _ = _layer_bwd(d_h, Ws[1], tables, saved1, chunk_dtype=BF16)
    early = [(n, 1) for n in BIG] + [(n, 0) for n in BIG[1:]]

    def early_scatter(g0):
        return _Comm([_scatter_transfer((g1 if l else g0)[n]) for n, l in early])

    def late_scatter(g0):
        mine = _pack_small({n: jnp.stack([g0[n].reshape(small[n].shape[1:]), g1[n].reshape(small[n].shape[1:])])
                            for n in SMALL})
        return _Comm([_scatter_transfer(g0["w_in"]), _slab_transfer(mine)])

    d_x, g0, landed = _layer_bwd(d_h, Ws[0], tables, saved0, chunk_dtype=BF16, sb_comm_fn=early_scatter,
                                 dx_comm_fn=late_scatter)
    parts = dict(zip(early, landed["sb"]))
    parts[("w_in", 0)], small_parts = landed["dx"]

    grad, delta, new_m, new_v = {}, {}, {}, {}
    for n in BIG:
        grad[n], delta[n], new_m[n], new_v[n] = _adam(w[n], [parts[(n, l)] for l in range(DEPTH)], m[n], v[n],
                                                      "adam_" + n)
    res = _adam(_pack_small(small)[None], [small_parts], _pack_small({n: m[n] for n in SMALL})[None],
                _pack_small({n: v[n] for n in SMALL})[None], "adam_small", tr=small_parts.shape[1])
    for tree, packed in zip((grad, delta, new_m, new_v), res):
        tree.update(_unpack_small(packed[0], small))

    return (loss, d_x[None], *[grad[n] for n in WEIGHTS], *[delta[n] for n in WEIGHTS],
            *[new_m[n] for n in WEIGHTS], *[new_v[n] for n in WEIGHTS])
```

```python
import functools
import math

import numpy as np
import jax
import jax.numpy as jnp
from jax import lax
from jax.experimental import pallas as pl
from jax.experimental.pallas import tpu as pltpu

F32 = jnp.float32
BF16 = jnp.bfloat16

N_DEV = 8
DEPTH = 2
D_MODEL = 1024
CHUNK = 128
RET_W = 512
SB_W = 512
SGU_W = 512
N_IN = 7680
LN_EPS = 1e-5
ALPHA = (2 * DEPTH) ** 0.25
ROPE_BASE = 10000.0
ADAM_LR, ADAM_B1, ADAM_B2, ADAM_EPS, ADAM_WD, ADAM_STEP = 0.001, 0.9, 0.999, 1e-08, 0.01, 10
VMEM_LIMIT = 56 * 1024 * 1024

_GELU_K = math.sqrt(2.0 / math.pi)
_GELU_C = 0.044715


def _cparams(sem=None):
    return pltpu.CompilerParams(dimension_semantics=sem, vmem_limit_bytes=VMEM_LIMIT)


def _dg(a, b, ca, cb):
    return lax.dot_general(a, b, (((ca,), (cb,)), ((), ())), preferred_element_type=F32)


def _bf(x):
    return x.astype(BF16)


def _sigmoid(x):
    return 1.0 / (1.0 + jnp.exp(-x))


def _gelu(x):
    t = jnp.tanh(_GELU_K * (x + _GELU_C * (x * x * x)))
    return x * (0.5 * (1.0 + t))


def _gelu_grad(x):
    t = jnp.tanh(_GELU_K * (x + _GELU_C * (x * x * x)))
    return 0.5 * (1.0 + t) + 0.5 * x * (1.0 - t * t) * (_GELU_K * (1.0 + 3.0 * _GELU_C * x * x))


def _norm_stats(u):
    mu = jnp.mean(u, axis=-1, keepdims=True)
    d = u - mu
    var = jnp.mean(d * d, axis=-1, keepdims=True)
    rstd = lax.rsqrt(var + LN_EPS)
    return d * rstd, rstd


def _norm_bwd(dxh, xh, rstd):
    return rstd * (dxh - jnp.mean(dxh, axis=-1, keepdims=True) - xh * jnp.mean(dxh * xh, axis=-1, keepdims=True))


class _Transfer:
    def __init__(self, src, dst_shape, src_at, dst_at):
        self.src, self.dst_shape, self.src_at, self.dst_at = src, tuple(dst_shape), src_at, dst_at


def _gather_transfer(shard, l, kind):
    _, r, c = shard.shape
    src_at = lambda ref, p: ref.at[l]
    if kind == "slab":
        return _Transfer(shard, (N_DEV, r, c), src_at, lambda ref, s: ref.at[s])
    if kind == "rows":
        return _Transfer(shard, (N_DEV * r, c), src_at, lambda ref, s: ref.at[pl.ds(pl.multiple_of(s * r, r), r), :])
    return _Transfer(shard, (r, N_DEV * c), src_at, lambda ref, s: ref.at[:, pl.ds(pl.multiple_of(s * c, c), c)])


def _scatter_transfer(chunks):
    return _Transfer(chunks, chunks.shape, lambda ref, p: ref.at[p], lambda ref, s: ref.at[s])


def _slab_transfer(arr):
    return _Transfer(arr, (N_DEV,) + arr.shape, lambda ref, p: ref, lambda ref, s: ref.at[s])


class _Comm:
    def __init__(self, transfers):
        self.transfers = list(transfers)
        self.n = len(self.transfers)
        self.arrays = [t.src for t in self.transfers]
        self.out_shape = [jax.ShapeDtypeStruct(t.dst_shape, t.src.dtype) for t in self.transfers]
        self.scratch = [pltpu.SemaphoreType.DMA((self.n * (N_DEV - 1),)), pltpu.SemaphoreType.DMA((self.n * (N_DEV - 1),)),
                        pltpu.SemaphoreType.DMA((self.n,))]

    def _copies(self, srcs, dsts, send_sems, recv_sems, local_sems):
        x, y, c = lax.axis_index("x"), lax.axis_index("y"), lax.axis_index("c")
        me = 4 * x + 2 * y + c
        copies = []
        for d in range(1, N_DEV):
            px = 1 - x if d & 4 else x
            py = 1 - y if d & 2 else y
            pc = 1 - c if d & 1 else c
            peer = 4 * px + 2 * py + pc
            for t, tr in enumerate(self.transfers):
                k = t * (N_DEV - 1) + d - 1
                copies.append(pltpu.make_async_remote_copy(
                    src_ref=tr.src_at(srcs[t], peer), dst_ref=tr.dst_at(dsts[t], me),
                    send_sem=send_sems.at[k], recv_sem=recv_sems.at[k],
                    device_id=(px, py, pc), device_id_type=pl.DeviceIdType.MESH))
        own = [pltpu.make_async_copy(tr.src_at(srcs[t], me), tr.dst_at(dsts[t], me), local_sems.at[t])
               for t, tr in enumerate(self.transfers)]
        return copies, own

    def start(self, srcs, dsts, *sems):
        copies, own = self._copies(srcs, dsts, *sems)
        for cp in own + copies:
            cp.start()

    def wait(self, srcs, dsts, *sems):
        copies, own = self._copies(srcs, dsts, *sems)
        for cp in copies + own:
            cp.wait()


def _pcall(body, *, name, grid, in_specs, out_specs, out_shape, scratch_shapes, sem, args, comm=None):
    in_specs, out_specs, out_shape = list(in_specs), list(out_specs), list(out_shape)
    if comm is None:
        outs = pl.pallas_call(body, name=name, grid=grid, in_specs=in_specs, out_specs=out_specs, out_shape=out_shape,
                              scratch_shapes=list(scratch_shapes), compiler_params=_cparams(sem))(*args)
        return list(outs), []
    n_in, n_out, n_scr, k = len(in_specs), len(out_specs), len(scratch_shapes), comm.n

    def carrier(*refs):
        ins, cin = refs[:n_in], refs[n_in:n_in + k]
        outs, cout = refs[n_in + k:n_in + k + n_out], refs[n_in + k + n_out:n_in + 2 * k + n_out]
        scr, sems = refs[n_in + 2 * k + n_out:n_in + 2 * k + n_out + n_scr], refs[n_in + 2 * k + n_out + n_scr:]
        ids = [pl.program_id(d) for d in range(len(grid))]
        first = functools.reduce(jnp.logical_and, [i == 0 for i in ids])
        last = functools.reduce(jnp.logical_and, [i == g - 1 for i, g in zip(ids, grid)])

        @pl.when(first)
        def _():
            comm.start(cin, cout, *sems)

        body(*ins, *outs, *scr)

        @pl.when(last)
        def _():
            comm.wait(cin, cout, *sems)

    hbm = pl.BlockSpec(memory_space=pl.ANY)
    outs = pl.pallas_call(
        carrier, name=name, grid=grid, in_specs=in_specs + [hbm] * k, out_specs=out_specs + [hbm] * k,
        out_shape=out_shape + comm.out_shape, scratch_shapes=list(scratch_shapes) + comm.scratch,
        compiler_params=_cparams(tuple("arbitrary" for _ in grid)),
    )(*args, *comm.arrays)
    return list(outs[:n_out]), list(outs[n_out:])


def _exchange(transfers, name):
    comm = _Comm(transfers)

    def body(*refs):
        k = comm.n
        comm.start(refs[:k], refs[k:2 * k], *refs[2 * k:])
        comm.wait(refs[:k], refs[k:2 * k], *refs[2 * k:])

    hbm = pl.BlockSpec(memory_space=pl.ANY)
    return pl.pallas_call(body, name=name, out_shape=comm.out_shape, in_specs=[hbm] * comm.n, out_specs=[hbm] * comm.n,
                          scratch_shapes=comm.scratch)(*comm.arrays)


def _matmul(a, b, mode, *, name, epi=None, extra=(), out_dtype=F32, tm=512, tn=512, tk=1024, chunks=None, comm=None):
    if mode == "nn":
        (M, K), N = a.shape, b.shape[1]
    elif mode == "nt":
        (M, K), N = a.shape, b.shape[0]
    else:
        (K, M), N = a.shape, b.shape[1]
    tm, tk = min(tm, M), min(tk, K)
    if K % tk:
        tk = next(c for c in (1536, 1280, 768, 512, 256, 128) if K % c == 0)
    tn = N if epi == "ln" else min(tn, N)
    if epi == "ln":
        tm = min(tm, 256)
    if chunks == "rows":
        tm = min(tm, M // N_DEV)
    if chunks == "cols":
        tn = min(tn, N // N_DEV)
    assert M % tm == 0 and N % tn == 0 and K % tk == 0, (name, M, N, K)
    nk = K // tk
    a_spec = {"nn": pl.BlockSpec((tm, tk), lambda i, j, k: (i, k)),
              "nt": pl.BlockSpec((tm, tk), lambda i, j, k: (i, k)),
              "tn": pl.BlockSpec((tk, tm), lambda i, j, k: (k, i))}[mode]
    b_spec = {"nn": pl.BlockSpec((tk, tn), lambda i, j, k: (k, j)),
              "nt": pl.BlockSpec((tn, tk), lambda i, j, k: (j, k)),
              "tn": pl.BlockSpec((tk, tn), lambda i, j, k: (k, j))}[mode]
    ca, cb = {"nn": (1, 0), "nt": (1, 1), "tn": (0, 0)}[mode]
    tile = pl.BlockSpec((tm, tn), lambda i, j, k: (i, j))
    row = pl.BlockSpec((1, tn), lambda i, j, k: (0, j))
    n_extra = {None: 0, "add": 1, "relu2": 0, "drelu2": 1, "ln": 3}[epi]
    assert len(extra) == n_extra
    extra_specs = {None: [], "add": [tile], "relu2": [], "drelu2": [tile], "ln": [tile, row, row]}[epi]
    if epi == "relu2":
        out_shape = (jax.ShapeDtypeStruct((M, N), F32), jax.ShapeDtypeStruct((M, N), BF16))
        out_specs = (tile, tile)
    elif epi == "ln":
        out_shape = (jax.ShapeDtypeStruct((M, N), F32), jax.ShapeDtypeStruct((M, N), F32))
        out_specs = (tile, tile)
    elif chunks == "rows":
        per = M // N_DEV // tm
        out_shape = (jax.ShapeDtypeStruct((N_DEV, M // N_DEV, N), out_dtype),)
        out_specs = (pl.BlockSpec((None, tm, tn), lambda i, j, k: (i // per, i % per, j)),)
    elif chunks == "cols":
        per = N // N_DEV // tn
        out_shape = (jax.ShapeDtypeStruct((N_DEV, M, N // N_DEV), out_dtype),)
        out_specs = (pl.BlockSpec((None, tm, tn), lambda i, j, k: (j // per, i, j % per)),)
    else:
        out_shape = (jax.ShapeDtypeStruct((M, N), out_dtype),)
        out_specs = (tile,)
    n_out = 2 if epi in ("relu2", "ln") else 1

    def body(*refs):
        a_ref, b_ref = refs[:2]
        ex = refs[2:2 + n_extra]
        outs = refs[2 + n_extra:2 + n_extra + n_out]
        acc_ref = refs[-1]
        k = pl.program_id(2)
        part = _dg(_bf(a_ref[...]), _bf(b_ref[...]), ca, cb)

        def finish(acc):
            if epi is None:
                outs[0][...] = acc.astype(out_dtype)
            elif epi == "add":
                outs[0][...] = (acc + ALPHA * ex[0][...]).astype(out_dtype)
            elif epi == "relu2":
                r = jnp.maximum(acc, 0.0)
                outs[0][...] = acc
                outs[1][...] = _bf(r * r)
            elif epi == "drelu2":
                outs[0][...] = (acc * (2.0 * jnp.maximum(ex[0][...], 0.0))).astype(out_dtype)
            else:
                u = ALPHA * ex[0][...] + acc
                xh, _ = _norm_stats(u)
                outs[0][...] = u
                outs[1][...] = xh * ex[1][...] + ex[2][...]

        if nk == 1:
            finish(part)
        else:
            @pl.when(k == 0)
            def _():
                acc_ref[...] = part

            @pl.when(jnp.logical_and(k > 0, k < nk - 1))
            def _():
                acc_ref[...] += part

            @pl.when(k == nk - 1)
            def _():
                finish(acc_ref[...] + part)

    outs, landed = _pcall(
        body, name=name, out_shape=out_shape, grid=(M // tm, N // tn, nk),
        in_specs=[a_spec, b_spec] + extra_specs, out_specs=out_specs,
        scratch_shapes=[pltpu.VMEM((tm, tn), F32)], sem=("parallel", "parallel", "arbitrary"),
        args=(a, b, *extra), comm=comm)
    res = outs[0] if n_out == 1 else tuple(outs)
    return res if comm is None else (res, landed)


def _ret_tables(S):
    half = 64
    inv_freq = ROPE_BASE ** (-jnp.arange(half, dtype=F32) / half)
    ang = jnp.arange(S, dtype=jnp.int32).astype(F32)[:, None] * inv_freq[None, :]
    cos, sin = jnp.cos(ang), jnp.sin(ang)
    cosf = jnp.concatenate([cos, cos], axis=1)
    sinf = jnp.concatenate([-sin, sin], axis=1)
    log_g = jnp.log(1.0 - 2.0 ** (-5.0 - jnp.arange(4, dtype=F32)))
    idx = jnp.arange(CHUNK, dtype=F32)
    diff = idx[:, None] - idx[None, :]
    md = jnp.where(diff[None] >= 0, jnp.exp(log_g[:, None, None] * diff[None]), 0.0)
    kd = jnp.exp(log_g[:, None] * (CHUNK - 1 - idx)[None, :])
    qd = jnp.exp(log_g[:, None] * (idx + 1.0)[None, :])
    cd = jnp.exp(log_g * CHUNK)
    bc = lambda t: jnp.broadcast_to(t[:, :, None], (4, CHUNK, CHUNK))
    return cosf, sinf, md, bc(qd), bc(kd), jnp.broadcast_to(cd[:, None, None], (4, 8, CHUNK))


def _rot(x, cosf, sinf):
    return x * cosf + pltpu.roll(x, 64, 1) * sinf


def _rot_t(dx, cosf, sinf):
    return dx * cosf - pltpu.roll(dx, 64, 1) * sinf


def _ret_specs(rev, N):
    rn = (lambda n: N - 1 - n) if rev else (lambda n: n)
    col = lambda c: pl.BlockSpec((CHUNK, 512), lambda n, c=c: (rn(n), c))
    tab = pl.BlockSpec((CHUNK, CHUNK), lambda n: (rn(n), 0))
    dec = pl.BlockSpec((4, CHUNK, CHUNK), lambda n: (0, 0, 0))
    cdec = pl.BlockSpec((4, 8, CHUNK), lambda n: (0, 0, 0))
    vec = pl.BlockSpec((1, 512), lambda n: (0, 0))
    st = pl.BlockSpec((1, 4, CHUNK, CHUNK), lambda n: (rn(n), 0, 0, 0))
    return col, tab, dec, cdec, vec, st


def _ret_fwd(proj, tables, gn_g, gn_b):
    S = proj.shape[0]
    N = S // CHUNK
    col, tab, dec, cdec, vec, st = _ret_specs(False, N)

    def body(q_ref, k_ref, v_ref, g_ref, cos_ref, sin_ref, md_ref, qd_ref, kd_ref, cd_ref, gng_ref, gnb_ref,
             out_ref, st_ref, state):
        @pl.when(pl.program_id(0) == 0)
        def _():
            state[...] = jnp.zeros_like(state)

        cosf, sinf = cos_ref[...], sin_ref[...]
        for h in range(4):
            sl = slice(h * 128, (h + 1) * 128)
            qr = _rot(q_ref[:, sl], cosf, sinf)
            kr = _rot(k_ref[:, sl], cosf, sinf) * (128 ** -0.5)
            vb = _bf(v_ref[:, sl])
            s0 = state[h]
            st_ref[0, h] = s0
            sc = _dg(_bf(qr), _bf(kr), 1, 1) * md_ref[h]
            r = _dg(_bf(sc), vb, 1, 0) + _dg(_bf(qr * qd_ref[h]), _bf(s0), 1, 0)
            state[h] = s0 * cd_ref[h, 0:1, :] + _dg(_bf(kr * kd_ref[h]), vb, 0, 0)
            y, _ = _norm_stats(r)
            rg = g_ref[:, sl]
            out_ref[:, sl] = rg * _sigmoid(rg) * (y * gng_ref[:, sl] + gnb_ref[:, sl])

    return pl.pallas_call(
        body, name="ret_fwd", grid=(N,),
        out_shape=(jax.ShapeDtypeStruct((S, RET_W), F32), jax.ShapeDtypeStruct((N, 4, CHUNK, CHUNK), F32)),
        in_specs=[col(0), col(1), col(2), col(3), tab, tab, dec, dec, dec, cdec, vec, vec],
        out_specs=(pl.BlockSpec((CHUNK, 512), lambda n: (n, 0)), st),
        scratch_shapes=[pltpu.VMEM((4, CHUNK, CHUNK), F32)],
        compiler_params=_cparams(("arbitrary",)),
    )(proj, proj, proj, proj, *tables, gn_g, gn_b)


def _ret_bwd(proj, tables, gn_g, gn_b, states, d_out):
    S = proj.shape[0]
    N = S // CHUNK
    col, tab, dec, cdec, vec, st = _ret_specs(True, N)

    def kernel_body(q_ref, k_ref, v_ref, g_ref, cos_ref, sin_ref, md_ref, qd_ref, kd_ref, cd_ref, gng_ref, gnb_ref,
                    st_ref, do_ref, dp_ref, dg_ref, db_ref, gstate):
        @pl.when(pl.program_id(0) == 0)
        def _():
            gstate[...] = jnp.zeros_like(gstate)
            dg_ref[...] = jnp.zeros_like(dg_ref)
            db_ref[...] = jnp.zeros_like(db_ref)

        cosf, sinf = cos_ref[...], sin_ref[...]
        for h in range(4):
            sl = slice(h * 128, (h + 1) * 128)
            qr = _rot(q_ref[:, sl], cosf, sinf)
            kr = _rot(k_ref[:, sl], cosf, sinf) * (128 ** -0.5)
            qb, kb, vb = _bf(qr), _bf(kr), _bf(v_ref[:, sl])
            s0b = _bf(st_ref[0, h])
            md, qd, kd = md_ref[h], qd_ref[h], kd_ref[h]
            scb = _bf(_dg(qb, kb, 1, 1) * md)
            qdb = _bf(qr * qd)
            kdb = _bf(kr * kd)
            r = _dg(scb, vb, 1, 0) + _dg(qdb, s0b, 1, 0)
            y, rstd = _norm_stats(r)
            gng = gng_ref[:, sl]
            gn = y * gng + gnb_ref[:, sl]
            rg = g_ref[:, sl]
            sg = _sigmoid(rg)
            d_o = do_ref[:, sl]
            d_gn = d_o * (rg * sg)
            dg_ref[:, sl] += jnp.sum(d_gn * y, axis=0, keepdims=True)
            db_ref[:, sl] += jnp.sum(d_gn, axis=0, keepdims=True)
            drb = _bf(_norm_bwd(d_gn * gng, y, rstd))
            g0 = gstate[h]
            gb = _bf(g0)
            dscb = _bf(_dg(drb, vb, 1, 1) * md)
            dqr = _dg(dscb, kb, 1, 0) + _dg(drb, s0b, 1, 1) * qd
            dkr = _dg(dscb, qb, 0, 0) + _dg(vb, gb, 1, 1) * kd
            dv = _dg(scb, drb, 0, 0) + _dg(kdb, gb, 1, 0)
            gstate[h] = g0 * cd_ref[h, 0:1, :] + _dg(qdb, drb, 0, 0)
            dp_ref[:, 0 * 512 + h * 128:0 * 512 + (h + 1) * 128] = _bf(_rot_t(dqr, cosf, sinf))
            dp_ref[:, 1 * 512 + h * 128:1 * 512 + (h + 1) * 128] = _bf(_rot_t(dkr, cosf, sinf) * (128 ** -0.5))
            dp_ref[:, 2 * 512 + h * 128:2 * 512 + (h + 1) * 128] = _bf(dv)
            dp_ref[:, 3 * 512 + h * 128:3 * 512 + (h + 1) * 128] = _bf(d_o * gn * (sg * (1.0 + rg * (1.0 - sg))))

    acc = pl.BlockSpec((1, 512), lambda n: (0, 0))
    return pl.pallas_call(
        kernel_body, name="ret_bwd", grid=(N,),
        out_shape=(jax.ShapeDtypeStruct((S, 2048), BF16), jax.ShapeDtypeStruct((1, 512), F32),
                   jax.ShapeDtypeStruct((1, 512), F32)),
        in_specs=[col(0), col(1), col(2), col(3), tab, tab, dec, dec, dec, cdec, vec, vec, st,
                  pl.BlockSpec((CHUNK, 512), lambda n: (N - 1 - n, 0))],
        out_specs=(pl.BlockSpec((CHUNK, 2048), lambda n: (N - 1 - n, 0)), acc, acc),
        scratch_shapes=[pltpu.VMEM((4, CHUNK, CHUNK), F32)],
        compiler_params=_cparams(("arbitrary",)),
    )(proj, proj, proj, proj, *tables, gn_g, gn_b, states, d_out)


SB_T = 256
SB_SCALE = 64 ** -0.5
SB_Q_COL, SB_K_COL, SB_V_COL = 2048 // 128, 2560 // 128, 3072 // 128


def _head_masks():
    lane = lax.broadcasted_iota(jnp.int32, (1, 128), 1)
    m0 = (lane < 64).astype(F32)
    return m0, 1.0 - m0


def _tri(n, cmp):
    r = lax.broadcasted_iota(jnp.int32, (n, n), 0)
    c = lax.broadcasted_iota(jnp.int32, (n, n), 1)
    return cmp(r, c)


def _tri_sum(x, tri):
    hi = _bf(x)
    lo = _bf(x - hi.astype(F32))
    return _dg(hi, tri, 1, 0) + _dg(lo, tri, 1, 0)


def _sb_weights(qms, kblks, upper, carry, causal):
    tiles = [(b, h) for b in range(len(kblks)) for h in range(2)]
    zs = [_dg(qms[h], kblks[b], 1, 1) for b, h in tiles]
    lgs = [-(jnp.maximum(z, 0.0) + jnp.log(1.0 + jnp.exp(-jnp.abs(z)))) for z in zs]
    if causal is not None:
        lgs = [jnp.where(causal, lg, 0.0) for lg in lgs]
    carries = list(carry)
    for t in range(len(tiles) - 2):
        carries.append(carries[t] + jnp.sum(lgs[t], axis=1, keepdims=True))
    his = [_bf(lg) for lg in lgs]
    los = [_bf(lg - hi.astype(F32)) for lg, hi in zip(lgs, his)]
    later = [_dg(hi, upper, 1, 0) for hi in his]
    later = [r + _dg(lo, upper, 1, 0) for r, lo in zip(later, los)]
    a = [jnp.exp(lg + z + (r + c)) for lg, z, r, c in zip(lgs, zs, later, carries)]
    if causal is not None:
        a = [jnp.where(causal, x, 0.0) for x in a]
    out = tuple(carries[t] + jnp.sum(lgs[t], axis=1, keepdims=True) for t in (len(tiles) - 2, len(tiles) - 1))
    return [a[2 * b:2 * b + 2] for b in range(len(kblks))], out


def _sb_fwd(proj, comm=None):
    S = proj.shape[0]
    T = min(SB_T, S)
    nq = S // T

    def body(q_ref, k_ref, v_ref, o_ref, kb_ref, vm_ref, acc_ref):
        i = pl.program_id(1)
        m0, m1 = _head_masks()

        @pl.when(i == 0)
        def _():
            v = v_ref[...]
            kb_ref[...] = _bf(k_ref[...])
            vm_ref[0] = _bf(v * m0)
            vm_ref[1] = _bf(v * m1)

        q = q_ref[...]
        qm = (_bf(q * (m0 * SB_SCALE)), _bf(q * (m1 * SB_SCALE)))
        upper = _tri(T, lambda r, c: r > c).astype(BF16)
        causal = _tri(T, lambda r, c: c < r)

        def tiles(js, carry, mask, first):
            ks = [pl.multiple_of(j * T, T) for j in js]
            a, out = _sb_weights(qm, [kb_ref[pl.ds(k, T), :] for k in ks], upper, carry, mask)
            parts = [_dg(_bf(a[b][h]), vm_ref[h, pl.ds(k, T), :], 1, 0) for b, k in enumerate(ks) for h in range(2)]
            part = functools.reduce(lambda u, w: u + w, parts)
            if first:
                acc_ref[...] = part
            else:
                acc_ref[...] += part
            return out

        zero = jnp.zeros((T, 1), F32)
        carry = tiles([i], (zero, zero), causal, True)
        carry = lax.fori_loop(0, i % 2, lambda _, c: tiles([i - 1], c, None, False), carry)
        top = i - 1 - i % 2
        lax.fori_loop(0, i // 2, lambda jj, c: tiles([top - 2 * jj, top - 2 * jj - 1], c, None, False), carry)
        o_ref[...] = acc_ref[...]

    full = lambda c: pl.BlockSpec((S, 128), lambda p, i, c=c: (0, c + p))
    outs, landed = _pcall(
        body, name="sb_fwd", grid=(4, nq), out_shape=[jax.ShapeDtypeStruct((S, SB_W), F32)],
        in_specs=[pl.BlockSpec((T, 128), lambda p, i: (i, SB_Q_COL + p)), full(SB_K_COL), full(SB_V_COL)],
        out_specs=[pl.BlockSpec((T, 128), lambda p, i: (i, p))],
        scratch_shapes=[pltpu.VMEM((S, 128), BF16), pltpu.VMEM((2, S, 128), BF16), pltpu.VMEM((T, 128), F32)],
        sem=("arbitrary", "arbitrary"), args=(proj, proj, proj), comm=comm)
    return outs[0] if comm is None else (outs[0], landed)


def _sb_bwd(proj, d_o, comm=None):
    S = proj.shape[0]
    T = min(SB_T, S)
    nq = S // T

    def body(q_ref, k_ref, v_ref, do_ref, dq_ref, dk_ref, dv_ref, kb_ref, kbm_ref, vb_ref, e_ref, dq_acc, dk_acc, dv_acc):
        i = pl.program_id(1)
        m0, m1 = _head_masks()

        @pl.when(i == 0)
        def _():
            k = k_ref[...]
            kb_ref[...] = _bf(k)
            kbm_ref[0] = _bf(k * m0)
            kbm_ref[1] = _bf(k * m1)
            vb_ref[...] = _bf(v_ref[...])
            dk_acc[...] = jnp.zeros_like(dk_acc)
            dv_acc[...] = jnp.zeros_like(dv_acc)

        q, d_out = q_ref[...], do_ref[...]
        qm = (_bf(q * (m0 * SB_SCALE)), _bf(q * (m1 * SB_SCALE)))
        dom = (_bf(d_out * m0), _bf(d_out * m1))
        upper = _tri(T, lambda r, c: r > c).astype(BF16)
        lower = _tri(T, lambda r, c: r < c).astype(BF16)
        causal = _tri(T, lambda r, c: c < r)

        def down(js, carry, mask):
            ks = [pl.multiple_of(j * T, T) for j in js]
            a, out = _sb_weights(qm, [kb_ref[pl.ds(k, T), :] for k in ks], upper, carry, mask)
            da = [[_dg(dom[h], vb_ref[pl.ds(k, T), :], 1, 1) for h in range(2)] for k in ks]
            for b, (j, k) in enumerate(zip(js, ks)):
                for h in range(2):
                    e_ref[h, j] = a[b][h] * da[b][h]
                dv_acc[pl.ds(k, T), :] += _dg(_bf(a[b][0]), dom[0], 0, 0) + _dg(_bf(a[b][1]), dom[1], 0, 0)
            return out

        def up(js, carry, mask):
            ks = [pl.multiple_of(j * T, T) for j in js]
            tiles = [(b, h) for b in range(len(js)) for h in range(2)]
            zs = [_dg(qm[h], kb_ref[pl.ds(ks[b], T), :], 1, 1) for b, h in tiles]
            es = [e_ref[h, js[b]] for b, h in tiles]
            carries = list(carry)
            for t in range(len(tiles)):
                carries.append(carries[t] + jnp.sum(es[t], axis=1, keepdims=True))
            his = [_bf(e) for e in es]
            los = [_bf(e - hi.astype(F32)) for e, hi in zip(es, his)]
            d_lg = [_dg(hi, lower, 1, 0) for hi in his]
            d_lg = [r + _dg(lo, lower, 1, 0) + c for r, lo, c in zip(d_lg, los, carries)]
            ens = [jnp.exp(-jnp.abs(z)) for z in zs]
            invs = [1.0 / (1.0 + en) for en in ens]
            betas = [jnp.where(z >= 0.0, inv, en * inv) for z, en, inv in zip(zs, ens, invs)]
            dzs = [e * (1.0 - b) - d * b for e, b, d in zip(es, betas, d_lg)]
            if mask is not None:
                dzs = [jnp.where(mask, dz, 0.0) for dz in dzs]
            dzs = [_bf(dz) for dz in dzs]
            parts = [_dg(dzs[t], kbm_ref[h, pl.ds(ks[b], T), :], 1, 0) for t, (b, h) in enumerate(tiles)]
            dq_acc[...] += functools.reduce(lambda u, w: u + w, parts)
            for b, k in enumerate(ks):
                dk_acc[pl.ds(k, T), :] += _dg(dzs[2 * b], qm[0], 0, 0) + _dg(dzs[2 * b + 1], qm[1], 0, 0)
            return tuple(carries[-2:])

        zero = jnp.zeros((T, 1), F32)
        carry = down([i], (zero, zero), causal)
        carry = lax.fori_loop(0, i % 2, lambda _, c: down([i - 1], c, None), carry)
        top = i - 1 - i % 2
        lax.fori_loop(0, i // 2, lambda jj, c: down([top - 2 * jj, top - 2 * jj - 1], c, None), carry)

        dq_acc[...] = jnp.zeros_like(dq_acc)
        carry = lax.fori_loop(0, i // 2, lambda jj, c: up([2 * jj, 2 * jj + 1], c, None), (zero, zero))
        carry = lax.fori_loop(0, i % 2, lambda _, c: up([i - 1], c, None), carry)
        up([i], carry, causal)
        dq_ref[...] = _bf(dq_acc[...] * SB_SCALE)

        @pl.when(i == nq - 1)
        def _():
            dk_ref[...] = _bf(dk_acc[...])
            dv_ref[...] = _bf(dv_acc[...])

    full = lambda c: pl.BlockSpec((S, 128), lambda p, i, c=c: (0, c + p))
    tile = pl.BlockSpec((T, 128), lambda p, i: (i, p))
    acc = pl.BlockSpec((S, 128), lambda p, i: (0, p))
    out = jax.ShapeDtypeStruct((S, SB_W), BF16)
    outs, landed = _pcall(
        body, name="sb_bwd", grid=(4, nq), out_shape=[out, out, out],
        in_specs=[pl.BlockSpec((T, 128), lambda p, i: (i, SB_Q_COL + p)), full(SB_K_COL), full(SB_V_COL), tile],
        out_specs=[tile, acc, acc],
        scratch_shapes=[pltpu.VMEM((S, 128), BF16), pltpu.VMEM((2, S, 128), BF16), pltpu.VMEM((S, 128), BF16),
                        pltpu.VMEM((2, nq, T, T), F32), pltpu.VMEM((T, 128), F32), pltpu.VMEM((S, 128), F32),
                        pltpu.VMEM((S, 128), F32)],
        sem=("arbitrary", "arbitrary"), args=(proj, proj, proj, d_o), comm=comm)
    return tuple(outs) if comm is None else (tuple(outs), landed)


SGU_U_COL, SGU_V_COL = 3584 // 512, 4096 // 512


def _causal(w):
    r = lax.broadcasted_iota(jnp.int32, (CHUNK, CHUNK), 0)
    c = lax.broadcasted_iota(jnp.int32, (CHUNK, CHUNK), 1)
    return jnp.where(r >= c, w, 0.0)


def _sgu_fwd(proj, ln_g, ln_b, w, b):
    S = proj.shape[0]
    N = S // CHUNK

    def body(u_ref, v_ref, g_ref, b_ref, w_ref, bias_ref, out_ref):
        u = _gelu(u_ref[...])
        xh, _ = _norm_stats(_gelu(v_ref[...]))
        vn = _bf(xh * g_ref[...] + b_ref[...])
        for g in range(4):
            sl = slice(g * 128, (g + 1) * 128)
            sv = _dg(_bf(_causal(w_ref[g])), vn[:, sl], 1, 0) + bias_ref[g]
            out_ref[:, sl] = u[:, sl] * sv

    vec = pl.BlockSpec((1, 512), lambda n: (0, 0))
    return pl.pallas_call(
        body, name="sgu_fwd", grid=(N,),
        out_shape=jax.ShapeDtypeStruct((S, SGU_W), F32),
        in_specs=[pl.BlockSpec((CHUNK, 512), lambda n: (n, SGU_U_COL)),
                  pl.BlockSpec((CHUNK, 512), lambda n: (n, SGU_V_COL)), vec, vec,
                  pl.BlockSpec((4, CHUNK, CHUNK), lambda n: (0, 0, 0)), pl.BlockSpec((4, CHUNK, 1), lambda n: (0, 0, 0))],
        out_specs=pl.BlockSpec((CHUNK, 512), lambda n: (n, 0)),
        compiler_params=_cparams(("parallel",)),
    )(proj, proj, ln_g, ln_b, w, b)


def _sgu_bwd(proj, ln_g, ln_b, w, b, d_out):
    S = proj.shape[0]
    N = S // CHUNK

    def body(u_ref, v_ref, g_ref, b_ref, w_ref, bias_ref, do_ref, dp_ref, dg_ref, db_ref, dw_ref, dbias_ref):
        @pl.when(pl.program_id(0) == 0)
        def _():
            dg_ref[...] = jnp.zeros_like(dg_ref)
            db_ref[...] = jnp.zeros_like(db_ref)
            dw_ref[...] = jnp.zeros_like(dw_ref)
            dbias_ref[...] = jnp.zeros_like(dbias_ref)

        gu, gv = u_ref[...], v_ref[...]
        u = _gelu(gu)
        xh, rstd = _norm_stats(_gelu(gv))
        ln_gain = g_ref[...]
        vn = _bf(xh * ln_gain + b_ref[...])
        d_o = do_ref[...]
        d_vn = []
        for g in range(4):
            sl = slice(g * 128, (g + 1) * 128)
            wc = _bf(_causal(w_ref[g]))
            sv = _dg(wc, vn[:, sl], 1, 0) + bias_ref[g]
            dp_ref[:, sl] = _bf(d_o[:, sl] * sv * _gelu_grad(gu[:, sl]))
            d_sv = d_o[:, sl] * u[:, sl]
            dbias_ref[g] += jnp.sum(d_sv, axis=1, keepdims=True)
            d_svb = _bf(d_sv)
            dw_ref[g] += _causal(_dg(d_svb, vn[:, sl], 1, 1))
            d_vn.append(_dg(wc, d_svb, 0, 0))
        d_vn = jnp.concatenate(d_vn, axis=1)
        dg_ref[...] += jnp.sum(d_vn * xh, axis=0, keepdims=True)
        db_ref[...] += jnp.sum(d_vn, axis=0, keepdims=True)
        dp_ref[:, 512:1024] = _bf(_norm_bwd(d_vn * ln_gain, xh, rstd) * _gelu_grad(gv))

    vec = pl.BlockSpec((1, 512), lambda n: (0, 0))
    wspec = pl.BlockSpec((4, CHUNK, CHUNK), lambda n: (0, 0, 0))
    bspec = pl.BlockSpec((4, CHUNK, 1), lambda n: (0, 0, 0))
    return pl.pallas_call(
        body, name="sgu_bwd", grid=(N,),
        out_shape=(jax.ShapeDtypeStruct((S, 1024), BF16), jax.ShapeDtypeStruct((1, 512), F32),
                   jax.ShapeDtypeStruct((1, 512), F32), jax.ShapeDtypeStruct((4, CHUNK, CHUNK), F32),
                   jax.ShapeDtypeStruct((4, CHUNK, 1), F32)),
        in_specs=[pl.BlockSpec((CHUNK, 512), lambda n: (n, SGU_U_COL)),
                  pl.BlockSpec((CHUNK, 512), lambda n: (n, SGU_V_COL)), vec, vec, wspec, bspec,
                  pl.BlockSpec((CHUNK, 512), lambda n: (n, 0))],
        out_specs=(pl.BlockSpec((CHUNK, 1024), lambda n: (n, 0)), vec, vec, wspec, bspec),
        compiler_params=_cparams(("arbitrary",)),
    )(proj, proj, ln_g, ln_b, w, b, d_out)


GATE_COL = 4608 // 512


def _merge_fwd(proj, branches, p_list, tm=512):
    S = proj.shape[0]
    tm = min(tm, S)

    def body(r_ref, s_ref, g_ref, pr_ref, ps_ref, pg_ref, gr_ref, gs_ref, gg_ref, m_ref, br_ref):
        acc = None
        for k, (x_ref, p_ref, gate_ref) in enumerate(((r_ref, pr_ref, gr_ref), (s_ref, ps_ref, gs_ref),
                                                      (g_ref, pg_ref, gg_ref))):
            br = _dg(_bf(x_ref[...]), _bf(p_ref[...]), 1, 0)
            br_ref[k] = br
            term = _sigmoid(gate_ref[...]) * br
            acc = term if acc is None else acc + term
        m_ref[...] = _bf(acc)

    xs = pl.BlockSpec((tm, 512), lambda i, n: (i, 0))
    ps = pl.BlockSpec((512, 512), lambda i, n: (0, n))
    gate = lambda k: pl.BlockSpec((tm, 512), lambda i, n, k=k: (i, GATE_COL + 2 * k + n))
    return pl.pallas_call(
        body, name="merge_fwd", grid=(S // tm, 2),
        out_shape=(jax.ShapeDtypeStruct((S, D_MODEL), BF16), jax.ShapeDtypeStruct((3, S, D_MODEL), F32)),
        in_specs=[xs, xs, xs, ps, ps, ps, gate(0), gate(1), gate(2)],
        out_specs=(pl.BlockSpec((tm, 512), lambda i, n: (i, n)), pl.BlockSpec((3, tm, 512), lambda i, n: (0, i, n))),
        compiler_params=_cparams(("parallel", "parallel")),
    )(*branches, *p_list, proj, proj, proj)


def _gate_bwd(proj, br, d_merged, tm=512):
    S = proj.shape[0]
    tm = min(tm, S)

    def body(dm_ref, br_ref, gr_ref, gs_ref, gg_ref, dbr_ref, dgate_ref):
        dm = dm_ref[...]
        for k, gate_ref in enumerate((gr_ref, gs_ref, gg_ref)):
            s = _sigmoid(gate_ref[...])
            dbr_ref[k] = _bf(dm * s)
            dgate_ref[k] = _bf(dm * br_ref[k] * (s * (1.0 - s)))

    gate = lambda k: pl.BlockSpec((tm, 512), lambda i, n, k=k: (i, GATE_COL + 2 * k + n))
    three = pl.BlockSpec((3, tm, 512), lambda i, n: (0, i, n))
    return pl.pallas_call(
        body, name="gate_bwd", grid=(S // tm, 2),
        out_shape=(jax.ShapeDtypeStruct((3, S, D_MODEL), BF16), jax.ShapeDtypeStruct((3, S, D_MODEL), BF16)),
        in_specs=[pl.BlockSpec((tm, 512), lambda i, n: (i, n)), three, gate(0), gate(1), gate(2)],
        out_specs=(three, three),
        compiler_params=_cparams(("parallel", "parallel")),
    )(d_merged, br, proj, proj, proj)


def _ln_bwd(dy, u, g, tm=256):
    S, D = u.shape
    tm = min(tm, S)

    def body(dy_ref, u_ref, g_ref, du_ref, dg_ref, db_ref):
        @pl.when(pl.program_id(0) == 0)
        def _():
            dg_ref[...] = jnp.zeros_like(dg_ref)
            db_ref[...] = jnp.zeros_like(db_ref)

        dy_t = dy_ref[...]
        xh, rstd = _norm_stats(u_ref[...])
        dg_ref[...] += jnp.sum(dy_t * xh, axis=0, keepdims=True)
        db_ref[...] += jnp.sum(dy_t, axis=0, keepdims=True)
        du_ref[...] = _norm_bwd(dy_t * g_ref[...], xh, rstd)

    tile = pl.BlockSpec((tm, D), lambda i: (i, 0))
    vec = pl.BlockSpec((1, D), lambda i: (0, 0))
    return pl.pallas_call(
        body, name="ln_bwd", grid=(S // tm,),
        out_shape=(jax.ShapeDtypeStruct((S, D), F32), jax.ShapeDtypeStruct((1, D), F32),
                   jax.ShapeDtypeStruct((1, D), F32)),
        in_specs=[tile, tile, vec], out_specs=(tile, vec, vec),
        compiler_params=_cparams(("arbitrary",)),
    )(dy, u, g)


def _loss_grad(y, target, tm=256):
    S, D = y.shape
    tm = min(tm, S)

    def body(y_ref, t_ref, dy_ref, sq_ref):
        @pl.when(pl.program_id(0) == 0)
        def _():
            sq_ref[...] = jnp.zeros_like(sq_ref)

        err = y_ref[...] - t_ref[...]
        dy_ref[...] = err * (1.0 / D)
        sq_ref[...] += jnp.sum(err * err, axis=0, keepdims=True)

    tile = pl.BlockSpec((tm, D), lambda i: (i, 0))
    vec = pl.BlockSpec((1, D), lambda i: (0, 0))
    return pl.pallas_call(
        body, name="loss_grad", grid=(S // tm,),
        out_shape=(jax.ShapeDtypeStruct((S, D), F32), jax.ShapeDtypeStruct((1, D), F32)),
        in_specs=[tile, tile], out_specs=(tile, vec),
        compiler_params=_cparams(("arbitrary",)),
    )(y, target)


def _layer_fwd(x, W, tables, sb_comm=None):
    proj = _matmul(x, W["w_in"], "nn", name="proj")
    retg, states = _ret_fwd(proj, tables, W["ret_gn_g"], W["ret_gn_b"])
    if sb_comm is None:
        sb = _sb_fwd(proj)
    else:
        sb, landed = _sb_fwd(proj, comm=sb_comm[0])
        sb_comm[1](landed)
    sg = _sgu_fwd(proj, W["sgu_ln_g"], W["sgu_ln_b"], W["sgu_w"], W["sgu_b"])
    merged, br = _merge_fwd(proj, (retg, sb, sg), (W["p_ret"], W["p_sb"], W["p_sgu"]))
    u1, x1 = _matmul(merged, W["w_out"], "nn", name="out_ln", epi="ln", extra=(x, W["ln1_g"], W["ln1_b"]))
    hpre, act = _matmul(x1, W["w_up"], "nn", name="up", epi="relu2")
    u2, x2 = _matmul(act, W["w_down"], "nn", name="down_ln", epi="ln", extra=(x1, W["ln2_g"], W["ln2_b"]))
    saved = dict(x=x, proj=proj, retg=retg, states=states, sb=sb, sg=sg, merged=merged, br=br, u1=u1, x1=x1,
                 hpre=hpre, act=act, u2=u2)
    return x2, saved


def _layer_bwd(d_x2, W, tables, sv, chunk_dtype=None, sb_comm_fn=None, dx_comm_fn=None):
    dt = F32 if chunk_dtype is None else chunk_dtype
    rows, cols = (None, None) if chunk_dtype is None else ("rows", "cols")
    g, landed = {}, {}
    du2, g["ln2_g"], g["ln2_b"] = _ln_bwd(d_x2, sv["u2"], W["ln2_g"])
    d_hpre = _matmul(du2, W["w_down"], "nt", name="d_act", epi="drelu2", extra=(sv["hpre"],), out_dtype=BF16)
    g["w_down"] = _matmul(sv["act"], du2, "tn", name="dw_down", out_dtype=dt, chunks=rows)
    g["w_up"] = _matmul(sv["x1"], d_hpre, "tn", name="dw_up", out_dtype=dt, chunks=cols)
    d_x1 = _matmul(d_hpre, W["w_up"], "nt", name="d_x1", epi="add", extra=(du2,))
    du1, g["ln1_g"], g["ln1_b"] = _ln_bwd(d_x1, sv["u1"], W["ln1_g"])
    d_merged = _matmul(du1, W["w_out"], "nt", name="d_merged")
    g["w_out"] = _matmul(sv["merged"], du1, "tn", name="dw_out", out_dtype=dt, chunks=rows)
    d_br, d_gate = _gate_bwd(sv["proj"], sv["br"], d_merged)
    d_branch = []
    for k, (nm, act) in enumerate((("p_ret", sv["retg"]), ("p_sb", sv["sb"]), ("p_sgu", sv["sg"]))):
        d_branch.append(_matmul(d_br[k], W[nm], "nt", name="d_" + nm[2:]))
        g[nm] = _matmul(act, d_br[k], "tn", name="dw_" + nm[2:], out_dtype=dt, chunks=cols)
    d_ret, g["ret_gn_g"], g["ret_gn_b"] = _ret_bwd(sv["proj"], tables, W["ret_gn_g"], W["ret_gn_b"], sv["states"],
                                                   d_branch[0])
    if sb_comm_fn is None:
        d_sq, d_sk, d_sv = _sb_bwd(sv["proj"], d_branch[1])
    else:
        (d_sq, d_sk, d_sv), landed["sb"] = _sb_bwd(sv["proj"], d_branch[1], comm=sb_comm_fn(g))
    d_sgu, g["sgu_ln_g"], g["sgu_ln_b"], g["sgu_w"], g["sgu_b"] = _sgu_bwd(
        sv["proj"], W["sgu_ln_g"], W["sgu_ln_b"], W["sgu_w"], W["sgu_b"], d_branch[2])
    d_proj = jnp.concatenate([d_ret, d_sq, d_sk, d_sv, d_sgu, d_gate[0], d_gate[1], d_gate[2]], axis=1)
    g["w_in"] = _matmul(sv["x"], d_proj, "tn", name="dw_in", out_dtype=dt)
    if chunk_dtype is not None:
        g["w_in"] = g["w_in"].reshape(D_MODEL, N_DEV, N_IN // N_DEV).transpose(1, 0, 2)
    if dx_comm_fn is None:
        d_x = _matmul(d_proj, W["w_in"], "nt", name="d_x", epi="add", extra=(du1,))
    else:
        d_x, landed["dx"] = _matmul(d_proj, W["w_in"], "nt", name="d_x", epi="add", extra=(du1,), comm=dx_comm_fn(g))
    return d_x, g, landed


BIG = ("w_in", "p_ret", "p_sb", "p_sgu", "w_out", "w_up", "w_down")
SMALL = ("ret_gn_g", "ret_gn_b", "sgu_ln_g", "sgu_ln_b", "sgu_w", "sgu_b", "ln1_g", "ln1_b", "ln2_g", "ln2_b")
GATHER_KIND = {"w_in": "slab", "p_ret": "cols", "p_sb": "cols", "p_sgu": "cols", "w_out": "rows", "w_up": "cols",
               "w_down": "rows"}


def _small_weights(small, l):
    W = {}
    for n in SMALL:
        if n == "sgu_w":
            W[n] = small[n][l]
        elif n == "sgu_b":
            W[n] = small[n][l].reshape(4, CHUNK, 1)
        else:
            W[n] = small[n][l].reshape(1, -1)
    return W


def _local_step(x, target, full, small):
    tables = _ret_tables(x.shape[0])
    Ws = [{**{n: full[n][l] for n in BIG}, **_small_weights(small, l)} for l in range(DEPTH)]
    saved = []
    h = x
    for l in range(DEPTH):
        h, sv = _layer_fwd(h, Ws[l], tables)
        saved.append(sv)
    d_h, sq = _loss_grad(h, target)
    grads = [None] * DEPTH
    for l in reversed(range(DEPTH)):
        d_h, grads[l], _ = _layer_bwd(d_h, Ws[l], tables, saved[l])
    return sq, d_h, grads


def _adam(w, parts, m, v, name, tr=256):
    L, R, C = w.shape
    tr = min(tr, R)
    assert R % tr == 0 and len(parts) == L

    def body(*refs):
        w_ref, p_refs, (m_ref, v_ref, g_ref, d_ref, nm_ref, nv_ref) = refs[0], refs[1:1 + L], refs[1 + L:]
        layer = pl.program_id(0)
        g = None
        for li, p_ref in enumerate(p_refs):
            s = p_ref[0].astype(F32)
            for j in range(1, N_DEV):
                s = s + p_ref[j].astype(F32)
            g = s if g is None else jnp.where(layer == li, s, g)
        m2 = ADAM_B1 * m_ref[...] + (1.0 - ADAM_B1) * g
        v2 = ADAM_B2 * v_ref[...] + (1.0 - ADAM_B2) * (g * g)
        m_hat = m2 / (1.0 - ADAM_B1 ** ADAM_STEP)
        v_hat = v2 / (1.0 - ADAM_B2 ** ADAM_STEP)
        g_ref[...] = g
        d_ref[...] = -ADAM_LR * (m_hat / (jnp.sqrt(v_hat) + ADAM_EPS) + ADAM_WD * w_ref[...])
        nm_ref[...] = m2
        nv_ref[...] = v2

    tile = pl.BlockSpec((None, tr, C), lambda l, i: (l, i, 0))
    part = lambda li: pl.BlockSpec((N_DEV, tr, C), lambda l, i, li=li: (0, jnp.where(l == li, i, 0), 0))
    out = jax.ShapeDtypeStruct((L, R, C), F32)
    return pl.pallas_call(
        body, name=name, grid=(L, R // tr), out_shape=(out, out, out, out),
        in_specs=[tile] + [part(li) for li in range(L)] + [tile, tile],
        out_specs=(tile, tile, tile, tile),
        compiler_params=_cparams(("parallel", "parallel")),
    )(w, *parts, m, v)


def _pack_small(tree):
    return jnp.concatenate([tree[n].reshape(-1, 128) for n in SMALL], axis=0)


def _unpack_small(packed, like):
    out, r = {}, 0
    for n in SMALL:
        rows = like[n].size // 128
        out[n] = packed[r:r + rows].reshape(like[n].shape)
        r += rows
    return out


WEIGHTS = ("w_in", "ret_gn_g", "ret_gn_b", "sgu_ln_g", "sgu_ln_b", "sgu_w", "sgu_b", "p_ret", "p_sb", "p_sgu", "w_out",
           "ln1_g", "ln1_b", "w_up", "w_down", "ln2_g", "ln2_b")


def kernel(x, w_in, ret_gn_g, ret_gn_b, sgu_ln_g, sgu_ln_b, sgu_w, sgu_b, p_ret, p_sb, p_sgu, w_out, ln1_g, ln1_b, w_up, w_down, ln2_g, ln2_b, loss_target, m_w_in, m_ret_gn_g, m_ret_gn_b, m_sgu_ln_g, m_sgu_ln_b, m_sgu_w, m_sgu_b, m_p_ret, m_p_sb, m_p_sgu, m_w_out, m_ln1_g, m_ln1_b, m_w_up, m_w_down, m_ln2_g, m_ln2_b, v_w_in, v_ret_gn_g, v_ret_gn_b, v_sgu_ln_g, v_sgu_ln_b, v_sgu_w, v_sgu_b, v_p_ret, v_p_sb, v_p_sgu, v_w_out, v_ln1_g, v_ln1_b, v_w_up, v_w_down, v_ln2_g, v_ln2_b):
    w = dict(zip(WEIGHTS, (w_in, ret_gn_g, ret_gn_b, sgu_ln_g, sgu_ln_b, sgu_w, sgu_b, p_ret, p_sb, p_sgu, w_out,
                           ln1_g, ln1_b, w_up, w_down, ln2_g, ln2_b)))
    m = dict(zip(WEIGHTS, (m_w_in, m_ret_gn_g, m_ret_gn_b, m_sgu_ln_g, m_sgu_ln_b, m_sgu_w, m_sgu_b, m_p_ret, m_p_sb,
                           m_p_sgu, m_w_out, m_ln1_g, m_ln1_b, m_w_up, m_w_down, m_ln2_g, m_ln2_b)))
    v = dict(zip(WEIGHTS, (v_w_in, v_ret_gn_g, v_ret_gn_b, v_sgu_ln_g, v_sgu_ln_b, v_sgu_w, v_sgu_b, v_p_ret, v_p_sb,
                           v_p_sgu, v_w_out, v_ln1_g, v_ln1_b, v_w_up, v_w_down, v_ln2_g, v_ln2_b)))

    small = {n: w[n] for n in SMALL}
    shard = {n: _bf(w[n]) for n in BIG}
    tables = _ret_tables(x.shape[1])
    Ws = [_small_weights(small, l) for l in range(DEPTH)]
    unslab = lambda z: z.transpose(1, 0, 2).reshape(D_MODEL, N_IN)

    (slabs,) = _exchange([_gather_transfer(shard["w_in"], 0, "slab")], "gather_w_in0")
    Ws[0]["w_in"] = unslab(slabs)
    later = [(n, 0) for n in BIG[1:]] + [(n, 1) for n in BIG]

    def weights_landed(landed):
        for (n, l), z in zip(later, landed):
            Ws[l][n] = unslab(z) if n == "w_in" else z

    gather = _Comm([_gather_transfer(shard[n], l, GATHER_KIND[n]) for n, l in later])
    h, saved0 = _layer_fwd(x[0], Ws[0], tables, sb_comm=(gather, weights_landed))
    h, saved1 = _layer_fwd(h, Ws[1], tables)
    d_h, sq = _loss_grad(h, loss_target[0])
    loss = lax.psum(0.5 * jnp.sum(sq) / D_MODEL, ("x", "y", "c"))

    d_h, g1, _ = _layer_bwd(d_h, Ws[1], tables, saved1, chunk_dtype=BF16)
    early = [(n, 1) for n in BIG] + [(n, 0) for n in BIG[1:]]

    def early_scatter(g0):
        return _Comm([_scatter_transfer((g1 if l else g0)[n]) for n, l in early])

    def late_scatter(g0):
        mine = _pack_small({n: jnp.stack([g0[n].reshape(small[n].shape[1:]), g1[n].reshape(small[n].shape[1:])])
                            for n in SMALL})
        return _Comm([_scatter_transfer(g0["w_in"]), _slab_transfer(mine)])

    d_x, g0, landed = _layer_bwd(d_h, Ws[0], tables, saved0, chunk_dtype=BF16, sb_comm_fn=early_scatter,
                                 dx_comm_fn=late_scatter)
    parts = dict(zip(early, landed["sb"]))
    parts[("w_in", 0)], small_parts = landed["dx"]

    grad, delta, new_m, new_v = {}, {}, {}, {}
    for n in BIG:
        grad[n], delta[n], new_m[n], new_v[n] = _adam(w[n], [parts[(n, l)] for l in range(DEPTH)], m[n], v[n],
                                                      "adam_" + n)
    res = _adam(_pack_small(small)[None], [small_parts], _pack_small({n: m[n] for n in SMALL})[None],
                _pack_small({n: v[n] for n in SMALL})[None], "adam_small", tr=small_parts.shape[1])
    for tree, packed in zip((grad, delta, new_m, new_v), res):
        tree.update(_unpack_small(packed[0], small))

    return (loss, d_x[None], *[grad[n] for n in WEIGHTS], *[delta[n] for n in WEIGHTS],
            *[new_m[n] for n in WEIGHTS], *[new_v[n] for n in WEIGHTS])
```

```python
import functools
import math

import numpy as np
import jax
import jax.numpy as jnp
from jax import lax
from jax.experimental import pallas as pl
from jax.experimental.pallas import tpu as pltpu

F32 = jnp.float32
BF16 = jnp.bfloat16

N_DEV = 8
DEPTH = 2
D_MODEL = 1024
CHUNK = 128
RET_W = 512
SB_W = 512
SGU_W = 512
N_IN = 7680
LN_EPS = 1e-5
ALPHA = (2 * DEPTH) ** 0.25
ROPE_BASE = 10000.0
ADAM_LR, ADAM_B1, ADAM_B2, ADAM_EPS, ADAM_WD, ADAM_STEP = 0.001, 0.9, 0.999, 1e-08, 0.01, 10
VMEM_LIMIT = 56 * 1024 * 1024

_GELU_K = math.sqrt(2.0 / math.pi)
_GELU_C = 0.044715


def _cparams(sem=None):
    return pltpu.CompilerParams(dimension_semantics=sem, vmem_limit_bytes=VMEM_LIMIT)


def _dg(a, b, ca, cb):
    return lax.dot_general(a, b, (((ca,), (cb,)), ((), ())), preferred_element_type=F32)


def _bf(x):
    return x.astype(BF16)


def _sigmoid(x):
    return 1.0 / (1.0 + jnp.exp(-x))


def _gelu(x):
    t = jnp.tanh(_GELU_K * (x + _GELU_C * (x * x * x)))
    return x * (0.5 * (1.0 + t))


def _gelu_grad(x):
    t = jnp.tanh(_GELU_K * (x + _GELU_C * (x * x * x)))
    return 0.5 * (1.0 + t) + 0.5 * x * (1.0 - t * t) * (_GELU_K * (1.0 + 3.0 * _GELU_C * x * x))


def _norm_stats(u):
    mu = jnp.mean(u, axis=-1, keepdims=True)
    d = u - mu
    var = jnp.mean(d * d, axis=-1, keepdims=True)
    rstd = lax.rsqrt(var + LN_EPS)
    return d * rstd, rstd


def _norm_bwd(dxh, xh, rstd):
    return rstd * (dxh - jnp.mean(dxh, axis=-1, keepdims=True) - xh * jnp.mean(dxh * xh, axis=-1, keepdims=True))


class _Transfer:
    def __init__(self, src, dst_shape, src_at, dst_at):
        self.src, self.dst_shape, self.src_at, self.dst_at = src, tuple(dst_shape), src_at, dst_at


def _gather_transfer(shard, l, kind):
    _, r, c = shard.shape
    src_at = lambda ref, p: ref.at[l]
    if kind == "slab":
        return _Transfer(shard, (N_DEV, r, c), src_at, lambda ref, s: ref.at[s])
    if kind == "rows":
        return _Transfer(shard, (N_DEV * r, c), src_at, lambda ref, s: ref.at[pl.ds(pl.multiple_of(s * r, r), r), :])
    return _Transfer(shard, (r, N_DEV * c), src_at, lambda ref, s: ref.at[:, pl.ds(pl.multiple_of(s * c, c), c)])


def _scatter_transfer(chunks):
    return _Transfer(chunks, chunks.shape, lambda ref, p: ref.at[p], lambda ref, s: ref.at[s])


def _slab_transfer(arr):
    return _Transfer(arr, (N_DEV,) + arr.shape, lambda ref, p: ref, lambda ref, s: ref.at[s])


class _Comm:
    def __init__(self, transfers):
        self.transfers = list(transfers)
        self.n = len(self.transfers)
        self.arrays = [t.src for t in self.transfers]
        self.out_shape = [jax.ShapeDtypeStruct(t.dst_shape, t.src.dtype) for t in self.transfers]
        self.scratch = [pltpu.SemaphoreType.DMA((self.n * (N_DEV - 1),)), pltpu.SemaphoreType.DMA((self.n * (N_DEV - 1),)),
                        pltpu.SemaphoreType.DMA((self.n,))]

    def _copies(self, srcs, dsts, send_sems, recv_sems, local_sems):
        x, y, c = lax.axis_index("x"), lax.axis_index("y"), lax.axis_index("c")
        me = 4 * x + 2 * y + c
        copies = []
        for d in range(1, N_DEV):
            px = 1 - x if d & 4 else x
            py = 1 - y if d & 2 else y
            pc = 1 - c if d & 1 else c
            peer = 4 * px + 2 * py + pc
            for t, tr in enumerate(self.transfers):
                k = t * (N_DEV - 1) + d - 1
                copies.append(pltpu.make_async_remote_copy(
                    src_ref=tr.src_at(srcs[t], peer), dst_ref=tr.dst_at(dsts[t], me),
                    send_sem=send_sems.at[k], recv_sem=recv_sems.at[k],
                    device_id=(px, py, pc), device_id_type=pl.DeviceIdType.MESH))
        own = [pltpu.make_async_copy(tr.src_at(srcs[t], me), tr.dst_at(dsts[t], me), local_sems.at[t])
               for t, tr in enumerate(self.transfers)]
        return copies, own

    def start(self, srcs, dsts, *sems):
        copies, own = self._copies(srcs, dsts, *sems)
        for cp in own + copies:
            cp.start()

    def wait(self, srcs, dsts, *sems):
        copies, own = self._copies(srcs, dsts, *sems)
        for cp in copies + own:
            cp.wait()


def _pcall(body, *, name, grid, in_specs, out_specs, out_shape, scratch_shapes, sem, args, comm=None):
    in_specs, out_specs, out_shape = list(in_specs), list(out_specs), list(out_shape)
    if comm is None:
        outs = pl.pallas_call(body, name=name, grid=grid, in_specs=in_specs, out_specs=out_specs, out_shape=out_shape,
                              scratch_shapes=list(scratch_shapes), compiler_params=_cparams(sem))(*args)
        return list(outs), []
    n_in, n_out, n_scr, k = len(in_specs), len(out_specs), len(scratch_shapes), comm.n

    def carrier(*refs):
        ins, cin = refs[:n_in], refs[n_in:n_in + k]
        outs, cout = refs[n_in + k:n_in + k + n_out], refs[n_in + k + n_out:n_in + 2 * k + n_out]
        scr, sems = refs[n_in + 2 * k + n_out:n_in + 2 * k + n_out + n_scr], refs[n_in + 2 * k + n_out + n_scr:]
        ids = [pl.program_id(d) for d in range(len(grid))]
        first = functools.reduce(jnp.logical_and, [i == 0 for i in ids])
        last = functools.reduce(jnp.logical_and, [i == g - 1 for i, g in zip(ids, grid)])

        @pl.when(first)
        def _():
            comm.start(cin, cout, *sems)

        body(*ins, *outs, *scr)

        @pl.when(last)
        def _():
            comm.wait(cin, cout, *sems)

    hbm = pl.BlockSpec(memory_space=pl.ANY)
    outs = pl.pallas_call(
        carrier, name=name, grid=grid, in_specs=in_specs + [hbm] * k, out_specs=out_specs + [hbm] * k,
        out_shape=out_shape + comm.out_shape, scratch_shapes=list(scratch_shapes) + comm.scratch,
        compiler_params=_cparams(tuple("arbitrary" for _ in grid)),
    )(*args, *comm.arrays)
    return list(outs[:n_out]), list(outs[n_out:])


def _exchange(transfers, name):
    comm = _Comm(transfers)

    def body(*refs):
        k = comm.n
        comm.start(refs[:k], refs[k:2 * k], *refs[2 * k:])
        comm.wait(refs[:k], refs[k:2 * k], *refs[2 * k:])

    hbm = pl.BlockSpec(memory_space=pl.ANY)
    return pl.pallas_call(body, name=name, out_shape=comm.out_shape, in_specs=[hbm] * comm.n, out_specs=[hbm] * comm.n,
                          scratch_shapes=comm.scratch)(*comm.arrays)


def _matmul(a, b, mode, *, name, tm, tn, tk, epi=None, extra=(), out_dtype=F32, chunks=None, comm=None):
    if mode == "nn":
        (M, K), N = a.shape, b.shape[1]
    elif mode == "nt":
        (M, K), N = a.shape, b.shape[0]
    else:
        (K, M), N = a.shape, b.shape[1]
    tm, tn, tk = min(tm, M), min(tn, N), min(tk, K)
    assert M % tm == 0 and N % tn == 0 and K % tk == 0 and (epi != "ln" or tn == N), (name, M, N, K)
    nk = K // tk
    a_spec = {"nn": pl.BlockSpec((tm, tk), lambda i, j, k: (i, k)),
              "nt": pl.BlockSpec((tm, tk), lambda i, j, k: (i, k)),
              "tn": pl.BlockSpec((tk, tm), lambda i, j, k: (k, i))}[mode]
    b_spec = {"nn": pl.BlockSpec((tk, tn), lambda i, j, k: (k, j)),
              "nt": pl.BlockSpec((tn, tk), lambda i, j, k: (j, k)),
              "tn": pl.BlockSpec((tk, tn), lambda i, j, k: (k, j))}[mode]
    ca, cb = {"nn": (1, 0), "nt": (1, 1), "tn": (0, 0)}[mode]
    tile = pl.BlockSpec((tm, tn), lambda i, j, k: (i, j))
    row = pl.BlockSpec((1, tn), lambda i, j, k: (0, j))
    n_extra = {None: 0, "add": 1, "relu2": 0, "drelu2": 1, "ln": 3}[epi]
    assert len(extra) == n_extra
    extra_specs = {None: [], "add": [tile], "relu2": [], "drelu2": [tile], "ln": [tile, row, row]}[epi]
    split = 0
    if epi == "relu2":
        out_shape, out_specs = (jax.ShapeDtypeStruct((M, N), BF16),), (tile,)
    elif epi == "ln":
        out_shape = (jax.ShapeDtypeStruct((M, N), F32), jax.ShapeDtypeStruct((M, N), F32),
                     jax.ShapeDtypeStruct((M, N), BF16))
        out_specs = (tile, tile, tile)
    elif chunks == "cols":
        c = N // N_DEV
        out_shape = (jax.ShapeDtypeStruct((N_DEV, M, c), out_dtype),)
        if tn == N:
            split = c
            out_specs = (pl.BlockSpec((N_DEV, tm, c), lambda i, j, k: (0, i, 0)),)
        else:
            assert c % tn == 0
            out_specs = (pl.BlockSpec((None, tm, tn), lambda i, j, k: (j // (c // tn), i, j % (c // tn))),)
    else:
        out_shape, out_specs = (jax.ShapeDtypeStruct((M, N), out_dtype),), (tile,)
    n_out = len(out_shape)

    def body(*refs):
        a_ref, b_ref = refs[:2]
        ex = refs[2:2 + n_extra]
        outs = refs[2 + n_extra:2 + n_extra + n_out]
        acc_ref = refs[-1]
        k = pl.program_id(2)
        part = _dg(_bf(a_ref[...]), _bf(b_ref[...]), ca, cb)

        def finish(acc):
            if epi == "add":
                outs[0][...] = (acc + ALPHA * ex[0][...]).astype(out_dtype)
            elif epi == "relu2":
                r = jnp.maximum(acc, 0.0)
                outs[0][...] = _bf(r * r)
            elif epi == "drelu2":
                outs[0][...] = (acc * (2.0 * jnp.sqrt(ex[0][...].astype(F32)))).astype(out_dtype)
            elif epi == "ln":
                u = ALPHA * ex[0][...] + acc
                xh, _ = _norm_stats(u)
                y = xh * ex[1][...] + ex[2][...]
                outs[0][...] = u
                outs[1][...] = y
                outs[2][...] = _bf(y)
            elif split:
                for p in range(N_DEV):
                    outs[0][p] = acc[:, p * split:(p + 1) * split].astype(out_dtype)
            else:
                outs[0][...] = acc.astype(out_dtype)

        if nk == 1:
            finish(part)
        else:
            @pl.when(k == 0)
            def _():
                acc_ref[...] = part

            @pl.when(jnp.logical_and(k > 0, k < nk - 1))
            def _():
                acc_ref[...] += part

            @pl.when(k == nk - 1)
            def _():
                finish(acc_ref[...] + part)

    outs, landed = _pcall(
        body, name=name, out_shape=out_shape, grid=(M // tm, N // tn, nk),
        in_specs=[a_spec, b_spec] + extra_specs, out_specs=out_specs,
        scratch_shapes=[pltpu.VMEM((tm, tn) if nk > 1 else (8, 128), F32)], sem=("parallel", "parallel", "arbitrary"),
        args=(a, b, *extra), comm=comm)
    res = outs[0] if n_out == 1 else tuple(outs)
    if chunks == "rows":
        res = res.reshape(N_DEV, M // N_DEV, N)
    return res if comm is None else (res, landed)


def _ret_tables(S):
    half = 64
    inv_freq = ROPE_BASE ** (-jnp.arange(half, dtype=F32) / half)
    ang = jnp.arange(S, dtype=jnp.int32).astype(F32)[:, None] * inv_freq[None, :]
    cos, sin = jnp.cos(ang), jnp.sin(ang)
    cosf = jnp.concatenate([cos, cos], axis=1)
    sinf = jnp.concatenate([-sin, sin], axis=1)
    log_g = jnp.log(1.0 - 2.0 ** (-5.0 - jnp.arange(4, dtype=F32)))
    idx = jnp.arange(CHUNK, dtype=F32)
    diff = idx[:, None] - idx[None, :]
    md = jnp.where(diff[None] >= 0, jnp.exp(log_g[:, None, None] * diff[None]), 0.0)
    kd = jnp.exp(log_g[:, None] * (CHUNK - 1 - idx)[None, :])
    qd = jnp.exp(log_g[:, None] * (idx + 1.0)[None, :])
    cd = jnp.exp(log_g * CHUNK)
    bc = lambda t: jnp.broadcast_to(t[:, :, None], (4, CHUNK, CHUNK))
    return cosf, sinf, md, bc(qd), bc(kd), jnp.broadcast_to(cd[:, None, None], (4, 8, CHUNK))


def _rot(x, cosf, sinf):
    return x * cosf + pltpu.roll(x, 64, 1) * sinf


def _rot_t(dx, cosf, sinf):
    return dx * cosf - pltpu.roll(dx, 64, 1) * sinf


def _ret_specs(rev, N):
    rn = (lambda n: N - 1 - n) if rev else (lambda n: n)
    col = lambda c: pl.BlockSpec((CHUNK, 512), lambda n, c=c: (rn(n), c))
    tab = pl.BlockSpec((CHUNK, CHUNK), lambda n: (rn(n), 0))
    dec = pl.BlockSpec((4, CHUNK, CHUNK), lambda n: (0, 0, 0))
    cdec = pl.BlockSpec((4, 8, CHUNK), lambda n: (0, 0, 0))
    vec = pl.BlockSpec((1, 512), lambda n: (0, 0))
    st = pl.BlockSpec((1, 4, CHUNK, CHUNK), lambda n: (rn(n), 0, 0, 0))
    return col, tab, dec, cdec, vec, st


def _ret_fwd(proj, tables, gn_g, gn_b):
    S = proj.shape[0]
    N = S // CHUNK
    col, tab, dec, cdec, vec, st = _ret_specs(False, N)

    def body(q_ref, k_ref, v_ref, g_ref, cos_ref, sin_ref, md_ref, qd_ref, kd_ref, cd_ref, gng_ref, gnb_ref,
             out_ref, st_ref, state):
        @pl.when(pl.program_id(0) == 0)
        def _():
            state[...] = jnp.zeros_like(state)

        cosf, sinf = cos_ref[...], sin_ref[...]
        for h in range(4):
            sl = slice(h * 128, (h + 1) * 128)
            qr = _rot(q_ref[:, sl], cosf, sinf)
            kr = _rot(k_ref[:, sl], cosf, sinf) * (128 ** -0.5)
            vb = _bf(v_ref[:, sl])
            s0 = state[h]
            st_ref[0, h] = s0
            sc = _dg(_bf(qr), _bf(kr), 1, 1) * md_ref[h]
            r = _dg(_bf(sc), vb, 1, 0) + _dg(_bf(qr * qd_ref[h]), _bf(s0), 1, 0)
            state[h] = s0 * cd_ref[h, 0:1, :] + _dg(_bf(kr * kd_ref[h]), vb, 0, 0)
            y, _ = _norm_stats(r)
            rg = g_ref[:, sl]
            out_ref[:, sl] = rg * _sigmoid(rg) * (y * gng_ref[:, sl] + gnb_ref[:, sl])

    return pl.pallas_call(
        body, name="ret_fwd", grid=(N,),
        out_shape=(jax.ShapeDtypeStruct((S, RET_W), F32), jax.ShapeDtypeStruct((N, 4, CHUNK, CHUNK), F32)),
        in_specs=[col(0), col(1), col(2), col(3), tab, tab, dec, dec, dec, cdec, vec, vec],
        out_specs=(pl.BlockSpec((CHUNK, 512), lambda n: (n, 0)), st),
        scratch_shapes=[pltpu.VMEM((4, CHUNK, CHUNK), F32)],
        compiler_params=_cparams(("arbitrary",)),
    )(proj, proj, proj, proj, *tables, gn_g, gn_b)


def _ret_bwd(proj, tables, gn_g, gn_b, states, d_out):
    S = proj.shape[0]
    N = S // CHUNK
    col, tab, dec, cdec, vec, st = _ret_specs(True, N)

    def kernel_body(q_ref, k_ref, v_ref, g_ref, cos_ref, sin_ref, md_ref, qd_ref, kd_ref, cd_ref, gng_ref, gnb_ref,
                    st_ref, do_ref, dp_ref, dg_ref, db_ref, gstate):
        @pl.when(pl.program_id(0) == 0)
        def _():
            gstate[...] = jnp.zeros_like(gstate)
            dg_ref[...] = jnp.zeros_like(dg_ref)
            db_ref[...] = jnp.zeros_like(db_ref)

        cosf, sinf = cos_ref[...], sin_ref[...]
        for h in range(4):
            sl = slice(h * 128, (h + 1) * 128)
            qr = _rot(q_ref[:, sl], cosf, sinf)
            kr = _rot(k_ref[:, sl], cosf, sinf) * (128 ** -0.5)
            qb, kb, vb = _bf(qr), _bf(kr), _bf(v_ref[:, sl])
            s0b = _bf(st_ref[0, h])
            md, qd, kd = md_ref[h], qd_ref[h], kd_ref[h]
            scb = _bf(_dg(qb, kb, 1, 1) * md)
            qdb = _bf(qr * qd)
            kdb = _bf(kr * kd)
            r = _dg(scb, vb, 1, 0) + _dg(qdb, s0b, 1, 0)
            y, rstd = _norm_stats(r)
            gng = gng_ref[:, sl]
            gn = y * gng + gnb_ref[:, sl]
            rg = g_ref[:, sl]
            sg = _sigmoid(rg)
            d_o = do_ref[:, sl]
            d_gn = d_o * (rg * sg)
            dg_ref[:, sl] += jnp.sum(d_gn * y, axis=0, keepdims=True)
            db_ref[:, sl] += jnp.sum(d_gn, axis=0, keepdims=True)
            drb = _bf(_norm_bwd(d_gn * gng, y, rstd))
            g0 = gstate[h]
            gb = _bf(g0)
            dscb = _bf(_dg(drb, vb, 1, 1) * md)
            dqr = _dg(dscb, kb, 1, 0) + _dg(drb, s0b, 1, 1) * qd
            dkr = _dg(dscb, qb, 0, 0) + _dg(vb, gb, 1, 1) * kd
            dv = _dg(scb, drb, 0, 0) + _dg(kdb, gb, 1, 0)
            gstate[h] = g0 * cd_ref[h, 0:1, :] + _dg(qdb, drb, 0, 0)
            dp_ref[:, 0 * 512 + h * 128:0 * 512 + (h + 1) * 128] = _bf(_rot_t(dqr, cosf, sinf))
            dp_ref[:, 1 * 512 + h * 128:1 * 512 + (h + 1) * 128] = _bf(_rot_t(dkr, cosf, sinf) * (128 ** -0.5))
            dp_ref[:, 2 * 512 + h * 128:2 * 512 + (h + 1) * 128] = _bf(dv)
            dp_ref[:, 3 * 512 + h * 128:3 * 512 + (h + 1) * 128] = _bf(d_o * gn * (sg * (1.0 + rg * (1.0 - sg))))

    acc = pl.BlockSpec((1, 512), lambda n: (0, 0))
    return pl.pallas_call(
        kernel_body, name="ret_bwd", grid=(N,),
        out_shape=(jax.ShapeDtypeStruct((S, 2048), BF16), jax.ShapeDtypeStruct((1, 512), F32),
                   jax.ShapeDtypeStruct((1, 512), F32)),
        in_specs=[col(0), col(1), col(2), col(3), tab, tab, dec, dec, dec, cdec, vec, vec, st,
                  pl.BlockSpec((CHUNK, 512), lambda n: (N - 1 - n, 0))],
        out_specs=(pl.BlockSpec((CHUNK, 2048), lambda n: (N - 1 - n, 0)), acc, acc),
        scratch_shapes=[pltpu.VMEM((4, CHUNK, CHUNK), F32)],
        compiler_params=_cparams(("arbitrary",)),
    )(proj, proj, proj, proj, *tables, gn_g, gn_b, states, d_out)


SB_T = 256
SB_SCALE = 64 ** -0.5
SB_Q_COL, SB_K_COL, SB_V_COL = 2048 // 128, 2560 // 128, 3072 // 128


def _head_masks():
    lane = lax.broadcasted_iota(jnp.int32, (1, 128), 1)
    m0 = (lane < 64).astype(F32)
    return m0, 1.0 - m0


def _tri(n, cmp):
    r = lax.broadcasted_iota(jnp.int32, (n, n), 0)
    c = lax.broadcasted_iota(jnp.int32, (n, n), 1)
    return cmp(r, c)


def _tri_sum(x, tri):
    hi = _bf(x)
    lo = _bf(x - hi.astype(F32))
    return _dg(hi, tri, 1, 0) + _dg(lo, tri, 1, 0)


def _sb_weights(qms, kblks, upper, carry, causal):
    tiles = [(b, h) for b in range(len(kblks)) for h in range(2)]
    zs = [_dg(qms[h], kblks[b], 1, 1) for b, h in tiles]
    lgs = [-(jnp.maximum(z, 0.0) + jnp.log(1.0 + jnp.exp(-jnp.abs(z)))) for z in zs]
    if causal is not None:
        lgs = [jnp.where(causal, lg, 0.0) for lg in lgs]
    carries = list(carry)
    for t in range(len(tiles) - 2):
        carries.append(carries[t] + jnp.sum(lgs[t], axis=1, keepdims=True))
    his = [_bf(lg) for lg in lgs]
    los = [_bf(lg - hi.astype(F32)) for lg, hi in zip(lgs, his)]
    later = [_dg(hi, upper, 1, 0) for hi in his]
    later = [r + _dg(lo, upper, 1, 0) for r, lo in zip(later, los)]
    a = [jnp.exp(lg + z + (r + c)) for lg, z, r, c in zip(lgs, zs, later, carries)]
    if causal is not None:
        a = [jnp.where(causal, x, 0.0) for x in a]
    out = tuple(carries[t] + jnp.sum(lgs[t], axis=1, keepdims=True) for t in (len(tiles) - 2, len(tiles) - 1))
    return [a[2 * b:2 * b + 2] for b in range(len(kblks))], out


def _sb_fwd(proj, comm=None):
    S = proj.shape[0]
    T = min(SB_T, S)
    nq = S // T

    def body(q_ref, k_ref, v_ref, o_ref, kb_ref, vm_ref, acc_ref):
        i = pl.program_id(1)
        m0, m1 = _head_masks()

        @pl.when(i == 0)
        def _():
            v = v_ref[...]
            kb_ref[...] = _bf(k_ref[...])
            vm_ref[0] = _bf(v * m0)
            vm_ref[1] = _bf(v * m1)

        q = q_ref[...]
        qm = (_bf(q * (m0 * SB_SCALE)), _bf(q * (m1 * SB_SCALE)))
        upper = _tri(T, lambda r, c: r > c).astype(BF16)
        causal = _tri(T, lambda r, c: c < r)

        def tiles(js, carry, mask, first):
            ks = [pl.multiple_of(j * T, T) for j in js]
            a, out = _sb_weights(qm, [kb_ref[pl.ds(k, T), :] for k in ks], upper, carry, mask)
            parts = [_dg(_bf(a[b][h]), vm_ref[h, pl.ds(k, T), :], 1, 0) for b, k in enumerate(ks) for h in range(2)]
            part = functools.reduce(lambda u, w: u + w, parts)
            if first:
                acc_ref[...] = part
            else:
                acc_ref[...] += part
            return out

        zero = jnp.zeros((T, 1), F32)
        carry = tiles([i], (zero, zero), causal, True)
        carry = lax.fori_loop(0, i % 2, lambda _, c: tiles([i - 1], c, None, False), carry)
        top = i - 1 - i % 2
        lax.fori_loop(0, i // 2, lambda jj, c: tiles([top - 2 * jj, top - 2 * jj - 1], c, None, False), carry)
        o_ref[...] = acc_ref[...]

    full = lambda c: pl.BlockSpec((S, 128), lambda p, i, c=c: (0, c + p))
    outs, landed = _pcall(
        body, name="sb_fwd", grid=(4, nq), out_shape=[jax.ShapeDtypeStruct((S, SB_W), F32)],
        in_specs=[pl.BlockSpec((T, 128), lambda p, i: (i, SB_Q_COL + p)), full(SB_K_COL), full(SB_V_COL)],
        out_specs=[pl.BlockSpec((T, 128), lambda p, i: (i, p))],
        scratch_shapes=[pltpu.VMEM((S, 128), BF16), pltpu.VMEM((2, S, 128), BF16), pltpu.VMEM((T, 128), F32)],
        sem=("arbitrary", "arbitrary"), args=(proj, proj, proj), comm=comm)
    return outs[0] if comm is None else (outs[0], landed)


def _sb_bwd(proj, d_o, comm=None):
    S = proj.shape[0]
    T = min(SB_T, S)
    nq = S // T

    def body(q_ref, k_ref, v_ref, do_ref, dq_ref, dk_ref, dv_ref, kb_ref, kbm_ref, vb_ref, e_ref, dq_acc, dk_acc, dv_acc):
        i = pl.program_id(1)
        m0, m1 = _head_masks()

        @pl.when(i == 0)
        def _():
            k = k_ref[...]
            kb_ref[...] = _bf(k)
            kbm_ref[0] = _bf(k * m0)
            kbm_ref[1] = _bf(k * m1)
            vb_ref[...] = _bf(v_ref[...])
            dk_acc[...] = jnp.zeros_like(dk_acc)
            dv_acc[...] = jnp.zeros_like(dv_acc)

        q, d_out = q_ref[...], do_ref[...]
        qm = (_bf(q * (m0 * SB_SCALE)), _bf(q * (m1 * SB_SCALE)))
        dom = (_bf(d_out * m0), _bf(d_out * m1))
        upper = _tri(T, lambda r, c: r > c).astype(BF16)
        lower = _tri(T, lambda r, c: r < c).astype(BF16)
        causal = _tri(T, lambda r, c: c < r)

        def down(js, carry, mask):
            ks = [pl.multiple_of(j * T, T) for j in js]
            a, out = _sb_weights(qm, [kb_ref[pl.ds(k, T), :] for k in ks], upper, carry, mask)
            da = [[_dg(dom[h], vb_ref[pl.ds(k, T), :], 1, 1) for h in range(2)] for k in ks]
            for b, (j, k) in enumerate(zip(js, ks)):
                for h in range(2):
                    e_ref[h, j] = a[b][h] * da[b][h]
                dv_acc[pl.ds(k, T), :] += _dg(_bf(a[b][0]), dom[0], 0, 0) + _dg(_bf(a[b][1]), dom[1], 0, 0)
            return out

        def up(js, carry, mask):
            ks = [pl.multiple_of(j * T, T) for j in js]
            tiles = [(b, h) for b in range(len(js)) for h in range(2)]
            zs = [_dg(qm[h], kb_ref[pl.ds(ks[b], T), :], 1, 1) for b, h in tiles]
            es = [e_ref[h, js[b]] for b, h in tiles]
            carries = list(carry)
            for t in range(len(tiles)):
                carries.append(carries[t] + jnp.sum(es[t], axis=1, keepdims=True))
            his = [_bf(e) for e in es]
            los = [_bf(e - hi.astype(F32)) for e, hi in zip(es, his)]
            d_lg = [_dg(hi, lower, 1, 0) for hi in his]
            d_lg = [r + _dg(lo, lower, 1, 0) + c for r, lo, c in zip(d_lg, los, carries)]
            ens = [jnp.exp(-jnp.abs(z)) for z in zs]
            invs = [1.0 / (1.0 + en) for en in ens]
            betas = [jnp.where(z >= 0.0, inv, en * inv) for z, en, inv in zip(zs, ens, invs)]
            dzs = [e * (1.0 - b) - d * b for e, b, d in zip(es, betas, d_lg)]
            if mask is not None:
                dzs = [jnp.where(mask, dz, 0.0) for dz in dzs]
            dzs = [_bf(dz) for dz in dzs]
            parts = [_dg(dzs[t], kbm_ref[h, pl.ds(ks[b], T), :], 1, 0) for t, (b, h) in enumerate(tiles)]
            dq_acc[...] += functools.reduce(lambda u, w: u + w, parts)
            for b, k in enumerate(ks):
                dk_acc[pl.ds(k, T), :] += _dg(dzs[2 * b], qm[0], 0, 0) + _dg(dzs[2 * b + 1], qm[1], 0, 0)
            return tuple(carries[-2:])

        zero = jnp.zeros((T, 1), F32)
        carry = down([i], (zero, zero), causal)
        carry = lax.fori_loop(0, i % 2, lambda _, c: down([i - 1], c, None), carry)
        top = i - 1 - i % 2
        lax.fori_loop(0, i // 2, lambda jj, c: down([top - 2 * jj, top - 2 * jj - 1], c, None), carry)

        dq_acc[...] = jnp.zeros_like(dq_acc)
        carry = lax.fori_loop(0, i // 2, lambda jj, c: up([2 * jj, 2 * jj + 1], c, None), (zero, zero))
        carry = lax.fori_loop(0, i % 2, lambda _, c: up([i - 1], c, None), carry)
        up([i], carry, causal)
        dq_ref[...] = _bf(dq_acc[...] * SB_SCALE)

        @pl.when(i == nq - 1)
        def _():
            dk_ref[...] = _bf(dk_acc[...])
            dv_ref[...] = _bf(dv_acc[...])

    full = lambda c: pl.BlockSpec((S, 128), lambda p, i, c=c: (0, c + p))
    tile = pl.BlockSpec((T, 128), lambda p, i: (i, p))
    acc = pl.BlockSpec((S, 128), lambda p, i: (0, p))
    out = jax.ShapeDtypeStruct((S, SB_W), BF16)
    outs, landed = _pcall(
        body, name="sb_bwd", grid=(4, nq), out_shape=[out, out, out],
        in_specs=[pl.BlockSpec((T, 128), lambda p, i: (i, SB_Q_COL + p)), full(SB_K_COL), full(SB_V_COL), tile],
        out_specs=[tile, acc, acc],
        scratch_shapes=[pltpu.VMEM((S, 128), BF16), pltpu.VMEM((2, S, 128), BF16), pltpu.VMEM((S, 128), BF16),
                        pltpu.VMEM((2, nq, T, T), F32), pltpu.VMEM((T, 128), F32), pltpu.VMEM((S, 128), F32),
                        pltpu.VMEM((S, 128), F32)],
        sem=("arbitrary", "arbitrary"), args=(proj, proj, proj, d_o), comm=comm)
    return tuple(outs) if comm is None else (tuple(outs), landed)


SGU_U_COL, SGU_V_COL = 3584 // 512, 4096 // 512


def _causal(w):
    r = lax.broadcasted_iota(jnp.int32, (CHUNK, CHUNK), 0)
    c = lax.broadcasted_iota(jnp.int32, (CHUNK, CHUNK), 1)
    return jnp.where(r >= c, w, 0.0)


def _sgu_fwd(proj, ln_g, ln_b, w, b):
    S = proj.shape[0]
    N = S // CHUNK

    def body(u_ref, v_ref, g_ref, b_ref, w_ref, bias_ref, out_ref):
        u = _gelu(u_ref[...])
        xh, _ = _norm_stats(_gelu(v_ref[...]))
        vn = _bf(xh * g_ref[...] + b_ref[...])
        for g in range(4):
            sl = slice(g * 128, (g + 1) * 128)
            sv = _dg(_bf(_causal(w_ref[g])), vn[:, sl], 1, 0) + bias_ref[g]
            out_ref[:, sl] = u[:, sl] * sv

    vec = pl.BlockSpec((1, 512), lambda n: (0, 0))
    return pl.pallas_call(
        body, name="sgu_fwd", grid=(N,),
        out_shape=jax.ShapeDtypeStruct((S, SGU_W), F32),
        in_specs=[pl.BlockSpec((CHUNK, 512), lambda n: (n, SGU_U_COL)),
                  pl.BlockSpec((CHUNK, 512), lambda n: (n, SGU_V_COL)), vec, vec,
                  pl.BlockSpec((4, CHUNK, CHUNK), lambda n: (0, 0, 0)), pl.BlockSpec((4, CHUNK, 1), lambda n: (0, 0, 0))],
        out_specs=pl.BlockSpec((CHUNK, 512), lambda n: (n, 0)),
        compiler_params=_cparams(("parallel",)),
    )(proj, proj, ln_g, ln_b, w, b)


def _sgu_bwd(proj, ln_g, ln_b, w, b, d_out):
    S = proj.shape[0]
    N = S // CHUNK

    def body(u_ref, v_ref, g_ref, b_ref, w_ref, bias_ref, do_ref, dp_ref, dg_ref, db_ref, dw_ref, dbias_ref):
        @pl.when(pl.program_id(0) == 0)
        def _():
            dg_ref[...] = jnp.zeros_like(dg_ref)
            db_ref[...] = jnp.zeros_like(db_ref)
            dw_ref[...] = jnp.zeros_like(dw_ref)
            dbias_ref[...] = jnp.zeros_like(dbias_ref)

        gu, gv = u_ref[...], v_ref[...]
        u = _gelu(gu)
        xh, rstd = _norm_stats(_gelu(gv))
        ln_gain = g_ref[...]
        vn = _bf(xh * ln_gain + b_ref[...])
        d_o = do_ref[...]
        d_vn = []
        for g in range(4):
            sl = slice(g * 128, (g + 1) * 128)
            wc = _bf(_causal(w_ref[g]))
            sv = _dg(wc, vn[:, sl], 1, 0) + bias_ref[g]
            dp_ref[:, sl] = _bf(d_o[:, sl] * sv * _gelu_grad(gu[:, sl]))
            d_sv = d_o[:, sl] * u[:, sl]
            dbias_ref[g] += jnp.sum(d_sv, axis=1, keepdims=True)
            d_svb = _bf(d_sv)
            dw_ref[g] += _causal(_dg(d_svb, vn[:, sl], 1, 1))
            d_vn.append(_dg(wc, d_svb, 0, 0))
        d_vn = jnp.concatenate(d_vn, axis=1)
        dg_ref[...] += jnp.sum(d_vn * xh, axis=0, keepdims=True)
        db_ref[...] += jnp.sum(d_vn, axis=0, keepdims=True)
        dp_ref[:, 512:1024] = _bf(_norm_bwd(d_vn * ln_gain, xh, rstd) * _gelu_grad(gv))

    vec = pl.BlockSpec((1, 512), lambda n: (0, 0))
    wspec = pl.BlockSpec((4, CHUNK, CHUNK), lambda n: (0, 0, 0))
    bspec = pl.BlockSpec((4, CHUNK, 1), lambda n: (0, 0, 0))
    return pl.pallas_call(
        body, name="sgu_bwd", grid=(N,),
        out_shape=(jax.ShapeDtypeStruct((S, 1024), BF16), jax.ShapeDtypeStruct((1, 512), F32),
                   jax.ShapeDtypeStruct((1, 512), F32), jax.ShapeDtypeStruct((4, CHUNK, CHUNK), F32),
                   jax.ShapeDtypeStruct((4, CHUNK, 1), F32)),
        in_specs=[pl.BlockSpec((CHUNK, 512), lambda n: (n, SGU_U_COL)),
                  pl.BlockSpec((CHUNK, 512), lambda n: (n, SGU_V_COL)), vec, vec, wspec, bspec,
                  pl.BlockSpec((CHUNK, 512), lambda n: (n, 0))],
        out_specs=(pl.BlockSpec((CHUNK, 1024), lambda n: (n, 0)), vec, vec, wspec, bspec),
        compiler_params=_cparams(("arbitrary",)),
    )(proj, proj, ln_g, ln_b, w, b, d_out)


GATE_COL = 4608 // 512


def _merge_fwd(proj, branches, p_list, tm=512):
    S = proj.shape[0]
    tm = min(tm, S)

    def body(r_ref, s_ref, g_ref, pr_ref, ps_ref, pg_ref, gr_ref, gs_ref, gg_ref, m_ref, br_ref):
        acc = None
        for k, (x_ref, p_ref, gate_ref) in enumerate(((r_ref, pr_ref, gr_ref), (s_ref, ps_ref, gs_ref),
                                                      (g_ref, pg_ref, gg_ref))):
            br = _dg(_bf(x_ref[...]), _bf(p_ref[...]), 1, 0)
            br_ref[k] = br
            term = _sigmoid(gate_ref[...]) * br
            acc = term if acc is None else acc + term
        m_ref[...] = _bf(acc)

    xs = pl.BlockSpec((tm, 512), lambda i, n: (i, 0))
    ps = pl.BlockSpec((512, 512), lambda i, n: (0, n))
    gate = lambda k: pl.BlockSpec((tm, 512), lambda i, n, k=k: (i, GATE_COL + 2 * k + n))
    return pl.pallas_call(
        body, name="merge_fwd", grid=(S // tm, 2),
        out_shape=(jax.ShapeDtypeStruct((S, D_MODEL), BF16), jax.ShapeDtypeStruct((3, S, D_MODEL), F32)),
        in_specs=[xs, xs, xs, ps, ps, ps, gate(0), gate(1), gate(2)],
        out_specs=(pl.BlockSpec((tm, 512), lambda i, n: (i, n)), pl.BlockSpec((3, tm, 512), lambda i, n: (0, i, n))),
        compiler_params=_cparams(("parallel", "parallel")),
    )(*branches, *p_list, proj, proj, proj)


def _gate_bwd(proj, br, d_merged, tm=512):
    S = proj.shape[0]
    tm = min(tm, S)

    def body(dm_ref, br_ref, gr_ref, gs_ref, gg_ref, dbr_ref, dgate_ref):
        dm = dm_ref[...]
        for k, gate_ref in enumerate((gr_ref, gs_ref, gg_ref)):
            s = _sigmoid(gate_ref[...])
            dbr_ref[k] = _bf(dm * s)
            dgate_ref[k] = _bf(dm * br_ref[k] * (s * (1.0 - s)))

    gate = lambda k: pl.BlockSpec((tm, 512), lambda i, n, k=k: (i, GATE_COL + 2 * k + n))
    three = pl.BlockSpec((3, tm, 512), lambda i, n: (0, i, n))
    return pl.pallas_call(
        body, name="gate_bwd", grid=(S // tm, 2),
        out_shape=(jax.ShapeDtypeStruct((3, S, D_MODEL), BF16), jax.ShapeDtypeStruct((3, S, D_MODEL), BF16)),
        in_specs=[pl.BlockSpec((tm, 512), lambda i, n: (i, n)), three, gate(0), gate(1), gate(2)],
        out_specs=(three, three),
        compiler_params=_cparams(("parallel", "parallel")),
    )(d_merged, br, proj, proj, proj)


def _ln_bwd(dy, u, g, tm=256):
    S, D = u.shape
    tm = min(tm, S)

    def body(dy_ref, u_ref, g_ref, du_ref, dub_ref, dg_ref, db_ref):
        @pl.when(pl.program_id(0) == 0)
        def _():
            dg_ref[...] = jnp.zeros_like(dg_ref)
            db_ref[...] = jnp.zeros_like(db_ref)

        dy_t = dy_ref[...]
        xh, rstd = _norm_stats(u_ref[...])
        dg_ref[...] += jnp.sum(dy_t * xh, axis=0, keepdims=True)
        db_ref[...] += jnp.sum(dy_t, axis=0, keepdims=True)
        du = _norm_bwd(dy_t * g_ref[...], xh, rstd)
        du_ref[...] = du
        dub_ref[...] = _bf(du)

    tile = pl.BlockSpec((tm, D), lambda i: (i, 0))
    vec = pl.BlockSpec((1, D), lambda i: (0, 0))
    return pl.pallas_call(
        body, name="ln_bwd", grid=(S // tm,),
        out_shape=(jax.ShapeDtypeStruct((S, D), F32), jax.ShapeDtypeStruct((S, D), BF16),
                   jax.ShapeDtypeStruct((1, D), F32), jax.ShapeDtypeStruct((1, D), F32)),
        in_specs=[tile, tile, vec], out_specs=(tile, tile, vec, vec),
        compiler_params=_cparams(("arbitrary",)),
    )(dy, u, g)


def _loss_grad(y, target, tm=256):
    S, D = y.shape
    tm = min(tm, S)

    def body(y_ref, t_ref, dy_ref, sq_ref):
        @pl.when(pl.program_id(0) == 0)
        def _():
            sq_ref[...] = jnp.zeros_like(sq_ref)

        err = y_ref[...] - t_ref[...]
        dy_ref[...] = err * (1.0 / D)
        sq_ref[...] += jnp.sum(err * err, axis=0, keepdims=True)

    tile = pl.BlockSpec((tm, D), lambda i: (i, 0))
    vec = pl.BlockSpec((1, D), lambda i: (0, 0))
    return pl.pallas_call(
        body, name="loss_grad", grid=(S // tm,),
        out_shape=(jax.ShapeDtypeStruct((S, D), F32), jax.ShapeDtypeStruct((1, D), F32)),
        in_specs=[tile, tile], out_specs=(tile, vec),
        compiler_params=_cparams(("arbitrary",)),
    )(y, target)


def _layer_fwd(x, x_bf, W, tables, sb_comm=None):
    proj = _matmul(x_bf, W["w_in"], "nn", name="proj", tm=1024, tn=768, tk=1024)
    retg, states = _ret_fwd(proj, tables, W["ret_gn_g"], W["ret_gn_b"])
    if sb_comm is None:
        sb = _sb_fwd(proj)
    else:
        sb, landed = _sb_fwd(proj, comm=sb_comm[0])
        sb_comm[1](landed)
    sg = _sgu_fwd(proj, W["sgu_ln_g"], W["sgu_ln_b"], W["sgu_w"], W["sgu_b"])
    merged, br = _merge_fwd(proj, (retg, sb, sg), (W["p_ret"], W["p_sb"], W["p_sgu"]))
    u1, x1, x1_bf = _matmul(merged, W["w_out"], "nn", name="out_ln", tm=512, tn=1024, tk=1024, epi="ln",
                            extra=(x, W["ln1_g"], W["ln1_b"]))
    act = _matmul(x1_bf, W["w_up"], "nn", name="up", tm=1024, tn=1024, tk=1024, epi="relu2")
    u2, x2, x2_bf = _matmul(act, W["w_down"], "nn", name="down_ln", tm=512, tn=1024, tk=1024, epi="ln",
                            extra=(x1, W["ln2_g"], W["ln2_b"]))
    saved = dict(x_bf=x_bf, proj=proj, retg=retg, states=states, sb=sb, sg=sg, merged=merged, br=br, u1=u1,
                 x1_bf=x1_bf, act=act, u2=u2)
    return x2, x2_bf, saved


def _layer_bwd(d_x2, W, tables, sv, chunk_dtype=None, sb_comm_fn=None, dx_comm_fn=None):
    dt = F32 if chunk_dtype is None else chunk_dtype
    rows, cols = (None, None) if chunk_dtype is None else ("rows", "cols")
    g, landed = {}, {}
    du2, du2_bf, g["ln2_g"], g["ln2_b"] = _ln_bwd(d_x2, sv["u2"], W["ln2_g"])
    d_hpre = _matmul(du2_bf, W["w_down"], "nt", name="d_act", tm=1024, tn=1024, tk=1024, epi="drelu2",
                     extra=(sv["act"],), out_dtype=BF16)
    g["w_down"] = _matmul(sv["act"], du2_bf, "tn", name="dw_down", tm=512, tn=1024, tk=4096, out_dtype=dt, chunks=rows)
    g["w_up"] = _matmul(sv["x1_bf"], d_hpre, "tn", name="dw_up", tm=1024, tn=512, tk=4096, out_dtype=dt, chunks=cols)
    d_x1 = _matmul(d_hpre, W["w_up"], "nt", name="d_x1", tm=1024, tn=1024, tk=1024, epi="add", extra=(du2,))
    du1, du1_bf, g["ln1_g"], g["ln1_b"] = _ln_bwd(d_x1, sv["u1"], W["ln1_g"])
    d_merged = _matmul(du1_bf, W["w_out"], "nt", name="d_merged", tm=1024, tn=1024, tk=1024)
    g["w_out"] = _matmul(sv["merged"], du1_bf, "tn", name="dw_out", tm=1024, tn=512, tk=4096, out_dtype=dt, chunks=rows)
    d_br, d_gate = _gate_bwd(sv["proj"], sv["br"], d_merged)
    d_branch = []
    for k, (nm, act) in enumerate((("p_ret", sv["retg"]), ("p_sb", sv["sb"]), ("p_sgu", sv["sg"]))):
        d_branch.append(_matmul(d_br[k], W[nm], "nt", name="d_" + nm[2:], tm=1024, tn=512, tk=1024))
        g[nm] = _matmul(act, d_br[k], "tn", name="dw_" + nm[2:], tm=512, tn=1024, tk=2048, out_dtype=dt, chunks=cols)
    d_ret, g["ret_gn_g"], g["ret_gn_b"] = _ret_bwd(sv["proj"], tables, W["ret_gn_g"], W["ret_gn_b"], sv["states"],
                                                   d_branch[0])
    if sb_comm_fn is None:
        d_sq, d_sk, d_sv = _sb_bwd(sv["proj"], d_branch[1])
    else:
        (d_sq, d_sk, d_sv), landed["sb"] = _sb_bwd(sv["proj"], d_branch[1], comm=sb_comm_fn(g))
    d_sgu, g["sgu_ln_g"], g["sgu_ln_b"], g["sgu_w"], g["sgu_b"] = _sgu_bwd(
        sv["proj"], W["sgu_ln_g"], W["sgu_ln_b"], W["sgu_w"], W["sgu_b"], d_branch[2])
    d_proj = jnp.concatenate([d_ret, d_sq, d_sk, d_sv, d_sgu, d_gate[0], d_gate[1], d_gate[2]], axis=1)
    g["w_in"] = _matmul(sv["x_bf"], d_proj, "tn", name="dw_in", tm=1024, tn=640, tk=4096, out_dtype=dt)
    if chunk_dtype is not None:
        g["w_in"] = g["w_in"].reshape(D_MODEL, N_DEV, N_IN // N_DEV).transpose(1, 0, 2)
    d_x = _matmul(d_proj, W["w_in"], "nt", name="d_x", tm=1024, tn=1024, tk=1920, epi="add", extra=(du1,),
                  comm=None if dx_comm_fn is None else dx_comm_fn(g))
    if dx_comm_fn is not None:
        d_x, landed["dx"] = d_x
    return d_x, g, landed


BIG = ("w_in", "p_ret", "p_sb", "p_sgu", "w_out", "w_up", "w_down")
SMALL = ("ret_gn_g", "ret_gn_b", "sgu_ln_g", "sgu_ln_b", "sgu_w", "sgu_b", "ln1_g", "ln1_b", "ln2_g", "ln2_b")
GATHER_KIND = {"w_in": "slab", "p_ret": "cols", "p_sb": "cols", "p_sgu": "cols", "w_out": "rows", "w_up": "cols",
               "w_down": "rows"}


def _small_weights(small, l):
    W = {}
    for n in SMALL:
        if n == "sgu_w":
            W[n] = small[n][l]
        elif n == "sgu_b":
            W[n] = small[n][l].reshape(4, CHUNK, 1)
        else:
            W[n] = small[n][l].reshape(1, -1)
    return W


def _local_step(x, target, full, small):
    tables = _ret_tables(x.shape[0])
    Ws = [{**{n: full[n][l] for n in BIG}, **_small_weights(small, l)} for l in range(DEPTH)]
    saved = []
    h, h_bf = x, _bf(x)
    for l in range(DEPTH):
        h, h_bf, sv = _layer_fwd(h, h_bf, Ws[l], tables)
        saved.append(sv)
    d_h, sq = _loss_grad(h, target)
    grads = [None] * DEPTH
    for l in reversed(range(DEPTH)):
        d_h, grads[l], _ = _layer_bwd(d_h, Ws[l], tables, saved[l])
    return sq, d_h, grads


def _adam(w, parts, m, v, name, tr=256):
    L, R, C = w.shape
    tr = min(tr, R)
    assert R % tr == 0 and len(parts) == L

    def body(*refs):
        w_ref, p_refs, (m_ref, v_ref, g_ref, d_ref, nm_ref, nv_ref) = refs[0], refs[1:1 + L], refs[1 + L:]
        layer = pl.program_id(0)
        g = None
        for li, p_ref in enumerate(p_refs):
            s = p_ref[0].astype(F32)
            for j in range(1, N_DEV):
                s = s + p_ref[j].astype(F32)
            g = s if g is None else jnp.where(layer == li, s, g)
        m2 = ADAM_B1 * m_ref[...] + (1.0 - ADAM_B1) * g
        v2 = ADAM_B2 * v_ref[...] + (1.0 - ADAM_B2) * (g * g)
        m_hat = m2 / (1.0 - ADAM_B1 ** ADAM_STEP)
        v_hat = v2 / (1.0 - ADAM_B2 ** ADAM_STEP)
        g_ref[...] = g
        d_ref[...] = -ADAM_LR * (m_hat / (jnp.sqrt(v_hat) + ADAM_EPS) + ADAM_WD * w_ref[...])
        nm_ref[...] = m2
        nv_ref[...] = v2

    tile = pl.BlockSpec((None, tr, C), lambda l, i: (l, i, 0))
    part = lambda li: pl.BlockSpec((N_DEV, tr, C), lambda l, i, li=li: (0, jnp.where(l == li, i, 0), 0))
    out = jax.ShapeDtypeStruct((L, R, C), F32)
    return pl.pallas_call(
        body, name=name, grid=(L, R // tr), out_shape=(out, out, out, out),
        in_specs=[tile] + [part(li) for li in range(L)] + [tile, tile],
        out_specs=(tile, tile, tile, tile),
        compiler_params=_cparams(("parallel", "parallel")),
    )(w, *parts, m, v)


def _pack_small(tree):
    return jnp.concatenate([tree[n].reshape(-1, 128) for n in SMALL], axis=0)


def _unpack_small(packed, like):
    out, r = {}, 0
    for n in SMALL:
        rows = like[n].size // 128
        out[n] = packed[r:r + rows].reshape(like[n].shape)
        r += rows
    return out


WEIGHTS = ("w_in", "ret_gn_g", "ret_gn_b", "sgu_ln_g", "sgu_ln_b", "sgu_w", "sgu_b", "p_ret", "p_sb", "p_sgu", "w_out",
           "ln1_g", "ln1_b", "w_up", "w_down", "ln2_g", "ln2_b")


def kernel(x, w_in, ret_gn_g, ret_gn_b, sgu_ln_g, sgu_ln_b, sgu_w, sgu_b, p_ret, p_sb, p_sgu, w_out, ln1_g, ln1_b, w_up, w_down, ln2_g, ln2_b, loss_target, m_w_in, m_ret_gn_g, m_ret_gn_b, m_sgu_ln_g, m_sgu_ln_b, m_sgu_w, m_sgu_b, m_p_ret, m_p_sb, m_p_sgu, m_w_out, m_ln1_g, m_ln1_b, m_w_up, m_w_down, m_ln2_g, m_ln2_b, v_w_in, v_ret_gn_g, v_ret_gn_b, v_sgu_ln_g, v_sgu_ln_b, v_sgu_w, v_sgu_b, v_p_ret, v_p_sb, v_p_sgu, v_w_out, v_ln1_g, v_ln1_b, v_w_up, v_w_down, v_ln2_g, v_ln2_b):
    w = dict(zip(WEIGHTS, (w_in, ret_gn_g, ret_gn_b, sgu_ln_g, sgu_ln_b, sgu_w, sgu_b, p_ret, p_sb, p_sgu, w_out,
                           ln1_g, ln1_b, w_up, w_down, ln2_g, ln2_b)))
    m = dict(zip(WEIGHTS, (m_w_in, m_ret_gn_g, m_ret_gn_b, m_sgu_ln_g, m_sgu_ln_b, m_sgu_w, m_sgu_b, m_p_ret, m_p_sb,
                           m_p_sgu, m_w_out, m_ln1_g, m_ln1_b, m_w_up, m_w_down, m_ln2_g, m_ln2_b)))
    v = dict(zip(WEIGHTS, (v_w_in, v_ret_gn_g, v_ret_gn_b, v_sgu_ln_g, v_sgu_ln_b, v_sgu_w, v_sgu_b, v_p_ret, v_p_sb,
                           v_p_sgu, v_w_out, v_ln1_g, v_ln1_b, v_w_up, v_w_down, v_ln2_g, v_ln2_b)))

    small = {n: w[n] for n in SMALL}
    shard = {n: _bf(w[n]) for n in BIG}
    tables = _ret_tables(x.shape[1])
    Ws = [_small_weights(small, l) for l in range(DEPTH)]
    unslab = lambda z: z.transpose(1, 0, 2).reshape(D_MODEL, N_IN)

    (slabs,) = _exchange([_gather_transfer(shard["w_in"], 0, "slab")], "gather_w_in0")
    Ws[0]["w_in"] = unslab(slabs)
    later = [(n, 0) for n in BIG[1:]] + [(n, 1) for n in BIG]

    def weights_landed(landed):
        for (n, l), z in zip(later, landed):
            Ws[l][n] = unslab(z) if n == "w_in" else z

    gather = _Comm([_gather_transfer(shard[n], l, GATHER_KIND[n]) for n, l in later])
    h, h_bf, saved0 = _layer_fwd(x[0], _bf(x[0]), Ws[0], tables, sb_comm=(gather, weights_landed))
    h, _, saved1 = _layer_fwd(h, h_bf, Ws[1], tables)
    d_h, sq = _loss_grad(h, loss_target[0])
    loss = lax.psum(0.5 * jnp.sum(sq) / D_MODEL, ("x", "y", "c"))

    d_h, g1, _ = _layer_bwd(d_h, Ws[1], tables, saved1, chunk_dtype=BF16)
    early = [(n, 1) for n in BIG] + [(n, 0) for n in BIG[1:]]

    def early_scatter(g0):
        return _Comm([_scatter_transfer((g1 if l else g0)[n]) for n, l in early])

    def late_scatter(g0):
        mine = _pack_small({n: jnp.stack([g0[n].reshape(small[n].shape[1:]), g1[n].reshape(small[n].shape[1:])])
                            for n in SMALL})
        return _Comm([_scatter_transfer(g0["w_in"]), _slab_transfer(mine)])

    d_x, g0, landed = _layer_bwd(d_h, Ws[0], tables, saved0, chunk_dtype=BF16, sb_comm_fn=early_scatter,
                                 dx_comm_fn=late_scatter)
    parts = dict(zip(early, landed["sb"]))
    parts[("w_in", 0)], small_parts = landed["dx"]

    grad, delta, new_m, new_v = {}, {}, {}, {}
    for n in BIG:
        grad[n], delta[n], new_m[n], new_v[n] = _adam(w[n], [parts[(n, l)] for l in range(DEPTH)], m[n], v[n],
                                                      "adam_" + n)
    res = _adam(_pack_small(small)[None], [small_parts], _pack_small({n: m[n] for n in SMALL})[None],
                _pack_small({n: v[n] for n in SMALL})[None], "adam_small", tr=small_parts.shape[1])
    for tree, packed in zip((grad, delta, new_m, new_v), res):
        tree.update(_unpack_small(packed[0], small))

    return (loss, d_x[None], *[grad[n] for n in WEIGHTS], *[delta[n] for n in WEIGHTS],
            *[new_m[n] for n in WEIGHTS], *[new_v[n] for n in WEIGHTS])
```

```python
import functools
import math

import numpy as np
import jax
import jax.numpy as jnp
from jax import lax
from jax.experimental import pallas as pl
from jax.experimental.pallas import tpu as pltpu

F32 = jnp.float32
BF16 = jnp.bfloat16

N_DEV = 8
DEPTH = 2
D_MODEL = 1024
CHUNK = 128
RET_W = 512
SB_W = 512
SGU_W = 512
N_IN = 7680
LN_EPS = 1e-5
ALPHA = (2 * DEPTH) ** 0.25
ROPE_BASE = 10000.0
ADAM_LR, ADAM_B1, ADAM_B2, ADAM_EPS, ADAM_WD, ADAM_STEP = 0.001, 0.9, 0.999, 1e-08, 0.01, 10
VMEM_LIMIT = 56 * 1024 * 1024

_GELU_K = math.sqrt(2.0 / math.pi)
_GELU_C = 0.044715


def _cparams(sem=None):
    return pltpu.CompilerParams(dimension_semantics=sem, vmem_limit_bytes=VMEM_LIMIT)


def _dg(a, b, ca, cb):
    return lax.dot_general(a, b, (((ca,), (cb,)), ((), ())), preferred_element_type=F32)


def _bf(x):
    return x.astype(BF16)


def _sigmoid(x):
    return 1.0 / (1.0 + jnp.exp(-x))


def _gelu(x):
    t = jnp.tanh(_GELU_K * (x + _GELU_C * (x * x * x)))
    return x * (0.5 * (1.0 + t))


def _gelu_grad(x):
    t = jnp.tanh(_GELU_K * (x + _GELU_C * (x * x * x)))
    return 0.5 * (1.0 + t) + 0.5 * x * (1.0 - t * t) * (_GELU_K * (1.0 + 3.0 * _GELU_C * x * x))


def _norm_stats(u):
    mu = jnp.mean(u, axis=-1, keepdims=True)
    d = u - mu
    var = jnp.mean(d * d, axis=-1, keepdims=True)
    rstd = lax.rsqrt(var + LN_EPS)
    return d * rstd, rstd


def _norm_bwd(dxh, xh, rstd):
    return rstd * (dxh - jnp.mean(dxh, axis=-1, keepdims=True) - xh * jnp.mean(dxh * xh, axis=-1, keepdims=True))


class _Transfer:
    def __init__(self, src, dst_shape, src_at, dst_at):
        self.src, self.dst_shape, self.src_at, self.dst_at = src, tuple(dst_shape), src_at, dst_at


def _gather_transfer(shard, l, kind):
    _, r, c = shard.shape
    src_at = lambda ref, p: ref.at[l]
    if kind == "slab":
        return _Transfer(shard, (N_DEV, r, c), src_at, lambda ref, s: ref.at[s])
    if kind == "rows":
        return _Transfer(shard, (N_DEV * r, c), src_at, lambda ref, s: ref.at[pl.ds(pl.multiple_of(s * r, r), r), :])
    return _Transfer(shard, (r, N_DEV * c), src_at, lambda ref, s: ref.at[:, pl.ds(pl.multiple_of(s * c, c), c)])


def _scatter_transfer(chunks):
    return _Transfer(chunks, chunks.shape, lambda ref, p: ref.at[p], lambda ref, s: ref.at[s])


def _slab_transfer(arr):
    return _Transfer(arr, (N_DEV,) + arr.shape, lambda ref, p: ref, lambda ref, s: ref.at[s])


class _Comm:
    def __init__(self, transfers, relay=False):
        self.transfers = list(transfers)
        self.relay = relay
        self.n = len(self.transfers)
        self.arrays = [t.src for t in self.transfers]
        self.out_shape = [jax.ShapeDtypeStruct(t.dst_shape, t.src.dtype) for t in self.transfers]
        self.scratch = [pltpu.SemaphoreType.DMA((self.n * (N_DEV - 1),)), pltpu.SemaphoreType.DMA((self.n * (N_DEV - 1),)),
                        pltpu.SemaphoreType.DMA((self.n,))]

    def _relay_copies(self, srcs, dsts, send_sems, recv_sems, local_sems):
        x, y, c = lax.axis_index("x"), lax.axis_index("y"), lax.axis_index("c")
        me = 4 * x + 2 * y + c
        chips = [(1 - x, y), (x, 1 - y), (1 - x, 1 - y)]
        first, passed, own = [], [], []
        for t, tr in enumerate(self.transfers):
            def copy(k, src, sender, to, t=t, tr=tr):
                return pltpu.make_async_remote_copy(
                    src_ref=src, dst_ref=tr.dst_at(dsts[t], sender), send_sem=send_sems.at[t * (N_DEV - 1) + k],
                    recv_sem=recv_sems.at[t * (N_DEV - 1) + k], device_id=to, device_id_type=pl.DeviceIdType.MESH)

            mine = tr.src_at(srcs[t], me)
            first.append([copy(0, mine, me, (x, y, 1 - c))] + [copy(1 + j, mine, me, (px, py, c))
                                                                for j, (px, py) in enumerate(chips)])
            passed.append([copy(4 + j, tr.dst_at(dsts[t], 4 * px + 2 * py + c), 4 * px + 2 * py + c, (x, y, 1 - c))
                           for j, (px, py) in enumerate(chips)])
            own.append(pltpu.make_async_copy(mine, tr.dst_at(dsts[t], me), local_sems.at[t]))
        return first, passed, own

    def _copies(self, srcs, dsts, send_sems, recv_sems, local_sems):
        x, y, c = lax.axis_index("x"), lax.axis_index("y"), lax.axis_index("c")
        me = 4 * x + 2 * y + c
        copies = []
        for d in range(1, N_DEV):
            px = 1 - x if d & 4 else x
            py = 1 - y if d & 2 else y
            pc = 1 - c if d & 1 else c
            peer = 4 * px + 2 * py + pc
            for t, tr in enumerate(self.transfers):
                k = t * (N_DEV - 1) + d - 1
                copies.append(pltpu.make_async_remote_copy(
                    src_ref=tr.src_at(srcs[t], peer), dst_ref=tr.dst_at(dsts[t], me),
                    send_sem=send_sems.at[k], recv_sem=recv_sems.at[k],
                    device_id=(px, py, pc), device_id_type=pl.DeviceIdType.MESH))
        own = [pltpu.make_async_copy(tr.src_at(srcs[t], me), tr.dst_at(dsts[t], me), local_sems.at[t])
               for t, tr in enumerate(self.transfers)]
        return copies, own

    def start(self, srcs, dsts, *sems):
        if self.relay:
            first, _, own = self._relay_copies(srcs, dsts, *sems)
            for cp in own + [cp for per_t in first for cp in per_t]:
                cp.start()
            return
        copies, own = self._copies(srcs, dsts, *sems)
        for cp in own + copies:
            cp.start()

    def finish(self, srcs, dsts, *sems):
        if self.relay:
            first, passed, own = self._relay_copies(srcs, dsts, *sems)
            for j in range(3):
                for t in range(self.n):
                    first[t][1 + j].wait_recv()
                    passed[t][j].start()
            for t in range(self.n):
                first[t][0].wait_recv()
                for cp in passed[t]:
                    cp.wait_recv()
            for t in range(self.n):
                for cp in first[t] + passed[t]:
                    cp.wait_send()
                own[t].wait()
            return
        copies, own = self._copies(srcs, dsts, *sems)
        for cp in copies + own:
            cp.wait()


def _pcall(body, *, name, grid, in_specs, out_specs, out_shape, scratch_shapes, sem, args, comm=None):
    in_specs, out_specs, out_shape = list(in_specs), list(out_specs), list(out_shape)
    if comm is None:
        outs = pl.pallas_call(body, name=name, grid=grid, in_specs=in_specs, out_specs=out_specs, out_shape=out_shape,
                              scratch_shapes=list(scratch_shapes), compiler_params=_cparams(sem))(*args)
        return list(outs), []
    n_in, n_out, n_scr, k = len(in_specs), len(out_specs), len(scratch_shapes), comm.n

    def carrier(*refs):
        ins, cin = refs[:n_in], refs[n_in:n_in + k]
        outs, cout = refs[n_in + k:n_in + k + n_out], refs[n_in + k + n_out:n_in + 2 * k + n_out]
        scr, sems = refs[n_in + 2 * k + n_out:n_in + 2 * k + n_out + n_scr], refs[n_in + 2 * k + n_out + n_scr:]
        ids = [pl.program_id(d) for d in range(len(grid))]
        first = functools.reduce(jnp.logical_and, [i == 0 for i in ids])
        last = functools.reduce(jnp.logical_and, [i == g - 1 for i, g in zip(ids, grid)])

        @pl.when(first)
        def _():
            comm.start(cin, cout, *sems)

        body(*ins, *outs, *scr)

        @pl.when(last)
        def _():
            comm.finish(cin, cout, *sems)

    hbm = pl.BlockSpec(memory_space=pl.ANY)
    outs = pl.pallas_call(
        carrier, name=name, grid=grid, in_specs=in_specs + [hbm] * k, out_specs=out_specs + [hbm] * k,
        out_shape=out_shape + comm.out_shape, scratch_shapes=list(scratch_shapes) + comm.scratch,
        compiler_params=_cparams(tuple("arbitrary" for _ in grid)),
    )(*args, *comm.arrays)
    return list(outs[:n_out]), list(outs[n_out:])


def _exchange(transfers, name, relay=False):
    comm = _Comm(transfers, relay)

    def body(*refs):
        k = comm.n
        comm.start(refs[:k], refs[k:2 * k], *refs[2 * k:])
        comm.finish(refs[:k], refs[k:2 * k], *refs[2 * k:])

    hbm = pl.BlockSpec(memory_space=pl.ANY)
    return pl.pallas_call(body, name=name, out_shape=comm.out_shape, in_specs=[hbm] * comm.n, out_specs=[hbm] * comm.n,
                          scratch_shapes=comm.scratch)(*comm.arrays)


def _matmul(a, b, mode, *, name, tm, tn, tk, epi=None, extra=(), out_dtype=F32, chunks=None, comm=None):
    if mode == "nn":
        (M, K), N = a.shape, b.shape[1]
    elif mode == "nt":
        (M, K), N = a.shape, b.shape[0]
    else:
        (K, M), N = a.shape, b.shape[1]
    tm, tn, tk = min(tm, M), min(tn, N), min(tk, K)
    assert M % tm == 0 and N % tn == 0 and K % tk == 0 and (epi != "ln" or tn == N), (name, M, N, K)
    nk = K // tk
    a_spec = {"nn": pl.BlockSpec((tm, tk), lambda i, j, k: (i, k)),
              "nt": pl.BlockSpec((tm, tk), lambda i, j, k: (i, k)),
              "tn": pl.BlockSpec((tk, tm), lambda i, j, k: (k, i))}[mode]
    b_spec = {"nn": pl.BlockSpec((tk, tn), lambda i, j, k: (k, j)),
              "nt": pl.BlockSpec((tn, tk), lambda i, j, k: (j, k)),
              "tn": pl.BlockSpec((tk, tn), lambda i, j, k: (k, j))}[mode]
    ca, cb = {"nn": (1, 0), "nt": (1, 1), "tn": (0, 0)}[mode]
    tile = pl.BlockSpec((tm, tn), lambda i, j, k: (i, j))
    row = pl.BlockSpec((1, tn), lambda i, j, k: (0, j))
    n_extra = {None: 0, "add": 1, "relu2": 0, "drelu2": 1, "ln": 3}[epi]
    assert len(extra) == n_extra
    extra_specs = {None: [], "add": [tile], "relu2": [], "drelu2": [tile], "ln": [tile, row, row]}[epi]
    split = 0
    if epi == "relu2":
        out_shape, out_specs = (jax.ShapeDtypeStruct((M, N), BF16),), (tile,)
    elif epi == "ln":
        out_shape = (jax.ShapeDtypeStruct((M, N), F32), jax.ShapeDtypeStruct((M, N), F32),
                     jax.ShapeDtypeStruct((M, N), BF16))
        out_specs = (tile, tile, tile)
    elif chunks == "cols":
        c = N // N_DEV
        out_shape = (jax.ShapeDtypeStruct((N_DEV, M, c), out_dtype),)
        if tn == N:
            split = c
            out_specs = (pl.BlockSpec((N_DEV, tm, c), lambda i, j, k: (0, i, 0)),)
        else:
            assert c % tn == 0
            out_specs = (pl.BlockSpec((None, tm, tn), lambda i, j, k: (j // (c // tn), i, j % (c // tn))),)
    else:
        out_shape, out_specs = (jax.ShapeDtypeStruct((M, N), out_dtype),), (tile,)
    n_out = len(out_shape)

    def body(*refs):
        a_ref, b_ref = refs[:2]
        ex = refs[2:2 + n_extra]
        outs = refs[2 + n_extra:2 + n_extra + n_out]
        acc_ref = refs[-1]
        k = pl.program_id(2)
        part = _dg(_bf(a_ref[...]), _bf(b_ref[...]), ca, cb)

        def finish(acc):
            if epi == "add":
                outs[0][...] = (acc + ALPHA * ex[0][...]).astype(out_dtype)
            elif epi == "relu2":
                r = jnp.maximum(acc, 0.0)
                outs[0][...] = _bf(r * r)
            elif epi == "drelu2":
                outs[0][...] = (acc * (2.0 * jnp.sqrt(ex[0][...].astype(F32)))).astype(out_dtype)
            elif epi == "ln":
                u = ALPHA * ex[0][...] + acc
                xh, _ = _norm_stats(u)
                y = xh * ex[1][...] + ex[2][...]
                outs[0][...] = u
                outs[1][...] = y
                outs[2][...] = _bf(y)
            elif split:
                for p in range(N_DEV):
                    outs[0][p] = acc[:, p * split:(p + 1) * split].astype(out_dtype)
            else:
                outs[0][...] = acc.astype(out_dtype)

        if nk == 1:
            finish(part)
        else:
            @pl.when(k == 0)
            def _():
                acc_ref[...] = part

            @pl.when(jnp.logical_and(k > 0, k < nk - 1))
            def _():
                acc_ref[...] += part

            @pl.when(k == nk - 1)
            def _():
                finish(acc_ref[...] + part)

    outs, landed = _pcall(
        body, name=name, out_shape=out_shape, grid=(M // tm, N // tn, nk),
        in_specs=[a_spec, b_spec] + extra_specs, out_specs=out_specs,
        scratch_shapes=[pltpu.VMEM((tm, tn) if nk > 1 else (8, 128), F32)], sem=("parallel", "parallel", "arbitrary"),
        args=(a, b, *extra), comm=comm)
    res = outs[0] if n_out == 1 else tuple(outs)
    if chunks == "rows":
        res = res.reshape(N_DEV, M // N_DEV, N)
    return res if comm is None else (res, landed)


def _ret_tables(S):
    half = 64
    inv_freq = ROPE_BASE ** (-jnp.arange(half, dtype=F32) / half)
    ang = jnp.arange(S, dtype=jnp.int32).astype(F32)[:, None] * inv_freq[None, :]
    cos, sin = jnp.cos(ang), jnp.sin(ang)
    cosf = jnp.concatenate([cos, cos], axis=1)
    sinf = jnp.concatenate([-sin, sin], axis=1)
    log_g = jnp.log(1.0 - 2.0 ** (-5.0 - jnp.arange(4, dtype=F32)))
    idx = jnp.arange(CHUNK, dtype=F32)
    diff = idx[:, None] - idx[None, :]
    md = jnp.where(diff[None] >= 0, jnp.exp(log_g[:, None, None] * diff[None]), 0.0)
    kd = jnp.exp(log_g[:, None] * (CHUNK - 1 - idx)[None, :])
    qd = jnp.exp(log_g[:, None] * (idx + 1.0)[None, :])
    cd = jnp.exp(log_g * CHUNK)
    bc = lambda t: jnp.broadcast_to(t[:, :, None], (4, CHUNK, CHUNK))
    return cosf, sinf, md, bc(qd), bc(kd), jnp.broadcast_to(cd[:, None, None], (4, 8, CHUNK))


def _rot(x, cosf, sinf):
    return x * cosf + pltpu.roll(x, 64, 1) * sinf


def _rot_t(dx, cosf, sinf):
    return dx * cosf - pltpu.roll(dx, 64, 1) * sinf


def _ret_specs(rev, N):
    rn = (lambda n: N - 1 - n) if rev else (lambda n: n)
    col = lambda c: pl.BlockSpec((CHUNK, 512), lambda n, c=c: (rn(n), c))
    tab = pl.BlockSpec((CHUNK, CHUNK), lambda n: (rn(n), 0))
    dec = pl.BlockSpec((4, CHUNK, CHUNK), lambda n: (0, 0, 0))
    cdec = pl.BlockSpec((4, 8, CHUNK), lambda n: (0, 0, 0))
    vec = pl.BlockSpec((1, 512), lambda n: (0, 0))
    st = pl.BlockSpec((1, 4, CHUNK, CHUNK), lambda n: (rn(n), 0, 0, 0))
    return col, tab, dec, cdec, vec, st


def _ret_fwd(proj, tables, gn_g, gn_b):
    S = proj.shape[0]
    N = S // CHUNK
    col, tab, dec, cdec, vec, st = _ret_specs(False, N)

    def body(q_ref, k_ref, v_ref, g_ref, cos_ref, sin_ref, md_ref, qd_ref, kd_ref, cd_ref, gng_ref, gnb_ref,
             out_ref, st_ref, state):
        @pl.when(pl.program_id(0) == 0)
        def _():
            state[...] = jnp.zeros_like(state)

        cosf, sinf = cos_ref[...], sin_ref[...]
        for h in range(4):
            sl = slice(h * 128, (h + 1) * 128)
            qr = _rot(q_ref[:, sl], cosf, sinf)
            kr = _rot(k_ref[:, sl], cosf, sinf) * (128 ** -0.5)
            vb = _bf(v_ref[:, sl])
            s0 = state[h]
            st_ref[0, h] = s0
            sc = _dg(_bf(qr), _bf(kr), 1, 1) * md_ref[h]
            r = _dg(_bf(sc), vb, 1, 0) + _dg(_bf(qr * qd_ref[h]), _bf(s0), 1, 0)
            state[h] = s0 * cd_ref[h, 0:1, :] + _dg(_bf(kr * kd_ref[h]), vb, 0, 0)
            y, _ = _norm_stats(r)
            rg = g_ref[:, sl]
            out_ref[:, sl] = rg * _sigmoid(rg) * (y * gng_ref[:, sl] + gnb_ref[:, sl])

    return pl.pallas_call(
        body, name="ret_fwd", grid=(N,),
        out_shape=(jax.ShapeDtypeStruct((S, RET_W), F32), jax.ShapeDtypeStruct((N, 4, CHUNK, CHUNK), F32)),
        in_specs=[col(0), col(1), col(2), col(3), tab, tab, dec, dec, dec, cdec, vec, vec],
        out_specs=(pl.BlockSpec((CHUNK, 512), lambda n: (n, 0)), st),
        scratch_shapes=[pltpu.VMEM((4, CHUNK, CHUNK), F32)],
        compiler_params=_cparams(("arbitrary",)),
    )(proj, proj, proj, proj, *tables, gn_g, gn_b)


def _ret_bwd(proj, tables, gn_g, gn_b, states, d_out):
    S = proj.shape[0]
    N = S // CHUNK
    col, tab, dec, cdec, vec, st = _ret_specs(True, N)

    def kernel_body(q_ref, k_ref, v_ref, g_ref, cos_ref, sin_ref, md_ref, qd_ref, kd_ref, cd_ref, gng_ref, gnb_ref,
                    st_ref, do_ref, dp_ref, dg_ref, db_ref, gstate):
        @pl.when(pl.program_id(0) == 0)
        def _():
            gstate[...] = jnp.zeros_like(gstate)
            dg_ref[...] = jnp.zeros_like(dg_ref)
            db_ref[...] = jnp.zeros_like(db_ref)

        cosf, sinf = cos_ref[...], sin_ref[...]
        for h in range(4):
            sl = slice(h * 128, (h + 1) * 128)
            qr = _rot(q_ref[:, sl], cosf, sinf)
            kr = _rot(k_ref[:, sl], cosf, sinf) * (128 ** -0.5)
            qb, kb, vb = _bf(qr), _bf(kr), _bf(v_ref[:, sl])
            s0b = _bf(st_ref[0, h])
            md, qd, kd = md_ref[h], qd_ref[h], kd_ref[h]
            scb = _bf(_dg(qb, kb, 1, 1) * md)
            qdb = _bf(qr * qd)
            kdb = _bf(kr * kd)
            r = _dg(scb, vb, 1, 0) + _dg(qdb, s0b, 1, 0)
            y, rstd = _norm_stats(r)
            gng = gng_ref[:, sl]
            gn = y * gng + gnb_ref[:, sl]
            rg = g_ref[:, sl]
            sg = _sigmoid(rg)
            d_o = do_ref[:, sl]
            d_gn = d_o * (rg * sg)
            dg_ref[:, sl] += jnp.sum(d_gn * y, axis=0, keepdims=True)
            db_ref[:, sl] += jnp.sum(d_gn, axis=0, keepdims=True)
            drb = _bf(_norm_bwd(d_gn * gng, y, rstd))
            g0 = gstate[h]
            gb = _bf(g0)
            dscb = _bf(_dg(drb, vb, 1, 1) * md)
            dqr = _dg(dscb, kb, 1, 0) + _dg(drb, s0b, 1, 1) * qd
            dkr = _dg(dscb, qb, 0, 0) + _dg(vb, gb, 1, 1) * kd
            dv = _dg(scb, drb, 0, 0) + _dg(kdb, gb, 1, 0)
            gstate[h] = g0 * cd_ref[h, 0:1, :] + _dg(qdb, drb, 0, 0)
            dp_ref[:, 0 * 512 + h * 128:0 * 512 + (h + 1) * 128] = _bf(_rot_t(dqr, cosf, sinf))
            dp_ref[:, 1 * 512 + h * 128:1 * 512 + (h + 1) * 128] = _bf(_rot_t(dkr, cosf, sinf) * (128 ** -0.5))
            dp_ref[:, 2 * 512 + h * 128:2 * 512 + (h + 1) * 128] = _bf(dv)
            dp_ref[:, 3 * 512 + h * 128:3 * 512 + (h + 1) * 128] = _bf(d_o * gn * (sg * (1.0 + rg * (1.0 - sg))))

    acc = pl.BlockSpec((1, 512), lambda n: (0, 0))
    return pl.pallas_call(
        kernel_body, name="ret_bwd", grid=(N,),
        out_shape=(jax.ShapeDtypeStruct((S, 2048), BF16), jax.ShapeDtypeStruct((1, 512), F32),
                   jax.ShapeDtypeStruct((1, 512), F32)),
        in_specs=[col(0), col(1), col(2), col(3), tab, tab, dec, dec, dec, cdec, vec, vec, st,
                  pl.BlockSpec((CHUNK, 512), lambda n: (N - 1 - n, 0))],
        out_specs=(pl.BlockSpec((CHUNK, 2048), lambda n: (N - 1 - n, 0)), acc, acc),
        scratch_shapes=[pltpu.VMEM((4, CHUNK, CHUNK), F32)],
        compiler_params=_cparams(("arbitrary",)),
    )(proj, proj, proj, proj, *tables, gn_g, gn_b, states, d_out)


SB_T = 256
SB_SCALE = 64 ** -0.5
SB_Q_COL, SB_K_COL, SB_V_COL = 2048 // 128, 2560 // 128, 3072 // 128


def _head_masks():
    lane = lax.broadcasted_iota(jnp.int32, (1, 128), 1)
    m0 = (lane < 64).astype(F32)
    return m0, 1.0 - m0


def _tri(n, cmp):
    r = lax.broadcasted_iota(jnp.int32, (n, n), 0)
    c = lax.broadcasted_iota(jnp.int32, (n, n), 1)
    return cmp(r, c)


def _tri_sum(x, tri):
    hi = _bf(x)
    lo = _bf(x - hi.astype(F32))
    return _dg(hi, tri, 1, 0) + _dg(lo, tri, 1, 0)


def _sb_weights(qms, kblks, upper, carry, causal):
    tiles = [(b, h) for b in range(len(kblks)) for h in range(2)]
    zs = [_dg(qms[h], kblks[b], 1, 1) for b, h in tiles]
    lgs = [-(jnp.maximum(z, 0.0) + jnp.log(1.0 + jnp.exp(-jnp.abs(z)))) for z in zs]
    if causal is not None:
        lgs = [jnp.where(causal, lg, 0.0) for lg in lgs]
    carries = list(carry)
    for t in range(len(tiles) - 2):
        carries.append(carries[t] + jnp.sum(lgs[t], axis=1, keepdims=True))
    his = [_bf(lg) for lg in lgs]
    los = [_bf(lg - hi.astype(F32)) for lg, hi in zip(lgs, his)]
    later = [_dg(hi, upper, 1, 0) for hi in his]
    later = [r + _dg(lo, upper, 1, 0) for r, lo in zip(later, los)]
    a = [jnp.exp(lg + z + (r + c)) for lg, z, r, c in zip(lgs, zs, later, carries)]
    if causal is not None:
        a = [jnp.where(causal, x, 0.0) for x in a]
    out = tuple(carries[t] + jnp.sum(lgs[t], axis=1, keepdims=True) for t in (len(tiles) - 2, len(tiles) - 1))
    return [a[2 * b:2 * b + 2] for b in range(len(kblks))], out


def _sb_fwd(proj, comm=None):
    S = proj.shape[0]
    T = min(SB_T, S)
    nq = S // T

    def body(q_ref, k_ref, v_ref, o_ref, kb_ref, vm_ref, acc_ref):
        i = pl.program_id(1)
        m0, m1 = _head_masks()

        @pl.when(i == 0)
        def _():
            v = v_ref[...]
            kb_ref[...] = _bf(k_ref[...])
            vm_ref[0] = _bf(v * m0)
            vm_ref[1] = _bf(v * m1)

        q = q_ref[...]
        qm = (_bf(q * (m0 * SB_SCALE)), _bf(q * (m1 * SB_SCALE)))
        upper = _tri(T, lambda r, c: r > c).astype(BF16)
        causal = _tri(T, lambda r, c: c < r)

        def tiles(js, carry, mask, first):
            ks = [pl.multiple_of(j * T, T) for j in js]
            a, out = _sb_weights(qm, [kb_ref[pl.ds(k, T), :] for k in ks], upper, carry, mask)
            parts = [_dg(_bf(a[b][h]), vm_ref[h, pl.ds(k, T), :], 1, 0) for b, k in enumerate(ks) for h in range(2)]
            part = functools.reduce(lambda u, w: u + w, parts)
            if first:
                acc_ref[...] = part
            else:
                acc_ref[...] += part
            return out

        zero = jnp.zeros((T, 1), F32)
        carry = tiles([i], (zero, zero), causal, True)
        carry = lax.fori_loop(0, i % 2, lambda _, c: tiles([i - 1], c, None, False), carry)
        top = i - 1 - i % 2
        lax.fori_loop(0, i // 2, lambda jj, c: tiles([top - 2 * jj, top - 2 * jj - 1], c, None, False), carry)
        o_ref[...] = acc_ref[...]

    full = lambda c: pl.BlockSpec((S, 128), lambda p, i, c=c: (0, c + p))
    outs, landed = _pcall(
        body, name="sb_fwd", grid=(4, nq), out_shape=[jax.ShapeDtypeStruct((S, SB_W), F32)],
        in_specs=[pl.BlockSpec((T, 128), lambda p, i: (i, SB_Q_COL + p)), full(SB_K_COL), full(SB_V_COL)],
        out_specs=[pl.BlockSpec((T, 128), lambda p, i: (i, p))],
        scratch_shapes=[pltpu.VMEM((S, 128), BF16), pltpu.VMEM((2, S, 128), BF16), pltpu.VMEM((T, 128), F32)],
        sem=("arbitrary", "arbitrary"), args=(proj, proj, proj), comm=comm)
    return outs[0] if comm is None else (outs[0], landed)


def _sb_bwd(proj, d_o, comm=None):
    S = proj.shape[0]
    T = min(SB_T, S)
    nq = S // T

    def body(q_ref, k_ref, v_ref, do_ref, dq_ref, dk_ref, dv_ref, kb_ref, kbm_ref, vb_ref, e_ref, dq_acc, dk_acc, dv_acc):
        i = pl.program_id(1)
        m0, m1 = _head_masks()

        @pl.when(i == 0)
        def _():
            k = k_ref[...]
            kb_ref[...] = _bf(k)
            kbm_ref[0] = _bf(k * m0)
            kbm_ref[1] = _bf(k * m1)
            vb_ref[...] = _bf(v_ref[...])
            dk_acc[...] = jnp.zeros_like(dk_acc)
            dv_acc[...] = jnp.zeros_like(dv_acc)

        q, d_out = q_ref[...], do_ref[...]
        qm = (_bf(q * (m0 * SB_SCALE)), _bf(q * (m1 * SB_SCALE)))
        dom = (_bf(d_out * m0), _bf(d_out * m1))
        upper = _tri(T, lambda r, c: r > c).astype(BF16)
        lower = _tri(T, lambda r, c: r < c).astype(BF16)
        causal = _tri(T, lambda r, c: c < r)

        def down(js, carry, mask):
            ks = [pl.multiple_of(j * T, T) for j in js]
            a, out = _sb_weights(qm, [kb_ref[pl.ds(k, T), :] for k in ks], upper, carry, mask)
            da = [[_dg(dom[h], vb_ref[pl.ds(k, T), :], 1, 1) for h in range(2)] for k in ks]
            for b, (j, k) in enumerate(zip(js, ks)):
                for h in range(2):
                    e_ref[h, j] = a[b][h] * da[b][h]
                dv_acc[pl.ds(k, T), :] += _dg(_bf(a[b][0]), dom[0], 0, 0) + _dg(_bf(a[b][1]), dom[1], 0, 0)
            return out

        def up(js, carry, mask):
            ks = [pl.multiple_of(j * T, T) for j in js]
            tiles = [(b, h) for b in range(len(js)) for h in range(2)]
            zs = [_dg(qm[h], kb_ref[pl.ds(ks[b], T), :], 1, 1) for b, h in tiles]
            es = [e_ref[h, js[b]] for b, h in tiles]
            carries = list(carry)
            for t in range(len(tiles)):
                carries.append(carries[t] + jnp.sum(es[t], axis=1, keepdims=True))
            his = [_bf(e) for e in es]
            los = [_bf(e - hi.astype(F32)) for e, hi in zip(es, his)]
            d_lg = [_dg(hi, lower, 1, 0) for hi in his]
            d_lg = [r + _dg(lo, lower, 1, 0) + c for r, lo, c in zip(d_lg, los, carries)]
            ens = [jnp.exp(-jnp.abs(z)) for z in zs]
            invs = [1.0 / (1.0 + en) for en in ens]
            betas = [jnp.where(z >= 0.0, inv, en * inv) for z, en, inv in zip(zs, ens, invs)]
            dzs = [e * (1.0 - b) - d * b for e, b, d in zip(es, betas, d_lg)]
            if mask is not None:
                dzs = [jnp.where(mask, dz, 0.0) for dz in dzs]
            dzs = [_bf(dz) for dz in dzs]
            parts = [_dg(dzs[t], kbm_ref[h, pl.ds(ks[b], T), :], 1, 0) for t, (b, h) in enumerate(tiles)]
            dq_acc[...] += functools.reduce(lambda u, w: u + w, parts)
            for b, k in enumerate(ks):
                dk_acc[pl.ds(k, T), :] += _dg(dzs[2 * b], qm[0], 0, 0) + _dg(dzs[2 * b + 1], qm[1], 0, 0)
            return tuple(carries[-2:])

        zero = jnp.zeros((T, 1), F32)
        carry = down([i], (zero, zero), causal)
        carry = lax.fori_loop(0, i % 2, lambda _, c: down([i - 1], c, None), carry)
        top = i - 1 - i % 2
        lax.fori_loop(0, i // 2, lambda jj, c: down([top - 2 * jj, top - 2 * jj - 1], c, None), carry)

        dq_acc[...] = jnp.zeros_like(dq_acc)
        carry = lax.fori_loop(0, i // 2, lambda jj, c: up([2 * jj, 2 * jj + 1], c, None), (zero, zero))
        carry = lax.fori_loop(0, i % 2, lambda _, c: up([i - 1], c, None), carry)
        up([i], carry, causal)
        dq_ref[...] = _bf(dq_acc[...] * SB_SCALE)

        @pl.when(i == nq - 1)
        def _():
            dk_ref[...] = _bf(dk_acc[...])
            dv_ref[...] = _bf(dv_acc[...])

    full = lambda c: pl.BlockSpec((S, 128), lambda p, i, c=c: (0, c + p))
    tile = pl.BlockSpec((T, 128), lambda p, i: (i, p))
    acc = pl.BlockSpec((S, 128), lambda p, i: (0, p))
    out = jax.ShapeDtypeStruct((S, SB_W), BF16)
    outs, landed = _pcall(
        body, name="sb_bwd", grid=(4, nq), out_shape=[out, out, out],
        in_specs=[pl.BlockSpec((T, 128), lambda p, i: (i, SB_Q_COL + p)), full(SB_K_COL), full(SB_V_COL), tile],
        out_specs=[tile, acc, acc],
        scratch_shapes=[pltpu.VMEM((S, 128), BF16), pltpu.VMEM((2, S, 128), BF16), pltpu.VMEM((S, 128), BF16),
                        pltpu.VMEM((2, nq, T, T), F32), pltpu.VMEM((T, 128), F32), pltpu.VMEM((S, 128), F32),
                        pltpu.VMEM((S, 128), F32)],
        sem=("arbitrary", "arbitrary"), args=(proj, proj, proj, d_o), comm=comm)
    return tuple(outs) if comm is None else (tuple(outs), landed)


SGU_U_COL, SGU_V_COL = 3584 // 512, 4096 // 512


def _causal(w):
    r = lax.broadcasted_iota(jnp.int32, (CHUNK, CHUNK), 0)
    c = lax.broadcasted_iota(jnp.int32, (CHUNK, CHUNK), 1)
    return jnp.where(r >= c, w, 0.0)


def _sgu_fwd(proj, ln_g, ln_b, w, b):
    S = proj.shape[0]
    N = S // CHUNK

    def body(u_ref, v_ref, g_ref, b_ref, w_ref, bias_ref, out_ref):
        u = _gelu(u_ref[...])
        xh, _ = _norm_stats(_gelu(v_ref[...]))
        vn = _bf(xh * g_ref[...] + b_ref[...])
        for g in range(4):
            sl = slice(g * 128, (g + 1) * 128)
            sv = _dg(_bf(_causal(w_ref[g])), vn[:, sl], 1, 0) + bias_ref[g]
            out_ref[:, sl] = u[:, sl] * sv

    vec = pl.BlockSpec((1, 512), lambda n: (0, 0))
    return pl.pallas_call(
        body, name="sgu_fwd", grid=(N,),
        out_shape=jax.ShapeDtypeStruct((S, SGU_W), F32),
        in_specs=[pl.BlockSpec((CHUNK, 512), lambda n: (n, SGU_U_COL)),
                  pl.BlockSpec((CHUNK, 512), lambda n: (n, SGU_V_COL)), vec, vec,
                  pl.BlockSpec((4, CHUNK, CHUNK), lambda n: (0, 0, 0)), pl.BlockSpec((4, CHUNK, 1), lambda n: (0, 0, 0))],
        out_specs=pl.BlockSpec((CHUNK, 512), lambda n: (n, 0)),
        compiler_params=_cparams(("parallel",)),
    )(proj, proj, ln_g, ln_b, w, b)


def _sgu_bwd(proj, ln_g, ln_b, w, b, d_out):
    S = proj.shape[0]
    N = S // CHUNK

    def body(u_ref, v_ref, g_ref, b_ref, w_ref, bias_ref, do_ref, dp_ref, dg_ref, db_ref, dw_ref, dbias_ref):
        @pl.when(pl.program_id(0) == 0)
        def _():
            dg_ref[...] = jnp.zeros_like(dg_ref)
            db_ref[...] = jnp.zeros_like(db_ref)
            dw_ref[...] = jnp.zeros_like(dw_ref)
            dbias_ref[...] = jnp.zeros_like(dbias_ref)

        gu, gv = u_ref[...], v_ref[...]
        u = _gelu(gu)
        xh, rstd = _norm_stats(_gelu(gv))
        ln_gain = g_ref[...]
        vn = _bf(xh * ln_gain + b_ref[...])
        d_o = do_ref[...]
        d_vn = []
        for g in range(4):
            sl = slice(g * 128, (g + 1) * 128)
            wc = _bf(_causal(w_ref[g]))
            sv = _dg(wc, vn[:, sl], 1, 0) + bias_ref[g]
            dp_ref[:, sl] = _bf(d_o[:, sl] * sv * _gelu_grad(gu[:, sl]))
            d_sv = d_o[:, sl] * u[:, sl]
            dbias_ref[g] += jnp.sum(d_sv, axis=1, keepdims=True)
            d_svb = _bf(d_sv)
            dw_ref[g] += _causal(_dg(d_svb, vn[:, sl], 1, 1))
            d_vn.append(_dg(wc, d_svb, 0, 0))
        d_vn = jnp.concatenate(d_vn, axis=1)
        dg_ref[...] += jnp.sum(d_vn * xh, axis=0, keepdims=True)
        db_ref[...] += jnp.sum(d_vn, axis=0, keepdims=True)
        dp_ref[:, 512:1024] = _bf(_norm_bwd(d_vn * ln_gain, xh, rstd) * _gelu_grad(gv))

    vec = pl.BlockSpec((1, 512), lambda n: (0, 0))
    wspec = pl.BlockSpec((4, CHUNK, CHUNK), lambda n: (0, 0, 0))
    bspec = pl.BlockSpec((4, CHUNK, 1), lambda n: (0, 0, 0))
    return pl.pallas_call(
        body, name="sgu_bwd", grid=(N,),
        out_shape=(jax.ShapeDtypeStruct((S, 1024), BF16), jax.ShapeDtypeStruct((1, 512), F32),
                   jax.ShapeDtypeStruct((1, 512), F32), jax.ShapeDtypeStruct((4, CHUNK, CHUNK), F32),
                   jax.ShapeDtypeStruct((4, CHUNK, 1), F32)),
        in_specs=[pl.BlockSpec((CHUNK, 512), lambda n: (n, SGU_U_COL)),
                  pl.BlockSpec((CHUNK, 512), lambda n: (n, SGU_V_COL)), vec, vec, wspec, bspec,
                  pl.BlockSpec((CHUNK, 512), lambda n: (n, 0))],
        out_specs=(pl.BlockSpec((CHUNK, 1024), lambda n: (n, 0)), vec, vec, wspec, bspec),
        compiler_params=_cparams(("arbitrary",)),
    )(proj, proj, ln_g, ln_b, w, b, d_out)


GATE_COL = 4608 // 512


def _merge_fwd(proj, branches, p_list, tm=512):
    S = proj.shape[0]
    tm = min(tm, S)

    def body(r_ref, s_ref, g_ref, pr_ref, ps_ref, pg_ref, gr_ref, gs_ref, gg_ref, m_ref, br_ref):
        acc = None
        for k, (x_ref, p_ref, gate_ref) in enumerate(((r_ref, pr_ref, gr_ref), (s_ref, ps_ref, gs_ref),
                                                      (g_ref, pg_ref, gg_ref))):
            br = _dg(_bf(x_ref[...]), _bf(p_ref[...]), 1, 0)
            br_ref[k] = br
            term = _sigmoid(gate_ref[...]) * br
            acc = term if acc is None else acc + term
        m_ref[...] = _bf(acc)

    xs = pl.BlockSpec((tm, 512), lambda i, n: (i, 0))
    ps = pl.BlockSpec((512, 512), lambda i, n: (0, n))
    gate = lambda k: pl.BlockSpec((tm, 512), lambda i, n, k=k: (i, GATE_COL + 2 * k + n))
    return pl.pallas_call(
        body, name="merge_fwd", grid=(S // tm, 2),
        out_shape=(jax.ShapeDtypeStruct((S, D_MODEL), BF16), jax.ShapeDtypeStruct((3, S, D_MODEL), F32)),
        in_specs=[xs, xs, xs, ps, ps, ps, gate(0), gate(1), gate(2)],
        out_specs=(pl.BlockSpec((tm, 512), lambda i, n: (i, n)), pl.BlockSpec((3, tm, 512), lambda i, n: (0, i, n))),
        compiler_params=_cparams(("parallel", "parallel")),
    )(*branches, *p_list, proj, proj, proj)


def _gate_bwd(proj, br, d_merged, tm=512):
    S = proj.shape[0]
    tm = min(tm, S)

    def body(dm_ref, br_ref, gr_ref, gs_ref, gg_ref, dbr_ref, dgate_ref):
        dm = dm_ref[...]
        for k, gate_ref in enumerate((gr_ref, gs_ref, gg_ref)):
            s = _sigmoid(gate_ref[...])
            dbr_ref[k] = _bf(dm * s)
            dgate_ref[k] = _bf(dm * br_ref[k] * (s * (1.0 - s)))

    gate = lambda k: pl.BlockSpec((tm, 512), lambda i, n, k=k: (i, GATE_COL + 2 * k + n))
    three = pl.BlockSpec((3, tm, 512), lambda i, n: (0, i, n))
    return pl.pallas_call(
        body, name="gate_bwd", grid=(S // tm, 2),
        out_shape=(jax.ShapeDtypeStruct((3, S, D_MODEL), BF16), jax.ShapeDtypeStruct((3, S, D_MODEL), BF16)),
        in_specs=[pl.BlockSpec((tm, 512), lambda i, n: (i, n)), three, gate(0), gate(1), gate(2)],
        out_specs=(three, three),
        compiler_params=_cparams(("parallel", "parallel")),
    )(d_merged, br, proj, proj, proj)


def _ln_bwd(dy, u, g, tm=256):
    S, D = u.shape
    tm = min(tm, S)

    def body(dy_ref, u_ref, g_ref, du_ref, dub_ref, dg_ref, db_ref):
        @pl.when(pl.program_id(0) == 0)
        def _():
            dg_ref[...] = jnp.zeros_like(dg_ref)
            db_ref[...] = jnp.zeros_like(db_ref)

        dy_t = dy_ref[...]
        xh, rstd = _norm_stats(u_ref[...])
        dg_ref[...] += jnp.sum(dy_t * xh, axis=0, keepdims=True)
        db_ref[...] += jnp.sum(dy_t, axis=0, keepdims=True)
        du = _norm_bwd(dy_t * g_ref[...], xh, rstd)
        du_ref[...] = du
        dub_ref[...] = _bf(du)

    tile = pl.BlockSpec((tm, D), lambda i: (i, 0))
    vec = pl.BlockSpec((1, D), lambda i: (0, 0))
    return pl.pallas_call(
        body, name="ln_bwd", grid=(S // tm,),
        out_shape=(jax.ShapeDtypeStruct((S, D), F32), jax.ShapeDtypeStruct((S, D), BF16),
                   jax.ShapeDtypeStruct((1, D), F32), jax.ShapeDtypeStruct((1, D), F32)),
        in_specs=[tile, tile, vec], out_specs=(tile, tile, vec, vec),
        compiler_params=_cparams(("arbitrary",)),
    )(dy, u, g)


def _loss_grad(y, target, tm=256):
    S, D = y.shape
    tm = min(tm, S)

    def body(y_ref, t_ref, dy_ref, sq_ref):
        @pl.when(pl.program_id(0) == 0)
        def _():
            sq_ref[...] = jnp.zeros_like(sq_ref)

        err = y_ref[...] - t_ref[...]
        dy_ref[...] = err * (1.0 / D)
        sq_ref[...] += jnp.sum(err * err, axis=0, keepdims=True)

    tile = pl.BlockSpec((tm, D), lambda i: (i, 0))
    vec = pl.BlockSpec((1, D), lambda i: (0, 0))
    return pl.pallas_call(
        body, name="loss_grad", grid=(S // tm,),
        out_shape=(jax.ShapeDtypeStruct((S, D), F32), jax.ShapeDtypeStruct((1, D), F32)),
        in_specs=[tile, tile], out_specs=(tile, vec),
        compiler_params=_cparams(("arbitrary",)),
    )(y, target)


def _layer_fwd(x, x_bf, W, tables, sb_comm=None):
    proj = _matmul(x_bf, W["w_in"], "nn", name="proj", tm=1024, tn=768, tk=1024)
    retg, states = _ret_fwd(proj, tables, W["ret_gn_g"], W["ret_gn_b"])
    if sb_comm is None:
        sb = _sb_fwd(proj)
    else:
        sb, landed = _sb_fwd(proj, comm=sb_comm[0])
        sb_comm[1](landed)
    sg = _sgu_fwd(proj, W["sgu_ln_g"], W["sgu_ln_b"], W["sgu_w"], W["sgu_b"])
    merged, br = _merge_fwd(proj, (retg, sb, sg), (W["p_ret"], W["p_sb"], W["p_sgu"]))
    u1, x1, x1_bf = _matmul(merged, W["w_out"], "nn", name="out_ln", tm=512, tn=1024, tk=1024, epi="ln",
                            extra=(x, W["ln1_g"], W["ln1_b"]))
    act = _matmul(x1_bf, W["w_up"], "nn", name="up", tm=1024, tn=1024, tk=1024, epi="relu2")
    u2, x2, x2_bf = _matmul(act, W["w_down"], "nn", name="down_ln", tm=512, tn=1024, tk=1024, epi="ln",
                            extra=(x1, W["ln2_g"], W["ln2_b"]))
    saved = dict(x_bf=x_bf, proj=proj, retg=retg, states=states, sb=sb, sg=sg, merged=merged, br=br, u1=u1,
                 x1_bf=x1_bf, act=act, u2=u2)
    return x2, x2_bf, saved


def _layer_bwd(d_x2, W, tables, sv, chunk_dtype=None, sb_comm_fn=None, dx_comm_fn=None):
    dt = F32 if chunk_dtype is None else chunk_dtype
    rows, cols = (None, None) if chunk_dtype is None else ("rows", "cols")
    g, landed = {}, {}
    du2, du2_bf, g["ln2_g"], g["ln2_b"] = _ln_bwd(d_x2, sv["u2"], W["ln2_g"])
    d_hpre = _matmul(du2_bf, W["w_down"], "nt", name="d_act", tm=1024, tn=1024, tk=1024, epi="drelu2",
                     extra=(sv["act"],), out_dtype=BF16)
    g["w_down"] = _matmul(sv["act"], du2_bf, "tn", name="dw_down", tm=512, tn=1024, tk=4096, out_dtype=dt, chunks=rows)
    g["w_up"] = _matmul(sv["x1_bf"], d_hpre, "tn", name="dw_up", tm=1024, tn=512, tk=4096, out_dtype=dt, chunks=cols)
    d_x1 = _matmul(d_hpre, W["w_up"], "nt", name="d_x1", tm=1024, tn=1024, tk=1024, epi="add", extra=(du2,))
    du1, du1_bf, g["ln1_g"], g["ln1_b"] = _ln_bwd(d_x1, sv["u1"], W["ln1_g"])
    d_merged = _matmul(du1_bf, W["w_out"], "nt", name="d_merged", tm=1024, tn=1024, tk=1024)
    g["w_out"] = _matmul(sv["merged"], du1_bf, "tn", name="dw_out", tm=1024, tn=512, tk=4096, out_dtype=dt, chunks=rows)
    d_br, d_gate = _gate_bwd(sv["proj"], sv["br"], d_merged)
    d_branch = []
    for k, (nm, act) in enumerate((("p_ret", sv["retg"]), ("p_sb", sv["sb"]), ("p_sgu", sv["sg"]))):
        d_branch.append(_matmul(d_br[k], W[nm], "nt", name="d_" + nm[2:], tm=1024, tn=512, tk=1024))
        g[nm] = _matmul(act, d_br[k], "tn", name="dw_" + nm[2:], tm=512, tn=1024, tk=2048, out_dtype=dt, chunks=cols)
    d_ret, g["ret_gn_g"], g["ret_gn_b"] = _ret_bwd(sv["proj"], tables, W["ret_gn_g"], W["ret_gn_b"], sv["states"],
                                                   d_branch[0])
    if sb_comm_fn is None:
        d_sq, d_sk, d_sv = _sb_bwd(sv["proj"], d_branch[1])
    else:
        (d_sq, d_sk, d_sv), landed["sb"] = _sb_bwd(sv["proj"], d_branch[1], comm=sb_comm_fn(g))
    d_sgu, g["sgu_ln_g"], g["sgu_ln_b"], g["sgu_w"], g["sgu_b"] = _sgu_bwd(
        sv["proj"], W["sgu_ln_g"], W["sgu_ln_b"], W["sgu_w"], W["sgu_b"], d_branch[2])
    d_proj = jnp.concatenate([d_ret, d_sq, d_sk, d_sv, d_sgu, d_gate[0], d_gate[1], d_gate[2]], axis=1)
    g["w_in"] = _matmul(sv["x_bf"], d_proj, "tn", name="dw_in", tm=1024, tn=640, tk=4096, out_dtype=dt)
    if chunk_dtype is not None:
        g["w_in"] = g["w_in"].reshape(D_MODEL, N_DEV, N_IN // N_DEV).transpose(1, 0, 2)
    d_x = _matmul(d_proj, W["w_in"], "nt", name="d_x", tm=1024, tn=1024, tk=1920, epi="add", extra=(du1,),
                  comm=None if dx_comm_fn is None else dx_comm_fn(g))
    if dx_comm_fn is not None:
        d_x, landed["dx"] = d_x
    return d_x, g, landed


BIG = ("w_in", "p_ret", "p_sb", "p_sgu", "w_out", "w_up", "w_down")
SMALL = ("ret_gn_g", "ret_gn_b", "sgu_ln_g", "sgu_ln_b", "sgu_w", "sgu_b", "ln1_g", "ln1_b", "ln2_g", "ln2_b")
GATHER_KIND = {"w_in": "slab", "p_ret": "cols", "p_sb": "cols", "p_sgu": "cols", "w_out": "rows", "w_up": "cols",
               "w_down": "rows"}


def _small_weights(small, l):
    W = {}
    for n in SMALL:
        if n == "sgu_w":
            W[n] = small[n][l]
        elif n == "sgu_b":
            W[n] = small[n][l].reshape(4, CHUNK, 1)
        else:
            W[n] = small[n][l].reshape(1, -1)
    return W


def _local_step(x, target, full, small):
    tables = _ret_tables(x.shape[0])
    Ws = [{**{n: full[n][l] for n in BIG}, **_small_weights(small, l)} for l in range(DEPTH)]
    saved = []
    h, h_bf = x, _bf(x)
    for l in range(DEPTH):
        h, h_bf, sv = _layer_fwd(h, h_bf, Ws[l], tables)
        saved.append(sv)
    d_h, sq = _loss_grad(h, target)
    grads = [None] * DEPTH
    for l in reversed(range(DEPTH)):
        d_h, grads[l], _ = _layer_bwd(d_h, Ws[l], tables, saved[l])
    return sq, d_h, grads


def _adam(w, parts, m, v, name, tr=256):
    L, R, C = w.shape
    tr = min(tr, R)
    assert R % tr == 0 and len(parts) == L

    def body(*refs):
        w_ref, p_refs, (m_ref, v_ref, g_ref, d_ref, nm_ref, nv_ref) = refs[0], refs[1:1 + L], refs[1 + L:]
        layer = pl.program_id(0)
        g = None
        for li, p_ref in enumerate(p_refs):
            s = p_ref[0].astype(F32)
            for j in range(1, N_DEV):
                s = s + p_ref[j].astype(F32)
            g = s if g is None else jnp.where(layer == li, s, g)
        m2 = ADAM_B1 * m_ref[...] + (1.0 - ADAM_B1) * g
        v2 = ADAM_B2 * v_ref[...] + (1.0 - ADAM_B2) * (g * g)
        m_hat = m2 / (1.0 - ADAM_B1 ** ADAM_STEP)
        v_hat = v2 / (1.0 - ADAM_B2 ** ADAM_STEP)
        g_ref[...] = g
        d_ref[...] = -ADAM_LR * (m_hat / (jnp.sqrt(v_hat) + ADAM_EPS) + ADAM_WD * w_ref[...])
        nm_ref[...] = m2
        nv_ref[...] = v2

    tile = pl.BlockSpec((None, tr, C), lambda l, i: (l, i, 0))
    part = lambda li: pl.BlockSpec((N_DEV, tr, C), lambda l, i, li=li: (0, jnp.where(l == li, i, 0), 0))
    out = jax.ShapeDtypeStruct((L, R, C), F32)
    return pl.pallas_call(
        body, name=name, grid=(L, R // tr), out_shape=(out, out, out, out),
        in_specs=[tile] + [part(li) for li in range(L)] + [tile, tile],
        out_specs=(tile, tile, tile, tile),
        compiler_params=_cparams(("parallel", "parallel")),
    )(w, *parts, m, v)


def _pack_small(tree):
    return jnp.concatenate([tree[n].reshape(-1, 128) for n in SMALL], axis=0)


def _unpack_small(packed, like):
    out, r = {}, 0
    for n in SMALL:
        rows = like[n].size // 128
        out[n] = packed[r:r + rows].reshape(like[n].shape)
        r += rows
    return out


WEIGHTS = ("w_in", "ret_gn_g", "ret_gn_b", "sgu_ln_g", "sgu_ln_b", "sgu_w", "sgu_b", "p_ret", "p_sb", "p_sgu", "w_out",
           "ln1_g", "ln1_b", "w_up", "w_down", "ln2_g", "ln2_b")


def kernel(x, w_in, ret_gn_g, ret_gn_b, sgu_ln_g, sgu_ln_b, sgu_w, sgu_b, p_ret, p_sb, p_sgu, w_out, ln1_g, ln1_b, w_up, w_down, ln2_g, ln2_b, loss_target, m_w_in, m_ret_gn_g, m_ret_gn_b, m_sgu_ln_g, m_sgu_ln_b, m_sgu_w, m_sgu_b, m_p_ret, m_p_sb, m_p_sgu, m_w_out, m_ln1_g, m_ln1_b, m_w_up, m_w_down, m_ln2_g, m_ln2_b, v_w_in, v_ret_gn_g, v_ret_gn_b, v_sgu_ln_g, v_sgu_ln_b, v_sgu_w, v_sgu_b, v_p_ret, v_p_sb, v_p_sgu, v_w_out, v_ln1_g, v_ln1_b, v_w_up, v_w_down, v_ln2_g, v_ln2_b):
    w = dict(zip(WEIGHTS, (w_in, ret_gn_g, ret_gn_b, sgu_ln_g, sgu_ln_b, sgu_w, sgu_b, p_ret, p_sb, p_sgu, w_out,
                           ln1_g, ln1_b, w_up, w_down, ln2_g, ln2_b)))
    m = dict(zip(WEIGHTS, (m_w_in, m_ret_gn_g, m_ret_gn_b, m_sgu_ln_g, m_sgu_ln_b, m_sgu_w, m_sgu_b, m_p_ret, m_p_sb,
                           m_p_sgu, m_w_out, m_ln1_g, m_ln1_b, m_w_up, m_w_down, m_ln2_g, m_ln2_b)))
    v = dict(zip(WEIGHTS, (v_w_in, v_ret_gn_g, v_ret_gn_b, v_sgu_ln_g, v_sgu_ln_b, v_sgu_w, v_sgu_b, v_p_ret, v_p_sb,
                           v_p_sgu, v_w_out, v_ln1_g, v_ln1_b, v_w_up, v_w_down, v_ln2_g, v_ln2_b)))

    small = {n: w[n] for n in SMALL}
    shard = {n: _bf(w[n]) for n in BIG}
    tables = _ret_tables(x.shape[1])
    Ws = [_small_weights(small, l) for l in range(DEPTH)]
    unslab = lambda z: z.transpose(1, 0, 2).reshape(D_MODEL, N_IN)

    (slabs,) = _exchange([_gather_transfer(shard["w_in"], 0, "slab")], "gather_w_in0", relay=True)
    Ws[0]["w_in"] = unslab(slabs)
    later = [(n, 0) for n in BIG[1:]] + [(n, 1) for n in BIG]

    def weights_landed(landed):
        for (n, l), z in zip(later, landed):
            Ws[l][n] = unslab(z) if n == "w_in" else z

    gather = _Comm([_gather_transfer(shard[n], l, GATHER_KIND[n]) for n, l in later], relay=True)
    h, h_bf, saved0 = _layer_fwd(x[0], _bf(x[0]), Ws[0], tables, sb_comm=(gather, weights_landed))
    h, _, saved1 = _layer_fwd(h, h_bf, Ws[1], tables)
    d_h, sq = _loss_grad(h, loss_target[0])
    loss = lax.psum(0.5 * jnp.sum(sq) / D_MODEL, ("x", "y", "c"))

    d_h, g1, _ = _layer_bwd(d_h, Ws[1], tables, saved1, chunk_dtype=BF16)
    early = [(n, 1) for n in BIG] + [(n, 0) for n in BIG[1:]]

    def early_scatter(g0):
        return _Comm([_scatter_transfer((g1 if l else g0)[n]) for n, l in early])

    def late_scatter(g0):
        mine = _pack_small({n: jnp.stack([g0[n].reshape(small[n].shape[1:]), g1[n].reshape(small[n].shape[1:])])
                            for n in SMALL})
        return _Comm([_scatter_transfer(g0["w_in"]), _slab_transfer(mine)])

    d_x, g0, landed = _layer_bwd(d_h, Ws[0], tables, saved0, chunk_dtype=BF16, sb_comm_fn=early_scatter,
                                 dx_comm_fn=late_scatter)
    parts = dict(zip(early, landed["sb"]))
    parts[("w_in", 0)], small_parts = landed["dx"]

    grad, delta, new_m, new_v = {}, {}, {}, {}
    for n in BIG:
        grad[n], delta[n], new_m[n], new_v[n] = _adam(w[n], [parts[(n, l)] for l in range(DEPTH)], m[n], v[n],
                                                      "adam_" + n)
    res = _adam(_pack_small(small)[None], [small_parts], _pack_small({n: m[n] for n in SMALL})[None],
                _pack_small({n: v[n] for n in SMALL})[None], "adam_small", tr=small_parts.shape[1])
    for tree, packed in zip((grad, delta, new_m, new_v), res):
        tree.update(_unpack_small(packed[0], small))

    return (loss, d_x[None], *[grad[n] for n in WEIGHTS], *[delta[n] for n in WEIGHTS],
            *[new_m[n] for n in WEIGHTS], *[new_v[n] for n in WEIGHTS])
```

```python
import functools
import math

import numpy as np
import jax
import jax.numpy as jnp
from jax import lax
from jax.experimental import pallas as pl
from jax.experimental.pallas import tpu as pltpu

F32 = jnp.float32
BF16 = jnp.bfloat16

N_DEV = 8
DEPTH = 2
D_MODEL = 1024
CHUNK = 128
RET_W = 512
SB_W = 512
SGU_W = 512
N_IN = 7680
LN_EPS = 1e-5
ALPHA = (2 * DEPTH) ** 0.25
ROPE_BASE = 10000.0
ADAM_LR, ADAM_B1, ADAM_B2, ADAM_EPS, ADAM_WD, ADAM_STEP = 0.001, 0.9, 0.999, 1e-08, 0.01, 10
VMEM_LIMIT = 56 * 1024 * 1024

_GELU_K = math.sqrt(2.0 / math.pi)
_GELU_C = 0.044715


def _cparams(sem=None):
    return pltpu.CompilerParams(dimension_semantics=sem, vmem_limit_bytes=VMEM_LIMIT)


def _dg(a, b, ca, cb):
    return lax.dot_general(a, b, (((ca,), (cb,)), ((), ())), preferred_element_type=F32)


def _bf(x):
    return x.astype(BF16)


def _sigmoid(x):
    return 1.0 / (1.0 + jnp.exp(-x))


def _gelu(x):
    t = jnp.tanh(_GELU_K * (x + _GELU_C * (x * x * x)))
    return x * (0.5 * (1.0 + t))


def _gelu_grad(x):
    t = jnp.tanh(_GELU_K * (x + _GELU_C * (x * x * x)))
    return 0.5 * (1.0 + t) + 0.5 * x * (1.0 - t * t) * (_GELU_K * (1.0 + 3.0 * _GELU_C * x * x))


def _norm_stats(u):
    mu = jnp.mean(u, axis=-1, keepdims=True)
    d = u - mu
    var = jnp.mean(d * d, axis=-1, keepdims=True)
    rstd = lax.rsqrt(var + LN_EPS)
    return d * rstd, rstd


def _norm_bwd(dxh, xh, rstd):
    return rstd * (dxh - jnp.mean(dxh, axis=-1, keepdims=True) - xh * jnp.mean(dxh * xh, axis=-1, keepdims=True))


class _Transfer:
    def __init__(self, src, dst_shape, src_at, dst_at):
        self.src, self.dst_shape, self.src_at, self.dst_at = src, tuple(dst_shape), src_at, dst_at


def _gather_transfer(shard, l, kind):
    _, r, c = shard.shape
    src_at = lambda ref, p: ref.at[l]
    if kind == "slab":
        return _Transfer(shard, (N_DEV, r, c), src_at, lambda ref, s: ref.at[s])
    if kind == "rows":
        return _Transfer(shard, (N_DEV * r, c), src_at, lambda ref, s: ref.at[pl.ds(pl.multiple_of(s * r, r), r), :])
    return _Transfer(shard, (r, N_DEV * c), src_at, lambda ref, s: ref.at[:, pl.ds(pl.multiple_of(s * c, c), c)])


def _scatter_transfer(chunks):
    return _Transfer(chunks, chunks.shape, lambda ref, p: ref.at[p], lambda ref, s: ref.at[s])


def _slab_transfer(arr):
    return _Transfer(arr, (N_DEV,) + arr.shape, lambda ref, p: ref, lambda ref, s: ref.at[s])


class _Comm:
    def __init__(self, transfers, relay=False):
        self.transfers = list(transfers)
        self.relay = relay
        self.n = len(self.transfers)
        self.arrays = [t.src for t in self.transfers]
        self.out_shape = [jax.ShapeDtypeStruct(t.dst_shape, t.src.dtype) for t in self.transfers]
        self.scratch = [pltpu.SemaphoreType.DMA((self.n * (N_DEV - 1),)), pltpu.SemaphoreType.DMA((self.n * (N_DEV - 1),)),
                        pltpu.SemaphoreType.DMA((self.n,))]

    def _relay_copies(self, srcs, dsts, send_sems, recv_sems, local_sems):
        x, y, c = lax.axis_index("x"), lax.axis_index("y"), lax.axis_index("c")
        me = 4 * x + 2 * y + c
        chips = [(1 - x, y), (x, 1 - y), (1 - x, 1 - y)]
        first, passed, own = [], [], []
        for t, tr in enumerate(self.transfers):
            def copy(k, src, sender, to, t=t, tr=tr):
                return pltpu.make_async_remote_copy(
                    src_ref=src, dst_ref=tr.dst_at(dsts[t], sender), send_sem=send_sems.at[t * (N_DEV - 1) + k],
                    recv_sem=recv_sems.at[t * (N_DEV - 1) + k], device_id=to, device_id_type=pl.DeviceIdType.MESH)

            mine = tr.src_at(srcs[t], me)
            first.append([copy(0, mine, me, (x, y, 1 - c))] + [copy(1 + j, mine, me, (px, py, c))
                                                                for j, (px, py) in enumerate(chips)])
            passed.append([copy(4 + j, tr.dst_at(dsts[t], 4 * px + 2 * py + c), 4 * px + 2 * py + c, (x, y, 1 - c))
                           for j, (px, py) in enumerate(chips)])
            own.append(pltpu.make_async_copy(mine, tr.dst_at(dsts[t], me), local_sems.at[t]))
        return first, passed, own

    def _copies(self, srcs, dsts, send_sems, recv_sems, local_sems):
        x, y, c = lax.axis_index("x"), lax.axis_index("y"), lax.axis_index("c")
        me = 4 * x + 2 * y + c
        copies = []
        for d in range(1, N_DEV):
            px = 1 - x if d & 4 else x
            py = 1 - y if d & 2 else y
            pc = 1 - c if d & 1 else c
            peer = 4 * px + 2 * py + pc
            for t, tr in enumerate(self.transfers):
                k = t * (N_DEV - 1) + d - 1
                copies.append(pltpu.make_async_remote_copy(
                    src_ref=tr.src_at(srcs[t], peer), dst_ref=tr.dst_at(dsts[t], me),
                    send_sem=send_sems.at[k], recv_sem=recv_sems.at[k],
                    device_id=(px, py, pc), device_id_type=pl.DeviceIdType.MESH))
        own = [pltpu.make_async_copy(tr.src_at(srcs[t], me), tr.dst_at(dsts[t], me), local_sems.at[t])
               for t, tr in enumerate(self.transfers)]
        return copies, own

    def start(self, srcs, dsts, *sems):
        if self.relay:
            first, _, own = self._relay_copies(srcs, dsts, *sems)
            for cp in own + [cp for per_t in first for cp in per_t]:
                cp.start()
            return
        copies, own = self._copies(srcs, dsts, *sems)
        for cp in own + copies:
            cp.start()

    def finish(self, srcs, dsts, *sems):
        if self.relay:
            first, passed, own = self._relay_copies(srcs, dsts, *sems)
            for j in range(3):
                for t in range(self.n):
                    first[t][1 + j].wait_recv()
                    passed[t][j].start()
            for t in range(self.n):
                first[t][0].wait_recv()
                for cp in passed[t]:
                    cp.wait_recv()
            for t in range(self.n):
                for cp in first[t] + passed[t]:
                    cp.wait_send()
                own[t].wait()
            return
        copies, own = self._copies(srcs, dsts, *sems)
        for cp in copies + own:
            cp.wait()


def _pcall(body, *, name, grid, in_specs, out_specs, out_shape, scratch_shapes, sem, args, comm=None):
    in_specs, out_specs, out_shape = list(in_specs), list(out_specs), list(out_shape)
    if comm is None:
        outs = pl.pallas_call(body, name=name, grid=grid, in_specs=in_specs, out_specs=out_specs, out_shape=out_shape,
                              scratch_shapes=list(scratch_shapes), compiler_params=_cparams(sem))(*args)
        return list(outs), []
    n_in, n_out, n_scr, k = len(in_specs), len(out_specs), len(scratch_shapes), comm.n

    def carrier(*refs):
        ins, cin = refs[:n_in], refs[n_in:n_in + k]
        outs, cout = refs[n_in + k:n_in + k + n_out], refs[n_in + k + n_out:n_in + 2 * k + n_out]
        scr, sems = refs[n_in + 2 * k + n_out:n_in + 2 * k + n_out + n_scr], refs[n_in + 2 * k + n_out + n_scr:]
        ids = [pl.program_id(d) for d in range(len(grid))]
        first = functools.reduce(jnp.logical_and, [i == 0 for i in ids])
        last = functools.reduce(jnp.logical_and, [i == g - 1 for i, g in zip(ids, grid)])

        @pl.when(first)
        def _():
            comm.start(cin, cout, *sems)

        body(*ins, *outs, *scr)

        @pl.when(last)
        def _():
            comm.finish(cin, cout, *sems)

    hbm = pl.BlockSpec(memory_space=pl.ANY)
    outs = pl.pallas_call(
        carrier, name=name, grid=grid, in_specs=in_specs + [hbm] * k, out_specs=out_specs + [hbm] * k,
        out_shape=out_shape + comm.out_shape, scratch_shapes=list(scratch_shapes) + comm.scratch,
        compiler_params=_cparams(tuple("arbitrary" for _ in grid)),
    )(*args, *comm.arrays)
    return list(outs[:n_out]), list(outs[n_out:])


def _exchange(transfers, name, relay=False):
    comm = _Comm(transfers, relay)

    def body(*refs):
        k = comm.n
        comm.start(refs[:k], refs[k:2 * k], *refs[2 * k:])
        comm.finish(refs[:k], refs[k:2 * k], *refs[2 * k:])

    hbm = pl.BlockSpec(memory_space=pl.ANY)
    return pl.pallas_call(body, name=name, out_shape=comm.out_shape, in_specs=[hbm] * comm.n, out_specs=[hbm] * comm.n,
                          scratch_shapes=comm.scratch)(*comm.arrays)


def _matmul(a, b, mode, *, name, tm, tn, tk, epi=None, extra=(), out_dtype=F32, chunks=None, comm=None):
    if mode == "nn":
        (M, K), N = a.shape, b.shape[1]
    elif mode == "nt":
        (M, K), N = a.shape, b.shape[0]
    else:
        (K, M), N = a.shape, b.shape[1]
    tm, tn, tk = min(tm, M), min(tn, N), min(tk, K)
    assert M % tm == 0 and N % tn == 0 and K % tk == 0 and (epi != "ln" or tn == N), (name, M, N, K)
    nk = K // tk
    a_spec = {"nn": pl.BlockSpec((tm, tk), lambda i, j, k: (i, k)),
              "nt": pl.BlockSpec((tm, tk), lambda i, j, k: (i, k)),
              "tn": pl.BlockSpec((tk, tm), lambda i, j, k: (k, i))}[mode]
    b_spec = {"nn": pl.BlockSpec((tk, tn), lambda i, j, k: (k, j)),
              "nt": pl.BlockSpec((tn, tk), lambda i, j, k: (j, k)),
              "tn": pl.BlockSpec((tk, tn), lambda i, j, k: (k, j))}[mode]
    ca, cb = {"nn": (1, 0), "nt": (1, 1), "tn": (0, 0)}[mode]
    tile = pl.BlockSpec((tm, tn), lambda i, j, k: (i, j))
    row = pl.BlockSpec((1, tn), lambda i, j, k: (0, j))
    n_extra = {None: 0, "add": 1, "relu2": 0, "drelu2": 1, "ln": 3}[epi]
    assert len(extra) == n_extra
    extra_specs = {None: [], "add": [tile], "relu2": [], "drelu2": [tile], "ln": [tile, row, row]}[epi]
    split = 0
    if epi == "relu2":
        out_shape, out_specs = (jax.ShapeDtypeStruct((M, N), BF16),), (tile,)
    elif epi == "ln":
        out_shape = (jax.ShapeDtypeStruct((M, N), F32), jax.ShapeDtypeStruct((M, N), F32),
                     jax.ShapeDtypeStruct((M, N), BF16))
        out_specs = (tile, tile, tile)
    elif chunks == "cols":
        c = N // N_DEV
        out_shape = (jax.ShapeDtypeStruct((N_DEV, M, c), out_dtype),)
        if tn == N:
            split = c
            out_specs = (pl.BlockSpec((N_DEV, tm, c), lambda i, j, k: (0, i, 0)),)
        else:
            assert c % tn == 0
            out_specs = (pl.BlockSpec((None, tm, tn), lambda i, j, k: (j // (c // tn), i, j % (c // tn))),)
    else:
        out_shape, out_specs = (jax.ShapeDtypeStruct((M, N), out_dtype),), (tile,)
    n_out = len(out_shape)

    def body(*refs):
        a_ref, b_ref = refs[:2]
        ex = refs[2:2 + n_extra]
        outs = refs[2 + n_extra:2 + n_extra + n_out]
        acc_ref = refs[-1]
        k = pl.program_id(2)
        part = _dg(_bf(a_ref[...]), _bf(b_ref[...]), ca, cb)

        def finish(acc):
            if epi == "add":
                outs[0][...] = (acc + ALPHA * ex[0][...]).astype(out_dtype)
            elif epi == "relu2":
                r = jnp.maximum(acc, 0.0)
                outs[0][...] = _bf(r * r)
            elif epi == "drelu2":
                outs[0][...] = (acc * (2.0 * jnp.sqrt(ex[0][...].astype(F32)))).astype(out_dtype)
            elif epi == "ln":
                u = ALPHA * ex[0][...] + acc
                xh, _ = _norm_stats(u)
                y = xh * ex[1][...] + ex[2][...]
                outs[0][...] = u
                outs[1][...] = y
                outs[2][...] = _bf(y)
            elif split:
                for p in range(N_DEV):
                    outs[0][p] = acc[:, p * split:(p + 1) * split].astype(out_dtype)
            else:
                outs[0][...] = acc.astype(out_dtype)

        if nk == 1:
            finish(part)
        else:
            @pl.when(k == 0)
            def _():
                acc_ref[...] = part

            @pl.when(jnp.logical_and(k > 0, k < nk - 1))
            def _():
                acc_ref[...] += part

            @pl.when(k == nk - 1)
            def _():
                finish(acc_ref[...] + part)

    outs, landed = _pcall(
        body, name=name, out_shape=out_shape, grid=(M // tm, N // tn, nk),
        in_specs=[a_spec, b_spec] + extra_specs, out_specs=out_specs,
        scratch_shapes=[pltpu.VMEM((tm, tn) if nk > 1 else (8, 128), F32)], sem=("parallel", "parallel", "arbitrary"),
        args=(a, b, *extra), comm=comm)
    res = outs[0] if n_out == 1 else tuple(outs)
    if chunks == "rows":
        res = res.reshape(N_DEV, M // N_DEV, N)
    return res if comm is None else (res, landed)


def _ret_tables(S):
    half = 64
    inv_freq = ROPE_BASE ** (-jnp.arange(half, dtype=F32) / half)
    ang = jnp.arange(S, dtype=jnp.int32).astype(F32)[:, None] * inv_freq[None, :]
    cos, sin = jnp.cos(ang), jnp.sin(ang)
    cosf = jnp.concatenate([cos, cos], axis=1)
    sinf = jnp.concatenate([-sin, sin], axis=1)
    log_g = jnp.log(1.0 - 2.0 ** (-5.0 - jnp.arange(4, dtype=F32)))
    idx = jnp.arange(CHUNK, dtype=F32)
    diff = idx[:, None] - idx[None, :]
    md = jnp.where(diff[None] >= 0, jnp.exp(log_g[:, None, None] * diff[None]), 0.0)
    kd = jnp.exp(log_g[:, None] * (CHUNK - 1 - idx)[None, :])
    qd = jnp.exp(log_g[:, None] * (idx + 1.0)[None, :])
    cd = jnp.exp(log_g * CHUNK)
    bc = lambda t: jnp.broadcast_to(t[:, :, None], (4, CHUNK, CHUNK))
    return cosf, sinf, md, bc(qd), bc(kd), jnp.broadcast_to(cd[:, None, None], (4, 8, CHUNK))


def _rot(x, cosf, sinf):
    return x * cosf + pltpu.roll(x, 64, 1) * sinf


def _rot_t(dx, cosf, sinf):
    return dx * cosf - pltpu.roll(dx, 64, 1) * sinf


def _ret_specs(rev, N):
    rn = (lambda n: N - 1 - n) if rev else (lambda n: n)
    col = lambda c: pl.BlockSpec((CHUNK, 512), lambda n, c=c: (rn(n), c))
    tab = pl.BlockSpec((CHUNK, CHUNK), lambda n: (rn(n), 0))
    dec = pl.BlockSpec((4, CHUNK, CHUNK), lambda n: (0, 0, 0))
    cdec = pl.BlockSpec((4, 8, CHUNK), lambda n: (0, 0, 0))
    vec = pl.BlockSpec((1, 512), lambda n: (0, 0))
    st = pl.BlockSpec((1, 4, CHUNK, CHUNK), lambda n: (rn(n), 0, 0, 0))
    return col, tab, dec, cdec, vec, st


def _ret_fwd(proj, tables, gn_g, gn_b):
    S = proj.shape[0]
    N = S // CHUNK
    col, tab, dec, cdec, vec, st = _ret_specs(False, N)

    def body(q_ref, k_ref, v_ref, g_ref, cos_ref, sin_ref, md_ref, qd_ref, kd_ref, cd_ref, gng_ref, gnb_ref,
             out_ref, st_ref, state):
        @pl.when(pl.program_id(0) == 0)
        def _():
            state[...] = jnp.zeros_like(state)

        cosf, sinf = cos_ref[...], sin_ref[...]
        for h in range(4):
            sl = slice(h * 128, (h + 1) * 128)
            qr = _rot(q_ref[:, sl], cosf, sinf)
            kr = _rot(k_ref[:, sl], cosf, sinf) * (128 ** -0.5)
            vb = _bf(v_ref[:, sl])
            s0 = state[h]
            st_ref[0, h] = s0
            sc = _dg(_bf(qr), _bf(kr), 1, 1) * md_ref[h]
            r = _dg(_bf(sc), vb, 1, 0) + _dg(_bf(qr * qd_ref[h]), _bf(s0), 1, 0)
            state[h] = s0 * cd_ref[h, 0:1, :] + _dg(_bf(kr * kd_ref[h]), vb, 0, 0)
            y, _ = _norm_stats(r)
            rg = g_ref[:, sl]
            out_ref[:, sl] = rg * _sigmoid(rg) * (y * gng_ref[:, sl] + gnb_ref[:, sl])

    return pl.pallas_call(
        body, name="ret_fwd", grid=(N,),
        out_shape=(jax.ShapeDtypeStruct((S, RET_W), F32), jax.ShapeDtypeStruct((N, 4, CHUNK, CHUNK), F32)),
        in_specs=[col(0), col(1), col(2), col(3), tab, tab, dec, dec, dec, cdec, vec, vec],
        out_specs=(pl.BlockSpec((CHUNK, 512), lambda n: (n, 0)), st),
        scratch_shapes=[pltpu.VMEM((4, CHUNK, CHUNK), F32)],
        compiler_params=_cparams(("arbitrary",)),
    )(proj, proj, proj, proj, *tables, gn_g, gn_b)


def _ret_bwd(proj, tables, gn_g, gn_b, states, d_out):
    S = proj.shape[0]
    N = S // CHUNK
    col, tab, dec, cdec, vec, st = _ret_specs(True, N)

    def kernel_body(q_ref, k_ref, v_ref, g_ref, cos_ref, sin_ref, md_ref, qd_ref, kd_ref, cd_ref, gng_ref, gnb_ref,
                    st_ref, do_ref, dp_ref, dg_ref, db_ref, gstate):
        @pl.when(pl.program_id(0) == 0)
        def _():
            gstate[...] = jnp.zeros_like(gstate)
            dg_ref[...] = jnp.zeros_like(dg_ref)
            db_ref[...] = jnp.zeros_like(db_ref)

        cosf, sinf = cos_ref[...], sin_ref[...]
        for h in range(4):
            sl = slice(h * 128, (h + 1) * 128)
            qr = _rot(q_ref[:, sl], cosf, sinf)
            kr = _rot(k_ref[:, sl], cosf, sinf) * (128 ** -0.5)
            qb, kb, vb = _bf(qr), _bf(kr), _bf(v_ref[:, sl])
            s0b = _bf(st_ref[0, h])
            md, qd, kd = md_ref[h], qd_ref[h], kd_ref[h]
            scb = _bf(_dg(qb, kb, 1, 1) * md)
            qdb = _bf(qr * qd)
            kdb = _bf(kr * kd)
            r = _dg(scb, vb, 1, 0) + _dg(qdb, s0b, 1, 0)
            y, rstd = _norm_stats(r)
            gng = gng_ref[:, sl]
            gn = y * gng + gnb_ref[:, sl]
            rg = g_ref[:, sl]
            sg = _sigmoid(rg)
            d_o = do_ref[:, sl]
            d_gn = d_o * (rg * sg)
            dg_ref[:, sl] += jnp.sum(d_gn * y, axis=0, keepdims=True)
            db_ref[:, sl] += jnp.sum(d_gn, axis=0, keepdims=True)
            drb = _bf(_norm_bwd(d_gn * gng, y, rstd))
            g0 = gstate[h]
            gb = _bf(g0)
            dscb = _bf(_dg(drb, vb, 1, 1) * md)
            dqr = _dg(dscb, kb, 1, 0) + _dg(drb, s0b, 1, 1) * qd
            dkr = _dg(dscb, qb, 0, 0) + _dg(vb, gb, 1, 1) * kd
            dv = _dg(scb, drb, 0, 0) + _dg(kdb, gb, 1, 0)
            gstate[h] = g0 * cd_ref[h, 0:1, :] + _dg(qdb, drb, 0, 0)
            dp_ref[:, 0 * 512 + h * 128:0 * 512 + (h + 1) * 128] = _bf(_rot_t(dqr, cosf, sinf))
            dp_ref[:, 1 * 512 + h * 128:1 * 512 + (h + 1) * 128] = _bf(_rot_t(dkr, cosf, sinf) * (128 ** -0.5))
            dp_ref[:, 2 * 512 + h * 128:2 * 512 + (h + 1) * 128] = _bf(dv)
            dp_ref[:, 3 * 512 + h * 128:3 * 512 + (h + 1) * 128] = _bf(d_o * gn * (sg * (1.0 + rg * (1.0 - sg))))

    acc = pl.BlockSpec((1, 512), lambda n: (0, 0))
    return pl.pallas_call(
        kernel_body, name="ret_bwd", grid=(N,),
        out_shape=(jax.ShapeDtypeStruct((S, 2048), BF16), jax.ShapeDtypeStruct((1, 512), F32),
                   jax.ShapeDtypeStruct((1, 512), F32)),
        in_specs=[col(0), col(1), col(2), col(3), tab, tab, dec, dec, dec, cdec, vec, vec, st,
                  pl.BlockSpec((CHUNK, 512), lambda n: (N - 1 - n, 0))],
        out_specs=(pl.BlockSpec((CHUNK, 2048), lambda n: (N - 1 - n, 0)), acc, acc),
        scratch_shapes=[pltpu.VMEM((4, CHUNK, CHUNK), F32)],
        compiler_params=_cparams(("arbitrary",)),
    )(proj, proj, proj, proj, *tables, gn_g, gn_b, states, d_out)


SB_T = 256
SB_SCALE = 64 ** -0.5
SB_Q_COL, SB_K_COL, SB_V_COL = 2048 // 128, 2560 // 128, 3072 // 128


def _head_masks():
    lane = lax.broadcasted_iota(jnp.int32, (1, 128), 1)
    m0 = (lane < 64).astype(F32)
    return m0, 1.0 - m0


def _tri(n, cmp):
    r = lax.broadcasted_iota(jnp.int32, (n, n), 0)
    c = lax.broadcasted_iota(jnp.int32, (n, n), 1)
    return cmp(r, c)


def _tri_sum(x, tri):
    hi = _bf(x)
    lo = _bf(x - hi.astype(F32))
    return _dg(hi, tri, 1, 0) + _dg(lo, tri, 1, 0)


def _sb_weights(qms, kblks, upper, carry, causal):
    tiles = [(b, h) for b in range(len(kblks)) for h in range(2)]
    zs = [_dg(qms[h], kblks[b], 1, 1) for b, h in tiles]
    lgs = [-(jnp.maximum(z, 0.0) + jnp.log(1.0 + jnp.exp(-jnp.abs(z)))) for z in zs]
    if causal is not None:
        lgs = [jnp.where(causal, lg, 0.0) for lg in lgs]
    carries = list(carry)
    for t in range(len(tiles) - 2):
        carries.append(carries[t] + jnp.sum(lgs[t], axis=1, keepdims=True))
    his = [_bf(lg) for lg in lgs]
    los = [_bf(lg - hi.astype(F32)) for lg, hi in zip(lgs, his)]
    later = [_dg(hi, upper, 1, 0) for hi in his]
    later = [r + _dg(lo, upper, 1, 0) for r, lo in zip(later, los)]
    a = [jnp.exp(lg + z + (r + c)) for lg, z, r, c in zip(lgs, zs, later, carries)]
    if causal is not None:
        a = [jnp.where(causal, x, 0.0) for x in a]
    out = tuple(carries[t] + jnp.sum(lgs[t], axis=1, keepdims=True) for t in (len(tiles) - 2, len(tiles) - 1))
    return [a[2 * b:2 * b + 2] for b in range(len(kblks))], out


def _sb_fwd(proj, comm=None):
    S = proj.shape[0]
    T = min(SB_T, S)
    nq = S // T

    def body(q_ref, k_ref, v_ref, o_ref, kb_ref, vm_ref, acc_ref):
        i = pl.program_id(1)
        m0, m1 = _head_masks()

        @pl.when(i == 0)
        def _():
            v = v_ref[...]
            kb_ref[...] = _bf(k_ref[...])
            vm_ref[0] = _bf(v * m0)
            vm_ref[1] = _bf(v * m1)

        q = q_ref[...]
        qm = (_bf(q * (m0 * SB_SCALE)), _bf(q * (m1 * SB_SCALE)))
        upper = _tri(T, lambda r, c: r > c).astype(BF16)
        causal = _tri(T, lambda r, c: c < r)

        def tiles(js, carry, mask, first):
            ks = [pl.multiple_of(j * T, T) for j in js]
            a, out = _sb_weights(qm, [kb_ref[pl.ds(k, T), :] for k in ks], upper, carry, mask)
            parts = [_dg(_bf(a[b][h]), vm_ref[h, pl.ds(k, T), :], 1, 0) for b, k in enumerate(ks) for h in range(2)]
            part = functools.reduce(lambda u, w: u + w, parts)
            if first:
                acc_ref[...] = part
            else:
                acc_ref[...] += part
            return out

        zero = jnp.zeros((T, 1), F32)
        carry = tiles([i], (zero, zero), causal, True)
        carry = lax.fori_loop(0, i % 2, lambda _, c: tiles([i - 1], c, None, False), carry)
        top = i - 1 - i % 2
        lax.fori_loop(0, i // 2, lambda jj, c: tiles([top - 2 * jj, top - 2 * jj - 1], c, None, False), carry)
        o_ref[...] = acc_ref[...]

    full = lambda c: pl.BlockSpec((S, 128), lambda p, i, c=c: (0, c + p))
    outs, landed = _pcall(
        body, name="sb_fwd", grid=(4, nq), out_shape=[jax.ShapeDtypeStruct((S, SB_W), F32)],
        in_specs=[pl.BlockSpec((T, 128), lambda p, i: (i, SB_Q_COL + p)), full(SB_K_COL), full(SB_V_COL)],
        out_specs=[pl.BlockSpec((T, 128), lambda p, i: (i, p))],
        scratch_shapes=[pltpu.VMEM((S, 128), BF16), pltpu.VMEM((2, S, 128), BF16), pltpu.VMEM((T, 128), F32)],
        sem=("arbitrary", "arbitrary"), args=(proj, proj, proj), comm=comm)
    return outs[0] if comm is None else (outs[0], landed)


def _sb_bwd(proj, d_o, comm=None):
    S = proj.shape[0]
    T = min(SB_T, S)
    nq = S // T

    def body(q_ref, k_ref, v_ref, do_ref, dq_ref, dk_ref, dv_ref, kb_ref, kbm_ref, vb_ref, e_ref, dq_acc, dk_acc, dv_acc):
        i = pl.program_id(1)
        m0, m1 = _head_masks()

        @pl.when(i == 0)
        def _():
            k = k_ref[...]
            kb_ref[...] = _bf(k)
            kbm_ref[0] = _bf(k * m0)
            kbm_ref[1] = _bf(k * m1)
            vb_ref[...] = _bf(v_ref[...])
            dk_acc[...] = jnp.zeros_like(dk_acc)
            dv_acc[...] = jnp.zeros_like(dv_acc)

        q, d_out = q_ref[...], do_ref[...]
        qm = (_bf(q * (m0 * SB_SCALE)), _bf(q * (m1 * SB_SCALE)))
        dom = (_bf(d_out * m0), _bf(d_out * m1))
        upper = _tri(T, lambda r, c: r > c).astype(BF16)
        lower = _tri(T, lambda r, c: r < c).astype(BF16)
        causal = _tri(T, lambda r, c: c < r)

        def down(js, carry, mask):
            ks = [pl.multiple_of(j * T, T) for j in js]
            a, out = _sb_weights(qm, [kb_ref[pl.ds(k, T), :] for k in ks], upper, carry, mask)
            da = [[_dg(dom[h], vb_ref[pl.ds(k, T), :], 1, 1) for h in range(2)] for k in ks]
            for b, (j, k) in enumerate(zip(js, ks)):
                for h in range(2):
                    e_ref[h, j] = a[b][h] * da[b][h]
                dv_acc[pl.ds(k, T), :] += _dg(_bf(a[b][0]), dom[0], 0, 0) + _dg(_bf(a[b][1]), dom[1], 0, 0)
            return out

        def up(js, carry, mask):
            ks = [pl.multiple_of(j * T, T) for j in js]
            tiles = [(b, h) for b in range(len(js)) for h in range(2)]
            zs = [_dg(qm[h], kb_ref[pl.ds(ks[b], T), :], 1, 1) for b, h in tiles]
            es = [e_ref[h, js[b]] for b, h in tiles]
            carries = list(carry)
            for t in range(len(tiles)):
                carries.append(carries[t] + jnp.sum(es[t], axis=1, keepdims=True))
            his = [_bf(e) for e in es]
            los = [_bf(e - hi.astype(F32)) for e, hi in zip(es, his)]
            d_lg = [_dg(hi, lower, 1, 0) for hi in his]
            d_lg = [r + _dg(lo, lower, 1, 0) + c for r, lo, c in zip(d_lg, los, carries)]
            ens = [jnp.exp(-jnp.abs(z)) for z in zs]
            invs = [1.0 / (1.0 + en) for en in ens]
            betas = [jnp.where(z >= 0.0, inv, en * inv) for z, en, inv in zip(zs, ens, invs)]
            dzs = [e * (1.0 - b) - d * b for e, b, d in zip(es, betas, d_lg)]
            if mask is not None:
                dzs = [jnp.where(mask, dz, 0.0) for dz in dzs]
            dzs = [_bf(dz) for dz in dzs]
            parts = [_dg(dzs[t], kbm_ref[h, pl.ds(ks[b], T), :], 1, 0) for t, (b, h) in enumerate(tiles)]
            dq_acc[...] += functools.reduce(lambda u, w: u + w, parts)
            for b, k in enumerate(ks):
                dk_acc[pl.ds(k, T), :] += _dg(dzs[2 * b], qm[0], 0, 0) + _dg(dzs[2 * b + 1], qm[1], 0, 0)
            return tuple(carries[-2:])

        zero = jnp.zeros((T, 1), F32)
        carry = down([i], (zero, zero), causal)
        carry = lax.fori_loop(0, i % 2, lambda _, c: down([i - 1], c, None), carry)
        top = i - 1 - i % 2
        lax.fori_loop(0, i // 2, lambda jj, c: down([top - 2 * jj, top - 2 * jj - 1], c, None), carry)

        dq_acc[...] = jnp.zeros_like(dq_acc)
        carry = lax.fori_loop(0, i // 2, lambda jj, c: up([2 * jj, 2 * jj + 1], c, None), (zero, zero))
        carry = lax.fori_loop(0, i % 2, lambda _, c: up([i - 1], c, None), carry)
        up([i], carry, causal)
        dq_ref[...] = _bf(dq_acc[...] * SB_SCALE)

        @pl.when(i == nq - 1)
        def _():
            dk_ref[...] = _bf(dk_acc[...])
            dv_ref[...] = _bf(dv_acc[...])

    full = lambda c: pl.BlockSpec((S, 128), lambda p, i, c=c: (0, c + p))
    tile = pl.BlockSpec((T, 128), lambda p, i: (i, p))
    acc = pl.BlockSpec((S, 128), lambda p, i: (0, p))
    out = jax.ShapeDtypeStruct((S, SB_W), BF16)
    outs, landed = _pcall(
        body, name="sb_bwd", grid=(4, nq), out_shape=[out, out, out],
        in_specs=[pl.BlockSpec((T, 128), lambda p, i: (i, SB_Q_COL + p)), full(SB_K_COL), full(SB_V_COL), tile],
        out_specs=[tile, acc, acc],
        scratch_shapes=[pltpu.VMEM((S, 128), BF16), pltpu.VMEM((2, S, 128), BF16), pltpu.VMEM((S, 128), BF16),
                        pltpu.VMEM((2, nq, T, T), F32), pltpu.VMEM((T, 128), F32), pltpu.VMEM((S, 128), F32),
                        pltpu.VMEM((S, 128), F32)],
        sem=("arbitrary", "arbitrary"), args=(proj, proj, proj, d_o), comm=comm)
    return tuple(outs) if comm is None else (tuple(outs), landed)


SGU_U_COL, SGU_V_COL = 3584 // 512, 4096 // 512


def _causal(w):
    r = lax.broadcasted_iota(jnp.int32, (CHUNK, CHUNK), 0)
    c = lax.broadcasted_iota(jnp.int32, (CHUNK, CHUNK), 1)
    return jnp.where(r >= c, w, 0.0)


def _sgu_fwd(proj, ln_g, ln_b, w, b):
    S = proj.shape[0]
    N = S // CHUNK

    def body(u_ref, v_ref, g_ref, b_ref, w_ref, bias_ref, out_ref):
        u = _gelu(u_ref[...])
        xh, _ = _norm_stats(_gelu(v_ref[...]))
        vn = _bf(xh * g_ref[...] + b_ref[...])
        for g in range(4):
            sl = slice(g * 128, (g + 1) * 128)
            sv = _dg(_bf(_causal(w_ref[g])), vn[:, sl], 1, 0) + bias_ref[g]
            out_ref[:, sl] = u[:, sl] * sv

    vec = pl.BlockSpec((1, 512), lambda n: (0, 0))
    return pl.pallas_call(
        body, name="sgu_fwd", grid=(N,),
        out_shape=jax.ShapeDtypeStruct((S, SGU_W), F32),
        in_specs=[pl.BlockSpec((CHUNK, 512), lambda n: (n, SGU_U_COL)),
                  pl.BlockSpec((CHUNK, 512), lambda n: (n, SGU_V_COL)), vec, vec,
                  pl.BlockSpec((4, CHUNK, CHUNK), lambda n: (0, 0, 0)), pl.BlockSpec((4, CHUNK, 1), lambda n: (0, 0, 0))],
        out_specs=pl.BlockSpec((CHUNK, 512), lambda n: (n, 0)),
        compiler_params=_cparams(("parallel",)),
    )(proj, proj, ln_g, ln_b, w, b)


def _sgu_bwd(proj, ln_g, ln_b, w, b, d_out):
    S = proj.shape[0]
    N = S // CHUNK

    def body(u_ref, v_ref, g_ref, b_ref, w_ref, bias_ref, do_ref, dp_ref, dg_ref, db_ref, dw_ref, dbias_ref):
        @pl.when(pl.program_id(0) == 0)
        def _():
            dg_ref[...] = jnp.zeros_like(dg_ref)
            db_ref[...] = jnp.zeros_like(db_ref)
            dw_ref[...] = jnp.zeros_like(dw_ref)
            dbias_ref[...] = jnp.zeros_like(dbias_ref)

        gu, gv = u_ref[...], v_ref[...]
        u = _gelu(gu)
        xh, rstd = _norm_stats(_gelu(gv))
        ln_gain = g_ref[...]
        vn = _bf(xh * ln_gain + b_ref[...])
        d_o = do_ref[...]
        d_vn = []
        for g in range(4):
            sl = slice(g * 128, (g + 1) * 128)
            wc = _bf(_causal(w_ref[g]))
            sv = _dg(wc, vn[:, sl], 1, 0) + bias_ref[g]
            dp_ref[:, sl] = _bf(d_o[:, sl] * sv * _gelu_grad(gu[:, sl]))
            d_sv = d_o[:, sl] * u[:, sl]
            dbias_ref[g] += jnp.sum(d_sv, axis=1, keepdims=True)
            d_svb = _bf(d_sv)
            dw_ref[g] += _causal(_dg(d_svb, vn[:, sl], 1, 1))
            d_vn.append(_dg(wc, d_svb, 0, 0))
        d_vn = jnp.concatenate(d_vn, axis=1)
        dg_ref[...] += jnp.sum(d_vn * xh, axis=0, keepdims=True)
        db_ref[...] += jnp.sum(d_vn, axis=0, keepdims=True)
        dp_ref[:, 512:1024] = _bf(_norm_bwd(d_vn * ln_gain, xh, rstd) * _gelu_grad(gv))

    vec = pl.BlockSpec((1, 512), lambda n: (0, 0))
    wspec = pl.BlockSpec((4, CHUNK, CHUNK), lambda n: (0, 0, 0))
    bspec = pl.BlockSpec((4, CHUNK, 1), lambda n: (0, 0, 0))
    return pl.pallas_call(
        body, name="sgu_bwd", grid=(N,),
        out_shape=(jax.ShapeDtypeStruct((S, 1024), BF16), jax.ShapeDtypeStruct((1, 512), F32),
                   jax.ShapeDtypeStruct((1, 512), F32), jax.ShapeDtypeStruct((4, CHUNK, CHUNK), F32),
                   jax.ShapeDtypeStruct((4, CHUNK, 1), F32)),
        in_specs=[pl.BlockSpec((CHUNK, 512), lambda n: (n, SGU_U_COL)),
                  pl.BlockSpec((CHUNK, 512), lambda n: (n, SGU_V_COL)), vec, vec, wspec, bspec,
                  pl.BlockSpec((CHUNK, 512), lambda n: (n, 0))],
        out_specs=(pl.BlockSpec((CHUNK, 1024), lambda n: (n, 0)), vec, vec, wspec, bspec),
        compiler_params=_cparams(("arbitrary",)),
    )(proj, proj, ln_g, ln_b, w, b, d_out)


GATE_COL = 4608 // 512


def _merge_fwd(proj, branches, p_list, tm=512):
    S = proj.shape[0]
    tm = min(tm, S)

    def body(r_ref, s_ref, g_ref, pr_ref, ps_ref, pg_ref, gr_ref, gs_ref, gg_ref, m_ref, br_ref):
        acc = None
        for k, (x_ref, p_ref, gate_ref) in enumerate(((r_ref, pr_ref, gr_ref), (s_ref, ps_ref, gs_ref),
                                                      (g_ref, pg_ref, gg_ref))):
            br = _dg(_bf(x_ref[...]), _bf(p_ref[...]), 1, 0)
            br_ref[k] = br
            term = _sigmoid(gate_ref[...]) * br
            acc = term if acc is None else acc + term
        m_ref[...] = _bf(acc)

    xs = pl.BlockSpec((tm, 512), lambda i, n: (i, 0))
    ps = pl.BlockSpec((512, 512), lambda i, n: (0, n))
    gate = lambda k: pl.BlockSpec((tm, 512), lambda i, n, k=k: (i, GATE_COL + 2 * k + n))
    return pl.pallas_call(
        body, name="merge_fwd", grid=(S // tm, 2),
        out_shape=(jax.ShapeDtypeStruct((S, D_MODEL), BF16), jax.ShapeDtypeStruct((3, S, D_MODEL), F32)),
        in_specs=[xs, xs, xs, ps, ps, ps, gate(0), gate(1), gate(2)],
        out_specs=(pl.BlockSpec((tm, 512), lambda i, n: (i, n)), pl.BlockSpec((3, tm, 512), lambda i, n: (0, i, n))),
        compiler_params=_cparams(("parallel", "parallel")),
    )(*branches, *p_list, proj, proj, proj)


def _gate_bwd(proj, br, d_merged, tm=512):
    S = proj.shape[0]
    tm = min(tm, S)

    def body(dm_ref, br_ref, gr_ref, gs_ref, gg_ref, dbr_ref, dgate_ref):
        dm = dm_ref[...]
        for k, gate_ref in enumerate((gr_ref, gs_ref, gg_ref)):
            s = _sigmoid(gate_ref[...])
            dbr_ref[k] = _bf(dm * s)
            dgate_ref[k] = _bf(dm * br_ref[k] * (s * (1.0 - s)))

    gate = lambda k: pl.BlockSpec((tm, 512), lambda i, n, k=k: (i, GATE_COL + 2 * k + n))
    three = pl.BlockSpec((3, tm, 512), lambda i, n: (0, i, n))
    return pl.pallas_call(
        body, name="gate_bwd", grid=(S // tm, 2),
        out_shape=(jax.ShapeDtypeStruct((3, S, D_MODEL), BF16), jax.ShapeDtypeStruct((3, S, D_MODEL), BF16)),
        in_specs=[pl.BlockSpec((tm, 512), lambda i, n: (i, n)), three, gate(0), gate(1), gate(2)],
        out_specs=(three, three),
        compiler_params=_cparams(("parallel", "parallel")),
    )(d_merged, br, proj, proj, proj)


def _ln_bwd(dy, u, g, tm=256):
    S, D = u.shape
    tm = min(tm, S)

    def body(dy_ref, u_ref, g_ref, du_ref, dub_ref, dg_ref, db_ref):
        @pl.when(pl.program_id(0) == 0)
        def _():
            dg_ref[...] = jnp.zeros_like(dg_ref)
            db_ref[...] = jnp.zeros_like(db_ref)

        dy_t = dy_ref[...]
        xh, rstd = _norm_stats(u_ref[...])
        dg_ref[...] += jnp.sum(dy_t * xh, axis=0, keepdims=True)
        db_ref[...] += jnp.sum(dy_t, axis=0, keepdims=True)
        du = _norm_bwd(dy_t * g_ref[...], xh, rstd)
        du_ref[...] = du
        dub_ref[...] = _bf(du)

    tile = pl.BlockSpec((tm, D), lambda i: (i, 0))
    vec = pl.BlockSpec((1, D), lambda i: (0, 0))
    return pl.pallas_call(
        body, name="ln_bwd", grid=(S // tm,),
        out_shape=(jax.ShapeDtypeStruct((S, D), F32), jax.ShapeDtypeStruct((S, D), BF16),
                   jax.ShapeDtypeStruct((1, D), F32), jax.ShapeDtypeStruct((1, D), F32)),
        in_specs=[tile, tile, vec], out_specs=(tile, tile, vec, vec),
        compiler_params=_cparams(("arbitrary",)),
    )(dy, u, g)


def _loss_grad(y, target, tm=256):
    S, D = y.shape
    tm = min(tm, S)

    def body(y_ref, t_ref, dy_ref, sq_ref):
        @pl.when(pl.program_id(0) == 0)
        def _():
            sq_ref[...] = jnp.zeros_like(sq_ref)

        err = y_ref[...] - t_ref[...]
        dy_ref[...] = err * (1.0 / D)
        sq_ref[...] += jnp.sum(err * err, axis=0, keepdims=True)

    tile = pl.BlockSpec((tm, D), lambda i: (i, 0))
    vec = pl.BlockSpec((1, D), lambda i: (0, 0))
    return pl.pallas_call(
        body, name="loss_grad", grid=(S // tm,),
        out_shape=(jax.ShapeDtypeStruct((S, D), F32), jax.ShapeDtypeStruct((1, D), F32)),
        in_specs=[tile, tile], out_specs=(tile, vec),
        compiler_params=_cparams(("arbitrary",)),
    )(y, target)


def _layer_fwd(x, x_bf, W, tables, sb_comm=None):
    proj = _matmul(x_bf, W["w_in_t"], "nt", name="proj", tm=1024, tn=768, tk=1024)
    retg, states = _ret_fwd(proj, tables, W["ret_gn_g"], W["ret_gn_b"])
    if sb_comm is None:
        sb = _sb_fwd(proj)
    else:
        sb, landed = _sb_fwd(proj, comm=sb_comm[0])
        sb_comm[1](landed)
    sg = _sgu_fwd(proj, W["sgu_ln_g"], W["sgu_ln_b"], W["sgu_w"], W["sgu_b"])
    merged, br = _merge_fwd(proj, (retg, sb, sg), (W["p_ret"], W["p_sb"], W["p_sgu"]))
    u1, x1, x1_bf = _matmul(merged, W["w_out"], "nn", name="out_ln", tm=512, tn=1024, tk=1024, epi="ln",
                            extra=(x, W["ln1_g"], W["ln1_b"]))
    act = _matmul(x1_bf, W["w_up"], "nn", name="up", tm=1024, tn=1024, tk=1024, epi="relu2")
    u2, x2, x2_bf = _matmul(act, W["w_down"], "nn", name="down_ln", tm=512, tn=1024, tk=1024, epi="ln",
                            extra=(x1, W["ln2_g"], W["ln2_b"]))
    saved = dict(x_bf=x_bf, proj=proj, retg=retg, states=states, sb=sb, sg=sg, merged=merged, br=br, u1=u1,
                 x1_bf=x1_bf, act=act, u2=u2)
    return x2, x2_bf, saved


def _layer_bwd(d_x2, W, tables, sv, chunk_dtype=None, sb_comm_fn=None, dx_comm_fn=None):
    dt = F32 if chunk_dtype is None else chunk_dtype
    rows, cols = (None, None) if chunk_dtype is None else ("rows", "cols")
    g, landed = {}, {}
    du2, du2_bf, g["ln2_g"], g["ln2_b"] = _ln_bwd(d_x2, sv["u2"], W["ln2_g"])
    d_hpre = _matmul(du2_bf, W["w_down"], "nt", name="d_act", tm=1024, tn=1024, tk=1024, epi="drelu2",
                     extra=(sv["act"],), out_dtype=BF16)
    g["w_down"] = _matmul(sv["act"], du2_bf, "tn", name="dw_down", tm=512, tn=1024, tk=4096, out_dtype=dt, chunks=rows)
    g["w_up"] = _matmul(sv["x1_bf"], d_hpre, "tn", name="dw_up", tm=1024, tn=512, tk=4096, out_dtype=dt, chunks=cols)
    d_x1 = _matmul(d_hpre, W["w_up"], "nt", name="d_x1", tm=1024, tn=1024, tk=1024, epi="add", extra=(du2,))
    du1, du1_bf, g["ln1_g"], g["ln1_b"] = _ln_bwd(d_x1, sv["u1"], W["ln1_g"])
    d_merged = _matmul(du1_bf, W["w_out"], "nt", name="d_merged", tm=1024, tn=1024, tk=1024)
    g["w_out"] = _matmul(sv["merged"], du1_bf, "tn", name="dw_out", tm=1024, tn=512, tk=4096, out_dtype=dt, chunks=rows)
    d_br, d_gate = _gate_bwd(sv["proj"], sv["br"], d_merged)
    d_branch = []
    for k, (nm, act) in enumerate((("p_ret", sv["retg"]), ("p_sb", sv["sb"]), ("p_sgu", sv["sg"]))):
        d_branch.append(_matmul(d_br[k], W[nm], "nt", name="d_" + nm[2:], tm=1024, tn=512, tk=1024))
        g[nm] = _matmul(act, d_br[k], "tn", name="dw_" + nm[2:], tm=512, tn=1024, tk=2048, out_dtype=dt, chunks=cols)
    d_ret, g["ret_gn_g"], g["ret_gn_b"] = _ret_bwd(sv["proj"], tables, W["ret_gn_g"], W["ret_gn_b"], sv["states"],
                                                   d_branch[0])
    if sb_comm_fn is None:
        d_sq, d_sk, d_sv = _sb_bwd(sv["proj"], d_branch[1])
    else:
        (d_sq, d_sk, d_sv), landed["sb"] = _sb_bwd(sv["proj"], d_branch[1], comm=sb_comm_fn(g))
    d_sgu, g["sgu_ln_g"], g["sgu_ln_b"], g["sgu_w"], g["sgu_b"] = _sgu_bwd(
        sv["proj"], W["sgu_ln_g"], W["sgu_ln_b"], W["sgu_w"], W["sgu_b"], d_branch[2])
    d_proj = jnp.concatenate([d_ret, d_sq, d_sk, d_sv, d_sgu, d_gate[0], d_gate[1], d_gate[2]], axis=1)
    g["w_in"] = _matmul(sv["x_bf"], d_proj, "tn", name="dw_in", tm=1024, tn=640, tk=4096, out_dtype=dt)
    if chunk_dtype is not None:
        g["w_in"] = g["w_in"].reshape(D_MODEL, N_DEV, N_IN // N_DEV).transpose(1, 0, 2)
    d_x = _matmul(d_proj, W["w_in_t"], "nn", name="d_x", tm=1024, tn=1024, tk=1920, epi="add", extra=(du1,),
                  comm=None if dx_comm_fn is None else dx_comm_fn(g))
    if dx_comm_fn is not None:
        d_x, landed["dx"] = d_x
    return d_x, g, landed


BIG = ("w_in", "p_ret", "p_sb", "p_sgu", "w_out", "w_up", "w_down")
SMALL = ("ret_gn_g", "ret_gn_b", "sgu_ln_g", "sgu_ln_b", "sgu_w", "sgu_b", "ln1_g", "ln1_b", "ln2_g", "ln2_b")
GATHER_KIND = {"w_in": "rows", "p_ret": "cols", "p_sb": "cols", "p_sgu": "cols", "w_out": "rows", "w_up": "cols",
               "w_down": "rows"}


def _small_weights(small, l):
    W = {}
    for n in SMALL:
        if n == "sgu_w":
            W[n] = small[n][l]
        elif n == "sgu_b":
            W[n] = small[n][l].reshape(4, CHUNK, 1)
        else:
            W[n] = small[n][l].reshape(1, -1)
    return W


def _local_step(x, target, full, small):
    tables = _ret_tables(x.shape[0])
    Ws = [{**{n: full[n][l] for n in BIG[1:]}, "w_in_t": full["w_in"][l].T, **_small_weights(small, l)}
          for l in range(DEPTH)]
    saved = []
    h, h_bf = x, _bf(x)
    for l in range(DEPTH):
        h, h_bf, sv = _layer_fwd(h, h_bf, Ws[l], tables)
        saved.append(sv)
    d_h, sq = _loss_grad(h, target)
    grads = [None] * DEPTH
    for l in reversed(range(DEPTH)):
        d_h, grads[l], _ = _layer_bwd(d_h, Ws[l], tables, saved[l])
    return sq, d_h, grads


def _adam(w, parts, m, v, name, tr=256):
    L, R, C = w.shape
    tr = min(tr, R)
    assert R % tr == 0 and len(parts) == L

    def body(*refs):
        w_ref, p_refs, (m_ref, v_ref, g_ref, d_ref, nm_ref, nv_ref) = refs[0], refs[1:1 + L], refs[1 + L:]
        layer = pl.program_id(0)
        g = None
        for li, p_ref in enumerate(p_refs):
            s = p_ref[0].astype(F32)
            for j in range(1, N_DEV):
                s = s + p_ref[j].astype(F32)
            g = s if g is None else jnp.where(layer == li, s, g)
        g_ref[...] = g
        d_ref[...], nm_ref[...], nv_ref[...] = _adam_update(w_ref[...], g, m_ref[...], v_ref[...])

    tile = pl.BlockSpec((None, tr, C), lambda l, i: (l, i, 0))
    part = lambda li: pl.BlockSpec((N_DEV, tr, C), lambda l, i, li=li: (0, jnp.where(l == li, i, 0), 0))
    out = jax.ShapeDtypeStruct((L, R, C), F32)
    return pl.pallas_call(
        body, name=name, grid=(L, R // tr), out_shape=(out, out, out, out),
        in_specs=[tile] + [part(li) for li in range(L)] + [tile, tile],
        out_specs=(tile, tile, tile, tile),
        compiler_params=_cparams(("parallel", "parallel")),
    )(w, *parts, m, v)


def _adam_update(w, g, m, v):
    m2 = ADAM_B1 * m + (1.0 - ADAM_B1) * g
    v2 = ADAM_B2 * v + (1.0 - ADAM_B2) * (g * g)
    m_hat = m2 / (1.0 - ADAM_B1 ** ADAM_STEP)
    v_hat = v2 / (1.0 - ADAM_B2 ** ADAM_STEP)
    return -ADAM_LR * (m_hat / (jnp.sqrt(v_hat) + ADAM_EPS) + ADAM_WD * w), m2, v2


def _adam_small(w, m, v, parts):
    k = len(SMALL)

    def body(*refs):
        w_refs, m_refs, v_refs, p_refs, outs = refs[:k], refs[k:2 * k], refs[2 * k:3 * k], refs[3 * k:5 * k], refs[5 * k:]
        for i in range(k):
            vector = len(w_refs[i].shape) == 2
            for l in range(DEPTH):
                p_ref = p_refs[DEPTH * i + l]
                g = p_ref[0]
                for j in range(1, N_DEV):
                    g = g + p_ref[j]
                at = (slice(l, l + 1), slice(None)) if vector else (l,)
                delta, m2, v2 = _adam_update(w_refs[i][at], g, m_refs[i][at], v_refs[i][at])
                for o_ref, val in zip(outs[4 * i:4 * i + 4], (g, delta, m2, v2)):
                    o_ref[at] = val

    vmem = pl.BlockSpec(memory_space=pltpu.VMEM)
    args = [w[n] for n in SMALL] + [m[n] for n in SMALL] + [v[n] for n in SMALL] + \
           [parts[(n, l)] for n in SMALL for l in range(DEPTH)]
    out_shape = [jax.ShapeDtypeStruct(w[n].shape, F32) for n in SMALL for _ in range(4)]
    outs = pl.pallas_call(body, name="adam_small", out_shape=out_shape, in_specs=[vmem] * len(args),
                          out_specs=[vmem] * len(out_shape), compiler_params=_cparams())(*args)
    return {n: tuple(outs[4 * i:4 * i + 4]) for i, n in enumerate(SMALL)}


WEIGHTS = ("w_in", "ret_gn_g", "ret_gn_b", "sgu_ln_g", "sgu_ln_b", "sgu_w", "sgu_b", "p_ret", "p_sb", "p_sgu", "w_out",
           "ln1_g", "ln1_b", "w_up", "w_down", "ln2_g", "ln2_b")


def kernel(x, w_in, ret_gn_g, ret_gn_b, sgu_ln_g, sgu_ln_b, sgu_w, sgu_b, p_ret, p_sb, p_sgu, w_out, ln1_g, ln1_b, w_up, w_down, ln2_g, ln2_b, loss_target, m_w_in, m_ret_gn_g, m_ret_gn_b, m_sgu_ln_g, m_sgu_ln_b, m_sgu_w, m_sgu_b, m_p_ret, m_p_sb, m_p_sgu, m_w_out, m_ln1_g, m_ln1_b, m_w_up, m_w_down, m_ln2_g, m_ln2_b, v_w_in, v_ret_gn_g, v_ret_gn_b, v_sgu_ln_g, v_sgu_ln_b, v_sgu_w, v_sgu_b, v_p_ret, v_p_sb, v_p_sgu, v_w_out, v_ln1_g, v_ln1_b, v_w_up, v_w_down, v_ln2_g, v_ln2_b):
    w = dict(zip(WEIGHTS, (w_in, ret_gn_g, ret_gn_b, sgu_ln_g, sgu_ln_b, sgu_w, sgu_b, p_ret, p_sb, p_sgu, w_out,
                           ln1_g, ln1_b, w_up, w_down, ln2_g, ln2_b)))
    m = dict(zip(WEIGHTS, (m_w_in, m_ret_gn_g, m_ret_gn_b, m_sgu_ln_g, m_sgu_ln_b, m_sgu_w, m_sgu_b, m_p_ret, m_p_sb,
                           m_p_sgu, m_w_out, m_ln1_g, m_ln1_b, m_w_up, m_w_down, m_ln2_g, m_ln2_b)))
    v = dict(zip(WEIGHTS, (v_w_in, v_ret_gn_g, v_ret_gn_b, v_sgu_ln_g, v_sgu_ln_b, v_sgu_w, v_sgu_b, v_p_ret, v_p_sb,
                           v_p_sgu, v_w_out, v_ln1_g, v_ln1_b, v_w_up, v_w_down, v_ln2_g, v_ln2_b)))

    small = {n: w[n] for n in SMALL}
    shard = {n: _bf(w[n]) for n in BIG}
    shard["w_in"] = shard["w_in"].transpose(0, 2, 1)
    S = x.shape[1]
    x0, target = x.reshape(S, D_MODEL), loss_target.reshape(S, D_MODEL)
    tables = _ret_tables(S)
    Ws = [_small_weights(small, l) for l in range(DEPTH)]

    (Ws[0]["w_in_t"],) = _exchange([_gather_transfer(shard["w_in"], 0, "rows")], "gather_w_in0", relay=True)
    later = [(n, 0) for n in BIG[1:]] + [(n, 1) for n in BIG]

    def weights_landed(landed):
        for (n, l), z in zip(later, landed):
            Ws[l]["w_in_t" if n == "w_in" else n] = z

    gather = _Comm([_gather_transfer(shard[n], l, GATHER_KIND[n]) for n, l in later], relay=True)
    h, h_bf, saved0 = _layer_fwd(x0, _bf(x0), Ws[0], tables, sb_comm=(gather, weights_landed))
    h, _, saved1 = _layer_fwd(h, h_bf, Ws[1], tables)
    d_h, sq = _loss_grad(h, target)
    loss = lax.psum(0.5 * jnp.sum(sq) / D_MODEL, ("x", "y", "c"))

    d_h, g1, _ = _layer_bwd(d_h, Ws[1], tables, saved1, chunk_dtype=BF16)
    early = [(n, 1) for n in BIG] + [(n, 0) for n in BIG[1:]]

    def early_scatter(g0):
        return _Comm([_scatter_transfer((g1 if l else g0)[n]) for n, l in early])

    small_keys = [(n, l) for n in SMALL for l in range(DEPTH)]

    def late_scatter(g0):
        mine = [(g1 if l else g0)[n] for n, l in small_keys]
        mine = [a.reshape(4, CHUNK) if n == "sgu_b" else a for a, (n, l) in zip(mine, small_keys)]
        return _Comm([_scatter_transfer(g0["w_in"])] + [_slab_transfer(a) for a in mine])

    d_x, g0, landed = _layer_bwd(d_h, Ws[0], tables, saved0, chunk_dtype=BF16, sb_comm_fn=early_scatter,
                                 dx_comm_fn=late_scatter)
    parts = dict(zip(early, landed["sb"]))
    parts[("w_in", 0)] = landed["dx"][0]
    small_parts = dict(zip(small_keys, landed["dx"][1:]))

    grad, delta, new_m, new_v = {}, {}, {}, {}
    for n in BIG:
        grad[n], delta[n], new_m[n], new_v[n] = _adam(w[n], [parts[(n, l)] for l in range(DEPTH)], m[n], v[n],
                                                      "adam_" + n)
    for n, res in _adam_small(small, m, v, small_parts).items():
        grad[n], delta[n], new_m[n], new_v[n] = res

    return (loss, d_x.reshape(x.shape), *[grad[n] for n in WEIGHTS], *[delta[n] for n in WEIGHTS],
            *[new_m[n] for n in WEIGHTS], *[new_v[n] for n in WEIGHTS])
```

```python
import functools
import math

import numpy as np
import jax
import jax.numpy as jnp
from jax import lax
from jax.experimental import pallas as pl
from jax.experimental.pallas import tpu as pltpu

F32 = jnp.float32
BF16 = jnp.bfloat16

N_DEV = 8
DEPTH = 2
D_MODEL = 1024
CHUNK = 128
RET_W = 512
SB_W = 512
SGU_W = 512
N_IN = 7680
LN_EPS = 1e-5
ALPHA = (2 * DEPTH) ** 0.25
ROPE_BASE = 10000.0
ADAM_LR, ADAM_B1, ADAM_B2, ADAM_EPS, ADAM_WD, ADAM_STEP = 0.001, 0.9, 0.999, 1e-08, 0.01, 10
VMEM_LIMIT = 56 * 1024 * 1024

_GELU_K = math.sqrt(2.0 / math.pi)
_GELU_C = 0.044715


def _cparams(sem=None):
    return pltpu.CompilerParams(dimension_semantics=sem, vmem_limit_bytes=VMEM_LIMIT)


def _dg(a, b, ca, cb):
    return lax.dot_general(a, b, (((ca,), (cb,)), ((), ())), preferred_element_type=F32)


def _bf(x):
    return x.astype(BF16)


def _sigmoid(x):
    return 1.0 / (1.0 + jnp.exp(-x))


def _gelu(x):
    t = jnp.tanh(_GELU_K * (x + _GELU_C * (x * x * x)))
    return x * (0.5 * (1.0 + t))


def _gelu_grad(x):
    t = jnp.tanh(_GELU_K * (x + _GELU_C * (x * x * x)))
    return 0.5 * (1.0 + t) + 0.5 * x * (1.0 - t * t) * (_GELU_K * (1.0 + 3.0 * _GELU_C * x * x))


def _norm_stats(u):
    mu = jnp.mean(u, axis=-1, keepdims=True)
    d = u - mu
    var = jnp.mean(d * d, axis=-1, keepdims=True)
    rstd = lax.rsqrt(var + LN_EPS)
    return d * rstd, rstd


def _norm_bwd(dxh, xh, rstd):
    return rstd * (dxh - jnp.mean(dxh, axis=-1, keepdims=True) - xh * jnp.mean(dxh * xh, axis=-1, keepdims=True))


class _Transfer:
    def __init__(self, src, dst_shape, src_at, dst_at, same_core=False):
        self.src, self.dst_shape, self.src_at, self.dst_at = src, tuple(dst_shape), src_at, dst_at
        self.same_core = same_core


def _gather_transfer(shard, l, kind):
    _, r, c = shard.shape
    src_at = lambda ref, p: ref.at[l]
    if kind == "slab":
        return _Transfer(shard, (N_DEV, r, c), src_at, lambda ref, s: ref.at[s])
    if kind == "rows":
        return _Transfer(shard, (N_DEV * r, c), src_at, lambda ref, s: ref.at[pl.ds(pl.multiple_of(s * r, r), r), :])
    return _Transfer(shard, (r, N_DEV * c), src_at, lambda ref, s: ref.at[:, pl.ds(pl.multiple_of(s * c, c), c)])


def _scatter_transfer(chunks):
    return _Transfer(chunks, chunks.shape, lambda ref, p: ref.at[p], lambda ref, s: ref.at[s])


def _slab_transfer(arr):
    return _Transfer(arr, (N_DEV,) + arr.shape, lambda ref, p: ref, lambda ref, s: ref.at[s])


class _Comm:
    def __init__(self, transfers, relay=False):
        self.transfers = list(transfers)
        self.relay = relay
        self.n = len(self.transfers)
        self.arrays = [t.src for t in self.transfers]
        self.out_shape = [jax.ShapeDtypeStruct(t.dst_shape, t.src.dtype) for t in self.transfers]
        self.scratch = [pltpu.SemaphoreType.DMA((self.n * (N_DEV - 1),)), pltpu.SemaphoreType.DMA((self.n * (N_DEV - 1),)),
                        pltpu.SemaphoreType.DMA((self.n,))]

    def _relay_copies(self, srcs, dsts, send_sems, recv_sems, local_sems):
        x, y, c = lax.axis_index("x"), lax.axis_index("y"), lax.axis_index("c")
        me = 4 * x + 2 * y + c
        chips = [(1 - x, y), (x, 1 - y), (1 - x, 1 - y)]
        first, passed, own = [], [], []
        for t, tr in enumerate(self.transfers):
            def copy(k, src, sender, to, t=t, tr=tr):
                return pltpu.make_async_remote_copy(
                    src_ref=src, dst_ref=tr.dst_at(dsts[t], sender), send_sem=send_sems.at[t * (N_DEV - 1) + k],
                    recv_sem=recv_sems.at[t * (N_DEV - 1) + k], device_id=to, device_id_type=pl.DeviceIdType.MESH)

            mine = tr.src_at(srcs[t], me)
            first.append([copy(0, mine, me, (x, y, 1 - c))] + [copy(1 + j, mine, me, (px, py, c))
                                                                for j, (px, py) in enumerate(chips)])
            passed.append([copy(4 + j, tr.dst_at(dsts[t], 4 * px + 2 * py + c), 4 * px + 2 * py + c, (x, y, 1 - c))
                           for j, (px, py) in enumerate(chips)])
            own.append(pltpu.make_async_copy(mine, tr.dst_at(dsts[t], me), local_sems.at[t]))
        return first, passed, own

    def _copies(self, srcs, dsts, send_sems, recv_sems, local_sems):
        x, y, c = lax.axis_index("x"), lax.axis_index("y"), lax.axis_index("c")
        me = 4 * x + 2 * y + c
        copies = []
        for d in range(1, N_DEV):
            px = 1 - x if d & 4 else x
            py = 1 - y if d & 2 else y
            pc = 1 - c if d & 1 else c
            for t, tr in enumerate(self.transfers):
                if tr.same_core and d & 1:
                    continue
                peer, mine = (2 * px + py, 2 * x + y) if tr.same_core else (4 * px + 2 * py + pc, me)
                k = t * (N_DEV - 1) + d - 1
                copies.append(pltpu.make_async_remote_copy(
                    src_ref=tr.src_at(srcs[t], peer), dst_ref=tr.dst_at(dsts[t], mine),
                    send_sem=send_sems.at[k], recv_sem=recv_sems.at[k],
                    device_id=(px, py, pc), device_id_type=pl.DeviceIdType.MESH))
        own = []
        for t, tr in enumerate(self.transfers):
            mine = 2 * x + y if tr.same_core else me
            own.append(pltpu.make_async_copy(tr.src_at(srcs[t], mine), tr.dst_at(dsts[t], mine), local_sems.at[t]))
        return copies, own

    def start(self, srcs, dsts, *sems):
        if self.relay:
            first, _, own = self._relay_copies(srcs, dsts, *sems)
            for cp in own + [cp for per_t in first for cp in per_t]:
                cp.start()
            return
        copies, own = self._copies(srcs, dsts, *sems)
        for cp in own + copies:
            cp.start()

    def finish(self, srcs, dsts, *sems):
        if self.relay:
            first, passed, own = self._relay_copies(srcs, dsts, *sems)
            for j in range(3):
                for t in range(self.n):
                    first[t][1 + j].wait_recv()
                    passed[t][j].start()
            for t in range(self.n):
                first[t][0].wait_recv()
                for cp in passed[t]:
                    cp.wait_recv()
            for t in range(self.n):
                for cp in first[t] + passed[t]:
                    cp.wait_send()
                own[t].wait()
            return
        copies, own = self._copies(srcs, dsts, *sems)
        for cp in copies + own:
            cp.wait()


def _pcall(body, *, name, grid, in_specs, out_specs, out_shape, scratch_shapes, sem, args, comm=None):
    in_specs, out_specs, out_shape = list(in_specs), list(out_specs), list(out_shape)
    if comm is None:
        outs = pl.pallas_call(body, name=name, grid=grid, in_specs=in_specs, out_specs=out_specs, out_shape=out_shape,
                              scratch_shapes=list(scratch_shapes), compiler_params=_cparams(sem))(*args)
        return list(outs), []
    n_in, n_out, n_scr, k = len(in_specs), len(out_specs), len(scratch_shapes), comm.n

    def carrier(*refs):
        ins, cin = refs[:n_in], refs[n_in:n_in + k]
        outs, cout = refs[n_in + k:n_in + k + n_out], refs[n_in + k + n_out:n_in + 2 * k + n_out]
        scr, sems = refs[n_in + 2 * k + n_out:n_in + 2 * k + n_out + n_scr], refs[n_in + 2 * k + n_out + n_scr:]
        ids = [pl.program_id(d) for d in range(len(grid))]
        first = functools.reduce(jnp.logical_and, [i == 0 for i in ids])
        last = functools.reduce(jnp.logical_and, [i == g - 1 for i, g in zip(ids, grid)])

        @pl.when(first)
        def _():
            comm.start(cin, cout, *sems)

        body(*ins, *outs, *scr)

        @pl.when(last)
        def _():
            comm.finish(cin, cout, *sems)

    hbm = pl.BlockSpec(memory_space=pl.ANY)
    outs = pl.pallas_call(
        carrier, name=name, grid=grid, in_specs=in_specs + [hbm] * k, out_specs=out_specs + [hbm] * k,
        out_shape=out_shape + comm.out_shape, scratch_shapes=list(scratch_shapes) + comm.scratch,
        compiler_params=_cparams(tuple("arbitrary" for _ in grid)),
    )(*args, *comm.arrays)
    return list(outs[:n_out]), list(outs[n_out:])


def _exchange(transfers, name, relay=False):
    comm = _Comm(transfers, relay)

    def body(*refs):
        k = comm.n
        comm.start(refs[:k], refs[k:2 * k], *refs[2 * k:])
        comm.finish(refs[:k], refs[k:2 * k], *refs[2 * k:])

    hbm = pl.BlockSpec(memory_space=pl.ANY)
    return pl.pallas_call(body, name=name, out_shape=comm.out_shape, in_specs=[hbm] * comm.n, out_specs=[hbm] * comm.n,
                          scratch_shapes=comm.scratch)(*comm.arrays)


def _pair_reduce(chunks, name, tr=320):
    _, r, c = chunks.shape
    tr = min(tr, r)
    assert r % tr == 0

    def swap(src_ref, dst_ref, send_sems, recv_sems):
        x, y, core = lax.axis_index("x"), lax.axis_index("y"), lax.axis_index("c")
        copies = [pltpu.make_async_remote_copy(
            src_ref=src_ref.at[2 * k + 1 - core], dst_ref=dst_ref.at[k], send_sem=send_sems.at[k],
            recv_sem=recv_sems.at[k], device_id=(x, y, 1 - core), device_id_type=pl.DeviceIdType.MESH) for k in range(4)]
        for cp in copies:
            cp.start()
        for cp in copies:
            cp.wait()

    hbm = pl.BlockSpec(memory_space=pl.ANY)
    theirs = pl.pallas_call(swap, name=name + "_swap", out_shape=jax.ShapeDtypeStruct((4, r, c), chunks.dtype),
                            in_specs=[hbm], out_specs=hbm,
                            scratch_shapes=[pltpu.SemaphoreType.DMA((4,)), pltpu.SemaphoreType.DMA((4,))])(chunks)

    def add(mine_ref, theirs_ref, out_ref):
        core = lax.axis_index("c")
        both = mine_ref[...].astype(F32)
        out_ref[...] = (jnp.where(core == 0, both[0], both[1]) + theirs_ref[...].astype(F32)).astype(out_ref.dtype)

    return pl.pallas_call(
        add, name=name + "_add", grid=(4, r // tr), out_shape=jax.ShapeDtypeStruct((4, r, c), chunks.dtype),
        in_specs=[pl.BlockSpec((None, 2, tr, c), lambda k, i: (k, 0, i, 0)), pl.BlockSpec((None, tr, c), lambda k, i: (k, i, 0))],
        out_specs=pl.BlockSpec((None, tr, c), lambda k, i: (k, i, 0)),
        compiler_params=_cparams(("parallel", "parallel")),
    )(chunks.reshape(4, 2, r, c), theirs)


def _chip_scatter_transfer(pairs):
    return _Transfer(pairs, pairs.shape, lambda ref, p: ref.at[p], lambda ref, s: ref.at[s], same_core=True)


def _matmul(a, b, mode, *, name, tm, tn, tk, epi=None, extra=(), out_dtype=F32, chunks=None, comm=None):
    if mode == "nn":
        (M, K), N = a.shape, b.shape[1]
    elif mode == "nt":
        (M, K), N = a.shape, b.shape[0]
    else:
        (K, M), N = a.shape, b.shape[1]
    tm, tn, tk = min(tm, M), min(tn, N), min(tk, K)
    assert M % tm == 0 and N % tn == 0 and K % tk == 0 and (epi != "ln" or tn == N), (name, M, N, K)
    nk = K // tk
    a_spec = {"nn": pl.BlockSpec((tm, tk), lambda i, j, k: (i, k)),
              "nt": pl.BlockSpec((tm, tk), lambda i, j, k: (i, k)),
              "tn": pl.BlockSpec((tk, tm), lambda i, j, k: (k, i))}[mode]
    b_spec = {"nn": pl.BlockSpec((tk, tn), lambda i, j, k: (k, j)),
              "nt": pl.BlockSpec((tn, tk), lambda i, j, k: (j, k)),
              "tn": pl.BlockSpec((tk, tn), lambda i, j, k: (k, j))}[mode]
    ca, cb = {"nn": (1, 0), "nt": (1, 1), "tn": (0, 0)}[mode]
    tile = pl.BlockSpec((tm, tn), lambda i, j, k: (i, j))
    row = pl.BlockSpec((1, tn), lambda i, j, k: (0, j))
    n_extra = {None: 0, "add": 1, "relu2": 0, "drelu2": 1, "ln": 3}[epi]
    assert len(extra) == n_extra
    extra_specs = {None: [], "add": [tile], "relu2": [], "drelu2": [tile], "ln": [tile, row, row]}[epi]
    split = 0
    if epi == "relu2":
        out_shape, out_specs = (jax.ShapeDtypeStruct((M, N), BF16),), (tile,)
    elif epi == "ln":
        out_shape = (jax.ShapeDtypeStruct((M, N), F32), jax.ShapeDtypeStruct((M, N), F32),
                     jax.ShapeDtypeStruct((M, N), BF16))
        out_specs = (tile, tile, tile)
    elif chunks == "cols":
        c = N // N_DEV
        out_shape = (jax.ShapeDtypeStruct((N_DEV, M, c), out_dtype),)
        if tn == N:
            split = c
            out_specs = (pl.BlockSpec((N_DEV, tm, c), lambda i, j, k: (0, i, 0)),)
        else:
            assert c % tn == 0
            out_specs = (pl.BlockSpec((None, tm, tn), lambda i, j, k: (j // (c // tn), i, j % (c // tn))),)
    else:
        out_shape, out_specs = (jax.ShapeDtypeStruct((M, N), out_dtype),), (tile,)
    n_out = len(out_shape)

    def body(*refs):
        a_ref, b_ref = refs[:2]
        ex = refs[2:2 + n_extra]
        outs = refs[2 + n_extra:2 + n_extra + n_out]
        acc_ref = refs[-1]
        k = pl.program_id(2)
        part = _dg(_bf(a_ref[...]), _bf(b_ref[...]), ca, cb)

        def finish(acc):
            if epi == "add":
                outs[0][...] = (acc + ALPHA * ex[0][...]).astype(out_dtype)
            elif epi == "relu2":
                r = jnp.maximum(acc, 0.0)
                outs[0][...] = _bf(r * r)
            elif epi == "drelu2":
                outs[0][...] = (acc * (2.0 * jnp.sqrt(ex[0][...].astype(F32)))).astype(out_dtype)
            elif epi == "ln":
                u = ALPHA * ex[0][...] + acc
                xh, _ = _norm_stats(u)
                y = xh * ex[1][...] + ex[2][...]
                outs[0][...] = u
                outs[1][...] = y
                outs[2][...] = _bf(y)
            elif split:
                for p in range(N_DEV):
                    outs[0][p] = acc[:, p * split:(p + 1) * split].astype(out_dtype)
            else:
                outs[0][...] = acc.astype(out_dtype)

        if nk == 1:
            finish(part)
        else:
            @pl.when(k == 0)
            def _():
                acc_ref[...] = part

            @pl.when(jnp.logical_and(k > 0, k < nk - 1))
            def _():
                acc_ref[...] += part

            @pl.when(k == nk - 1)
            def _():
                finish(acc_ref[...] + part)

    outs, landed = _pcall(
        body, name=name, out_shape=out_shape, grid=(M // tm, N // tn, nk),
        in_specs=[a_spec, b_spec] + extra_specs, out_specs=out_specs,
        scratch_shapes=[pltpu.VMEM((tm, tn) if nk > 1 else (8, 128), F32)], sem=("parallel", "parallel", "arbitrary"),
        args=(a, b, *extra), comm=comm)
    res = outs[0] if n_out == 1 else tuple(outs)
    if chunks == "rows":
        res = res.reshape(N_DEV, M // N_DEV, N)
    return res if comm is None else (res, landed)


def _ret_tables(S):
    half = 64
    inv_freq = ROPE_BASE ** (-jnp.arange(half, dtype=F32) / half)
    ang = jnp.arange(S, dtype=jnp.int32).astype(F32)[:, None] * inv_freq[None, :]
    cos, sin = jnp.cos(ang), jnp.sin(ang)
    cosf = jnp.concatenate([cos, cos], axis=1)
    sinf = jnp.concatenate([-sin, sin], axis=1)
    log_g = jnp.log(1.0 - 2.0 ** (-5.0 - jnp.arange(4, dtype=F32)))
    idx = jnp.arange(CHUNK, dtype=F32)
    diff = idx[:, None] - idx[None, :]
    md = jnp.where(diff[None] >= 0, jnp.exp(log_g[:, None, None] * diff[None]), 0.0)
    kd = jnp.exp(log_g[:, None] * (CHUNK - 1 - idx)[None, :])
    qd = jnp.exp(log_g[:, None] * (idx + 1.0)[None, :])
    cd = jnp.exp(log_g * CHUNK)
    bc = lambda t: jnp.broadcast_to(t[:, :, None], (4, CHUNK, CHUNK))
    return cosf, sinf, md, bc(qd), bc(kd), jnp.broadcast_to(cd[:, None, None], (4, 8, CHUNK))


def _rot(x, cosf, sinf):
    return x * cosf + pltpu.roll(x, 64, 1) * sinf


def _rot_t(dx, cosf, sinf):
    return dx * cosf - pltpu.roll(dx, 64, 1) * sinf


def _ret_specs(rev, N):
    rn = (lambda n: N - 1 - n) if rev else (lambda n: n)
    col = lambda c: pl.BlockSpec((CHUNK, 512), lambda n, c=c: (rn(n), c))
    tab = pl.BlockSpec((CHUNK, CHUNK), lambda n: (rn(n), 0))
    dec = pl.BlockSpec((4, CHUNK, CHUNK), lambda n: (0, 0, 0))
    cdec = pl.BlockSpec((4, 8, CHUNK), lambda n: (0, 0, 0))
    vec = pl.BlockSpec((1, 512), lambda n: (0, 0))
    st = pl.BlockSpec((1, 4, CHUNK, CHUNK), lambda n: (rn(n), 0, 0, 0))
    return col, tab, dec, cdec, vec, st


def _ret_fwd(proj, tables, gn_g, gn_b):
    S = proj.shape[0]
    N = S // CHUNK
    col, tab, dec, cdec, vec, st = _ret_specs(False, N)

    def body(q_ref, k_ref, v_ref, g_ref, cos_ref, sin_ref, md_ref, qd_ref, kd_ref, cd_ref, gng_ref, gnb_ref,
             out_ref, st_ref, state):
        @pl.when(pl.program_id(0) == 0)
        def _():
            state[...] = jnp.zeros_like(state)

        cosf, sinf = cos_ref[...], sin_ref[...]
        for h in range(4):
            sl = slice(h * 128, (h + 1) * 128)
            qr = _rot(q_ref[:, sl], cosf, sinf)
            kr = _rot(k_ref[:, sl], cosf, sinf) * (128 ** -0.5)
            vb = _bf(v_ref[:, sl])
            s0 = state[h]
            st_ref[0, h] = s0
            sc = _dg(_bf(qr), _bf(kr), 1, 1) * md_ref[h]
            r = _dg(_bf(sc), vb, 1, 0) + _dg(_bf(qr * qd_ref[h]), _bf(s0), 1, 0)
            state[h] = s0 * cd_ref[h, 0:1, :] + _dg(_bf(kr * kd_ref[h]), vb, 0, 0)
            y, _ = _norm_stats(r)
            rg = g_ref[:, sl]
            out_ref[:, sl] = rg * _sigmoid(rg) * (y * gng_ref[:, sl] + gnb_ref[:, sl])

    return pl.pallas_call(
        body, name="ret_fwd", grid=(N,),
        out_shape=(jax.ShapeDtypeStruct((S, RET_W), F32), jax.ShapeDtypeStruct((N, 4, CHUNK, CHUNK), F32)),
        in_specs=[col(0), col(1), col(2), col(3), tab, tab, dec, dec, dec, cdec, vec, vec],
        out_specs=(pl.BlockSpec((CHUNK, 512), lambda n: (n, 0)), st),
        scratch_shapes=[pltpu.VMEM((4, CHUNK, CHUNK), F32)],
        compiler_params=_cparams(("arbitrary",)),
    )(proj, proj, proj, proj, *tables, gn_g, gn_b)


def _ret_bwd(proj, tables, gn_g, gn_b, states, d_out):
    S = proj.shape[0]
    N = S // CHUNK
    col, tab, dec, cdec, vec, st = _ret_specs(True, N)

    def kernel_body(q_ref, k_ref, v_ref, g_ref, cos_ref, sin_ref, md_ref, qd_ref, kd_ref, cd_ref, gng_ref, gnb_ref,
                    st_ref, do_ref, dp_ref, dg_ref, db_ref, gstate):
        @pl.when(pl.program_id(0) == 0)
        def _():
            gstate[...] = jnp.zeros_like(gstate)
            dg_ref[...] = jnp.zeros_like(dg_ref)
            db_ref[...] = jnp.zeros_like(db_ref)

        cosf, sinf = cos_ref[...], sin_ref[...]
        for h in range(4):
            sl = slice(h * 128, (h + 1) * 128)
            qr = _rot(q_ref[:, sl], cosf, sinf)
            kr = _rot(k_ref[:, sl], cosf, sinf) * (128 ** -0.5)
            qb, kb, vb = _bf(qr), _bf(kr), _bf(v_ref[:, sl])
            s0b = _bf(st_ref[0, h])
            md, qd, kd = md_ref[h], qd_ref[h], kd_ref[h]
            scb = _bf(_dg(qb, kb, 1, 1) * md)
            qdb = _bf(qr * qd)
            kdb = _bf(kr * kd)
            r = _dg(scb, vb, 1, 0) + _dg(qdb, s0b, 1, 0)
            y, rstd = _norm_stats(r)
            gng = gng_ref[:, sl]
            gn = y * gng + gnb_ref[:, sl]
            rg = g_ref[:, sl]
            sg = _sigmoid(rg)
            d_o = do_ref[:, sl]
            d_gn = d_o * (rg * sg)
            dg_ref[:, sl] += jnp.sum(d_gn * y, axis=0, keepdims=True)
            db_ref[:, sl] += jnp.sum(d_gn, axis=0, keepdims=True)
            drb = _bf(_norm_bwd(d_gn * gng, y, rstd))
            g0 = gstate[h]
            gb = _bf(g0)
            dscb = _bf(_dg(drb, vb, 1, 1) * md)
            dqr = _dg(dscb, kb, 1, 0) + _dg(drb, s0b, 1, 1) * qd
            dkr = _dg(dscb, qb, 0, 0) + _dg(vb, gb, 1, 1) * kd
            dv = _dg(scb, drb, 0, 0) + _dg(kdb, gb, 1, 0)
            gstate[h] = g0 * cd_ref[h, 0:1, :] + _dg(qdb, drb, 0, 0)
            dp_ref[:, 0 * 512 + h * 128:0 * 512 + (h + 1) * 128] = _bf(_rot_t(dqr, cosf, sinf))
            dp_ref[:, 1 * 512 + h * 128:1 * 512 + (h + 1) * 128] = _bf(_rot_t(dkr, cosf, sinf) * (128 ** -0.5))
            dp_ref[:, 2 * 512 + h * 128:2 * 512 + (h + 1) * 128] = _bf(dv)
            dp_ref[:, 3 * 512 + h * 128:3 * 512 + (h + 1) * 128] = _bf(d_o * gn * (sg * (1.0 + rg * (1.0 - sg))))

    acc = pl.BlockSpec((1, 512), lambda n: (0, 0))
    return pl.pallas_call(
        kernel_body, name="ret_bwd", grid=(N,),
        out_shape=(jax.ShapeDtypeStruct((S, 2048), BF16), jax.ShapeDtypeStruct((1, 512), F32),
                   jax.ShapeDtypeStruct((1, 512), F32)),
        in_specs=[col(0), col(1), col(2), col(3), tab, tab, dec, dec, dec, cdec, vec, vec, st,
                  pl.BlockSpec((CHUNK, 512), lambda n: (N - 1 - n, 0))],
        out_specs=(pl.BlockSpec((CHUNK, 2048), lambda n: (N - 1 - n, 0)), acc, acc),
        scratch_shapes=[pltpu.VMEM((4, CHUNK, CHUNK), F32)],
        compiler_params=_cparams(("arbitrary",)),
    )(proj, proj, proj, proj, *tables, gn_g, gn_b, states, d_out)


SB_T = 256
SB_SCALE = 64 ** -0.5
SB_Q_COL, SB_K_COL, SB_V_COL = 2048 // 128, 2560 // 128, 3072 // 128


def _head_masks():
    lane = lax.broadcasted_iota(jnp.int32, (1, 128), 1)
    m0 = (lane < 64).astype(F32)
    return m0, 1.0 - m0


def _tri(n, cmp):
    r = lax.broadcasted_iota(jnp.int32, (n, n), 0)
    c = lax.broadcasted_iota(jnp.int32, (n, n), 1)
    return cmp(r, c)


def _tri_sum(x, tri):
    hi = _bf(x)
    lo = _bf(x - hi.astype(F32))
    return _dg(hi, tri, 1, 0) + _dg(lo, tri, 1, 0)


def _sb_weights(qms, kblks, upper, carry, causal):
    tiles = [(b, h) for b in range(len(kblks)) for h in range(2)]
    zs = [_dg(qms[h], kblks[b], 1, 1) for b, h in tiles]
    lgs = [-(jnp.maximum(z, 0.0) + jnp.log(1.0 + jnp.exp(-jnp.abs(z)))) for z in zs]
    if causal is not None:
        lgs = [jnp.where(causal, lg, 0.0) for lg in lgs]
    carries = list(carry)
    for t in range(len(tiles) - 2):
        carries.append(carries[t] + jnp.sum(lgs[t], axis=1, keepdims=True))
    his = [_bf(lg) for lg in lgs]
    los = [_bf(lg - hi.astype(F32)) for lg, hi in zip(lgs, his)]
    later = [_dg(hi, upper, 1, 0) for hi in his]
    later = [r + _dg(lo, upper, 1, 0) for r, lo in zip(later, los)]
    a = [jnp.exp(lg + z + (r + c)) for lg, z, r, c in zip(lgs, zs, later, carries)]
    if causal is not None:
        a = [jnp.where(causal, x, 0.0) for x in a]
    out = tuple(carries[t] + jnp.sum(lgs[t], axis=1, keepdims=True) for t in (len(tiles) - 2, len(tiles) - 1))
    return [a[2 * b:2 * b + 2] for b in range(len(kblks))], out


def _sb_fwd(proj, comm=None):
    S = proj.shape[0]
    T = min(SB_T, S)
    nq = S // T

    def body(q_ref, k_ref, v_ref, o_ref, kb_ref, vm_ref, acc_ref):
        i = pl.program_id(1)
        m0, m1 = _head_masks()

        @pl.when(i == 0)
        def _():
            v = v_ref[...]
            kb_ref[...] = _bf(k_ref[...])
            vm_ref[0] = _bf(v * m0)
            vm_ref[1] = _bf(v * m1)

        q = q_ref[...]
        qm = (_bf(q * (m0 * SB_SCALE)), _bf(q * (m1 * SB_SCALE)))
        upper = _tri(T, lambda r, c: r > c).astype(BF16)
        causal = _tri(T, lambda r, c: c < r)

        def tiles(js, carry, mask, first):
            ks = [pl.multiple_of(j * T, T) for j in js]
            a, out = _sb_weights(qm, [kb_ref[pl.ds(k, T), :] for k in ks], upper, carry, mask)
            parts = [_dg(_bf(a[b][h]), vm_ref[h, pl.ds(k, T), :], 1, 0) for b, k in enumerate(ks) for h in range(2)]
            part = functools.reduce(lambda u, w: u + w, parts)
            if first:
                acc_ref[...] = part
            else:
                acc_ref[...] += part
            return out

        zero = jnp.zeros((T, 1), F32)
        carry = tiles([i], (zero, zero), causal, True)
        carry = lax.fori_loop(0, i % 2, lambda _, c: tiles([i - 1], c, None, False), carry)
        top = i - 1 - i % 2
        lax.fori_loop(0, i // 2, lambda jj, c: tiles([top - 2 * jj, top - 2 * jj - 1], c, None, False), carry)
        o_ref[...] = acc_ref[...]

    full = lambda c: pl.BlockSpec((S, 128), lambda p, i, c=c: (0, c + p))
    outs, landed = _pcall(
        body, name="sb_fwd", grid=(4, nq), out_shape=[jax.ShapeDtypeStruct((S, SB_W), F32)],
        in_specs=[pl.BlockSpec((T, 128), lambda p, i: (i, SB_Q_COL + p)), full(SB_K_COL), full(SB_V_COL)],
        out_specs=[pl.BlockSpec((T, 128), lambda p, i: (i, p))],
        scratch_shapes=[pltpu.VMEM((S, 128), BF16), pltpu.VMEM((2, S, 128), BF16), pltpu.VMEM((T, 128), F32)],
        sem=("arbitrary", "arbitrary"), args=(proj, proj, proj), comm=comm)
    return outs[0] if comm is None else (outs[0], landed)


def _sb_bwd(proj, d_o, comm=None):
    S = proj.shape[0]
    T = min(SB_T, S)
    nq = S // T

    def body(q_ref, k_ref, v_ref, do_ref, dq_ref, dk_ref, dv_ref, kb_ref, kbm_ref, vb_ref, e_ref, dq_acc, dk_acc, dv_acc):
        i = pl.program_id(1)
        m0, m1 = _head_masks()

        @pl.when(i == 0)
        def _():
            k = k_ref[...]
            kb_ref[...] = _bf(k)
            kbm_ref[0] = _bf(k * m0)
            kbm_ref[1] = _bf(k * m1)
            vb_ref[...] = _bf(v_ref[...])
            dk_acc[...] = jnp.zeros_like(dk_acc)
            dv_acc[...] = jnp.zeros_like(dv_acc)

        q, d_out = q_ref[...], do_ref[...]
        qm = (_bf(q * (m0 * SB_SCALE)), _bf(q * (m1 * SB_SCALE)))
        dom = (_bf(d_out * m0), _bf(d_out * m1))
        upper = _tri(T, lambda r, c: r > c).astype(BF16)
        lower = _tri(T, lambda r, c: r < c).astype(BF16)
        causal = _tri(T, lambda r, c: c < r)

        def down(js, carry, mask):
            ks = [pl.multiple_of(j * T, T) for j in js]
            a, out = _sb_weights(qm, [kb_ref[pl.ds(k, T), :] for k in ks], upper, carry, mask)
            da = [[_dg(dom[h], vb_ref[pl.ds(k, T), :], 1, 1) for h in range(2)] for k in ks]
            for b, (j, k) in enumerate(zip(js, ks)):
                for h in range(2):
                    e_ref[h, j] = a[b][h] * da[b][h]
                dv_acc[pl.ds(k, T), :] += _dg(_bf(a[b][0]), dom[0], 0, 0) + _dg(_bf(a[b][1]), dom[1], 0, 0)
            return out

        def up(js, carry, mask):
            ks = [pl.multiple_of(j * T, T) for j in js]
            tiles = [(b, h) for b in range(len(js)) for h in range(2)]
            zs = [_dg(qm[h], kb_ref[pl.ds(ks[b], T), :], 1, 1) for b, h in tiles]
            es = [e_ref[h, js[b]] for b, h in tiles]
            carries = list(carry)
            for t in range(len(tiles)):
                carries.append(carries[t] + jnp.sum(es[t], axis=1, keepdims=True))
            his = [_bf(e) for e in es]
            los = [_bf(e - hi.astype(F32)) for e, hi in zip(es, his)]
            d_lg = [_dg(hi, lower, 1, 0) for hi in his]
            d_lg = [r + _dg(lo, lower, 1, 0) + c for r, lo, c in zip(d_lg, los, carries)]
            ens = [jnp.exp(-jnp.abs(z)) for z in zs]
            invs = [1.0 / (1.0 + en) for en in ens]
            betas = [jnp.where(z >= 0.0, inv, en * inv) for z, en, inv in zip(zs, ens, invs)]
            dzs = [e * (1.0 - b) - d * b for e, b, d in zip(es, betas, d_lg)]
            if mask is not None:
                dzs = [jnp.where(mask, dz, 0.0) for dz in dzs]
            dzs = [_bf(dz) for dz in dzs]
            parts = [_dg(dzs[t], kbm_ref[h, pl.ds(ks[b], T), :], 1, 0) for t, (b, h) in enumerate(tiles)]
            dq_acc[...] += functools.reduce(lambda u, w: u + w, parts)
            for b, k in enumerate(ks):
                dk_acc[pl.ds(k, T), :] += _dg(dzs[2 * b], qm[0], 0, 0) + _dg(dzs[2 * b + 1], qm[1], 0, 0)
            return tuple(carries[-2:])

        zero = jnp.zeros((T, 1), F32)
        carry = down([i], (zero, zero), causal)
        carry = lax.fori_loop(0, i % 2, lambda _, c: down([i - 1], c, None), carry)
        top = i - 1 - i % 2
        lax.fori_loop(0, i // 2, lambda jj, c: down([top - 2 * jj, top - 2 * jj - 1], c, None), carry)

        dq_acc[...] = jnp.zeros_like(dq_acc)
        carry = lax.fori_loop(0, i // 2, lambda jj, c: up([2 * jj, 2 * jj + 1], c, None), (zero, zero))
        carry = lax.fori_loop(0, i % 2, lambda _, c: up([i - 1], c, None), carry)
        up([i], carry, causal)
        dq_ref[...] = _bf(dq_acc[...] * SB_SCALE)

        @pl.when(i == nq - 1)
        def _():
            dk_ref[...] = _bf(dk_acc[...])
            dv_ref[...] = _bf(dv_acc[...])

    full = lambda c: pl.BlockSpec((S, 128), lambda p, i, c=c: (0, c + p))
    tile = pl.BlockSpec((T, 128), lambda p, i: (i, p))
    acc = pl.BlockSpec((S, 128), lambda p, i: (0, p))
    out = jax.ShapeDtypeStruct((S, SB_W), BF16)
    outs, landed = _pcall(
        body, name="sb_bwd", grid=(4, nq), out_shape=[out, out, out],
        in_specs=[pl.BlockSpec((T, 128), lambda p, i: (i, SB_Q_COL + p)), full(SB_K_COL), full(SB_V_COL), tile],
        out_specs=[tile, acc, acc],
        scratch_shapes=[pltpu.VMEM((S, 128), BF16), pltpu.VMEM((2, S, 128), BF16), pltpu.VMEM((S, 128), BF16),
                        pltpu.VMEM((2, nq, T, T), F32), pltpu.VMEM((T, 128), F32), pltpu.VMEM((S, 128), F32),
                        pltpu.VMEM((S, 128), F32)],
        sem=("arbitrary", "arbitrary"), args=(proj, proj, proj, d_o), comm=comm)
    return tuple(outs) if comm is None else (tuple(outs), landed)


SGU_U_COL, SGU_V_COL = 3584 // 512, 4096 // 512


def _causal(w):
    r = lax.broadcasted_iota(jnp.int32, (CHUNK, CHUNK), 0)
    c = lax.broadcasted_iota(jnp.int32, (CHUNK, CHUNK), 1)
    return jnp.where(r >= c, w, 0.0)


def _sgu_fwd(proj, ln_g, ln_b, w, b):
    S = proj.shape[0]
    N = S // CHUNK

    def body(u_ref, v_ref, g_ref, b_ref, w_ref, bias_ref, out_ref):
        u = _gelu(u_ref[...])
        xh, _ = _norm_stats(_gelu(v_ref[...]))
        vn = _bf(xh * g_ref[...] + b_ref[...])
        for g in range(4):
            sl = slice(g * 128, (g + 1) * 128)
            sv = _dg(_bf(_causal(w_ref[g])), vn[:, sl], 1, 0) + bias_ref[g]
            out_ref[:, sl] = u[:, sl] * sv

    vec = pl.BlockSpec((1, 512), lambda n: (0, 0))
    return pl.pallas_call(
        body, name="sgu_fwd", grid=(N,),
        out_shape=jax.ShapeDtypeStruct((S, SGU_W), F32),
        in_specs=[pl.BlockSpec((CHUNK, 512), lambda n: (n, SGU_U_COL)),
                  pl.BlockSpec((CHUNK, 512), lambda n: (n, SGU_V_COL)), vec, vec,
                  pl.BlockSpec((4, CHUNK, CHUNK), lambda n: (0, 0, 0)), pl.BlockSpec((4, CHUNK, 1), lambda n: (0, 0, 0))],
        out_specs=pl.BlockSpec((CHUNK, 512), lambda n: (n, 0)),
        compiler_params=_cparams(("parallel",)),
    )(proj, proj, ln_g, ln_b, w, b)


def _sgu_bwd(proj, ln_g, ln_b, w, b, d_out):
    S = proj.shape[0]
    N = S // CHUNK

    def body(u_ref, v_ref, g_ref, b_ref, w_ref, bias_ref, do_ref, dp_ref, dg_ref, db_ref, dw_ref, dbias_ref):
        @pl.when(pl.program_id(0) == 0)
        def _():
            dg_ref[...] = jnp.zeros_like(dg_ref)
            db_ref[...] = jnp.zeros_like(db_ref)
            dw_ref[...] = jnp.zeros_like(dw_ref)
            dbias_ref[...] = jnp.zeros_like(dbias_ref)

        gu, gv = u_ref[...], v_ref[...]
        u = _gelu(gu)
        xh, rstd = _norm_stats(_gelu(gv))
        ln_gain = g_ref[...]
        vn = _bf(xh * ln_gain + b_ref[...])
        d_o = do_ref[...]
        d_vn = []
        for g in range(4):
            sl = slice(g * 128, (g + 1) * 128)
            wc = _bf(_causal(w_ref[g]))
            sv = _dg(wc, vn[:, sl], 1, 0) + bias_ref[g]
            dp_ref[:, sl] = _bf(d_o[:, sl] * sv * _gelu_grad(gu[:, sl]))
            d_sv = d_o[:, sl] * u[:, sl]
            dbias_ref[g] += jnp.sum(d_sv, axis=1, keepdims=True)
            d_svb = _bf(d_sv)
            dw_ref[g] += _causal(_dg(d_svb, vn[:, sl], 1, 1))
            d_vn.append(_dg(wc, d_svb, 0, 0))
        d_vn = jnp.concatenate(d_vn, axis=1)
        dg_ref[...] += jnp.sum(d_vn * xh, axis=0, keepdims=True)
        db_ref[...] += jnp.sum(d_vn, axis=0, keepdims=True)
        dp_ref[:, 512:1024] = _bf(_norm_bwd(d_vn * ln_gain, xh, rstd) * _gelu_grad(gv))

    vec = pl.BlockSpec((1, 512), lambda n: (0, 0))
    wspec = pl.BlockSpec((4, CHUNK, CHUNK), lambda n: (0, 0, 0))
    bspec = pl.BlockSpec((4, CHUNK, 1), lambda n: (0, 0, 0))
    return pl.pallas_call(
        body, name="sgu_bwd", grid=(N,),
        out_shape=(jax.ShapeDtypeStruct((S, 1024), BF16), jax.ShapeDtypeStruct((1, 512), F32),
                   jax.ShapeDtypeStruct((1, 512), F32), jax.ShapeDtypeStruct((4, CHUNK, CHUNK), F32),
                   jax.ShapeDtypeStruct((4, CHUNK, 1), F32)),
        in_specs=[pl.BlockSpec((CHUNK, 512), lambda n: (n, SGU_U_COL)),
                  pl.BlockSpec((CHUNK, 512), lambda n: (n, SGU_V_COL)), vec, vec, wspec, bspec,
                  pl.BlockSpec((CHUNK, 512), lambda n: (n, 0))],
        out_specs=(pl.BlockSpec((CHUNK, 1024), lambda n: (n, 0)), vec, vec, wspec, bspec),
        compiler_params=_cparams(("arbitrary",)),
    )(proj, proj, ln_g, ln_b, w, b, d_out)


GATE_COL = 4608 // 512


def _merge_fwd(proj, branches, p_list, tm=512):
    S = proj.shape[0]
    tm = min(tm, S)

    def body(r_ref, s_ref, g_ref, pr_ref, ps_ref, pg_ref, gr_ref, gs_ref, gg_ref, m_ref, br_ref):
        acc = None
        for k, (x_ref, p_ref, gate_ref) in enumerate(((r_ref, pr_ref, gr_ref), (s_ref, ps_ref, gs_ref),
                                                      (g_ref, pg_ref, gg_ref))):
            br = _dg(_bf(x_ref[...]), _bf(p_ref[...]), 1, 0)
            br_ref[k] = br
            term = _sigmoid(gate_ref[...]) * br
            acc = term if acc is None else acc + term
        m_ref[...] = _bf(acc)

    xs = pl.BlockSpec((tm, 512), lambda i, n: (i, 0))
    ps = pl.BlockSpec((512, 512), lambda i, n: (0, n))
    gate = lambda k: pl.BlockSpec((tm, 512), lambda i, n, k=k: (i, GATE_COL + 2 * k + n))
    return pl.pallas_call(
        body, name="merge_fwd", grid=(S // tm, 2),
        out_shape=(jax.ShapeDtypeStruct((S, D_MODEL), BF16), jax.ShapeDtypeStruct((3, S, D_MODEL), F32)),
        in_specs=[xs, xs, xs, ps, ps, ps, gate(0), gate(1), gate(2)],
        out_specs=(pl.BlockSpec((tm, 512), lambda i, n: (i, n)), pl.BlockSpec((3, tm, 512), lambda i, n: (0, i, n))),
        compiler_params=_cparams(("parallel", "parallel")),
    )(*branches, *p_list, proj, proj, proj)


def _gate_bwd(proj, br, d_merged, tm=512):
    S = proj.shape[0]
    tm = min(tm, S)

    def body(dm_ref, br_ref, gr_ref, gs_ref, gg_ref, dbr_ref, dgate_ref):
        dm = dm_ref[...]
        for k, gate_ref in enumerate((gr_ref, gs_ref, gg_ref)):
            s = _sigmoid(gate_ref[...])
            dbr_ref[k] = _bf(dm * s)
            dgate_ref[k] = _bf(dm * br_ref[k] * (s * (1.0 - s)))

    gate = lambda k: pl.BlockSpec((tm, 512), lambda i, n, k=k: (i, GATE_COL + 2 * k + n))
    three = pl.BlockSpec((3, tm, 512), lambda i, n: (0, i, n))
    return pl.pallas_call(
        body, name="gate_bwd", grid=(S // tm, 2),
        out_shape=(jax.ShapeDtypeStruct((3, S, D_MODEL), BF16), jax.ShapeDtypeStruct((3, S, D_MODEL), BF16)),
        in_specs=[pl.BlockSpec((tm, 512), lambda i, n: (i, n)), three, gate(0), gate(1), gate(2)],
        out_specs=(three, three),
        compiler_params=_cparams(("parallel", "parallel")),
    )(d_merged, br, proj, proj, proj)


def _ln_bwd(dy, u, g, tm=256):
    S, D = u.shape
    tm = min(tm, S)

    def body(dy_ref, u_ref, g_ref, du_ref, dub_ref, dg_ref, db_ref):
        @pl.when(pl.program_id(0) == 0)
        def _():
            dg_ref[...] = jnp.zeros_like(dg_ref)
            db_ref[...] = jnp.zeros_like(db_ref)

        dy_t = dy_ref[...]
        xh, rstd = _norm_stats(u_ref[...])
        dg_ref[...] += jnp.sum(dy_t * xh, axis=0, keepdims=True)
        db_ref[...] += jnp.sum(dy_t, axis=0, keepdims=True)
        du = _norm_bwd(dy_t * g_ref[...], xh, rstd)
        du_ref[...] = du
        dub_ref[...] = _bf(du)

    tile = pl.BlockSpec((tm, D), lambda i: (i, 0))
    vec = pl.BlockSpec((1, D), lambda i: (0, 0))
    return pl.pallas_call(
        body, name="ln_bwd", grid=(S // tm,),
        out_shape=(jax.ShapeDtypeStruct((S, D), F32), jax.ShapeDtypeStruct((S, D), BF16),
                   jax.ShapeDtypeStruct((1, D), F32), jax.ShapeDtypeStruct((1, D), F32)),
        in_specs=[tile, tile, vec], out_specs=(tile, tile, vec, vec),
        compiler_params=_cparams(("arbitrary",)),
    )(dy, u, g)


def _loss_grad(y, target, tm=256):
    S, D = y.shape
    tm = min(tm, S)

    def body(y_ref, t_ref, dy_ref, sq_ref):
        @pl.when(pl.program_id(0) == 0)
        def _():
            sq_ref[...] = jnp.zeros_like(sq_ref)

        err = y_ref[...] - t_ref[...]
        dy_ref[...] = err * (1.0 / D)
        sq_ref[...] += jnp.sum(err * err, axis=0, keepdims=True)

    tile = pl.BlockSpec((tm, D), lambda i: (i, 0))
    vec = pl.BlockSpec((1, D), lambda i: (0, 0))
    return pl.pallas_call(
        body, name="loss_grad", grid=(S // tm,),
        out_shape=(jax.ShapeDtypeStruct((S, D), F32), jax.ShapeDtypeStruct((1, D), F32)),
        in_specs=[tile, tile], out_specs=(tile, vec),
        compiler_params=_cparams(("arbitrary",)),
    )(y, target)


def _layer_fwd(x, x_bf, W, tables, sb_comm=None):
    proj = _matmul(x_bf, W["w_in_t"], "nt", name="proj", tm=1024, tn=768, tk=1024)
    retg, states = _ret_fwd(proj, tables, W["ret_gn_g"], W["ret_gn_b"])
    if sb_comm is None:
        sb = _sb_fwd(proj)
    else:
        sb, landed = _sb_fwd(proj, comm=sb_comm[0])
        sb_comm[1](landed)
    sg = _sgu_fwd(proj, W["sgu_ln_g"], W["sgu_ln_b"], W["sgu_w"], W["sgu_b"])
    merged, br = _merge_fwd(proj, (retg, sb, sg), (W["p_ret"], W["p_sb"], W["p_sgu"]))
    u1, x1, x1_bf = _matmul(merged, W["w_out"], "nn", name="out_ln", tm=512, tn=1024, tk=1024, epi="ln",
                            extra=(x, W["ln1_g"], W["ln1_b"]))
    act = _matmul(x1_bf, W["w_up"], "nn", name="up", tm=1024, tn=1024, tk=1024, epi="relu2")
    u2, x2, x2_bf = _matmul(act, W["w_down"], "nn", name="down_ln", tm=512, tn=1024, tk=4096, epi="ln",
                            extra=(x1, W["ln2_g"], W["ln2_b"]))
    saved = dict(x_bf=x_bf, proj=proj, retg=retg, states=states, sb=sb, sg=sg, merged=merged, br=br, u1=u1,
                 x1_bf=x1_bf, act=act, u2=u2)
    return x2, x2_bf, saved


def _layer_bwd(d_x2, W, tables, sv, chunk_dtype=None, sb_comm_fn=None, dx_comm_fn=None):
    dt = F32 if chunk_dtype is None else chunk_dtype
    rows, cols = (None, None) if chunk_dtype is None else ("rows", "cols")
    g, landed = {}, {}
    du2, du2_bf, g["ln2_g"], g["ln2_b"] = _ln_bwd(d_x2, sv["u2"], W["ln2_g"])
    d_hpre = _matmul(du2_bf, W["w_down"], "nt", name="d_act", tm=1024, tn=1024, tk=1024, epi="drelu2",
                     extra=(sv["act"],), out_dtype=BF16)
    g["w_down"] = _matmul(sv["act"], du2_bf, "tn", name="dw_down", tm=512, tn=1024, tk=4096, out_dtype=dt, chunks=rows)
    g["w_up"] = _matmul(sv["x1_bf"], d_hpre, "tn", name="dw_up", tm=1024, tn=512, tk=4096, out_dtype=dt, chunks=cols)
    d_x1 = _matmul(d_hpre, W["w_up"], "nt", name="d_x1", tm=512, tn=1024, tk=4096, epi="add", extra=(du2,))
    du1, du1_bf, g["ln1_g"], g["ln1_b"] = _ln_bwd(d_x1, sv["u1"], W["ln1_g"])
    d_merged = _matmul(du1_bf, W["w_out"], "nt", name="d_merged", tm=1024, tn=1024, tk=1024)
    g["w_out"] = _matmul(sv["merged"], du1_bf, "tn", name="dw_out", tm=1024, tn=512, tk=4096, out_dtype=dt, chunks=rows)
    d_br, d_gate = _gate_bwd(sv["proj"], sv["br"], d_merged)
    d_branch = []
    for k, (nm, act) in enumerate((("p_ret", sv["retg"]), ("p_sb", sv["sb"]), ("p_sgu", sv["sg"]))):
        d_branch.append(_matmul(d_br[k], W[nm], "nt", name="d_" + nm[2:], tm=1024, tn=512, tk=1024))
        g[nm] = _matmul(act, d_br[k], "tn", name="dw_" + nm[2:], tm=512, tn=1024, tk=2048, out_dtype=dt, chunks=cols)
    d_ret, g["ret_gn_g"], g["ret_gn_b"] = _ret_bwd(sv["proj"], tables, W["ret_gn_g"], W["ret_gn_b"], sv["states"],
                                                   d_branch[0])
    if sb_comm_fn is None:
        d_sq, d_sk, d_sv = _sb_bwd(sv["proj"], d_branch[1])
    else:
        (d_sq, d_sk, d_sv), landed["sb"] = _sb_bwd(sv["proj"], d_branch[1], comm=sb_comm_fn(g))
    d_sgu, g["sgu_ln_g"], g["sgu_ln_b"], g["sgu_w"], g["sgu_b"] = _sgu_bwd(
        sv["proj"], W["sgu_ln_g"], W["sgu_ln_b"], W["sgu_w"], W["sgu_b"], d_branch[2])
    d_proj = jnp.concatenate([d_ret, d_sq, d_sk, d_sv, d_sgu, d_gate[0], d_gate[1], d_gate[2]], axis=1)
    g["w_in"] = _matmul(d_proj, sv["x_bf"], "tn", name="dw_in", tm=768, tn=1024, tk=4096, out_dtype=dt, chunks=rows)
    if chunk_dtype is None:
        g["w_in"] = g["w_in"].T
    d_x = _matmul(d_proj, W["w_in_t"], "nn", name="d_x", tm=512, tn=1024, tk=3840, epi="add", extra=(du1,),
                  comm=None if dx_comm_fn is None else dx_comm_fn(g))
    if dx_comm_fn is not None:
        d_x, landed["dx"] = d_x
    return d_x, g, landed


BIG = ("w_in", "p_ret", "p_sb", "p_sgu", "w_out", "w_up", "w_down")
SMALL = ("ret_gn_g", "ret_gn_b", "sgu_ln_g", "sgu_ln_b", "sgu_w", "sgu_b", "ln1_g", "ln1_b", "ln2_g", "ln2_b")
GATHER_KIND = {"w_in": "rows", "p_ret": "cols", "p_sb": "cols", "p_sgu": "cols", "w_out": "rows", "w_up": "cols",
               "w_down": "rows"}


def _small_weights(small, l):
    W = {}
    for n in SMALL:
        if n == "sgu_w":
            W[n] = small[n][l]
        elif n == "sgu_b":
            W[n] = small[n][l].reshape(4, CHUNK, 1)
        else:
            W[n] = small[n][l].reshape(1, -1)
    return W


def _local_step(x, target, full, small):
    tables = _ret_tables(x.shape[0])
    Ws = [{**{n: full[n][l] for n in BIG[1:]}, "w_in_t": full["w_in"][l].T, **_small_weights(small, l)}
          for l in range(DEPTH)]
    saved = []
    h, h_bf = x, _bf(x)
    for l in range(DEPTH):
        h, h_bf, sv = _layer_fwd(h, h_bf, Ws[l], tables)
        saved.append(sv)
    d_h, sq = _loss_grad(h, target)
    grads = [None] * DEPTH
    for l in reversed(range(DEPTH)):
        d_h, grads[l], _ = _layer_bwd(d_h, Ws[l], tables, saved[l])
    return sq, d_h, grads


def _adam(w, parts, m, v, name):
    L, R, C = w.shape
    tr = next(t for t in (320, 256, 128) if R % t == 0)
    assert len(parts) == L

    def body(*refs):
        w_ref, p_refs, (m_ref, v_ref, g_ref, d_ref, nm_ref, nv_ref) = refs[0], refs[1:1 + L], refs[1 + L:]
        layer = pl.program_id(0)
        g = None
        for li, p_ref in enumerate(p_refs):
            s = p_ref[0].astype(F32)
            for j in range(1, p_ref.shape[0]):
                s = s + p_ref[j].astype(F32)
            g = s if g is None else jnp.where(layer == li, s, g)
        g_ref[...] = g
        d_ref[...], nm_ref[...], nv_ref[...] = _adam_update(w_ref[...], g, m_ref[...], v_ref[...])

    tile = pl.BlockSpec((None, tr, C), lambda l, i: (l, i, 0))
    part = lambda li: pl.BlockSpec((parts[li].shape[0], tr, C), lambda l, i, li=li: (0, jnp.where(l == li, i, 0), 0))
    out = jax.ShapeDtypeStruct((L, R, C), F32)
    return pl.pallas_call(
        body, name=name, grid=(L, R // tr), out_shape=(out, out, out, out),
        in_specs=[tile] + [part(li) for li in range(L)] + [tile, tile],
        out_specs=(tile, tile, tile, tile),
        compiler_params=_cparams(("parallel", "parallel")),
    )(w, *parts, m, v)


def _adam_update(w, g, m, v):
    m2 = ADAM_B1 * m + (1.0 - ADAM_B1) * g
    v2 = ADAM_B2 * v + (1.0 - ADAM_B2) * (g * g)
    m_hat = m2 / (1.0 - ADAM_B1 ** ADAM_STEP)
    v_hat = v2 / (1.0 - ADAM_B2 ** ADAM_STEP)
    return -ADAM_LR * (m_hat / (jnp.sqrt(v_hat) + ADAM_EPS) + ADAM_WD * w), m2, v2


def _adam_small(w, m, v, parts):
    k = len(SMALL)

    def body(*refs):
        w_refs, m_refs, v_refs, p_refs, outs = refs[:k], refs[k:2 * k], refs[2 * k:3 * k], refs[3 * k:5 * k], refs[5 * k:]
        for i in range(k):
            vector = len(w_refs[i].shape) == 2
            for l in range(DEPTH):
                p_ref = p_refs[DEPTH * i + l]
                g = p_ref[0]
                for j in range(1, N_DEV):
                    g = g + p_ref[j]
                at = (slice(l, l + 1), slice(None)) if vector else (l,)
                delta, m2, v2 = _adam_update(w_refs[i][at], g, m_refs[i][at], v_refs[i][at])
                for o_ref, val in zip(outs[4 * i:4 * i + 4], (g, delta, m2, v2)):
                    o_ref[at] = val

    vmem = pl.BlockSpec(memory_space=pltpu.VMEM)
    args = [w[n] for n in SMALL] + [m[n] for n in SMALL] + [v[n] for n in SMALL] + \
           [parts[(n, l)] for n in SMALL for l in range(DEPTH)]
    out_shape = [jax.ShapeDtypeStruct(w[n].shape, F32) for n in SMALL for _ in range(4)]
    outs = pl.pallas_call(body, name="adam_small", out_shape=out_shape, in_specs=[vmem] * len(args),
                          out_specs=[vmem] * len(out_shape), compiler_params=_cparams())(*args)
    return {n: tuple(outs[4 * i:4 * i + 4]) for i, n in enumerate(SMALL)}


WEIGHTS = ("w_in", "ret_gn_g", "ret_gn_b", "sgu_ln_g", "sgu_ln_b", "sgu_w", "sgu_b", "p_ret", "p_sb", "p_sgu", "w_out",
           "ln1_g", "ln1_b", "w_up", "w_down", "ln2_g", "ln2_b")


def kernel(x, w_in, ret_gn_g, ret_gn_b, sgu_ln_g, sgu_ln_b, sgu_w, sgu_b, p_ret, p_sb, p_sgu, w_out, ln1_g, ln1_b, w_up, w_down, ln2_g, ln2_b, loss_target, m_w_in, m_ret_gn_g, m_ret_gn_b, m_sgu_ln_g, m_sgu_ln_b, m_sgu_w, m_sgu_b, m_p_ret, m_p_sb, m_p_sgu, m_w_out, m_ln1_g, m_ln1_b, m_w_up, m_w_down, m_ln2_g, m_ln2_b, v_w_in, v_ret_gn_g, v_ret_gn_b, v_sgu_ln_g, v_sgu_ln_b, v_sgu_w, v_sgu_b, v_p_ret, v_p_sb, v_p_sgu, v_w_out, v_ln1_g, v_ln1_b, v_w_up, v_w_down, v_ln2_g, v_ln2_b):
    w = dict(zip(WEIGHTS, (w_in, ret_gn_g, ret_gn_b, sgu_ln_g, sgu_ln_b, sgu_w, sgu_b, p_ret, p_sb, p_sgu, w_out,
                           ln1_g, ln1_b, w_up, w_down, ln2_g, ln2_b)))
    m = dict(zip(WEIGHTS, (m_w_in, m_ret_gn_g, m_ret_gn_b, m_sgu_ln_g, m_sgu_ln_b, m_sgu_w, m_sgu_b, m_p_ret, m_p_sb,
                           m_p_sgu, m_w_out, m_ln1_g, m_ln1_b, m_w_up, m_w_down, m_ln2_g, m_ln2_b)))
    v = dict(zip(WEIGHTS, (v_w_in, v_ret_gn_g, v_ret_gn_b, v_sgu_ln_g, v_sgu_ln_b, v_sgu_w, v_sgu_b, v_p_ret, v_p_sb,
                           v_p_sgu, v_w_out, v_ln1_g, v_ln1_b, v_w_up, v_w_down, v_ln2_g, v_ln2_b)))

    small = {n: w[n] for n in SMALL}
    shard = {n: _bf(w[n]) for n in BIG}
    shard["w_in"] = shard["w_in"].transpose(0, 2, 1)
    S = x.shape[1]
    x0, target = x.reshape(S, D_MODEL), loss_target.reshape(S, D_MODEL)
    tables = _ret_tables(S)
    Ws = [_small_weights(small, l) for l in range(DEPTH)]

    (Ws[0]["w_in_t"],) = _exchange([_gather_transfer(shard["w_in"], 0, "rows")], "gather_w_in0", relay=True)
    later = [(n, 0) for n in BIG[1:]] + [(n, 1) for n in BIG]

    def weights_landed(landed):
        for (n, l), z in zip(later, landed):
            Ws[l]["w_in_t" if n == "w_in" else n] = z

    gather = _Comm([_gather_transfer(shard[n], l, GATHER_KIND[n]) for n, l in later], relay=True)
    h, h_bf, saved0 = _layer_fwd(x0, _bf(x0), Ws[0], tables, sb_comm=(gather, weights_landed))
    h, _, saved1 = _layer_fwd(h, h_bf, Ws[1], tables)
    d_h, sq = _loss_grad(h, target)
    loss = lax.psum(0.5 * jnp.sum(sq) / D_MODEL, ("x", "y", "c"))

    d_h, g1, _ = _layer_bwd(d_h, Ws[1], tables, saved1, chunk_dtype=BF16)
    early = [(n, 1) for n in BIG] + [(n, 0) for n in BIG[1:]]

    def early_scatter(g0):
        return _Comm([_scatter_transfer((g1 if l else g0)[n]) for n, l in early])

    small_keys = [(n, l) for n in SMALL for l in range(DEPTH)]

    def late_scatter(g0):
        pairs = _pair_reduce(g0["w_in"], "w_in0_pairs")
        mine = [(g1 if l else g0)[n] for n, l in small_keys]
        mine = [a.reshape(4, CHUNK) if n == "sgu_b" else a for a, (n, l) in zip(mine, small_keys)]
        return _Comm([_chip_scatter_transfer(pairs)] + [_slab_transfer(a) for a in mine])

    d_x, g0, landed = _layer_bwd(d_h, Ws[0], tables, saved0, chunk_dtype=BF16, sb_comm_fn=early_scatter,
                                 dx_comm_fn=late_scatter)
    parts = dict(zip(early, landed["sb"]))
    parts[("w_in", 0)] = landed["dx"][0]
    small_parts = dict(zip(small_keys, landed["dx"][1:]))

    grad, delta, new_m, new_v = {}, {}, {}, {}
    for n in BIG:
        view = (lambda a: a.transpose(0, 2, 1)) if n == "w_in" else (lambda a: a)
        res = _adam(view(w[n]), [parts[(n, l)] for l in range(DEPTH)], view(m[n]), view(v[n]), "adam_" + n)
        grad[n], delta[n], new_m[n], new_v[n] = (view(r) for r in res)
    for n, res in _adam_small(small, m, v, small_parts).items():
        grad[n], delta[n], new_m[n], new_v[n] = res

    return (loss, d_x.reshape(x.shape), *[grad[n] for n in WEIGHTS], *[delta[n] for n in WEIGHTS],
            *[new_m[n] for n in WEIGHTS], *[new_v[n] for n in WEIGHTS])
```

```python
import functools
import math

import numpy as np
import jax
import jax.numpy as jnp
from jax import lax
from jax.experimental import pallas as pl
from jax.experimental.pallas import tpu as pltpu

F32 = jnp.float32
BF16 = jnp.bfloat16

N_DEV = 8
DEPTH = 2
D_MODEL = 1024
CHUNK = 128
RET_W = 512
SB_W = 512
SGU_W = 512
N_IN = 7680
LN_EPS = 1e-5
ALPHA = (2 * DEPTH) ** 0.25
ROPE_BASE = 10000.0
ADAM_LR, ADAM_B1, ADAM_B2, ADAM_EPS, ADAM_WD, ADAM_STEP = 0.001, 0.9, 0.999, 1e-08, 0.01, 10
VMEM_LIMIT = 56 * 1024 * 1024

_GELU_K = math.sqrt(2.0 / math.pi)
_GELU_C = 0.044715


def _cparams(sem=None):
    return pltpu.CompilerParams(dimension_semantics=sem, vmem_limit_bytes=VMEM_LIMIT)


def _dg(a, b, ca, cb):
    return lax.dot_general(a, b, (((ca,), (cb,)), ((), ())), preferred_element_type=F32)


def _bf(x):
    return x.astype(BF16)


def _sigmoid(x):
    return 1.0 / (1.0 + jnp.exp(-x))


def _gelu(x):
    t = jnp.tanh(_GELU_K * (x + _GELU_C * (x * x * x)))
    return x * (0.5 * (1.0 + t))


def _gelu_grad(x):
    t = jnp.tanh(_GELU_K * (x + _GELU_C * (x * x * x)))
    return 0.5 * (1.0 + t) + 0.5 * x * (1.0 - t * t) * (_GELU_K * (1.0 + 3.0 * _GELU_C * x * x))


def _norm_stats(u):
    mu = jnp.mean(u, axis=-1, keepdims=True)
    d = u - mu
    var = jnp.mean(d * d, axis=-1, keepdims=True)
    rstd = lax.rsqrt(var + LN_EPS)
    return d * rstd, rstd


def _norm_bwd(dxh, xh, rstd):
    return rstd * (dxh - jnp.mean(dxh, axis=-1, keepdims=True) - xh * jnp.mean(dxh * xh, axis=-1, keepdims=True))


class _Transfer:
    def __init__(self, src, dst_shape, src_at, dst_at, same_core=False):
        self.src, self.dst_shape, self.src_at, self.dst_at = src, tuple(dst_shape), src_at, dst_at
        self.same_core = same_core


def _gather_transfer(shard, l, kind):
    _, r, c = shard.shape
    src_at = lambda ref, p: ref.at[l]
    if kind == "slab":
        return _Transfer(shard, (N_DEV, r, c), src_at, lambda ref, s: ref.at[s])
    if kind == "rows":
        return _Transfer(shard, (N_DEV * r, c), src_at, lambda ref, s: ref.at[pl.ds(pl.multiple_of(s * r, r), r), :])
    return _Transfer(shard, (r, N_DEV * c), src_at, lambda ref, s: ref.at[:, pl.ds(pl.multiple_of(s * c, c), c)])


def _scatter_transfer(chunks):
    return _Transfer(chunks, chunks.shape, lambda ref, p: ref.at[p], lambda ref, s: ref.at[s])


def _slab_transfer(arr):
    return _Transfer(arr, (N_DEV,) + arr.shape, lambda ref, p: ref, lambda ref, s: ref.at[s])


class _Comm:
    def __init__(self, transfers, relay=False):
        self.transfers = list(transfers)
        self.relay = relay
        self.n = len(self.transfers)
        self.arrays = [t.src for t in self.transfers]
        self.out_shape = [jax.ShapeDtypeStruct(t.dst_shape, t.src.dtype) for t in self.transfers]
        self.scratch = [pltpu.SemaphoreType.DMA((self.n * (N_DEV - 1),)), pltpu.SemaphoreType.DMA((self.n * (N_DEV - 1),)),
                        pltpu.SemaphoreType.DMA((self.n,))]

    def _relay_copies(self, srcs, dsts, send_sems, recv_sems, local_sems):
        x, y, c = lax.axis_index("x"), lax.axis_index("y"), lax.axis_index("c")
        me = 4 * x + 2 * y + c
        chips = [(1 - x, y), (x, 1 - y), (1 - x, 1 - y)]
        first, passed, own = [], [], []
        for t, tr in enumerate(self.transfers):
            def copy(k, src, sender, to, t=t, tr=tr):
                return pltpu.make_async_remote_copy(
                    src_ref=src, dst_ref=tr.dst_at(dsts[t], sender), send_sem=send_sems.at[t * (N_DEV - 1) + k],
                    recv_sem=recv_sems.at[t * (N_DEV - 1) + k], device_id=to, device_id_type=pl.DeviceIdType.MESH)

            mine = tr.src_at(srcs[t], me)
            first.append([copy(0, mine, me, (x, y, 1 - c))] + [copy(1 + j, mine, me, (px, py, c))
                                                                for j, (px, py) in enumerate(chips)])
            passed.append([copy(4 + j, tr.dst_at(dsts[t], 4 * px + 2 * py + c), 4 * px + 2 * py + c, (x, y, 1 - c))
                           for j, (px, py) in enumerate(chips)])
            own.append(pltpu.make_async_copy(mine, tr.dst_at(dsts[t], me), local_sems.at[t]))
        return first, passed, own

    def _copies(self, srcs, dsts, send_sems, recv_sems, local_sems):
        x, y, c = lax.axis_index("x"), lax.axis_index("y"), lax.axis_index("c")
        me = 4 * x + 2 * y + c
        copies = []
        for d in range(1, N_DEV):
            px = 1 - x if d & 4 else x
            py = 1 - y if d & 2 else y
            pc = 1 - c if d & 1 else c
            for t, tr in enumerate(self.transfers):
                if tr.same_core and d & 1:
                    continue
                peer, mine = (2 * px + py, 2 * x + y) if tr.same_core else (4 * px + 2 * py + pc, me)
                k = t * (N_DEV - 1) + d - 1
                copies.append(pltpu.make_async_remote_copy(
                    src_ref=tr.src_at(srcs[t], peer), dst_ref=tr.dst_at(dsts[t], mine),
                    send_sem=send_sems.at[k], recv_sem=recv_sems.at[k],
                    device_id=(px, py, pc), device_id_type=pl.DeviceIdType.MESH))
        own = []
        for t, tr in enumerate(self.transfers):
            mine = 2 * x + y if tr.same_core else me
            own.append(pltpu.make_async_copy(tr.src_at(srcs[t], mine), tr.dst_at(dsts[t], mine), local_sems.at[t]))
        return copies, own

    def start(self, srcs, dsts, *sems):
        if self.relay:
            first, _, own = self._relay_copies(srcs, dsts, *sems)
            for cp in own + [cp for per_t in first for cp in per_t]:
                cp.start()
            return
        copies, own = self._copies(srcs, dsts, *sems)
        for cp in own + copies:
            cp.start()

    def finish(self, srcs, dsts, *sems):
        if self.relay:
            first, passed, own = self._relay_copies(srcs, dsts, *sems)
            for j in range(3):
                for t in range(self.n):
                    first[t][1 + j].wait_recv()
                    passed[t][j].start()
            for t in range(self.n):
                first[t][0].wait_recv()
                for cp in passed[t]:
                    cp.wait_recv()
            for t in range(self.n):
                for cp in first[t] + passed[t]:
                    cp.wait_send()
                own[t].wait()
            return
        copies, own = self._copies(srcs, dsts, *sems)
        for cp in copies + own:
            cp.wait()


def _pcall(body, *, name, grid, in_specs, out_specs, out_shape, scratch_shapes, sem, args, comm=None):
    in_specs, out_specs, out_shape = list(in_specs), list(out_specs), list(out_shape)
    if comm is None:
        outs = pl.pallas_call(body, name=name, grid=grid, in_specs=in_specs, out_specs=out_specs, out_shape=out_shape,
                              scratch_shapes=list(scratch_shapes), compiler_params=_cparams(sem))(*args)
        return list(outs), []
    n_in, n_out, n_scr, k = len(in_specs), len(out_specs), len(scratch_shapes), comm.n

    def carrier(*refs):
        ins, cin = refs[:n_in], refs[n_in:n_in + k]
        outs, cout = refs[n_in + k:n_in + k + n_out], refs[n_in + k + n_out:n_in + 2 * k + n_out]
        scr, sems = refs[n_in + 2 * k + n_out:n_in + 2 * k + n_out + n_scr], refs[n_in + 2 * k + n_out + n_scr:]
        ids = [pl.program_id(d) for d in range(len(grid))]
        first = functools.reduce(jnp.logical_and, [i == 0 for i in ids])
        last = functools.reduce(jnp.logical_and, [i == g - 1 for i, g in zip(ids, grid)])

        @pl.when(first)
        def _():
            comm.start(cin, cout, *sems)

        body(*ins, *outs, *scr)

        @pl.when(last)
        def _():
            comm.finish(cin, cout, *sems)

    hbm = pl.BlockSpec(memory_space=pl.ANY)
    outs = pl.pallas_call(
        carrier, name=name, grid=grid, in_specs=in_specs + [hbm] * k, out_specs=out_specs + [hbm] * k,
        out_shape=out_shape + comm.out_shape, scratch_shapes=list(scratch_shapes) + comm.scratch,
        compiler_params=_cparams(tuple("arbitrary" for _ in grid)),
    )(*args, *comm.arrays)
    return list(outs[:n_out]), list(outs[n_out:])


def _exchange(transfers, name, relay=False):
    comm = _Comm(transfers, relay)

    def body(*refs):
        k = comm.n
        comm.start(refs[:k], refs[k:2 * k], *refs[2 * k:])
        comm.finish(refs[:k], refs[k:2 * k], *refs[2 * k:])

    hbm = pl.BlockSpec(memory_space=pl.ANY)
    return pl.pallas_call(body, name=name, out_shape=comm.out_shape, in_specs=[hbm] * comm.n, out_specs=[hbm] * comm.n,
                          scratch_shapes=comm.scratch)(*comm.arrays)


def _pair_reduce(chunks, name, tr=320):
    _, r, c = chunks.shape
    tr = min(tr, r)
    assert r % tr == 0

    def swap(src_ref, dst_ref, send_sems, recv_sems):
        x, y, core = lax.axis_index("x"), lax.axis_index("y"), lax.axis_index("c")
        copies = [pltpu.make_async_remote_copy(
            src_ref=src_ref.at[2 * k + 1 - core], dst_ref=dst_ref.at[k], send_sem=send_sems.at[k],
            recv_sem=recv_sems.at[k], device_id=(x, y, 1 - core), device_id_type=pl.DeviceIdType.MESH) for k in range(4)]
        for cp in copies:
            cp.start()
        for cp in copies:
            cp.wait()

    hbm = pl.BlockSpec(memory_space=pl.ANY)
    theirs = pl.pallas_call(swap, name=name + "_swap", out_shape=jax.ShapeDtypeStruct((4, r, c), chunks.dtype),
                            in_specs=[hbm], out_specs=hbm,
                            scratch_shapes=[pltpu.SemaphoreType.DMA((4,)), pltpu.SemaphoreType.DMA((4,))])(chunks)

    def add(mine_ref, theirs_ref, out_ref):
        core = lax.axis_index("c")
        both = mine_ref[...].astype(F32)
        out_ref[...] = (jnp.where(core == 0, both[0], both[1]) + theirs_ref[...].astype(F32)).astype(out_ref.dtype)

    return pl.pallas_call(
        add, name=name + "_add", grid=(4, r // tr), out_shape=jax.ShapeDtypeStruct((4, r, c), chunks.dtype),
        in_specs=[pl.BlockSpec((None, 2, tr, c), lambda k, i: (k, 0, i, 0)), pl.BlockSpec((None, tr, c), lambda k, i: (k, i, 0))],
        out_specs=pl.BlockSpec((None, tr, c), lambda k, i: (k, i, 0)),
        compiler_params=_cparams(("parallel", "parallel")),
    )(chunks.reshape(4, 2, r, c), theirs)


def _chip_scatter_transfer(pairs):
    return _Transfer(pairs, pairs.shape, lambda ref, p: ref.at[p], lambda ref, s: ref.at[s], same_core=True)


def _matmul(a, b, mode, *, name, tm, tn, tk, epi=None, extra=(), out_dtype=F32, chunks=None, comm=None):
    if mode == "nn":
        (M, K), N = a.shape, b.shape[1]
    elif mode == "nt":
        (M, K), N = a.shape, b.shape[0]
    else:
        (K, M), N = a.shape, b.shape[1]
    tm, tn, tk = min(tm, M), min(tn, N), min(tk, K)
    assert M % tm == 0 and N % tn == 0 and K % tk == 0 and (epi != "ln" or tn == N), (name, M, N, K)
    nk = K // tk
    a_spec = {"nn": pl.BlockSpec((tm, tk), lambda i, j, k: (i, k)),
              "nt": pl.BlockSpec((tm, tk), lambda i, j, k: (i, k)),
              "tn": pl.BlockSpec((tk, tm), lambda i, j, k: (k, i))}[mode]
    b_spec = {"nn": pl.BlockSpec((tk, tn), lambda i, j, k: (k, j)),
              "nt": pl.BlockSpec((tn, tk), lambda i, j, k: (j, k)),
              "tn": pl.BlockSpec((tk, tn), lambda i, j, k: (k, j))}[mode]
    ca, cb = {"nn": (1, 0), "nt": (1, 1), "tn": (0, 0)}[mode]
    tile = pl.BlockSpec((tm, tn), lambda i, j, k: (i, j))
    row = pl.BlockSpec((1, tn), lambda i, j, k: (0, j))
    n_extra = {None: 0, "add": 1, "relu2": 0, "drelu2": 1, "ln": 3}[epi]
    assert len(extra) == n_extra
    extra_specs = {None: [], "add": [tile], "relu2": [], "drelu2": [tile], "ln": [tile, row, row]}[epi]
    split = 0
    if epi == "relu2":
        out_shape, out_specs = (jax.ShapeDtypeStruct((M, N), BF16),), (tile,)
    elif epi == "ln":
        out_shape = (jax.ShapeDtypeStruct((M, N), F32), jax.ShapeDtypeStruct((M, N), F32),
                     jax.ShapeDtypeStruct((M, N), BF16))
        out_specs = (tile, tile, tile)
    elif chunks == "cols":
        c = N // N_DEV
        out_shape = (jax.ShapeDtypeStruct((N_DEV, M, c), out_dtype),)
        if tn == N:
            split = c
            out_specs = (pl.BlockSpec((N_DEV, tm, c), lambda i, j, k: (0, i, 0)),)
        else:
            assert c % tn == 0
            out_specs = (pl.BlockSpec((None, tm, tn), lambda i, j, k: (j // (c // tn), i, j % (c // tn))),)
    else:
        out_shape, out_specs = (jax.ShapeDtypeStruct((M, N), out_dtype),), (tile,)
    n_out = len(out_shape)

    def body(*refs):
        a_ref, b_ref = refs[:2]
        ex = refs[2:2 + n_extra]
        outs = refs[2 + n_extra:2 + n_extra + n_out]
        acc_ref = refs[-1]
        k = pl.program_id(2)
        part = _dg(_bf(a_ref[...]), _bf(b_ref[...]), ca, cb)

        def finish(acc):
            if epi == "add":
                outs[0][...] = (acc + ALPHA * ex[0][...]).astype(out_dtype)
            elif epi == "relu2":
                r = jnp.maximum(acc, 0.0)
                outs[0][...] = _bf(r * r)
            elif epi == "drelu2":
                outs[0][...] = (acc * (2.0 * jnp.sqrt(ex[0][...].astype(F32)))).astype(out_dtype)
            elif epi == "ln":
                u = ALPHA * ex[0][...] + acc
                xh, _ = _norm_stats(u)
                y = xh * ex[1][...] + ex[2][...]
                outs[0][...] = u
                outs[1][...] = y
                outs[2][...] = _bf(y)
            elif split:
                for p in range(N_DEV):
                    outs[0][p] = acc[:, p * split:(p + 1) * split].astype(out_dtype)
            else:
                outs[0][...] = acc.astype(out_dtype)

        if nk == 1:
            finish(part)
        else:
            @pl.when(k == 0)
            def _():
                acc_ref[...] = part

            @pl.when(jnp.logical_and(k > 0, k < nk - 1))
            def _():
                acc_ref[...] += part

            @pl.when(k == nk - 1)
            def _():
                finish(acc_ref[...] + part)

    outs, landed = _pcall(
        body, name=name, out_shape=out_shape, grid=(M // tm, N // tn, nk),
        in_specs=[a_spec, b_spec] + extra_specs, out_specs=out_specs,
        scratch_shapes=[pltpu.VMEM((tm, tn) if nk > 1 else (8, 128), F32)], sem=("parallel", "parallel", "arbitrary"),
        args=(a, b, *extra), comm=comm)
    res = outs[0] if n_out == 1 else tuple(outs)
    if chunks == "rows":
        res = res.reshape(N_DEV, M // N_DEV, N)
    return res if comm is None else (res, landed)


def _ret_tables(S):
    half = 64
    inv_freq = ROPE_BASE ** (-jnp.arange(half, dtype=F32) / half)
    ang = jnp.arange(S, dtype=jnp.int32).astype(F32)[:, None] * inv_freq[None, :]
    cos, sin = jnp.cos(ang), jnp.sin(ang)
    cosf = jnp.concatenate([cos, cos], axis=1)
    sinf = jnp.concatenate([-sin, sin], axis=1)
    log_g = jnp.log(1.0 - 2.0 ** (-5.0 - jnp.arange(4, dtype=F32)))
    idx = jnp.arange(CHUNK, dtype=F32)
    diff = idx[:, None] - idx[None, :]
    md = jnp.where(diff[None] >= 0, jnp.exp(log_g[:, None, None] * diff[None]), 0.0)
    kd = jnp.exp(log_g[:, None] * (CHUNK - 1 - idx)[None, :])
    qd = jnp.exp(log_g[:, None] * (idx + 1.0)[None, :])
    cd = jnp.exp(log_g * CHUNK)
    bc = lambda t: jnp.broadcast_to(t[:, :, None], (4, CHUNK, CHUNK))
    return cosf, sinf, md, bc(qd), bc(kd), jnp.broadcast_to(cd[:, None, None], (4, 8, CHUNK))


def _rot(x, cosf, sinf):
    return x * cosf + pltpu.roll(x, 64, 1) * sinf


def _rot_t(dx, cosf, sinf):
    return dx * cosf - pltpu.roll(dx, 64, 1) * sinf


def _ret_specs(rev, N):
    rn = (lambda n: N - 1 - n) if rev else (lambda n: n)
    col = lambda c: pl.BlockSpec((CHUNK, 512), lambda n, c=c: (rn(n), c))
    tab = pl.BlockSpec((CHUNK, CHUNK), lambda n: (rn(n), 0))
    dec = pl.BlockSpec((4, CHUNK, CHUNK), lambda n: (0, 0, 0))
    cdec = pl.BlockSpec((4, 8, CHUNK), lambda n: (0, 0, 0))
    vec = pl.BlockSpec((1, 512), lambda n: (0, 0))
    st = pl.BlockSpec((1, 4, CHUNK, CHUNK), lambda n: (rn(n), 0, 0, 0))
    return col, tab, dec, cdec, vec, st


def _ret_fwd(proj, tables, gn_g, gn_b):
    S = proj.shape[0]
    N = S // CHUNK
    col, tab, dec, cdec, vec, st = _ret_specs(False, N)

    def body(q_ref, k_ref, v_ref, g_ref, cos_ref, sin_ref, md_ref, qd_ref, kd_ref, cd_ref, gng_ref, gnb_ref,
             out_ref, st_ref, state):
        @pl.when(pl.program_id(0) == 0)
        def _():
            state[...] = jnp.zeros_like(state)

        cosf, sinf = cos_ref[...], sin_ref[...]
        for h in range(4):
            sl = slice(h * 128, (h + 1) * 128)
            qr = _rot(q_ref[:, sl], cosf, sinf)
            kr = _rot(k_ref[:, sl], cosf, sinf) * (128 ** -0.5)
            vb = _bf(v_ref[:, sl])
            s0 = state[h]
            st_ref[0, h] = s0
            sc = _dg(_bf(qr), _bf(kr), 1, 1) * md_ref[h]
            r = _dg(_bf(sc), vb, 1, 0) + _dg(_bf(qr * qd_ref[h]), _bf(s0), 1, 0)
            state[h] = s0 * cd_ref[h, 0:1, :] + _dg(_bf(kr * kd_ref[h]), vb, 0, 0)
            y, _ = _norm_stats(r)
            rg = g_ref[:, sl]
            out_ref[:, sl] = rg * _sigmoid(rg) * (y * gng_ref[:, sl] + gnb_ref[:, sl])

    return pl.pallas_call(
        body, name="ret_fwd", grid=(N,),
        out_shape=(jax.ShapeDtypeStruct((S, RET_W), F32), jax.ShapeDtypeStruct((N, 4, CHUNK, CHUNK), F32)),
        in_specs=[col(0), col(1), col(2), col(3), tab, tab, dec, dec, dec, cdec, vec, vec],
        out_specs=(pl.BlockSpec((CHUNK, 512), lambda n: (n, 0)), st),
        scratch_shapes=[pltpu.VMEM((4, CHUNK, CHUNK), F32)],
        compiler_params=_cparams(("arbitrary",)),
    )(proj, proj, proj, proj, *tables, gn_g, gn_b)


def _ret_bwd(proj, tables, gn_g, gn_b, states, d_out):
    S = proj.shape[0]
    N = S // CHUNK
    col, tab, dec, cdec, vec, st = _ret_specs(True, N)

    def kernel_body(q_ref, k_ref, v_ref, g_ref, cos_ref, sin_ref, md_ref, qd_ref, kd_ref, cd_ref, gng_ref, gnb_ref,
                    st_ref, do_ref, dp_ref, dg_ref, db_ref, gstate):
        @pl.when(pl.program_id(0) == 0)
        def _():
            gstate[...] = jnp.zeros_like(gstate)
            dg_ref[...] = jnp.zeros_like(dg_ref)
            db_ref[...] = jnp.zeros_like(db_ref)

        cosf, sinf = cos_ref[...], sin_ref[...]
        for h in range(4):
            sl = slice(h * 128, (h + 1) * 128)
            qr = _rot(q_ref[:, sl], cosf, sinf)
            kr = _rot(k_ref[:, sl], cosf, sinf) * (128 ** -0.5)
            qb, kb, vb = _bf(qr), _bf(kr), _bf(v_ref[:, sl])
            s0b = _bf(st_ref[0, h])
            md, qd, kd = md_ref[h], qd_ref[h], kd_ref[h]
            scb = _bf(_dg(qb, kb, 1, 1) * md)
            qdb = _bf(qr * qd)
            kdb = _bf(kr * kd)
            r = _dg(scb, vb, 1, 0) + _dg(qdb, s0b, 1, 0)
            y, rstd = _norm_stats(r)
            gng = gng_ref[:, sl]
            gn = y * gng + gnb_ref[:, sl]
            rg = g_ref[:, sl]
            sg = _sigmoid(rg)
            d_o = do_ref[:, sl]
            d_gn = d_o * (rg * sg)
            dg_ref[:, sl] += jnp.sum(d_gn * y, axis=0, keepdims=True)
            db_ref[:, sl] += jnp.sum(d_gn, axis=0, keepdims=True)
            drb = _bf(_norm_bwd(d_gn * gng, y, rstd))
            g0 = gstate[h]
            gb = _bf(g0)
            dscb = _bf(_dg(drb, vb, 1, 1) * md)
            dqr = _dg(dscb, kb, 1, 0) + _dg(drb, s0b, 1, 1) * qd
            dkr = _dg(dscb, qb, 0, 0) + _dg(vb, gb, 1, 1) * kd
            dv = _dg(scb, drb, 0, 0) + _dg(kdb, gb, 1, 0)
            gstate[h] = g0 * cd_ref[h, 0:1, :] + _dg(qdb, drb, 0, 0)
            dp_ref[:, 0 * 512 + h * 128:0 * 512 + (h + 1) * 128] = _bf(_rot_t(dqr, cosf, sinf))
            dp_ref[:, 1 * 512 + h * 128:1 * 512 + (h + 1) * 128] = _bf(_rot_t(dkr, cosf, sinf) * (128 ** -0.5))
            dp_ref[:, 2 * 512 + h * 128:2 * 512 + (h + 1) * 128] = _bf(dv)
            dp_ref[:, 3 * 512 + h * 128:3 * 512 + (h + 1) * 128] = _bf(d_o * gn * (sg * (1.0 + rg * (1.0 - sg))))

    acc = pl.BlockSpec((1, 512), lambda n: (0, 0))
    return pl.pallas_call(
        kernel_body, name="ret_bwd", grid=(N,),
        out_shape=(jax.ShapeDtypeStruct((S, 2048), BF16), jax.ShapeDtypeStruct((1, 512), F32),
                   jax.ShapeDtypeStruct((1, 512), F32)),
        in_specs=[col(0), col(1), col(2), col(3), tab, tab, dec, dec, dec, cdec, vec, vec, st,
                  pl.BlockSpec((CHUNK, 512), lambda n: (N - 1 - n, 0))],
        out_specs=(pl.BlockSpec((CHUNK, 2048), lambda n: (N - 1 - n, 0)), acc, acc),
        scratch_shapes=[pltpu.VMEM((4, CHUNK, CHUNK), F32)],
        compiler_params=_cparams(("arbitrary",)),
    )(proj, proj, proj, proj, *tables, gn_g, gn_b, states, d_out)


SB_T = 256
SB_SCALE = 64 ** -0.5
SB_Q_COL, SB_K_COL, SB_V_COL = 2048 // 128, 2560 // 128, 3072 // 128


def _head_masks():
    lane = lax.broadcasted_iota(jnp.int32, (1, 128), 1)
    m0 = (lane < 64).astype(F32)
    return m0, 1.0 - m0


def _tri(n, cmp):
    r = lax.broadcasted_iota(jnp.int32, (n, n), 0)
    c = lax.broadcasted_iota(jnp.int32, (n, n), 1)
    return cmp(r, c)


def _tri_sum(x, tri):
    hi = _bf(x)
    lo = _bf(x - hi.astype(F32))
    return _dg(hi, tri, 1, 0) + _dg(lo, tri, 1, 0)


def _sb_weights(qms, kblks, upper, carry, causal):
    tiles = [(b, h) for b in range(len(kblks)) for h in range(2)]
    zs = [_dg(qms[h], kblks[b], 1, 1) for b, h in tiles]
    lgs = [-(jnp.maximum(z, 0.0) + jnp.log(1.0 + jnp.exp(-jnp.abs(z)))) for z in zs]
    if causal is not None:
        lgs = [jnp.where(causal, lg, 0.0) for lg in lgs]
    carries = list(carry)
    for t in range(len(tiles) - 2):
        carries.append(carries[t] + jnp.sum(lgs[t], axis=1, keepdims=True))
    his = [_bf(lg) for lg in lgs]
    los = [_bf(lg - hi.astype(F32)) for lg, hi in zip(lgs, his)]
    later = [_dg(hi, upper, 1, 0) for hi in his]
    later = [r + _dg(lo, upper, 1, 0) for r, lo in zip(later, los)]
    a = [jnp.exp(lg + z + (r + c)) for lg, z, r, c in zip(lgs, zs, later, carries)]
    if causal is not None:
        a = [jnp.where(causal, x, 0.0) for x in a]
    out = tuple(carries[t] + jnp.sum(lgs[t], axis=1, keepdims=True) for t in (len(tiles) - 2, len(tiles) - 1))
    return [a[2 * b:2 * b + 2] for b in range(len(kblks))], out


def _sb_fwd(proj, comm=None):
    S = proj.shape[0]
    T = min(SB_T, S)
    nq = S // T

    def body(q_ref, k_ref, v_ref, o_ref, a_ref, kb_ref, vm_ref, acc_ref):
        i = pl.program_id(1)
        m0, m1 = _head_masks()

        @pl.when(i == 0)
        def _():
            v = v_ref[...]
            kb_ref[...] = _bf(k_ref[...])
            vm_ref[0] = _bf(v * m0)
            vm_ref[1] = _bf(v * m1)

        q = q_ref[...]
        qm = (_bf(q * (m0 * SB_SCALE)), _bf(q * (m1 * SB_SCALE)))
        upper = _tri(T, lambda r, c: r > c).astype(BF16)
        causal = _tri(T, lambda r, c: c < r)

        def tiles(js, carry, mask, first):
            ks = [pl.multiple_of(j * T, T) for j in js]
            a, out = _sb_weights(qm, [kb_ref[pl.ds(k, T), :] for k in ks], upper, carry, mask)
            a = [[_bf(t) for t in per_block] for per_block in a]
            for b, j in enumerate(js):
                for h in range(2):
                    a_ref[h, j] = a[b][h]
            parts = [_dg(a[b][h], vm_ref[h, pl.ds(k, T), :], 1, 0) for b, k in enumerate(ks) for h in range(2)]
            part = functools.reduce(lambda u, w: u + w, parts)
            if first:
                acc_ref[...] = part
            else:
                acc_ref[...] += part
            return out

        zero = jnp.zeros((T, 1), F32)
        carry = tiles([i], (zero, zero), causal, True)
        carry = lax.fori_loop(0, i % 2, lambda _, c: tiles([i - 1], c, None, False), carry)
        top = i - 1 - i % 2
        lax.fori_loop(0, i // 2, lambda jj, c: tiles([top - 2 * jj, top - 2 * jj - 1], c, None, False), carry)
        o_ref[...] = acc_ref[...]

    full = lambda c: pl.BlockSpec((S, 128), lambda p, i, c=c: (0, c + p))
    outs, landed = _pcall(
        body, name="sb_fwd", grid=(4, nq),
        out_shape=[jax.ShapeDtypeStruct((S, SB_W), F32), jax.ShapeDtypeStruct((4, 2, nq, nq, T, T), BF16)],
        in_specs=[pl.BlockSpec((T, 128), lambda p, i: (i, SB_Q_COL + p)), full(SB_K_COL), full(SB_V_COL)],
        out_specs=[pl.BlockSpec((T, 128), lambda p, i: (i, p)),
                   pl.BlockSpec((None, 2, None, nq, T, T), lambda p, i: (p, 0, i, 0, 0, 0))],
        scratch_shapes=[pltpu.VMEM((S, 128), BF16), pltpu.VMEM((2, S, 128), BF16), pltpu.VMEM((T, 128), F32)],
        sem=("arbitrary", "arbitrary"), args=(proj, proj, proj), comm=comm)
    return tuple(outs) if comm is None else (tuple(outs), landed)


def _sb_bwd(proj, a_saved, d_o, comm=None):
    S = proj.shape[0]
    T = min(SB_T, S)
    nq = S // T

    def body(q_ref, k_ref, v_ref, do_ref, a_ref, dq_ref, dk_ref, dv_ref, kb_ref, kbm_ref, vb_ref, dq_acc, dk_acc, dv_acc):
        i = pl.program_id(1)
        m0, m1 = _head_masks()

        @pl.when(i == 0)
        def _():
            k = k_ref[...]
            kb_ref[...] = _bf(k)
            kbm_ref[0] = _bf(k * m0)
            kbm_ref[1] = _bf(k * m1)
            vb_ref[...] = _bf(v_ref[...])
            dk_acc[...] = jnp.zeros_like(dk_acc)
            dv_acc[...] = jnp.zeros_like(dv_acc)

        q, d_out = q_ref[...], do_ref[...]
        qm = (_bf(q * (m0 * SB_SCALE)), _bf(q * (m1 * SB_SCALE)))
        dom = (_bf(d_out * m0), _bf(d_out * m1))
        lower = _tri(T, lambda r, c: r < c).astype(BF16)
        causal = _tri(T, lambda r, c: c < r)

        def up(js, carry, mask):
            ks = [pl.multiple_of(j * T, T) for j in js]
            tiles = [(b, h) for b in range(len(js)) for h in range(2)]
            zs = [_dg(qm[h], kb_ref[pl.ds(ks[b], T), :], 1, 1) for b, h in tiles]
            a = [a_ref[h, js[b]] for b, h in tiles]
            es = [w.astype(F32) * _dg(dom[h], vb_ref[pl.ds(ks[b], T), :], 1, 1) for w, (b, h) in zip(a, tiles)]
            carries = list(carry)
            for t in range(len(tiles)):
                carries.append(carries[t] + jnp.sum(es[t], axis=1, keepdims=True))
            his = [_bf(e) for e in es]
            los = [_bf(e - hi.astype(F32)) for e, hi in zip(es, his)]
            d_lg = [_dg(hi, lower, 1, 0) for hi in his]
            d_lg = [r + _dg(lo, lower, 1, 0) + c for r, lo, c in zip(d_lg, los, carries)]
            ens = [jnp.exp(-jnp.abs(z)) for z in zs]
            invs = [1.0 / (1.0 + en) for en in ens]
            betas = [jnp.where(z >= 0.0, inv, en * inv) for z, en, inv in zip(zs, ens, invs)]
            dzs = [e * (1.0 - b) - d * b for e, b, d in zip(es, betas, d_lg)]
            if mask is not None:
                dzs = [jnp.where(mask, dz, 0.0) for dz in dzs]
            dzs = [_bf(dz) for dz in dzs]
            parts = [_dg(dzs[t], kbm_ref[h, pl.ds(ks[b], T), :], 1, 0) for t, (b, h) in enumerate(tiles)]
            dq_acc[...] += functools.reduce(lambda u, w: u + w, parts)
            for b, k in enumerate(ks):
                dk_acc[pl.ds(k, T), :] += _dg(dzs[2 * b], qm[0], 0, 0) + _dg(dzs[2 * b + 1], qm[1], 0, 0)
                dv_acc[pl.ds(k, T), :] += _dg(a[2 * b], dom[0], 0, 0) + _dg(a[2 * b + 1], dom[1], 0, 0)
            return tuple(carries[-2:])

        zero = jnp.zeros((T, 1), F32)
        dq_acc[...] = jnp.zeros_like(dq_acc)
        carry = lax.fori_loop(0, i // 2, lambda jj, c: up([2 * jj, 2 * jj + 1], c, None), (zero, zero))
        carry = lax.fori_loop(0, i % 2, lambda _, c: up([i - 1], c, None), carry)
        up([i], carry, causal)
        dq_ref[...] = _bf(dq_acc[...] * SB_SCALE)

        @pl.when(i == nq - 1)
        def _():
            dk_ref[...] = _bf(dk_acc[...])
            dv_ref[...] = _bf(dv_acc[...])

    full = lambda c: pl.BlockSpec((S, 128), lambda p, i, c=c: (0, c + p))
    tile = pl.BlockSpec((T, 128), lambda p, i: (i, p))
    acc = pl.BlockSpec((S, 128), lambda p, i: (0, p))
    out = jax.ShapeDtypeStruct((S, SB_W), BF16)
    outs, landed = _pcall(
        body, name="sb_bwd", grid=(4, nq), out_shape=[out, out, out],
        in_specs=[pl.BlockSpec((T, 128), lambda p, i: (i, SB_Q_COL + p)), full(SB_K_COL), full(SB_V_COL), tile,
                  pl.BlockSpec((None, 2, None, nq, T, T), lambda p, i: (p, 0, i, 0, 0, 0))],
        out_specs=[tile, acc, acc],
        scratch_shapes=[pltpu.VMEM((S, 128), BF16), pltpu.VMEM((2, S, 128), BF16), pltpu.VMEM((S, 128), BF16),
                        pltpu.VMEM((T, 128), F32), pltpu.VMEM((S, 128), F32), pltpu.VMEM((S, 128), F32)],
        sem=("arbitrary", "arbitrary"), args=(proj, proj, proj, d_o, a_saved), comm=comm)
    return tuple(outs) if comm is None else (tuple(outs), landed)


SGU_U_COL, SGU_V_COL = 3584 // 512, 4096 // 512


def _causal(w):
    r = lax.broadcasted_iota(jnp.int32, (CHUNK, CHUNK), 0)
    c = lax.broadcasted_iota(jnp.int32, (CHUNK, CHUNK), 1)
    return jnp.where(r >= c, w, 0.0)


def _sgu_fwd(proj, ln_g, ln_b, w, b):
    S = proj.shape[0]
    N = S // CHUNK

    def body(u_ref, v_ref, g_ref, b_ref, w_ref, bias_ref, out_ref):
        u = _gelu(u_ref[...])
        xh, _ = _norm_stats(_gelu(v_ref[...]))
        vn = _bf(xh * g_ref[...] + b_ref[...])
        for g in range(4):
            sl = slice(g * 128, (g + 1) * 128)
            sv = _dg(_bf(_causal(w_ref[g])), vn[:, sl], 1, 0) + bias_ref[g]
            out_ref[:, sl] = u[:, sl] * sv

    vec = pl.BlockSpec((1, 512), lambda n: (0, 0))
    return pl.pallas_call(
        body, name="sgu_fwd", grid=(N,),
        out_shape=jax.ShapeDtypeStruct((S, SGU_W), F32),
        in_specs=[pl.BlockSpec((CHUNK, 512), lambda n: (n, SGU_U_COL)),
                  pl.BlockSpec((CHUNK, 512), lambda n: (n, SGU_V_COL)), vec, vec,
                  pl.BlockSpec((4, CHUNK, CHUNK), lambda n: (0, 0, 0)), pl.BlockSpec((4, CHUNK, 1), lambda n: (0, 0, 0))],
        out_specs=pl.BlockSpec((CHUNK, 512), lambda n: (n, 0)),
        compiler_params=_cparams(("parallel",)),
    )(proj, proj, ln_g, ln_b, w, b)


def _sgu_bwd(proj, ln_g, ln_b, w, b, d_out):
    S = proj.shape[0]
    N = S // CHUNK

    def body(u_ref, v_ref, g_ref, b_ref, w_ref, bias_ref, do_ref, dp_ref, dg_ref, db_ref, dw_ref, dbias_ref):
        @pl.when(pl.program_id(0) == 0)
        def _():
            dg_ref[...] = jnp.zeros_like(dg_ref)
            db_ref[...] = jnp.zeros_like(db_ref)
            dw_ref[...] = jnp.zeros_like(dw_ref)
            dbias_ref[...] = jnp.zeros_like(dbias_ref)

        gu, gv = u_ref[...], v_ref[...]
        u = _gelu(gu)
        xh, rstd = _norm_stats(_gelu(gv))
        ln_gain = g_ref[...]
        vn = _bf(xh * ln_gain + b_ref[...])
        d_o = do_ref[...]
        d_vn = []
        for g in range(4):
            sl = slice(g * 128, (g + 1) * 128)
            wc = _bf(_causal(w_ref[g]))
            sv = _dg(wc, vn[:, sl], 1, 0) + bias_ref[g]
            dp_ref[:, sl] = _bf(d_o[:, sl] * sv * _gelu_grad(gu[:, sl]))
            d_sv = d_o[:, sl] * u[:, sl]
            dbias_ref[g] += jnp.sum(d_sv, axis=1, keepdims=True)
            d_svb = _bf(d_sv)
            dw_ref[g] += _causal(_dg(d_svb, vn[:, sl], 1, 1))
            d_vn.append(_dg(wc, d_svb, 0, 0))
        d_vn = jnp.concatenate(d_vn, axis=1)
        dg_ref[...] += jnp.sum(d_vn * xh, axis=0, keepdims=True)
        db_ref[...] += jnp.sum(d_vn, axis=0, keepdims=True)
        dp_ref[:, 512:1024] = _bf(_norm_bwd(d_vn * ln_gain, xh, rstd) * _gelu_grad(gv))

    vec = pl.BlockSpec((1, 512), lambda n: (0, 0))
    wspec = pl.BlockSpec((4, CHUNK, CHUNK), lambda n: (0, 0, 0))
    bspec = pl.BlockSpec((4, CHUNK, 1), lambda n: (0, 0, 0))
    return pl.pallas_call(
        body, name="sgu_bwd", grid=(N,),
        out_shape=(jax.ShapeDtypeStruct((S, 1024), BF16), jax.ShapeDtypeStruct((1, 512), F32),
                   jax.ShapeDtypeStruct((1, 512), F32), jax.ShapeDtypeStruct((4, CHUNK, CHUNK), F32),
                   jax.ShapeDtypeStruct((4, CHUNK, 1), F32)),
        in_specs=[pl.BlockSpec((CHUNK, 512), lambda n: (n, SGU_U_COL)),
                  pl.BlockSpec((CHUNK, 512), lambda n: (n, SGU_V_COL)), vec, vec, wspec, bspec,
                  pl.BlockSpec((CHUNK, 512), lambda n: (n, 0))],
        out_specs=(pl.BlockSpec((CHUNK, 1024), lambda n: (n, 0)), vec, vec, wspec, bspec),
        compiler_params=_cparams(("arbitrary",)),
    )(proj, proj, ln_g, ln_b, w, b, d_out)


GATE_COL = 4608 // 512


def _merge_fwd(proj, branches, p_list, tm=512):
    S = proj.shape[0]
    tm = min(tm, S)

    def body(r_ref, s_ref, g_ref, pr_ref, ps_ref, pg_ref, gr_ref, gs_ref, gg_ref, m_ref, br_ref):
        acc = None
        for k, (x_ref, p_ref, gate_ref) in enumerate(((r_ref, pr_ref, gr_ref), (s_ref, ps_ref, gs_ref),
                                                      (g_ref, pg_ref, gg_ref))):
            br = _dg(_bf(x_ref[...]), _bf(p_ref[...]), 1, 0)
            br_ref[k] = br
            term = _sigmoid(gate_ref[...]) * br
            acc = term if acc is None else acc + term
        m_ref[...] = _bf(acc)

    xs = pl.BlockSpec((tm, 512), lambda i, n: (i, 0))
    ps = pl.BlockSpec((512, 512), lambda i, n: (0, n))
    gate = lambda k: pl.BlockSpec((tm, 512), lambda i, n, k=k: (i, GATE_COL + 2 * k + n))
    return pl.pallas_call(
        body, name="merge_fwd", grid=(S // tm, 2),
        out_shape=(jax.ShapeDtypeStruct((S, D_MODEL), BF16), jax.ShapeDtypeStruct((3, S, D_MODEL), F32)),
        in_specs=[xs, xs, xs, ps, ps, ps, gate(0), gate(1), gate(2)],
        out_specs=(pl.BlockSpec((tm, 512), lambda i, n: (i, n)), pl.BlockSpec((3, tm, 512), lambda i, n: (0, i, n))),
        compiler_params=_cparams(("parallel", "parallel")),
    )(*branches, *p_list, proj, proj, proj)


def _gate_bwd(proj, br, d_merged, tm=512):
    S = proj.shape[0]
    tm = min(tm, S)

    def body(dm_ref, br_ref, gr_ref, gs_ref, gg_ref, dbr_ref, dgate_ref):
        dm = dm_ref[...]
        for k, gate_ref in enumerate((gr_ref, gs_ref, gg_ref)):
            s = _sigmoid(gate_ref[...])
            dbr_ref[k] = _bf(dm * s)
            dgate_ref[k] = _bf(dm * br_ref[k] * (s * (1.0 - s)))

    gate = lambda k: pl.BlockSpec((tm, 512), lambda i, n, k=k: (i, GATE_COL + 2 * k + n))
    three = pl.BlockSpec((3, tm, 512), lambda i, n: (0, i, n))
    return pl.pallas_call(
        body, name="gate_bwd", grid=(S // tm, 2),
        out_shape=(jax.ShapeDtypeStruct((3, S, D_MODEL), BF16), jax.ShapeDtypeStruct((3, S, D_MODEL), BF16)),
        in_specs=[pl.BlockSpec((tm, 512), lambda i, n: (i, n)), three, gate(0), gate(1), gate(2)],
        out_specs=(three, three),
        compiler_params=_cparams(("parallel", "parallel")),
    )(d_merged, br, proj, proj, proj)


def _ln_bwd(dy, u, g, tm=256):
    S, D = u.shape
    tm = min(tm, S)

    def body(dy_ref, u_ref, g_ref, du_ref, dub_ref, dg_ref, db_ref):
        @pl.when(pl.program_id(0) == 0)
        def _():
            dg_ref[...] = jnp.zeros_like(dg_ref)
            db_ref[...] = jnp.zeros_like(db_ref)

        dy_t = dy_ref[...]
        xh, rstd = _norm_stats(u_ref[...])
        dg_ref[...] += jnp.sum(dy_t * xh, axis=0, keepdims=True)
        db_ref[...] += jnp.sum(dy_t, axis=0, keepdims=True)
        du = _norm_bwd(dy_t * g_ref[...], xh, rstd)
        du_ref[...] = du
        dub_ref[...] = _bf(du)

    tile = pl.BlockSpec((tm, D), lambda i: (i, 0))
    vec = pl.BlockSpec((1, D), lambda i: (0, 0))
    return pl.pallas_call(
        body, name="ln_bwd", grid=(S // tm,),
        out_shape=(jax.ShapeDtypeStruct((S, D), F32), jax.ShapeDtypeStruct((S, D), BF16),
                   jax.ShapeDtypeStruct((1, D), F32), jax.ShapeDtypeStruct((1, D), F32)),
        in_specs=[tile, tile, vec], out_specs=(tile, tile, vec, vec),
        compiler_params=_cparams(("arbitrary",)),
    )(dy, u, g)


def _loss_grad(y, target, tm=256):
    S, D = y.shape
    tm = min(tm, S)

    def body(y_ref, t_ref, dy_ref, sq_ref):
        @pl.when(pl.program_id(0) == 0)
        def _():
            sq_ref[...] = jnp.zeros_like(sq_ref)

        err = y_ref[...] - t_ref[...]
        dy_ref[...] = err * (1.0 / D)
        sq_ref[...] += jnp.sum(err * err, axis=0, keepdims=True)

    tile = pl.BlockSpec((tm, D), lambda i: (i, 0))
    vec = pl.BlockSpec((1, D), lambda i: (0, 0))
    return pl.pallas_call(
        body, name="loss_grad", grid=(S // tm,),
        out_shape=(jax.ShapeDtypeStruct((S, D), F32), jax.ShapeDtypeStruct((1, D), F32)),
        in_specs=[tile, tile], out_specs=(tile, vec),
        compiler_params=_cparams(("arbitrary",)),
    )(y, target)


def _layer_fwd(x, x_bf, W, tables, sb_comm=None):
    proj = _matmul(x_bf, W["w_in_t"], "nt", name="proj", tm=1024, tn=768, tk=1024)
    retg, states = _ret_fwd(proj, tables, W["ret_gn_g"], W["ret_gn_b"])
    if sb_comm is None:
        sb, sb_a = _sb_fwd(proj)
    else:
        (sb, sb_a), landed = _sb_fwd(proj, comm=sb_comm[0])
        sb_comm[1](landed)
    sg = _sgu_fwd(proj, W["sgu_ln_g"], W["sgu_ln_b"], W["sgu_w"], W["sgu_b"])
    merged, br = _merge_fwd(proj, (retg, sb, sg), (W["p_ret"], W["p_sb"], W["p_sgu"]))
    u1, x1, x1_bf = _matmul(merged, W["w_out"], "nn", name="out_ln", tm=512, tn=1024, tk=1024, epi="ln",
                            extra=(x, W["ln1_g"], W["ln1_b"]))
    act = _matmul(x1_bf, W["w_up"], "nn", name="up", tm=1024, tn=1024, tk=1024, epi="relu2")
    u2, x2, x2_bf = _matmul(act, W["w_down"], "nn", name="down_ln", tm=512, tn=1024, tk=4096, epi="ln",
                            extra=(x1, W["ln2_g"], W["ln2_b"]))
    saved = dict(x_bf=x_bf, proj=proj, retg=retg, states=states, sb=sb, sb_a=sb_a, sg=sg, merged=merged, br=br, u1=u1,
                 x1_bf=x1_bf, act=act, u2=u2)
    return x2, x2_bf, saved


def _layer_bwd(d_x2, W, tables, sv, chunk_dtype=None, sb_comm_fn=None, dx_comm_fn=None):
    dt = F32 if chunk_dtype is None else chunk_dtype
    rows, cols = (None, None) if chunk_dtype is None else ("rows", "cols")
    g, landed = {}, {}
    du2, du2_bf, g["ln2_g"], g["ln2_b"] = _ln_bwd(d_x2, sv["u2"], W["ln2_g"])
    d_hpre = _matmul(du2_bf, W["w_down"], "nt", name="d_act", tm=1024, tn=1024, tk=1024, epi="drelu2",
                     extra=(sv["act"],), out_dtype=BF16)
    g["w_down"] = _matmul(sv["act"], du2_bf, "tn", name="dw_down", tm=512, tn=1024, tk=4096, out_dtype=dt, chunks=rows)
    g["w_up"] = _matmul(sv["x1_bf"], d_hpre, "tn", name="dw_up", tm=1024, tn=512, tk=4096, out_dtype=dt, chunks=cols)
    d_x1 = _matmul(d_hpre, W["w_up"], "nt", name="d_x1", tm=512, tn=1024, tk=4096, epi="add", extra=(du2,))
    du1, du1_bf, g["ln1_g"], g["ln1_b"] = _ln_bwd(d_x1, sv["u1"], W["ln1_g"])
    d_merged = _matmul(du1_bf, W["w_out"], "nt", name="d_merged", tm=1024, tn=1024, tk=1024)
    g["w_out"] = _matmul(sv["merged"], du1_bf, "tn", name="dw_out", tm=1024, tn=512, tk=4096, out_dtype=dt, chunks=rows)
    d_br, d_gate = _gate_bwd(sv["proj"], sv["br"], d_merged)
    d_branch = []
    for k, (nm, act) in enumerate((("p_ret", sv["retg"]), ("p_sb", sv["sb"]), ("p_sgu", sv["sg"]))):
        d_branch.append(_matmul(d_br[k], W[nm], "nt", name="d_" + nm[2:], tm=1024, tn=512, tk=1024))
        g[nm] = _matmul(act, d_br[k], "tn", name="dw_" + nm[2:], tm=512, tn=1024, tk=2048, out_dtype=dt, chunks=cols)
    d_ret, g["ret_gn_g"], g["ret_gn_b"] = _ret_bwd(sv["proj"], tables, W["ret_gn_g"], W["ret_gn_b"], sv["states"],
                                                   d_branch[0])
    if sb_comm_fn is None:
        d_sq, d_sk, d_sv = _sb_bwd(sv["proj"], sv["sb_a"], d_branch[1])
    else:
        (d_sq, d_sk, d_sv), landed["sb"] = _sb_bwd(sv["proj"], sv["sb_a"], d_branch[1], comm=sb_comm_fn(g))
    d_sgu, g["sgu_ln_g"], g["sgu_ln_b"], g["sgu_w"], g["sgu_b"] = _sgu_bwd(
        sv["proj"], W["sgu_ln_g"], W["sgu_ln_b"], W["sgu_w"], W["sgu_b"], d_branch[2])
    d_proj = jnp.concatenate([d_ret, d_sq, d_sk, d_sv, d_sgu, d_gate[0], d_gate[1], d_gate[2]], axis=1)
    g["w_in"] = _matmul(d_proj, sv["x_bf"], "tn", name="dw_in", tm=768, tn=1024, tk=4096, out_dtype=dt, chunks=rows)
    if chunk_dtype is None:
        g["w_in"] = g["w_in"].T
    d_x = _matmul(d_proj, W["w_in_t"], "nn", name="d_x", tm=512, tn=1024, tk=3840, epi="add", extra=(du1,),
                  comm=None if dx_comm_fn is None else dx_comm_fn(g))
    if dx_comm_fn is not None:
        d_x, landed["dx"] = d_x
    return d_x, g, landed


BIG = ("w_in", "p_ret", "p_sb", "p_sgu", "w_out", "w_up", "w_down")
SMALL = ("ret_gn_g", "ret_gn_b", "sgu_ln_g", "sgu_ln_b", "sgu_w", "sgu_b", "ln1_g", "ln1_b", "ln2_g", "ln2_b")
GATHER_KIND = {"w_in": "rows", "p_ret": "cols", "p_sb": "cols", "p_sgu": "cols", "w_out": "rows", "w_up": "cols",
               "w_down": "rows"}


def _small_weights(small, l):
    W = {}
    for n in SMALL:
        if n == "sgu_w":
            W[n] = small[n][l]
        elif n == "sgu_b":
            W[n] = small[n][l].reshape(4, CHUNK, 1)
        else:
            W[n] = small[n][l].reshape(1, -1)
    return W


def _local_step(x, target, full, small):
    tables = _ret_tables(x.shape[0])
    Ws = [{**{n: full[n][l] for n in BIG[1:]}, "w_in_t": full["w_in"][l].T, **_small_weights(small, l)}
          for l in range(DEPTH)]
    saved = []
    h, h_bf = x, _bf(x)
    for l in range(DEPTH):
        h, h_bf, sv = _layer_fwd(h, h_bf, Ws[l], tables)
        saved.append(sv)
    d_h, sq = _loss_grad(h, target)
    grads = [None] * DEPTH
    for l in reversed(range(DEPTH)):
        d_h, grads[l], _ = _layer_bwd(d_h, Ws[l], tables, saved[l])
    return sq, d_h, grads


def _adam(w, parts, m, v, name):
    L, R, C = w.shape
    tr = next(t for t in (320, 256, 128) if R % t == 0)
    assert len(parts) == L

    def body(*refs):
        w_ref, p_refs, (m_ref, v_ref, g_ref, d_ref, nm_ref, nv_ref) = refs[0], refs[1:1 + L], refs[1 + L:]
        layer = pl.program_id(0)
        g = None
        for li, p_ref in enumerate(p_refs):
            s = p_ref[0].astype(F32)
            for j in range(1, p_ref.shape[0]):
                s = s + p_ref[j].astype(F32)
            g = s if g is None else jnp.where(layer == li, s, g)
        g_ref[...] = g
        d_ref[...], nm_ref[...], nv_ref[...] = _adam_update(w_ref[...], g, m_ref[...], v_ref[...])

    tile = pl.BlockSpec((None, tr, C), lambda l, i: (l, i, 0))
    part = lambda li: pl.BlockSpec((parts[li].shape[0], tr, C), lambda l, i, li=li: (0, jnp.where(l == li, i, 0), 0))
    out = jax.ShapeDtypeStruct((L, R, C), F32)
    return pl.pallas_call(
        body, name=name, grid=(L, R // tr), out_shape=(out, out, out, out),
        in_specs=[tile] + [part(li) for li in range(L)] + [tile, tile],
        out_specs=(tile, tile, tile, tile),
        compiler_params=_cparams(("parallel", "parallel")),
    )(w, *parts, m, v)


def _adam_update(w, g, m, v):
    m2 = ADAM_B1 * m + (1.0 - ADAM_B1) * g
    v2 = ADAM_B2 * v + (1.0 - ADAM_B2) * (g * g)
    m_hat = m2 / (1.0 - ADAM_B1 ** ADAM_STEP)
    v_hat = v2 / (1.0 - ADAM_B2 ** ADAM_STEP)
    return -ADAM_LR * (m_hat / (jnp.sqrt(v_hat) + ADAM_EPS) + ADAM_WD * w), m2, v2


def _adam_small(w, m, v, parts):
    k = len(SMALL)

    def body(*refs):
        w_refs, m_refs, v_refs, p_refs, outs = refs[:k], refs[k:2 * k], refs[2 * k:3 * k], refs[3 * k:5 * k], refs[5 * k:]
        for i in range(k):
            vector = len(w_refs[i].shape) == 2
            for l in range(DEPTH):
                p_ref = p_refs[DEPTH * i + l]
                g = p_ref[0]
                for j in range(1, N_DEV):
                    g = g + p_ref[j]
                at = (slice(l, l + 1), slice(None)) if vector else (l,)
                delta, m2, v2 = _adam_update(w_refs[i][at], g, m_refs[i][at], v_refs[i][at])
                for o_ref, val in zip(outs[4 * i:4 * i + 4], (g, delta, m2, v2)):
                    o_ref[at] = val

    vmem = pl.BlockSpec(memory_space=pltpu.VMEM)
    args = [w[n] for n in SMALL] + [m[n] for n in SMALL] + [v[n] for n in SMALL] + \
           [parts[(n, l)] for n in SMALL for l in range(DEPTH)]
    out_shape = [jax.ShapeDtypeStruct(w[n].shape, F32) for n in SMALL for _ in range(4)]
    outs = pl.pallas_call(body, name="adam_small", out_shape=out_shape, in_specs=[vmem] * len(args),
                          out_specs=[vmem] * len(out_shape), compiler_params=_cparams())(*args)
    return {n: tuple(outs[4 * i:4 * i + 4]) for i, n in enumerate(SMALL)}


WEIGHTS = ("w_in", "ret_gn_g", "ret_gn_b", "sgu_ln_g", "sgu_ln_b", "sgu_w", "sgu_b", "p_ret", "p_sb", "p_sgu", "w_out",
           "ln1_g", "ln1_b", "w_up", "w_down", "ln2_g", "ln2_b")


def kernel(x, w_in, ret_gn_g, ret_gn_b, sgu_ln_g, sgu_ln_b, sgu_w, sgu_b, p_ret, p_sb, p_sgu, w_out, ln1_g, ln1_b, w_up, w_down, ln2_g, ln2_b, loss_target, m_w_in, m_ret_gn_g, m_ret_gn_b, m_sgu_ln_g, m_sgu_ln_b, m_sgu_w, m_sgu_b, m_p_ret, m_p_sb, m_p_sgu, m_w_out, m_ln1_g, m_ln1_b, m_w_up, m_w_down, m_ln2_g, m_ln2_b, v_w_in, v_ret_gn_g, v_ret_gn_b, v_sgu_ln_g, v_sgu_ln_b, v_sgu_w, v_sgu_b, v_p_ret, v_p_sb, v_p_sgu, v_w_out, v_ln1_g, v_ln1_b, v_w_up, v_w_down, v_ln2_g, v_ln2_b):
    w = dict(zip(WEIGHTS, (w_in, ret_gn_g, ret_gn_b, sgu_ln_g, sgu_ln_b, sgu_w, sgu_b, p_ret, p_sb, p_sgu, w_out,
                           ln1_g, ln1_b, w_up, w_down, ln2_g, ln2_b)))
    m = dict(zip(WEIGHTS, (m_w_in, m_ret_gn_g, m_ret_gn_b, m_sgu_ln_g, m_sgu_ln_b, m_sgu_w, m_sgu_b, m_p_ret, m_p_sb,
                           m_p_sgu, m_w_out, m_ln1_g, m_ln1_b, m_w_up, m_w_down, m_ln2_g, m_ln2_b)))
    v = dict(zip(WEIGHTS, (v_w_in, v_ret_gn_g, v_ret_gn_b, v_sgu_ln_g, v_sgu_ln_b, v_sgu_w, v_sgu_b, v_p_ret, v_p_sb,
                           v_p_sgu, v_w_out, v_ln1_g, v_ln1_b, v_w_up, v_w_down, v_ln2_g, v_ln2_b)))

    small = {n: w[n] for n in SMALL}
    shard = {n: _bf(w[n]) for n in BIG}
    shard["w_in"] = shard["w_in"].transpose(0, 2, 1)
    S = x.shape[1]
    x0, target = x.reshape(S, D_MODEL), loss_target.reshape(S, D_MODEL)
    tables = _ret_tables(S)
    Ws = [_small_weights(small, l) for l in range(DEPTH)]

    (Ws[0]["w_in_t"],) = _exchange([_gather_transfer(shard["w_in"], 0, "rows")], "gather_w_in0", relay=True)
    later = [(n, 0) for n in BIG[1:]] + [(n, 1) for n in BIG]

    def weights_landed(landed):
        for (n, l), z in zip(later, landed):
            Ws[l]["w_in_t" if n == "w_in" else n] = z

    gather = _Comm([_gather_transfer(shard[n], l, GATHER_KIND[n]) for n, l in later], relay=True)
    h, h_bf, saved0 = _layer_fwd(x0, _bf(x0), Ws[0], tables, sb_comm=(gather, weights_landed))
    h, _, saved1 = _layer_fwd(h, h_bf, Ws[1], tables)
    d_h, sq = _loss_grad(h, target)
    loss = lax.psum(0.5 * jnp.sum(sq) / D_MODEL, ("x", "y", "c"))

    d_h, g1, _ = _layer_bwd(d_h, Ws[1], tables, saved1, chunk_dtype=BF16)
    early = [(n, 1) for n in BIG] + [(n, 0) for n in BIG[1:]]

    def small_slabs(g):
        return [_slab_transfer(g[n].reshape(4, CHUNK) if n == "sgu_b" else g[n]) for n in SMALL]

    def early_scatter(g0):
        return _Comm([_scatter_transfer((g1 if l else g0)[n]) for n, l in early] + small_slabs(g1))

    def late_scatter(g0):
        pairs = _pair_reduce(g0["w_in"], "w_in0_pairs")
        return _Comm([_chip_scatter_transfer(pairs)] + small_slabs(g0))

    d_x, g0, landed = _layer_bwd(d_h, Ws[0], tables, saved0, chunk_dtype=BF16, sb_comm_fn=early_scatter,
                                 dx_comm_fn=late_scatter)
    parts = dict(zip(early, landed["sb"]))
    parts[("w_in", 0)] = landed["dx"][0]
    small_parts = {**{(n, 1): z for n, z in zip(SMALL, landed["sb"][len(early):])},
                   **{(n, 0): z for n, z in zip(SMALL, landed["dx"][1:])}}

    grad, delta, new_m, new_v = {}, {}, {}, {}
    for n in BIG:
        view = (lambda a: a.transpose(0, 2, 1)) if n == "w_in" else (lambda a: a)
        res = _adam(view(w[n]), [parts[(n, l)] for l in range(DEPTH)], view(m[n]), view(v[n]), "adam_" + n)
        grad[n], delta[n], new_m[n], new_v[n] = (view(r) for r in res)
    for n, res in _adam_small(small, m, v, small_parts).items():
        grad[n], delta[n], new_m[n], new_v[n] = res

    return (loss, d_x.reshape(x.shape), *[grad[n] for n in WEIGHTS], *[delta[n] for n in WEIGHTS],
            *[new_m[n] for n in WEIGHTS], *[new_v[n] for n in WEIGHTS])
```

```python
import functools
import math

import numpy as np
import jax
import jax.numpy as jnp
from jax import lax
from jax.experimental import pallas as pl
from jax.experimental.pallas import tpu as pltpu

F32 = jnp.float32
BF16 = jnp.bfloat16

N_DEV = 8
DEPTH = 2
D_MODEL = 1024
CHUNK = 128
RET_W = 512
SB_W = 512
SGU_W = 512
N_IN = 7680
LN_EPS = 1e-5
ALPHA = (2 * DEPTH) ** 0.25
ROPE_BASE = 10000.0
ADAM_LR, ADAM_B1, ADAM_B2, ADAM_EPS, ADAM_WD, ADAM_STEP = 0.001, 0.9, 0.999, 1e-08, 0.01, 10
VMEM_LIMIT = 56 * 1024 * 1024

_GELU_K = math.sqrt(2.0 / math.pi)
_GELU_C = 0.044715


def _cparams(sem=None):
    return pltpu.CompilerParams(dimension_semantics=sem, vmem_limit_bytes=VMEM_LIMIT)


def _dg(a, b, ca, cb):
    return lax.dot_general(a, b, (((ca,), (cb,)), ((), ())), preferred_element_type=F32)


def _bf(x):
    return x.astype(BF16)


def _sigmoid(x):
    return 1.0 / (1.0 + jnp.exp(-x))


def _gelu(x):
    t = jnp.tanh(_GELU_K * (x + _GELU_C * (x * x * x)))
    return x * (0.5 * (1.0 + t))


def _gelu_grad(x):
    t = jnp.tanh(_GELU_K * (x + _GELU_C * (x * x * x)))
    return 0.5 * (1.0 + t) + 0.5 * x * (1.0 - t * t) * (_GELU_K * (1.0 + 3.0 * _GELU_C * x * x))


def _norm_stats(u):
    mu = jnp.mean(u, axis=-1, keepdims=True)
    d = u - mu
    var = jnp.mean(d * d, axis=-1, keepdims=True)
    rstd = lax.rsqrt(var + LN_EPS)
    return d * rstd, rstd


def _norm_bwd(dxh, xh, rstd):
    return rstd * (dxh - jnp.mean(dxh, axis=-1, keepdims=True) - xh * jnp.mean(dxh * xh, axis=-1, keepdims=True))


class _Transfer:
    def __init__(self, src, dst_shape, src_at, dst_at, same_core=False):
        self.src, self.dst_shape, self.src_at, self.dst_at = src, tuple(dst_shape), src_at, dst_at
        self.same_core = same_core


def _gather_transfer(shard, l, kind):
    _, r, c = shard.shape
    src_at = lambda ref, p: ref.at[l]
    if kind == "slab":
        return _Transfer(shard, (N_DEV, r, c), src_at, lambda ref, s: ref.at[s])
    if kind == "rows":
        return _Transfer(shard, (N_DEV * r, c), src_at, lambda ref, s: ref.at[pl.ds(pl.multiple_of(s * r, r), r), :])
    return _Transfer(shard, (r, N_DEV * c), src_at, lambda ref, s: ref.at[:, pl.ds(pl.multiple_of(s * c, c), c)])


def _scatter_transfer(chunks):
    return _Transfer(chunks, chunks.shape, lambda ref, p: ref.at[p], lambda ref, s: ref.at[s])


def _slab_transfer(arr):
    return _Transfer(arr, (N_DEV,) + arr.shape, lambda ref, p: ref, lambda ref, s: ref.at[s])


class _Comm:
    def __init__(self, transfers, relay=False):
        self.transfers = list(transfers)
        self.relay = relay
        self.n = len(self.transfers)
        self.arrays = [t.src for t in self.transfers]
        self.out_shape = [jax.ShapeDtypeStruct(t.dst_shape, t.src.dtype) for t in self.transfers]
        self.scratch = [pltpu.SemaphoreType.DMA((self.n * (N_DEV - 1),)), pltpu.SemaphoreType.DMA((self.n * (N_DEV - 1),)),
                        pltpu.SemaphoreType.DMA((self.n,))]

    def _relay_copies(self, srcs, dsts, send_sems, recv_sems, local_sems):
        x, y, c = lax.axis_index("x"), lax.axis_index("y"), lax.axis_index("c")
        me = 4 * x + 2 * y + c
        chips = [(1 - x, y), (x, 1 - y), (1 - x, 1 - y)]
        first, passed, own = [], [], []
        for t, tr in enumerate(self.transfers):
            def copy(k, src, sender, to, t=t, tr=tr):
                return pltpu.make_async_remote_copy(
                    src_ref=src, dst_ref=tr.dst_at(dsts[t], sender), send_sem=send_sems.at[t * (N_DEV - 1) + k],
                    recv_sem=recv_sems.at[t * (N_DEV - 1) + k], device_id=to, device_id_type=pl.DeviceIdType.MESH)

            mine = tr.src_at(srcs[t], me)
            first.append([copy(0, mine, me, (x, y, 1 - c))] + [copy(1 + j, mine, me, (px, py, c))
                                                                for j, (px, py) in enumerate(chips)])
            passed.append([copy(4 + j, tr.dst_at(dsts[t], 4 * px + 2 * py + c), 4 * px + 2 * py + c, (x, y, 1 - c))
                           for j, (px, py) in enumerate(chips)])
            own.append(pltpu.make_async_copy(mine, tr.dst_at(dsts[t], me), local_sems.at[t]))
        return first, passed, own

    def _copies(self, srcs, dsts, send_sems, recv_sems, local_sems):
        x, y, c = lax.axis_index("x"), lax.axis_index("y"), lax.axis_index("c")
        me = 4 * x + 2 * y + c
        copies = []
        for d in range(1, N_DEV):
            px = 1 - x if d & 4 else x
            py = 1 - y if d & 2 else y
            pc = 1 - c if d & 1 else c
            for t, tr in enumerate(self.transfers):
                if tr.same_core and d & 1:
                    continue
                peer, mine = (2 * px + py, 2 * x + y) if tr.same_core else (4 * px + 2 * py + pc, me)
                k = t * (N_DEV - 1) + d - 1
                copies.append(pltpu.make_async_remote_copy(
                    src_ref=tr.src_at(srcs[t], peer), dst_ref=tr.dst_at(dsts[t], mine),
                    send_sem=send_sems.at[k], recv_sem=recv_sems.at[k],
                    device_id=(px, py, pc), device_id_type=pl.DeviceIdType.MESH))
        own = []
        for t, tr in enumerate(self.transfers):
            mine = 2 * x + y if tr.same_core else me
            own.append(pltpu.make_async_copy(tr.src_at(srcs[t], mine), tr.dst_at(dsts[t], mine), local_sems.at[t]))
        return copies, own

    def start(self, srcs, dsts, *sems):
        if self.relay:
            first, _, own = self._relay_copies(srcs, dsts, *sems)
            for cp in own + [cp for per_t in first for cp in per_t]:
                cp.start()
            return
        copies, own = self._copies(srcs, dsts, *sems)
        for cp in own + copies:
            cp.start()

    def finish(self, srcs, dsts, *sems):
        if self.relay:
            first, passed, own = self._relay_copies(srcs, dsts, *sems)
            for j in range(3):
                for t in range(self.n):
                    first[t][1 + j].wait_recv()
                    passed[t][j].start()
            for t in range(self.n):
                first[t][0].wait_recv()
                for cp in passed[t]:
                    cp.wait_recv()
            for t in range(self.n):
                for cp in first[t] + passed[t]:
                    cp.wait_send()
                own[t].wait()
            return
        copies, own = self._copies(srcs, dsts, *sems)
        for cp in copies + own:
            cp.wait()


def _pcall(body, *, name, grid, in_specs, out_specs, out_shape, scratch_shapes, sem, args, comm=None):
    in_specs, out_specs, out_shape = list(in_specs), list(out_specs), list(out_shape)
    if comm is None:
        outs = pl.pallas_call(body, name=name, grid=grid, in_specs=in_specs, out_specs=out_specs, out_shape=out_shape,
                              scratch_shapes=list(scratch_shapes), compiler_params=_cparams(sem))(*args)
        return list(outs), []
    n_in, n_out, n_scr, k = len(in_specs), len(out_specs), len(scratch_shapes), comm.n

    def carrier(*refs):
        ins, cin = refs[:n_in], refs[n_in:n_in + k]
        outs, cout = refs[n_in + k:n_in + k + n_out], refs[n_in + k + n_out:n_in + 2 * k + n_out]
        scr, sems = refs[n_in + 2 * k + n_out:n_in + 2 * k + n_out + n_scr], refs[n_in + 2 * k + n_out + n_scr:]
        ids = [pl.program_id(d) for d in range(len(grid))]
        first = functools.reduce(jnp.logical_and, [i == 0 for i in ids])
        last = functools.reduce(jnp.logical_and, [i == g - 1 for i, g in zip(ids, grid)])

        @pl.when(first)
        def _():
            comm.start(cin, cout, *sems)

        body(*ins, *outs, *scr)

        @pl.when(last)
        def _():
            comm.finish(cin, cout, *sems)

    hbm = pl.BlockSpec(memory_space=pl.ANY)
    outs = pl.pallas_call(
        carrier, name=name, grid=grid, in_specs=in_specs + [hbm] * k, out_specs=out_specs + [hbm] * k,
        out_shape=out_shape + comm.out_shape, scratch_shapes=list(scratch_shapes) + comm.scratch,
        compiler_params=_cparams(tuple("arbitrary" for _ in grid)),
    )(*args, *comm.arrays)
    return list(outs[:n_out]), list(outs[n_out:])


def _exchange(transfers, name, relay=False):
    comm = _Comm(transfers, relay)

    def body(*refs):
        k = comm.n
        comm.start(refs[:k], refs[k:2 * k], *refs[2 * k:])
        comm.finish(refs[:k], refs[k:2 * k], *refs[2 * k:])

    hbm = pl.BlockSpec(memory_space=pl.ANY)
    return pl.pallas_call(body, name=name, out_shape=comm.out_shape, in_specs=[hbm] * comm.n, out_specs=[hbm] * comm.n,
                          scratch_shapes=comm.scratch)(*comm.arrays)


def _pair_reduce(chunks, name, tr=320):
    _, r, c = chunks.shape
    tr = min(tr, r)
    assert r % tr == 0

    def swap(src_ref, dst_ref, send_sems, recv_sems):
        x, y, core = lax.axis_index("x"), lax.axis_index("y"), lax.axis_index("c")
        copies = [pltpu.make_async_remote_copy(
            src_ref=src_ref.at[2 * k + 1 - core], dst_ref=dst_ref.at[k], send_sem=send_sems.at[k],
            recv_sem=recv_sems.at[k], device_id=(x, y, 1 - core), device_id_type=pl.DeviceIdType.MESH) for k in range(4)]
        for cp in copies:
            cp.start()
        for cp in copies:
            cp.wait()

    hbm = pl.BlockSpec(memory_space=pl.ANY)
    theirs = pl.pallas_call(swap, name=name + "_swap", out_shape=jax.ShapeDtypeStruct((4, r, c), chunks.dtype),
                            in_specs=[hbm], out_specs=hbm,
                            scratch_shapes=[pltpu.SemaphoreType.DMA((4,)), pltpu.SemaphoreType.DMA((4,))])(chunks)

    def add(mine_ref, theirs_ref, out_ref):
        core = lax.axis_index("c")
        both = mine_ref[...].astype(F32)
        out_ref[...] = (jnp.where(core == 0, both[0], both[1]) + theirs_ref[...].astype(F32)).astype(out_ref.dtype)

    return pl.pallas_call(
        add, name=name + "_add", grid=(4, r // tr), out_shape=jax.ShapeDtypeStruct((4, r, c), chunks.dtype),
        in_specs=[pl.BlockSpec((None, 2, tr, c), lambda k, i: (k, 0, i, 0)), pl.BlockSpec((None, tr, c), lambda k, i: (k, i, 0))],
        out_specs=pl.BlockSpec((None, tr, c), lambda k, i: (k, i, 0)),
        compiler_params=_cparams(("parallel", "parallel")),
    )(chunks.reshape(4, 2, r, c), theirs)


def _chip_scatter_transfer(pairs):
    return _Transfer(pairs, pairs.shape, lambda ref, p: ref.at[p], lambda ref, s: ref.at[s], same_core=True)


def _matmul(a, b, mode, *, name, tm, tn, tk, epi=None, extra=(), out_dtype=F32, chunks=None, comm=None):
    if mode == "nn":
        (M, K), N = a.shape, b.shape[1]
    elif mode == "nt":
        (M, K), N = a.shape, b.shape[0]
    else:
        (K, M), N = a.shape, b.shape[1]
    tm, tn, tk = min(tm, M), min(tn, N), min(tk, K)
    assert M % tm == 0 and N % tn == 0 and K % tk == 0 and (epi != "ln" or tn == N), (name, M, N, K)
    nk = K // tk
    a_spec = {"nn": pl.BlockSpec((tm, tk), lambda i, j, k: (i, k)),
              "nt": pl.BlockSpec((tm, tk), lambda i, j, k: (i, k)),
              "tn": pl.BlockSpec((tk, tm), lambda i, j, k: (k, i))}[mode]
    b_spec = {"nn": pl.BlockSpec((tk, tn), lambda i, j, k: (k, j)),
              "nt": pl.BlockSpec((tn, tk), lambda i, j, k: (j, k)),
              "tn": pl.BlockSpec((tk, tn), lambda i, j, k: (k, j))}[mode]
    ca, cb = {"nn": (1, 0), "nt": (1, 1), "tn": (0, 0)}[mode]
    tile = pl.BlockSpec((tm, tn), lambda i, j, k: (i, j))
    row = pl.BlockSpec((1, tn), lambda i, j, k: (0, j))
    n_extra = {None: 0, "add": 1, "relu2": 0, "drelu2": 1, "ln": 3}[epi]
    assert len(extra) == n_extra
    extra_specs = {None: [], "add": [tile], "relu2": [], "drelu2": [tile], "ln": [tile, row, row]}[epi]
    split = 0
    if epi == "relu2":
        out_shape, out_specs = (jax.ShapeDtypeStruct((M, N), BF16),), (tile,)
    elif epi == "ln":
        out_shape = (jax.ShapeDtypeStruct((M, N), F32), jax.ShapeDtypeStruct((M, N), F32),
                     jax.ShapeDtypeStruct((M, N), BF16))
        out_specs = (tile, tile, tile)
    elif chunks == "cols":
        c = N // N_DEV
        out_shape = (jax.ShapeDtypeStruct((N_DEV, M, c), out_dtype),)
        if tn == N:
            split = c
            out_specs = (pl.BlockSpec((N_DEV, tm, c), lambda i, j, k: (0, i, 0)),)
        else:
            assert c % tn == 0
            out_specs = (pl.BlockSpec((None, tm, tn), lambda i, j, k: (j // (c // tn), i, j % (c // tn))),)
    else:
        out_shape, out_specs = (jax.ShapeDtypeStruct((M, N), out_dtype),), (tile,)
    n_out = len(out_shape)

    def body(*refs):
        a_ref, b_ref = refs[:2]
        ex = refs[2:2 + n_extra]
        outs = refs[2 + n_extra:2 + n_extra + n_out]
        acc_ref = refs[-1]
        k = pl.program_id(2)
        part = _dg(_bf(a_ref[...]), _bf(b_ref[...]), ca, cb)

        def finish(acc):
            if epi == "add":
                outs[0][...] = (acc + ALPHA * ex[0][...]).astype(out_dtype)
            elif epi == "relu2":
                r = jnp.maximum(acc, 0.0)
                outs[0][...] = _bf(r * r)
            elif epi == "drelu2":
                outs[0][...] = (acc * (2.0 * jnp.sqrt(ex[0][...].astype(F32)))).astype(out_dtype)
            elif epi == "ln":
                u = ALPHA * ex[0][...] + acc
                xh, _ = _norm_stats(u)
                y = xh * ex[1][...] + ex[2][...]
                outs[0][...] = u
                outs[1][...] = y
                outs[2][...] = _bf(y)
            elif split:
                for p in range(N_DEV):
                    outs[0][p] = acc[:, p * split:(p + 1) * split].astype(out_dtype)
            else:
                outs[0][...] = acc.astype(out_dtype)

        if nk == 1:
            finish(part)
        else:
            @pl.when(k == 0)
            def _():
                acc_ref[...] = part

            @pl.when(jnp.logical_and(k > 0, k < nk - 1))
            def _():
                acc_ref[...] += part

            @pl.when(k == nk - 1)
            def _():
                finish(acc_ref[...] + part)

    outs, landed = _pcall(
        body, name=name, out_shape=out_shape, grid=(M // tm, N // tn, nk),
        in_specs=[a_spec, b_spec] + extra_specs, out_specs=out_specs,
        scratch_shapes=[pltpu.VMEM((tm, tn) if nk > 1 else (8, 128), F32)], sem=("parallel", "parallel", "arbitrary"),
        args=(a, b, *extra), comm=comm)
    res = outs[0] if n_out == 1 else tuple(outs)
    if chunks == "rows":
        res = res.reshape(N_DEV, M // N_DEV, N)
    return res if comm is None else (res, landed)


def _ret_tables(S):
    half = 64
    inv_freq = ROPE_BASE ** (-jnp.arange(half, dtype=F32) / half)
    ang = jnp.arange(S, dtype=jnp.int32).astype(F32)[:, None] * inv_freq[None, :]
    cos, sin = jnp.cos(ang), jnp.sin(ang)
    cosf = jnp.concatenate([cos, cos], axis=1)
    sinf = jnp.concatenate([-sin, sin], axis=1)
    log_g = jnp.log(1.0 - 2.0 ** (-5.0 - jnp.arange(4, dtype=F32)))
    idx = jnp.arange(CHUNK, dtype=F32)
    diff = idx[:, None] - idx[None, :]
    md = jnp.where(diff[None] >= 0, jnp.exp(log_g[:, None, None] * diff[None]), 0.0)
    kd = jnp.exp(log_g[:, None] * (CHUNK - 1 - idx)[None, :])
    qd = jnp.exp(log_g[:, None] * (idx + 1.0)[None, :])
    cd = jnp.exp(log_g * CHUNK)
    bc = lambda t: jnp.broadcast_to(t[:, :, None], (4, CHUNK, CHUNK))
    return cosf, sinf, md, bc(qd), bc(kd), jnp.broadcast_to(cd[:, None, None], (4, 8, CHUNK))


def _rot(x, cosf, sinf):
    return x * cosf + pltpu.roll(x, 64, 1) * sinf


def _rot_t(dx, cosf, sinf):
    return dx * cosf - pltpu.roll(dx, 64, 1) * sinf


def _ret_specs(rev, N):
    rn = (lambda n: N - 1 - n) if rev else (lambda n: n)
    col = lambda c: pl.BlockSpec((CHUNK, 512), lambda n, c=c: (rn(n), c))
    tab = pl.BlockSpec((CHUNK, CHUNK), lambda n: (rn(n), 0))
    dec = pl.BlockSpec((4, CHUNK, CHUNK), lambda n: (0, 0, 0))
    cdec = pl.BlockSpec((4, 8, CHUNK), lambda n: (0, 0, 0))
    vec = pl.BlockSpec((1, 512), lambda n: (0, 0))
    st = pl.BlockSpec((1, 4, CHUNK, CHUNK), lambda n: (rn(n), 0, 0, 0))
    return col, tab, dec, cdec, vec, st


def _ret_fwd(proj, tables, gn_g, gn_b):
    S = proj.shape[0]
    N = S // CHUNK
    col, tab, dec, cdec, vec, st = _ret_specs(False, N)

    def body(q_ref, k_ref, v_ref, g_ref, cos_ref, sin_ref, md_ref, qd_ref, kd_ref, cd_ref, gng_ref, gnb_ref,
             out_ref, st_ref, state):
        @pl.when(pl.program_id(0) == 0)
        def _():
            state[...] = jnp.zeros_like(state)

        cosf, sinf = cos_ref[...], sin_ref[...]
        for h in range(4):
            sl = slice(h * 128, (h + 1) * 128)
            qr = _rot(q_ref[:, sl], cosf, sinf)
            kr = _rot(k_ref[:, sl], cosf, sinf) * (128 ** -0.5)
            vb = _bf(v_ref[:, sl])
            s0 = state[h]
            st_ref[0, h] = s0
            sc = _dg(_bf(qr), _bf(kr), 1, 1) * md_ref[h]
            r = _dg(_bf(sc), vb, 1, 0) + _dg(_bf(qr * qd_ref[h]), _bf(s0), 1, 0)
            state[h] = s0 * cd_ref[h, 0:1, :] + _dg(_bf(kr * kd_ref[h]), vb, 0, 0)
            y, _ = _norm_stats(r)
            rg = g_ref[:, sl]
            out_ref[:, sl] = rg * _sigmoid(rg) * (y * gng_ref[:, sl] + gnb_ref[:, sl])

    return pl.pallas_call(
        body, name="ret_fwd", grid=(N,),
        out_shape=(jax.ShapeDtypeStruct((S, RET_W), F32), jax.ShapeDtypeStruct((N, 4, CHUNK, CHUNK), F32)),
        in_specs=[col(0), col(1), col(2), col(3), tab, tab, dec, dec, dec, cdec, vec, vec],
        out_specs=(pl.BlockSpec((CHUNK, 512), lambda n: (n, 0)), st),
        scratch_shapes=[pltpu.VMEM((4, CHUNK, CHUNK), F32)],
        compiler_params=_cparams(("arbitrary",)),
    )(proj, proj, proj, proj, *tables, gn_g, gn_b)


def _ret_bwd(proj, tables, gn_g, gn_b, states, d_out):
    S = proj.shape[0]
    N = S // CHUNK
    col, tab, dec, cdec, vec, st = _ret_specs(True, N)

    def kernel_body(q_ref, k_ref, v_ref, g_ref, cos_ref, sin_ref, md_ref, qd_ref, kd_ref, cd_ref, gng_ref, gnb_ref,
                    st_ref, do_ref, dp_ref, dg_ref, db_ref, gstate):
        @pl.when(pl.program_id(0) == 0)
        def _():
            gstate[...] = jnp.zeros_like(gstate)
            dg_ref[...] = jnp.zeros_like(dg_ref)
            db_ref[...] = jnp.zeros_like(db_ref)

        cosf, sinf = cos_ref[...], sin_ref[...]
        for h in range(4):
            sl = slice(h * 128, (h + 1) * 128)
            qr = _rot(q_ref[:, sl], cosf, sinf)
            kr = _rot(k_ref[:, sl], cosf, sinf) * (128 ** -0.5)
            qb, kb, vb = _bf(qr), _bf(kr), _bf(v_ref[:, sl])
            s0b = _bf(st_ref[0, h])
            md, qd, kd = md_ref[h], qd_ref[h], kd_ref[h]
            scb = _bf(_dg(qb, kb, 1, 1) * md)
            qdb = _bf(qr * qd)
            kdb = _bf(kr * kd)
            r = _dg(scb, vb, 1, 0) + _dg(qdb, s0b, 1, 0)
            y, rstd = _norm_stats(r)
            gng = gng_ref[:, sl]
            gn = y * gng + gnb_ref[:, sl]
            rg = g_ref[:, sl]
            sg = _sigmoid(rg)
            d_o = do_ref[:, sl]
            d_gn = d_o * (rg * sg)
            dg_ref[:, sl] += jnp.sum(d_gn * y, axis=0, keepdims=True)
            db_ref[:, sl] += jnp.sum(d_gn, axis=0, keepdims=True)
            drb = _bf(_norm_bwd(d_gn * gng, y, rstd))
            g0 = gstate[h]
            gb = _bf(g0)
            dscb = _bf(_dg(drb, vb, 1, 1) * md)
            dqr = _dg(dscb, kb, 1, 0) + _dg(drb, s0b, 1, 1) * qd
            dkr = _dg(dscb, qb, 0, 0) + _dg(vb, gb, 1, 1) * kd
            dv = _dg(scb, drb, 0, 0) + _dg(kdb, gb, 1, 0)
            gstate[h] = g0 * cd_ref[h, 0:1, :] + _dg(qdb, drb, 0, 0)
            dp_ref[:, 0 * 512 + h * 128:0 * 512 + (h + 1) * 128] = _bf(_rot_t(dqr, cosf, sinf))
            dp_ref[:, 1 * 512 + h * 128:1 * 512 + (h + 1) * 128] = _bf(_rot_t(dkr, cosf, sinf) * (128 ** -0.5))
            dp_ref[:, 2 * 512 + h * 128:2 * 512 + (h + 1) * 128] = _bf(dv)
            dp_ref[:, 3 * 512 + h * 128:3 * 512 + (h + 1) * 128] = _bf(d_o * gn * (sg * (1.0 + rg * (1.0 - sg))))

    acc = pl.BlockSpec((1, 512), lambda n: (0, 0))
    return pl.pallas_call(
        kernel_body, name="ret_bwd", grid=(N,),
        out_shape=(jax.ShapeDtypeStruct((S, 2048), BF16), jax.ShapeDtypeStruct((1, 512), F32),
                   jax.ShapeDtypeStruct((1, 512), F32)),
        in_specs=[col(0), col(1), col(2), col(3), tab, tab, dec, dec, dec, cdec, vec, vec, st,
                  pl.BlockSpec((CHUNK, 512), lambda n: (N - 1 - n, 0))],
        out_specs=(pl.BlockSpec((CHUNK, 2048), lambda n: (N - 1 - n, 0)), acc, acc),
        scratch_shapes=[pltpu.VMEM((4, CHUNK, CHUNK), F32)],
        compiler_params=_cparams(("arbitrary",)),
    )(proj, proj, proj, proj, *tables, gn_g, gn_b, states, d_out)


SB_T = 256
SB_SCALE = 64 ** -0.5
SB_Q_COL, SB_K_COL, SB_V_COL = 2048 // 128, 2560 // 128, 3072 // 128


def _head_masks():
    lane = lax.broadcasted_iota(jnp.int32, (1, 128), 1)
    m0 = (lane < 64).astype(F32)
    return m0, 1.0 - m0


def _tri(n, cmp):
    r = lax.broadcasted_iota(jnp.int32, (n, n), 0)
    c = lax.broadcasted_iota(jnp.int32, (n, n), 1)
    return cmp(r, c)


def _tri_sum(x, tri):
    hi = _bf(x)
    lo = _bf(x - hi.astype(F32))
    return _dg(hi, tri, 1, 0) + _dg(lo, tri, 1, 0)


def _sb_weights(qms, kblks, upper, carry, causal):
    tiles = [(b, h) for b in range(len(kblks)) for h in range(2)]
    zs = [_dg(qms[h], kblks[b], 1, 1) for b, h in tiles]
    lgs = [-(jnp.maximum(z, 0.0) + jnp.log(1.0 + jnp.exp(-jnp.abs(z)))) for z in zs]
    if causal is not None:
        lgs = [jnp.where(causal, lg, 0.0) for lg in lgs]
    carries = list(carry)
    for t in range(len(tiles) - 2):
        carries.append(carries[t] + jnp.sum(lgs[t], axis=1, keepdims=True))
    his = [_bf(lg) for lg in lgs]
    los = [_bf(lg - hi.astype(F32)) for lg, hi in zip(lgs, his)]
    later = [_dg(hi, upper, 1, 0) for hi in his]
    later = [r + _dg(lo, upper, 1, 0) for r, lo in zip(later, los)]
    a = [jnp.exp(lg + z + (r + c)) for lg, z, r, c in zip(lgs, zs, later, carries)]
    if causal is not None:
        a = [jnp.where(causal, x, 0.0) for x in a]
    out = tuple(carries[t] + jnp.sum(lgs[t], axis=1, keepdims=True) for t in (len(tiles) - 2, len(tiles) - 1))
    return [a[2 * b:2 * b + 2] for b in range(len(kblks))], out


def _sb_fwd(proj, comm=None):
    S = proj.shape[0]
    T = min(SB_T, S)
    nq = S // T

    def body(q_ref, k_ref, v_ref, o_ref, a_ref, kb_ref, vm_ref, acc_ref):
        i = pl.program_id(1)
        m0, m1 = _head_masks()

        @pl.when(i == 0)
        def _():
            v = v_ref[...]
            kb_ref[...] = _bf(k_ref[...])
            vm_ref[0] = _bf(v * m0)
            vm_ref[1] = _bf(v * m1)

        q = q_ref[...]
        qm = (_bf(q * (m0 * SB_SCALE)), _bf(q * (m1 * SB_SCALE)))
        upper = _tri(T, lambda r, c: r > c).astype(BF16)
        causal = _tri(T, lambda r, c: c < r)

        def tiles(js, carry, mask, first):
            ks = [pl.multiple_of(j * T, T) for j in js]
            a, out = _sb_weights(qm, [kb_ref[pl.ds(k, T), :] for k in ks], upper, carry, mask)
            a = [[_bf(t) for t in per_block] for per_block in a]
            for b, j in enumerate(js):
                for h in range(2):
                    a_ref[h, j] = a[b][h]
            parts = [_dg(a[b][h], vm_ref[h, pl.ds(k, T), :], 1, 0) for b, k in enumerate(ks) for h in range(2)]
            part = functools.reduce(lambda u, w: u + w, parts)
            if first:
                acc_ref[...] = part
            else:
                acc_ref[...] += part
            return out

        zero = jnp.zeros((T, 1), F32)
        carry = tiles([i], (zero, zero), causal, True)
        carry = lax.fori_loop(0, i % 2, lambda _, c: tiles([i - 1], c, None, False), carry)
        top = i - 1 - i % 2
        lax.fori_loop(0, i // 2, lambda jj, c: tiles([top - 2 * jj, top - 2 * jj - 1], c, None, False), carry)
        o_ref[...] = acc_ref[...]

    full = lambda c: pl.BlockSpec((S, 128), lambda p, i, c=c: (0, c + p))
    outs, landed = _pcall(
        body, name="sb_fwd", grid=(4, nq),
        out_shape=[jax.ShapeDtypeStruct((S, SB_W), F32), jax.ShapeDtypeStruct((4, 2, nq, nq, T, T), BF16)],
        in_specs=[pl.BlockSpec((T, 128), lambda p, i: (i, SB_Q_COL + p)), full(SB_K_COL), full(SB_V_COL)],
        out_specs=[pl.BlockSpec((T, 128), lambda p, i: (i, p)),
                   pl.BlockSpec((None, 2, None, nq, T, T), lambda p, i: (p, 0, i, 0, 0, 0))],
        scratch_shapes=[pltpu.VMEM((S, 128), BF16), pltpu.VMEM((2, S, 128), BF16), pltpu.VMEM((T, 128), F32)],
        sem=("arbitrary", "arbitrary"), args=(proj, proj, proj), comm=comm)
    return tuple(outs) if comm is None else (tuple(outs), landed)


def _sb_bwd(proj, a_saved, d_o, comm=None):
    S = proj.shape[0]
    T = min(SB_T, S)
    nq = S // T

    def body(q_ref, k_ref, v_ref, do_ref, a_ref, dq_ref, dk_ref, dv_ref, kb_ref, kbm_ref, vb_ref, dq_acc, dk_acc, dv_acc):
        i = pl.program_id(1)
        m0, m1 = _head_masks()

        @pl.when(i == 0)
        def _():
            k = k_ref[...]
            kb_ref[...] = _bf(k)
            kbm_ref[0] = _bf(k * m0)
            kbm_ref[1] = _bf(k * m1)
            vb_ref[...] = _bf(v_ref[...])
            dk_acc[...] = jnp.zeros_like(dk_acc)
            dv_acc[...] = jnp.zeros_like(dv_acc)

        q, d_out = q_ref[...], do_ref[...]
        qm = (_bf(q * (m0 * SB_SCALE)), _bf(q * (m1 * SB_SCALE)))
        dom = (_bf(d_out * m0), _bf(d_out * m1))
        lower = _tri(T, lambda r, c: r < c).astype(BF16)
        causal = _tri(T, lambda r, c: c < r)

        def up(js, carry, mask):
            ks = [pl.multiple_of(j * T, T) for j in js]
            tiles = [(b, h) for b in range(len(js)) for h in range(2)]
            zs = [_dg(qm[h], kb_ref[pl.ds(ks[b], T), :], 1, 1) for b, h in tiles]
            a = [a_ref[h, js[b]] for b, h in tiles]
            es = [w.astype(F32) * _dg(dom[h], vb_ref[pl.ds(ks[b], T), :], 1, 1) for w, (b, h) in zip(a, tiles)]
            carries = list(carry)
            for t in range(len(tiles)):
                carries.append(carries[t] + jnp.sum(es[t], axis=1, keepdims=True))
            his = [_bf(e) for e in es]
            los = [_bf(e - hi.astype(F32)) for e, hi in zip(es, his)]
            d_lg = [_dg(hi, lower, 1, 0) for hi in his]
            d_lg = [r + _dg(lo, lower, 1, 0) + c for r, lo, c in zip(d_lg, los, carries)]
            ens = [jnp.exp(-jnp.abs(z)) for z in zs]
            invs = [1.0 / (1.0 + en) for en in ens]
            betas = [jnp.where(z >= 0.0, inv, en * inv) for z, en, inv in zip(zs, ens, invs)]
            dzs = [e * (1.0 - b) - d * b for e, b, d in zip(es, betas, d_lg)]
            if mask is not None:
                dzs = [jnp.where(mask, dz, 0.0) for dz in dzs]
            dzs = [_bf(dz) for dz in dzs]
            parts = [_dg(dzs[t], kbm_ref[h, pl.ds(ks[b], T), :], 1, 0) for t, (b, h) in enumerate(tiles)]
            dq_acc[...] += functools.reduce(lambda u, w: u + w, parts)
            for b, k in enumerate(ks):
                dk_acc[pl.ds(k, T), :] += _dg(dzs[2 * b], qm[0], 0, 0) + _dg(dzs[2 * b + 1], qm[1], 0, 0)
                dv_acc[pl.ds(k, T), :] += _dg(a[2 * b], dom[0], 0, 0) + _dg(a[2 * b + 1], dom[1], 0, 0)
            return tuple(carries[-2:])

        zero = jnp.zeros((T, 1), F32)
        dq_acc[...] = jnp.zeros_like(dq_acc)
        carry = lax.fori_loop(0, i // 2, lambda jj, c: up([2 * jj, 2 * jj + 1], c, None), (zero, zero))
        carry = lax.fori_loop(0, i % 2, lambda _, c: up([i - 1], c, None), carry)
        up([i], carry, causal)
        dq_ref[...] = _bf(dq_acc[...] * SB_SCALE)

        @pl.when(i == nq - 1)
        def _():
            dk_ref[...] = _bf(dk_acc[...])
            dv_ref[...] = _bf(dv_acc[...])

    full = lambda c: pl.BlockSpec((S, 128), lambda p, i, c=c: (0, c + p))
    tile = pl.BlockSpec((T, 128), lambda p, i: (i, p))
    acc = pl.BlockSpec((S, 128), lambda p, i: (0, p))
    out = jax.ShapeDtypeStruct((S, SB_W), BF16)
    outs, landed = _pcall(
        body, name="sb_bwd", grid=(4, nq), out_shape=[out, out, out],
        in_specs=[pl.BlockSpec((T, 128), lambda p, i: (i, SB_Q_COL + p)), full(SB_K_COL), full(SB_V_COL), tile,
                  pl.BlockSpec((None, 2, None, nq, T, T), lambda p, i: (p, 0, i, 0, 0, 0))],
        out_specs=[tile, acc, acc],
        scratch_shapes=[pltpu.VMEM((S, 128), BF16), pltpu.VMEM((2, S, 128), BF16), pltpu.VMEM((S, 128), BF16),
                        pltpu.VMEM((T, 128), F32), pltpu.VMEM((S, 128), F32), pltpu.VMEM((S, 128), F32)],
        sem=("arbitrary", "arbitrary"), args=(proj, proj, proj, d_o, a_saved), comm=comm)
    return tuple(outs) if comm is None else (tuple(outs), landed)


SGU_U_COL, SGU_V_COL = 3584 // 512, 4096 // 512


def _causal(w):
    r = lax.broadcasted_iota(jnp.int32, (CHUNK, CHUNK), 0)
    c = lax.broadcasted_iota(jnp.int32, (CHUNK, CHUNK), 1)
    return jnp.where(r >= c, w, 0.0)


def _sgu_fwd(proj, ln_g, ln_b, w, b):
    S = proj.shape[0]
    N = S // CHUNK

    def body(u_ref, v_ref, g_ref, b_ref, w_ref, bias_ref, out_ref):
        u = _gelu(u_ref[...])
        xh, _ = _norm_stats(_gelu(v_ref[...]))
        vn = _bf(xh * g_ref[...] + b_ref[...])
        for g in range(4):
            sl = slice(g * 128, (g + 1) * 128)
            sv = _dg(_bf(_causal(w_ref[g])), vn[:, sl], 1, 0) + bias_ref[g]
            out_ref[:, sl] = u[:, sl] * sv

    vec = pl.BlockSpec((1, 512), lambda n: (0, 0))
    return pl.pallas_call(
        body, name="sgu_fwd", grid=(N,),
        out_shape=jax.ShapeDtypeStruct((S, SGU_W), F32),
        in_specs=[pl.BlockSpec((CHUNK, 512), lambda n: (n, SGU_U_COL)),
                  pl.BlockSpec((CHUNK, 512), lambda n: (n, SGU_V_COL)), vec, vec,
                  pl.BlockSpec((4, CHUNK, CHUNK), lambda n: (0, 0, 0)), pl.BlockSpec((4, CHUNK, 1), lambda n: (0, 0, 0))],
        out_specs=pl.BlockSpec((CHUNK, 512), lambda n: (n, 0)),
        compiler_params=_cparams(("parallel",)),
    )(proj, proj, ln_g, ln_b, w, b)


def _sgu_bwd(proj, ln_g, ln_b, w, b, d_out):
    S = proj.shape[0]
    N = S // CHUNK

    def body(u_ref, v_ref, g_ref, b_ref, w_ref, bias_ref, do_ref, dp_ref, dg_ref, db_ref, dw_ref, dbias_ref):
        @pl.when(pl.program_id(0) == 0)
        def _():
            dg_ref[...] = jnp.zeros_like(dg_ref)
            db_ref[...] = jnp.zeros_like(db_ref)
            dw_ref[...] = jnp.zeros_like(dw_ref)
            dbias_ref[...] = jnp.zeros_like(dbias_ref)

        gu, gv = u_ref[...], v_ref[...]
        u = _gelu(gu)
        xh, rstd = _norm_stats(_gelu(gv))
        ln_gain = g_ref[...]
        vn = _bf(xh * ln_gain + b_ref[...])
        d_o = do_ref[...]
        d_vn = []
        for g in range(4):
            sl = slice(g * 128, (g + 1) * 128)
            wc = _bf(_causal(w_ref[g]))
            sv = _dg(wc, vn[:, sl], 1, 0) + bias_ref[g]
            dp_ref[:, sl] = _bf(d_o[:, sl] * sv * _gelu_grad(gu[:, sl]))
            d_sv = d_o[:, sl] * u[:, sl]
            dbias_ref[g] += jnp.sum(d_sv, axis=1, keepdims=True)
            d_svb = _bf(d_sv)
            dw_ref[g] += _causal(_dg(d_svb, vn[:, sl], 1, 1))
            d_vn.append(_dg(wc, d_svb, 0, 0))
        d_vn = jnp.concatenate(d_vn, axis=1)
        dg_ref[...] += jnp.sum(d_vn * xh, axis=0, keepdims=True)
        db_ref[...] += jnp.sum(d_vn, axis=0, keepdims=True)
        dp_ref[:, 512:1024] = _bf(_norm_bwd(d_vn * ln_gain, xh, rstd) * _gelu_grad(gv))

    vec = pl.BlockSpec((1, 512), lambda n: (0, 0))
    wspec = pl.BlockSpec((4, CHUNK, CHUNK), lambda n: (0, 0, 0))
    bspec = pl.BlockSpec((4, CHUNK, 1), lambda n: (0, 0, 0))
    return pl.pallas_call(
        body, name="sgu_bwd", grid=(N,),
        out_shape=(jax.ShapeDtypeStruct((S, 1024), BF16), jax.ShapeDtypeStruct((1, 512), F32),
                   jax.ShapeDtypeStruct((1, 512), F32), jax.ShapeDtypeStruct((4, CHUNK, CHUNK), F32),
                   jax.ShapeDtypeStruct((4, CHUNK, 1), F32)),
        in_specs=[pl.BlockSpec((CHUNK, 512), lambda n: (n, SGU_U_COL)),
                  pl.BlockSpec((CHUNK, 512), lambda n: (n, SGU_V_COL)), vec, vec, wspec, bspec,
                  pl.BlockSpec((CHUNK, 512), lambda n: (n, 0))],
        out_specs=(pl.BlockSpec((CHUNK, 1024), lambda n: (n, 0)), vec, vec, wspec, bspec),
        compiler_params=_cparams(("arbitrary",)),
    )(proj, proj, ln_g, ln_b, w, b, d_out)


GATE_COL = 4608 // 512


def _merge_fwd(proj, branches, p_list, tm=512):
    S = proj.shape[0]
    tm = min(tm, S)

    def body(r_ref, s_ref, g_ref, pr_ref, ps_ref, pg_ref, gr_ref, gs_ref, gg_ref, m_ref, br_ref):
        acc = None
        for k, (x_ref, p_ref, gate_ref) in enumerate(((r_ref, pr_ref, gr_ref), (s_ref, ps_ref, gs_ref),
                                                      (g_ref, pg_ref, gg_ref))):
            br = _dg(_bf(x_ref[...]), _bf(p_ref[...]), 1, 0)
            br_ref[k] = br
            term = _sigmoid(gate_ref[...]) * br
            acc = term if acc is None else acc + term
        m_ref[...] = _bf(acc)

    xs = pl.BlockSpec((tm, 512), lambda i, n: (i, 0))
    ps = pl.BlockSpec((512, 512), lambda i, n: (0, n))
    gate = lambda k: pl.BlockSpec((tm, 512), lambda i, n, k=k: (i, GATE_COL + 2 * k + n))
    return pl.pallas_call(
        body, name="merge_fwd", grid=(S // tm, 2),
        out_shape=(jax.ShapeDtypeStruct((S, D_MODEL), BF16), jax.ShapeDtypeStruct((3, S, D_MODEL), F32)),
        in_specs=[xs, xs, xs, ps, ps, ps, gate(0), gate(1), gate(2)],
        out_specs=(pl.BlockSpec((tm, 512), lambda i, n: (i, n)), pl.BlockSpec((3, tm, 512), lambda i, n: (0, i, n))),
        compiler_params=_cparams(("parallel", "parallel")),
    )(*branches, *p_list, proj, proj, proj)


def _gate_bwd(proj, br, d_merged, tm=512):
    S = proj.shape[0]
    tm = min(tm, S)

    def body(dm_ref, br_ref, gr_ref, gs_ref, gg_ref, *out_refs):
        dm = dm_ref[...]
        for k, gate_ref in enumerate((gr_ref, gs_ref, gg_ref)):
            s = _sigmoid(gate_ref[...])
            out_refs[k][...] = _bf(dm * s)
            out_refs[3 + k][...] = _bf(dm * br_ref[k] * (s * (1.0 - s)))

    gate = lambda k: pl.BlockSpec((tm, 512), lambda i, n, k=k: (i, GATE_COL + 2 * k + n))
    three = pl.BlockSpec((3, tm, 512), lambda i, n: (0, i, n))
    tile = pl.BlockSpec((tm, 512), lambda i, n: (i, n))
    outs = pl.pallas_call(
        body, name="gate_bwd", grid=(S // tm, 2),
        out_shape=[jax.ShapeDtypeStruct((S, D_MODEL), BF16)] * 6,
        in_specs=[tile, three, gate(0), gate(1), gate(2)], out_specs=[tile] * 6,
        compiler_params=_cparams(("parallel", "parallel")),
    )(d_merged, br, proj, proj, proj)
    return outs[:3], outs[3:]


def _ln_bwd(dy, u, g, tm=256):
    S, D = u.shape
    tm = min(tm, S)

    def body(dy_ref, u_ref, g_ref, du_ref, dub_ref, dg_ref, db_ref):
        @pl.when(pl.program_id(0) == 0)
        def _():
            dg_ref[...] = jnp.zeros_like(dg_ref)
            db_ref[...] = jnp.zeros_like(db_ref)

        dy_t = dy_ref[...]
        xh, rstd = _norm_stats(u_ref[...])
        dg_ref[...] += jnp.sum(dy_t * xh, axis=0, keepdims=True)
        db_ref[...] += jnp.sum(dy_t, axis=0, keepdims=True)
        du = _norm_bwd(dy_t * g_ref[...], xh, rstd)
        du_ref[...] = du
        dub_ref[...] = _bf(du)

    tile = pl.BlockSpec((tm, D), lambda i: (i, 0))
    vec = pl.BlockSpec((1, D), lambda i: (0, 0))
    return pl.pallas_call(
        body, name="ln_bwd", grid=(S // tm,),
        out_shape=(jax.ShapeDtypeStruct((S, D), F32), jax.ShapeDtypeStruct((S, D), BF16),
                   jax.ShapeDtypeStruct((1, D), F32), jax.ShapeDtypeStruct((1, D), F32)),
        in_specs=[tile, tile, vec], out_specs=(tile, tile, vec, vec),
        compiler_params=_cparams(("arbitrary",)),
    )(dy, u, g)


def _loss_grad(y, target, tm=256):
    S, D = y.shape
    tm = min(tm, S)

    def body(y_ref, t_ref, dy_ref, sq_ref):
        @pl.when(pl.program_id(0) == 0)
        def _():
            sq_ref[...] = jnp.zeros_like(sq_ref)

        err = y_ref[...] - t_ref[...]
        dy_ref[...] = err * (1.0 / D)
        sq_ref[...] += jnp.sum(err * err, axis=0, keepdims=True)

    tile = pl.BlockSpec((tm, D), lambda i: (i, 0))
    vec = pl.BlockSpec((1, D), lambda i: (0, 0))
    return pl.pallas_call(
        body, name="loss_grad", grid=(S // tm,),
        out_shape=(jax.ShapeDtypeStruct((S, D), F32), jax.ShapeDtypeStruct((1, D), F32)),
        in_specs=[tile, tile], out_specs=(tile, vec),
        compiler_params=_cparams(("arbitrary",)),
    )(y, target)


def _layer_fwd(x, x_bf, W, tables, sb_comm=None):
    proj = _matmul(x_bf, W["w_in_t"], "nt", name="proj", tm=1024, tn=768, tk=1024)
    retg, states = _ret_fwd(proj, tables, W["ret_gn_g"], W["ret_gn_b"])
    if sb_comm is None:
        sb, sb_a = _sb_fwd(proj)
    else:
        (sb, sb_a), landed = _sb_fwd(proj, comm=sb_comm[0])
        sb_comm[1](landed)
    sg = _sgu_fwd(proj, W["sgu_ln_g"], W["sgu_ln_b"], W["sgu_w"], W["sgu_b"])
    merged, br = _merge_fwd(proj, (retg, sb, sg), (W["p_ret"], W["p_sb"], W["p_sgu"]))
    u1, x1, x1_bf = _matmul(merged, W["w_out"], "nn", name="out_ln", tm=512, tn=1024, tk=1024, epi="ln",
                            extra=(x, W["ln1_g"], W["ln1_b"]))
    act = _matmul(x1_bf, W["w_up"], "nn", name="up", tm=1024, tn=1024, tk=1024, epi="relu2")
    u2, x2, x2_bf = _matmul(act, W["w_down"], "nn", name="down_ln", tm=512, tn=1024, tk=4096, epi="ln",
                            extra=(x1, W["ln2_g"], W["ln2_b"]))
    saved = dict(x_bf=x_bf, proj=proj, retg=retg, states=states, sb=sb, sb_a=sb_a, sg=sg, merged=merged, br=br, u1=u1,
                 x1_bf=x1_bf, act=act, u2=u2)
    return x2, x2_bf, saved


def _layer_bwd(d_x2, W, tables, sv, chunk_dtype=None, sb_comm_fn=None, dx_comm_fn=None):
    dt = F32 if chunk_dtype is None else chunk_dtype
    rows, cols = (None, None) if chunk_dtype is None else ("rows", "cols")
    g, landed = {}, {}
    du2, du2_bf, g["ln2_g"], g["ln2_b"] = _ln_bwd(d_x2, sv["u2"], W["ln2_g"])
    d_hpre = _matmul(du2_bf, W["w_down"], "nt", name="d_act", tm=1024, tn=1024, tk=1024, epi="drelu2",
                     extra=(sv["act"],), out_dtype=BF16)
    g["w_down"] = _matmul(sv["act"], du2_bf, "tn", name="dw_down", tm=512, tn=1024, tk=4096, out_dtype=dt, chunks=rows)
    g["w_up"] = _matmul(sv["x1_bf"], d_hpre, "tn", name="dw_up", tm=1024, tn=512, tk=4096, out_dtype=dt, chunks=cols)
    d_x1 = _matmul(d_hpre, W["w_up"], "nt", name="d_x1", tm=512, tn=1024, tk=4096, epi="add", extra=(du2,))
    du1, du1_bf, g["ln1_g"], g["ln1_b"] = _ln_bwd(d_x1, sv["u1"], W["ln1_g"])
    d_merged = _matmul(du1_bf, W["w_out"], "nt", name="d_merged", tm=1024, tn=1024, tk=1024)
    g["w_out"] = _matmul(sv["merged"], du1_bf, "tn", name="dw_out", tm=1024, tn=512, tk=4096, out_dtype=dt, chunks=rows)
    d_br, d_gate = _gate_bwd(sv["proj"], sv["br"], d_merged)
    d_branch = []
    for k, (nm, act) in enumerate((("p_ret", sv["retg"]), ("p_sb", sv["sb"]), ("p_sgu", sv["sg"]))):
        d_branch.append(_matmul(d_br[k], W[nm], "nt", name="d_" + nm[2:], tm=1024, tn=512, tk=1024))
        g[nm] = _matmul(act, d_br[k], "tn", name="dw_" + nm[2:], tm=512, tn=1024, tk=2048, out_dtype=dt, chunks=cols)
    d_ret, g["ret_gn_g"], g["ret_gn_b"] = _ret_bwd(sv["proj"], tables, W["ret_gn_g"], W["ret_gn_b"], sv["states"],
                                                   d_branch[0])
    if sb_comm_fn is None:
        d_sq, d_sk, d_sv = _sb_bwd(sv["proj"], sv["sb_a"], d_branch[1])
    else:
        (d_sq, d_sk, d_sv), landed["sb"] = _sb_bwd(sv["proj"], sv["sb_a"], d_branch[1], comm=sb_comm_fn(g))
    d_sgu, g["sgu_ln_g"], g["sgu_ln_b"], g["sgu_w"], g["sgu_b"] = _sgu_bwd(
        sv["proj"], W["sgu_ln_g"], W["sgu_ln_b"], W["sgu_w"], W["sgu_b"], d_branch[2])
    d_proj = jnp.concatenate([d_ret, d_sq, d_sk, d_sv, d_sgu, d_gate[0], d_gate[1], d_gate[2]], axis=1)
    g["w_in"] = _matmul(d_proj, sv["x_bf"], "tn", name="dw_in", tm=768, tn=1024, tk=4096, out_dtype=dt, chunks=rows)
    if chunk_dtype is None:
        g["w_in"] = g["w_in"].T
    d_x = _matmul(d_proj, W["w_in_t"], "nn", name="d_x", tm=512, tn=1024, tk=3840, epi="add", extra=(du1,),
                  comm=None if dx_comm_fn is None else dx_comm_fn(g))
    if dx_comm_fn is not None:
        d_x, landed["dx"] = d_x
    return d_x, g, landed


BIG = ("w_in", "p_ret", "p_sb", "p_sgu", "w_out", "w_up", "w_down")
SMALL = ("ret_gn_g", "ret_gn_b", "sgu_ln_g", "sgu_ln_b", "sgu_w", "sgu_b", "ln1_g", "ln1_b", "ln2_g", "ln2_b")
GATHER_KIND = {"w_in": "rows", "p_ret": "cols", "p_sb": "cols", "p_sgu": "cols", "w_out": "rows", "w_up": "cols",
               "w_down": "rows"}


def _small_weights(small, l):
    W = {}
    for n in SMALL:
        if n == "sgu_w":
            W[n] = small[n][l]
        elif n == "sgu_b":
            W[n] = small[n][l].reshape(4, CHUNK, 1)
        else:
            W[n] = small[n][l].reshape(1, -1)
    return W


def _local_step(x, target, full, small):
    tables = _ret_tables(x.shape[0])
    Ws = [{**{n: full[n][l] for n in BIG[1:]}, "w_in_t": full["w_in"][l].T, **_small_weights(small, l)}
          for l in range(DEPTH)]
    saved = []
    h, h_bf = x, _bf(x)
    for l in range(DEPTH):
        h, h_bf, sv = _layer_fwd(h, h_bf, Ws[l], tables)
        saved.append(sv)
    d_h, sq = _loss_grad(h, target)
    grads = [None] * DEPTH
    for l in reversed(range(DEPTH)):
        d_h, grads[l], _ = _layer_bwd(d_h, Ws[l], tables, saved[l])
    return sq, d_h, grads


def _adam(w, parts, m, v, name):
    L, R, C = w.shape
    tr = next(t for t in (320, 256, 128) if R % t == 0)
    assert len(parts) == L

    def body(*refs):
        w_ref, p_refs, (m_ref, v_ref, g_ref, d_ref, nm_ref, nv_ref) = refs[0], refs[1:1 + L], refs[1 + L:]
        layer = pl.program_id(0)
        g = None
        for li, p_ref in enumerate(p_refs):
            s = p_ref[0].astype(F32)
            for j in range(1, p_ref.shape[0]):
                s = s + p_ref[j].astype(F32)
            g = s if g is None else jnp.where(layer == li, s, g)
        g_ref[...] = g
        d_ref[...], nm_ref[...], nv_ref[...] = _adam_update(w_ref[...], g, m_ref[...], v_ref[...])

    tile = pl.BlockSpec((None, tr, C), lambda l, i: (l, i, 0))
    part = lambda li: pl.BlockSpec((parts[li].shape[0], tr, C), lambda l, i, li=li: (0, jnp.where(l == li, i, 0), 0))
    out = jax.ShapeDtypeStruct((L, R, C), F32)
    return pl.pallas_call(
        body, name=name, grid=(L, R // tr), out_shape=(out, out, out, out),
        in_specs=[tile] + [part(li) for li in range(L)] + [tile, tile],
        out_specs=(tile, tile, tile, tile),
        compiler_params=_cparams(("parallel", "parallel")),
    )(w, *parts, m, v)


def _adam_update(w, g, m, v):
    m2 = ADAM_B1 * m + (1.0 - ADAM_B1) * g
    v2 = ADAM_B2 * v + (1.0 - ADAM_B2) * (g * g)
    m_hat = m2 / (1.0 - ADAM_B1 ** ADAM_STEP)
    v_hat = v2 / (1.0 - ADAM_B2 ** ADAM_STEP)
    return -ADAM_LR * (m_hat / (jnp.sqrt(v_hat) + ADAM_EPS) + ADAM_WD * w), m2, v2


def _adam_small(w, m, v, parts):
    k = len(SMALL)

    def body(*refs):
        w_refs, m_refs, v_refs, p_refs, outs = refs[:k], refs[k:2 * k], refs[2 * k:3 * k], refs[3 * k:5 * k], refs[5 * k:]
        for i in range(k):
            vector = len(w_refs[i].shape) == 2
            for l in range(DEPTH):
                p_ref = p_refs[DEPTH * i + l]
                g = p_ref[0]
                for j in range(1, N_DEV):
                    g = g + p_ref[j]
                at = (slice(l, l + 1), slice(None)) if vector else (l,)
                delta, m2, v2 = _adam_update(w_refs[i][at], g, m_refs[i][at], v_refs[i][at])
                for o_ref, val in zip(outs[4 * i:4 * i + 4], (g, delta, m2, v2)):
                    o_ref[at] = val

    vmem = pl.BlockSpec(memory_space=pltpu.VMEM)
    args = [w[n] for n in SMALL] + [m[n] for n in SMALL] + [v[n] for n in SMALL] + \
           [parts[(n, l)] for n in SMALL for l in range(DEPTH)]
    out_shape = [jax.ShapeDtypeStruct(w[n].shape, F32) for n in SMALL for _ in range(4)]
    outs = pl.pallas_call(body, name="adam_small", out_shape=out_shape, in_specs=[vmem] * len(args),
                          out_specs=[vmem] * len(out_shape), compiler_params=_cparams())(*args)
    return {n: tuple(outs[4 * i:4 * i + 4]) for i, n in enumerate(SMALL)}


WEIGHTS = ("w_in", "ret_gn_g", "ret_gn_b", "sgu_ln_g", "sgu_ln_b", "sgu_w", "sgu_b", "p_ret", "p_sb", "p_sgu", "w_out",
           "ln1_g", "ln1_b", "w_up", "w_down", "ln2_g", "ln2_b")


def kernel(x, w_in, ret_gn_g, ret_gn_b, sgu_ln_g, sgu_ln_b, sgu_w, sgu_b, p_ret, p_sb, p_sgu, w_out, ln1_g, ln1_b, w_up, w_down, ln2_g, ln2_b, loss_target, m_w_in, m_ret_gn_g, m_ret_gn_b, m_sgu_ln_g, m_sgu_ln_b, m_sgu_w, m_sgu_b, m_p_ret, m_p_sb, m_p_sgu, m_w_out, m_ln1_g, m_ln1_b, m_w_up, m_w_down, m_ln2_g, m_ln2_b, v_w_in, v_ret_gn_g, v_ret_gn_b, v_sgu_ln_g, v_sgu_ln_b, v_sgu_w, v_sgu_b, v_p_ret, v_p_sb, v_p_sgu, v_w_out, v_ln1_g, v_ln1_b, v_w_up, v_w_down, v_ln2_g, v_ln2_b):
    w = dict(zip(WEIGHTS, (w_in, ret_gn_g, ret_gn_b, sgu_ln_g, sgu_ln_b, sgu_w, sgu_b, p_ret, p_sb, p_sgu, w_out,
                           ln1_g, ln1_b, w_up, w_down, ln2_g, ln2_b)))
    m = dict(zip(WEIGHTS, (m_w_in, m_ret_gn_g, m_ret_gn_b, m_sgu_ln_g, m_sgu_ln_b, m_sgu_w, m_sgu_b, m_p_ret, m_p_sb,
                           m_p_sgu, m_w_out, m_ln1_g, m_ln1_b, m_w_up, m_w_down, m_ln2_g, m_ln2_b)))
    v = dict(zip(WEIGHTS, (v_w_in, v_ret_gn_g, v_ret_gn_b, v_sgu_ln_g, v_sgu_ln_b, v_sgu_w, v_sgu_b, v_p_ret, v_p_sb,
                           v_p_sgu, v_w_out, v_ln1_g, v_ln1_b, v_w_up, v_w_down, v_ln2_g, v_ln2_b)))

    small = {n: w[n] for n in SMALL}
    shard = {n: _bf(w[n]) for n in BIG}
    shard["w_in"] = shard["w_in"].transpose(0, 2, 1)
    S = x.shape[1]
    x0, target = x.reshape(S, D_MODEL), loss_target.reshape(S, D_MODEL)
    tables = _ret_tables(S)
    Ws = [_small_weights(small, l) for l in range(DEPTH)]

    (Ws[0]["w_in_t"],) = _exchange([_gather_transfer(shard["w_in"], 0, "rows")], "gather_w_in0", relay=True)
    later = [(n, 0) for n in BIG[1:]] + [(n, 1) for n in BIG]

    def weights_landed(landed):
        for (n, l), z in zip(later, landed):
            Ws[l]["w_in_t" if n == "w_in" else n] = z

    gather = _Comm([_gather_transfer(shard[n], l, GATHER_KIND[n]) for n, l in later], relay=True)
    h, h_bf, saved0 = _layer_fwd(x0, _bf(x0), Ws[0], tables, sb_comm=(gather, weights_landed))
    h, _, saved1 = _layer_fwd(h, h_bf, Ws[1], tables)
    d_h, sq = _loss_grad(h, target)
    loss = lax.psum(0.5 * jnp.sum(sq) / D_MODEL, ("x", "y", "c"))

    d_h, g1, landed1 = _layer_bwd(d_h, Ws[1], tables, saved1, chunk_dtype=BF16,
                                  sb_comm_fn=lambda g: _Comm([_scatter_transfer(g[n]) for n in BIG[1:]]))
    early = [("w_in", 1)] + [(n, 0) for n in BIG[1:]]

    def small_slabs(g):
        return [_slab_transfer(g[n].reshape(4, CHUNK) if n == "sgu_b" else g[n]) for n in SMALL]

    def early_scatter(g0):
        return _Comm([_scatter_transfer((g1 if l else g0)[n]) for n, l in early] + small_slabs(g1))

    def late_scatter(g0):
        pairs = _pair_reduce(g0["w_in"], "w_in0_pairs")
        return _Comm([_chip_scatter_transfer(pairs)] + small_slabs(g0))

    d_x, g0, landed = _layer_bwd(d_h, Ws[0], tables, saved0, chunk_dtype=BF16, sb_comm_fn=early_scatter,
                                 dx_comm_fn=late_scatter)
    parts = {**dict(zip(early, landed["sb"])), **{(n, 1): z for n, z in zip(BIG[1:], landed1["sb"])}}
    parts[("w_in", 0)] = landed["dx"][0]
    small_parts = {**{(n, 1): z for n, z in zip(SMALL, landed["sb"][len(early):])},
                   **{(n, 0): z for n, z in zip(SMALL, landed["dx"][1:])}}

    grad, delta, new_m, new_v = {}, {}, {}, {}
    for n in BIG:
        view = (lambda a: a.transpose(0, 2, 1)) if n == "w_in" else (lambda a: a)
        res = _adam(view(w[n]), [parts[(n, l)] for l in range(DEPTH)], view(m[n]), view(v[n]), "adam_" + n)
        grad[n], delta[n], new_m[n], new_v[n] = (view(r) for r in res)
    for n, res in _adam_small(small, m, v, small_parts).items():
        grad[n], delta[n], new_m[n], new_v[n] = res

    return (loss, d_x.reshape(x.shape), *[grad[n] for n in WEIGHTS], *[delta[n] for n in WEIGHTS],
            *[new_m[n] for n in WEIGHTS], *[new_v[n] for n in WEIGHTS])
```

```python
import functools
import math

import numpy as np
import jax
import jax.numpy as jnp
from jax import lax
from jax.experimental import pallas as pl
from jax.experimental.pallas import tpu as pltpu

F32 = jnp.float32
BF16 = jnp.bfloat16

N_DEV = 8
DEPTH = 2
D_MODEL = 1024
CHUNK = 128
RET_W = 512
SB_W = 512
SGU_W = 512
N_IN = 7680
LN_EPS = 1e-5
ALPHA = (2 * DEPTH) ** 0.25
ROPE_BASE = 10000.0
ADAM_LR, ADAM_B1, ADAM_B2, ADAM_EPS, ADAM_WD, ADAM_STEP = 0.001, 0.9, 0.999, 1e-08, 0.01, 10
VMEM_LIMIT = 56 * 1024 * 1024

_GELU_K = math.sqrt(2.0 / math.pi)
_GELU_C = 0.044715


def _cparams(sem=None):
    return pltpu.CompilerParams(dimension_semantics=sem, vmem_limit_bytes=VMEM_LIMIT)


def _dg(a, b, ca, cb):
    return lax.dot_general(a, b, (((ca,), (cb,)), ((), ())), preferred_element_type=F32)


def _bf(x):
    return x.astype(BF16)


def _sigmoid(x):
    return 1.0 / (1.0 + jnp.exp(-x))


def _gelu(x):
    t = jnp.tanh(_GELU_K * (x + _GELU_C * (x * x * x)))
    return x * (0.5 * (1.0 + t))


def _gelu_grad(x):
    t = jnp.tanh(_GELU_K * (x + _GELU_C * (x * x * x)))
    return 0.5 * (1.0 + t) + 0.5 * x * (1.0 - t * t) * (_GELU_K * (1.0 + 3.0 * _GELU_C * x * x))


def _norm_stats(u):
    mu = jnp.mean(u, axis=-1, keepdims=True)
    d = u - mu
    var = jnp.mean(d * d, axis=-1, keepdims=True)
    rstd = lax.rsqrt(var + LN_EPS)
    return d * rstd, rstd


def _norm_bwd(dxh, xh, rstd):
    return rstd * (dxh - jnp.mean(dxh, axis=-1, keepdims=True) - xh * jnp.mean(dxh * xh, axis=-1, keepdims=True))


class _Transfer:
    def __init__(self, src, dst_shape, src_at, dst_at, same_core=False):
        self.src, self.dst_shape, self.src_at, self.dst_at = src, tuple(dst_shape), src_at, dst_at
        self.same_core = same_core


def _gather_transfer(shard, l, kind):
    _, r, c = shard.shape
    src_at = lambda ref, p: ref.at[l]
    if kind == "slab":
        return _Transfer(shard, (N_DEV, r, c), src_at, lambda ref, s: ref.at[s])
    if kind == "rows":
        return _Transfer(shard, (N_DEV * r, c), src_at, lambda ref, s: ref.at[pl.ds(pl.multiple_of(s * r, r), r), :])
    return _Transfer(shard, (r, N_DEV * c), src_at, lambda ref, s: ref.at[:, pl.ds(pl.multiple_of(s * c, c), c)])


def _scatter_transfer(chunks):
    return _Transfer(chunks, chunks.shape, lambda ref, p: ref.at[p], lambda ref, s: ref.at[s])


def _slab_transfer(arr):
    return _Transfer(arr, (N_DEV,) + arr.shape, lambda ref, p: ref, lambda ref, s: ref.at[s])


class _Comm:
    def __init__(self, transfers, relay=False):
        self.transfers = list(transfers)
        self.relay = relay
        self.n = len(self.transfers)
        self.arrays = [t.src for t in self.transfers]
        self.out_shape = [jax.ShapeDtypeStruct(t.dst_shape, t.src.dtype) for t in self.transfers]
        self.scratch = [pltpu.SemaphoreType.DMA((self.n * (N_DEV - 1),)), pltpu.SemaphoreType.DMA((self.n * (N_DEV - 1),)),
                        pltpu.SemaphoreType.DMA((self.n,))]

    def _relay_copies(self, srcs, dsts, send_sems, recv_sems, local_sems):
        x, y, c = lax.axis_index("x"), lax.axis_index("y"), lax.axis_index("c")
        me = 4 * x + 2 * y + c
        chips = [(1 - x, y), (x, 1 - y), (1 - x, 1 - y)]
        first, passed, own = [], [], []
        for t, tr in enumerate(self.transfers):
            def copy(k, src, sender, to, t=t, tr=tr):
                return pltpu.make_async_remote_copy(
                    src_ref=src, dst_ref=tr.dst_at(dsts[t], sender), send_sem=send_sems.at[t * (N_DEV - 1) + k],
                    recv_sem=recv_sems.at[t * (N_DEV - 1) + k], device_id=to, device_id_type=pl.DeviceIdType.MESH)

            mine = tr.src_at(srcs[t], me)
            first.append([copy(0, mine, me, (x, y, 1 - c))] + [copy(1 + j, mine, me, (px, py, c))
                                                                for j, (px, py) in enumerate(chips)])
            passed.append([copy(4 + j, tr.dst_at(dsts[t], 4 * px + 2 * py + c), 4 * px + 2 * py + c, (x, y, 1 - c))
                           for j, (px, py) in enumerate(chips)])
            own.append(pltpu.make_async_copy(mine, tr.dst_at(dsts[t], me), local_sems.at[t]))
        return first, passed, own

    def _copies(self, srcs, dsts, send_sems, recv_sems, local_sems):
        x, y, c = lax.axis_index("x"), lax.axis_index("y"), lax.axis_index("c")
        me = 4 * x + 2 * y + c
        copies = []
        for d in range(1, N_DEV):
            px = 1 - x if d & 4 else x
            py = 1 - y if d & 2 else y
            pc = 1 - c if d & 1 else c
            for t, tr in enumerate(self.transfers):
                if tr.same_core and d & 1:
                    continue
                peer, mine = (2 * px + py, 2 * x + y) if tr.same_core else (4 * px + 2 * py + pc, me)
                k = t * (N_DEV - 1) + d - 1
                copies.append(pltpu.make_async_remote_copy(
                    src_ref=tr.src_at(srcs[t], peer), dst_ref=tr.dst_at(dsts[t], mine),
                    send_sem=send_sems.at[k], recv_sem=recv_sems.at[k],
                    device_id=(px, py, pc), device_id_type=pl.DeviceIdType.MESH))
        own = []
        for t, tr in enumerate(self.transfers):
            mine = 2 * x + y if tr.same_core else me
            own.append(pltpu.make_async_copy(tr.src_at(srcs[t], mine), tr.dst_at(dsts[t], mine), local_sems.at[t]))
        return copies, own

    def start(self, srcs, dsts, *sems):
        if self.relay:
            first, _, own = self._relay_copies(srcs, dsts, *sems)
            for cp in own + [cp for per_t in first for cp in per_t]:
                cp.start()
            return
        copies, own = self._copies(srcs, dsts, *sems)
        for cp in own + copies:
            cp.start()

    def finish(self, srcs, dsts, *sems):
        if self.relay:
            first, passed, own = self._relay_copies(srcs, dsts, *sems)
            for j in range(3):
                for t in range(self.n):
                    first[t][1 + j].wait_recv()
                    passed[t][j].start()
            for t in range(self.n):
                first[t][0].wait_recv()
                for cp in passed[t]:
                    cp.wait_recv()
            for t in range(self.n):
                for cp in first[t] + passed[t]:
                    cp.wait_send()
                own[t].wait()
            return
        copies, own = self._copies(srcs, dsts, *sems)
        for cp in copies + own:
            cp.wait()


def _pcall(body, *, name, grid, in_specs, out_specs, out_shape, scratch_shapes, sem, args, comm=None):
    in_specs, out_specs, out_shape = list(in_specs), list(out_specs), list(out_shape)
    if comm is None:
        outs = pl.pallas_call(body, name=name, grid=grid, in_specs=in_specs, out_specs=out_specs, out_shape=out_shape,
                              scratch_shapes=list(scratch_shapes), compiler_params=_cparams(sem))(*args)
        return list(outs), []
    n_in, n_out, n_scr, k = len(in_specs), len(out_specs), len(scratch_shapes), comm.n

    def carrier(*refs):
        ins, cin = refs[:n_in], refs[n_in:n_in + k]
        outs, cout = refs[n_in + k:n_in + k + n_out], refs[n_in + k + n_out:n_in + 2 * k + n_out]
        scr, sems = refs[n_in + 2 * k + n_out:n_in + 2 * k + n_out + n_scr], refs[n_in + 2 * k + n_out + n_scr:]
        ids = [pl.program_id(d) for d in range(len(grid))]
        first = functools.reduce(jnp.logical_and, [i == 0 for i in ids])
        last = functools.reduce(jnp.logical_and, [i == g - 1 for i, g in zip(ids, grid)])

        @pl.when(first)
        def _():
            comm.start(cin, cout, *sems)

        body(*ins, *outs, *scr)

        @pl.when(last)
        def _():
            comm.finish(cin, cout, *sems)

    hbm = pl.BlockSpec(memory_space=pl.ANY)
    outs = pl.pallas_call(
        carrier, name=name, grid=grid, in_specs=in_specs + [hbm] * k, out_specs=out_specs + [hbm] * k,
        out_shape=out_shape + comm.out_shape, scratch_shapes=list(scratch_shapes) + comm.scratch,
        compiler_params=_cparams(tuple("arbitrary" for _ in grid)),
    )(*args, *comm.arrays)
    return list(outs[:n_out]), list(outs[n_out:])


def _exchange(transfers, name, relay=False):
    comm = _Comm(transfers, relay)

    def body(*refs):
        k = comm.n
        comm.start(refs[:k], refs[k:2 * k], *refs[2 * k:])
        comm.finish(refs[:k], refs[k:2 * k], *refs[2 * k:])

    hbm = pl.BlockSpec(memory_space=pl.ANY)
    return pl.pallas_call(body, name=name, out_shape=comm.out_shape, in_specs=[hbm] * comm.n, out_specs=[hbm] * comm.n,
                          scratch_shapes=comm.scratch)(*comm.arrays)


def _pair_reduce(chunks, name, tr=320):
    _, r, c = chunks.shape
    tr = min(tr, r)
    assert r % tr == 0

    def swap(src_ref, dst_ref, send_sems, recv_sems):
        x, y, core = lax.axis_index("x"), lax.axis_index("y"), lax.axis_index("c")
        copies = [pltpu.make_async_remote_copy(
            src_ref=src_ref.at[2 * k + 1 - core], dst_ref=dst_ref.at[k], send_sem=send_sems.at[k],
            recv_sem=recv_sems.at[k], device_id=(x, y, 1 - core), device_id_type=pl.DeviceIdType.MESH) for k in range(4)]
        for cp in copies:
            cp.start()
        for cp in copies:
            cp.wait()

    hbm = pl.BlockSpec(memory_space=pl.ANY)
    theirs = pl.pallas_call(swap, name=name + "_swap", out_shape=jax.ShapeDtypeStruct((4, r, c), chunks.dtype),
                            in_specs=[hbm], out_specs=hbm,
                            scratch_shapes=[pltpu.SemaphoreType.DMA((4,)), pltpu.SemaphoreType.DMA((4,))])(chunks)

    def add(mine_ref, theirs_ref, out_ref):
        core = lax.axis_index("c")
        both = mine_ref[...].astype(F32)
        out_ref[...] = (jnp.where(core == 0, both[0], both[1]) + theirs_ref[...].astype(F32)).astype(out_ref.dtype)

    return pl.pallas_call(
        add, name=name + "_add", grid=(4, r // tr), out_shape=jax.ShapeDtypeStruct((4, r, c), chunks.dtype),
        in_specs=[pl.BlockSpec((None, 2, tr, c), lambda k, i: (k, 0, i, 0)), pl.BlockSpec((None, tr, c), lambda k, i: (k, i, 0))],
        out_specs=pl.BlockSpec((None, tr, c), lambda k, i: (k, i, 0)),
        compiler_params=_cparams(("parallel", "parallel")),
    )(chunks.reshape(4, 2, r, c), theirs)


def _chip_scatter_transfer(pairs):
    return _Transfer(pairs, pairs.shape, lambda ref, p: ref.at[p], lambda ref, s: ref.at[s], same_core=True)


def _matmul(a, b, mode, *, name, tm, tn, tk, epi=None, extra=(), out_dtype=F32, chunks=None, comm=None):
    pieces = list(a) if isinstance(a, (list, tuple)) else [a]
    rows_a, cols_a = pieces[0].shape[0], sum(p.shape[1] for p in pieces)
    if mode == "nn":
        (M, K), N = (rows_a, cols_a), b.shape[1]
    elif mode == "nt":
        (M, K), N = (rows_a, cols_a), b.shape[0]
    else:
        (K, M), N = (rows_a, cols_a), b.shape[1]
    tm, tn, tk = min(tm, M), min(tn, N), min(tk, K)
    assert M % tm == 0 and N % tn == 0 and K % tk == 0 and (epi != "ln" or tn == N), (name, M, N, K)
    nk = K // tk
    tile_cols, axis = (tm, 0) if mode == "tn" else (tk, 2)
    assert all(p.shape[1] % tile_cols == 0 for p in pieces)
    counts = [p.shape[1] // tile_cols for p in pieces]
    starts = [sum(counts[:q]) for q in range(len(pieces))]

    def a_spec_of(q):
        at = lambda t: jnp.clip(t - starts[q], 0, counts[q] - 1) if len(pieces) > 1 else t
        return {"nn": pl.BlockSpec((tm, tk), lambda i, j, k: (i, at(k))),
                "nt": pl.BlockSpec((tm, tk), lambda i, j, k: (i, at(k))),
                "tn": pl.BlockSpec((tk, tm), lambda i, j, k: (k, at(i)))}[mode]

    n_a = len(pieces)
    b_mode = pl.Buffered(1) if (nk == 1 and tn == N and n_a > 1) else None
    b_spec = {"nn": pl.BlockSpec((tk, tn), lambda i, j, k: (k, j), pipeline_mode=b_mode),
              "nt": pl.BlockSpec((tn, tk), lambda i, j, k: (j, k), pipeline_mode=b_mode),
              "tn": pl.BlockSpec((tk, tn), lambda i, j, k: (k, j), pipeline_mode=b_mode)}[mode]
    ca, cb = {"nn": (1, 0), "nt": (1, 1), "tn": (0, 0)}[mode]
    tile = pl.BlockSpec((tm, tn), lambda i, j, k: (i, j))
    row = pl.BlockSpec((1, tn), lambda i, j, k: (0, j))
    n_extra = {None: 0, "add": 1, "relu2": 0, "drelu2": 1, "ln": 3}[epi]
    assert len(extra) == n_extra
    extra_specs = {None: [], "add": [tile], "relu2": [], "drelu2": [tile], "ln": [tile, row, row]}[epi]
    split = 0
    if epi == "relu2":
        out_shape, out_specs = (jax.ShapeDtypeStruct((M, N), BF16),), (tile,)
    elif epi == "ln":
        out_shape = (jax.ShapeDtypeStruct((M, N), F32), jax.ShapeDtypeStruct((M, N), F32),
                     jax.ShapeDtypeStruct((M, N), BF16))
        out_specs = (tile, tile, tile)
    elif chunks == "cols":
        c = N // N_DEV
        out_shape = (jax.ShapeDtypeStruct((N_DEV, M, c), out_dtype),)
        if tn == N:
            split = c
            out_specs = (pl.BlockSpec((N_DEV, tm, c), lambda i, j, k: (0, i, 0)),)
        else:
            assert c % tn == 0
            out_specs = (pl.BlockSpec((None, tm, tn), lambda i, j, k: (j // (c // tn), i, j % (c // tn))),)
    else:
        out_shape, out_specs = (jax.ShapeDtypeStruct((M, N), out_dtype),), (tile,)
    n_out = len(out_shape)

    def body(*refs):
        a_refs, b_ref = refs[:n_a], refs[n_a]
        ex = refs[n_a + 1:n_a + 1 + n_extra]
        outs = refs[n_a + 1 + n_extra:n_a + 1 + n_extra + n_out]
        acc_ref = refs[-1]
        k = pl.program_id(2)

        def finish(acc):
            if epi == "add":
                outs[0][...] = (acc + ALPHA * ex[0][...]).astype(out_dtype)
            elif epi == "relu2":
                r = jnp.maximum(acc, 0.0)
                outs[0][...] = _bf(r * r)
            elif epi == "drelu2":
                outs[0][...] = (acc * (2.0 * jnp.sqrt(ex[0][...].astype(F32)))).astype(out_dtype)
            elif epi == "ln":
                u = ALPHA * ex[0][...] + acc
                xh, _ = _norm_stats(u)
                y = xh * ex[1][...] + ex[2][...]
                outs[0][...] = u
                outs[1][...] = y
                outs[2][...] = _bf(y)
            elif split:
                for p in range(N_DEV):
                    outs[0][p] = acc[:, p * split:(p + 1) * split].astype(out_dtype)
            else:
                outs[0][...] = acc.astype(out_dtype)

        def step(a_ref, first, middle, last):
            part = _dg(_bf(a_ref[...]), _bf(b_ref[...]), ca, cb)
            if nk == 1:
                finish(part)
                return
            if first:
                @pl.when(k == 0)
                def _():
                    acc_ref[...] = part

            if middle:
                @pl.when(jnp.logical_and(k > 0, k < nk - 1))
                def _():
                    acc_ref[...] += part

            if last:
                @pl.when(k == nk - 1)
                def _():
                    finish(acc_ref[...] + part)

        if n_a == 1:
            step(a_refs[0], True, True, True)
        else:
            t = pl.program_id(axis)
            for q in range(n_a):
                along_k = axis == 2
                first = not along_k or starts[q] == 0
                last = not along_k or starts[q] + counts[q] == nk
                middle = not along_k or counts[q] > int(first) + int(last)

                @pl.when(jnp.logical_and(t >= starts[q], t < starts[q] + counts[q]))
                def _(q=q, first=first, middle=middle, last=last):
                    step(a_refs[q], first, middle, last)

    outs, landed = _pcall(
        body, name=name, out_shape=out_shape, grid=(M // tm, N // tn, nk),
        in_specs=[a_spec_of(q) for q in range(n_a)] + [b_spec] + extra_specs, out_specs=out_specs,
        scratch_shapes=[pltpu.VMEM((tm, tn) if nk > 1 else (8, 128), F32)], sem=("parallel", "parallel", "arbitrary"),
        args=(*pieces, b, *extra), comm=comm)
    res = outs[0] if n_out == 1 else tuple(outs)
    if chunks == "rows":
        res = res.reshape(N_DEV, M // N_DEV, N)
    return res if comm is None else (res, landed)


def _matmul_rows_of(pieces, b, res, *, name, tm, comm=None):
    M, (K, N) = pieces[0].shape[0], b.shape
    tm = min(tm, M)
    subs, start = [], 0
    for q, p in enumerate(pieces):
        w = p.shape[1]
        step = w if start % w == 0 else 512
        assert w % step == 0 and start % step == 0
        subs += [(q, off, step, start + off) for off in range(0, w, step)]
        start += w
    assert start == K
    n_p, n_s = len(pieces), len(subs)

    def body(*refs):
        a_refs, b_refs, res_ref, out_ref = refs[:n_p], refs[n_p:n_p + n_s], refs[n_p + n_s], refs[n_p + n_s + 1]
        acc = None
        for (q, off, w, _), b_ref in zip(subs, b_refs):
            part = _dg(_bf(a_refs[q][:, off:off + w]), _bf(b_ref[...]), 1, 0)
            acc = part if acc is None else acc + part
        out_ref[...] = acc + ALPHA * res_ref[...]

    tile = pl.BlockSpec((tm, N), lambda i: (i, 0))
    outs, landed = _pcall(
        body, name=name, grid=(M // tm,), out_shape=[jax.ShapeDtypeStruct((M, N), F32)],
        in_specs=[pl.BlockSpec((tm, p.shape[1]), lambda i: (i, 0)) for p in pieces] +
                 [pl.BlockSpec((w, N), lambda i, r=row // w: (r, 0), pipeline_mode=pl.Buffered(1))
                  for _, _, w, row in subs] + [tile],
        out_specs=[tile], scratch_shapes=[], sem=("parallel",), args=(*pieces, *([b] * n_s), res), comm=comm)
    return outs[0] if comm is None else (outs[0], landed)


def _ret_tables(S):
    half = 64
    inv_freq = ROPE_BASE ** (-jnp.arange(half, dtype=F32) / half)
    ang = jnp.arange(S, dtype=jnp.int32).astype(F32)[:, None] * inv_freq[None, :]
    cos, sin = jnp.cos(ang), jnp.sin(ang)
    cosf = jnp.concatenate([cos, cos], axis=1)
    sinf = jnp.concatenate([-sin, sin], axis=1)
    log_g = jnp.log(1.0 - 2.0 ** (-5.0 - jnp.arange(4, dtype=F32)))
    idx = jnp.arange(CHUNK, dtype=F32)
    diff = idx[:, None] - idx[None, :]
    md = jnp.where(diff[None] >= 0, jnp.exp(log_g[:, None, None] * diff[None]), 0.0)
    kd = jnp.exp(log_g[:, None] * (CHUNK - 1 - idx)[None, :])
    qd = jnp.exp(log_g[:, None] * (idx + 1.0)[None, :])
    cd = jnp.exp(log_g * CHUNK)
    bc = lambda t: jnp.broadcast_to(t[:, :, None], (4, CHUNK, CHUNK))
    return cosf, sinf, md, bc(qd), bc(kd), jnp.broadcast_to(cd[:, None, None], (4, 8, CHUNK))


def _rot(x, cosf, sinf):
    return x * cosf + pltpu.roll(x, 64, 1) * sinf


def _rot_t(dx, cosf, sinf):
    return dx * cosf - pltpu.roll(dx, 64, 1) * sinf


def _ret_specs(rev, N):
    rn = (lambda n: N - 1 - n) if rev else (lambda n: n)
    col = lambda c: pl.BlockSpec((CHUNK, 512), lambda n, c=c: (rn(n), c))
    tab = pl.BlockSpec((CHUNK, CHUNK), lambda n: (rn(n), 0))
    dec = pl.BlockSpec((4, CHUNK, CHUNK), lambda n: (0, 0, 0))
    cdec = pl.BlockSpec((4, 8, CHUNK), lambda n: (0, 0, 0))
    vec = pl.BlockSpec((1, 512), lambda n: (0, 0))
    st = pl.BlockSpec((1, 4, CHUNK, CHUNK), lambda n: (rn(n), 0, 0, 0))
    return col, tab, dec, cdec, vec, st


def _ret_fwd(proj, tables, gn_g, gn_b):
    S = proj.shape[0]
    N = S // CHUNK
    col, tab, dec, cdec, vec, st = _ret_specs(False, N)

    def body(q_ref, k_ref, v_ref, g_ref, cos_ref, sin_ref, md_ref, qd_ref, kd_ref, cd_ref, gng_ref, gnb_ref,
             out_ref, st_ref, state):
        @pl.when(pl.program_id(0) == 0)
        def _():
            state[...] = jnp.zeros_like(state)

        cosf, sinf = cos_ref[...], sin_ref[...]
        for h in range(4):
            sl = slice(h * 128, (h + 1) * 128)
            qr = _rot(q_ref[:, sl], cosf, sinf)
            kr = _rot(k_ref[:, sl], cosf, sinf) * (128 ** -0.5)
            vb = _bf(v_ref[:, sl])
            s0 = state[h]
            st_ref[0, h] = s0
            sc = _dg(_bf(qr), _bf(kr), 1, 1) * md_ref[h]
            r = _dg(_bf(sc), vb, 1, 0) + _dg(_bf(qr * qd_ref[h]), _bf(s0), 1, 0)
            state[h] = s0 * cd_ref[h, 0:1, :] + _dg(_bf(kr * kd_ref[h]), vb, 0, 0)
            y, _ = _norm_stats(r)
            rg = g_ref[:, sl]
            out_ref[:, sl] = rg * _sigmoid(rg) * (y * gng_ref[:, sl] + gnb_ref[:, sl])

    return pl.pallas_call(
        body, name="ret_fwd", grid=(N,),
        out_shape=(jax.ShapeDtypeStruct((S, RET_W), F32), jax.ShapeDtypeStruct((N, 4, CHUNK, CHUNK), F32)),
        in_specs=[col(0), col(1), col(2), col(3), tab, tab, dec, dec, dec, cdec, vec, vec],
        out_specs=(pl.BlockSpec((CHUNK, 512), lambda n: (n, 0)), st),
        scratch_shapes=[pltpu.VMEM((4, CHUNK, CHUNK), F32)],
        compiler_params=_cparams(("arbitrary",)),
    )(proj, proj, proj, proj, *tables, gn_g, gn_b)


def _ret_bwd(proj, tables, gn_g, gn_b, states, d_out):
    S = proj.shape[0]
    N = S // CHUNK
    col, tab, dec, cdec, vec, st = _ret_specs(True, N)

    def kernel_body(q_ref, k_ref, v_ref, g_ref, cos_ref, sin_ref, md_ref, qd_ref, kd_ref, cd_ref, gng_ref, gnb_ref,
                    st_ref, do_ref, dp_ref, dg_ref, db_ref, gstate):
        @pl.when(pl.program_id(0) == 0)
        def _():
            gstate[...] = jnp.zeros_like(gstate)
            dg_ref[...] = jnp.zeros_like(dg_ref)
            db_ref[...] = jnp.zeros_like(db_ref)

        cosf, sinf = cos_ref[...], sin_ref[...]
        for h in range(4):
            sl = slice(h * 128, (h + 1) * 128)
            qr = _rot(q_ref[:, sl], cosf, sinf)
            kr = _rot(k_ref[:, sl], cosf, sinf) * (128 ** -0.5)
            qb, kb, vb = _bf(qr), _bf(kr), _bf(v_ref[:, sl])
            s0b = _bf(st_ref[0, h])
            md, qd, kd = md_ref[h], qd_ref[h], kd_ref[h]
            scb = _bf(_dg(qb, kb, 1, 1) * md)
            qdb = _bf(qr * qd)
            kdb = _bf(kr * kd)
            r = _dg(scb, vb, 1, 0) + _dg(qdb, s0b, 1, 0)
            y, rstd = _norm_stats(r)
            gng = gng_ref[:, sl]
            gn = y * gng + gnb_ref[:, sl]
            rg = g_ref[:, sl]
            sg = _sigmoid(rg)
            d_o = do_ref[:, sl]
            d_gn = d_o * (rg * sg)
            dg_ref[:, sl] += jnp.sum(d_gn * y, axis=0, keepdims=True)
            db_ref[:, sl] += jnp.sum(d_gn, axis=0, keepdims=True)
            drb = _bf(_norm_bwd(d_gn * gng, y, rstd))
            g0 = gstate[h]
            gb = _bf(g0)
            dscb = _bf(_dg(drb, vb, 1, 1) * md)
            dqr = _dg(dscb, kb, 1, 0) + _dg(drb, s0b, 1, 1) * qd
            dkr = _dg(dscb, qb, 0, 0) + _dg(vb, gb, 1, 1) * kd
            dv = _dg(scb, drb, 0, 0) + _dg(kdb, gb, 1, 0)
            gstate[h] = g0 * cd_ref[h, 0:1, :] + _dg(qdb, drb, 0, 0)
            dp_ref[:, 0 * 512 + h * 128:0 * 512 + (h + 1) * 128] = _bf(_rot_t(dqr, cosf, sinf))
            dp_ref[:, 1 * 512 + h * 128:1 * 512 + (h + 1) * 128] = _bf(_rot_t(dkr, cosf, sinf) * (128 ** -0.5))
            dp_ref[:, 2 * 512 + h * 128:2 * 512 + (h + 1) * 128] = _bf(dv)
            dp_ref[:, 3 * 512 + h * 128:3 * 512 + (h + 1) * 128] = _bf(d_o * gn * (sg * (1.0 + rg * (1.0 - sg))))

    acc = pl.BlockSpec((1, 512), lambda n: (0, 0))
    return pl.pallas_call(
        kernel_body, name="ret_bwd", grid=(N,),
        out_shape=(jax.ShapeDtypeStruct((S, 2048), BF16), jax.ShapeDtypeStruct((1, 512), F32),
                   jax.ShapeDtypeStruct((1, 512), F32)),
        in_specs=[col(0), col(1), col(2), col(3), tab, tab, dec, dec, dec, cdec, vec, vec, st,
                  pl.BlockSpec((CHUNK, 512), lambda n: (N - 1 - n, 0))],
        out_specs=(pl.BlockSpec((CHUNK, 2048), lambda n: (N - 1 - n, 0)), acc, acc),
        scratch_shapes=[pltpu.VMEM((4, CHUNK, CHUNK), F32)],
        compiler_params=_cparams(("arbitrary",)),
    )(proj, proj, proj, proj, *tables, gn_g, gn_b, states, d_out)


SB_T = 256
SB_SCALE = 64 ** -0.5
SB_Q_COL, SB_K_COL, SB_V_COL = 2048 // 128, 2560 // 128, 3072 // 128


def _head_masks():
    lane = lax.broadcasted_iota(jnp.int32, (1, 128), 1)
    m0 = (lane < 64).astype(F32)
    return m0, 1.0 - m0


def _tri(n, cmp):
    r = lax.broadcasted_iota(jnp.int32, (n, n), 0)
    c = lax.broadcasted_iota(jnp.int32, (n, n), 1)
    return cmp(r, c)


def _tri_sum(x, tri):
    hi = _bf(x)
    lo = _bf(x - hi.astype(F32))
    return _dg(hi, tri, 1, 0) + _dg(lo, tri, 1, 0)


def _sb_weights(qms, kblks, upper, carry, causal):
    tiles = [(b, h) for b in range(len(kblks)) for h in range(2)]
    zs = [_dg(qms[h], kblks[b], 1, 1) for b, h in tiles]
    lgs = [-(jnp.maximum(z, 0.0) + jnp.log(1.0 + jnp.exp(-jnp.abs(z)))) for z in zs]
    if causal is not None:
        lgs = [jnp.where(causal, lg, 0.0) for lg in lgs]
    carries = list(carry)
    for t in range(len(tiles) - 2):
        carries.append(carries[t] + jnp.sum(lgs[t], axis=1, keepdims=True))
    his = [_bf(lg) for lg in lgs]
    los = [_bf(lg - hi.astype(F32)) for lg, hi in zip(lgs, his)]
    later = [_dg(hi, upper, 1, 0) for hi in his]
    later = [r + _dg(lo, upper, 1, 0) for r, lo in zip(later, los)]
    a = [jnp.exp(lg + z + (r + c)) for lg, z, r, c in zip(lgs, zs, later, carries)]
    if causal is not None:
        a = [jnp.where(causal, x, 0.0) for x in a]
    out = tuple(carries[t] + jnp.sum(lgs[t], axis=1, keepdims=True) for t in (len(tiles) - 2, len(tiles) - 1))
    return [a[2 * b:2 * b + 2] for b in range(len(kblks))], out


def _sb_fwd(proj, comm=None):
    S = proj.shape[0]
    T = min(SB_T, S)
    nq = S // T

    def body(q_ref, k_ref, v_ref, o_ref, a_ref, kb_ref, vm_ref, acc_ref):
        i = pl.program_id(1)
        m0, m1 = _head_masks()

        @pl.when(i == 0)
        def _():
            v = v_ref[...]
            kb_ref[...] = _bf(k_ref[...])
            vm_ref[0] = _bf(v * m0)
            vm_ref[1] = _bf(v * m1)

        q = q_ref[...]
        qm = (_bf(q * (m0 * SB_SCALE)), _bf(q * (m1 * SB_SCALE)))
        upper = _tri(T, lambda r, c: r > c).astype(BF16)
        causal = _tri(T, lambda r, c: c < r)

        def tiles(js, carry, mask, first):
            ks = [pl.multiple_of(j * T, T) for j in js]
            a, out = _sb_weights(qm, [kb_ref[pl.ds(k, T), :] for k in ks], upper, carry, mask)
            a = [[_bf(t) for t in per_block] for per_block in a]
            for b, j in enumerate(js):
                for h in range(2):
                    a_ref[h, j] = a[b][h]
            parts = [_dg(a[b][h], vm_ref[h, pl.ds(k, T), :], 1, 0) for b, k in enumerate(ks) for h in range(2)]
            part = functools.reduce(lambda u, w: u + w, parts)
            if first:
                acc_ref[...] = part
            else:
                acc_ref[...] += part
            return out

        zero = jnp.zeros((T, 1), F32)
        carry = tiles([i], (zero, zero), causal, True)
        carry = lax.fori_loop(0, i % 2, lambda _, c: tiles([i - 1], c, None, False), carry)
        top = i - 1 - i % 2
        lax.fori_loop(0, i // 2, lambda jj, c: tiles([top - 2 * jj, top - 2 * jj - 1], c, None, False), carry)
        o_ref[...] = acc_ref[...]

    full = lambda c: pl.BlockSpec((S, 128), lambda p, i, c=c: (0, c + p))
    outs, landed = _pcall(
        body, name="sb_fwd", grid=(4, nq),
        out_shape=[jax.ShapeDtypeStruct((S, SB_W), F32), jax.ShapeDtypeStruct((4, 2, nq, nq, T, T), BF16)],
        in_specs=[pl.BlockSpec((T, 128), lambda p, i: (i, SB_Q_COL + p)), full(SB_K_COL), full(SB_V_COL)],
        out_specs=[pl.BlockSpec((T, 128), lambda p, i: (i, p)),
                   pl.BlockSpec((None, 2, None, nq, T, T), lambda p, i: (p, 0, i, 0, 0, 0))],
        scratch_shapes=[pltpu.VMEM((S, 128), BF16), pltpu.VMEM((2, S, 128), BF16), pltpu.VMEM((T, 128), F32)],
        sem=("arbitrary", "arbitrary"), args=(proj, proj, proj), comm=comm)
    return tuple(outs) if comm is None else (tuple(outs), landed)


def _sb_bwd(proj, a_saved, d_o, comm=None):
    S = proj.shape[0]
    T = min(SB_T, S)
    nq = S // T

    def body(q_ref, k_ref, v_ref, do_ref, a_ref, dq_ref, dk_ref, dv_ref, kb_ref, kbm_ref, vb_ref, dq_acc, dk_acc, dv_acc):
        i = pl.program_id(1)
        m0, m1 = _head_masks()

        @pl.when(i == 0)
        def _():
            k = k_ref[...]
            kb_ref[...] = _bf(k)
            kbm_ref[0] = _bf(k * m0)
            kbm_ref[1] = _bf(k * m1)
            vb_ref[...] = _bf(v_ref[...])
            dk_acc[...] = jnp.zeros_like(dk_acc)
            dv_acc[...] = jnp.zeros_like(dv_acc)

        q, d_out = q_ref[...], do_ref[...]
        qm = (_bf(q * (m0 * SB_SCALE)), _bf(q * (m1 * SB_SCALE)))
        dom = (_bf(d_out * m0), _bf(d_out * m1))
        lower = _tri(T, lambda r, c: r < c).astype(BF16)
        causal = _tri(T, lambda r, c: c < r)

        def up(js, carry, mask):
            ks = [pl.multiple_of(j * T, T) for j in js]
            tiles = [(b, h) for b in range(len(js)) for h in range(2)]
            zs = [_dg(qm[h], kb_ref[pl.ds(ks[b], T), :], 1, 1) for b, h in tiles]
            a = [a_ref[h, js[b]] for b, h in tiles]
            es = [w.astype(F32) * _dg(dom[h], vb_ref[pl.ds(ks[b], T), :], 1, 1) for w, (b, h) in zip(a, tiles)]
            carries = list(carry)
            for t in range(len(tiles)):
                carries.append(carries[t] + jnp.sum(es[t], axis=1, keepdims=True))
            his = [_bf(e) for e in es]
            los = [_bf(e - hi.astype(F32)) for e, hi in zip(es, his)]
            d_lg = [_dg(hi, lower, 1, 0) for hi in his]
            d_lg = [r + _dg(lo, lower, 1, 0) + c for r, lo, c in zip(d_lg, los, carries)]
            ens = [jnp.exp(-jnp.abs(z)) for z in zs]
            invs = [1.0 / (1.0 + en) for en in ens]
            betas = [jnp.where(z >= 0.0, inv, en * inv) for z, en, inv in zip(zs, ens, invs)]
            dzs = [e * (1.0 - b) - d * b for e, b, d in zip(es, betas, d_lg)]
            if mask is not None:
                dzs = [jnp.where(mask, dz, 0.0) for dz in dzs]
            dzs = [_bf(dz) for dz in dzs]
            parts = [_dg(dzs[t], kbm_ref[h, pl.ds(ks[b], T), :], 1, 0) for t, (b, h) in enumerate(tiles)]
            dq_acc[...] += functools.reduce(lambda u, w: u + w, parts)
            for b, k in enumerate(ks):
                dk_acc[pl.ds(k, T), :] += _dg(dzs[2 * b], qm[0], 0, 0) + _dg(dzs[2 * b + 1], qm[1], 0, 0)
                dv_acc[pl.ds(k, T), :] += _dg(a[2 * b], dom[0], 0, 0) + _dg(a[2 * b + 1], dom[1], 0, 0)
            return tuple(carries[-2:])

        zero = jnp.zeros((T, 1), F32)
        dq_acc[...] = jnp.zeros_like(dq_acc)
        carry = lax.fori_loop(0, i // 2, lambda jj, c: up([2 * jj, 2 * jj + 1], c, None), (zero, zero))
        carry = lax.fori_loop(0, i % 2, lambda _, c: up([i - 1], c, None), carry)
        up([i], carry, causal)
        dq_ref[...] = _bf(dq_acc[...] * SB_SCALE)

        @pl.when(i == nq - 1)
        def _():
            dk_ref[...] = _bf(dk_acc[...])
            dv_ref[...] = _bf(dv_acc[...])

    full = lambda c: pl.BlockSpec((S, 128), lambda p, i, c=c: (0, c + p))
    tile = pl.BlockSpec((T, 128), lambda p, i: (i, p))
    acc = pl.BlockSpec((S, 128), lambda p, i: (0, p))
    out = jax.ShapeDtypeStruct((S, SB_W), BF16)
    outs, landed = _pcall(
        body, name="sb_bwd", grid=(4, nq), out_shape=[out, out, out],
        in_specs=[pl.BlockSpec((T, 128), lambda p, i: (i, SB_Q_COL + p)), full(SB_K_COL), full(SB_V_COL), tile,
                  pl.BlockSpec((None, 2, None, nq, T, T), lambda p, i: (p, 0, i, 0, 0, 0))],
        out_specs=[tile, acc, acc],
        scratch_shapes=[pltpu.VMEM((S, 128), BF16), pltpu.VMEM((2, S, 128), BF16), pltpu.VMEM((S, 128), BF16),
                        pltpu.VMEM((T, 128), F32), pltpu.VMEM((S, 128), F32), pltpu.VMEM((S, 128), F32)],
        sem=("arbitrary", "arbitrary"), args=(proj, proj, proj, d_o, a_saved), comm=comm)
    return tuple(outs) if comm is None else (tuple(outs), landed)


SGU_U_COL, SGU_V_COL = 3584 // 512, 4096 // 512


def _causal(w):
    r = lax.broadcasted_iota(jnp.int32, (CHUNK, CHUNK), 0)
    c = lax.broadcasted_iota(jnp.int32, (CHUNK, CHUNK), 1)
    return jnp.where(r >= c, w, 0.0)


def _sgu_fwd(proj, ln_g, ln_b, w, b):
    S = proj.shape[0]
    N = S // CHUNK

    def body(u_ref, v_ref, g_ref, b_ref, w_ref, bias_ref, out_ref):
        u = _gelu(u_ref[...])
        xh, _ = _norm_stats(_gelu(v_ref[...]))
        vn = _bf(xh * g_ref[...] + b_ref[...])
        for g in range(4):
            sl = slice(g * 128, (g + 1) * 128)
            sv = _dg(_bf(_causal(w_ref[g])), vn[:, sl], 1, 0) + bias_ref[g]
            out_ref[:, sl] = u[:, sl] * sv

    vec = pl.BlockSpec((1, 512), lambda n: (0, 0))
    return pl.pallas_call(
        body, name="sgu_fwd", grid=(N,),
        out_shape=jax.ShapeDtypeStruct((S, SGU_W), F32),
        in_specs=[pl.BlockSpec((CHUNK, 512), lambda n: (n, SGU_U_COL)),
                  pl.BlockSpec((CHUNK, 512), lambda n: (n, SGU_V_COL)), vec, vec,
                  pl.BlockSpec((4, CHUNK, CHUNK), lambda n: (0, 0, 0)), pl.BlockSpec((4, CHUNK, 1), lambda n: (0, 0, 0))],
        out_specs=pl.BlockSpec((CHUNK, 512), lambda n: (n, 0)),
        compiler_params=_cparams(("parallel",)),
    )(proj, proj, ln_g, ln_b, w, b)


def _sgu_bwd(proj, ln_g, ln_b, w, b, d_out):
    S = proj.shape[0]
    N = S // CHUNK

    def body(u_ref, v_ref, g_ref, b_ref, w_ref, bias_ref, do_ref, dp_ref, dg_ref, db_ref, dw_ref, dbias_ref):
        @pl.when(pl.program_id(0) == 0)
        def _():
            dg_ref[...] = jnp.zeros_like(dg_ref)
            db_ref[...] = jnp.zeros_like(db_ref)
            dw_ref[...] = jnp.zeros_like(dw_ref)
            dbias_ref[...] = jnp.zeros_like(dbias_ref)

        gu, gv = u_ref[...], v_ref[...]
        u = _gelu(gu)
        xh, rstd = _norm_stats(_gelu(gv))
        ln_gain = g_ref[...]
        vn = _bf(xh * ln_gain + b_ref[...])
        d_o = do_ref[...]
        d_vn = []
        for g in range(4):
            sl = slice(g * 128, (g + 1) * 128)
            wc = _bf(_causal(w_ref[g]))
            sv = _dg(wc, vn[:, sl], 1, 0) + bias_ref[g]
            dp_ref[:, sl] = _bf(d_o[:, sl] * sv * _gelu_grad(gu[:, sl]))
            d_sv = d_o[:, sl] * u[:, sl]
            dbias_ref[g] += jnp.sum(d_sv, axis=1, keepdims=True)
            d_svb = _bf(d_sv)
            dw_ref[g] += _causal(_dg(d_svb, vn[:, sl], 1, 1))
            d_vn.append(_dg(wc, d_svb, 0, 0))
        d_vn = jnp.concatenate(d_vn, axis=1)
        dg_ref[...] += jnp.sum(d_vn * xh, axis=0, keepdims=True)
        db_ref[...] += jnp.sum(d_vn, axis=0, keepdims=True)
        dp_ref[:, 512:1024] = _bf(_norm_bwd(d_vn * ln_gain, xh, rstd) * _gelu_grad(gv))

    vec = pl.BlockSpec((1, 512), lambda n: (0, 0))
    wspec = pl.BlockSpec((4, CHUNK, CHUNK), lambda n: (0, 0, 0))
    bspec = pl.BlockSpec((4, CHUNK, 1), lambda n: (0, 0, 0))
    return pl.pallas_call(
        body, name="sgu_bwd", grid=(N,),
        out_shape=(jax.ShapeDtypeStruct((S, 1024), BF16), jax.ShapeDtypeStruct((1, 512), F32),
                   jax.ShapeDtypeStruct((1, 512), F32), jax.ShapeDtypeStruct((4, CHUNK, CHUNK), F32),
                   jax.ShapeDtypeStruct((4, CHUNK, 1), F32)),
        in_specs=[pl.BlockSpec((CHUNK, 512), lambda n: (n, SGU_U_COL)),
                  pl.BlockSpec((CHUNK, 512), lambda n: (n, SGU_V_COL)), vec, vec, wspec, bspec,
                  pl.BlockSpec((CHUNK, 512), lambda n: (n, 0))],
        out_specs=(pl.BlockSpec((CHUNK, 1024), lambda n: (n, 0)), vec, vec, wspec, bspec),
        compiler_params=_cparams(("arbitrary",)),
    )(proj, proj, ln_g, ln_b, w, b, d_out)


GATE_COL = 4608 // 512


def _merge_fwd(proj, branches, p_list, tm=512):
    S = proj.shape[0]
    tm = min(tm, S)

    def body(r_ref, s_ref, g_ref, pr_ref, ps_ref, pg_ref, gr_ref, gs_ref, gg_ref, m_ref, br_ref):
        acc = None
        for k, (x_ref, p_ref, gate_ref) in enumerate(((r_ref, pr_ref, gr_ref), (s_ref, ps_ref, gs_ref),
                                                      (g_ref, pg_ref, gg_ref))):
            br = _dg(_bf(x_ref[...]), _bf(p_ref[...]), 1, 0)
            br_ref[k] = br
            term = _sigmoid(gate_ref[...]) * br
            acc = term if acc is None else acc + term
        m_ref[...] = _bf(acc)

    xs = pl.BlockSpec((tm, 512), lambda i, n: (i, 0))
    ps = pl.BlockSpec((512, 512), lambda i, n: (0, n))
    gate = lambda k: pl.BlockSpec((tm, 512), lambda i, n, k=k: (i, GATE_COL + 2 * k + n))
    return pl.pallas_call(
        body, name="merge_fwd", grid=(S // tm, 2),
        out_shape=(jax.ShapeDtypeStruct((S, D_MODEL), BF16), jax.ShapeDtypeStruct((3, S, D_MODEL), F32)),
        in_specs=[xs, xs, xs, ps, ps, ps, gate(0), gate(1), gate(2)],
        out_specs=(pl.BlockSpec((tm, 512), lambda i, n: (i, n)), pl.BlockSpec((3, tm, 512), lambda i, n: (0, i, n))),
        compiler_params=_cparams(("parallel", "parallel")),
    )(*branches, *p_list, proj, proj, proj)


def _gate_bwd(proj, br, d_merged, tm=512):
    S = proj.shape[0]
    tm = min(tm, S)

    def body(dm_ref, br_ref, gr_ref, gs_ref, gg_ref, *out_refs):
        dm = dm_ref[...]
        for k, gate_ref in enumerate((gr_ref, gs_ref, gg_ref)):
            s = _sigmoid(gate_ref[...])
            out_refs[k][...] = _bf(dm * s)
            out_refs[3 + k][...] = _bf(dm * br_ref[k] * (s * (1.0 - s)))

    gate = lambda k: pl.BlockSpec((tm, 512), lambda i, n, k=k: (i, GATE_COL + 2 * k + n))
    three = pl.BlockSpec((3, tm, 512), lambda i, n: (0, i, n))
    tile = pl.BlockSpec((tm, 512), lambda i, n: (i, n))
    outs = pl.pallas_call(
        body, name="gate_bwd", grid=(S // tm, 2),
        out_shape=[jax.ShapeDtypeStruct((S, D_MODEL), BF16)] * 6,
        in_specs=[tile, three, gate(0), gate(1), gate(2)], out_specs=[tile] * 6,
        compiler_params=_cparams(("parallel", "parallel")),
    )(d_merged, br, proj, proj, proj)
    return outs[:3], outs[3:]


def _ln_bwd(dy, u, g, tm=256):
    S, D = u.shape
    tm = min(tm, S)

    def body(dy_ref, u_ref, g_ref, du_ref, dub_ref, dg_ref, db_ref):
        @pl.when(pl.program_id(0) == 0)
        def _():
            dg_ref[...] = jnp.zeros_like(dg_ref)
            db_ref[...] = jnp.zeros_like(db_ref)

        dy_t = dy_ref[...]
        xh, rstd = _norm_stats(u_ref[...])
        dg_ref[...] += jnp.sum(dy_t * xh, axis=0, keepdims=True)
        db_ref[...] += jnp.sum(dy_t, axis=0, keepdims=True)
        du = _norm_bwd(dy_t * g_ref[...], xh, rstd)
        du_ref[...] = du
        dub_ref[...] = _bf(du)

    tile = pl.BlockSpec((tm, D), lambda i: (i, 0))
    vec = pl.BlockSpec((1, D), lambda i: (0, 0))
    return pl.pallas_call(
        body, name="ln_bwd", grid=(S // tm,),
        out_shape=(jax.ShapeDtypeStruct((S, D), F32), jax.ShapeDtypeStruct((S, D), BF16),
                   jax.ShapeDtypeStruct((1, D), F32), jax.ShapeDtypeStruct((1, D), F32)),
        in_specs=[tile, tile, vec], out_specs=(tile, tile, vec, vec),
        compiler_params=_cparams(("arbitrary",)),
    )(dy, u, g)


def _loss_grad(y, target, tm=256):
    S, D = y.shape
    tm = min(tm, S)

    def body(y_ref, t_ref, dy_ref, sq_ref):
        @pl.when(pl.program_id(0) == 0)
        def _():
            sq_ref[...] = jnp.zeros_like(sq_ref)

        err = y_ref[...] - t_ref[...]
        dy_ref[...] = err * (1.0 / D)
        sq_ref[...] += jnp.sum(err * err, axis=0, keepdims=True)

    tile = pl.BlockSpec((tm, D), lambda i: (i, 0))
    vec = pl.BlockSpec((1, D), lambda i: (0, 0))
    return pl.pallas_call(
        body, name="loss_grad", grid=(S // tm,),
        out_shape=(jax.ShapeDtypeStruct((S, D), F32), jax.ShapeDtypeStruct((1, D), F32)),
        in_specs=[tile, tile], out_specs=(tile, vec),
        compiler_params=_cparams(("arbitrary",)),
    )(y, target)


def _layer_fwd(x, x_bf, W, tables, sb_comm=None):
    proj = _matmul(x_bf, W["w_in_t"], "nt", name="proj", tm=1024, tn=768, tk=1024)
    retg, states = _ret_fwd(proj, tables, W["ret_gn_g"], W["ret_gn_b"])
    if sb_comm is None:
        sb, sb_a = _sb_fwd(proj)
    else:
        (sb, sb_a), landed = _sb_fwd(proj, comm=sb_comm[0])
        sb_comm[1](landed)
    sg = _sgu_fwd(proj, W["sgu_ln_g"], W["sgu_ln_b"], W["sgu_w"], W["sgu_b"])
    merged, br = _merge_fwd(proj, (retg, sb, sg), (W["p_ret"], W["p_sb"], W["p_sgu"]))
    u1, x1, x1_bf = _matmul(merged, W["w_out"], "nn", name="out_ln", tm=512, tn=1024, tk=1024, epi="ln",
                            extra=(x, W["ln1_g"], W["ln1_b"]))
    act = _matmul(x1_bf, W["w_up"], "nn", name="up", tm=1024, tn=1024, tk=1024, epi="relu2")
    u2, x2, x2_bf = _matmul(act, W["w_down"], "nn", name="down_ln", tm=512, tn=1024, tk=4096, epi="ln",
                            extra=(x1, W["ln2_g"], W["ln2_b"]))
    saved = dict(x_bf=x_bf, proj=proj, retg=retg, states=states, sb=sb, sb_a=sb_a, sg=sg, merged=merged, br=br, u1=u1,
                 x1_bf=x1_bf, act=act, u2=u2)
    return x2, x2_bf, saved


def _layer_bwd(d_x2, W, tables, sv, chunk_dtype=None, sb_comm_fn=None, dx_comm_fn=None):
    dt = F32 if chunk_dtype is None else chunk_dtype
    rows, cols = (None, None) if chunk_dtype is None else ("rows", "cols")
    g, landed = {}, {}
    du2, du2_bf, g["ln2_g"], g["ln2_b"] = _ln_bwd(d_x2, sv["u2"], W["ln2_g"])
    d_hpre = _matmul(du2_bf, W["w_down"], "nt", name="d_act", tm=1024, tn=1024, tk=1024, epi="drelu2",
                     extra=(sv["act"],), out_dtype=BF16)
    g["w_down"] = _matmul(sv["act"], du2_bf, "tn", name="dw_down", tm=512, tn=1024, tk=4096, out_dtype=dt, chunks=rows)
    g["w_up"] = _matmul(sv["x1_bf"], d_hpre, "tn", name="dw_up", tm=1024, tn=512, tk=4096, out_dtype=dt, chunks=cols)
    d_x1 = _matmul(d_hpre, W["w_up"], "nt", name="d_x1", tm=512, tn=1024, tk=4096, epi="add", extra=(du2,))
    du1, du1_bf, g["ln1_g"], g["ln1_b"] = _ln_bwd(d_x1, sv["u1"], W["ln1_g"])
    d_merged = _matmul(du1_bf, W["w_out"], "nt", name="d_merged", tm=1024, tn=1024, tk=1024)
    g["w_out"] = _matmul(sv["merged"], du1_bf, "tn", name="dw_out", tm=1024, tn=512, tk=4096, out_dtype=dt, chunks=rows)
    d_br, d_gate = _gate_bwd(sv["proj"], sv["br"], d_merged)
    d_branch = []
    for k, (nm, act) in enumerate((("p_ret", sv["retg"]), ("p_sb", sv["sb"]), ("p_sgu", sv["sg"]))):
        d_branch.append(_matmul(d_br[k], W[nm], "nt", name="d_" + nm[2:], tm=1024, tn=512, tk=1024))
        g[nm] = _matmul(act, d_br[k], "tn", name="dw_" + nm[2:], tm=512, tn=1024, tk=2048, out_dtype=dt, chunks=cols)
    d_ret, g["ret_gn_g"], g["ret_gn_b"] = _ret_bwd(sv["proj"], tables, W["ret_gn_g"], W["ret_gn_b"], sv["states"],
                                                   d_branch[0])
    if sb_comm_fn is None:
        d_sq, d_sk, d_sv = _sb_bwd(sv["proj"], sv["sb_a"], d_branch[1])
    else:
        (d_sq, d_sk, d_sv), landed["sb"] = _sb_bwd(sv["proj"], sv["sb_a"], d_branch[1], comm=sb_comm_fn(g))
    d_sgu, g["sgu_ln_g"], g["sgu_ln_b"], g["sgu_w"], g["sgu_b"] = _sgu_bwd(
        sv["proj"], W["sgu_ln_g"], W["sgu_ln_b"], W["sgu_w"], W["sgu_b"], d_branch[2])
    d_proj = [d_ret, d_sq, d_sk, d_sv, d_sgu, d_gate[0], d_gate[1], d_gate[2]]
    g["w_in"] = _matmul(d_proj, sv["x_bf"], "tn", name="dw_in", tm=256, tn=1024, tk=4096, out_dtype=dt, chunks=rows)
    if chunk_dtype is None:
        g["w_in"] = g["w_in"].T
    d_x = _matmul_rows_of(d_proj, W["w_in_t"], du1, name="d_x", tm=512,
                          comm=None if dx_comm_fn is None else dx_comm_fn(g))
    if dx_comm_fn is not None:
        d_x, landed["dx"] = d_x
    return d_x, g, landed


BIG = ("w_in", "p_ret", "p_sb", "p_sgu", "w_out", "w_up", "w_down")
SMALL = ("ret_gn_g", "ret_gn_b", "sgu_ln_g", "sgu_ln_b", "sgu_w", "sgu_b", "ln1_g", "ln1_b", "ln2_g", "ln2_b")
GATHER_KIND = {"w_in": "rows", "p_ret": "cols", "p_sb": "cols", "p_sgu": "cols", "w_out": "rows", "w_up": "cols",
               "w_down": "rows"}


def _small_weights(small, l):
    W = {}
    for n in SMALL:
        if n == "sgu_w":
            W[n] = small[n][l]
        elif n == "sgu_b":
            W[n] = small[n][l].reshape(4, CHUNK, 1)
        else:
            W[n] = small[n][l].reshape(1, -1)
    return W


def _local_step(x, target, full, small):
    tables = _ret_tables(x.shape[0])
    Ws = [{**{n: full[n][l] for n in BIG[1:]}, "w_in_t": full["w_in"][l].T, **_small_weights(small, l)}
          for l in range(DEPTH)]
    saved = []
    h, h_bf = x, _bf(x)
    for l in range(DEPTH):
        h, h_bf, sv = _layer_fwd(h, h_bf, Ws[l], tables)
        saved.append(sv)
    d_h, sq = _loss_grad(h, target)
    grads = [None] * DEPTH
    for l in reversed(range(DEPTH)):
        d_h, grads[l], _ = _layer_bwd(d_h, Ws[l], tables, saved[l])
    return sq, d_h, grads


def _adam(w, parts, m, v, name):
    L, R, C = w.shape
    tr = next(t for t in (320, 256, 128) if R % t == 0)
    assert len(parts) == L

    def body(*refs):
        w_ref, p_refs, (m_ref, v_ref, g_ref, d_ref, nm_ref, nv_ref) = refs[0], refs[1:1 + L], refs[1 + L:]
        layer = pl.program_id(0)
        g = None
        for li, p_ref in enumerate(p_refs):
            s = p_ref[0].astype(F32)
            for j in range(1, p_ref.shape[0]):
                s = s + p_ref[j].astype(F32)
            g = s if g is None else jnp.where(layer == li, s, g)
        g_ref[...] = g
        d_ref[...], nm_ref[...], nv_ref[...] = _adam_update(w_ref[...], g, m_ref[...], v_ref[...])

    tile = pl.BlockSpec((None, tr, C), lambda l, i: (l, i, 0))
    part = lambda li: pl.BlockSpec((parts[li].shape[0], tr, C), lambda l, i, li=li: (0, jnp.where(l == li, i, 0), 0))
    out = jax.ShapeDtypeStruct((L, R, C), F32)
    return pl.pallas_call(
        body, name=name, grid=(L, R // tr), out_shape=(out, out, out, out),
        in_specs=[tile] + [part(li) for li in range(L)] + [tile, tile],
        out_specs=(tile, tile, tile, tile),
        compiler_params=_cparams(("parallel", "parallel")),
    )(w, *parts, m, v)


def _adam_update(w, g, m, v):
    m2 = ADAM_B1 * m + (1.0 - ADAM_B1) * g
    v2 = ADAM_B2 * v + (1.0 - ADAM_B2) * (g * g)
    m_hat = m2 / (1.0 - ADAM_B1 ** ADAM_STEP)
    v_hat = v2 / (1.0 - ADAM_B2 ** ADAM_STEP)
    return -ADAM_LR * (m_hat / (jnp.sqrt(v_hat) + ADAM_EPS) + ADAM_WD * w), m2, v2


def _adam_small(w, m, v, parts):
    k = len(SMALL)

    def body(*refs):
        w_refs, m_refs, v_refs, p_refs, outs = refs[:k], refs[k:2 * k], refs[2 * k:3 * k], refs[3 * k:5 * k], refs[5 * k:]
        for i in range(k):
            vector = len(w_refs[i].shape) == 2
            for l in range(DEPTH):
                p_ref = p_refs[DEPTH * i + l]
                g = p_ref[0]
                for j in range(1, N_DEV):
                    g = g + p_ref[j]
                at = (slice(l, l + 1), slice(None)) if vector else (l,)
                delta, m2, v2 = _adam_update(w_refs[i][at], g, m_refs[i][at], v_refs[i][at])
                for o_ref, val in zip(outs[4 * i:4 * i + 4], (g, delta, m2, v2)):
                    o_ref[at] = val

    vmem = pl.BlockSpec(memory_space=pltpu.VMEM)
    args = [w[n] for n in SMALL] + [m[n] for n in SMALL] + [v[n] for n in SMALL] + \
           [parts[(n, l)] for n in SMALL for l in range(DEPTH)]
    out_shape = [jax.ShapeDtypeStruct(w[n].shape, F32) for n in SMALL for _ in range(4)]
    outs = pl.pallas_call(body, name="adam_small", out_shape=out_shape, in_specs=[vmem] * len(args),
                          out_specs=[vmem] * len(out_shape), compiler_params=_cparams())(*args)
    return {n: tuple(outs[4 * i:4 * i + 4]) for i, n in enumerate(SMALL)}


WEIGHTS = ("w_in", "ret_gn_g", "ret_gn_b", "sgu_ln_g", "sgu_ln_b", "sgu_w", "sgu_b", "p_ret", "p_sb", "p_sgu", "w_out",
           "ln1_g", "ln1_b", "w_up", "w_down", "ln2_g", "ln2_b")


def kernel(x, w_in, ret_gn_g, ret_gn_b, sgu_ln_g, sgu_ln_b, sgu_w, sgu_b, p_ret, p_sb, p_sgu, w_out, ln1_g, ln1_b, w_up, w_down, ln2_g, ln2_b, loss_target, m_w_in, m_ret_gn_g, m_ret_gn_b, m_sgu_ln_g, m_sgu_ln_b, m_sgu_w, m_sgu_b, m_p_ret, m_p_sb, m_p_sgu, m_w_out, m_ln1_g, m_ln1_b, m_w_up, m_w_down, m_ln2_g, m_ln2_b, v_w_in, v_ret_gn_g, v_ret_gn_b, v_sgu_ln_g, v_sgu_ln_b, v_sgu_w, v_sgu_b, v_p_ret, v_p_sb, v_p_sgu, v_w_out, v_ln1_g, v_ln1_b, v_w_up, v_w_down, v_ln2_g, v_ln2_b):
    w = dict(zip(WEIGHTS, (w_in, ret_gn_g, ret_gn_b, sgu_ln_g, sgu_ln_b, sgu_w, sgu_b, p_ret, p_sb, p_sgu, w_out,
                           ln1_g, ln1_b, w_up, w_down, ln2_g, ln2_b)))
    m = dict(zip(WEIGHTS, (m_w_in, m_ret_gn_g, m_ret_gn_b, m_sgu_ln_g, m_sgu_ln_b, m_sgu_w, m_sgu_b, m_p_ret, m_p_sb,
                           m_p_sgu, m_w_out, m_ln1_g, m_ln1_b, m_w_up, m_w_down, m_ln2_g, m_ln2_b)))
    v = dict(zip(WEIGHTS, (v_w_in, v_ret_gn_g, v_ret_gn_b, v_sgu_ln_g, v_sgu_ln_b, v_sgu_w, v_sgu_b, v_p_ret, v_p_sb,
                           v_p_sgu, v_w_out, v_ln1_g, v_ln1_b, v_w_up, v_w_down, v_ln2_g, v_ln2_b)))

    small = {n: w[n] for n in SMALL}
    shard = {n: _bf(w[n]) for n in BIG}
    shard["w_in"] = shard["w_in"].transpose(0, 2, 1)
    S = x.shape[1]
    x0, target = x.reshape(S, D_MODEL), loss_target.reshape(S, D_MODEL)
    tables = _ret_tables(S)
    Ws = [_small_weights(small, l) for l in range(DEPTH)]

    (Ws[0]["w_in_t"],) = _exchange([_gather_transfer(shard["w_in"], 0, "rows")], "gather_w_in0", relay=True)
    later = [(n, 0) for n in BIG[1:]] + [(n, 1) for n in BIG]

    def weights_landed(landed):
        for (n, l), z in zip(later, landed):
            Ws[l]["w_in_t" if n == "w_in" else n] = z

    gather = _Comm([_gather_transfer(shard[n], l, GATHER_KIND[n]) for n, l in later], relay=True)
    h, h_bf, saved0 = _layer_fwd(x0, _bf(x0), Ws[0], tables, sb_comm=(gather, weights_landed))
    h, _, saved1 = _layer_fwd(h, h_bf, Ws[1], tables)
    d_h, sq = _loss_grad(h, target)
    loss = lax.psum(0.5 * jnp.sum(sq) / D_MODEL, ("x", "y", "c"))

    d_h, g1, landed1 = _layer_bwd(d_h, Ws[1], tables, saved1, chunk_dtype=BF16,
                                  sb_comm_fn=lambda g: _Comm([_scatter_transfer(g[n]) for n in BIG[1:]]))
    early = [("w_in", 1)] + [(n, 0) for n in BIG[1:]]

    def small_slabs(g):
        return [_slab_transfer(g[n].reshape(4, CHUNK) if n == "sgu_b" else g[n]) for n in SMALL]

    def early_scatter(g0):
        return _Comm([_scatter_transfer((g1 if l else g0)[n]) for n, l in early] + small_slabs(g1))

    def late_scatter(g0):
        pairs = _pair_reduce(g0["w_in"], "w_in0_pairs")
        return _Comm([_chip_scatter_transfer(pairs)] + small_slabs(g0))

    d_x, g0, landed = _layer_bwd(d_h, Ws[0], tables, saved0, chunk_dtype=BF16, sb_comm_fn=early_scatter,
                                 dx_comm_fn=late_scatter)
    parts = {**dict(zip(early, landed["sb"])), **{(n, 1): z for n, z in zip(BIG[1:], landed1["sb"])}}
    parts[("w_in", 0)] = landed["dx"][0]
    small_parts = {**{(n, 1): z for n, z in zip(SMALL, landed["sb"][len(early):])},
                   **{(n, 0): z for n, z in zip(SMALL, landed["dx"][1:])}}

    grad, delta, new_m, new_v = {}, {}, {}, {}
    for n in BIG:
        view = (lambda a: a.transpose(0, 2, 1)) if n == "w_in" else (lambda a: a)
        res = _adam(view(w[n]), [parts[(n, l)] for l in range(DEPTH)], view(m[n]), view(v[n]), "adam_" + n)
        grad[n], delta[n], new_m[n], new_v[n] = (view(r) for r in res)
    for n, res in _adam_small(small, m, v, small_parts).items():
        grad[n], delta[n], new_m[n], new_v[n] = res

    return (loss, d_x.reshape(x.shape), *[grad[n] for n in WEIGHTS], *[delta[n] for n in WEIGHTS],
            *[new_m[n] for n in WEIGHTS], *[new_v[n] for n in WEIGHTS])
```

```python
import functools
import math

import numpy as np
import jax
import jax.numpy as jnp
from jax import lax
from jax.experimental import pallas as pl
from jax.experimental.pallas import tpu as pltpu

F32 = jnp.float32
BF16 = jnp.bfloat16

N_DEV = 8
DEPTH = 2
D_MODEL = 1024
CHUNK = 128
RET_W = 512
SB_W = 512
SGU_W = 512
N_IN = 7680
LN_EPS = 1e-5
ALPHA = (2 * DEPTH) ** 0.25
ROPE_BASE = 10000.0
ADAM_LR, ADAM_B1, ADAM_B2, ADAM_EPS, ADAM_WD, ADAM_STEP = 0.001, 0.9, 0.999, 1e-08, 0.01, 10
VMEM_LIMIT = 56 * 1024 * 1024

_GELU_K = math.sqrt(2.0 / math.pi)
_GELU_C = 0.044715


def _cparams(sem=None):
    return pltpu.CompilerParams(dimension_semantics=sem, vmem_limit_bytes=VMEM_LIMIT)


def _dg(a, b, ca, cb):
    return lax.dot_general(a, b, (((ca,), (cb,)), ((), ())), preferred_element_type=F32)


def _bf(x):
    return x.astype(BF16)


def _sigmoid(x):
    return 1.0 / (1.0 + jnp.exp(-x))


def _gelu(x):
    t = jnp.tanh(_GELU_K * (x + _GELU_C * (x * x * x)))
    return x * (0.5 * (1.0 + t))


def _gelu_grad(x):
    t = jnp.tanh(_GELU_K * (x + _GELU_C * (x * x * x)))
    return 0.5 * (1.0 + t) + 0.5 * x * (1.0 - t * t) * (_GELU_K * (1.0 + 3.0 * _GELU_C * x * x))


def _norm_stats(u):
    mu = jnp.mean(u, axis=-1, keepdims=True)
    d = u - mu
    var = jnp.mean(d * d, axis=-1, keepdims=True)
    rstd = lax.rsqrt(var + LN_EPS)
    return d * rstd, rstd


def _norm_bwd(dxh, xh, rstd):
    return rstd * (dxh - jnp.mean(dxh, axis=-1, keepdims=True) - xh * jnp.mean(dxh * xh, axis=-1, keepdims=True))


class _Transfer:
    def __init__(self, src, dst_shape, src_at, dst_at, same_core=False):
        self.src, self.dst_shape, self.src_at, self.dst_at = src, tuple(dst_shape), src_at, dst_at
        self.same_core = same_core


def _gather_transfer(shard, l, kind):
    _, r, c = shard.shape
    src_at = lambda ref, p: ref.at[l]
    if kind == "slab":
        return _Transfer(shard, (N_DEV, r, c), src_at, lambda ref, s: ref.at[s])
    if kind == "rows":
        return _Transfer(shard, (N_DEV * r, c), src_at, lambda ref, s: ref.at[pl.ds(pl.multiple_of(s * r, r), r), :])
    return _Transfer(shard, (r, N_DEV * c), src_at, lambda ref, s: ref.at[:, pl.ds(pl.multiple_of(s * c, c), c)])


def _scatter_transfer(chunks):
    return _Transfer(chunks, chunks.shape, lambda ref, p: ref.at[p], lambda ref, s: ref.at[s])


def _slab_transfer(arr):
    return _Transfer(arr, (N_DEV,) + arr.shape, lambda ref, p: ref, lambda ref, s: ref.at[s])


class _Comm:
    def __init__(self, transfers, relay=False):
        self.transfers = list(transfers)
        self.relay = relay
        self.n = len(self.transfers)
        self.arrays = [t.src for t in self.transfers]
        self.out_shape = [jax.ShapeDtypeStruct(t.dst_shape, t.src.dtype) for t in self.transfers]
        self.scratch = [pltpu.SemaphoreType.DMA((self.n * (N_DEV - 1),)), pltpu.SemaphoreType.DMA((self.n * (N_DEV - 1),)),
                        pltpu.SemaphoreType.DMA((self.n,))]

    def _relay_copies(self, srcs, dsts, send_sems, recv_sems, local_sems):
        x, y, c = lax.axis_index("x"), lax.axis_index("y"), lax.axis_index("c")
        me = 4 * x + 2 * y + c
        chips = [(1 - x, y), (x, 1 - y), (1 - x, 1 - y)]
        first, passed, own = [], [], []
        for t, tr in enumerate(self.transfers):
            def copy(k, src, sender, to, t=t, tr=tr):
                return pltpu.make_async_remote_copy(
                    src_ref=src, dst_ref=tr.dst_at(dsts[t], sender), send_sem=send_sems.at[t * (N_DEV - 1) + k],
                    recv_sem=recv_sems.at[t * (N_DEV - 1) + k], device_id=to, device_id_type=pl.DeviceIdType.MESH)

            mine = tr.src_at(srcs[t], me)
            first.append([copy(0, mine, me, (x, y, 1 - c))] + [copy(1 + j, mine, me, (px, py, c))
                                                                for j, (px, py) in enumerate(chips)])
            passed.append([copy(4 + j, tr.dst_at(dsts[t], 4 * px + 2 * py + c), 4 * px + 2 * py + c, (x, y, 1 - c))
                           for j, (px, py) in enumerate(chips)])
            own.append(pltpu.make_async_copy(mine, tr.dst_at(dsts[t], me), local_sems.at[t]))
        return first, passed, own

    def _copies(self, srcs, dsts, send_sems, recv_sems, local_sems):
        x, y, c = lax.axis_index("x"), lax.axis_index("y"), lax.axis_index("c")
        me = 4 * x + 2 * y + c
        copies = []
        for d in range(1, N_DEV):
            px = 1 - x if d & 4 else x
            py = 1 - y if d & 2 else y
            pc = 1 - c if d & 1 else c
            for t, tr in enumerate(self.transfers):
                if tr.same_core and d & 1:
                    continue
                peer, mine = (2 * px + py, 2 * x + y) if tr.same_core else (4 * px + 2 * py + pc, me)
                k = t * (N_DEV - 1) + d - 1
                copies.append(pltpu.make_async_remote_copy(
                    src_ref=tr.src_at(srcs[t], peer), dst_ref=tr.dst_at(dsts[t], mine),
                    send_sem=send_sems.at[k], recv_sem=recv_sems.at[k],
                    device_id=(px, py, pc), device_id_type=pl.DeviceIdType.MESH))
        own = []
        for t, tr in enumerate(self.transfers):
            mine = 2 * x + y if tr.same_core else me
            own.append(pltpu.make_async_copy(tr.src_at(srcs[t], mine), tr.dst_at(dsts[t], mine), local_sems.at[t]))
        return copies, own

    def start(self, srcs, dsts, *sems):
        if self.relay:
            first, _, own = self._relay_copies(srcs, dsts, *sems)
            for cp in own + [cp for per_t in first for cp in per_t]:
                cp.start()
            return
        copies, own = self._copies(srcs, dsts, *sems)
        for cp in own + copies:
            cp.start()

    def finish(self, srcs, dsts, *sems):
        if self.relay:
            first, passed, own = self._relay_copies(srcs, dsts, *sems)
            for j in range(3):
                for t in range(self.n):
                    first[t][1 + j].wait_recv()
                    passed[t][j].start()
            for t in range(self.n):
                first[t][0].wait_recv()
                for cp in passed[t]:
                    cp.wait_recv()
            for t in range(self.n):
                for cp in first[t] + passed[t]:
                    cp.wait_send()
                own[t].wait()
            return
        copies, own = self._copies(srcs, dsts, *sems)
        for cp in copies + own:
            cp.wait()


def _pcall(body, *, name, grid, in_specs, out_specs, out_shape, scratch_shapes, sem, args, comm=None):
    in_specs, out_specs, out_shape = list(in_specs), list(out_specs), list(out_shape)
    if comm is None:
        outs = pl.pallas_call(body, name=name, grid=grid, in_specs=in_specs, out_specs=out_specs, out_shape=out_shape,
                              scratch_shapes=list(scratch_shapes), compiler_params=_cparams(sem))(*args)
        return list(outs), []
    n_in, n_out, n_scr, k = len(in_specs), len(out_specs), len(scratch_shapes), comm.n

    def carrier(*refs):
        ins, cin = refs[:n_in], refs[n_in:n_in + k]
        outs, cout = refs[n_in + k:n_in + k + n_out], refs[n_in + k + n_out:n_in + 2 * k + n_out]
        scr, sems = refs[n_in + 2 * k + n_out:n_in + 2 * k + n_out + n_scr], refs[n_in + 2 * k + n_out + n_scr:]
        ids = [pl.program_id(d) for d in range(len(grid))]
        first = functools.reduce(jnp.logical_and, [i == 0 for i in ids])
        last = functools.reduce(jnp.logical_and, [i == g - 1 for i, g in zip(ids, grid)])

        @pl.when(first)
        def _():
            comm.start(cin, cout, *sems)

        body(*ins, *outs, *scr)

        @pl.when(last)
        def _():
            comm.finish(cin, cout, *sems)

    hbm = pl.BlockSpec(memory_space=pl.ANY)
    outs = pl.pallas_call(
        carrier, name=name, grid=grid, in_specs=in_specs + [hbm] * k, out_specs=out_specs + [hbm] * k,
        out_shape=out_shape + comm.out_shape, scratch_shapes=list(scratch_shapes) + comm.scratch,
        compiler_params=_cparams(tuple("arbitrary" for _ in grid)),
    )(*args, *comm.arrays)
    return list(outs[:n_out]), list(outs[n_out:])


def _exchange(transfers, name, relay=False):
    comm = _Comm(transfers, relay)

    def body(*refs):
        k = comm.n
        comm.start(refs[:k], refs[k:2 * k], *refs[2 * k:])
        comm.finish(refs[:k], refs[k:2 * k], *refs[2 * k:])

    hbm = pl.BlockSpec(memory_space=pl.ANY)
    return pl.pallas_call(body, name=name, out_shape=comm.out_shape, in_specs=[hbm] * comm.n, out_specs=[hbm] * comm.n,
                          scratch_shapes=comm.scratch)(*comm.arrays)


def _pair_reduce(chunks, name, tr=320):
    _, r, c = chunks.shape
    tr = min(tr, r)
    assert r % tr == 0

    def swap(src_ref, dst_ref, send_sems, recv_sems):
        x, y, core = lax.axis_index("x"), lax.axis_index("y"), lax.axis_index("c")
        copies = [pltpu.make_async_remote_copy(
            src_ref=src_ref.at[2 * k + 1 - core], dst_ref=dst_ref.at[k], send_sem=send_sems.at[k],
            recv_sem=recv_sems.at[k], device_id=(x, y, 1 - core), device_id_type=pl.DeviceIdType.MESH) for k in range(4)]
        for cp in copies:
            cp.start()
        for cp in copies:
            cp.wait()

    hbm = pl.BlockSpec(memory_space=pl.ANY)
    theirs = pl.pallas_call(swap, name=name + "_swap", out_shape=jax.ShapeDtypeStruct((4, r, c), chunks.dtype),
                            in_specs=[hbm], out_specs=hbm,
                            scratch_shapes=[pltpu.SemaphoreType.DMA((4,)), pltpu.SemaphoreType.DMA((4,))])(chunks)

    def add(mine_ref, theirs_ref, out_ref):
        core = lax.axis_index("c")
        both = mine_ref[...].astype(F32)
        out_ref[...] = (jnp.where(core == 0, both[0], both[1]) + theirs_ref[...].astype(F32)).astype(out_ref.dtype)

    return pl.pallas_call(
        add, name=name + "_add", grid=(4, r // tr), out_shape=jax.ShapeDtypeStruct((4, r, c), chunks.dtype),
        in_specs=[pl.BlockSpec((None, 2, tr, c), lambda k, i: (k, 0, i, 0)), pl.BlockSpec((None, tr, c), lambda k, i: (k, i, 0))],
        out_specs=pl.BlockSpec((None, tr, c), lambda k, i: (k, i, 0)),
        compiler_params=_cparams(("parallel", "parallel")),
    )(chunks.reshape(4, 2, r, c), theirs)


def _chip_scatter_transfer(pairs):
    return _Transfer(pairs, pairs.shape, lambda ref, p: ref.at[p], lambda ref, s: ref.at[s], same_core=True)


def _matmul(a, b, mode, *, name, tm, tn, tk, epi=None, extra=(), out_dtype=F32, chunks=None, comm=None):
    pieces = list(a) if isinstance(a, (list, tuple)) else [a]
    rows_a, cols_a = pieces[0].shape[0], sum(p.shape[1] for p in pieces)
    if mode == "nn":
        (M, K), N = (rows_a, cols_a), b.shape[1]
    elif mode == "nt":
        (M, K), N = (rows_a, cols_a), b.shape[0]
    else:
        (K, M), N = (rows_a, cols_a), b.shape[1]
    tm, tn, tk = min(tm, M), min(tn, N), min(tk, K)
    assert M % tm == 0 and N % tn == 0 and K % tk == 0 and (epi != "ln" or tn == N), (name, M, N, K)
    nk = K // tk
    tile_cols, axis = (tm, 0) if mode == "tn" else (tk, 2)
    assert all(p.shape[1] % tile_cols == 0 for p in pieces)
    counts = [p.shape[1] // tile_cols for p in pieces]
    starts = [sum(counts[:q]) for q in range(len(pieces))]

    def a_spec_of(q):
        at = lambda t: jnp.clip(t - starts[q], 0, counts[q] - 1) if len(pieces) > 1 else t
        return {"nn": pl.BlockSpec((tm, tk), lambda i, j, k: (i, at(k))),
                "nt": pl.BlockSpec((tm, tk), lambda i, j, k: (i, at(k))),
                "tn": pl.BlockSpec((tk, tm), lambda i, j, k: (k, at(i)))}[mode]

    n_a = len(pieces)
    b_mode = pl.Buffered(1) if (nk == 1 and tn == N and n_a > 1) else None
    b_spec = {"nn": pl.BlockSpec((tk, tn), lambda i, j, k: (k, j), pipeline_mode=b_mode),
              "nt": pl.BlockSpec((tn, tk), lambda i, j, k: (j, k), pipeline_mode=b_mode),
              "tn": pl.BlockSpec((tk, tn), lambda i, j, k: (k, j), pipeline_mode=b_mode)}[mode]
    ca, cb = {"nn": (1, 0), "nt": (1, 1), "tn": (0, 0)}[mode]
    tile = pl.BlockSpec((tm, tn), lambda i, j, k: (i, j))
    row = pl.BlockSpec((1, tn), lambda i, j, k: (0, j))
    n_extra = {None: 0, "add": 1, "relu2": 0, "drelu2": 1, "ln": 3}[epi]
    assert len(extra) == n_extra
    extra_specs = {None: [], "add": [tile], "relu2": [], "drelu2": [tile], "ln": [tile, row, row]}[epi]
    split = 0
    if epi == "relu2":
        out_shape, out_specs = (jax.ShapeDtypeStruct((M, N), BF16),), (tile,)
    elif epi == "ln":
        out_shape = (jax.ShapeDtypeStruct((M, N), F32), jax.ShapeDtypeStruct((M, N), F32),
                     jax.ShapeDtypeStruct((M, N), BF16))
        out_specs = (tile, tile, tile)
    elif chunks == "cols":
        c = N // N_DEV
        out_shape = (jax.ShapeDtypeStruct((N_DEV, M, c), out_dtype),)
        if tn == N:
            split = c
            out_specs = (pl.BlockSpec((N_DEV, tm, c), lambda i, j, k: (0, i, 0)),)
        else:
            assert c % tn == 0
            out_specs = (pl.BlockSpec((None, tm, tn), lambda i, j, k: (j // (c // tn), i, j % (c // tn))),)
    else:
        out_shape, out_specs = (jax.ShapeDtypeStruct((M, N), out_dtype),), (tile,)
    n_out = len(out_shape)

    def body(*refs):
        a_refs, b_ref = refs[:n_a], refs[n_a]
        ex = refs[n_a + 1:n_a + 1 + n_extra]
        outs = refs[n_a + 1 + n_extra:n_a + 1 + n_extra + n_out]
        acc_ref = refs[-1]
        k = pl.program_id(2)

        def finish(acc):
            if epi == "add":
                outs[0][...] = (acc + ALPHA * ex[0][...]).astype(out_dtype)
            elif epi == "relu2":
                r = jnp.maximum(acc, 0.0)
                outs[0][...] = _bf(r * r)
            elif epi == "drelu2":
                outs[0][...] = (acc * (2.0 * jnp.sqrt(ex[0][...].astype(F32)))).astype(out_dtype)
            elif epi == "ln":
                u = ALPHA * ex[0][...] + acc
                xh, _ = _norm_stats(u)
                y = xh * ex[1][...] + ex[2][...]
                outs[0][...] = u
                outs[1][...] = y
                outs[2][...] = _bf(y)
            elif split:
                for p in range(N_DEV):
                    outs[0][p] = acc[:, p * split:(p + 1) * split].astype(out_dtype)
            else:
                outs[0][...] = acc.astype(out_dtype)

        def step(a_ref, first, middle, last):
            part = _dg(_bf(a_ref[...]), _bf(b_ref[...]), ca, cb)
            if nk == 1:
                finish(part)
                return
            if first:
                @pl.when(k == 0)
                def _():
                    acc_ref[...] = part

            if middle:
                @pl.when(jnp.logical_and(k > 0, k < nk - 1))
                def _():
                    acc_ref[...] += part

            if last:
                @pl.when(k == nk - 1)
                def _():
                    finish(acc_ref[...] + part)

        if n_a == 1:
            step(a_refs[0], True, True, True)
        else:
            t = pl.program_id(axis)
            for q in range(n_a):
                along_k = axis == 2
                first = not along_k or starts[q] == 0
                last = not along_k or starts[q] + counts[q] == nk
                middle = not along_k or counts[q] > int(first) + int(last)

                @pl.when(jnp.logical_and(t >= starts[q], t < starts[q] + counts[q]))
                def _(q=q, first=first, middle=middle, last=last):
                    step(a_refs[q], first, middle, last)

    outs, landed = _pcall(
        body, name=name, out_shape=out_shape, grid=(M // tm, N // tn, nk),
        in_specs=[a_spec_of(q) for q in range(n_a)] + [b_spec] + extra_specs, out_specs=out_specs,
        scratch_shapes=[pltpu.VMEM((tm, tn) if nk > 1 else (8, 128), F32)], sem=("parallel", "parallel", "arbitrary"),
        args=(*pieces, b, *extra), comm=comm)
    res = outs[0] if n_out == 1 else tuple(outs)
    if chunks == "rows":
        res = res.reshape(N_DEV, M // N_DEV, N)
    return res if comm is None else (res, landed)


def _matmul_rows_of(pieces, b, res, *, name, tm, comm=None):
    M, (K, N) = pieces[0].shape[0], b.shape
    tm = min(tm, M)
    subs, start = [], 0
    for q, p in enumerate(pieces):
        w = p.shape[1]
        step = w if start % w == 0 else 512
        assert w % step == 0 and start % step == 0
        subs += [(q, off, step, start + off) for off in range(0, w, step)]
        start += w
    assert start == K
    n_p, n_s = len(pieces), len(subs)

    def body(*refs):
        a_refs, b_refs, res_ref, out_ref = refs[:n_p], refs[n_p:n_p + n_s], refs[n_p + n_s], refs[n_p + n_s + 1]
        acc = None
        for (q, off, w, _), b_ref in zip(subs, b_refs):
            part = _dg(_bf(a_refs[q][:, off:off + w]), _bf(b_ref[...]), 1, 0)
            acc = part if acc is None else acc + part
        out_ref[...] = acc + ALPHA * res_ref[...]

    tile = pl.BlockSpec((tm, N), lambda i: (i, 0))
    outs, landed = _pcall(
        body, name=name, grid=(M // tm,), out_shape=[jax.ShapeDtypeStruct((M, N), F32)],
        in_specs=[pl.BlockSpec((tm, p.shape[1]), lambda i: (i, 0)) for p in pieces] +
                 [pl.BlockSpec((w, N), lambda i, r=row // w: (r, 0), pipeline_mode=pl.Buffered(1))
                  for _, _, w, row in subs] + [tile],
        out_specs=[tile], scratch_shapes=[], sem=("parallel",), args=(*pieces, *([b] * n_s), res), comm=comm)
    return outs[0] if comm is None else (outs[0], landed)


def _ret_tables(S):
    half = 64
    inv_freq = ROPE_BASE ** (-jnp.arange(half, dtype=F32) / half)
    ang = jnp.arange(S, dtype=jnp.int32).astype(F32)[:, None] * inv_freq[None, :]
    cos, sin = jnp.cos(ang), jnp.sin(ang)
    cosf = jnp.concatenate([cos, cos], axis=1)
    sinf = jnp.concatenate([-sin, sin], axis=1)
    log_g = jnp.log(1.0 - 2.0 ** (-5.0 - jnp.arange(4, dtype=F32)))
    idx = jnp.arange(CHUNK, dtype=F32)
    diff = idx[:, None] - idx[None, :]
    md = jnp.where(diff[None] >= 0, jnp.exp(log_g[:, None, None] * diff[None]), 0.0)
    kd = jnp.exp(log_g[:, None] * (CHUNK - 1 - idx)[None, :])
    qd = jnp.exp(log_g[:, None] * (idx + 1.0)[None, :])
    cd = jnp.exp(log_g * CHUNK)
    bc = lambda t: jnp.broadcast_to(t[:, :, None], (4, CHUNK, CHUNK))
    return cosf, sinf, md, bc(qd), bc(kd), jnp.broadcast_to(cd[:, None, None], (4, 8, CHUNK))


def _rot(x, cosf, sinf):
    return x * cosf + pltpu.roll(x, 64, 1) * sinf


def _rot_t(dx, cosf, sinf):
    return dx * cosf - pltpu.roll(dx, 64, 1) * sinf


def _ret_specs(rev, N):
    rn = (lambda n: N - 1 - n) if rev else (lambda n: n)
    col = lambda c: pl.BlockSpec((CHUNK, 512), lambda n, c=c: (rn(n), c))
    tab = pl.BlockSpec((CHUNK, CHUNK), lambda n: (rn(n), 0))
    dec = pl.BlockSpec((4, CHUNK, CHUNK), lambda n: (0, 0, 0))
    cdec = pl.BlockSpec((4, 8, CHUNK), lambda n: (0, 0, 0))
    vec = pl.BlockSpec((1, 512), lambda n: (0, 0))
    st = pl.BlockSpec((1, 4, CHUNK, CHUNK), lambda n: (rn(n), 0, 0, 0))
    return col, tab, dec, cdec, vec, st


def _ret_fwd(proj, tables, gn_g, gn_b):
    S = proj.shape[0]
    N = S // CHUNK
    col, tab, dec, cdec, vec, st = _ret_specs(False, N)

    def body(q_ref, k_ref, v_ref, g_ref, cos_ref, sin_ref, md_ref, qd_ref, kd_ref, cd_ref, gng_ref, gnb_ref,
             out_ref, st_ref, state):
        @pl.when(pl.program_id(0) == 0)
        def _():
            state[...] = jnp.zeros_like(state)

        cosf, sinf = cos_ref[...], sin_ref[...]
        for h in range(4):
            sl = slice(h * 128, (h + 1) * 128)
            qr = _rot(q_ref[:, sl], cosf, sinf)
            kr = _rot(k_ref[:, sl], cosf, sinf) * (128 ** -0.5)
            vb = _bf(v_ref[:, sl])
            s0 = state[h]
            st_ref[0, h] = s0
            sc = _dg(_bf(qr), _bf(kr), 1, 1) * md_ref[h]
            r = _dg(_bf(sc), vb, 1, 0) + _dg(_bf(qr * qd_ref[h]), _bf(s0), 1, 0)
            state[h] = s0 * cd_ref[h, 0:1, :] + _dg(_bf(kr * kd_ref[h]), vb, 0, 0)
            y, _ = _norm_stats(r)
            rg = g_ref[:, sl]
            out_ref[:, sl] = rg * _sigmoid(rg) * (y * gng_ref[:, sl] + gnb_ref[:, sl])

    return pl.pallas_call(
        body, name="ret_fwd", grid=(N,),
        out_shape=(jax.ShapeDtypeStruct((S, RET_W), F32), jax.ShapeDtypeStruct((N, 4, CHUNK, CHUNK), F32)),
        in_specs=[col(0), col(1), col(2), col(3), tab, tab, dec, dec, dec, cdec, vec, vec],
        out_specs=(pl.BlockSpec((CHUNK, 512), lambda n: (n, 0)), st),
        scratch_shapes=[pltpu.VMEM((4, CHUNK, CHUNK), F32)],
        compiler_params=_cparams(("arbitrary",)),
    )(proj, proj, proj, proj, *tables, gn_g, gn_b)


def _ret_bwd(proj, tables, gn_g, gn_b, states, d_out):
    S = proj.shape[0]
    N = S // CHUNK
    col, tab, dec, cdec, vec, st = _ret_specs(True, N)

    def kernel_body(q_ref, k_ref, v_ref, g_ref, cos_ref, sin_ref, md_ref, qd_ref, kd_ref, cd_ref, gng_ref, gnb_ref,
                    st_ref, do_ref, dp_ref, dg_ref, db_ref, gstate):
        @pl.when(pl.program_id(0) == 0)
        def _():
            gstate[...] = jnp.zeros_like(gstate)
            dg_ref[...] = jnp.zeros_like(dg_ref)
            db_ref[...] = jnp.zeros_like(db_ref)

        cosf, sinf = cos_ref[...], sin_ref[...]
        for h in range(4):
            sl = slice(h * 128, (h + 1) * 128)
            qr = _rot(q_ref[:, sl], cosf, sinf)
            kr = _rot(k_ref[:, sl], cosf, sinf) * (128 ** -0.5)
            qb, kb, vb = _bf(qr), _bf(kr), _bf(v_ref[:, sl])
            s0b = _bf(st_ref[0, h])
            md, qd, kd = md_ref[h], qd_ref[h], kd_ref[h]
            scb = _bf(_dg(qb, kb, 1, 1) * md)
            qdb = _bf(qr * qd)
            kdb = _bf(kr * kd)
            r = _dg(scb, vb, 1, 0) + _dg(qdb, s0b, 1, 0)
            y, rstd = _norm_stats(r)
            gng = gng_ref[:, sl]
            gn = y * gng + gnb_ref[:, sl]
            rg = g_ref[:, sl]
            sg = _sigmoid(rg)
            d_o = do_ref[:, sl]
            d_gn = d_o * (rg * sg)
            dg_ref[:, sl] += jnp.sum(d_gn * y, axis=0, keepdims=True)
            db_ref[:, sl] += jnp.sum(d_gn, axis=0, keepdims=True)
            drb = _bf(_norm_bwd(d_gn * gng, y, rstd))
            g0 = gstate[h]
            gb = _bf(g0)
            dscb = _bf(_dg(drb, vb, 1, 1) * md)
            dqr = _dg(dscb, kb, 1, 0) + _dg(drb, s0b, 1, 1) * qd
            dkr = _dg(dscb, qb, 0, 0) + _dg(vb, gb, 1, 1) * kd
            dv = _dg(scb, drb, 0, 0) + _dg(kdb, gb, 1, 0)
            gstate[h] = g0 * cd_ref[h, 0:1, :] + _dg(qdb, drb, 0, 0)
            dp_ref[:, 0 * 512 + h * 128:0 * 512 + (h + 1) * 128] = _bf(_rot_t(dqr, cosf, sinf))
            dp_ref[:, 1 * 512 + h * 128:1 * 512 + (h + 1) * 128] = _bf(_rot_t(dkr, cosf, sinf) * (128 ** -0.5))
            dp_ref[:, 2 * 512 + h * 128:2 * 512 + (h + 1) * 128] = _bf(dv)
            dp_ref[:, 3 * 512 + h * 128:3 * 512 + (h + 1) * 128] = _bf(d_o * gn * (sg * (1.0 + rg * (1.0 - sg))))

    acc = pl.BlockSpec((1, 512), lambda n: (0, 0))
    return pl.pallas_call(
        kernel_body, name="ret_bwd", grid=(N,),
        out_shape=(jax.ShapeDtypeStruct((S, 2048), BF16), jax.ShapeDtypeStruct((1, 512), F32),
                   jax.ShapeDtypeStruct((1, 512), F32)),
        in_specs=[col(0), col(1), col(2), col(3), tab, tab, dec, dec, dec, cdec, vec, vec, st,
                  pl.BlockSpec((CHUNK, 512), lambda n: (N - 1 - n, 0))],
        out_specs=(pl.BlockSpec((CHUNK, 2048), lambda n: (N - 1 - n, 0)), acc, acc),
        scratch_shapes=[pltpu.VMEM((4, CHUNK, CHUNK), F32)],
        compiler_params=_cparams(("arbitrary",)),
    )(proj, proj, proj, proj, *tables, gn_g, gn_b, states, d_out)


SB_T = 256
SB_SCALE = 64 ** -0.5
SB_Q_COL, SB_K_COL, SB_V_COL = 2048 // 128, 2560 // 128, 3072 // 128


def _head_masks():
    lane = lax.broadcasted_iota(jnp.int32, (1, 128), 1)
    m0 = (lane < 64).astype(F32)
    return m0, 1.0 - m0


def _tri(n, cmp):
    r = lax.broadcasted_iota(jnp.int32, (n, n), 0)
    c = lax.broadcasted_iota(jnp.int32, (n, n), 1)
    return cmp(r, c)


def _tri_sum(x, tri):
    hi = _bf(x)
    lo = _bf(x - hi.astype(F32))
    return _dg(hi, tri, 1, 0) + _dg(lo, tri, 1, 0)


def _sb_weights(qms, kblks, upper, carry, causal):
    tiles = [(b, h) for b in range(len(kblks)) for h in range(2)]
    zs = [_dg(qms[h], kblks[b], 1, 1) for b, h in tiles]
    lgs = [-(jnp.maximum(z, 0.0) + jnp.log(1.0 + jnp.exp(-jnp.abs(z)))) for z in zs]
    if causal is not None:
        lgs = [jnp.where(causal, lg, 0.0) for lg in lgs]
    carries = list(carry)
    for t in range(len(tiles) - 2):
        carries.append(carries[t] + jnp.sum(lgs[t], axis=1, keepdims=True))
    his = [_bf(lg) for lg in lgs]
    los = [_bf(lg - hi.astype(F32)) for lg, hi in zip(lgs, his)]
    later = [_dg(hi, upper, 1, 0) for hi in his]
    later = [r + _dg(lo, upper, 1, 0) for r, lo in zip(later, los)]
    a = [jnp.exp(lg + z + (r + c)) for lg, z, r, c in zip(lgs, zs, later, carries)]
    if causal is not None:
        a = [jnp.where(causal, x, 0.0) for x in a]
    out = tuple(carries[t] + jnp.sum(lgs[t], axis=1, keepdims=True) for t in (len(tiles) - 2, len(tiles) - 1))
    return [a[2 * b:2 * b + 2] for b in range(len(kblks))], out


def _sb_fwd(proj, comm=None):
    S = proj.shape[0]
    T = min(SB_T, S)
    nq = S // T

    def body(q_ref, k_ref, v_ref, o_ref, a_ref, kb_ref, vm_ref, acc_ref):
        i = pl.program_id(1)
        m0, m1 = _head_masks()

        @pl.when(i == 0)
        def _():
            v = v_ref[...]
            kb_ref[...] = _bf(k_ref[...])
            vm_ref[0] = _bf(v * m0)
            vm_ref[1] = _bf(v * m1)

        q = q_ref[...]
        qm = (_bf(q * (m0 * SB_SCALE)), _bf(q * (m1 * SB_SCALE)))
        upper = _tri(T, lambda r, c: r > c).astype(BF16)
        causal = _tri(T, lambda r, c: c < r)

        def tiles(js, carry, mask, first):
            ks = [pl.multiple_of(j * T, T) for j in js]
            a, out = _sb_weights(qm, [kb_ref[pl.ds(k, T), :] for k in ks], upper, carry, mask)
            a = [[_bf(t) for t in per_block] for per_block in a]
            for b, j in enumerate(js):
                for h in range(2):
                    a_ref[h, j] = a[b][h]
            parts = [_dg(a[b][h], vm_ref[h, pl.ds(k, T), :], 1, 0) for b, k in enumerate(ks) for h in range(2)]
            part = functools.reduce(lambda u, w: u + w, parts)
            if first:
                acc_ref[...] = part
            else:
                acc_ref[...] += part
            return out

        zero = jnp.zeros((T, 1), F32)
        carry = tiles([i], (zero, zero), causal, True)
        carry = lax.fori_loop(0, i % 2, lambda _, c: tiles([i - 1], c, None, False), carry)
        top = i - 1 - i % 2
        carry = lax.fori_loop(0, (i // 2) % 2, lambda _, c: tiles([top, top - 1], c, None, False), carry)
        top = top - 2 * ((i // 2) % 2)
        lax.fori_loop(0, i // 4, lambda jj, c: tiles([top - 4 * jj - b for b in range(4)], c, None, False), carry)
        o_ref[...] = acc_ref[...]

    full = lambda c: pl.BlockSpec((S, 128), lambda p, i, c=c: (0, c + p))
    outs, landed = _pcall(
        body, name="sb_fwd", grid=(4, nq),
        out_shape=[jax.ShapeDtypeStruct((S, SB_W), F32), jax.ShapeDtypeStruct((4, 2, nq, nq, T, T), BF16)],
        in_specs=[pl.BlockSpec((T, 128), lambda p, i: (i, SB_Q_COL + p)), full(SB_K_COL), full(SB_V_COL)],
        out_specs=[pl.BlockSpec((T, 128), lambda p, i: (i, p)),
                   pl.BlockSpec((None, 2, None, nq, T, T), lambda p, i: (p, 0, i, 0, 0, 0))],
        scratch_shapes=[pltpu.VMEM((S, 128), BF16), pltpu.VMEM((2, S, 128), BF16), pltpu.VMEM((T, 128), F32)],
        sem=("arbitrary", "arbitrary"), args=(proj, proj, proj), comm=comm)
    return tuple(outs) if comm is None else (tuple(outs), landed)


def _sb_bwd(proj, a_saved, d_o, comm=None):
    S = proj.shape[0]
    T = min(SB_T, S)
    nq = S // T

    def body(q_ref, k_ref, v_ref, do_ref, a_ref, dq_ref, dk_ref, dv_ref, kb_ref, kbm_ref, vb_ref, dq_acc, dk_acc, dv_acc):
        i = pl.program_id(1)
        m0, m1 = _head_masks()

        @pl.when(i == 0)
        def _():
            k = k_ref[...]
            kb_ref[...] = _bf(k)
            kbm_ref[0] = _bf(k * m0)
            kbm_ref[1] = _bf(k * m1)
            vb_ref[...] = _bf(v_ref[...])
            dk_acc[...] = jnp.zeros_like(dk_acc)
            dv_acc[...] = jnp.zeros_like(dv_acc)

        q, d_out = q_ref[...], do_ref[...]
        qm = (_bf(q * (m0 * SB_SCALE)), _bf(q * (m1 * SB_SCALE)))
        dom = (_bf(d_out * m0), _bf(d_out * m1))
        lower = _tri(T, lambda r, c: r < c).astype(BF16)
        causal = _tri(T, lambda r, c: c < r)

        def up(js, carry, mask):
            ks = [pl.multiple_of(j * T, T) for j in js]
            tiles = [(b, h) for b in range(len(js)) for h in range(2)]
            zs = [_dg(qm[h], kb_ref[pl.ds(ks[b], T), :], 1, 1) for b, h in tiles]
            a = [a_ref[h, js[b]] for b, h in tiles]
            es = [w.astype(F32) * _dg(dom[h], vb_ref[pl.ds(ks[b], T), :], 1, 1) for w, (b, h) in zip(a, tiles)]
            carries = list(carry)
            for t in range(len(tiles)):
                carries.append(carries[t] + jnp.sum(es[t], axis=1, keepdims=True))
            his = [_bf(e) for e in es]
            los = [_bf(e - hi.astype(F32)) for e, hi in zip(es, his)]
            d_lg = [_dg(hi, lower, 1, 0) for hi in his]
            d_lg = [r + _dg(lo, lower, 1, 0) + c for r, lo, c in zip(d_lg, los, carries)]
            ens = [jnp.exp(-jnp.abs(z)) for z in zs]
            invs = [1.0 / (1.0 + en) for en in ens]
            betas = [jnp.where(z >= 0.0, inv, en * inv) for z, en, inv in zip(zs, ens, invs)]
            dzs = [e * (1.0 - b) - d * b for e, b, d in zip(es, betas, d_lg)]
            if mask is not None:
                dzs = [jnp.where(mask, dz, 0.0) for dz in dzs]
            dzs = [_bf(dz) for dz in dzs]
            parts = [_dg(dzs[t], kbm_ref[h, pl.ds(ks[b], T), :], 1, 0) for t, (b, h) in enumerate(tiles)]
            dq_acc[...] += functools.reduce(lambda u, w: u + w, parts)
            for b, k in enumerate(ks):
                dk_acc[pl.ds(k, T), :] += _dg(dzs[2 * b], qm[0], 0, 0) + _dg(dzs[2 * b + 1], qm[1], 0, 0)
                dv_acc[pl.ds(k, T), :] += _dg(a[2 * b], dom[0], 0, 0) + _dg(a[2 * b + 1], dom[1], 0, 0)
            return tuple(carries[-2:])

        zero = jnp.zeros((T, 1), F32)
        dq_acc[...] = jnp.zeros_like(dq_acc)
        carry = lax.fori_loop(0, i // 4, lambda jj, c: up([4 * jj + b for b in range(4)], c, None), (zero, zero))
        done = 4 * (i // 4)
        carry = lax.fori_loop(0, (i // 2) % 2, lambda _, c: up([done, done + 1], c, None), carry)
        carry = lax.fori_loop(0, i % 2, lambda _, c: up([i - 1], c, None), carry)
        up([i], carry, causal)
        dq_ref[...] = _bf(dq_acc[...] * SB_SCALE)

        @pl.when(i == nq - 1)
        def _():
            dk_ref[...] = _bf(dk_acc[...])
            dv_ref[...] = _bf(dv_acc[...])

    full = lambda c: pl.BlockSpec((S, 128), lambda p, i, c=c: (0, c + p))
    tile = pl.BlockSpec((T, 128), lambda p, i: (i, p))
    acc = pl.BlockSpec((S, 128), lambda p, i: (0, p))
    out = jax.ShapeDtypeStruct((S, SB_W), BF16)
    outs, landed = _pcall(
        body, name="sb_bwd", grid=(4, nq), out_shape=[out, out, out],
        in_specs=[pl.BlockSpec((T, 128), lambda p, i: (i, SB_Q_COL + p)), full(SB_K_COL), full(SB_V_COL), tile,
                  pl.BlockSpec((None, 2, None, nq, T, T), lambda p, i: (p, 0, i, 0, 0, 0))],
        out_specs=[tile, acc, acc],
        scratch_shapes=[pltpu.VMEM((S, 128), BF16), pltpu.VMEM((2, S, 128), BF16), pltpu.VMEM((S, 128), BF16),
                        pltpu.VMEM((T, 128), F32), pltpu.VMEM((S, 128), F32), pltpu.VMEM((S, 128), F32)],
        sem=("arbitrary", "arbitrary"), args=(proj, proj, proj, d_o, a_saved), comm=comm)
    return tuple(outs) if comm is None else (tuple(outs), landed)


SGU_U_COL, SGU_V_COL = 3584 // 512, 4096 // 512


def _causal(w):
    r = lax.broadcasted_iota(jnp.int32, (CHUNK, CHUNK), 0)
    c = lax.broadcasted_iota(jnp.int32, (CHUNK, CHUNK), 1)
    return jnp.where(r >= c, w, 0.0)


def _sgu_fwd(proj, ln_g, ln_b, w, b):
    S = proj.shape[0]
    N = S // CHUNK

    def body(u_ref, v_ref, g_ref, b_ref, w_ref, bias_ref, out_ref):
        u = _gelu(u_ref[...])
        xh, _ = _norm_stats(_gelu(v_ref[...]))
        vn = _bf(xh * g_ref[...] + b_ref[...])
        for g in range(4):
            sl = slice(g * 128, (g + 1) * 128)
            sv = _dg(_bf(_causal(w_ref[g])), vn[:, sl], 1, 0) + bias_ref[g]
            out_ref[:, sl] = u[:, sl] * sv

    vec = pl.BlockSpec((1, 512), lambda n: (0, 0))
    return pl.pallas_call(
        body, name="sgu_fwd", grid=(N,),
        out_shape=jax.ShapeDtypeStruct((S, SGU_W), F32),
        in_specs=[pl.BlockSpec((CHUNK, 512), lambda n: (n, SGU_U_COL)),
                  pl.BlockSpec((CHUNK, 512), lambda n: (n, SGU_V_COL)), vec, vec,
                  pl.BlockSpec((4, CHUNK, CHUNK), lambda n: (0, 0, 0)), pl.BlockSpec((4, CHUNK, 1), lambda n: (0, 0, 0))],
        out_specs=pl.BlockSpec((CHUNK, 512), lambda n: (n, 0)),
        compiler_params=_cparams(("parallel",)),
    )(proj, proj, ln_g, ln_b, w, b)


def _sgu_bwd(proj, ln_g, ln_b, w, b, d_out):
    S = proj.shape[0]
    N = S // CHUNK

    def body(u_ref, v_ref, g_ref, b_ref, w_ref, bias_ref, do_ref, dp_ref, dg_ref, db_ref, dw_ref, dbias_ref):
        @pl.when(pl.program_id(0) == 0)
        def _():
            dg_ref[...] = jnp.zeros_like(dg_ref)
            db_ref[...] = jnp.zeros_like(db_ref)
            dw_ref[...] = jnp.zeros_like(dw_ref)
            dbias_ref[...] = jnp.zeros_like(dbias_ref)

        gu, gv = u_ref[...], v_ref[...]
        u = _gelu(gu)
        xh, rstd = _norm_stats(_gelu(gv))
        ln_gain = g_ref[...]
        vn = _bf(xh * ln_gain + b_ref[...])
        d_o = do_ref[...]
        d_vn = []
        for g in range(4):
            sl = slice(g * 128, (g + 1) * 128)
            wc = _bf(_causal(w_ref[g]))
            sv = _dg(wc, vn[:, sl], 1, 0) + bias_ref[g]
            dp_ref[:, sl] = _bf(d_o[:, sl] * sv * _gelu_grad(gu[:, sl]))
            d_sv = d_o[:, sl] * u[:, sl]
            dbias_ref[g] += jnp.sum(d_sv, axis=1, keepdims=True)
            d_svb = _bf(d_sv)
            dw_ref[g] += _causal(_dg(d_svb, vn[:, sl], 1, 1))
            d_vn.append(_dg(wc, d_svb, 0, 0))
        d_vn = jnp.concatenate(d_vn, axis=1)
        dg_ref[...] += jnp.sum(d_vn * xh, axis=0, keepdims=True)
        db_ref[...] += jnp.sum(d_vn, axis=0, keepdims=True)
        dp_ref[:, 512:1024] = _bf(_norm_bwd(d_vn * ln_gain, xh, rstd) * _gelu_grad(gv))

    vec = pl.BlockSpec((1, 512), lambda n: (0, 0))
    wspec = pl.BlockSpec((4, CHUNK, CHUNK), lambda n: (0, 0, 0))
    bspec = pl.BlockSpec((4, CHUNK, 1), lambda n: (0, 0, 0))
    return pl.pallas_call(
        body, name="sgu_bwd", grid=(N,),
        out_shape=(jax.ShapeDtypeStruct((S, 1024), BF16), jax.ShapeDtypeStruct((1, 512), F32),
                   jax.ShapeDtypeStruct((1, 512), F32), jax.ShapeDtypeStruct((4, CHUNK, CHUNK), F32),
                   jax.ShapeDtypeStruct((4, CHUNK, 1), F32)),
        in_specs=[pl.BlockSpec((CHUNK, 512), lambda n: (n, SGU_U_COL)),
                  pl.BlockSpec((CHUNK, 512), lambda n: (n, SGU_V_COL)), vec, vec, wspec, bspec,
                  pl.BlockSpec((CHUNK, 512), lambda n: (n, 0))],
        out_specs=(pl.BlockSpec((CHUNK, 1024), lambda n: (n, 0)), vec, vec, wspec, bspec),
        compiler_params=_cparams(("arbitrary",)),
    )(proj, proj, ln_g, ln_b, w, b, d_out)


GATE_COL = 4608 // 512


def _merge_fwd(proj, branches, p_list, tm=512):
    S = proj.shape[0]
    tm = min(tm, S)

    def body(r_ref, s_ref, g_ref, pr_ref, ps_ref, pg_ref, gr_ref, gs_ref, gg_ref, m_ref, br_ref):
        acc = None
        for k, (x_ref, p_ref, gate_ref) in enumerate(((r_ref, pr_ref, gr_ref), (s_ref, ps_ref, gs_ref),
                                                      (g_ref, pg_ref, gg_ref))):
            br = _dg(_bf(x_ref[...]), _bf(p_ref[...]), 1, 0)
            br_ref[k] = _bf(br)
            term = _sigmoid(gate_ref[...]) * br
            acc = term if acc is None else acc + term
        m_ref[...] = _bf(acc)

    xs = pl.BlockSpec((tm, 512), lambda i, n: (i, 0))
    ps = pl.BlockSpec((512, 512), lambda i, n: (0, n))
    gate = lambda k: pl.BlockSpec((tm, 512), lambda i, n, k=k: (i, GATE_COL + 2 * k + n))
    return pl.pallas_call(
        body, name="merge_fwd", grid=(S // tm, 2),
        out_shape=(jax.ShapeDtypeStruct((S, D_MODEL), BF16), jax.ShapeDtypeStruct((3, S, D_MODEL), BF16)),
        in_specs=[xs, xs, xs, ps, ps, ps, gate(0), gate(1), gate(2)],
        out_specs=(pl.BlockSpec((tm, 512), lambda i, n: (i, n)), pl.BlockSpec((3, tm, 512), lambda i, n: (0, i, n))),
        compiler_params=_cparams(("parallel", "parallel")),
    )(*branches, *p_list, proj, proj, proj)


def _gate_bwd(proj, br, d_merged, tm=512):
    S = proj.shape[0]
    tm = min(tm, S)

    def body(dm_ref, br_ref, gr_ref, gs_ref, gg_ref, *out_refs):
        dm = dm_ref[...]
        for k, gate_ref in enumerate((gr_ref, gs_ref, gg_ref)):
            s = _sigmoid(gate_ref[...])
            out_refs[k][...] = _bf(dm * s)
            out_refs[3 + k][...] = _bf(dm * br_ref[k].astype(F32) * (s * (1.0 - s)))

    gate = lambda k: pl.BlockSpec((tm, 512), lambda i, n, k=k: (i, GATE_COL + 2 * k + n))
    three = pl.BlockSpec((3, tm, 512), lambda i, n: (0, i, n))
    tile = pl.BlockSpec((tm, 512), lambda i, n: (i, n))
    outs = pl.pallas_call(
        body, name="gate_bwd", grid=(S // tm, 2),
        out_shape=[jax.ShapeDtypeStruct((S, D_MODEL), BF16)] * 6,
        in_specs=[tile, three, gate(0), gate(1), gate(2)], out_specs=[tile] * 6,
        compiler_params=_cparams(("parallel", "parallel")),
    )(d_merged, br, proj, proj, proj)
    return outs[:3], outs[3:]


def _ln_bwd(dy, u, g, target=None, tm=256):
    S, D = u.shape
    tm = min(tm, S)
    loss = target is not None

    def body(*refs):
        dy_ref, u_ref, g_ref = refs[:3]
        du_ref, dub_ref, dg_ref, db_ref = refs[3 + loss:7 + loss]

        @pl.when(pl.program_id(0) == 0)
        def _():
            for acc_ref in refs[5 + loss:]:
                acc_ref[...] = jnp.zeros_like(acc_ref)

        dy_t = dy_ref[...]
        if loss:
            err = dy_t - refs[3][...]
            refs[-1][...] += jnp.sum(err * err, axis=0, keepdims=True)
            dy_t = err * (1.0 / D)
        xh, rstd = _norm_stats(u_ref[...])
        dg_ref[...] += jnp.sum(dy_t * xh, axis=0, keepdims=True)
        db_ref[...] += jnp.sum(dy_t, axis=0, keepdims=True)
        du = _norm_bwd(dy_t * g_ref[...], xh, rstd)
        du_ref[...] = du
        dub_ref[...] = _bf(du)

    tile = pl.BlockSpec((tm, D), lambda i: (i, 0))
    vec = pl.BlockSpec((1, D), lambda i: (0, 0))
    row = jax.ShapeDtypeStruct((1, D), F32)
    return pl.pallas_call(
        body, name="ln_bwd", grid=(S // tm,),
        out_shape=[jax.ShapeDtypeStruct((S, D), F32), jax.ShapeDtypeStruct((S, D), BF16)] + [row] * (2 + loss),
        in_specs=[tile, tile, vec] + [tile] * loss, out_specs=[tile, tile] + [vec] * (2 + loss),
        compiler_params=_cparams(("arbitrary",)),
    )(dy, u, g, *([target] if loss else []))


def _layer_fwd(x, x_bf, W, tables, sb_comm=None):
    proj = _matmul(x_bf, W["w_in_t"], "nt", name="proj", tm=1024, tn=768, tk=1024)
    retg, states = _ret_fwd(proj, tables, W["ret_gn_g"], W["ret_gn_b"])
    if sb_comm is None:
        sb, sb_a = _sb_fwd(proj)
    else:
        (sb, sb_a), landed = _sb_fwd(proj, comm=sb_comm[0])
        sb_comm[1](landed)
    sg = _sgu_fwd(proj, W["sgu_ln_g"], W["sgu_ln_b"], W["sgu_w"], W["sgu_b"])
    merged, br = _merge_fwd(proj, (retg, sb, sg), (W["p_ret"], W["p_sb"], W["p_sgu"]))
    u1, x1, x1_bf = _matmul(merged, W["w_out"], "nn", name="out_ln", tm=512, tn=1024, tk=1024, epi="ln",
                            extra=(x, W["ln1_g"], W["ln1_b"]))
    act = _matmul(x1_bf, W["w_up"], "nn", name="up", tm=1024, tn=1024, tk=1024, epi="relu2")
    u2, x2, x2_bf = _matmul(act, W["w_down"], "nn", name="down_ln", tm=512, tn=1024, tk=4096, epi="ln",
                            extra=(x1, W["ln2_g"], W["ln2_b"]))
    saved = dict(x_bf=x_bf, proj=proj, retg=retg, states=states, sb=sb, sb_a=sb_a, sg=sg, merged=merged, br=br, u1=u1,
                 x1_bf=x1_bf, act=act, u2=u2)
    return x2, x2_bf, saved


def _layer_bwd(d_x2, W, tables, sv, chunk_dtype=None, sb_comm_fn=None, dx_comm_fn=None, target=None):
    dt = F32 if chunk_dtype is None else chunk_dtype
    rows, cols = (None, None) if chunk_dtype is None else ("rows", "cols")
    g, landed = {}, {}
    du2, du2_bf, g["ln2_g"], g["ln2_b"], *sq = _ln_bwd(d_x2, sv["u2"], W["ln2_g"], target=target)
    if sq:
        landed["sq"] = sq[0]
    d_hpre = _matmul(du2_bf, W["w_down"], "nt", name="d_act", tm=1024, tn=1024, tk=1024, epi="drelu2",
                     extra=(sv["act"],), out_dtype=BF16)
    g["w_down"] = _matmul(sv["act"], du2_bf, "tn", name="dw_down", tm=512, tn=1024, tk=4096, out_dtype=dt, chunks=rows)
    g["w_up"] = _matmul(sv["x1_bf"], d_hpre, "tn", name="dw_up", tm=1024, tn=512, tk=4096, out_dtype=dt, chunks=cols)
    d_x1 = _matmul(d_hpre, W["w_up"], "nt", name="d_x1", tm=512, tn=1024, tk=4096, epi="add", extra=(du2,))
    du1, du1_bf, g["ln1_g"], g["ln1_b"] = _ln_bwd(d_x1, sv["u1"], W["ln1_g"])
    d_merged = _matmul(du1_bf, W["w_out"], "nt", name="d_merged", tm=1024, tn=1024, tk=1024)
    g["w_out"] = _matmul(sv["merged"], du1_bf, "tn", name="dw_out", tm=1024, tn=512, tk=4096, out_dtype=dt, chunks=rows)
    d_br, d_gate = _gate_bwd(sv["proj"], sv["br"], d_merged)
    d_branch = []
    for k, (nm, act) in enumerate((("p_ret", sv["retg"]), ("p_sb", sv["sb"]), ("p_sgu", sv["sg"]))):
        d_branch.append(_matmul(d_br[k], W[nm], "nt", name="d_" + nm[2:], tm=1024, tn=512, tk=1024))
        g[nm] = _matmul(act, d_br[k], "tn", name="dw_" + nm[2:], tm=512, tn=1024, tk=2048, out_dtype=dt, chunks=cols)
    d_ret, g["ret_gn_g"], g["ret_gn_b"] = _ret_bwd(sv["proj"], tables, W["ret_gn_g"], W["ret_gn_b"], sv["states"],
                                                   d_branch[0])
    if sb_comm_fn is None:
        d_sq, d_sk, d_sv = _sb_bwd(sv["proj"], sv["sb_a"], d_branch[1])
    else:
        (d_sq, d_sk, d_sv), landed["sb"] = _sb_bwd(sv["proj"], sv["sb_a"], d_branch[1], comm=sb_comm_fn(g))
    d_sgu, g["sgu_ln_g"], g["sgu_ln_b"], g["sgu_w"], g["sgu_b"] = _sgu_bwd(
        sv["proj"], W["sgu_ln_g"], W["sgu_ln_b"], W["sgu_w"], W["sgu_b"], d_branch[2])
    d_proj = [d_ret, d_sq, d_sk, d_sv, d_sgu, d_gate[0], d_gate[1], d_gate[2]]
    g["w_in"] = _matmul(d_proj, sv["x_bf"], "tn", name="dw_in", tm=256, tn=1024, tk=4096, out_dtype=dt, chunks=rows)
    if chunk_dtype is None:
        g["w_in"] = g["w_in"].T
    d_x = _matmul_rows_of(d_proj, W["w_in_t"], du1, name="d_x", tm=512,
                          comm=None if dx_comm_fn is None else dx_comm_fn(g))
    if dx_comm_fn is not None:
        d_x, landed["dx"] = d_x
    return d_x, g, landed


BIG = ("w_in", "p_ret", "p_sb", "p_sgu", "w_out", "w_up", "w_down")
SMALL = ("ret_gn_g", "ret_gn_b", "sgu_ln_g", "sgu_ln_b", "sgu_w", "sgu_b", "ln1_g", "ln1_b", "ln2_g", "ln2_b")
GATHER_KIND = {"w_in": "rows", "p_ret": "cols", "p_sb": "cols", "p_sgu": "cols", "w_out": "rows", "w_up": "cols",
               "w_down": "rows"}


def _small_weights(small, l):
    W = {}
    for n in SMALL:
        if n == "sgu_w":
            W[n] = small[n][l]
        elif n == "sgu_b":
            W[n] = small[n][l].reshape(4, CHUNK, 1)
        else:
            W[n] = small[n][l].reshape(1, -1)
    return W


def _local_step(x, target, full, small):
    tables = _ret_tables(x.shape[0])
    Ws = [{**{n: full[n][l] for n in BIG[1:]}, "w_in_t": full["w_in"][l].T, **_small_weights(small, l)}
          for l in range(DEPTH)]
    saved = []
    h, h_bf = x, _bf(x)
    for l in range(DEPTH):
        h, h_bf, sv = _layer_fwd(h, h_bf, Ws[l], tables)
        saved.append(sv)
    grads = [None] * DEPTH
    d_h, grads[-1], landed = _layer_bwd(h, Ws[-1], tables, saved[-1], target=target)
    for l in reversed(range(DEPTH - 1)):
        d_h, grads[l], _ = _layer_bwd(d_h, Ws[l], tables, saved[l])
    return landed["sq"], d_h, grads


def _adam(w, parts, m, v, name):
    L, R, C = w.shape
    tr = next(t for t in (320, 256, 128) if R % t == 0)
    assert len(parts) == L

    def body(*refs):
        w_ref, p_refs, (m_ref, v_ref, g_ref, d_ref, nm_ref, nv_ref) = refs[0], refs[1:1 + L], refs[1 + L:]
        layer = pl.program_id(0)
        g = None
        for li, p_ref in enumerate(p_refs):
            s = p_ref[0].astype(F32)
            for j in range(1, p_ref.shape[0]):
                s = s + p_ref[j].astype(F32)
            g = s if g is None else jnp.where(layer == li, s, g)
        g_ref[...] = g
        d_ref[...], nm_ref[...], nv_ref[...] = _adam_update(w_ref[...], g, m_ref[...], v_ref[...])

    tile = pl.BlockSpec((None, tr, C), lambda l, i: (l, i, 0))
    part = lambda li: pl.BlockSpec((parts[li].shape[0], tr, C), lambda l, i, li=li: (0, jnp.where(l == li, i, 0), 0))
    out = jax.ShapeDtypeStruct((L, R, C), F32)
    return pl.pallas_call(
        body, name=name, grid=(L, R // tr), out_shape=(out, out, out, out),
        in_specs=[tile] + [part(li) for li in range(L)] + [tile, tile],
        out_specs=(tile, tile, tile, tile),
        compiler_params=_cparams(("parallel", "parallel")),
    )(w, *parts, m, v)


def _adam_update(w, g, m, v):
    m2 = ADAM_B1 * m + (1.0 - ADAM_B1) * g
    v2 = ADAM_B2 * v + (1.0 - ADAM_B2) * (g * g)
    m_hat = m2 / (1.0 - ADAM_B1 ** ADAM_STEP)
    v_hat = v2 / (1.0 - ADAM_B2 ** ADAM_STEP)
    return -ADAM_LR * (m_hat / (jnp.sqrt(v_hat) + ADAM_EPS) + ADAM_WD * w), m2, v2


def _adam_small(w, m, v, parts):
    k = len(SMALL)

    def body(*refs):
        w_refs, m_refs, v_refs, p_refs, outs = refs[:k], refs[k:2 * k], refs[2 * k:3 * k], refs[3 * k:5 * k], refs[5 * k:]
        for i in range(k):
            vector = len(w_refs[i].shape) == 2
            for l in range(DEPTH):
                p_ref = p_refs[DEPTH * i + l]
                g = p_ref[0]
                for j in range(1, N_DEV):
                    g = g + p_ref[j]
                at = (slice(l, l + 1), slice(None)) if vector else (l,)
                delta, m2, v2 = _adam_update(w_refs[i][at], g, m_refs[i][at], v_refs[i][at])
                for o_ref, val in zip(outs[4 * i:4 * i + 4], (g, delta, m2, v2)):
                    o_ref[at] = val

    vmem = pl.BlockSpec(memory_space=pltpu.VMEM)
    args = [w[n] for n in SMALL] + [m[n] for n in SMALL] + [v[n] for n in SMALL] + \
           [parts[(n, l)] for n in SMALL for l in range(DEPTH)]
    out_shape = [jax.ShapeDtypeStruct(w[n].shape, F32) for n in SMALL for _ in range(4)]
    outs = pl.pallas_call(body, name="adam_small", out_shape=out_shape, in_specs=[vmem] * len(args),
                          out_specs=[vmem] * len(out_shape), compiler_params=_cparams())(*args)
    return {n: tuple(outs[4 * i:4 * i + 4]) for i, n in enumerate(SMALL)}


WEIGHTS = ("w_in", "ret_gn_g", "ret_gn_b", "sgu_ln_g", "sgu_ln_b", "sgu_w", "sgu_b", "p_ret", "p_sb", "p_sgu", "w_out",
           "ln1_g", "ln1_b", "w_up", "w_down", "ln2_g", "ln2_b")


def kernel(x, w_in, ret_gn_g, ret_gn_b, sgu_ln_g, sgu_ln_b, sgu_w, sgu_b, p_ret, p_sb, p_sgu, w_out, ln1_g, ln1_b, w_up, w_down, ln2_g, ln2_b, loss_target, m_w_in, m_ret_gn_g, m_ret_gn_b, m_sgu_ln_g, m_sgu_ln_b, m_sgu_w, m_sgu_b, m_p_ret, m_p_sb, m_p_sgu, m_w_out, m_ln1_g, m_ln1_b, m_w_up, m_w_down, m_ln2_g, m_ln2_b, v_w_in, v_ret_gn_g, v_ret_gn_b, v_sgu_ln_g, v_sgu_ln_b, v_sgu_w, v_sgu_b, v_p_ret, v_p_sb, v_p_sgu, v_w_out, v_ln1_g, v_ln1_b, v_w_up, v_w_down, v_ln2_g, v_ln2_b):
    w = dict(zip(WEIGHTS, (w_in, ret_gn_g, ret_gn_b, sgu_ln_g, sgu_ln_b, sgu_w, sgu_b, p_ret, p_sb, p_sgu, w_out,
                           ln1_g, ln1_b, w_up, w_down, ln2_g, ln2_b)))
    m = dict(zip(WEIGHTS, (m_w_in, m_ret_gn_g, m_ret_gn_b, m_sgu_ln_g, m_sgu_ln_b, m_sgu_w, m_sgu_b, m_p_ret, m_p_sb,
                           m_p_sgu, m_w_out, m_ln1_g, m_ln1_b, m_w_up, m_w_down, m_ln2_g, m_ln2_b)))
    v = dict(zip(WEIGHTS, (v_w_in, v_ret_gn_g, v_ret_gn_b, v_sgu_ln_g, v_sgu_ln_b, v_sgu_w, v_sgu_b, v_p_ret, v_p_sb,
                           v_p_sgu, v_w_out, v_ln1_g, v_ln1_b, v_w_up, v_w_down, v_ln2_g, v_ln2_b)))

    small = {n: w[n] for n in SMALL}
    shard = {n: _bf(w[n]) for n in BIG}
    shard["w_in"] = shard["w_in"].transpose(0, 2, 1)
    S = x.shape[1]
    x0, target = x.reshape(S, D_MODEL), loss_target.reshape(S, D_MODEL)
    tables = _ret_tables(S)
    Ws = [_small_weights(small, l) for l in range(DEPTH)]

    (Ws[0]["w_in_t"],) = _exchange([_gather_transfer(shard["w_in"], 0, "rows")], "gather_w_in0", relay=True)
    later = [(n, 0) for n in BIG[1:]] + [(n, 1) for n in BIG]

    def weights_landed(landed):
        for (n, l), z in zip(later, landed):
            Ws[l]["w_in_t" if n == "w_in" else n] = z

    gather = _Comm([_gather_transfer(shard[n], l, GATHER_KIND[n]) for n, l in later], relay=True)
    h, h_bf, saved0 = _layer_fwd(x0, _bf(x0), Ws[0], tables, sb_comm=(gather, weights_landed))
    h, _, saved1 = _layer_fwd(h, h_bf, Ws[1], tables)
    d_h, g1, landed1 = _layer_bwd(h, Ws[1], tables, saved1, chunk_dtype=BF16, target=target,
                                  sb_comm_fn=lambda g: _Comm([_scatter_transfer(g[n]) for n in BIG[1:]]))
    loss = lax.psum(0.5 * jnp.sum(landed1["sq"]) / D_MODEL, ("x", "y", "c"))
    early = [("w_in", 1)] + [(n, 0) for n in BIG[1:]]

    def small_slabs(g):
        return [_slab_transfer(g[n].reshape(4, CHUNK) if n == "sgu_b" else g[n]) for n in SMALL]

    def early_scatter(g0):
        return _Comm([_scatter_transfer((g1 if l else g0)[n]) for n, l in early] + small_slabs(g1))

    def late_scatter(g0):
        pairs = _pair_reduce(g0["w_in"], "w_in0_pairs")
        return _Comm([_chip_scatter_transfer(pairs)] + small_slabs(g0))

    d_x, g0, landed = _layer_bwd(d_h, Ws[0], tables, saved0, chunk_dtype=BF16, sb_comm_fn=early_scatter,
                                 dx_comm_fn=late_scatter)
    parts = {**dict(zip(early, landed["sb"])), **{(n, 1): z for n, z in zip(BIG[1:], landed1["sb"])}}
    parts[("w_in", 0)] = landed["dx"][0]
    small_parts = {**{(n, 1): z for n, z in zip(SMALL, landed["sb"][len(early):])},
                   **{(n, 0): z for n, z in zip(SMALL, landed["dx"][1:])}}

    grad, delta, new_m, new_v = {}, {}, {}, {}
    for n in BIG:
        view = (lambda a: a.transpose(0, 2, 1)) if n == "w_in" else (lambda a: a)
        res = _adam(view(w[n]), [parts[(n, l)] for l in range(DEPTH)], view(m[n]), view(v[n]), "adam_" + n)
        grad[n], delta[n], new_m[n], new_v[n] = (view(r) for r in res)
    for n, res in _adam_small(small, m, v, small_parts).items():
        grad[n], delta[n], new_m[n], new_v[n] = res

    return (loss, d_x.reshape(x.shape), *[grad[n] for n in WEIGHTS], *[delta[n] for n in WEIGHTS],
            *[new_m[n] for n in WEIGHTS], *[new_v[n] for n in WEIGHTS])
```

```python
import functools
import math

import numpy as np
import jax
import jax.numpy as jnp
from jax import lax
from jax.experimental import pallas as pl
from jax.experimental.pallas import tpu as pltpu

F32 = jnp.float32
BF16 = jnp.bfloat16

N_DEV = 8
DEPTH = 2
D_MODEL = 1024
CHUNK = 128
RET_W = 512
SB_W = 512
SGU_W = 512
N_IN = 7680
LN_EPS = 1e-5
ALPHA = (2 * DEPTH) ** 0.25
ROPE_BASE = 10000.0
ADAM_LR, ADAM_B1, ADAM_B2, ADAM_EPS, ADAM_WD, ADAM_STEP = 0.001, 0.9, 0.999, 1e-08, 0.01, 10
VMEM_LIMIT = 56 * 1024 * 1024

_GELU_K = math.sqrt(2.0 / math.pi)
_GELU_C = 0.044715


def _cparams(sem=None):
    return pltpu.CompilerParams(dimension_semantics=sem, vmem_limit_bytes=VMEM_LIMIT)


def _dg(a, b, ca, cb):
    return lax.dot_general(a, b, (((ca,), (cb,)), ((), ())), preferred_element_type=F32)


def _bf(x):
    return x.astype(BF16)


def _sigmoid(x):
    return 1.0 / (1.0 + jnp.exp(-x))


def _gelu(x):
    t = jnp.tanh(_GELU_K * (x + _GELU_C * (x * x * x)))
    return x * (0.5 * (1.0 + t))


def _gelu_grad(x):
    t = jnp.tanh(_GELU_K * (x + _GELU_C * (x * x * x)))
    return 0.5 * (1.0 + t) + 0.5 * x * (1.0 - t * t) * (_GELU_K * (1.0 + 3.0 * _GELU_C * x * x))


def _norm_stats(u):
    mu = jnp.mean(u, axis=-1, keepdims=True)
    d = u - mu
    var = jnp.mean(d * d, axis=-1, keepdims=True)
    rstd = lax.rsqrt(var + LN_EPS)
    return d * rstd, rstd


def _norm_bwd(dxh, xh, rstd):
    return rstd * (dxh - jnp.mean(dxh, axis=-1, keepdims=True) - xh * jnp.mean(dxh * xh, axis=-1, keepdims=True))


class _Transfer:
    def __init__(self, src, dst_shape, src_at, dst_at, same_core=False):
        self.src, self.dst_shape, self.src_at, self.dst_at = src, tuple(dst_shape), src_at, dst_at
        self.same_core = same_core


def _gather_transfer(shard, l, kind):
    _, r, c = shard.shape
    src_at = lambda ref, p: ref.at[l]
    if kind == "slab":
        return _Transfer(shard, (N_DEV, r, c), src_at, lambda ref, s: ref.at[s])
    if kind == "rows":
        return _Transfer(shard, (N_DEV * r, c), src_at, lambda ref, s: ref.at[pl.ds(pl.multiple_of(s * r, r), r), :])
    return _Transfer(shard, (r, N_DEV * c), src_at, lambda ref, s: ref.at[:, pl.ds(pl.multiple_of(s * c, c), c)])


def _scatter_transfer(chunks):
    return _Transfer(chunks, chunks.shape, lambda ref, p: ref.at[p], lambda ref, s: ref.at[s])


def _slab_transfer(arr):
    return _Transfer(arr, (N_DEV,) + arr.shape, lambda ref, p: ref, lambda ref, s: ref.at[s])


class _Comm:
    def __init__(self, transfers, relay=False):
        self.transfers = list(transfers)
        self.relay = relay
        self.n = len(self.transfers)
        self.arrays = [t.src for t in self.transfers]
        self.out_shape = [jax.ShapeDtypeStruct(t.dst_shape, t.src.dtype) for t in self.transfers]
        self.scratch = [pltpu.SemaphoreType.DMA((self.n * (N_DEV - 1),)), pltpu.SemaphoreType.DMA((self.n * (N_DEV - 1),)),
                        pltpu.SemaphoreType.DMA((self.n,))]

    def _relay_copies(self, srcs, dsts, send_sems, recv_sems, local_sems):
        x, y, c = lax.axis_index("x"), lax.axis_index("y"), lax.axis_index("c")
        me = 4 * x + 2 * y + c
        chips = [(1 - x, y), (x, 1 - y), (1 - x, 1 - y)]
        first, passed, own = [], [], []
        for t, tr in enumerate(self.transfers):
            def copy(k, src, sender, to, t=t, tr=tr):
                return pltpu.make_async_remote_copy(
                    src_ref=src, dst_ref=tr.dst_at(dsts[t], sender), send_sem=send_sems.at[t * (N_DEV - 1) + k],
                    recv_sem=recv_sems.at[t * (N_DEV - 1) + k], device_id=to, device_id_type=pl.DeviceIdType.MESH)

            mine = tr.src_at(srcs[t], me)
            first.append([copy(0, mine, me, (x, y, 1 - c))] + [copy(1 + j, mine, me, (px, py, c))
                                                                for j, (px, py) in enumerate(chips)])
            passed.append([copy(4 + j, tr.dst_at(dsts[t], 4 * px + 2 * py + c), 4 * px + 2 * py + c, (x, y, 1 - c))
                           for j, (px, py) in enumerate(chips)])
            own.append(pltpu.make_async_copy(mine, tr.dst_at(dsts[t], me), local_sems.at[t]))
        return first, passed, own

    def _copies(self, srcs, dsts, send_sems, recv_sems, local_sems):
        x, y, c = lax.axis_index("x"), lax.axis_index("y"), lax.axis_index("c")
        me = 4 * x + 2 * y + c
        copies = []
        for d in range(1, N_DEV):
            px = 1 - x if d & 4 else x
            py = 1 - y if d & 2 else y
            pc = 1 - c if d & 1 else c
            for t, tr in enumerate(self.transfers):
                if tr.same_core and d & 1:
                    continue
                peer, mine = (2 * px + py, 2 * x + y) if tr.same_core else (4 * px + 2 * py + pc, me)
                k = t * (N_DEV - 1) + d - 1
                copies.append(pltpu.make_async_remote_copy(
                    src_ref=tr.src_at(srcs[t], peer), dst_ref=tr.dst_at(dsts[t], mine),
                    send_sem=send_sems.at[k], recv_sem=recv_sems.at[k],
                    device_id=(px, py, pc), device_id_type=pl.DeviceIdType.MESH))
        own = []
        for t, tr in enumerate(self.transfers):
            mine = 2 * x + y if tr.same_core else me
            own.append(pltpu.make_async_copy(tr.src_at(srcs[t], mine), tr.dst_at(dsts[t], mine), local_sems.at[t]))
        return copies, own

    def start(self, srcs, dsts, *sems):
        if self.relay:
            first, _, own = self._relay_copies(srcs, dsts, *sems)
            for cp in own + [cp for per_t in first for cp in per_t]:
                cp.start()
            return
        copies, own = self._copies(srcs, dsts, *sems)
        for cp in own + copies:
            cp.start()

    def finish(self, srcs, dsts, *sems):
        if self.relay:
            first, passed, own = self._relay_copies(srcs, dsts, *sems)
            for j in range(3):
                for t in range(self.n):
                    first[t][1 + j].wait_recv()
                    passed[t][j].start()
            for t in range(self.n):
                first[t][0].wait_recv()
                for cp in passed[t]:
                    cp.wait_recv()
            for t in range(self.n):
                for cp in first[t] + passed[t]:
                    cp.wait_send()
                own[t].wait()
            return
        copies, own = self._copies(srcs, dsts, *sems)
        for cp in copies + own:
            cp.wait()


def _pcall(body, *, name, grid, in_specs, out_specs, out_shape, scratch_shapes, sem, args, comm=None):
    in_specs, out_specs, out_shape = list(in_specs), list(out_specs), list(out_shape)
    if comm is None:
        outs = pl.pallas_call(body, name=name, grid=grid, in_specs=in_specs, out_specs=out_specs, out_shape=out_shape,
                              scratch_shapes=list(scratch_shapes), compiler_params=_cparams(sem))(*args)
        return list(outs), []
    n_in, n_out, n_scr, k = len(in_specs), len(out_specs), len(scratch_shapes), comm.n

    def carrier(*refs):
        ins, cin = refs[:n_in], refs[n_in:n_in + k]
        outs, cout = refs[n_in + k:n_in + k + n_out], refs[n_in + k + n_out:n_in + 2 * k + n_out]
        scr, sems = refs[n_in + 2 * k + n_out:n_in + 2 * k + n_out + n_scr], refs[n_in + 2 * k + n_out + n_scr:]
        ids = [pl.program_id(d) for d in range(len(grid))]
        first = functools.reduce(jnp.logical_and, [i == 0 for i in ids])
        last = functools.reduce(jnp.logical_and, [i == g - 1 for i, g in zip(ids, grid)])

        @pl.when(first)
        def _():
            comm.start(cin, cout, *sems)

        body(*ins, *outs, *scr)

        @pl.when(last)
        def _():
            comm.finish(cin, cout, *sems)

    hbm = pl.BlockSpec(memory_space=pl.ANY)
    outs = pl.pallas_call(
        carrier, name=name, grid=grid, in_specs=in_specs + [hbm] * k, out_specs=out_specs + [hbm] * k,
        out_shape=out_shape + comm.out_shape, scratch_shapes=list(scratch_shapes) + comm.scratch,
        compiler_params=_cparams(tuple("arbitrary" for _ in grid)),
    )(*args, *comm.arrays)
    return list(outs[:n_out]), list(outs[n_out:])


def _exchange(transfers, name, relay=False):
    comm = _Comm(transfers, relay)

    def body(*refs):
        k = comm.n
        comm.start(refs[:k], refs[k:2 * k], *refs[2 * k:])
        comm.finish(refs[:k], refs[k:2 * k], *refs[2 * k:])

    hbm = pl.BlockSpec(memory_space=pl.ANY)
    return pl.pallas_call(body, name=name, out_shape=comm.out_shape, in_specs=[hbm] * comm.n, out_specs=[hbm] * comm.n,
                          scratch_shapes=comm.scratch)(*comm.arrays)


def _pair_reduce(chunks, name, tr=320):
    _, r, c = chunks.shape
    tr = min(tr, r)
    assert r % tr == 0

    def swap(src_ref, dst_ref, send_sems, recv_sems):
        x, y, core = lax.axis_index("x"), lax.axis_index("y"), lax.axis_index("c")
        copies = [pltpu.make_async_remote_copy(
            src_ref=src_ref.at[2 * k + 1 - core], dst_ref=dst_ref.at[k], send_sem=send_sems.at[k],
            recv_sem=recv_sems.at[k], device_id=(x, y, 1 - core), device_id_type=pl.DeviceIdType.MESH) for k in range(4)]
        for cp in copies:
            cp.start()
        for cp in copies:
            cp.wait()

    hbm = pl.BlockSpec(memory_space=pl.ANY)
    theirs = pl.pallas_call(swap, name=name + "_swap", out_shape=jax.ShapeDtypeStruct((4, r, c), chunks.dtype),
                            in_specs=[hbm], out_specs=hbm,
                            scratch_shapes=[pltpu.SemaphoreType.DMA((4,)), pltpu.SemaphoreType.DMA((4,))])(chunks)

    def add(mine_ref, theirs_ref, out_ref):
        core = lax.axis_index("c")
        both = mine_ref[...].astype(F32)
        out_ref[...] = (jnp.where(core == 0, both[0], both[1]) + theirs_ref[...].astype(F32)).astype(out_ref.dtype)

    return pl.pallas_call(
        add, name=name + "_add", grid=(4, r // tr), out_shape=jax.ShapeDtypeStruct((4, r, c), chunks.dtype),
        in_specs=[pl.BlockSpec((None, 2, tr, c), lambda k, i: (k, 0, i, 0)), pl.BlockSpec((None, tr, c), lambda k, i: (k, i, 0))],
        out_specs=pl.BlockSpec((None, tr, c), lambda k, i: (k, i, 0)),
        compiler_params=_cparams(("parallel", "parallel")),
    )(chunks.reshape(4, 2, r, c), theirs)


def _chip_scatter_transfer(pairs):
    return _Transfer(pairs, pairs.shape, lambda ref, p: ref.at[p], lambda ref, s: ref.at[s], same_core=True)


def _matmul(a, b, mode, *, name, tm, tn, tk, epi=None, extra=(), out_dtype=F32, chunks=None, comm=None):
    pieces = list(a) if isinstance(a, (list, tuple)) else [a]
    rows_a, cols_a = pieces[0].shape[0], sum(p.shape[1] for p in pieces)
    if mode == "nn":
        (M, K), N = (rows_a, cols_a), b.shape[1]
    elif mode == "nt":
        (M, K), N = (rows_a, cols_a), b.shape[0]
    else:
        (K, M), N = (rows_a, cols_a), b.shape[1]
    tm, tn, tk = min(tm, M), min(tn, N), min(tk, K)
    assert M % tm == 0 and N % tn == 0 and K % tk == 0 and (epi != "ln" or tn == N), (name, M, N, K)
    nk = K // tk
    tile_cols, axis = (tm, 0) if mode == "tn" else (tk, 2)
    assert all(p.shape[1] % tile_cols == 0 for p in pieces)
    counts = [p.shape[1] // tile_cols for p in pieces]
    starts = [sum(counts[:q]) for q in range(len(pieces))]

    def a_spec_of(q):
        at = lambda t: jnp.clip(t - starts[q], 0, counts[q] - 1) if len(pieces) > 1 else t
        return {"nn": pl.BlockSpec((tm, tk), lambda i, j, k: (i, at(k))),
                "nt": pl.BlockSpec((tm, tk), lambda i, j, k: (i, at(k))),
                "tn": pl.BlockSpec((tk, tm), lambda i, j, k: (k, at(i)))}[mode]

    n_a = len(pieces)
    b_mode = pl.Buffered(1) if (nk == 1 and tn == N and n_a > 1) else None
    b_spec = {"nn": pl.BlockSpec((tk, tn), lambda i, j, k: (k, j), pipeline_mode=b_mode),
              "nt": pl.BlockSpec((tn, tk), lambda i, j, k: (j, k), pipeline_mode=b_mode),
              "tn": pl.BlockSpec((tk, tn), lambda i, j, k: (k, j), pipeline_mode=b_mode)}[mode]
    ca, cb = {"nn": (1, 0), "nt": (1, 1), "tn": (0, 0)}[mode]
    tile = pl.BlockSpec((tm, tn), lambda i, j, k: (i, j))
    row = pl.BlockSpec((1, tn), lambda i, j, k: (0, j))
    n_extra = {None: 0, "add": 1, "relu2": 0, "drelu2": 1, "ln": 3}[epi]
    assert len(extra) == n_extra
    extra_specs = {None: [], "add": [tile], "relu2": [], "drelu2": [tile], "ln": [tile, row, row]}[epi]
    split = 0
    if epi == "relu2":
        out_shape, out_specs = (jax.ShapeDtypeStruct((M, N), BF16),), (tile,)
    elif epi == "ln":
        out_shape = (jax.ShapeDtypeStruct((M, N), F32), jax.ShapeDtypeStruct((M, N), F32),
                     jax.ShapeDtypeStruct((M, N), BF16))
        out_specs = (tile, tile, tile)
    elif chunks == "cols":
        c = N // N_DEV
        out_shape = (jax.ShapeDtypeStruct((N_DEV, M, c), out_dtype),)
        if tn == N:
            split = c
            out_specs = (pl.BlockSpec((N_DEV, tm, c), lambda i, j, k: (0, i, 0)),)
        else:
            assert c % tn == 0
            out_specs = (pl.BlockSpec((None, tm, tn), lambda i, j, k: (j // (c // tn), i, j % (c // tn))),)
    else:
        out_shape, out_specs = (jax.ShapeDtypeStruct((M, N), out_dtype),), (tile,)
    n_out = len(out_shape)

    def body(*refs):
        a_refs, b_ref = refs[:n_a], refs[n_a]
        ex = refs[n_a + 1:n_a + 1 + n_extra]
        outs = refs[n_a + 1 + n_extra:n_a + 1 + n_extra + n_out]
        acc_ref = refs[-1]
        k = pl.program_id(2)

        def finish(acc):
            if epi == "add":
                outs[0][...] = (acc + ALPHA * ex[0][...]).astype(out_dtype)
            elif epi == "relu2":
                r = jnp.maximum(acc, 0.0)
                outs[0][...] = _bf(r * r)
            elif epi == "drelu2":
                outs[0][...] = (acc * (2.0 * jnp.sqrt(ex[0][...].astype(F32)))).astype(out_dtype)
            elif epi == "ln":
                u = ALPHA * ex[0][...] + acc
                xh, _ = _norm_stats(u)
                y = xh * ex[1][...] + ex[2][...]
                outs[0][...] = u
                outs[1][...] = y
                outs[2][...] = _bf(y)
            elif split:
                for p in range(N_DEV):
                    outs[0][p] = acc[:, p * split:(p + 1) * split].astype(out_dtype)
            else:
                outs[0][...] = acc.astype(out_dtype)

        def step(a_ref, first, middle, last):
            part = _dg(_bf(a_ref[...]), _bf(b_ref[...]), ca, cb)
            if nk == 1:
                finish(part)
                return
            if first:
                @pl.when(k == 0)
                def _():
                    acc_ref[...] = part

            if middle:
                @pl.when(jnp.logical_and(k > 0, k < nk - 1))
                def _():
                    acc_ref[...] += part

            if last:
                @pl.when(k == nk - 1)
                def _():
                    finish(acc_ref[...] + part)

        if n_a == 1:
            step(a_refs[0], True, True, True)
        else:
            t = pl.program_id(axis)
            for q in range(n_a):
                along_k = axis == 2
                first = not along_k or starts[q] == 0
                last = not along_k or starts[q] + counts[q] == nk
                middle = not along_k or counts[q] > int(first) + int(last)

                @pl.when(jnp.logical_and(t >= starts[q], t < starts[q] + counts[q]))
                def _(q=q, first=first, middle=middle, last=last):
                    step(a_refs[q], first, middle, last)

    outs, landed = _pcall(
        body, name=name, out_shape=out_shape, grid=(M // tm, N // tn, nk),
        in_specs=[a_spec_of(q) for q in range(n_a)] + [b_spec] + extra_specs, out_specs=out_specs,
        scratch_shapes=[pltpu.VMEM((tm, tn) if nk > 1 else (8, 128), F32)], sem=("parallel", "parallel", "arbitrary"),
        args=(*pieces, b, *extra), comm=comm)
    res = outs[0] if n_out == 1 else tuple(outs)
    if chunks == "rows":
        res = res.reshape(N_DEV, M // N_DEV, N)
    return res if comm is None else (res, landed)


def _matmul_rows_of(pieces, b, res, *, name, tm, comm=None):
    M, (K, N) = pieces[0].shape[0], b.shape
    tm = min(tm, M)
    subs, start = [], 0
    for q, p in enumerate(pieces):
        w = p.shape[1]
        step = w if start % w == 0 else 512
        assert w % step == 0 and start % step == 0
        subs += [(q, off, step, start + off) for off in range(0, w, step)]
        start += w
    assert start == K
    n_p, n_s = len(pieces), len(subs)

    def body(*refs):
        a_refs, b_refs, res_ref, out_ref = refs[:n_p], refs[n_p:n_p + n_s], refs[n_p + n_s], refs[n_p + n_s + 1]
        acc = None
        for (q, off, w, _), b_ref in zip(subs, b_refs):
            part = _dg(_bf(a_refs[q][:, off:off + w]), _bf(b_ref[...]), 1, 0)
            acc = part if acc is None else acc + part
        out_ref[...] = acc + ALPHA * res_ref[...]

    tile = pl.BlockSpec((tm, N), lambda i: (i, 0))
    outs, landed = _pcall(
        body, name=name, grid=(M // tm,), out_shape=[jax.ShapeDtypeStruct((M, N), F32)],
        in_specs=[pl.BlockSpec((tm, p.shape[1]), lambda i: (i, 0)) for p in pieces] +
                 [pl.BlockSpec((w, N), lambda i, r=row // w: (r, 0), pipeline_mode=pl.Buffered(1))
                  for _, _, w, row in subs] + [tile],
        out_specs=[tile], scratch_shapes=[], sem=("parallel",), args=(*pieces, *([b] * n_s), res), comm=comm)
    return outs[0] if comm is None else (outs[0], landed)


def _ret_tables(S):
    half = 64
    inv_freq = ROPE_BASE ** (-jnp.arange(half, dtype=F32) / half)
    ang = jnp.arange(S, dtype=jnp.int32).astype(F32)[:, None] * inv_freq[None, :]
    cos, sin = jnp.cos(ang), jnp.sin(ang)
    cosf = jnp.concatenate([cos, cos], axis=1)
    sinf = jnp.concatenate([-sin, sin], axis=1)
    log_g = jnp.log(1.0 - 2.0 ** (-5.0 - jnp.arange(4, dtype=F32)))
    idx = jnp.arange(CHUNK, dtype=F32)
    diff = idx[:, None] - idx[None, :]
    md = jnp.where(diff[None] >= 0, jnp.exp(log_g[:, None, None] * diff[None]), 0.0)
    kd = jnp.exp(log_g[:, None] * (CHUNK - 1 - idx)[None, :])
    qd = jnp.exp(log_g[:, None] * (idx + 1.0)[None, :])
    cd = jnp.exp(log_g * CHUNK)
    bc = lambda t: jnp.broadcast_to(t[:, :, None], (4, CHUNK, CHUNK))
    return cosf, sinf, md, bc(qd), bc(kd), jnp.broadcast_to(cd[:, None, None], (4, 8, CHUNK))


def _rot(x, cosf, sinf):
    return x * cosf + pltpu.roll(x, 64, 1) * sinf


def _rot_t(dx, cosf, sinf):
    return dx * cosf - pltpu.roll(dx, 64, 1) * sinf


def _ret_specs(rev, N):
    rn = (lambda n: N - 1 - n) if rev else (lambda n: n)
    col = lambda c: pl.BlockSpec((CHUNK, 512), lambda n, c=c: (rn(n), c))
    tab = pl.BlockSpec((CHUNK, CHUNK), lambda n: (rn(n), 0))
    dec = pl.BlockSpec((4, CHUNK, CHUNK), lambda n: (0, 0, 0))
    cdec = pl.BlockSpec((4, 8, CHUNK), lambda n: (0, 0, 0))
    vec = pl.BlockSpec((1, 512), lambda n: (0, 0))
    st = pl.BlockSpec((1, 4, CHUNK, CHUNK), lambda n: (rn(n), 0, 0, 0))
    return col, tab, dec, cdec, vec, st


def _ret_fwd(proj, tables, gn_g, gn_b):
    S = proj.shape[0]
    N = S // CHUNK
    col, tab, dec, cdec, vec, st = _ret_specs(False, N)

    def body(q_ref, k_ref, v_ref, g_ref, cos_ref, sin_ref, md_ref, qd_ref, kd_ref, cd_ref, gng_ref, gnb_ref,
             out_ref, st_ref, state):
        @pl.when(pl.program_id(0) == 0)
        def _():
            state[...] = jnp.zeros_like(state)

        cosf, sinf = cos_ref[...], sin_ref[...]
        for h in range(4):
            sl = slice(h * 128, (h + 1) * 128)
            qr = _rot(q_ref[:, sl].astype(F32), cosf, sinf)
            kr = _rot(k_ref[:, sl].astype(F32), cosf, sinf) * (128 ** -0.5)
            vb = _bf(v_ref[:, sl])
            s0 = state[h]
            st_ref[0, h] = s0
            sc = _dg(_bf(qr), _bf(kr), 1, 1) * md_ref[h]
            r = _dg(_bf(sc), vb, 1, 0) + _dg(_bf(qr * qd_ref[h]), _bf(s0), 1, 0)
            state[h] = s0 * cd_ref[h, 0:1, :] + _dg(_bf(kr * kd_ref[h]), vb, 0, 0)
            y, _ = _norm_stats(r)
            rg = g_ref[:, sl].astype(F32)
            out_ref[:, sl] = rg * _sigmoid(rg) * (y * gng_ref[:, sl] + gnb_ref[:, sl])

    return pl.pallas_call(
        body, name="ret_fwd", grid=(N,),
        out_shape=(jax.ShapeDtypeStruct((S, RET_W), F32), jax.ShapeDtypeStruct((N, 4, CHUNK, CHUNK), F32)),
        in_specs=[col(0), col(1), col(2), col(3), tab, tab, dec, dec, dec, cdec, vec, vec],
        out_specs=(pl.BlockSpec((CHUNK, 512), lambda n: (n, 0)), st),
        scratch_shapes=[pltpu.VMEM((4, CHUNK, CHUNK), F32)],
        compiler_params=_cparams(("arbitrary",)),
    )(proj, proj, proj, proj, *tables, gn_g, gn_b)


def _ret_bwd(proj, tables, gn_g, gn_b, states, d_out):
    S = proj.shape[0]
    N = S // CHUNK
    col, tab, dec, cdec, vec, st = _ret_specs(True, N)

    def kernel_body(q_ref, k_ref, v_ref, g_ref, cos_ref, sin_ref, md_ref, qd_ref, kd_ref, cd_ref, gng_ref, gnb_ref,
                    st_ref, do_ref, dp_ref, dg_ref, db_ref, gstate):
        @pl.when(pl.program_id(0) == 0)
        def _():
            gstate[...] = jnp.zeros_like(gstate)
            dg_ref[...] = jnp.zeros_like(dg_ref)
            db_ref[...] = jnp.zeros_like(db_ref)

        cosf, sinf = cos_ref[...], sin_ref[...]
        for h in range(4):
            sl = slice(h * 128, (h + 1) * 128)
            qr = _rot(q_ref[:, sl].astype(F32), cosf, sinf)
            kr = _rot(k_ref[:, sl].astype(F32), cosf, sinf) * (128 ** -0.5)
            qb, kb, vb = _bf(qr), _bf(kr), _bf(v_ref[:, sl])
            s0b = _bf(st_ref[0, h])
            md, qd, kd = md_ref[h], qd_ref[h], kd_ref[h]
            scb = _bf(_dg(qb, kb, 1, 1) * md)
            qdb = _bf(qr * qd)
            kdb = _bf(kr * kd)
            r = _dg(scb, vb, 1, 0) + _dg(qdb, s0b, 1, 0)
            y, rstd = _norm_stats(r)
            gng = gng_ref[:, sl]
            gn = y * gng + gnb_ref[:, sl]
            rg = g_ref[:, sl].astype(F32)
            sg = _sigmoid(rg)
            d_o = do_ref[:, sl]
            d_gn = d_o * (rg * sg)
            dg_ref[:, sl] += jnp.sum(d_gn * y, axis=0, keepdims=True)
            db_ref[:, sl] += jnp.sum(d_gn, axis=0, keepdims=True)
            drb = _bf(_norm_bwd(d_gn * gng, y, rstd))
            g0 = gstate[h]
            gb = _bf(g0)
            dscb = _bf(_dg(drb, vb, 1, 1) * md)
            dqr = _dg(dscb, kb, 1, 0) + _dg(drb, s0b, 1, 1) * qd
            dkr = _dg(dscb, qb, 0, 0) + _dg(vb, gb, 1, 1) * kd
            dv = _dg(scb, drb, 0, 0) + _dg(kdb, gb, 1, 0)
            gstate[h] = g0 * cd_ref[h, 0:1, :] + _dg(qdb, drb, 0, 0)
            dp_ref[:, 0 * 512 + h * 128:0 * 512 + (h + 1) * 128] = _bf(_rot_t(dqr, cosf, sinf))
            dp_ref[:, 1 * 512 + h * 128:1 * 512 + (h + 1) * 128] = _bf(_rot_t(dkr, cosf, sinf) * (128 ** -0.5))
            dp_ref[:, 2 * 512 + h * 128:2 * 512 + (h + 1) * 128] = _bf(dv)
            dp_ref[:, 3 * 512 + h * 128:3 * 512 + (h + 1) * 128] = _bf(d_o * gn * (sg * (1.0 + rg * (1.0 - sg))))

    acc = pl.BlockSpec((1, 512), lambda n: (0, 0))
    return pl.pallas_call(
        kernel_body, name="ret_bwd", grid=(N,),
        out_shape=(jax.ShapeDtypeStruct((S, 2048), BF16), jax.ShapeDtypeStruct((1, 512), F32),
                   jax.ShapeDtypeStruct((1, 512), F32)),
        in_specs=[col(0), col(1), col(2), col(3), tab, tab, dec, dec, dec, cdec, vec, vec, st,
                  pl.BlockSpec((CHUNK, 512), lambda n: (N - 1 - n, 0))],
        out_specs=(pl.BlockSpec((CHUNK, 2048), lambda n: (N - 1 - n, 0)), acc, acc),
        scratch_shapes=[pltpu.VMEM((4, CHUNK, CHUNK), F32)],
        compiler_params=_cparams(("arbitrary",)),
    )(proj, proj, proj, proj, *tables, gn_g, gn_b, states, d_out)


SB_T = 256
SB_SCALE = 64 ** -0.5
SB_Q_COL, SB_K_COL, SB_V_COL = 2048 // 128, 2560 // 128, 3072 // 128


def _head_masks():
    lane = lax.broadcasted_iota(jnp.int32, (1, 128), 1)
    m0 = (lane < 64).astype(F32)
    return m0, 1.0 - m0


def _tri(n, cmp):
    r = lax.broadcasted_iota(jnp.int32, (n, n), 0)
    c = lax.broadcasted_iota(jnp.int32, (n, n), 1)
    return cmp(r, c)


def _tri_sum(x, tri):
    hi = _bf(x)
    lo = _bf(x - hi.astype(F32))
    return _dg(hi, tri, 1, 0) + _dg(lo, tri, 1, 0)


def _sb_weights(qms, kblks, upper, carry, causal):
    tiles = [(b, h) for b in range(len(kblks)) for h in range(2)]
    zs = [_dg(qms[h], kblks[b], 1, 1) for b, h in tiles]
    lgs = [-(jnp.maximum(z, 0.0) + jnp.log(1.0 + jnp.exp(-jnp.abs(z)))) for z in zs]
    if causal is not None:
        lgs = [jnp.where(causal, lg, 0.0) for lg in lgs]
    carries = list(carry)
    for t in range(len(tiles) - 2):
        carries.append(carries[t] + jnp.sum(lgs[t], axis=1, keepdims=True))
    his = [_bf(lg) for lg in lgs]
    los = [_bf(lg - hi.astype(F32)) for lg, hi in zip(lgs, his)]
    later = [_dg(hi, upper, 1, 0) for hi in his]
    later = [r + _dg(lo, upper, 1, 0) for r, lo in zip(later, los)]
    a = [jnp.exp(lg + z + (r + c)) for lg, z, r, c in zip(lgs, zs, later, carries)]
    if causal is not None:
        a = [jnp.where(causal, x, 0.0) for x in a]
    out = tuple(carries[t] + jnp.sum(lgs[t], axis=1, keepdims=True) for t in (len(tiles) - 2, len(tiles) - 1))
    return [a[2 * b:2 * b + 2] for b in range(len(kblks))], out


def _sb_fwd(proj, comm=None):
    S = proj.shape[0]
    T = min(SB_T, S)
    nq = S // T

    def body(q_ref, k_ref, v_ref, o_ref, a_ref, kb_ref, vm_ref, acc_ref):
        i = pl.program_id(1)
        m0, m1 = _head_masks()

        @pl.when(i == 0)
        def _():
            v = v_ref[...]
            kb_ref[...] = _bf(k_ref[...])
            vm_ref[0] = _bf(v * m0)
            vm_ref[1] = _bf(v * m1)

        q = q_ref[...]
        qm = (_bf(q * (m0 * SB_SCALE)), _bf(q * (m1 * SB_SCALE)))
        upper = _tri(T, lambda r, c: r > c).astype(BF16)
        causal = _tri(T, lambda r, c: c < r)

        def tiles(js, carry, mask, first):
            ks = [pl.multiple_of(j * T, T) for j in js]
            a, out = _sb_weights(qm, [kb_ref[pl.ds(k, T), :] for k in ks], upper, carry, mask)
            a = [[_bf(t) for t in per_block] for per_block in a]
            for b, j in enumerate(js):
                for h in range(2):
                    a_ref[h, j] = a[b][h]
            parts = [_dg(a[b][h], vm_ref[h, pl.ds(k, T), :], 1, 0) for b, k in enumerate(ks) for h in range(2)]
            part = functools.reduce(lambda u, w: u + w, parts)
            if first:
                acc_ref[...] = part
            else:
                acc_ref[...] += part
            return out

        zero = jnp.zeros((T, 1), F32)
        carry = tiles([i], (zero, zero), causal, True)
        carry = lax.fori_loop(0, i % 2, lambda _, c: tiles([i - 1], c, None, False), carry)
        top = i - 1 - i % 2
        carry = lax.fori_loop(0, (i // 2) % 2, lambda _, c: tiles([top, top - 1], c, None, False), carry)
        top = top - 2 * ((i // 2) % 2)
        lax.fori_loop(0, i // 4, lambda jj, c: tiles([top - 4 * jj - b for b in range(4)], c, None, False), carry)
        o_ref[...] = acc_ref[...]

    full = lambda c: pl.BlockSpec((S, 128), lambda p, i, c=c: (0, c + p))
    outs, landed = _pcall(
        body, name="sb_fwd", grid=(4, nq),
        out_shape=[jax.ShapeDtypeStruct((S, SB_W), F32), jax.ShapeDtypeStruct((4, 2, nq, nq, T, T), BF16)],
        in_specs=[pl.BlockSpec((T, 128), lambda p, i: (i, SB_Q_COL + p)), full(SB_K_COL), full(SB_V_COL)],
        out_specs=[pl.BlockSpec((T, 128), lambda p, i: (i, p)),
                   pl.BlockSpec((None, 2, None, nq, T, T), lambda p, i: (p, 0, i, 0, 0, 0))],
        scratch_shapes=[pltpu.VMEM((S, 128), BF16), pltpu.VMEM((2, S, 128), BF16), pltpu.VMEM((T, 128), F32)],
        sem=("arbitrary", "arbitrary"), args=(proj, proj, proj), comm=comm)
    return tuple(outs) if comm is None else (tuple(outs), landed)


def _sb_bwd(proj, a_saved, d_o, comm=None):
    S = proj.shape[0]
    T = min(SB_T, S)
    nq = S // T

    def body(q_ref, k_ref, v_ref, do_ref, a_ref, dq_ref, dk_ref, dv_ref, kb_ref, kbm_ref, vb_ref, dq_acc, dk_acc, dv_acc):
        i = pl.program_id(1)
        m0, m1 = _head_masks()

        @pl.when(i == 0)
        def _():
            k = k_ref[...]
            kb_ref[...] = _bf(k)
            kbm_ref[0] = _bf(k * m0)
            kbm_ref[1] = _bf(k * m1)
            vb_ref[...] = _bf(v_ref[...])
            dk_acc[...] = jnp.zeros_like(dk_acc)
            dv_acc[...] = jnp.zeros_like(dv_acc)

        q, d_out = q_ref[...], do_ref[...]
        qm = (_bf(q * (m0 * SB_SCALE)), _bf(q * (m1 * SB_SCALE)))
        dom = (_bf(d_out * m0), _bf(d_out * m1))
        lower = _tri(T, lambda r, c: r < c).astype(BF16)
        causal = _tri(T, lambda r, c: c < r)

        def up(js, carry, mask):
            ks = [pl.multiple_of(j * T, T) for j in js]
            tiles = [(b, h) for b in range(len(js)) for h in range(2)]
            zs = [_dg(qm[h], kb_ref[pl.ds(ks[b], T), :], 1, 1) for b, h in tiles]
            a = [a_ref[h, js[b]] for b, h in tiles]
            es = [w.astype(F32) * _dg(dom[h], vb_ref[pl.ds(ks[b], T), :], 1, 1) for w, (b, h) in zip(a, tiles)]
            carries = list(carry)
            for t in range(len(tiles)):
                carries.append(carries[t] + jnp.sum(es[t], axis=1, keepdims=True))
            his = [_bf(e) for e in es]
            los = [_bf(e - hi.astype(F32)) for e, hi in zip(es, his)]
            d_lg = [_dg(hi, lower, 1, 0) for hi in his]
            d_lg = [r + _dg(lo, lower, 1, 0) + c for r, lo, c in zip(d_lg, los, carries)]
            ens = [jnp.exp(-jnp.abs(z)) for z in zs]
            invs = [1.0 / (1.0 + en) for en in ens]
            betas = [jnp.where(z >= 0.0, inv, en * inv) for z, en, inv in zip(zs, ens, invs)]
            dzs = [e * (1.0 - b) - d * b for e, b, d in zip(es, betas, d_lg)]
            if mask is not None:
                dzs = [jnp.where(mask, dz, 0.0) for dz in dzs]
            dzs = [_bf(dz) for dz in dzs]
            parts = [_dg(dzs[t], kbm_ref[h, pl.ds(ks[b], T), :], 1, 0) for t, (b, h) in enumerate(tiles)]
            dq_acc[...] += functools.reduce(lambda u, w: u + w, parts)
            for b, k in enumerate(ks):
                dk_acc[pl.ds(k, T), :] += _dg(dzs[2 * b], qm[0], 0, 0) + _dg(dzs[2 * b + 1], qm[1], 0, 0)
                dv_acc[pl.ds(k, T), :] += _dg(a[2 * b], dom[0], 0, 0) + _dg(a[2 * b + 1], dom[1], 0, 0)
            return tuple(carries[-2:])

        zero = jnp.zeros((T, 1), F32)
        dq_acc[...] = jnp.zeros_like(dq_acc)
        carry = lax.fori_loop(0, i // 4, lambda jj, c: up([4 * jj + b for b in range(4)], c, None), (zero, zero))
        done = 4 * (i // 4)
        carry = lax.fori_loop(0, (i // 2) % 2, lambda _, c: up([done, done + 1], c, None), carry)
        carry = lax.fori_loop(0, i % 2, lambda _, c: up([i - 1], c, None), carry)
        up([i], carry, causal)
        dq_ref[...] = _bf(dq_acc[...] * SB_SCALE)

        @pl.when(i == nq - 1)
        def _():
            dk_ref[...] = _bf(dk_acc[...])
            dv_ref[...] = _bf(dv_acc[...])

    full = lambda c: pl.BlockSpec((S, 128), lambda p, i, c=c: (0, c + p))
    tile = pl.BlockSpec((T, 128), lambda p, i: (i, p))
    acc = pl.BlockSpec((S, 128), lambda p, i: (0, p))
    out = jax.ShapeDtypeStruct((S, SB_W), BF16)
    outs, landed = _pcall(
        body, name="sb_bwd", grid=(4, nq), out_shape=[out, out, out],
        in_specs=[pl.BlockSpec((T, 128), lambda p, i: (i, SB_Q_COL + p)), full(SB_K_COL), full(SB_V_COL), tile,
                  pl.BlockSpec((None, 2, None, nq, T, T), lambda p, i: (p, 0, i, 0, 0, 0))],
        out_specs=[tile, acc, acc],
        scratch_shapes=[pltpu.VMEM((S, 128), BF16), pltpu.VMEM((2, S, 128), BF16), pltpu.VMEM((S, 128), BF16),
                        pltpu.VMEM((T, 128), F32), pltpu.VMEM((S, 128), F32), pltpu.VMEM((S, 128), F32)],
        sem=("arbitrary", "arbitrary"), args=(proj, proj, proj, d_o, a_saved), comm=comm)
    return tuple(outs) if comm is None else (tuple(outs), landed)


SGU_U_COL, SGU_V_COL = 3584 // 512, 4096 // 512


def _causal(w):
    r = lax.broadcasted_iota(jnp.int32, (CHUNK, CHUNK), 0)
    c = lax.broadcasted_iota(jnp.int32, (CHUNK, CHUNK), 1)
    return jnp.where(r >= c, w, 0.0)


def _sgu_fwd(proj, ln_g, ln_b, w, b):
    S = proj.shape[0]
    N = S // CHUNK

    def body(u_ref, v_ref, g_ref, b_ref, w_ref, bias_ref, out_ref):
        u = _gelu(u_ref[...].astype(F32))
        xh, _ = _norm_stats(_gelu(v_ref[...].astype(F32)))
        vn = _bf(xh * g_ref[...] + b_ref[...])
        for g in range(4):
            sl = slice(g * 128, (g + 1) * 128)
            sv = _dg(_bf(_causal(w_ref[g])), vn[:, sl], 1, 0) + bias_ref[g]
            out_ref[:, sl] = u[:, sl] * sv

    vec = pl.BlockSpec((1, 512), lambda n: (0, 0))
    return pl.pallas_call(
        body, name="sgu_fwd", grid=(N,),
        out_shape=jax.ShapeDtypeStruct((S, SGU_W), F32),
        in_specs=[pl.BlockSpec((CHUNK, 512), lambda n: (n, SGU_U_COL)),
                  pl.BlockSpec((CHUNK, 512), lambda n: (n, SGU_V_COL)), vec, vec,
                  pl.BlockSpec((4, CHUNK, CHUNK), lambda n: (0, 0, 0)), pl.BlockSpec((4, CHUNK, 1), lambda n: (0, 0, 0))],
        out_specs=pl.BlockSpec((CHUNK, 512), lambda n: (n, 0)),
        compiler_params=_cparams(("parallel",)),
    )(proj, proj, ln_g, ln_b, w, b)


def _sgu_bwd(proj, ln_g, ln_b, w, b, d_out):
    S = proj.shape[0]
    N = S // CHUNK

    def body(u_ref, v_ref, g_ref, b_ref, w_ref, bias_ref, do_ref, dp_ref, dg_ref, db_ref, dw_ref, dbias_ref):
        @pl.when(pl.program_id(0) == 0)
        def _():
            dg_ref[...] = jnp.zeros_like(dg_ref)
            db_ref[...] = jnp.zeros_like(db_ref)
            dw_ref[...] = jnp.zeros_like(dw_ref)
            dbias_ref[...] = jnp.zeros_like(dbias_ref)

        gu, gv = u_ref[...].astype(F32), v_ref[...].astype(F32)
        u = _gelu(gu)
        xh, rstd = _norm_stats(_gelu(gv))
        ln_gain = g_ref[...]
        vn = _bf(xh * ln_gain + b_ref[...])
        d_o = do_ref[...]
        d_vn = []
        for g in range(4):
            sl = slice(g * 128, (g + 1) * 128)
            wc = _bf(_causal(w_ref[g]))
            sv = _dg(wc, vn[:, sl], 1, 0) + bias_ref[g]
            dp_ref[:, sl] = _bf(d_o[:, sl] * sv * _gelu_grad(gu[:, sl]))
            d_sv = d_o[:, sl] * u[:, sl]
            dbias_ref[g] += jnp.sum(d_sv, axis=1, keepdims=True)
            d_svb = _bf(d_sv)
            dw_ref[g] += _causal(_dg(d_svb, vn[:, sl], 1, 1))
            d_vn.append(_dg(wc, d_svb, 0, 0))
        d_vn = jnp.concatenate(d_vn, axis=1)
        dg_ref[...] += jnp.sum(d_vn * xh, axis=0, keepdims=True)
        db_ref[...] += jnp.sum(d_vn, axis=0, keepdims=True)
        dp_ref[:, 512:1024] = _bf(_norm_bwd(d_vn * ln_gain, xh, rstd) * _gelu_grad(gv))

    vec = pl.BlockSpec((1, 512), lambda n: (0, 0))
    wspec = pl.BlockSpec((4, CHUNK, CHUNK), lambda n: (0, 0, 0))
    bspec = pl.BlockSpec((4, CHUNK, 1), lambda n: (0, 0, 0))
    return pl.pallas_call(
        body, name="sgu_bwd", grid=(N,),
        out_shape=(jax.ShapeDtypeStruct((S, 1024), BF16), jax.ShapeDtypeStruct((1, 512), F32),
                   jax.ShapeDtypeStruct((1, 512), F32), jax.ShapeDtypeStruct((4, CHUNK, CHUNK), F32),
                   jax.ShapeDtypeStruct((4, CHUNK, 1), F32)),
        in_specs=[pl.BlockSpec((CHUNK, 512), lambda n: (n, SGU_U_COL)),
                  pl.BlockSpec((CHUNK, 512), lambda n: (n, SGU_V_COL)), vec, vec, wspec, bspec,
                  pl.BlockSpec((CHUNK, 512), lambda n: (n, 0))],
        out_specs=(pl.BlockSpec((CHUNK, 1024), lambda n: (n, 0)), vec, vec, wspec, bspec),
        compiler_params=_cparams(("arbitrary",)),
    )(proj, proj, ln_g, ln_b, w, b, d_out)


GATE_COL = 4608 // 512


def _merge_fwd(proj, branches, p_list, tm=512):
    S = proj.shape[0]
    tm = min(tm, S)

    def body(r_ref, s_ref, g_ref, pr_ref, ps_ref, pg_ref, gr_ref, gs_ref, gg_ref, m_ref, br_ref):
        acc = None
        for k, (x_ref, p_ref, gate_ref) in enumerate(((r_ref, pr_ref, gr_ref), (s_ref, ps_ref, gs_ref),
                                                      (g_ref, pg_ref, gg_ref))):
            br = _dg(_bf(x_ref[...]), _bf(p_ref[...]), 1, 0)
            br_ref[k] = _bf(br)
            term = _sigmoid(gate_ref[...].astype(F32)) * br
            acc = term if acc is None else acc + term
        m_ref[...] = _bf(acc)

    xs = pl.BlockSpec((tm, 512), lambda i, n: (i, 0))
    ps = pl.BlockSpec((512, 512), lambda i, n: (0, n))
    gate = lambda k: pl.BlockSpec((tm, 512), lambda i, n, k=k: (i, GATE_COL + 2 * k + n))
    return pl.pallas_call(
        body, name="merge_fwd", grid=(S // tm, 2),
        out_shape=(jax.ShapeDtypeStruct((S, D_MODEL), BF16), jax.ShapeDtypeStruct((3, S, D_MODEL), BF16)),
        in_specs=[xs, xs, xs, ps, ps, ps, gate(0), gate(1), gate(2)],
        out_specs=(pl.BlockSpec((tm, 512), lambda i, n: (i, n)), pl.BlockSpec((3, tm, 512), lambda i, n: (0, i, n))),
        compiler_params=_cparams(("parallel", "parallel")),
    )(*branches, *p_list, proj, proj, proj)


def _gate_bwd(proj, br, d_merged, tm=512):
    S = proj.shape[0]
    tm = min(tm, S)

    def body(dm_ref, br_ref, gr_ref, gs_ref, gg_ref, *out_refs):
        dm = dm_ref[...]
        for k, gate_ref in enumerate((gr_ref, gs_ref, gg_ref)):
            s = _sigmoid(gate_ref[...].astype(F32))
            out_refs[k][...] = _bf(dm * s)
            out_refs[3 + k][...] = _bf(dm * br_ref[k].astype(F32) * (s * (1.0 - s)))

    gate = lambda k: pl.BlockSpec((tm, 512), lambda i, n, k=k: (i, GATE_COL + 2 * k + n))
    three = pl.BlockSpec((3, tm, 512), lambda i, n: (0, i, n))
    tile = pl.BlockSpec((tm, 512), lambda i, n: (i, n))
    outs = pl.pallas_call(
        body, name="gate_bwd", grid=(S // tm, 2),
        out_shape=[jax.ShapeDtypeStruct((S, D_MODEL), BF16)] * 6,
        in_specs=[tile, three, gate(0), gate(1), gate(2)], out_specs=[tile] * 6,
        compiler_params=_cparams(("parallel", "parallel")),
    )(d_merged, br, proj, proj, proj)
    return outs[:3], outs[3:]


def _ln_bwd(dy, u, g, target=None, tm=256):
    S, D = u.shape
    tm = min(tm, S)
    loss = target is not None

    def body(*refs):
        dy_ref, u_ref, g_ref = refs[:3]
        du_ref, dub_ref, dg_ref, db_ref = refs[3 + loss:7 + loss]

        @pl.when(pl.program_id(0) == 0)
        def _():
            for acc_ref in refs[5 + loss:]:
                acc_ref[...] = jnp.zeros_like(acc_ref)

        dy_t = dy_ref[...]
        if loss:
            err = dy_t - refs[3][...]
            refs[-1][...] += jnp.sum(err * err, axis=0, keepdims=True)
            dy_t = err * (1.0 / D)
        xh, rstd = _norm_stats(u_ref[...])
        dg_ref[...] += jnp.sum(dy_t * xh, axis=0, keepdims=True)
        db_ref[...] += jnp.sum(dy_t, axis=0, keepdims=True)
        du = _norm_bwd(dy_t * g_ref[...], xh, rstd)
        du_ref[...] = du
        dub_ref[...] = _bf(du)

    tile = pl.BlockSpec((tm, D), lambda i: (i, 0))
    vec = pl.BlockSpec((1, D), lambda i: (0, 0))
    row = jax.ShapeDtypeStruct((1, D), F32)
    return pl.pallas_call(
        body, name="ln_bwd", grid=(S // tm,),
        out_shape=[jax.ShapeDtypeStruct((S, D), F32), jax.ShapeDtypeStruct((S, D), BF16)] + [row] * (2 + loss),
        in_specs=[tile, tile, vec] + [tile] * loss, out_specs=[tile, tile] + [vec] * (2 + loss),
        compiler_params=_cparams(("arbitrary",)),
    )(dy, u, g, *([target] if loss else []))


def _layer_fwd(x, x_bf, W, tables, sb_comm=None):
    proj = _matmul(x_bf, W["w_in_t"], "nt", name="proj", tm=1024, tn=768, tk=1024, out_dtype=BF16)
    retg, states = _ret_fwd(proj, tables, W["ret_gn_g"], W["ret_gn_b"])
    if sb_comm is None:
        sb, sb_a = _sb_fwd(proj)
    else:
        (sb, sb_a), landed = _sb_fwd(proj, comm=sb_comm[0])
        sb_comm[1](landed)
    sg = _sgu_fwd(proj, W["sgu_ln_g"], W["sgu_ln_b"], W["sgu_w"], W["sgu_b"])
    merged, br = _merge_fwd(proj, (retg, sb, sg), (W["p_ret"], W["p_sb"], W["p_sgu"]))
    u1, x1, x1_bf = _matmul(merged, W["w_out"], "nn", name="out_ln", tm=512, tn=1024, tk=1024, epi="ln",
                            extra=(x, W["ln1_g"], W["ln1_b"]))
    act = _matmul(x1_bf, W["w_up"], "nn", name="up", tm=1024, tn=1024, tk=1024, epi="relu2")
    u2, x2, x2_bf = _matmul(act, W["w_down"], "nn", name="down_ln", tm=512, tn=1024, tk=4096, epi="ln",
                            extra=(x1, W["ln2_g"], W["ln2_b"]))
    saved = dict(x_bf=x_bf, proj=proj, retg=retg, states=states, sb=sb, sb_a=sb_a, sg=sg, merged=merged, br=br, u1=u1,
                 x1_bf=x1_bf, act=act, u2=u2)
    return x2, x2_bf, saved


def _layer_bwd(d_x2, W, tables, sv, chunk_dtype=None, sb_comm_fn=None, dx_comm_fn=None, target=None):
    dt = F32 if chunk_dtype is None else chunk_dtype
    rows, cols = (None, None) if chunk_dtype is None else ("rows", "cols")
    g, landed = {}, {}
    du2, du2_bf, g["ln2_g"], g["ln2_b"], *sq = _ln_bwd(d_x2, sv["u2"], W["ln2_g"], target=target)
    if sq:
        landed["sq"] = sq[0]
    d_hpre = _matmul(du2_bf, W["w_down"], "nt", name="d_act", tm=1024, tn=1024, tk=1024, epi="drelu2",
                     extra=(sv["act"],), out_dtype=BF16)
    g["w_down"] = _matmul(sv["act"], du2_bf, "tn", name="dw_down", tm=512, tn=1024, tk=4096, out_dtype=dt, chunks=rows)
    g["w_up"] = _matmul(sv["x1_bf"], d_hpre, "tn", name="dw_up", tm=1024, tn=512, tk=4096, out_dtype=dt, chunks=cols)
    d_x1 = _matmul(d_hpre, W["w_up"], "nt", name="d_x1", tm=512, tn=1024, tk=4096, epi="add", extra=(du2,))
    du1, du1_bf, g["ln1_g"], g["ln1_b"] = _ln_bwd(d_x1, sv["u1"], W["ln1_g"])
    d_merged = _matmul(du1_bf, W["w_out"], "nt", name="d_merged", tm=1024, tn=1024, tk=1024)
    g["w_out"] = _matmul(sv["merged"], du1_bf, "tn", name="dw_out", tm=1024, tn=512, tk=4096, out_dtype=dt, chunks=rows)
    d_br, d_gate = _gate_bwd(sv["proj"], sv["br"], d_merged)
    d_branch = []
    for k, (nm, act) in enumerate((("p_ret", sv["retg"]), ("p_sb", sv["sb"]), ("p_sgu", sv["sg"]))):
        d_branch.append(_matmul(d_br[k], W[nm], "nt", name="d_" + nm[2:], tm=1024, tn=512, tk=1024))
        g[nm] = _matmul(act, d_br[k], "tn", name="dw_" + nm[2:], tm=512, tn=1024, tk=2048, out_dtype=dt, chunks=cols)
    d_ret, g["ret_gn_g"], g["ret_gn_b"] = _ret_bwd(sv["proj"], tables, W["ret_gn_g"], W["ret_gn_b"], sv["states"],
                                                   d_branch[0])
    if sb_comm_fn is None:
        d_sq, d_sk, d_sv = _sb_bwd(sv["proj"], sv["sb_a"], d_branch[1])
    else:
        (d_sq, d_sk, d_sv), landed["sb"] = _sb_bwd(sv["proj"], sv["sb_a"], d_branch[1], comm=sb_comm_fn(g))
    d_sgu, g["sgu_ln_g"], g["sgu_ln_b"], g["sgu_w"], g["sgu_b"] = _sgu_bwd(
        sv["proj"], W["sgu_ln_g"], W["sgu_ln_b"], W["sgu_w"], W["sgu_b"], d_branch[2])
    d_proj = [d_ret, d_sq, d_sk, d_sv, d_sgu, d_gate[0], d_gate[1], d_gate[2]]
    g["w_in"] = _matmul(d_proj, sv["x_bf"], "tn", name="dw_in", tm=256, tn=1024, tk=4096, out_dtype=dt, chunks=rows)
    if chunk_dtype is None:
        g["w_in"] = g["w_in"].T
    d_x = _matmul_rows_of(d_proj, W["w_in_t"], du1, name="d_x", tm=512,
                          comm=None if dx_comm_fn is None else dx_comm_fn(g))
    if dx_comm_fn is not None:
        d_x, landed["dx"] = d_x
    return d_x, g, landed


BIG = ("w_in", "p_ret", "p_sb", "p_sgu", "w_out", "w_up", "w_down")
SMALL = ("ret_gn_g", "ret_gn_b", "sgu_ln_g", "sgu_ln_b", "sgu_w", "sgu_b", "ln1_g", "ln1_b", "ln2_g", "ln2_b")
GATHER_KIND = {"w_in": "rows", "p_ret": "cols", "p_sb": "cols", "p_sgu": "cols", "w_out": "rows", "w_up": "cols",
               "w_down": "rows"}


def _small_weights(small, l):
    W = {}
    for n in SMALL:
        if n == "sgu_w":
            W[n] = small[n][l]
        elif n == "sgu_b":
            W[n] = small[n][l].reshape(4, CHUNK, 1)
        else:
            W[n] = small[n][l].reshape(1, -1)
    return W


def _local_step(x, target, full, small):
    tables = _ret_tables(x.shape[0])
    Ws = [{**{n: full[n][l] for n in BIG[1:]}, "w_in_t": full["w_in"][l].T, **_small_weights(small, l)}
          for l in range(DEPTH)]
    saved = []
    h, h_bf = x, _bf(x)
    for l in range(DEPTH):
        h, h_bf, sv = _layer_fwd(h, h_bf, Ws[l], tables)
        saved.append(sv)
    grads = [None] * DEPTH
    d_h, grads[-1], landed = _layer_bwd(h, Ws[-1], tables, saved[-1], target=target)
    for l in reversed(range(DEPTH - 1)):
        d_h, grads[l], _ = _layer_bwd(d_h, Ws[l], tables, saved[l])
    return landed["sq"], d_h, grads


def _adam(w, parts, m, v, name):
    L, R, C = w.shape
    tr = next(t for t in (320, 256, 128) if R % t == 0)
    assert len(parts) == L

    def body(*refs):
        w_ref, p_refs, (m_ref, v_ref, g_ref, d_ref, nm_ref, nv_ref) = refs[0], refs[1:1 + L], refs[1 + L:]
        layer = pl.program_id(0)
        g = None
        for li, p_ref in enumerate(p_refs):
            s = p_ref[0].astype(F32)
            for j in range(1, p_ref.shape[0]):
                s = s + p_ref[j].astype(F32)
            g = s if g is None else jnp.where(layer == li, s, g)
        g_ref[...] = g
        d_ref[...], nm_ref[...], nv_ref[...] = _adam_update(w_ref[...], g, m_ref[...], v_ref[...])

    tile = pl.BlockSpec((None, tr, C), lambda l, i: (l, i, 0))
    part = lambda li: pl.BlockSpec((parts[li].shape[0], tr, C), lambda l, i, li=li: (0, jnp.where(l == li, i, 0), 0))
    out = jax.ShapeDtypeStruct((L, R, C), F32)
    return pl.pallas_call(
        body, name=name, grid=(L, R // tr), out_shape=(out, out, out, out),
        in_specs=[tile] + [part(li) for li in range(L)] + [tile, tile],
        out_specs=(tile, tile, tile, tile),
        compiler_params=_cparams(("parallel", "parallel")),
    )(w, *parts, m, v)


def _adam_update(w, g, m, v):
    m2 = ADAM_B1 * m + (1.0 - ADAM_B1) * g
    v2 = ADAM_B2 * v + (1.0 - ADAM_B2) * (g * g)
    m_hat = m2 / (1.0 - ADAM_B1 ** ADAM_STEP)
    v_hat = v2 / (1.0 - ADAM_B2 ** ADAM_STEP)
    return -ADAM_LR * (m_hat / (jnp.sqrt(v_hat) + ADAM_EPS) + ADAM_WD * w), m2, v2


def _adam_small(w, m, v, parts):
    k = len(SMALL)

    def body(*refs):
        w_refs, m_refs, v_refs, p_refs, outs = refs[:k], refs[k:2 * k], refs[2 * k:3 * k], refs[3 * k:5 * k], refs[5 * k:]
        for i in range(k):
            vector = len(w_refs[i].shape) == 2
            for l in range(DEPTH):
                p_ref = p_refs[DEPTH * i + l]
                g = p_ref[0]
                for j in range(1, N_DEV):
                    g = g + p_ref[j]
                at = (slice(l, l + 1), slice(None)) if vector else (l,)
                delta, m2, v2 = _adam_update(w_refs[i][at], g, m_refs[i][at], v_refs[i][at])
                for o_ref, val in zip(outs[4 * i:4 * i + 4], (g, delta, m2, v2)):
                    o_ref[at] = val

    vmem = pl.BlockSpec(memory_space=pltpu.VMEM)
    args = [w[n] for n in SMALL] + [m[n] for n in SMALL] + [v[n] for n in SMALL] + \
           [parts[(n, l)] for n in SMALL for l in range(DEPTH)]
    out_shape = [jax.ShapeDtypeStruct(w[n].shape, F32) for n in SMALL for _ in range(4)]
    outs = pl.pallas_call(body, name="adam_small", out_shape=out_shape, in_specs=[vmem] * len(args),
                          out_specs=[vmem] * len(out_shape), compiler_params=_cparams())(*args)
    return {n: tuple(outs[4 * i:4 * i + 4]) for i, n in enumerate(SMALL)}


WEIGHTS = ("w_in", "ret_gn_g", "ret_gn_b", "sgu_ln_g", "sgu_ln_b", "sgu_w", "sgu_b", "p_ret", "p_sb", "p_sgu", "w_out",
           "ln1_g", "ln1_b", "w_up", "w_down", "ln2_g", "ln2_b")


def kernel(x, w_in, ret_gn_g, ret_gn_b, sgu_ln_g, sgu_ln_b, sgu_w, sgu_b, p_ret, p_sb, p_sgu, w_out, ln1_g, ln1_b, w_up, w_down, ln2_g, ln2_b, loss_target, m_w_in, m_ret_gn_g, m_ret_gn_b, m_sgu_ln_g, m_sgu_ln_b, m_sgu_w, m_sgu_b, m_p_ret, m_p_sb, m_p_sgu, m_w_out, m_ln1_g, m_ln1_b, m_w_up, m_w_down, m_ln2_g, m_ln2_b, v_w_in, v_ret_gn_g, v_ret_gn_b, v_sgu_ln_g, v_sgu_ln_b, v_sgu_w, v_sgu_b, v_p_ret, v_p_sb, v_p_sgu, v_w_out, v_ln1_g, v_ln1_b, v_w_up, v_w_down, v_ln2_g, v_ln2_b):
    w = dict(zip(WEIGHTS, (w_in, ret_gn_g, ret_gn_b, sgu_ln_g, sgu_ln_b, sgu_w, sgu_b, p_ret, p_sb, p_sgu, w_out,
                           ln1_g, ln1_b, w_up, w_down, ln2_g, ln2_b)))
    m = dict(zip(WEIGHTS, (m_w_in, m_ret_gn_g, m_ret_gn_b, m_sgu_ln_g, m_sgu_ln_b, m_sgu_w, m_sgu_b, m_p_ret, m_p_sb,
                           m_p_sgu, m_w_out, m_ln1_g, m_ln1_b, m_w_up, m_w_down, m_ln2_g, m_ln2_b)))
    v = dict(zip(WEIGHTS, (v_w_in, v_ret_gn_g, v_ret_gn_b, v_sgu_ln_g, v_sgu_ln_b, v_sgu_w, v_sgu_b, v_p_ret, v_p_sb,
                           v_p_sgu, v_w_out, v_ln1_g, v_ln1_b, v_w_up, v_w_down, v_ln2_g, v_ln2_b)))

    small = {n: w[n] for n in SMALL}
    shard = {n: _bf(w[n]) for n in BIG}
    shard["w_in"] = shard["w_in"].transpose(0, 2, 1)
    S = x.shape[1]
    x0, target = x.reshape(S, D_MODEL), loss_target.reshape(S, D_MODEL)
    tables = _ret_tables(S)
    Ws = [_small_weights(small, l) for l in range(DEPTH)]

    (Ws[0]["w_in_t"],) = _exchange([_gather_transfer(shard["w_in"], 0, "rows")], "gather_w_in0", relay=True)
    later = [(n, 0) for n in BIG[1:]] + [(n, 1) for n in BIG]

    def weights_landed(landed):
        for (n, l), z in zip(later, landed):
            Ws[l]["w_in_t" if n == "w_in" else n] = z

    gather = _Comm([_gather_transfer(shard[n], l, GATHER_KIND[n]) for n, l in later], relay=True)
    h, h_bf, saved0 = _layer_fwd(x0, _bf(x0), Ws[0], tables, sb_comm=(gather, weights_landed))
    h, _, saved1 = _layer_fwd(h, h_bf, Ws[1], tables)
    d_h, g1, landed1 = _layer_bwd(h, Ws[1], tables, saved1, chunk_dtype=BF16, target=target,
                                  sb_comm_fn=lambda g: _Comm([_scatter_transfer(g[n]) for n in BIG[1:]]))
    loss = lax.psum(0.5 * jnp.sum(landed1["sq"]) / D_MODEL, ("x", "y", "c"))
    early = [("w_in", 1)] + [(n, 0) for n in BIG[1:]]

    def small_slabs(g):
        return [_slab_transfer(g[n].reshape(4, CHUNK) if n == "sgu_b" else g[n]) for n in SMALL]

    def early_scatter(g0):
        return _Comm([_scatter_transfer((g1 if l else g0)[n]) for n, l in early] + small_slabs(g1))

    def late_scatter(g0):
        pairs = _pair_reduce(g0["w_in"], "w_in0_pairs")
        return _Comm([_chip_scatter_transfer(pairs)] + small_slabs(g0))

    d_x, g0, landed = _layer_bwd(d_h, Ws[0], tables, saved0, chunk_dtype=BF16, sb_comm_fn=early_scatter,
                                 dx_comm_fn=late_scatter)
    parts = {**dict(zip(early, landed["sb"])), **{(n, 1): z for n, z in zip(BIG[1:], landed1["sb"])}}
    parts[("w_in", 0)] = landed["dx"][0]
    small_parts = {**{(n, 1): z for n, z in zip(SMALL, landed["sb"][len(early):])},
                   **{(n, 0): z for n, z in zip(SMALL, landed["dx"][1:])}}

    grad, delta, new_m, new_v = {}, {}, {}, {}
    for n in BIG:
        view = (lambda a: a.transpose(0, 2, 1)) if n == "w_in" else (lambda a: a)
        res = _adam(view(w[n]), [parts[(n, l)] for l in range(DEPTH)], view(m[n]), view(v[n]), "adam_" + n)
        grad[n], delta[n], new_m[n], new_v[n] = (view(r) for r in res)
    for n, res in _adam_small(small, m, v, small_parts).items():
        grad[n], delta[n], new_m[n], new_v[n] = res

    return (loss, d_x.reshape(x.shape), *[grad[n] for n in WEIGHTS], *[delta[n] for n in WEIGHTS],
            *[new_m[n] for n in WEIGHTS], *[new_v[n] for n in WEIGHTS])
```

```python
import functools
import math

import numpy as np
import jax
import jax.numpy as jnp
from jax import lax
from jax.experimental import pallas as pl
from jax.experimental.pallas import tpu as pltpu

F32 = jnp.float32
BF16 = jnp.bfloat16

N_DEV = 8
DEPTH = 2
D_MODEL = 1024
CHUNK = 128
RET_W = 512
SB_W = 512
SGU_W = 512
N_IN = 7680
LN_EPS = 1e-5
ALPHA = (2 * DEPTH) ** 0.25
ROPE_BASE = 10000.0
ADAM_LR, ADAM_B1, ADAM_B2, ADAM_EPS, ADAM_WD, ADAM_STEP = 0.001, 0.9, 0.999, 1e-08, 0.01, 10
VMEM_LIMIT = 56 * 1024 * 1024

_GELU_K = math.sqrt(2.0 / math.pi)
_GELU_C = 0.044715


def _cparams(sem=None):
    return pltpu.CompilerParams(dimension_semantics=sem, vmem_limit_bytes=VMEM_LIMIT)


def _dg(a, b, ca, cb):
    return lax.dot_general(a, b, (((ca,), (cb,)), ((), ())), preferred_element_type=F32)


def _bf(x):
    return x.astype(BF16)


def _sigmoid(x):
    return 1.0 / (1.0 + jnp.exp(-x))


def _gelu(x):
    t = jnp.tanh(_GELU_K * (x + _GELU_C * (x * x * x)))
    return x * (0.5 * (1.0 + t))


def _gelu_grad(x):
    t = jnp.tanh(_GELU_K * (x + _GELU_C * (x * x * x)))
    return 0.5 * (1.0 + t) + 0.5 * x * (1.0 - t * t) * (_GELU_K * (1.0 + 3.0 * _GELU_C * x * x))


def _norm_stats(u):
    mu = jnp.mean(u, axis=-1, keepdims=True)
    d = u - mu
    var = jnp.mean(d * d, axis=-1, keepdims=True)
    rstd = lax.rsqrt(var + LN_EPS)
    return d * rstd, rstd


def _norm_bwd(dxh, xh, rstd):
    return rstd * (dxh - jnp.mean(dxh, axis=-1, keepdims=True) - xh * jnp.mean(dxh * xh, axis=-1, keepdims=True))


class _Transfer:
    def __init__(self, src, dst_shape, src_at, dst_at, same_core=False):
        self.src, self.dst_shape, self.src_at, self.dst_at = src, tuple(dst_shape), src_at, dst_at
        self.same_core = same_core


def _gather_transfer(shard, l, kind):
    _, r, c = shard.shape
    src_at = lambda ref, p: ref.at[l]
    if kind == "slab":
        return _Transfer(shard, (N_DEV, r, c), src_at, lambda ref, s: ref.at[s])
    if kind == "rows":
        return _Transfer(shard, (N_DEV * r, c), src_at, lambda ref, s: ref.at[pl.ds(pl.multiple_of(s * r, r), r), :])
    return _Transfer(shard, (r, N_DEV * c), src_at, lambda ref, s: ref.at[:, pl.ds(pl.multiple_of(s * c, c), c)])


def _scatter_transfer(chunks):
    return _Transfer(chunks, chunks.shape, lambda ref, p: ref.at[p], lambda ref, s: ref.at[s])


def _slab_transfer(arr):
    return _Transfer(arr, (N_DEV,) + arr.shape, lambda ref, p: ref, lambda ref, s: ref.at[s])


class _Comm:
    def __init__(self, transfers, relay=False):
        self.transfers = list(transfers)
        self.relay = relay
        self.n = len(self.transfers)
        self.arrays = [t.src for t in self.transfers]
        self.out_shape = [jax.ShapeDtypeStruct(t.dst_shape, t.src.dtype) for t in self.transfers]
        self.scratch = [pltpu.SemaphoreType.DMA((self.n * (N_DEV - 1),)), pltpu.SemaphoreType.DMA((self.n * (N_DEV - 1),)),
                        pltpu.SemaphoreType.DMA((self.n,))]

    def _relay_copies(self, srcs, dsts, send_sems, recv_sems, local_sems):
        x, y, c = lax.axis_index("x"), lax.axis_index("y"), lax.axis_index("c")
        me = 4 * x + 2 * y + c
        chips = [(1 - x, y), (x, 1 - y), (1 - x, 1 - y)]
        first, passed, own = [], [], []
        for t, tr in enumerate(self.transfers):
            def copy(k, src, sender, to, t=t, tr=tr):
                return pltpu.make_async_remote_copy(
                    src_ref=src, dst_ref=tr.dst_at(dsts[t], sender), send_sem=send_sems.at[t * (N_DEV - 1) + k],
                    recv_sem=recv_sems.at[t * (N_DEV - 1) + k], device_id=to, device_id_type=pl.DeviceIdType.MESH)

            mine = tr.src_at(srcs[t], me)
            first.append([copy(0, mine, me, (x, y, 1 - c))] + [copy(1 + j, mine, me, (px, py, c))
                                                                for j, (px, py) in enumerate(chips)])
            passed.append([copy(4 + j, tr.dst_at(dsts[t], 4 * px + 2 * py + c), 4 * px + 2 * py + c, (x, y, 1 - c))
                           for j, (px, py) in enumerate(chips)])
            own.append(pltpu.make_async_copy(mine, tr.dst_at(dsts[t], me), local_sems.at[t]))
        return first, passed, own

    def _copies(self, srcs, dsts, send_sems, recv_sems, local_sems):
        x, y, c = lax.axis_index("x"), lax.axis_index("y"), lax.axis_index("c")
        me = 4 * x + 2 * y + c
        copies = []
        for d in range(1, N_DEV):
            px = 1 - x if d & 4 else x
            py = 1 - y if d & 2 else y
            pc = 1 - c if d & 1 else c
            for t, tr in enumerate(self.transfers):
                if tr.same_core and d & 1:
                    continue
                peer, mine = (2 * px + py, 2 * x + y) if tr.same_core else (4 * px + 2 * py + pc, me)
                k = t * (N_DEV - 1) + d - 1
                copies.append(pltpu.make_async_remote_copy(
                    src_ref=tr.src_at(srcs[t], peer), dst_ref=tr.dst_at(dsts[t], mine),
                    send_sem=send_sems.at[k], recv_sem=recv_sems.at[k],
                    device_id=(px, py, pc), device_id_type=pl.DeviceIdType.MESH))
        own = []
        for t, tr in enumerate(self.transfers):
            mine = 2 * x + y if tr.same_core else me
            own.append(pltpu.make_async_copy(tr.src_at(srcs[t], mine), tr.dst_at(dsts[t], mine), local_sems.at[t]))
        return copies, own

    def start(self, srcs, dsts, *sems):
        if self.relay:
            first, _, own = self._relay_copies(srcs, dsts, *sems)
            for cp in own + [cp for per_t in first for cp in per_t]:
                cp.start()
            return
        copies, own = self._copies(srcs, dsts, *sems)
        for cp in own + copies:
            cp.start()

    def finish(self, srcs, dsts, *sems):
        if self.relay:
            first, passed, own = self._relay_copies(srcs, dsts, *sems)
            for j in range(3):
                for t in range(self.n):
                    first[t][1 + j].wait_recv()
                    passed[t][j].start()
            for t in range(self.n):
                first[t][0].wait_recv()
                for cp in passed[t]:
                    cp.wait_recv()
            for t in range(self.n):
                for cp in first[t] + passed[t]:
                    cp.wait_send()
                own[t].wait()
            return
        copies, own = self._copies(srcs, dsts, *sems)
        for cp in copies + own:
            cp.wait()


def _pcall(body, *, name, grid, in_specs, out_specs, out_shape, scratch_shapes, sem, args, comm=None):
    in_specs, out_specs, out_shape = list(in_specs), list(out_specs), list(out_shape)
    if comm is None:
        outs = pl.pallas_call(body, name=name, grid=grid, in_specs=in_specs, out_specs=out_specs, out_shape=out_shape,
                              scratch_shapes=list(scratch_shapes), compiler_params=_cparams(sem))(*args)
        return list(outs), []
    n_in, n_out, n_scr, k = len(in_specs), len(out_specs), len(scratch_shapes), comm.n

    def carrier(*refs):
        ins, cin = refs[:n_in], refs[n_in:n_in + k]
        outs, cout = refs[n_in + k:n_in + k + n_out], refs[n_in + k + n_out:n_in + 2 * k + n_out]
        scr, sems = refs[n_in + 2 * k + n_out:n_in + 2 * k + n_out + n_scr], refs[n_in + 2 * k + n_out + n_scr:]
        ids = [pl.program_id(d) for d in range(len(grid))]
        first = functools.reduce(jnp.logical_and, [i == 0 for i in ids])
        last = functools.reduce(jnp.logical_and, [i == g - 1 for i, g in zip(ids, grid)])

        @pl.when(first)
        def _():
            comm.start(cin, cout, *sems)

        body(*ins, *outs, *scr)

        @pl.when(last)
        def _():
            comm.finish(cin, cout, *sems)

    hbm = pl.BlockSpec(memory_space=pl.ANY)
    outs = pl.pallas_call(
        carrier, name=name, grid=grid, in_specs=in_specs + [hbm] * k, out_specs=out_specs + [hbm] * k,
        out_shape=out_shape + comm.out_shape, scratch_shapes=list(scratch_shapes) + comm.scratch,
        compiler_params=_cparams(tuple("arbitrary" for _ in grid)),
    )(*args, *comm.arrays)
    return list(outs[:n_out]), list(outs[n_out:])


def _exchange(transfers, name, relay=False):
    comm = _Comm(transfers, relay)

    def body(*refs):
        k = comm.n
        comm.start(refs[:k], refs[k:2 * k], *refs[2 * k:])
        comm.finish(refs[:k], refs[k:2 * k], *refs[2 * k:])

    hbm = pl.BlockSpec(memory_space=pl.ANY)
    return pl.pallas_call(body, name=name, out_shape=comm.out_shape, in_specs=[hbm] * comm.n, out_specs=[hbm] * comm.n,
                          scratch_shapes=comm.scratch)(*comm.arrays)


def _pair_reduce(chunks, name, tr=320):
    _, r, c = chunks.shape
    tr = min(tr, r)
    assert r % tr == 0

    def swap(src_ref, dst_ref, send_sems, recv_sems):
        x, y, core = lax.axis_index("x"), lax.axis_index("y"), lax.axis_index("c")
        copies = [pltpu.make_async_remote_copy(
            src_ref=src_ref.at[2 * k + 1 - core], dst_ref=dst_ref.at[k], send_sem=send_sems.at[k],
            recv_sem=recv_sems.at[k], device_id=(x, y, 1 - core), device_id_type=pl.DeviceIdType.MESH) for k in range(4)]
        for cp in copies:
            cp.start()
        for cp in copies:
            cp.wait()

    hbm = pl.BlockSpec(memory_space=pl.ANY)
    theirs = pl.pallas_call(swap, name=name + "_swap", out_shape=jax.ShapeDtypeStruct((4, r, c), chunks.dtype),
                            in_specs=[hbm], out_specs=hbm,
                            scratch_shapes=[pltpu.SemaphoreType.DMA((4,)), pltpu.SemaphoreType.DMA((4,))])(chunks)

    def add(mine_ref, theirs_ref, out_ref):
        core = lax.axis_index("c")
        both = mine_ref[...].astype(F32)
        out_ref[...] = (jnp.where(core == 0, both[0], both[1]) + theirs_ref[...].astype(F32)).astype(out_ref.dtype)

    return pl.pallas_call(
        add, name=name + "_add", grid=(4, r // tr), out_shape=jax.ShapeDtypeStruct((4, r, c), chunks.dtype),
        in_specs=[pl.BlockSpec((None, 2, tr, c), lambda k, i: (k, 0, i, 0)), pl.BlockSpec((None, tr, c), lambda k, i: (k, i, 0))],
        out_specs=pl.BlockSpec((None, tr, c), lambda k, i: (k, i, 0)),
        compiler_params=_cparams(("parallel", "parallel")),
    )(chunks.reshape(4, 2, r, c), theirs)


def _chip_scatter_transfer(pairs):
    return _Transfer(pairs, pairs.shape, lambda ref, p: ref.at[p], lambda ref, s: ref.at[s], same_core=True)


def _matmul(a, b, mode, *, name, tm, tn, tk, epi=None, extra=(), out_dtype=F32, chunks=None, comm=None):
    pieces = list(a) if isinstance(a, (list, tuple)) else [a]
    rows_a, cols_a = pieces[0].shape[0], sum(p.shape[1] for p in pieces)
    if mode == "nn":
        (M, K), N = (rows_a, cols_a), b.shape[1]
    elif mode == "nt":
        (M, K), N = (rows_a, cols_a), b.shape[0]
    else:
        (K, M), N = (rows_a, cols_a), b.shape[1]
    tm, tn, tk = min(tm, M), min(tn, N), min(tk, K)
    assert M % tm == 0 and N % tn == 0 and K % tk == 0 and (epi != "ln" or tn == N), (name, M, N, K)
    nk = K // tk
    tile_cols, axis = (tm, 0) if mode == "tn" else (tk, 2)
    assert all(p.shape[1] % tile_cols == 0 for p in pieces)
    counts = [p.shape[1] // tile_cols for p in pieces]
    starts = [sum(counts[:q]) for q in range(len(pieces))]

    def a_spec_of(q):
        at = lambda t: jnp.clip(t - starts[q], 0, counts[q] - 1) if len(pieces) > 1 else t
        return {"nn": pl.BlockSpec((tm, tk), lambda i, j, k: (i, at(k))),
                "nt": pl.BlockSpec((tm, tk), lambda i, j, k: (i, at(k))),
                "tn": pl.BlockSpec((tk, tm), lambda i, j, k: (k, at(i)))}[mode]

    n_a = len(pieces)
    b_mode = pl.Buffered(1) if (nk == 1 and tn == N and n_a > 1) else None
    b_spec = {"nn": pl.BlockSpec((tk, tn), lambda i, j, k: (k, j), pipeline_mode=b_mode),
              "nt": pl.BlockSpec((tn, tk), lambda i, j, k: (j, k), pipeline_mode=b_mode),
              "tn": pl.BlockSpec((tk, tn), lambda i, j, k: (k, j), pipeline_mode=b_mode)}[mode]
    ca, cb = {"nn": (1, 0), "nt": (1, 1), "tn": (0, 0)}[mode]
    tile = pl.BlockSpec((tm, tn), lambda i, j, k: (i, j))
    row = pl.BlockSpec((1, tn), lambda i, j, k: (0, j))
    n_extra = {None: 0, "add": 1, "relu2": 0, "drelu2": 1, "ln": 3}[epi]
    assert len(extra) == n_extra
    extra_specs = {None: [], "add": [tile], "relu2": [], "drelu2": [tile], "ln": [tile, row, row]}[epi]
    split = 0
    if epi == "relu2":
        out_shape, out_specs = (jax.ShapeDtypeStruct((M, N), BF16),), (tile,)
    elif epi == "ln":
        out_shape = (jax.ShapeDtypeStruct((M, N), F32), jax.ShapeDtypeStruct((M, N), F32),
                     jax.ShapeDtypeStruct((M, N), BF16))
        out_specs = (tile, tile, tile)
    elif chunks == "cols":
        c = N // N_DEV
        out_shape = (jax.ShapeDtypeStruct((N_DEV, M, c), out_dtype),)
        if tn == N:
            split = c
            out_specs = (pl.BlockSpec((N_DEV, tm, c), lambda i, j, k: (0, i, 0)),)
        else:
            assert c % tn == 0
            out_specs = (pl.BlockSpec((None, tm, tn), lambda i, j, k: (j // (c // tn), i, j % (c // tn))),)
    else:
        out_shape, out_specs = (jax.ShapeDtypeStruct((M, N), out_dtype),), (tile,)
    n_out = len(out_shape)

    def body(*refs):
        a_refs, b_ref = refs[:n_a], refs[n_a]
        ex = refs[n_a + 1:n_a + 1 + n_extra]
        outs = refs[n_a + 1 + n_extra:n_a + 1 + n_extra + n_out]
        acc_ref = refs[-1]
        k = pl.program_id(2)

        def finish(acc):
            if epi == "add":
                outs[0][...] = (acc + ALPHA * ex[0][...]).astype(out_dtype)
            elif epi == "relu2":
                r = jnp.maximum(acc, 0.0)
                outs[0][...] = _bf(r * r)
            elif epi == "drelu2":
                outs[0][...] = (acc * (2.0 * jnp.sqrt(ex[0][...].astype(F32)))).astype(out_dtype)
            elif epi == "ln":
                u = ALPHA * ex[0][...] + acc
                xh, _ = _norm_stats(u)
                y = xh * ex[1][...] + ex[2][...]
                outs[0][...] = u
                outs[1][...] = y
                outs[2][...] = _bf(y)
            elif split:
                for p in range(N_DEV):
                    outs[0][p] = acc[:, p * split:(p + 1) * split].astype(out_dtype)
            else:
                outs[0][...] = acc.astype(out_dtype)

        def step(a_ref, first, middle, last):
            part = _dg(_bf(a_ref[...]), _bf(b_ref[...]), ca, cb)
            if nk == 1:
                finish(part)
                return
            if first:
                @pl.when(k == 0)
                def _():
                    acc_ref[...] = part

            if middle:
                @pl.when(jnp.logical_and(k > 0, k < nk - 1))
                def _():
                    acc_ref[...] += part

            if last:
                @pl.when(k == nk - 1)
                def _():
                    finish(acc_ref[...] + part)

        if n_a == 1:
            step(a_refs[0], True, True, True)
        else:
            t = pl.program_id(axis)
            for q in range(n_a):
                along_k = axis == 2
                first = not along_k or starts[q] == 0
                last = not along_k or starts[q] + counts[q] == nk
                middle = not along_k or counts[q] > int(first) + int(last)

                @pl.when(jnp.logical_and(t >= starts[q], t < starts[q] + counts[q]))
                def _(q=q, first=first, middle=middle, last=last):
                    step(a_refs[q], first, middle, last)

    outs, landed = _pcall(
        body, name=name, out_shape=out_shape, grid=(M // tm, N // tn, nk),
        in_specs=[a_spec_of(q) for q in range(n_a)] + [b_spec] + extra_specs, out_specs=out_specs,
        scratch_shapes=[pltpu.VMEM((tm, tn) if nk > 1 else (8, 128), F32)], sem=("parallel", "parallel", "arbitrary"),
        args=(*pieces, b, *extra), comm=comm)
    res = outs[0] if n_out == 1 else tuple(outs)
    if chunks == "rows":
        res = res.reshape(N_DEV, M // N_DEV, N)
    return res if comm is None else (res, landed)


def _matmul_rows_of(pieces, b, res, *, name, tm, comm=None):
    M, (K, N) = pieces[0].shape[0], b.shape
    tm = min(tm, M)
    subs, start = [], 0
    for q, p in enumerate(pieces):
        w = p.shape[1]
        step = w if start % w == 0 else 512
        assert w % step == 0 and start % step == 0
        subs += [(q, off, step, start + off) for off in range(0, w, step)]
        start += w
    assert start == K
    n_p, n_s = len(pieces), len(subs)

    def body(*refs):
        a_refs, b_refs, res_ref, out_ref = refs[:n_p], refs[n_p:n_p + n_s], refs[n_p + n_s], refs[n_p + n_s + 1]
        acc = None
        for (q, off, w, _), b_ref in zip(subs, b_refs):
            part = _dg(_bf(a_refs[q][:, off:off + w]), _bf(b_ref[...]), 1, 0)
            acc = part if acc is None else acc + part
        out_ref[...] = acc + ALPHA * res_ref[...]

    tile = pl.BlockSpec((tm, N), lambda i: (i, 0))
    outs, landed = _pcall(
        body, name=name, grid=(M // tm,), out_shape=[jax.ShapeDtypeStruct((M, N), F32)],
        in_specs=[pl.BlockSpec((tm, p.shape[1]), lambda i: (i, 0)) for p in pieces] +
                 [pl.BlockSpec((w, N), lambda i, r=row // w: (r, 0), pipeline_mode=pl.Buffered(1))
                  for _, _, w, row in subs] + [tile],
        out_specs=[tile], scratch_shapes=[], sem=("parallel",), args=(*pieces, *([b] * n_s), res), comm=comm)
    return outs[0] if comm is None else (outs[0], landed)


def _ret_tables(S):
    half = 64
    inv_freq = ROPE_BASE ** (-jnp.arange(half, dtype=F32) / half)
    ang = jnp.arange(S, dtype=jnp.int32).astype(F32)[:, None] * inv_freq[None, :]
    cos, sin = jnp.cos(ang), jnp.sin(ang)
    cosf = jnp.concatenate([cos, cos], axis=1)
    sinf = jnp.concatenate([-sin, sin], axis=1)
    log_g = jnp.log(1.0 - 2.0 ** (-5.0 - jnp.arange(4, dtype=F32)))
    idx = jnp.arange(CHUNK, dtype=F32)
    diff = idx[:, None] - idx[None, :]
    md = jnp.where(diff[None] >= 0, jnp.exp(log_g[:, None, None] * diff[None]), 0.0)
    kd = jnp.exp(log_g[:, None] * (CHUNK - 1 - idx)[None, :])
    qd = jnp.exp(log_g[:, None] * (idx + 1.0)[None, :])
    cd = jnp.exp(log_g * CHUNK)
    bc = lambda t: jnp.broadcast_to(t[:, :, None], (4, CHUNK, CHUNK))
    return cosf, sinf, md, bc(qd), bc(kd), jnp.broadcast_to(cd[:, None, None], (4, 8, CHUNK))


def _rot(x, cosf, sinf):
    return x * cosf + pltpu.roll(x, 64, 1) * sinf


def _rot_t(dx, cosf, sinf):
    return dx * cosf - pltpu.roll(dx, 64, 1) * sinf


RET_CHUNKS = 2


def _ret_specs(rev, S):
    R = min(RET_CHUNKS, S // CHUNK)
    rows, steps = R * CHUNK, S // (R * CHUNK)
    rn = (lambda n: steps - 1 - n) if rev else (lambda n: n)
    col = lambda c: pl.BlockSpec((rows, 512), lambda n, c=c: (rn(n), c))
    tab = pl.BlockSpec((rows, CHUNK), lambda n: (rn(n), 0))
    dec = pl.BlockSpec((4, CHUNK, CHUNK), lambda n: (0, 0, 0))
    cdec = pl.BlockSpec((4, 8, CHUNK), lambda n: (0, 0, 0))
    vec = pl.BlockSpec((1, 512), lambda n: (0, 0))
    st = pl.BlockSpec((R, 4, CHUNK, CHUNK), lambda n: (rn(n), 0, 0, 0))
    return R, steps, rn, col, tab, dec, cdec, vec, st


def _ret_fwd(proj, tables, gn_g, gn_b):
    S = proj.shape[0]
    R, steps, _, col, tab, dec, cdec, vec, st = _ret_specs(False, S)

    def body(q_ref, k_ref, v_ref, g_ref, cos_ref, sin_ref, md_ref, qd_ref, kd_ref, cd_ref, gng_ref, gnb_ref,
             out_ref, st_ref, state):
        @pl.when(pl.program_id(0) == 0)
        def _():
            state[...] = jnp.zeros_like(state)

        tiles = [(c, h) for c in range(R) for h in range(4)]
        rs = lambda c: slice(c * CHUNK, (c + 1) * CHUNK)
        sl = lambda h: slice(h * 128, (h + 1) * 128)
        qr = [_rot(q_ref[rs(c), sl(h)].astype(F32), cos_ref[rs(c), :], sin_ref[rs(c), :]) for c, h in tiles]
        kr = [_rot(k_ref[rs(c), sl(h)].astype(F32), cos_ref[rs(c), :], sin_ref[rs(c), :]) * (128 ** -0.5)
              for c, h in tiles]
        vb = [_bf(v_ref[rs(c), sl(h)]) for c, h in tiles]
        kv = [_dg(_bf(k * kd_ref[h]), v, 0, 0) for k, v, (c, h) in zip(kr, vb, tiles)]
        before = {}
        for h in range(4):
            s_h = state[h]
            for c in range(R):
                st_ref[c, h] = s_h
                before[(c, h)] = s_h
                s_h = s_h * cd_ref[h, 0:1, :] + kv[c * 4 + h]
            state[h] = s_h
        sc = [_dg(_bf(q), _bf(k), 1, 1) * md_ref[h] for q, k, (c, h) in zip(qr, kr, tiles)]
        r = [_dg(_bf(x), v, 1, 0) + _dg(_bf(q * qd_ref[h]), _bf(before[(c, h)]), 1, 0)
             for x, v, q, (c, h) in zip(sc, vb, qr, tiles)]
        for x, (c, h) in zip(r, tiles):
            y, _ = _norm_stats(x)
            rg = g_ref[rs(c), sl(h)].astype(F32)
            out_ref[rs(c), sl(h)] = rg * _sigmoid(rg) * (y * gng_ref[:, sl(h)] + gnb_ref[:, sl(h)])

    return pl.pallas_call(
        body, name="ret_fwd", grid=(steps,),
        out_shape=(jax.ShapeDtypeStruct((S, RET_W), F32), jax.ShapeDtypeStruct((S // CHUNK, 4, CHUNK, CHUNK), F32)),
        in_specs=[col(0), col(1), col(2), col(3), tab, tab, dec, dec, dec, cdec, vec, vec],
        out_specs=(pl.BlockSpec((R * CHUNK, 512), lambda n: (n, 0)), st),
        scratch_shapes=[pltpu.VMEM((4, CHUNK, CHUNK), F32)],
        compiler_params=_cparams(("arbitrary",)),
    )(proj, proj, proj, proj, *tables, gn_g, gn_b)


def _ret_bwd(proj, tables, gn_g, gn_b, states, d_out):
    S = proj.shape[0]
    R, steps, rn, col, tab, dec, cdec, vec, st = _ret_specs(True, S)

    def kernel_body(q_ref, k_ref, v_ref, g_ref, cos_ref, sin_ref, md_ref, qd_ref, kd_ref, cd_ref, gng_ref, gnb_ref,
                    st_ref, do_ref, dp_ref, dg_ref, db_ref, gstate):
        @pl.when(pl.program_id(0) == 0)
        def _():
            gstate[...] = jnp.zeros_like(gstate)
            dg_ref[...] = jnp.zeros_like(dg_ref)
            db_ref[...] = jnp.zeros_like(db_ref)

        tiles = [(c, h) for c in range(R) for h in range(4)]
        rs = lambda c: slice(c * CHUNK, (c + 1) * CHUNK)
        sl = lambda h: slice(h * 128, (h + 1) * 128)
        rot = lambda ref, c, h: _rot(ref[rs(c), sl(h)].astype(F32), cos_ref[rs(c), :], sin_ref[rs(c), :])
        qr = [rot(q_ref, c, h) for c, h in tiles]
        kr = [rot(k_ref, c, h) * (128 ** -0.5) for c, h in tiles]
        qb, kb = [_bf(x) for x in qr], [_bf(x) for x in kr]
        vb = [_bf(v_ref[rs(c), sl(h)]) for c, h in tiles]
        s0b = [_bf(st_ref[c, h]) for c, h in tiles]
        scb = [_bf(_dg(q, k, 1, 1) * md_ref[h]) for q, k, (c, h) in zip(qb, kb, tiles)]
        qdb = [_bf(q * qd_ref[h]) for q, (c, h) in zip(qr, tiles)]
        kdb = [_bf(k * kd_ref[h]) for k, (c, h) in zip(kr, tiles)]
        r = [_dg(x, v, 1, 0) + _dg(q, s, 1, 0) for x, v, q, s in zip(scb, vb, qdb, s0b)]
        drb, d_rg = [], []
        for x, (c, h) in zip(r, tiles):
            y, rstd = _norm_stats(x)
            gng = gng_ref[:, sl(h)]
            rg = g_ref[rs(c), sl(h)].astype(F32)
            sg = _sigmoid(rg)
            d_o = do_ref[rs(c), sl(h)]
            d_gn = d_o * (rg * sg)
            dg_ref[:, sl(h)] += jnp.sum(d_gn * y, axis=0, keepdims=True)
            db_ref[:, sl(h)] += jnp.sum(d_gn, axis=0, keepdims=True)
            drb.append(_bf(_norm_bwd(d_gn * gng, y, rstd)))
            d_rg.append(_bf(d_o * (y * gng + gnb_ref[:, sl(h)]) * (sg * (1.0 + rg * (1.0 - sg)))))
        grow = [_dg(q, d, 0, 0) for q, d in zip(qdb, drb)]
        after = {}
        for h in range(4):
            g_h = gstate[h]
            for c in reversed(range(R)):
                after[(c, h)] = _bf(g_h)
                g_h = g_h * cd_ref[h, 0:1, :] + grow[c * 4 + h]
            gstate[h] = g_h
        dscb = [_bf(_dg(d, v, 1, 1) * md_ref[h]) for d, v, (c, h) in zip(drb, vb, tiles)]
        for t, (c, h) in enumerate(tiles):
            gb = after[(c, h)]
            dqr = _dg(dscb[t], kb[t], 1, 0) + _dg(drb[t], s0b[t], 1, 1) * qd_ref[h]
            dkr = _dg(dscb[t], qb[t], 0, 0) + _dg(vb[t], gb, 1, 1) * kd_ref[h]
            dv = _dg(scb[t], drb[t], 0, 0) + _dg(kdb[t], gb, 1, 0)
            cosf, sinf = cos_ref[rs(c), :], sin_ref[rs(c), :]
            dp_ref[rs(c), 0 * 512 + h * 128:0 * 512 + (h + 1) * 128] = _bf(_rot_t(dqr, cosf, sinf))
            dp_ref[rs(c), 1 * 512 + h * 128:1 * 512 + (h + 1) * 128] = _bf(_rot_t(dkr, cosf, sinf) * (128 ** -0.5))
            dp_ref[rs(c), 2 * 512 + h * 128:2 * 512 + (h + 1) * 128] = _bf(dv)
            dp_ref[rs(c), 3 * 512 + h * 128:3 * 512 + (h + 1) * 128] = d_rg[t]

    acc = pl.BlockSpec((1, 512), lambda n: (0, 0))
    return pl.pallas_call(
        kernel_body, name="ret_bwd", grid=(steps,),
        out_shape=(jax.ShapeDtypeStruct((S, 2048), BF16), jax.ShapeDtypeStruct((1, 512), F32),
                   jax.ShapeDtypeStruct((1, 512), F32)),
        in_specs=[col(0), col(1), col(2), col(3), tab, tab, dec, dec, dec, cdec, vec, vec, st,
                  pl.BlockSpec((R * CHUNK, 512), lambda n: (rn(n), 0))],
        out_specs=(pl.BlockSpec((R * CHUNK, 2048), lambda n: (rn(n), 0)), acc, acc),
        scratch_shapes=[pltpu.VMEM((4, CHUNK, CHUNK), F32)],
        compiler_params=_cparams(("arbitrary",)),
    )(proj, proj, proj, proj, *tables, gn_g, gn_b, states, d_out)


SB_T = 256
SB_SCALE = 64 ** -0.5
SB_Q_COL, SB_K_COL, SB_V_COL = 2048 // 128, 2560 // 128, 3072 // 128


def _head_masks():
    lane = lax.broadcasted_iota(jnp.int32, (1, 128), 1)
    m0 = (lane < 64).astype(F32)
    return m0, 1.0 - m0


def _tri(n, cmp):
    r = lax.broadcasted_iota(jnp.int32, (n, n), 0)
    c = lax.broadcasted_iota(jnp.int32, (n, n), 1)
    return cmp(r, c)


def _tri_sum(x, tri):
    hi = _bf(x)
    lo = _bf(x - hi.astype(F32))
    return _dg(hi, tri, 1, 0) + _dg(lo, tri, 1, 0)


def _sb_weights(qms, kblks, upper, carry, causal):
    tiles = [(b, h) for b in range(len(kblks)) for h in range(2)]
    zs = [_dg(qms[h], kblks[b], 1, 1) for b, h in tiles]
    lgs = [-(jnp.maximum(z, 0.0) + jnp.log(1.0 + jnp.exp(-jnp.abs(z)))) for z in zs]
    if causal is not None:
        lgs = [jnp.where(causal, lg, 0.0) for lg in lgs]
    carries = list(carry)
    for t in range(len(tiles) - 2):
        carries.append(carries[t] + jnp.sum(lgs[t], axis=1, keepdims=True))
    his = [_bf(lg) for lg in lgs]
    los = [_bf(lg - hi.astype(F32)) for lg, hi in zip(lgs, his)]
    later = [_dg(hi, upper, 1, 0) for hi in his]
    later = [r + _dg(lo, upper, 1, 0) for r, lo in zip(later, los)]
    a = [jnp.exp(lg + z + (r + c)) for lg, z, r, c in zip(lgs, zs, later, carries)]
    if causal is not None:
        a = [jnp.where(causal, x, 0.0) for x in a]
    out = tuple(carries[t] + jnp.sum(lgs[t], axis=1, keepdims=True) for t in (len(tiles) - 2, len(tiles) - 1))
    return [a[2 * b:2 * b + 2] for b in range(len(kblks))], out


def _sb_fwd(proj, comm=None):
    S = proj.shape[0]
    T = min(SB_T, S)
    nq = S // T

    def body(q_ref, k_ref, v_ref, o_ref, a_ref, kb_ref, vm_ref, acc_ref):
        i = pl.program_id(1)
        m0, m1 = _head_masks()

        @pl.when(i == 0)
        def _():
            v = v_ref[...]
            kb_ref[...] = _bf(k_ref[...])
            vm_ref[0] = _bf(v * m0)
            vm_ref[1] = _bf(v * m1)

        q = q_ref[...]
        qm = (_bf(q * (m0 * SB_SCALE)), _bf(q * (m1 * SB_SCALE)))
        upper = _tri(T, lambda r, c: r > c).astype(BF16)
        causal = _tri(T, lambda r, c: c < r)

        def tiles(js, carry, mask, first):
            ks = [pl.multiple_of(j * T, T) for j in js]
            a, out = _sb_weights(qm, [kb_ref[pl.ds(k, T), :] for k in ks], upper, carry, mask)
            a = [[_bf(t) for t in per_block] for per_block in a]
            for b, j in enumerate(js):
                for h in range(2):
                    a_ref[h, j] = a[b][h]
            parts = [_dg(a[b][h], vm_ref[h, pl.ds(k, T), :], 1, 0) for b, k in enumerate(ks) for h in range(2)]
            part = functools.reduce(lambda u, w: u + w, parts)
            if first:
                acc_ref[...] = part
            else:
                acc_ref[...] += part
            return out

        zero = jnp.zeros((T, 1), F32)
        carry = tiles([i], (zero, zero), causal, True)
        carry = lax.fori_loop(0, i % 2, lambda _, c: tiles([i - 1], c, None, False), carry)
        top = i - 1 - i % 2
        carry = lax.fori_loop(0, (i // 2) % 2, lambda _, c: tiles([top, top - 1], c, None, False), carry)
        top = top - 2 * ((i // 2) % 2)
        lax.fori_loop(0, i // 4, lambda jj, c: tiles([top - 4 * jj - b for b in range(4)], c, None, False), carry)
        o_ref[...] = acc_ref[...]

    full = lambda c: pl.BlockSpec((S, 128), lambda p, i, c=c: (0, c + p))
    outs, landed = _pcall(
        body, name="sb_fwd", grid=(4, nq),
        out_shape=[jax.ShapeDtypeStruct((S, SB_W), F32), jax.ShapeDtypeStruct((4, 2, nq, nq, T, T), BF16)],
        in_specs=[pl.BlockSpec((T, 128), lambda p, i: (i, SB_Q_COL + p)), full(SB_K_COL), full(SB_V_COL)],
        out_specs=[pl.BlockSpec((T, 128), lambda p, i: (i, p)),
                   pl.BlockSpec((None, 2, None, nq, T, T), lambda p, i: (p, 0, i, 0, 0, 0))],
        scratch_shapes=[pltpu.VMEM((S, 128), BF16), pltpu.VMEM((2, S, 128), BF16), pltpu.VMEM((T, 128), F32)],
        sem=("arbitrary", "arbitrary"), args=(proj, proj, proj), comm=comm)
    return tuple(outs) if comm is None else (tuple(outs), landed)


def _sb_bwd(proj, a_saved, d_o, comm=None):
    S = proj.shape[0]
    T = min(SB_T, S)
    nq = S // T

    def body(q_ref, k_ref, v_ref, do_ref, a_ref, dq_ref, dk_ref, dv_ref, kb_ref, kbm_ref, vb_ref, dq_acc, dk_acc, dv_acc):
        i = pl.program_id(1)
        m0, m1 = _head_masks()

        @pl.when(i == 0)
        def _():
            k = k_ref[...]
            kb_ref[...] = _bf(k)
            kbm_ref[0] = _bf(k * m0)
            kbm_ref[1] = _bf(k * m1)
            vb_ref[...] = _bf(v_ref[...])
            dk_acc[...] = jnp.zeros_like(dk_acc)
            dv_acc[...] = jnp.zeros_like(dv_acc)

        q, d_out = q_ref[...], do_ref[...]
        qm = (_bf(q * (m0 * SB_SCALE)), _bf(q * (m1 * SB_SCALE)))
        dom = (_bf(d_out * m0), _bf(d_out * m1))
        lower = _tri(T, lambda r, c: r < c).astype(BF16)
        causal = _tri(T, lambda r, c: c < r)

        def up(js, carry, mask):
            ks = [pl.multiple_of(j * T, T) for j in js]
            tiles = [(b, h) for b in range(len(js)) for h in range(2)]
            zs = [_dg(qm[h], kb_ref[pl.ds(ks[b], T), :], 1, 1) for b, h in tiles]
            a = [a_ref[h, js[b]] for b, h in tiles]
            es = [w.astype(F32) * _dg(dom[h], vb_ref[pl.ds(ks[b], T), :], 1, 1) for w, (b, h) in zip(a, tiles)]
            carries = list(carry)
            for t in range(len(tiles)):
                carries.append(carries[t] + jnp.sum(es[t], axis=1, keepdims=True))
            d_lg = [_dg(_bf(e), lower, 1, 0) + c for e, c in zip(es, carries)]
            ens = [jnp.exp(-jnp.abs(z)) for z in zs]
            invs = [1.0 / (1.0 + en) for en in ens]
            betas = [jnp.where(z >= 0.0, inv, en * inv) for z, en, inv in zip(zs, ens, invs)]
            dzs = [e * (1.0 - b) - d * b for e, b, d in zip(es, betas, d_lg)]
            if mask is not None:
                dzs = [jnp.where(mask, dz, 0.0) for dz in dzs]
            dzs = [_bf(dz) for dz in dzs]
            parts = [_dg(dzs[t], kbm_ref[h, pl.ds(ks[b], T), :], 1, 0) for t, (b, h) in enumerate(tiles)]
            dq_acc[...] += functools.reduce(lambda u, w: u + w, parts)
            for b, k in enumerate(ks):
                dk_acc[pl.ds(k, T), :] += _dg(dzs[2 * b], qm[0], 0, 0) + _dg(dzs[2 * b + 1], qm[1], 0, 0)
                dv_acc[pl.ds(k, T), :] += _dg(a[2 * b], dom[0], 0, 0) + _dg(a[2 * b + 1], dom[1], 0, 0)
            return tuple(carries[-2:])

        zero = jnp.zeros((T, 1), F32)
        dq_acc[...] = jnp.zeros_like(dq_acc)
        carry = lax.fori_loop(0, i // 4, lambda jj, c: up([4 * jj + b for b in range(4)], c, None), (zero, zero))
        done = 4 * (i // 4)
        carry = lax.fori_loop(0, (i // 2) % 2, lambda _, c: up([done, done + 1], c, None), carry)
        carry = lax.fori_loop(0, i % 2, lambda _, c: up([i - 1], c, None), carry)
        up([i], carry, causal)
        dq_ref[...] = _bf(dq_acc[...] * SB_SCALE)

        @pl.when(i == nq - 1)
        def _():
            dk_ref[...] = _bf(dk_acc[...])
            dv_ref[...] = _bf(dv_acc[...])

    full = lambda c: pl.BlockSpec((S, 128), lambda p, i, c=c: (0, c + p))
    tile = pl.BlockSpec((T, 128), lambda p, i: (i, p))
    acc = pl.BlockSpec((S, 128), lambda p, i: (0, p))
    out = jax.ShapeDtypeStruct((S, SB_W), BF16)
    outs, landed = _pcall(
        body, name="sb_bwd", grid=(4, nq), out_shape=[out, out, out],
        in_specs=[pl.BlockSpec((T, 128), lambda p, i: (i, SB_Q_COL + p)), full(SB_K_COL), full(SB_V_COL), tile,
                  pl.BlockSpec((None, 2, None, nq, T, T), lambda p, i: (p, 0, i, 0, 0, 0))],
        out_specs=[tile, acc, acc],
        scratch_shapes=[pltpu.VMEM((S, 128), BF16), pltpu.VMEM((2, S, 128), BF16), pltpu.VMEM((S, 128), BF16),
                        pltpu.VMEM((T, 128), F32), pltpu.VMEM((S, 128), F32), pltpu.VMEM((S, 128), F32)],
        sem=("arbitrary", "arbitrary"), args=(proj, proj, proj, d_o, a_saved), comm=comm)
    return tuple(outs) if comm is None else (tuple(outs), landed)


SGU_U_COL, SGU_V_COL = 3584 // 512, 4096 // 512


def _causal(w):
    r = lax.broadcasted_iota(jnp.int32, (CHUNK, CHUNK), 0)
    c = lax.broadcasted_iota(jnp.int32, (CHUNK, CHUNK), 1)
    return jnp.where(r >= c, w, 0.0)


SGU_CHUNKS = 4


def _sgu_fwd(proj, ln_g, ln_b, w, b):
    S = proj.shape[0]
    R = min(SGU_CHUNKS, S // CHUNK)
    rows = R * CHUNK

    def body(u_ref, v_ref, g_ref, b_ref, w_ref, bias_ref, out_ref):
        wc = [_bf(_causal(w_ref[g])) for g in range(4)]
        for r in range(R):
            rs = slice(r * CHUNK, (r + 1) * CHUNK)
            u = _gelu(u_ref[rs, :].astype(F32))
            xh, _ = _norm_stats(_gelu(v_ref[rs, :].astype(F32)))
            vn = _bf(xh * g_ref[...] + b_ref[...])
            for g in range(4):
                sl = slice(g * 128, (g + 1) * 128)
                out_ref[rs, sl] = u[:, sl] * (_dg(wc[g], vn[:, sl], 1, 0) + bias_ref[g])

    vec = pl.BlockSpec((1, 512), lambda n: (0, 0))
    return pl.pallas_call(
        body, name="sgu_fwd", grid=(S // rows,),
        out_shape=jax.ShapeDtypeStruct((S, SGU_W), F32),
        in_specs=[pl.BlockSpec((rows, 512), lambda n: (n, SGU_U_COL)),
                  pl.BlockSpec((rows, 512), lambda n: (n, SGU_V_COL)), vec, vec,
                  pl.BlockSpec((4, CHUNK, CHUNK), lambda n: (0, 0, 0)), pl.BlockSpec((4, CHUNK, 1), lambda n: (0, 0, 0))],
        out_specs=pl.BlockSpec((rows, 512), lambda n: (n, 0)),
        compiler_params=_cparams(("parallel",)),
    )(proj, proj, ln_g, ln_b, w, b)


def _sgu_bwd(proj, ln_g, ln_b, w, b, d_out):
    S = proj.shape[0]
    R = min(SGU_CHUNKS, S // CHUNK)
    rows = R * CHUNK

    def body(u_ref, v_ref, g_ref, b_ref, w_ref, bias_ref, do_ref, dp_ref, dg_ref, db_ref, dw_ref, dbias_ref):
        @pl.when(pl.program_id(0) == 0)
        def _():
            dg_ref[...] = jnp.zeros_like(dg_ref)
            db_ref[...] = jnp.zeros_like(db_ref)
            dw_ref[...] = jnp.zeros_like(dw_ref)
            dbias_ref[...] = jnp.zeros_like(dbias_ref)

        ln_gain = g_ref[...]
        wc = [_bf(_causal(w_ref[g])) for g in range(4)]
        for r in range(R):
            rs = slice(r * CHUNK, (r + 1) * CHUNK)
            gu, gv = u_ref[rs, :].astype(F32), v_ref[rs, :].astype(F32)
            u = _gelu(gu)
            xh, rstd = _norm_stats(_gelu(gv))
            vn = _bf(xh * ln_gain + b_ref[...])
            d_o = do_ref[rs, :]
            d_vn = []
            for g in range(4):
                sl = slice(g * 128, (g + 1) * 128)
                sv = _dg(wc[g], vn[:, sl], 1, 0) + bias_ref[g]
                dp_ref[rs, sl] = _bf(d_o[:, sl] * sv * _gelu_grad(gu[:, sl]))
                d_sv = d_o[:, sl] * u[:, sl]
                dbias_ref[g] += jnp.sum(d_sv, axis=1, keepdims=True)
                d_svb = _bf(d_sv)
                dw_ref[g] += _causal(_dg(d_svb, vn[:, sl], 1, 1))
                d_vn.append(_dg(wc[g], d_svb, 0, 0))
            d_vn = jnp.concatenate(d_vn, axis=1)
            dg_ref[...] += jnp.sum(d_vn * xh, axis=0, keepdims=True)
            db_ref[...] += jnp.sum(d_vn, axis=0, keepdims=True)
            dp_ref[rs, 512:1024] = _bf(_norm_bwd(d_vn * ln_gain, xh, rstd) * _gelu_grad(gv))

    vec = pl.BlockSpec((1, 512), lambda n: (0, 0))
    wspec = pl.BlockSpec((4, CHUNK, CHUNK), lambda n: (0, 0, 0))
    bspec = pl.BlockSpec((4, CHUNK, 1), lambda n: (0, 0, 0))
    return pl.pallas_call(
        body, name="sgu_bwd", grid=(S // rows,),
        out_shape=(jax.ShapeDtypeStruct((S, 1024), BF16), jax.ShapeDtypeStruct((1, 512), F32),
                   jax.ShapeDtypeStruct((1, 512), F32), jax.ShapeDtypeStruct((4, CHUNK, CHUNK), F32),
                   jax.ShapeDtypeStruct((4, CHUNK, 1), F32)),
        in_specs=[pl.BlockSpec((rows, 512), lambda n: (n, SGU_U_COL)),
                  pl.BlockSpec((rows, 512), lambda n: (n, SGU_V_COL)), vec, vec, wspec, bspec,
                  pl.BlockSpec((rows, 512), lambda n: (n, 0))],
        out_specs=(pl.BlockSpec((rows, 1024), lambda n: (n, 0)), vec, vec, wspec, bspec),
        compiler_params=_cparams(("arbitrary",)),
    )(proj, proj, ln_g, ln_b, w, b, d_out)


GATE_COL = 4608 // 512


def _merge_fwd(proj, branches, p_list, tm=512):
    S = proj.shape[0]
    tm = min(tm, S)

    def body(r_ref, s_ref, g_ref, pr_ref, ps_ref, pg_ref, gr_ref, gs_ref, gg_ref, m_ref, br_ref):
        acc = None
        for k, (x_ref, p_ref, gate_ref) in enumerate(((r_ref, pr_ref, gr_ref), (s_ref, ps_ref, gs_ref),
                                                      (g_ref, pg_ref, gg_ref))):
            br = _dg(_bf(x_ref[...]), _bf(p_ref[...]), 1, 0)
            br_ref[k] = _bf(br)
            term = _sigmoid(gate_ref[...].astype(F32)) * br
            acc = term if acc is None else acc + term
        m_ref[...] = _bf(acc)

    xs = pl.BlockSpec((tm, 512), lambda i, n: (i, 0))
    ps = pl.BlockSpec((512, 512), lambda i, n: (0, n))
    gate = lambda k: pl.BlockSpec((tm, 512), lambda i, n, k=k: (i, GATE_COL + 2 * k + n))
    return pl.pallas_call(
        body, name="merge_fwd", grid=(S // tm, 2),
        out_shape=(jax.ShapeDtypeStruct((S, D_MODEL), BF16), jax.ShapeDtypeStruct((3, S, D_MODEL), BF16)),
        in_specs=[xs, xs, xs, ps, ps, ps, gate(0), gate(1), gate(2)],
        out_specs=(pl.BlockSpec((tm, 512), lambda i, n: (i, n)), pl.BlockSpec((3, tm, 512), lambda i, n: (0, i, n))),
        compiler_params=_cparams(("parallel", "parallel")),
    )(*branches, *p_list, proj, proj, proj)


def _gate_bwd(proj, br, d_merged, tm=512):
    S = proj.shape[0]
    tm = min(tm, S)

    def body(dm_ref, br_ref, gr_ref, gs_ref, gg_ref, *out_refs):
        dm = dm_ref[...]
        for k, gate_ref in enumerate((gr_ref, gs_ref, gg_ref)):
            s = _sigmoid(gate_ref[...].astype(F32))
            out_refs[k][...] = _bf(dm * s)
            out_refs[3 + k][...] = _bf(dm * br_ref[k].astype(F32) * (s * (1.0 - s)))

    gate = lambda k: pl.BlockSpec((tm, 512), lambda i, n, k=k: (i, GATE_COL + 2 * k + n))
    three = pl.BlockSpec((3, tm, 512), lambda i, n: (0, i, n))
    tile = pl.BlockSpec((tm, 512), lambda i, n: (i, n))
    outs = pl.pallas_call(
        body, name="gate_bwd", grid=(S // tm, 2),
        out_shape=[jax.ShapeDtypeStruct((S, D_MODEL), BF16)] * 6,
        in_specs=[tile, three, gate(0), gate(1), gate(2)], out_specs=[tile] * 6,
        compiler_params=_cparams(("parallel", "parallel")),
    )(d_merged, br, proj, proj, proj)
    return outs[:3], outs[3:]


def _ln_bwd(dy, u, g, target=None, tm=256):
    S, D = u.shape
    tm = min(tm, S)
    loss = target is not None

    def body(*refs):
        dy_ref, u_ref, g_ref = refs[:3]
        du_ref, dub_ref, dg_ref, db_ref = refs[3 + loss:7 + loss]

        @pl.when(pl.program_id(0) == 0)
        def _():
            for acc_ref in refs[5 + loss:]:
                acc_ref[...] = jnp.zeros_like(acc_ref)

        dy_t = dy_ref[...]
        if loss:
            err = dy_t - refs[3][...]
            refs[-1][...] += jnp.sum(err * err, axis=0, keepdims=True)
            dy_t = err * (1.0 / D)
        xh, rstd = _norm_stats(u_ref[...])
        dg_ref[...] += jnp.sum(dy_t * xh, axis=0, keepdims=True)
        db_ref[...] += jnp.sum(dy_t, axis=0, keepdims=True)
        du = _norm_bwd(dy_t * g_ref[...], xh, rstd)
        du_ref[...] = du
        dub_ref[...] = _bf(du)

    tile = pl.BlockSpec((tm, D), lambda i: (i, 0))
    vec = pl.BlockSpec((1, D), lambda i: (0, 0))
    row = jax.ShapeDtypeStruct((1, D), F32)
    return pl.pallas_call(
        body, name="ln_bwd", grid=(S // tm,),
        out_shape=[jax.ShapeDtypeStruct((S, D), F32), jax.ShapeDtypeStruct((S, D), BF16)] + [row] * (2 + loss),
        in_specs=[tile, tile, vec] + [tile] * loss, out_specs=[tile, tile] + [vec] * (2 + loss),
        compiler_params=_cparams(("arbitrary",)),
    )(dy, u, g, *([target] if loss else []))


def _layer_fwd(x, x_bf, W, tables, sb_comm=None):
    proj = _matmul(x_bf, W["w_in_t"], "nt", name="proj", tm=1024, tn=768, tk=1024)
    retg, states = _ret_fwd(proj, tables, W["ret_gn_g"], W["ret_gn_b"])
    if sb_comm is None:
        sb, sb_a = _sb_fwd(proj)
    else:
        (sb, sb_a), landed = _sb_fwd(proj, comm=sb_comm[0])
        sb_comm[1](landed)
    sg = _sgu_fwd(proj, W["sgu_ln_g"], W["sgu_ln_b"], W["sgu_w"], W["sgu_b"])
    merged, br = _merge_fwd(proj, (retg, sb, sg), (W["p_ret"], W["p_sb"], W["p_sgu"]))
    u1, x1, x1_bf = _matmul(merged, W["w_out"], "nn", name="out_ln", tm=512, tn=1024, tk=1024, epi="ln",
                            extra=(x, W["ln1_g"], W["ln1_b"]))
    act = _matmul(x1_bf, W["w_up"], "nn", name="up", tm=1024, tn=1024, tk=1024, epi="relu2")
    u2, x2, x2_bf = _matmul(act, W["w_down"], "nn", name="down_ln", tm=512, tn=1024, tk=4096, epi="ln",
                            extra=(x1, W["ln2_g"], W["ln2_b"]))
    saved = dict(x_bf=x_bf, proj=proj, retg=retg, states=states, sb=sb, sb_a=sb_a, sg=sg, merged=merged, br=br, u1=u1,
                 x1_bf=x1_bf, act=act, u2=u2)
    return x2, x2_bf, saved


def _layer_bwd(d_x2, W, tables, sv, chunk_dtype=None, sb_comm_fn=None, dx_comm_fn=None, target=None):
    dt = F32 if chunk_dtype is None else chunk_dtype
    rows, cols = (None, None) if chunk_dtype is None else ("rows", "cols")
    g, landed = {}, {}
    du2, du2_bf, g["ln2_g"], g["ln2_b"], *sq = _ln_bwd(d_x2, sv["u2"], W["ln2_g"], target=target)
    if sq:
        landed["sq"] = sq[0]
    d_hpre = _matmul(du2_bf, W["w_down"], "nt", name="d_act", tm=1024, tn=1024, tk=1024, epi="drelu2",
                     extra=(sv["act"],), out_dtype=BF16)
    g["w_down"] = _matmul(sv["act"], du2_bf, "tn", name="dw_down", tm=512, tn=1024, tk=4096, out_dtype=dt, chunks=rows)
    g["w_up"] = _matmul(sv["x1_bf"], d_hpre, "tn", name="dw_up", tm=1024, tn=512, tk=4096, out_dtype=dt, chunks=cols)
    d_x1 = _matmul(d_hpre, W["w_up"], "nt", name="d_x1", tm=512, tn=1024, tk=4096, epi="add", extra=(du2,))
    du1, du1_bf, g["ln1_g"], g["ln1_b"] = _ln_bwd(d_x1, sv["u1"], W["ln1_g"])
    d_merged = _matmul(du1_bf, W["w_out"], "nt", name="d_merged", tm=1024, tn=1024, tk=1024)
    g["w_out"] = _matmul(sv["merged"], du1_bf, "tn", name="dw_out", tm=1024, tn=512, tk=4096, out_dtype=dt, chunks=rows)
    d_br, d_gate = _gate_bwd(sv["proj"], sv["br"], d_merged)
    d_branch = []
    for k, (nm, act) in enumerate((("p_ret", sv["retg"]), ("p_sb", sv["sb"]), ("p_sgu", sv["sg"]))):
        d_branch.append(_matmul(d_br[k], W[nm], "nt", name="d_" + nm[2:], tm=1024, tn=512, tk=1024))
        g[nm] = _matmul(act, d_br[k], "tn", name="dw_" + nm[2:], tm=512, tn=1024, tk=2048, out_dtype=dt, chunks=cols)
    d_ret, g["ret_gn_g"], g["ret_gn_b"] = _ret_bwd(sv["proj"], tables, W["ret_gn_g"], W["ret_gn_b"], sv["states"],
                                                   d_branch[0])
    if sb_comm_fn is None:
        d_sq, d_sk, d_sv = _sb_bwd(sv["proj"], sv["sb_a"], d_branch[1])
    else:
        (d_sq, d_sk, d_sv), landed["sb"] = _sb_bwd(sv["proj"], sv["sb_a"], d_branch[1], comm=sb_comm_fn(g))
    d_sgu, g["sgu_ln_g"], g["sgu_ln_b"], g["sgu_w"], g["sgu_b"] = _sgu_bwd(
        sv["proj"], W["sgu_ln_g"], W["sgu_ln_b"], W["sgu_w"], W["sgu_b"], d_branch[2])
    d_proj = [d_ret, d_sq, d_sk, d_sv, d_sgu, d_gate[0], d_gate[1], d_gate[2]]
    g["w_in"] = _matmul(d_proj, sv["x_bf"], "tn", name="dw_in", tm=256, tn=1024, tk=4096, out_dtype=dt, chunks=rows)
    if chunk_dtype is None:
        g["w_in"] = g["w_in"].T
    d_x = _matmul_rows_of(d_proj, W["w_in_t"], du1, name="d_x", tm=512,
                          comm=None if dx_comm_fn is None else dx_comm_fn(g))
    if dx_comm_fn is not None:
        d_x, landed["dx"] = d_x
    return d_x, g, landed


BIG = ("w_in", "p_ret", "p_sb", "p_sgu", "w_out", "w_up", "w_down")
SMALL = ("ret_gn_g", "ret_gn_b", "sgu_ln_g", "sgu_ln_b", "sgu_w", "sgu_b", "ln1_g", "ln1_b", "ln2_g", "ln2_b")
GATHER_KIND = {"w_in": "rows", "p_ret": "cols", "p_sb": "cols", "p_sgu": "cols", "w_out": "rows", "w_up": "cols",
               "w_down": "rows"}


def _small_weights(small, l):
    W = {}
    for n in SMALL:
        if n == "sgu_w":
            W[n] = small[n][l]
        elif n == "sgu_b":
            W[n] = small[n][l].reshape(4, CHUNK, 1)
        else:
            W[n] = small[n][l].reshape(1, -1)
    return W


def _local_step(x, target, full, small):
    tables = _ret_tables(x.shape[0])
    Ws = [{**{n: full[n][l] for n in BIG[1:]}, "w_in_t": full["w_in"][l].T, **_small_weights(small, l)}
          for l in range(DEPTH)]
    saved = []
    h, h_bf = x, _bf(x)
    for l in range(DEPTH):
        h, h_bf, sv = _layer_fwd(h, h_bf, Ws[l], tables)
        saved.append(sv)
    grads = [None] * DEPTH
    d_h, grads[-1], landed = _layer_bwd(h, Ws[-1], tables, saved[-1], target=target)
    for l in reversed(range(DEPTH - 1)):
        d_h, grads[l], _ = _layer_bwd(d_h, Ws[l], tables, saved[l])
    return landed["sq"], d_h, grads


def _adam(w, parts, m, v, name):
    L, R, C = w.shape
    tr = next(t for t in (320, 256, 128) if R % t == 0)
    assert len(parts) == L

    def body(*refs):
        w_ref, p_refs, (m_ref, v_ref, g_ref, d_ref, nm_ref, nv_ref) = refs[0], refs[1:1 + L], refs[1 + L:]
        layer = pl.program_id(0)
        g = None
        for li, p_ref in enumerate(p_refs):
            s = p_ref[0].astype(F32)
            for j in range(1, p_ref.shape[0]):
                s = s + p_ref[j].astype(F32)
            g = s if g is None else jnp.where(layer == li, s, g)
        g_ref[...] = g
        d_ref[...], nm_ref[...], nv_ref[...] = _adam_update(w_ref[...], g, m_ref[...], v_ref[...])

    tile = pl.BlockSpec((None, tr, C), lambda l, i: (l, i, 0))
    part = lambda li: pl.BlockSpec((parts[li].shape[0], tr, C), lambda l, i, li=li: (0, jnp.where(l == li, i, 0), 0))
    out = jax.ShapeDtypeStruct((L, R, C), F32)
    return pl.pallas_call(
        body, name=name, grid=(L, R // tr), out_shape=(out, out, out, out),
        in_specs=[tile] + [part(li) for li in range(L)] + [tile, tile],
        out_specs=(tile, tile, tile, tile),
        compiler_params=_cparams(("parallel", "parallel")),
    )(w, *parts, m, v)


def _adam_update(w, g, m, v):
    m2 = ADAM_B1 * m + (1.0 - ADAM_B1) * g
    v2 = ADAM_B2 * v + (1.0 - ADAM_B2) * (g * g)
    m_hat = m2 / (1.0 - ADAM_B1 ** ADAM_STEP)
    v_hat = v2 / (1.0 - ADAM_B2 ** ADAM_STEP)
    return -ADAM_LR * (m_hat / (jnp.sqrt(v_hat) + ADAM_EPS) + ADAM_WD * w), m2, v2


def _adam_small(w, m, v, parts):
    k = len(SMALL)

    def body(*refs):
        w_refs, m_refs, v_refs, p_refs, outs = refs[:k], refs[k:2 * k], refs[2 * k:3 * k], refs[3 * k:5 * k], refs[5 * k:]
        for i in range(k):
            vector = len(w_refs[i].shape) == 2
            for l in range(DEPTH):
                p_ref = p_refs[DEPTH * i + l]
                g = p_ref[0]
                for j in range(1, N_DEV):
                    g = g + p_ref[j]
                at = (slice(l, l + 1), slice(None)) if vector else (l,)
                delta, m2, v2 = _adam_update(w_refs[i][at], g, m_refs[i][at], v_refs[i][at])
                for o_ref, val in zip(outs[4 * i:4 * i + 4], (g, delta, m2, v2)):
                    o_ref[at] = val

    vmem = pl.BlockSpec(memory_space=pltpu.VMEM)
    args = [w[n] for n in SMALL] + [m[n] for n in SMALL] + [v[n] for n in SMALL] + \
           [parts[(n, l)] for n in SMALL for l in range(DEPTH)]
    out_shape = [jax.ShapeDtypeStruct(w[n].shape, F32) for n in SMALL for _ in range(4)]
    outs = pl.pallas_call(body, name="adam_small", out_shape=out_shape, in_specs=[vmem] * len(args),
                          out_specs=[vmem] * len(out_shape), compiler_params=_cparams())(*args)
    return {n: tuple(outs[4 * i:4 * i + 4]) for i, n in enumerate(SMALL)}


WEIGHTS = ("w_in", "ret_gn_g", "ret_gn_b", "sgu_ln_g", "sgu_ln_b", "sgu_w", "sgu_b", "p_ret", "p_sb", "p_sgu", "w_out",
           "ln1_g", "ln1_b", "w_up", "w_down", "ln2_g", "ln2_b")


def kernel(x, w_in, ret_gn_g, ret_gn_b, sgu_ln_g, sgu_ln_b, sgu_w, sgu_b, p_ret, p_sb, p_sgu, w_out, ln1_g, ln1_b, w_up, w_down, ln2_g, ln2_b, loss_target, m_w_in, m_ret_gn_g, m_ret_gn_b, m_sgu_ln_g, m_sgu_ln_b, m_sgu_w, m_sgu_b, m_p_ret, m_p_sb, m_p_sgu, m_w_out, m_ln1_g, m_ln1_b, m_w_up, m_w_down, m_ln2_g, m_ln2_b, v_w_in, v_ret_gn_g, v_ret_gn_b, v_sgu_ln_g, v_sgu_ln_b, v_sgu_w, v_sgu_b, v_p_ret, v_p_sb, v_p_sgu, v_w_out, v_ln1_g, v_ln1_b, v_w_up, v_w_down, v_ln2_g, v_ln2_b):
    w = dict(zip(WEIGHTS, (w_in, ret_gn_g, ret_gn_b, sgu_ln_g, sgu_ln_b, sgu_w, sgu_b, p_ret, p_sb, p_sgu, w_out,
                           ln1_g, ln1_b, w_up, w_down, ln2_g, ln2_b)))
    m = dict(zip(WEIGHTS, (m_w_in, m_ret_gn_g, m_ret_gn_b, m_sgu_ln_g, m_sgu_ln_b, m_sgu_w, m_sgu_b, m_p_ret, m_p_sb,
                           m_p_sgu, m_w_out, m_ln1_g, m_ln1_b, m_w_up, m_w_down, m_ln2_g, m_ln2_b)))
    v = dict(zip(WEIGHTS, (v_w_in, v_ret_gn_g, v_ret_gn_b, v_sgu_ln_g, v_sgu_ln_b, v_sgu_w, v_sgu_b, v_p_ret, v_p_sb,
                           v_p_sgu, v_w_out, v_ln1_g, v_ln1_b, v_w_up, v_w_down, v_ln2_g, v_ln2_b)))

    small = {n: w[n] for n in SMALL}
    shard = {n: _bf(w[n]) for n in BIG}
    shard["w_in"] = shard["w_in"].transpose(0, 2, 1)
    S = x.shape[1]
    x0, target = x.reshape(S, D_MODEL), loss_target.reshape(S, D_MODEL)
    tables = _ret_tables(S)
    Ws = [_small_weights(small, l) for l in range(DEPTH)]

    (Ws[0]["w_in_t"],) = _exchange([_gather_transfer(shard["w_in"], 0, "rows")], "gather_w_in0", relay=True)
    later = [(n, 0) for n in BIG[1:]] + [(n, 1) for n in BIG]

    def weights_landed(landed):
        for (n, l), z in zip(later, landed):
            Ws[l]["w_in_t" if n == "w_in" else n] = z

    gather = _Comm([_gather_transfer(shard[n], l, GATHER_KIND[n]) for n, l in later], relay=True)
    h, h_bf, saved0 = _layer_fwd(x0, _bf(x0), Ws[0], tables, sb_comm=(gather, weights_landed))
    h, _, saved1 = _layer_fwd(h, h_bf, Ws[1], tables)
    d_h, g1, landed1 = _layer_bwd(h, Ws[1], tables, saved1, chunk_dtype=BF16, target=target,
                                  sb_comm_fn=lambda g: _Comm([_scatter_transfer(g[n]) for n in BIG[1:]]))
    loss = lax.psum(0.5 * jnp.sum(landed1["sq"]) / D_MODEL, ("x", "y", "c"))
    early = [("w_in", 1)] + [(n, 0) for n in BIG[1:]]

    def small_slabs(g):
        return [_slab_transfer(g[n].reshape(4, CHUNK) if n == "sgu_b" else g[n]) for n in SMALL]

    def early_scatter(g0):
        return _Comm([_scatter_transfer((g1 if l else g0)[n]) for n, l in early] + small_slabs(g1))

    def late_scatter(g0):
        pairs = _pair_reduce(g0["w_in"], "w_in0_pairs")
        return _Comm([_chip_scatter_transfer(pairs)] + small_slabs(g0))

    d_x, g0, landed = _layer_bwd(d_h, Ws[0], tables, saved0, chunk_dtype=BF16, sb_comm_fn=early_scatter,
                                 dx_comm_fn=late_scatter)
    parts = {**dict(zip(early, landed["sb"])), **{(n, 1): z for n, z in zip(BIG[1:], landed1["sb"])}}
    parts[("w_in", 0)] = landed["dx"][0]
    small_parts = {**{(n, 1): z for n, z in zip(SMALL, landed["sb"][len(early):])},
                   **{(n, 0): z for n, z in zip(SMALL, landed["dx"][1:])}}

    grad, delta, new_m, new_v = {}, {}, {}, {}
    for n in BIG:
        view = (lambda a: a.transpose(0, 2, 1)) if n == "w_in" else (lambda a: a)
        res = _adam(view(w[n]), [parts[(n, l)] for l in range(DEPTH)], view(m[n]), view(v[n]), "adam_" + n)
        grad[n], delta[n], new_m[n], new_v[n] = (view(r) for r in res)
    for n, res in _adam_small(small, m, v, small_parts).items():
        grad[n], delta[n], new_m[n], new_v[n] = res

    return (loss, d_x.reshape(x.shape), *[grad[n] for n in WEIGHTS], *[delta[n] for n in WEIGHTS],
            *[new_m[n] for n in WEIGHTS], *[new_v[n] for n in WEIGHTS])
```

```python
import functools
import math

import numpy as np
import jax
import jax.numpy as jnp
from jax import lax
from jax.experimental import pallas as pl
from jax.experimental.pallas import tpu as pltpu

F32 = jnp.float32
BF16 = jnp.bfloat16

N_DEV = 8
DEPTH = 2
D_MODEL = 1024
CHUNK = 128
RET_W = 512
SB_W = 512
SGU_W = 512
N_IN = 7680
LN_EPS = 1e-5
ALPHA = (2 * DEPTH) ** 0.25
ROPE_BASE = 10000.0
ADAM_LR, ADAM_B1, ADAM_B2, ADAM_EPS, ADAM_WD, ADAM_STEP = 0.001, 0.9, 0.999, 1e-08, 0.01, 10
VMEM_LIMIT = 56 * 1024 * 1024

_GELU_K = math.sqrt(2.0 / math.pi)
_GELU_C = 0.044715


def _cparams(sem=None):
    return pltpu.CompilerParams(dimension_semantics=sem, vmem_limit_bytes=VMEM_LIMIT)


def _dg(a, b, ca, cb):
    return lax.dot_general(a, b, (((ca,), (cb,)), ((), ())), preferred_element_type=F32)


def _bf(x):
    return x.astype(BF16)


def _sigmoid(x):
    return 1.0 / (1.0 + jnp.exp(-x))


def _gelu(x):
    t = jnp.tanh(_GELU_K * (x + _GELU_C * (x * x * x)))
    return x * (0.5 * (1.0 + t))


def _gelu_grad(x):
    t = jnp.tanh(_GELU_K * (x + _GELU_C * (x * x * x)))
    return 0.5 * (1.0 + t) + 0.5 * x * (1.0 - t * t) * (_GELU_K * (1.0 + 3.0 * _GELU_C * x * x))


def _norm_stats(u):
    mu = jnp.mean(u, axis=-1, keepdims=True)
    d = u - mu
    var = jnp.mean(d * d, axis=-1, keepdims=True)
    rstd = lax.rsqrt(var + LN_EPS)
    return d * rstd, rstd


def _norm_bwd(dxh, xh, rstd):
    return rstd * (dxh - jnp.mean(dxh, axis=-1, keepdims=True) - xh * jnp.mean(dxh * xh, axis=-1, keepdims=True))


class _Transfer:
    def __init__(self, src, dst_shape, src_at, dst_at, same_core=False):
        self.src, self.dst_shape, self.src_at, self.dst_at = src, tuple(dst_shape), src_at, dst_at
        self.same_core = same_core


def _gather_transfer(shard, l, kind):
    _, r, c = shard.shape
    src_at = lambda ref, p: ref.at[l]
    if kind == "slab":
        return _Transfer(shard, (N_DEV, r, c), src_at, lambda ref, s: ref.at[s])
    if kind == "rows":
        return _Transfer(shard, (N_DEV * r, c), src_at, lambda ref, s: ref.at[pl.ds(pl.multiple_of(s * r, r), r), :])
    return _Transfer(shard, (r, N_DEV * c), src_at, lambda ref, s: ref.at[:, pl.ds(pl.multiple_of(s * c, c), c)])


def _scatter_transfer(chunks):
    return _Transfer(chunks, chunks.shape, lambda ref, p: ref.at[p], lambda ref, s: ref.at[s])


def _slab_transfer(arr):
    return _Transfer(arr, (N_DEV,) + arr.shape, lambda ref, p: ref, lambda ref, s: ref.at[s])


class _Comm:
    def __init__(self, transfers, relay=False):
        self.transfers = list(transfers)
        self.relay = relay
        self.n = len(self.transfers)
        self.arrays = [t.src for t in self.transfers]
        self.out_shape = [jax.ShapeDtypeStruct(t.dst_shape, t.src.dtype) for t in self.transfers]
        self.scratch = [pltpu.SemaphoreType.DMA((self.n * (N_DEV - 1),)), pltpu.SemaphoreType.DMA((self.n * (N_DEV - 1),)),
                        pltpu.SemaphoreType.DMA((self.n,))]

    def _relay_copies(self, srcs, dsts, send_sems, recv_sems, local_sems):
        x, y, c = lax.axis_index("x"), lax.axis_index("y"), lax.axis_index("c")
        me = 4 * x + 2 * y + c
        chips = [(1 - x, y), (x, 1 - y), (1 - x, 1 - y)]
        first, passed, own = [], [], []
        for t, tr in enumerate(self.transfers):
            def copy(k, src, sender, to, t=t, tr=tr):
                return pltpu.make_async_remote_copy(
                    src_ref=src, dst_ref=tr.dst_at(dsts[t], sender), send_sem=send_sems.at[t * (N_DEV - 1) + k],
                    recv_sem=recv_sems.at[t * (N_DEV - 1) + k], device_id=to, device_id_type=pl.DeviceIdType.MESH)

            mine = tr.src_at(srcs[t], me)
            first.append([copy(0, mine, me, (x, y, 1 - c))] + [copy(1 + j, mine, me, (px, py, c))
                                                                for j, (px, py) in enumerate(chips)])
            passed.append([copy(4 + j, tr.dst_at(dsts[t], 4 * px + 2 * py + c), 4 * px + 2 * py + c, (x, y, 1 - c))
                           for j, (px, py) in enumerate(chips)])
            own.append(pltpu.make_async_copy(mine, tr.dst_at(dsts[t], me), local_sems.at[t]))
        return first, passed, own

    def _copies(self, srcs, dsts, send_sems, recv_sems, local_sems):
        x, y, c = lax.axis_index("x"), lax.axis_index("y"), lax.axis_index("c")
        me = 4 * x + 2 * y + c
        copies = []
        for d in range(1, N_DEV):
            px = 1 - x if d & 4 else x
            py = 1 - y if d & 2 else y
            pc = 1 - c if d & 1 else c
            for t, tr in enumerate(self.transfers):
                if tr.same_core and d & 1:
                    continue
                peer, mine = (2 * px + py, 2 * x + y) if tr.same_core else (4 * px + 2 * py + pc, me)
                k = t * (N_DEV - 1) + d - 1
                copies.append(pltpu.make_async_remote_copy(
                    src_ref=tr.src_at(srcs[t], peer), dst_ref=tr.dst_at(dsts[t], mine),
                    send_sem=send_sems.at[k], recv_sem=recv_sems.at[k],
                    device_id=(px, py, pc), device_id_type=pl.DeviceIdType.MESH))
        own = []
        for t, tr in enumerate(self.transfers):
            mine = 2 * x + y if tr.same_core else me
            own.append(pltpu.make_async_copy(tr.src_at(srcs[t], mine), tr.dst_at(dsts[t], mine), local_sems.at[t]))
        return copies, own

    def start(self, srcs, dsts, *sems):
        if self.relay:
            first, _, own = self._relay_copies(srcs, dsts, *sems)
            for cp in own + [cp for per_t in first for cp in per_t]:
                cp.start()
            return
        copies, own = self._copies(srcs, dsts, *sems)
        for cp in own + copies:
            cp.start()

    def finish(self, srcs, dsts, *sems):
        if self.relay:
            first, passed, own = self._relay_copies(srcs, dsts, *sems)
            for j in range(3):
                for t in range(self.n):
                    first[t][1 + j].wait_recv()
                    passed[t][j].start()
            for t in range(self.n):
                first[t][0].wait_recv()
                for cp in passed[t]:
                    cp.wait_recv()
            for t in range(self.n):
                for cp in first[t] + passed[t]:
                    cp.wait_send()
                own[t].wait()
            return
        copies, own = self._copies(srcs, dsts, *sems)
        for cp in copies + own:
            cp.wait()


def _pcall(body, *, name, grid, in_specs, out_specs, out_shape, scratch_shapes, sem, args, comm=None):
    in_specs, out_specs, out_shape = list(in_specs), list(out_specs), list(out_shape)
    if comm is None:
        outs = pl.pallas_call(body, name=name, grid=grid, in_specs=in_specs, out_specs=out_specs, out_shape=out_shape,
                              scratch_shapes=list(scratch_shapes), compiler_params=_cparams(sem))(*args)
        return list(outs), []
    n_in, n_out, n_scr, k = len(in_specs), len(out_specs), len(scratch_shapes), comm.n

    def carrier(*refs):
        ins, cin = refs[:n_in], refs[n_in:n_in + k]
        outs, cout = refs[n_in + k:n_in + k + n_out], refs[n_in + k + n_out:n_in + 2 * k + n_out]
        scr, sems = refs[n_in + 2 * k + n_out:n_in + 2 * k + n_out + n_scr], refs[n_in + 2 * k + n_out + n_scr:]
        ids = [pl.program_id(d) for d in range(len(grid))]
        first = functools.reduce(jnp.logical_and, [i == 0 for i in ids])
        last = functools.reduce(jnp.logical_and, [i == g - 1 for i, g in zip(ids, grid)])

        @pl.when(first)
        def _():
            comm.start(cin, cout, *sems)

        body(*ins, *outs, *scr)

        @pl.when(last)
        def _():
            comm.finish(cin, cout, *sems)

    hbm = pl.BlockSpec(memory_space=pl.ANY)
    outs = pl.pallas_call(
        carrier, name=name, grid=grid, in_specs=in_specs + [hbm] * k, out_specs=out_specs + [hbm] * k,
        out_shape=out_shape + comm.out_shape, scratch_shapes=list(scratch_shapes) + comm.scratch,
        compiler_params=_cparams(tuple("arbitrary" for _ in grid)),
    )(*args, *comm.arrays)
    return list(outs[:n_out]), list(outs[n_out:])


def _exchange(transfers, name, relay=False):
    comm = _Comm(transfers, relay)

    def body(*refs):
        k = comm.n
        comm.start(refs[:k], refs[k:2 * k], *refs[2 * k:])
        comm.finish(refs[:k], refs[k:2 * k], *refs[2 * k:])

    hbm = pl.BlockSpec(memory_space=pl.ANY)
    return pl.pallas_call(body, name=name, out_shape=comm.out_shape, in_specs=[hbm] * comm.n, out_specs=[hbm] * comm.n,
                          scratch_shapes=comm.scratch)(*comm.arrays)


def _pair_reduce(chunks, name, tr=320):
    _, r, c = chunks.shape
    tr = min(tr, r)
    assert r % tr == 0

    def swap(src_ref, dst_ref, send_sems, recv_sems):
        x, y, core = lax.axis_index("x"), lax.axis_index("y"), lax.axis_index("c")
        copies = [pltpu.make_async_remote_copy(
            src_ref=src_ref.at[2 * k + 1 - core], dst_ref=dst_ref.at[k], send_sem=send_sems.at[k],
            recv_sem=recv_sems.at[k], device_id=(x, y, 1 - core), device_id_type=pl.DeviceIdType.MESH) for k in range(4)]
        for cp in copies:
            cp.start()
        for cp in copies:
            cp.wait()

    hbm = pl.BlockSpec(memory_space=pl.ANY)
    theirs = pl.pallas_call(swap, name=name + "_swap", out_shape=jax.ShapeDtypeStruct((4, r, c), chunks.dtype),
                            in_specs=[hbm], out_specs=hbm,
                            scratch_shapes=[pltpu.SemaphoreType.DMA((4,)), pltpu.SemaphoreType.DMA((4,))])(chunks)

    def add(mine_ref, theirs_ref, out_ref):
        core = lax.axis_index("c")
        both = mine_ref[...].astype(F32)
        out_ref[...] = (jnp.where(core == 0, both[0], both[1]) + theirs_ref[...].astype(F32)).astype(out_ref.dtype)

    return pl.pallas_call(
        add, name=name + "_add", grid=(4, r // tr), out_shape=jax.ShapeDtypeStruct((4, r, c), chunks.dtype),
        in_specs=[pl.BlockSpec((None, 2, tr, c), lambda k, i: (k, 0, i, 0)), pl.BlockSpec((None, tr, c), lambda k, i: (k, i, 0))],
        out_specs=pl.BlockSpec((None, tr, c), lambda k, i: (k, i, 0)),
        compiler_params=_cparams(("parallel", "parallel")),
    )(chunks.reshape(4, 2, r, c), theirs)


def _chip_scatter_transfer(pairs):
    return _Transfer(pairs, pairs.shape, lambda ref, p: ref.at[p], lambda ref, s: ref.at[s], same_core=True)


def _matmul(a, b, mode, *, name, tm, tn, tk, epi=None, extra=(), out_dtype=F32, chunks=None, comm=None):
    pieces = list(a) if isinstance(a, (list, tuple)) else [a]
    rows_a, cols_a = pieces[0].shape[0], sum(p.shape[1] for p in pieces)
    if mode == "nn":
        (M, K), N = (rows_a, cols_a), b.shape[1]
    elif mode == "nt":
        (M, K), N = (rows_a, cols_a), b.shape[0]
    else:
        (K, M), N = (rows_a, cols_a), b.shape[1]
    tm, tn, tk = min(tm, M), min(tn, N), min(tk, K)
    assert M % tm == 0 and N % tn == 0 and K % tk == 0 and (epi != "ln" or tn == N), (name, M, N, K)
    nk = K // tk
    tile_cols, axis = (tm, 0) if mode == "tn" else (tk, 2)
    assert all(p.shape[1] % tile_cols == 0 for p in pieces)
    counts = [p.shape[1] // tile_cols for p in pieces]
    starts = [sum(counts[:q]) for q in range(len(pieces))]

    def a_spec_of(q):
        at = lambda t: jnp.clip(t - starts[q], 0, counts[q] - 1) if len(pieces) > 1 else t
        return {"nn": pl.BlockSpec((tm, tk), lambda i, j, k: (i, at(k))),
                "nt": pl.BlockSpec((tm, tk), lambda i, j, k: (i, at(k))),
                "tn": pl.BlockSpec((tk, tm), lambda i, j, k: (k, at(i)))}[mode]

    n_a = len(pieces)
    b_mode = pl.Buffered(1) if (nk == 1 and tn == N and n_a > 1) else None
    b_spec = {"nn": pl.BlockSpec((tk, tn), lambda i, j, k: (k, j), pipeline_mode=b_mode),
              "nt": pl.BlockSpec((tn, tk), lambda i, j, k: (j, k), pipeline_mode=b_mode),
              "tn": pl.BlockSpec((tk, tn), lambda i, j, k: (k, j), pipeline_mode=b_mode)}[mode]
    ca, cb = {"nn": (1, 0), "nt": (1, 1), "tn": (0, 0)}[mode]
    tile = pl.BlockSpec((tm, tn), lambda i, j, k: (i, j))
    row = pl.BlockSpec((1, tn), lambda i, j, k: (0, j))
    n_extra = {None: 0, "add": 1, "relu2": 0, "drelu2": 1, "ln": 3}[epi]
    assert len(extra) == n_extra
    extra_specs = {None: [], "add": [tile], "relu2": [], "drelu2": [tile], "ln": [tile, row, row]}[epi]
    split = 0
    if epi == "relu2":
        out_shape, out_specs = (jax.ShapeDtypeStruct((M, N), BF16),), (tile,)
    elif epi == "ln":
        out_shape = (jax.ShapeDtypeStruct((M, N), F32), jax.ShapeDtypeStruct((M, N), F32),
                     jax.ShapeDtypeStruct((M, N), BF16))
        out_specs = (tile, tile, tile)
    elif chunks == "cols":
        c = N // N_DEV
        out_shape = (jax.ShapeDtypeStruct((N_DEV, M, c), out_dtype),)
        if tn == N:
            split = c
            out_specs = (pl.BlockSpec((N_DEV, tm, c), lambda i, j, k: (0, i, 0)),)
        else:
            assert c % tn == 0
            out_specs = (pl.BlockSpec((None, tm, tn), lambda i, j, k: (j // (c // tn), i, j % (c // tn))),)
    else:
        out_shape, out_specs = (jax.ShapeDtypeStruct((M, N), out_dtype),), (tile,)
    n_out = len(out_shape)

    def body(*refs):
        a_refs, b_ref = refs[:n_a], refs[n_a]
        ex = refs[n_a + 1:n_a + 1 + n_extra]
        outs = refs[n_a + 1 + n_extra:n_a + 1 + n_extra + n_out]
        acc_ref = refs[-1]
        k = pl.program_id(2)

        def finish(acc):
            if epi == "add":
                outs[0][...] = (acc + ALPHA * ex[0][...]).astype(out_dtype)
            elif epi == "relu2":
                r = jnp.maximum(acc, 0.0)
                outs[0][...] = _bf(r * r)
            elif epi == "drelu2":
                outs[0][...] = (acc * (2.0 * jnp.sqrt(ex[0][...].astype(F32)))).astype(out_dtype)
            elif epi == "ln":
                u = ALPHA * ex[0][...] + acc
                xh, _ = _norm_stats(u)
                y = xh * ex[1][...] + ex[2][...]
                outs[0][...] = u
                outs[1][...] = y
                outs[2][...] = _bf(y)
            elif split:
                for p in range(N_DEV):
                    outs[0][p] = acc[:, p * split:(p + 1) * split].astype(out_dtype)
            else:
                outs[0][...] = acc.astype(out_dtype)

        def step(a_ref, first, middle, last):
            part = _dg(_bf(a_ref[...]), _bf(b_ref[...]), ca, cb)
            if nk == 1:
                finish(part)
                return
            if first:
                @pl.when(k == 0)
                def _():
                    acc_ref[...] = part

            if middle:
                @pl.when(jnp.logical_and(k > 0, k < nk - 1))
                def _():
                    acc_ref[...] += part

            if last:
                @pl.when(k == nk - 1)
                def _():
                    finish(acc_ref[...] + part)

        if n_a == 1:
            step(a_refs[0], True, True, True)
        else:
            t = pl.program_id(axis)
            for q in range(n_a):
                along_k = axis == 2
                first = not along_k or starts[q] == 0
                last = not along_k or starts[q] + counts[q] == nk
                middle = not along_k or counts[q] > int(first) + int(last)

                @pl.when(jnp.logical_and(t >= starts[q], t < starts[q] + counts[q]))
                def _(q=q, first=first, middle=middle, last=last):
                    step(a_refs[q], first, middle, last)

    outs, landed = _pcall(
        body, name=name, out_shape=out_shape, grid=(M // tm, N // tn, nk),
        in_specs=[a_spec_of(q) for q in range(n_a)] + [b_spec] + extra_specs, out_specs=out_specs,
        scratch_shapes=[pltpu.VMEM((tm, tn) if nk > 1 else (8, 128), F32)], sem=("parallel", "parallel", "arbitrary"),
        args=(*pieces, b, *extra), comm=comm)
    res = outs[0] if n_out == 1 else tuple(outs)
    if chunks == "rows":
        res = res.reshape(N_DEV, M // N_DEV, N)
    return res if comm is None else (res, landed)


def _matmul_rows_of(pieces, b, res, *, name, tm, comm=None):
    M, (K, N) = pieces[0].shape[0], b.shape
    tm = min(tm, M)
    subs, start = [], 0
    for q, p in enumerate(pieces):
        w = p.shape[1]
        step = w if start % w == 0 else 512
        assert w % step == 0 and start % step == 0
        subs += [(q, off, step, start + off) for off in range(0, w, step)]
        start += w
    assert start == K
    n_p, n_s = len(pieces), len(subs)

    def body(*refs):
        a_refs, b_refs, res_ref, out_ref = refs[:n_p], refs[n_p:n_p + n_s], refs[n_p + n_s], refs[n_p + n_s + 1]
        acc = None
        for (q, off, w, _), b_ref in zip(subs, b_refs):
            part = _dg(_bf(a_refs[q][:, off:off + w]), _bf(b_ref[...]), 1, 0)
            acc = part if acc is None else acc + part
        out_ref[...] = acc + ALPHA * res_ref[...]

    tile = pl.BlockSpec((tm, N), lambda i: (i, 0))
    outs, landed = _pcall(
        body, name=name, grid=(M // tm,), out_shape=[jax.ShapeDtypeStruct((M, N), F32)],
        in_specs=[pl.BlockSpec((tm, p.shape[1]), lambda i: (i, 0)) for p in pieces] +
                 [pl.BlockSpec((w, N), lambda i, r=row // w: (r, 0), pipeline_mode=pl.Buffered(1))
                  for _, _, w, row in subs] + [tile],
        out_specs=[tile], scratch_shapes=[], sem=("parallel",), args=(*pieces, *([b] * n_s), res), comm=comm)
    return outs[0] if comm is None else (outs[0], landed)


def _ret_tables(S):
    half = 64
    inv_freq = ROPE_BASE ** (-jnp.arange(half, dtype=F32) / half)
    ang = jnp.arange(S, dtype=jnp.int32).astype(F32)[:, None] * inv_freq[None, :]
    cos, sin = jnp.cos(ang), jnp.sin(ang)
    cosf = jnp.concatenate([cos, cos], axis=1)
    sinf = jnp.concatenate([-sin, sin], axis=1)
    log_g = jnp.log(1.0 - 2.0 ** (-5.0 - jnp.arange(4, dtype=F32)))
    idx = jnp.arange(CHUNK, dtype=F32)
    diff = idx[:, None] - idx[None, :]
    md = jnp.where(diff[None] >= 0, jnp.exp(log_g[:, None, None] * diff[None]), 0.0)
    kd = jnp.exp(log_g[:, None] * (CHUNK - 1 - idx)[None, :])
    qd = jnp.exp(log_g[:, None] * (idx + 1.0)[None, :])
    cd = jnp.exp(log_g * CHUNK)
    bc = lambda t: jnp.broadcast_to(t[:, :, None], (4, CHUNK, CHUNK))
    return cosf, sinf, md, bc(qd), bc(kd), jnp.broadcast_to(cd[:, None, None], (4, 8, CHUNK))


def _rot(x, cosf, sinf):
    return x * cosf + pltpu.roll(x, 64, 1) * sinf


def _rot_t(dx, cosf, sinf):
    return dx * cosf - pltpu.roll(dx, 64, 1) * sinf


RET_CHUNKS = 2


def _ret_specs(rev, S):
    R = min(RET_CHUNKS, S // CHUNK)
    rows, steps = R * CHUNK, S // (R * CHUNK)
    rn = (lambda n: steps - 1 - n) if rev else (lambda n: n)
    col = lambda c: pl.BlockSpec((rows, 512), lambda n, c=c: (rn(n), c))
    tab = pl.BlockSpec((rows, CHUNK), lambda n: (rn(n), 0))
    dec = pl.BlockSpec((4, CHUNK, CHUNK), lambda n: (0, 0, 0))
    cdec = pl.BlockSpec((4, 8, CHUNK), lambda n: (0, 0, 0))
    vec = pl.BlockSpec((1, 512), lambda n: (0, 0))
    st = pl.BlockSpec((R, 4, CHUNK, CHUNK), lambda n: (rn(n), 0, 0, 0))
    return R, steps, rn, col, tab, dec, cdec, vec, st


def _ret_fwd(proj, tables, gn_g, gn_b):
    S = proj.shape[0]
    R, steps, _, col, tab, dec, cdec, vec, st = _ret_specs(False, S)

    def body(q_ref, k_ref, v_ref, g_ref, cos_ref, sin_ref, md_ref, qd_ref, kd_ref, cd_ref, gng_ref, gnb_ref,
             out_ref, st_ref, state):
        @pl.when(pl.program_id(0) == 0)
        def _():
            state[...] = jnp.zeros_like(state)

        tiles = [(c, h) for c in range(R) for h in range(4)]
        rs = lambda c: slice(c * CHUNK, (c + 1) * CHUNK)
        sl = lambda h: slice(h * 128, (h + 1) * 128)
        qr = [_rot(q_ref[rs(c), sl(h)].astype(F32), cos_ref[rs(c), :], sin_ref[rs(c), :]) for c, h in tiles]
        kr = [_rot(k_ref[rs(c), sl(h)].astype(F32), cos_ref[rs(c), :], sin_ref[rs(c), :]) * (128 ** -0.5)
              for c, h in tiles]
        vb = [_bf(v_ref[rs(c), sl(h)]) for c, h in tiles]
        kv = [_dg(_bf(k * kd_ref[h]), v, 0, 0) for k, v, (c, h) in zip(kr, vb, tiles)]
        before = {}
        for h in range(4):
            s_h = state[h]
            for c in range(R):
                st_ref[c, h] = s_h
                before[(c, h)] = s_h
                s_h = s_h * cd_ref[h, 0:1, :] + kv[c * 4 + h]
            state[h] = s_h
        sc = [_dg(_bf(q), _bf(k), 1, 1) * md_ref[h] for q, k, (c, h) in zip(qr, kr, tiles)]
        r = [_dg(_bf(x), v, 1, 0) + _dg(_bf(q * qd_ref[h]), _bf(before[(c, h)]), 1, 0)
             for x, v, q, (c, h) in zip(sc, vb, qr, tiles)]
        for x, (c, h) in zip(r, tiles):
            y, _ = _norm_stats(x)
            rg = g_ref[rs(c), sl(h)].astype(F32)
            out_ref[rs(c), sl(h)] = rg * _sigmoid(rg) * (y * gng_ref[:, sl(h)] + gnb_ref[:, sl(h)])

    return pl.pallas_call(
        body, name="ret_fwd", grid=(steps,),
        out_shape=(jax.ShapeDtypeStruct((S, RET_W), F32), jax.ShapeDtypeStruct((S // CHUNK, 4, CHUNK, CHUNK), F32)),
        in_specs=[col(0), col(1), col(2), col(3), tab, tab, dec, dec, dec, cdec, vec, vec],
        out_specs=(pl.BlockSpec((R * CHUNK, 512), lambda n: (n, 0)), st),
        scratch_shapes=[pltpu.VMEM((4, CHUNK, CHUNK), F32)],
        compiler_params=_cparams(("arbitrary",)),
    )(proj, proj, proj, proj, *tables, gn_g, gn_b)


def _ret_bwd(proj, tables, gn_g, gn_b, states, d_out):
    S = proj.shape[0]
    R, steps, rn, col, tab, dec, cdec, vec, st = _ret_specs(True, S)

    def kernel_body(q_ref, k_ref, v_ref, g_ref, cos_ref, sin_ref, md_ref, qd_ref, kd_ref, cd_ref, gng_ref, gnb_ref,
                    st_ref, do_ref, dp_ref, dg_ref, db_ref, gstate):
        @pl.when(pl.program_id(0) == 0)
        def _():
            gstate[...] = jnp.zeros_like(gstate)
            dg_ref[...] = jnp.zeros_like(dg_ref)
            db_ref[...] = jnp.zeros_like(db_ref)

        tiles = [(c, h) for c in range(R) for h in range(4)]
        rs = lambda c: slice(c * CHUNK, (c + 1) * CHUNK)
        sl = lambda h: slice(h * 128, (h + 1) * 128)
        rot = lambda ref, c, h: _rot(ref[rs(c), sl(h)].astype(F32), cos_ref[rs(c), :], sin_ref[rs(c), :])
        qr = [rot(q_ref, c, h) for c, h in tiles]
        kr = [rot(k_ref, c, h) * (128 ** -0.5) for c, h in tiles]
        qb, kb = [_bf(x) for x in qr], [_bf(x) for x in kr]
        vb = [_bf(v_ref[rs(c), sl(h)]) for c, h in tiles]
        s0b = [_bf(st_ref[c, h]) for c, h in tiles]
        scb = [_bf(_dg(q, k, 1, 1) * md_ref[h]) for q, k, (c, h) in zip(qb, kb, tiles)]
        qdb = [_bf(q * qd_ref[h]) for q, (c, h) in zip(qr, tiles)]
        kdb = [_bf(k * kd_ref[h]) for k, (c, h) in zip(kr, tiles)]
        r = [_dg(x, v, 1, 0) + _dg(q, s, 1, 0) for x, v, q, s in zip(scb, vb, qdb, s0b)]
        drb, d_rg = [], []
        for x, (c, h) in zip(r, tiles):
            y, rstd = _norm_stats(x)
            gng = gng_ref[:, sl(h)]
            rg = g_ref[rs(c), sl(h)].astype(F32)
            sg = _sigmoid(rg)
            d_o = do_ref[rs(c), sl(h)]
            d_gn = d_o * (rg * sg)
            dg_ref[:, sl(h)] += jnp.sum(d_gn * y, axis=0, keepdims=True)
            db_ref[:, sl(h)] += jnp.sum(d_gn, axis=0, keepdims=True)
            drb.append(_bf(_norm_bwd(d_gn * gng, y, rstd)))
            d_rg.append(_bf(d_o * (y * gng + gnb_ref[:, sl(h)]) * (sg * (1.0 + rg * (1.0 - sg)))))
        grow = [_dg(q, d, 0, 0) for q, d in zip(qdb, drb)]
        after = {}
        for h in range(4):
            g_h = gstate[h]
            for c in reversed(range(R)):
                after[(c, h)] = _bf(g_h)
                g_h = g_h * cd_ref[h, 0:1, :] + grow[c * 4 + h]
            gstate[h] = g_h
        dscb = [_bf(_dg(d, v, 1, 1) * md_ref[h]) for d, v, (c, h) in zip(drb, vb, tiles)]
        for t, (c, h) in enumerate(tiles):
            gb = after[(c, h)]
            dqr = _dg(dscb[t], kb[t], 1, 0) + _dg(drb[t], s0b[t], 1, 1) * qd_ref[h]
            dkr = _dg(dscb[t], qb[t], 0, 0) + _dg(vb[t], gb, 1, 1) * kd_ref[h]
            dv = _dg(scb[t], drb[t], 0, 0) + _dg(kdb[t], gb, 1, 0)
            cosf, sinf = cos_ref[rs(c), :], sin_ref[rs(c), :]
            dp_ref[rs(c), 0 * 512 + h * 128:0 * 512 + (h + 1) * 128] = _bf(_rot_t(dqr, cosf, sinf))
            dp_ref[rs(c), 1 * 512 + h * 128:1 * 512 + (h + 1) * 128] = _bf(_rot_t(dkr, cosf, sinf) * (128 ** -0.5))
            dp_ref[rs(c), 2 * 512 + h * 128:2 * 512 + (h + 1) * 128] = _bf(dv)
            dp_ref[rs(c), 3 * 512 + h * 128:3 * 512 + (h + 1) * 128] = d_rg[t]

    acc = pl.BlockSpec((1, 512), lambda n: (0, 0))
    return pl.pallas_call(
        kernel_body, name="ret_bwd", grid=(steps,),
        out_shape=(jax.ShapeDtypeStruct((S, 2048), BF16), jax.ShapeDtypeStruct((1, 512), F32),
                   jax.ShapeDtypeStruct((1, 512), F32)),
        in_specs=[col(0), col(1), col(2), col(3), tab, tab, dec, dec, dec, cdec, vec, vec, st,
                  pl.BlockSpec((R * CHUNK, 512), lambda n: (rn(n), 0))],
        out_specs=(pl.BlockSpec((R * CHUNK, 2048), lambda n: (rn(n), 0)), acc, acc),
        scratch_shapes=[pltpu.VMEM((4, CHUNK, CHUNK), F32)],
        compiler_params=_cparams(("arbitrary",)),
    )(proj, proj, proj, proj, *tables, gn_g, gn_b, states, d_out)


SB_T = 256
SB_SCALE = 64 ** -0.5
SB_Q_COL, SB_K_COL, SB_V_COL = 2048 // 128, 2560 // 128, 3072 // 128


def _head_masks():
    lane = lax.broadcasted_iota(jnp.int32, (1, 128), 1)
    m0 = (lane < 64).astype(F32)
    return m0, 1.0 - m0


def _tri(n, cmp):
    r = lax.broadcasted_iota(jnp.int32, (n, n), 0)
    c = lax.broadcasted_iota(jnp.int32, (n, n), 1)
    return cmp(r, c)


def _tri_sum(x, tri):
    hi = _bf(x)
    lo = _bf(x - hi.astype(F32))
    return _dg(hi, tri, 1, 0) + _dg(lo, tri, 1, 0)


def _sb_weights(qms, kblks, upper, carry, causal):
    tiles = [(b, h) for b in range(len(kblks)) for h in range(2)]
    zs = [_dg(qms[h], kblks[b], 1, 1) for b, h in tiles]
    lgs = [-(jnp.maximum(z, 0.0) + jnp.log(1.0 + jnp.exp(-jnp.abs(z)))) for z in zs]
    if causal is not None:
        lgs = [jnp.where(causal, lg, 0.0) if b == 0 else lg for lg, (b, h) in zip(lgs, tiles)]
    carries = list(carry)
    for t in range(len(tiles) - 2):
        carries.append(carries[t] + jnp.sum(lgs[t], axis=1, keepdims=True))
    his = [_bf(lg) for lg in lgs]
    los = [_bf(lg - hi.astype(F32)) for lg, hi in zip(lgs, his)]
    later = [_dg(hi, upper, 1, 0) for hi in his]
    later = [r + _dg(lo, upper, 1, 0) for r, lo in zip(later, los)]
    a = [jnp.exp(lg + z + (r + c)) for lg, z, r, c in zip(lgs, zs, later, carries)]
    if causal is not None:
        a = [jnp.where(causal, x, 0.0) if b == 0 else x for x, (b, h) in zip(a, tiles)]
    out = tuple(carries[t] + jnp.sum(lgs[t], axis=1, keepdims=True) for t in (len(tiles) - 2, len(tiles) - 1))
    return [a[2 * b:2 * b + 2] for b in range(len(kblks))], out


def _sb_fwd(proj, comm=None):
    S = proj.shape[0]
    T = min(SB_T, S)
    nq = S // T

    def body(q_ref, k_ref, v_ref, o_ref, a_ref, kb_ref, vm_ref, acc_ref):
        i = pl.program_id(1)
        m0, m1 = _head_masks()

        @pl.when(i == 0)
        def _():
            v = v_ref[...]
            kb_ref[...] = _bf(k_ref[...])
            vm_ref[0] = _bf(v * m0)
            vm_ref[1] = _bf(v * m1)

        q = q_ref[...]
        qm = (_bf(q * (m0 * SB_SCALE)), _bf(q * (m1 * SB_SCALE)))
        upper = _tri(T, lambda r, c: r > c).astype(BF16)
        causal = _tri(T, lambda r, c: c < r)

        def tiles(js, carry, mask, first):
            ks = [pl.multiple_of(j * T, T) for j in js]
            a, out = _sb_weights(qm, [kb_ref[pl.ds(k, T), :] for k in ks], upper, carry, mask)
            a = [[_bf(t) for t in per_block] for per_block in a]
            for b, j in enumerate(js):
                for h in range(2):
                    a_ref[h, j] = a[b][h]
            parts = [_dg(a[b][h], vm_ref[h, pl.ds(k, T), :], 1, 0) for b, k in enumerate(ks) for h in range(2)]
            part = functools.reduce(lambda u, w: u + w, parts)
            if first:
                acc_ref[...] = part
            else:
                acc_ref[...] += part
            return out

        zero = jnp.zeros((T, 1), F32)
        carry = lax.cond(i == 0, lambda: tiles([i], (zero, zero), causal, True),
                         lambda: tiles([i, i - 1], (zero, zero), causal, True))
        n = jnp.maximum(i - 1, 0)
        carry = lax.fori_loop(0, n % 2, lambda _, c: tiles([n - 1], c, None, False), carry)
        top = n - 1 - n % 2
        carry = lax.fori_loop(0, (n // 2) % 2, lambda _, c: tiles([top, top - 1], c, None, False), carry)
        top = top - 2 * ((n // 2) % 2)
        lax.fori_loop(0, n // 4, lambda jj, c: tiles([top - 4 * jj - b for b in range(4)], c, None, False), carry)
        o_ref[...] = acc_ref[...]

    full = lambda c: pl.BlockSpec((S, 128), lambda p, i, c=c: (0, c + p))
    outs, landed = _pcall(
        body, name="sb_fwd", grid=(4, nq),
        out_shape=[jax.ShapeDtypeStruct((S, SB_W), F32), jax.ShapeDtypeStruct((4, 2, nq, nq, T, T), BF16)],
        in_specs=[pl.BlockSpec((T, 128), lambda p, i: (i, SB_Q_COL + p)), full(SB_K_COL), full(SB_V_COL)],
        out_specs=[pl.BlockSpec((T, 128), lambda p, i: (i, p)),
                   pl.BlockSpec((None, 2, None, nq, T, T), lambda p, i: (p, 0, i, 0, 0, 0))],
        scratch_shapes=[pltpu.VMEM((S, 128), BF16), pltpu.VMEM((2, S, 128), BF16), pltpu.VMEM((T, 128), F32)],
        sem=("arbitrary", "arbitrary"), args=(proj, proj, proj), comm=comm)
    return tuple(outs) if comm is None else (tuple(outs), landed)


def _sb_bwd(proj, a_saved, d_o, comm=None):
    S = proj.shape[0]
    T = min(SB_T, S)
    nq = S // T

    def body(q_ref, k_ref, v_ref, do_ref, a_ref, dq_ref, dk_ref, dv_ref, kb_ref, kbm_ref, vb_ref, dq_acc, dk_acc, dv_acc):
        i = pl.program_id(1)
        m0, m1 = _head_masks()

        @pl.when(i == 0)
        def _():
            k = k_ref[...]
            kb_ref[...] = _bf(k)
            kbm_ref[0] = _bf(k * m0)
            kbm_ref[1] = _bf(k * m1)
            vb_ref[...] = _bf(v_ref[...])
            dk_acc[...] = jnp.zeros_like(dk_acc)
            dv_acc[...] = jnp.zeros_like(dv_acc)

        q, d_out = q_ref[...], do_ref[...]
        qm = (_bf(q * (m0 * SB_SCALE)), _bf(q * (m1 * SB_SCALE)))
        dom = (_bf(d_out * m0), _bf(d_out * m1))
        lower = _tri(T, lambda r, c: r < c).astype(BF16)
        causal = _tri(T, lambda r, c: c < r)

        def up(js, carry, mask):
            ks = [pl.multiple_of(j * T, T) for j in js]
            tiles = [(b, h) for b in range(len(js)) for h in range(2)]
            zs = [_dg(qm[h], kb_ref[pl.ds(ks[b], T), :], 1, 1) for b, h in tiles]
            a = [a_ref[h, js[b]] for b, h in tiles]
            es = [w.astype(F32) * _dg(dom[h], vb_ref[pl.ds(ks[b], T), :], 1, 1) for w, (b, h) in zip(a, tiles)]
            carries = list(carry)
            for t in range(len(tiles)):
                carries.append(carries[t] + jnp.sum(es[t], axis=1, keepdims=True))
            d_lg = [_dg(_bf(e), lower, 1, 0) + c for e, c in zip(es, carries)]
            ens = [jnp.exp(-jnp.abs(z)) for z in zs]
            invs = [1.0 / (1.0 + en) for en in ens]
            betas = [jnp.where(z >= 0.0, inv, en * inv) for z, en, inv in zip(zs, ens, invs)]
            dzs = [e * (1.0 - b) - d * b for e, b, d in zip(es, betas, d_lg)]
            if mask is not None:
                dzs = [jnp.where(mask, dz, 0.0) if b == len(js) - 1 else dz for dz, (b, h) in zip(dzs, tiles)]
            dzs = [_bf(dz) for dz in dzs]
            parts = [_dg(dzs[t], kbm_ref[h, pl.ds(ks[b], T), :], 1, 0) for t, (b, h) in enumerate(tiles)]
            dq_acc[...] += functools.reduce(lambda u, w: u + w, parts)
            for b, k in enumerate(ks):
                dk_acc[pl.ds(k, T), :] += _dg(dzs[2 * b], qm[0], 0, 0) + _dg(dzs[2 * b + 1], qm[1], 0, 0)
                dv_acc[pl.ds(k, T), :] += _dg(a[2 * b], dom[0], 0, 0) + _dg(a[2 * b + 1], dom[1], 0, 0)
            return tuple(carries[-2:])

        zero = jnp.zeros((T, 1), F32)
        dq_acc[...] = jnp.zeros_like(dq_acc)
        n = jnp.maximum(i - 1, 0)
        carry = lax.fori_loop(0, n // 4, lambda jj, c: up([4 * jj + b for b in range(4)], c, None), (zero, zero))
        done = 4 * (n // 4)
        carry = lax.fori_loop(0, (n // 2) % 2, lambda _, c: up([done, done + 1], c, None), carry)
        carry = lax.fori_loop(0, n % 2, lambda _, c: up([n - 1], c, None), carry)

        @pl.when(i == 0)
        def _():
            up([i], carry, causal)

        @pl.when(i > 0)
        def _():
            up([i - 1, i], carry, causal)

        dq_ref[...] = _bf(dq_acc[...] * SB_SCALE)

        @pl.when(i == nq - 1)
        def _():
            dk_ref[...] = _bf(dk_acc[...])
            dv_ref[...] = _bf(dv_acc[...])

    full = lambda c: pl.BlockSpec((S, 128), lambda p, i, c=c: (0, c + p))
    tile = pl.BlockSpec((T, 128), lambda p, i: (i, p))
    acc = pl.BlockSpec((S, 128), lambda p, i: (0, p))
    out = jax.ShapeDtypeStruct((S, SB_W), BF16)
    outs, landed = _pcall(
        body, name="sb_bwd", grid=(4, nq), out_shape=[out, out, out],
        in_specs=[pl.BlockSpec((T, 128), lambda p, i: (i, SB_Q_COL + p)), full(SB_K_COL), full(SB_V_COL), tile,
                  pl.BlockSpec((None, 2, None, nq, T, T), lambda p, i: (p, 0, i, 0, 0, 0))],
        out_specs=[tile, acc, acc],
        scratch_shapes=[pltpu.VMEM((S, 128), BF16), pltpu.VMEM((2, S, 128), BF16), pltpu.VMEM((S, 128), BF16),
                        pltpu.VMEM((T, 128), F32), pltpu.VMEM((S, 128), F32), pltpu.VMEM((S, 128), F32)],
        sem=("arbitrary", "arbitrary"), args=(proj, proj, proj, d_o, a_saved), comm=comm)
    return tuple(outs) if comm is None else (tuple(outs), landed)


SGU_U_COL, SGU_V_COL = 3584 // 512, 4096 // 512


def _causal(w):
    r = lax.broadcasted_iota(jnp.int32, (CHUNK, CHUNK), 0)
    c = lax.broadcasted_iota(jnp.int32, (CHUNK, CHUNK), 1)
    return jnp.where(r >= c, w, 0.0)


SGU_CHUNKS = 4


def _sgu_fwd(proj, ln_g, ln_b, w, b):
    S = proj.shape[0]
    R = min(SGU_CHUNKS, S // CHUNK)
    rows = R * CHUNK

    def body(u_ref, v_ref, g_ref, b_ref, w_ref, bias_ref, out_ref):
        wc = [_bf(_causal(w_ref[g])) for g in range(4)]
        for r in range(R):
            rs = slice(r * CHUNK, (r + 1) * CHUNK)
            u = _gelu(u_ref[rs, :].astype(F32))
            xh, _ = _norm_stats(_gelu(v_ref[rs, :].astype(F32)))
            vn = _bf(xh * g_ref[...] + b_ref[...])
            for g in range(4):
                sl = slice(g * 128, (g + 1) * 128)
                out_ref[rs, sl] = u[:, sl] * (_dg(wc[g], vn[:, sl], 1, 0) + bias_ref[g])

    vec = pl.BlockSpec((1, 512), lambda n: (0, 0))
    return pl.pallas_call(
        body, name="sgu_fwd", grid=(S // rows,),
        out_shape=jax.ShapeDtypeStruct((S, SGU_W), F32),
        in_specs=[pl.BlockSpec((rows, 512), lambda n: (n, SGU_U_COL)),
                  pl.BlockSpec((rows, 512), lambda n: (n, SGU_V_COL)), vec, vec,
                  pl.BlockSpec((4, CHUNK, CHUNK), lambda n: (0, 0, 0)), pl.BlockSpec((4, CHUNK, 1), lambda n: (0, 0, 0))],
        out_specs=pl.BlockSpec((rows, 512), lambda n: (n, 0)),
        compiler_params=_cparams(("parallel",)),
    )(proj, proj, ln_g, ln_b, w, b)


def _sgu_bwd(proj, ln_g, ln_b, w, b, d_out):
    S = proj.shape[0]
    R = min(SGU_CHUNKS, S // CHUNK)
    rows = R * CHUNK

    def body(u_ref, v_ref, g_ref, b_ref, w_ref, bias_ref, do_ref, dp_ref, dg_ref, db_ref, dw_ref, dbias_ref):
        @pl.when(pl.program_id(0) == 0)
        def _():
            dg_ref[...] = jnp.zeros_like(dg_ref)
            db_ref[...] = jnp.zeros_like(db_ref)
            dw_ref[...] = jnp.zeros_like(dw_ref)
            dbias_ref[...] = jnp.zeros_like(dbias_ref)

        ln_gain = g_ref[...]
        wc = [_bf(_causal(w_ref[g])) for g in range(4)]
        for r in range(R):
            rs = slice(r * CHUNK, (r + 1) * CHUNK)
            gu, gv = u_ref[rs, :].astype(F32), v_ref[rs, :].astype(F32)
            u = _gelu(gu)
            xh, rstd = _norm_stats(_gelu(gv))
            vn = _bf(xh * ln_gain + b_ref[...])
            d_o = do_ref[rs, :]
            d_vn = []
            for g in range(4):
                sl = slice(g * 128, (g + 1) * 128)
                sv = _dg(wc[g], vn[:, sl], 1, 0) + bias_ref[g]
                dp_ref[rs, sl] = _bf(d_o[:, sl] * sv * _gelu_grad(gu[:, sl]))
                d_sv = d_o[:, sl] * u[:, sl]
                dbias_ref[g] += jnp.sum(d_sv, axis=1, keepdims=True)
                d_svb = _bf(d_sv)
                dw_ref[g] += _causal(_dg(d_svb, vn[:, sl], 1, 1))
                d_vn.append(_dg(wc[g], d_svb, 0, 0))
            d_vn = jnp.concatenate(d_vn, axis=1)
            dg_ref[...] += jnp.sum(d_vn * xh, axis=0, keepdims=True)
            db_ref[...] += jnp.sum(d_vn, axis=0, keepdims=True)
            dp_ref[rs, 512:1024] = _bf(_norm_bwd(d_vn * ln_gain, xh, rstd) * _gelu_grad(gv))

    vec = pl.BlockSpec((1, 512), lambda n: (0, 0))
    wspec = pl.BlockSpec((4, CHUNK, CHUNK), lambda n: (0, 0, 0))
    bspec = pl.BlockSpec((4, CHUNK, 1), lambda n: (0, 0, 0))
    return pl.pallas_call(
        body, name="sgu_bwd", grid=(S // rows,),
        out_shape=(jax.ShapeDtypeStruct((S, 1024), BF16), jax.ShapeDtypeStruct((1, 512), F32),
                   jax.ShapeDtypeStruct((1, 512), F32), jax.ShapeDtypeStruct((4, CHUNK, CHUNK), F32),
                   jax.ShapeDtypeStruct((4, CHUNK, 1), F32)),
        in_specs=[pl.BlockSpec((rows, 512), lambda n: (n, SGU_U_COL)),
                  pl.BlockSpec((rows, 512), lambda n: (n, SGU_V_COL)), vec, vec, wspec, bspec,
                  pl.BlockSpec((rows, 512), lambda n: (n, 0))],
        out_specs=(pl.BlockSpec((rows, 1024), lambda n: (n, 0)), vec, vec, wspec, bspec),
        compiler_params=_cparams(("arbitrary",)),
    )(proj, proj, ln_g, ln_b, w, b, d_out)


GATE_COL = 4608 // 512


def _merge_fwd(proj, branches, p_list, tm=512):
    S = proj.shape[0]
    tm = min(tm, S)

    def body(r_ref, s_ref, g_ref, pr_ref, ps_ref, pg_ref, gr_ref, gs_ref, gg_ref, m_ref, br_ref):
        acc = None
        for k, (x_ref, p_ref, gate_ref) in enumerate(((r_ref, pr_ref, gr_ref), (s_ref, ps_ref, gs_ref),
                                                      (g_ref, pg_ref, gg_ref))):
            br = _dg(_bf(x_ref[...]), _bf(p_ref[...]), 1, 0)
            br_ref[k] = _bf(br)
            term = _sigmoid(gate_ref[...].astype(F32)) * br
            acc = term if acc is None else acc + term
        m_ref[...] = _bf(acc)

    xs = pl.BlockSpec((tm, 512), lambda i, n: (i, 0))
    ps = pl.BlockSpec((512, 512), lambda i, n: (0, n))
    gate = lambda k: pl.BlockSpec((tm, 512), lambda i, n, k=k: (i, GATE_COL + 2 * k + n))
    return pl.pallas_call(
        body, name="merge_fwd", grid=(S // tm, 2),
        out_shape=(jax.ShapeDtypeStruct((S, D_MODEL), BF16), jax.ShapeDtypeStruct((3, S, D_MODEL), BF16)),
        in_specs=[xs, xs, xs, ps, ps, ps, gate(0), gate(1), gate(2)],
        out_specs=(pl.BlockSpec((tm, 512), lambda i, n: (i, n)), pl.BlockSpec((3, tm, 512), lambda i, n: (0, i, n))),
        compiler_params=_cparams(("parallel", "parallel")),
    )(*branches, *p_list, proj, proj, proj)


def _gate_bwd(proj, br, d_merged, tm=512):
    S = proj.shape[0]
    tm = min(tm, S)

    def body(dm_ref, br_ref, gr_ref, gs_ref, gg_ref, *out_refs):
        dm = dm_ref[...]
        for k, gate_ref in enumerate((gr_ref, gs_ref, gg_ref)):
            s = _sigmoid(gate_ref[...].astype(F32))
            out_refs[k][...] = _bf(dm * s)
            out_refs[3 + k][...] = _bf(dm * br_ref[k].astype(F32) * (s * (1.0 - s)))

    gate = lambda k: pl.BlockSpec((tm, 512), lambda i, n, k=k: (i, GATE_COL + 2 * k + n))
    three = pl.BlockSpec((3, tm, 512), lambda i, n: (0, i, n))
    tile = pl.BlockSpec((tm, 512), lambda i, n: (i, n))
    outs = pl.pallas_call(
        body, name="gate_bwd", grid=(S // tm, 2),
        out_shape=[jax.ShapeDtypeStruct((S, D_MODEL), BF16)] * 6,
        in_specs=[tile, three, gate(0), gate(1), gate(2)], out_specs=[tile] * 6,
        compiler_params=_cparams(("parallel", "parallel")),
    )(d_merged, br, proj, proj, proj)
    return outs[:3], outs[3:]


def _ln_bwd(dy, u, g, target=None, tm=256):
    S, D = u.shape
    tm = min(tm, S)
    loss = target is not None

    def body(*refs):
        dy_ref, u_ref, g_ref = refs[:3]
        du_ref, dub_ref, dg_ref, db_ref = refs[3 + loss:7 + loss]

        @pl.when(pl.program_id(0) == 0)
        def _():
            for acc_ref in refs[5 + loss:]:
                acc_ref[...] = jnp.zeros_like(acc_ref)

        dy_t = dy_ref[...]
        if loss:
            err = dy_t - refs[3][...]
            refs[-1][...] += jnp.sum(err * err, axis=0, keepdims=True)
            dy_t = err * (1.0 / D)
        xh, rstd = _norm_stats(u_ref[...])
        dg_ref[...] += jnp.sum(dy_t * xh, axis=0, keepdims=True)
        db_ref[...] += jnp.sum(dy_t, axis=0, keepdims=True)
        du = _norm_bwd(dy_t * g_ref[...], xh, rstd)
        du_ref[...] = du
        dub_ref[...] = _bf(du)

    tile = pl.BlockSpec((tm, D), lambda i: (i, 0))
    vec = pl.BlockSpec((1, D), lambda i: (0, 0))
    row = jax.ShapeDtypeStruct((1, D), F32)
    return pl.pallas_call(
        body, name="ln_bwd", grid=(S // tm,),
        out_shape=[jax.ShapeDtypeStruct((S, D), F32), jax.ShapeDtypeStruct((S, D), BF16)] + [row] * (2 + loss),
        in_specs=[tile, tile, vec] + [tile] * loss, out_specs=[tile, tile] + [vec] * (2 + loss),
        compiler_params=_cparams(("arbitrary",)),
    )(dy, u, g, *([target] if loss else []))


def _layer_fwd(x, x_bf, W, tables, sb_comm=None):
    proj = _matmul(x_bf, W["w_in_t"], "nt", name="proj", tm=1024, tn=768, tk=1024)
    retg, states = _ret_fwd(proj, tables, W["ret_gn_g"], W["ret_gn_b"])
    if sb_comm is None:
        sb, sb_a = _sb_fwd(proj)
    else:
        (sb, sb_a), landed = _sb_fwd(proj, comm=sb_comm[0])
        sb_comm[1](landed)
    sg = _sgu_fwd(proj, W["sgu_ln_g"], W["sgu_ln_b"], W["sgu_w"], W["sgu_b"])
    merged, br = _merge_fwd(proj, (retg, sb, sg), (W["p_ret"], W["p_sb"], W["p_sgu"]))
    u1, x1, x1_bf = _matmul(merged, W["w_out"], "nn", name="out_ln", tm=512, tn=1024, tk=1024, epi="ln",
                            extra=(x, W["ln1_g"], W["ln1_b"]))
    act = _matmul(x1_bf, W["w_up"], "nn", name="up", tm=1024, tn=1024, tk=1024, epi="relu2")
    u2, x2, x2_bf = _matmul(act, W["w_down"], "nn", name="down_ln", tm=512, tn=1024, tk=4096, epi="ln",
                            extra=(x1, W["ln2_g"], W["ln2_b"]))
    saved = dict(x_bf=x_bf, proj=proj, retg=retg, states=states, sb=sb, sb_a=sb_a, sg=sg, merged=merged, br=br, u1=u1,
                 x1_bf=x1_bf, act=act, u2=u2)
    return x2, x2_bf, saved


def _layer_bwd(d_x2, W, tables, sv, chunk_dtype=None, sb_comm_fn=None, dwin_comm_fn=None, dx_comm_fn=None, target=None):
    dt = F32 if chunk_dtype is None else chunk_dtype
    rows, cols = (None, None) if chunk_dtype is None else ("rows", "cols")
    g, landed = {}, {}
    du2, du2_bf, g["ln2_g"], g["ln2_b"], *sq = _ln_bwd(d_x2, sv["u2"], W["ln2_g"], target=target)
    if sq:
        landed["sq"] = sq[0]
    d_hpre = _matmul(du2_bf, W["w_down"], "nt", name="d_act", tm=1024, tn=1024, tk=1024, epi="drelu2",
                     extra=(sv["act"],), out_dtype=BF16)
    g["w_down"] = _matmul(sv["act"], du2_bf, "tn", name="dw_down", tm=512, tn=1024, tk=4096, out_dtype=dt, chunks=rows)
    g["w_up"] = _matmul(sv["x1_bf"], d_hpre, "tn", name="dw_up", tm=1024, tn=512, tk=4096, out_dtype=dt, chunks=cols)
    d_x1 = _matmul(d_hpre, W["w_up"], "nt", name="d_x1", tm=512, tn=1024, tk=4096, epi="add", extra=(du2,))
    du1, du1_bf, g["ln1_g"], g["ln1_b"] = _ln_bwd(d_x1, sv["u1"], W["ln1_g"])
    d_merged = _matmul(du1_bf, W["w_out"], "nt", name="d_merged", tm=1024, tn=1024, tk=1024)
    g["w_out"] = _matmul(sv["merged"], du1_bf, "tn", name="dw_out", tm=1024, tn=512, tk=4096, out_dtype=dt, chunks=rows)
    d_br, d_gate = _gate_bwd(sv["proj"], sv["br"], d_merged)
    d_branch = []
    for k, (nm, act) in enumerate((("p_ret", sv["retg"]), ("p_sb", sv["sb"]), ("p_sgu", sv["sg"]))):
        d_branch.append(_matmul(d_br[k], W[nm], "nt", name="d_" + nm[2:], tm=1024, tn=512, tk=1024))
        g[nm] = _matmul(act, d_br[k], "tn", name="dw_" + nm[2:], tm=512, tn=1024, tk=2048, out_dtype=dt, chunks=cols)
    d_ret, g["ret_gn_g"], g["ret_gn_b"] = _ret_bwd(sv["proj"], tables, W["ret_gn_g"], W["ret_gn_b"], sv["states"],
                                                   d_branch[0])
    if sb_comm_fn is None:
        d_sq, d_sk, d_sv = _sb_bwd(sv["proj"], sv["sb_a"], d_branch[1])
    else:
        (d_sq, d_sk, d_sv), landed["sb"] = _sb_bwd(sv["proj"], sv["sb_a"], d_branch[1], comm=sb_comm_fn(g))
    d_sgu, g["sgu_ln_g"], g["sgu_ln_b"], g["sgu_w"], g["sgu_b"] = _sgu_bwd(
        sv["proj"], W["sgu_ln_g"], W["sgu_ln_b"], W["sgu_w"], W["sgu_b"], d_branch[2])
    d_proj = [d_ret, d_sq, d_sk, d_sv, d_sgu, d_gate[0], d_gate[1], d_gate[2]]
    g["w_in"] = _matmul(d_proj, sv["x_bf"], "tn", name="dw_in", tm=256, tn=1024, tk=4096, out_dtype=dt, chunks=rows,
                        comm=None if dwin_comm_fn is None else dwin_comm_fn(g))
    if dwin_comm_fn is not None:
        g["w_in"], landed["dwin"] = g["w_in"]
    if chunk_dtype is None:
        g["w_in"] = g["w_in"].T
    d_x = _matmul_rows_of(d_proj, W["w_in_t"], du1, name="d_x", tm=512,
                          comm=None if dx_comm_fn is None else dx_comm_fn(g))
    if dx_comm_fn is not None:
        d_x, landed["dx"] = d_x
    return d_x, g, landed


BIG = ("w_in", "p_ret", "p_sb", "p_sgu", "w_out", "w_up", "w_down")
SMALL = ("ret_gn_g", "ret_gn_b", "sgu_ln_g", "sgu_ln_b", "sgu_w", "sgu_b", "ln1_g", "ln1_b", "ln2_g", "ln2_b")
GATHER_KIND = {"w_in": "rows", "p_ret": "cols", "p_sb": "cols", "p_sgu": "cols", "w_out": "rows", "w_up": "cols",
               "w_down": "rows"}


def _small_weights(small, l):
    W = {}
    for n in SMALL:
        if n == "sgu_w":
            W[n] = small[n][l]
        elif n == "sgu_b":
            W[n] = small[n][l].reshape(4, CHUNK, 1)
        else:
            W[n] = small[n][l].reshape(1, -1)
    return W


def _local_step(x, target, full, small):
    tables = _ret_tables(x.shape[0])
    Ws = [{**{n: full[n][l] for n in BIG[1:]}, "w_in_t": full["w_in"][l].T, **_small_weights(small, l)}
          for l in range(DEPTH)]
    saved = []
    h, h_bf = x, _bf(x)
    for l in range(DEPTH):
        h, h_bf, sv = _layer_fwd(h, h_bf, Ws[l], tables)
        saved.append(sv)
    grads = [None] * DEPTH
    d_h, grads[-1], landed = _layer_bwd(h, Ws[-1], tables, saved[-1], target=target)
    for l in reversed(range(DEPTH - 1)):
        d_h, grads[l], _ = _layer_bwd(d_h, Ws[l], tables, saved[l])
    return landed["sq"], d_h, grads


def _adam(w, parts, m, v, name):
    L, R, C = w.shape
    tr = next(t for t in (320, 256, 128) if R % t == 0)
    assert len(parts) == L

    def body(*refs):
        w_ref, p_refs, (m_ref, v_ref, g_ref, d_ref, nm_ref, nv_ref) = refs[0], refs[1:1 + L], refs[1 + L:]
        layer = pl.program_id(0)
        g = None
        for li, p_ref in enumerate(p_refs):
            s = p_ref[0].astype(F32)
            for j in range(1, p_ref.shape[0]):
                s = s + p_ref[j].astype(F32)
            g = s if g is None else jnp.where(layer == li, s, g)
        g_ref[...] = g
        d_ref[...], nm_ref[...], nv_ref[...] = _adam_update(w_ref[...], g, m_ref[...], v_ref[...])

    tile = pl.BlockSpec((None, tr, C), lambda l, i: (l, i, 0))
    part = lambda li: pl.BlockSpec((parts[li].shape[0], tr, C), lambda l, i, li=li: (0, jnp.where(l == li, i, 0), 0))
    out = jax.ShapeDtypeStruct((L, R, C), F32)
    return pl.pallas_call(
        body, name=name, grid=(L, R // tr), out_shape=(out, out, out, out),
        in_specs=[tile] + [part(li) for li in range(L)] + [tile, tile],
        out_specs=(tile, tile, tile, tile),
        compiler_params=_cparams(("parallel", "parallel")),
    )(w, *parts, m, v)


def _adam_update(w, g, m, v):
    m2 = ADAM_B1 * m + (1.0 - ADAM_B1) * g
    v2 = ADAM_B2 * v + (1.0 - ADAM_B2) * (g * g)
    m_hat = m2 / (1.0 - ADAM_B1 ** ADAM_STEP)
    v_hat = v2 / (1.0 - ADAM_B2 ** ADAM_STEP)
    return -ADAM_LR * (m_hat / (jnp.sqrt(v_hat) + ADAM_EPS) + ADAM_WD * w), m2, v2


def _adam_small(w, m, v, parts):
    k = len(SMALL)

    def body(*refs):
        w_refs, m_refs, v_refs, p_refs, outs = refs[:k], refs[k:2 * k], refs[2 * k:3 * k], refs[3 * k:5 * k], refs[5 * k:]
        for i in range(k):
            vector = len(w_refs[i].shape) == 2
            for l in range(DEPTH):
                p_ref = p_refs[DEPTH * i + l]
                g = p_ref[0]
                for j in range(1, N_DEV):
                    g = g + p_ref[j]
                at = (slice(l, l + 1), slice(None)) if vector else (l,)
                delta, m2, v2 = _adam_update(w_refs[i][at], g, m_refs[i][at], v_refs[i][at])
                for o_ref, val in zip(outs[4 * i:4 * i + 4], (g, delta, m2, v2)):
                    o_ref[at] = val

    vmem = pl.BlockSpec(memory_space=pltpu.VMEM)
    args = [w[n] for n in SMALL] + [m[n] for n in SMALL] + [v[n] for n in SMALL] + \
           [parts[(n, l)] for n in SMALL for l in range(DEPTH)]
    out_shape = [jax.ShapeDtypeStruct(w[n].shape, F32) for n in SMALL for _ in range(4)]
    outs = pl.pallas_call(body, name="adam_small", out_shape=out_shape, in_specs=[vmem] * len(args),
                          out_specs=[vmem] * len(out_shape), compiler_params=_cparams())(*args)
    return {n: tuple(outs[4 * i:4 * i + 4]) for i, n in enumerate(SMALL)}


WEIGHTS = ("w_in", "ret_gn_g", "ret_gn_b", "sgu_ln_g", "sgu_ln_b", "sgu_w", "sgu_b", "p_ret", "p_sb", "p_sgu", "w_out",
           "ln1_g", "ln1_b", "w_up", "w_down", "ln2_g", "ln2_b")


def kernel(x, w_in, ret_gn_g, ret_gn_b, sgu_ln_g, sgu_ln_b, sgu_w, sgu_b, p_ret, p_sb, p_sgu, w_out, ln1_g, ln1_b, w_up, w_down, ln2_g, ln2_b, loss_target, m_w_in, m_ret_gn_g, m_ret_gn_b, m_sgu_ln_g, m_sgu_ln_b, m_sgu_w, m_sgu_b, m_p_ret, m_p_sb, m_p_sgu, m_w_out, m_ln1_g, m_ln1_b, m_w_up, m_w_down, m_ln2_g, m_ln2_b, v_w_in, v_ret_gn_g, v_ret_gn_b, v_sgu_ln_g, v_sgu_ln_b, v_sgu_w, v_sgu_b, v_p_ret, v_p_sb, v_p_sgu, v_w_out, v_ln1_g, v_ln1_b, v_w_up, v_w_down, v_ln2_g, v_ln2_b):
    w = dict(zip(WEIGHTS, (w_in, ret_gn_g, ret_gn_b, sgu_ln_g, sgu_ln_b, sgu_w, sgu_b, p_ret, p_sb, p_sgu, w_out,
                           ln1_g, ln1_b, w_up, w_down, ln2_g, ln2_b)))
    m = dict(zip(WEIGHTS, (m_w_in, m_ret_gn_g, m_ret_gn_b, m_sgu_ln_g, m_sgu_ln_b, m_sgu_w, m_sgu_b, m_p_ret, m_p_sb,
                           m_p_sgu, m_w_out, m_ln1_g, m_ln1_b, m_w_up, m_w_down, m_ln2_g, m_ln2_b)))
    v = dict(zip(WEIGHTS, (v_w_in, v_ret_gn_g, v_ret_gn_b, v_sgu_ln_g, v_sgu_ln_b, v_sgu_w, v_sgu_b, v_p_ret, v_p_sb,
                           v_p_sgu, v_w_out, v_ln1_g, v_ln1_b, v_w_up, v_w_down, v_ln2_g, v_ln2_b)))

    small = {n: w[n] for n in SMALL}
    shard = {n: _bf(w[n]) for n in BIG}
    shard["w_in"] = shard["w_in"].transpose(0, 2, 1)
    S = x.shape[1]
    x0, target = x.reshape(S, D_MODEL), loss_target.reshape(S, D_MODEL)
    tables = _ret_tables(S)
    Ws = [_small_weights(small, l) for l in range(DEPTH)]

    (Ws[0]["w_in_t"],) = _exchange([_gather_transfer(shard["w_in"], 0, "rows")], "gather_w_in0", relay=True)
    later = [(n, 0) for n in BIG[1:]] + [(n, 1) for n in BIG]

    def weights_landed(landed):
        for (n, l), z in zip(later, landed):
            Ws[l]["w_in_t" if n == "w_in" else n] = z

    gather = _Comm([_gather_transfer(shard[n], l, GATHER_KIND[n]) for n, l in later], relay=True)
    h, h_bf, saved0 = _layer_fwd(x0, _bf(x0), Ws[0], tables, sb_comm=(gather, weights_landed))
    h, _, saved1 = _layer_fwd(h, h_bf, Ws[1], tables)
    d_h, g1, landed1 = _layer_bwd(h, Ws[1], tables, saved1, chunk_dtype=BF16, target=target,
                                  sb_comm_fn=lambda g: _Comm([_scatter_transfer(g[n]) for n in BIG[1:]]))
    loss = lax.psum(0.5 * jnp.sum(landed1["sq"]) / D_MODEL, ("x", "y", "c"))
    early = [("w_in", 1)] + [(n, 0) for n in BIG[1:]]

    def small_slabs(g):
        return [_slab_transfer(g[n].reshape(4, CHUNK) if n == "sgu_b" else g[n]) for n in SMALL]

    def early_scatter(g0):
        return _Comm([_scatter_transfer((g1 if l else g0)[n]) for n, l in early] + small_slabs(g1))

    def late_scatter(g0):
        pairs = _pair_reduce(g0["w_in"], "w_in0_pairs")
        return _Comm([_chip_scatter_transfer(pairs)])

    d_x, g0, landed = _layer_bwd(d_h, Ws[0], tables, saved0, chunk_dtype=BF16, sb_comm_fn=early_scatter,
                                 dwin_comm_fn=lambda g: _Comm(small_slabs(g)), dx_comm_fn=late_scatter)
    parts = {**dict(zip(early, landed["sb"])), **{(n, 1): z for n, z in zip(BIG[1:], landed1["sb"])}}
    parts[("w_in", 0)] = landed["dx"][0]
    small_parts = {**{(n, 1): z for n, z in zip(SMALL, landed["sb"][len(early):])},
                   **{(n, 0): z for n, z in zip(SMALL, landed["dwin"])}}

    grad, delta, new_m, new_v = {}, {}, {}, {}
    for n in BIG:
        view = (lambda a: a.transpose(0, 2, 1)) if n == "w_in" else (lambda a: a)
        res = _adam(view(w[n]), [parts[(n, l)] for l in range(DEPTH)], view(m[n]), view(v[n]), "adam_" + n)
        grad[n], delta[n], new_m[n], new_v[n] = (view(r) for r in res)
    for n, res in _adam_small(small, m, v, small_parts).items():
        grad[n], delta[n], new_m[n], new_v[n] = res

    return (loss, d_x.reshape(x.shape), *[grad[n] for n in WEIGHTS], *[delta[n] for n in WEIGHTS],
            *[new_m[n] for n in WEIGHTS], *[new_v[n] for n in WEIGHTS])
```

```python
import functools
import math

import numpy as np
import jax
import jax.numpy as jnp
from jax import lax
from jax.experimental import pallas as pl
from jax.experimental.pallas import tpu as pltpu

F32 = jnp.float32
BF16 = jnp.bfloat16

N_DEV = 8
DEPTH = 2
D_MODEL = 1024
CHUNK = 128
RET_W = 512
SB_W = 512
SGU_W = 512
N_IN = 7680
LN_EPS = 1e-5
ALPHA = (2 * DEPTH) ** 0.25
ROPE_BASE = 10000.0
ADAM_LR, ADAM_B1, ADAM_B2, ADAM_EPS, ADAM_WD, ADAM_STEP = 0.001, 0.9, 0.999, 1e-08, 0.01, 10
VMEM_LIMIT = 56 * 1024 * 1024

_GELU_K = math.sqrt(2.0 / math.pi)
_GELU_C = 0.044715


def _cparams(sem=None):
    return pltpu.CompilerParams(dimension_semantics=sem, vmem_limit_bytes=VMEM_LIMIT)


def _dg(a, b, ca, cb):
    return lax.dot_general(a, b, (((ca,), (cb,)), ((), ())), preferred_element_type=F32)


def _bf(x):
    return x.astype(BF16)


def _sigmoid(x):
    return 1.0 / (1.0 + jnp.exp(-x))


def _gelu(x):
    t = jnp.tanh(_GELU_K * (x + _GELU_C * (x * x * x)))
    return x * (0.5 * (1.0 + t))


def _gelu_grad(x):
    t = jnp.tanh(_GELU_K * (x + _GELU_C * (x * x * x)))
    return 0.5 * (1.0 + t) + 0.5 * x * (1.0 - t * t) * (_GELU_K * (1.0 + 3.0 * _GELU_C * x * x))


def _norm_stats(u):
    mu = jnp.mean(u, axis=-1, keepdims=True)
    d = u - mu
    var = jnp.mean(d * d, axis=-1, keepdims=True)
    rstd = lax.rsqrt(var + LN_EPS)
    return d * rstd, rstd


def _norm_bwd(dxh, xh, rstd):
    return rstd * (dxh - jnp.mean(dxh, axis=-1, keepdims=True) - xh * jnp.mean(dxh * xh, axis=-1, keepdims=True))


class _Transfer:
    def __init__(self, src, dst_shape, src_at, dst_at, same_core=False):
        self.src, self.dst_shape, self.src_at, self.dst_at = src, tuple(dst_shape), src_at, dst_at
        self.same_core = same_core


def _gather_transfer(shard, l, kind):
    _, r, c = shard.shape
    src_at = lambda ref, p: ref.at[l]
    if kind == "slab":
        return _Transfer(shard, (N_DEV, r, c), src_at, lambda ref, s: ref.at[s])
    if kind == "rows":
        return _Transfer(shard, (N_DEV * r, c), src_at, lambda ref, s: ref.at[pl.ds(pl.multiple_of(s * r, r), r), :])
    return _Transfer(shard, (r, N_DEV * c), src_at, lambda ref, s: ref.at[:, pl.ds(pl.multiple_of(s * c, c), c)])


def _scatter_transfer(chunks):
    return _Transfer(chunks, chunks.shape, lambda ref, p: ref.at[p], lambda ref, s: ref.at[s])


def _slab_transfer(arr):
    return _Transfer(arr, (N_DEV,) + arr.shape, lambda ref, p: ref, lambda ref, s: ref.at[s])


class _Comm:
    def __init__(self, transfers, relay=False):
        self.transfers = list(transfers)
        self.relay = relay
        self.n = len(self.transfers)
        self.arrays = [t.src for t in self.transfers]
        self.out_shape = [jax.ShapeDtypeStruct(t.dst_shape, t.src.dtype) for t in self.transfers]
        self.scratch = [pltpu.SemaphoreType.DMA((self.n * (N_DEV - 1),)), pltpu.SemaphoreType.DMA((self.n * (N_DEV - 1),)),
                        pltpu.SemaphoreType.DMA((self.n,))]

    def _relay_copies(self, srcs, dsts, send_sems, recv_sems, local_sems):
        x, y, c = lax.axis_index("x"), lax.axis_index("y"), lax.axis_index("c")
        me = 4 * x + 2 * y + c
        chips = [(1 - x, y), (x, 1 - y), (1 - x, 1 - y)]
        first, passed, own = [], [], []
        for t, tr in enumerate(self.transfers):
            def copy(k, src, sender, to, t=t, tr=tr):
                return pltpu.make_async_remote_copy(
                    src_ref=src, dst_ref=tr.dst_at(dsts[t], sender), send_sem=send_sems.at[t * (N_DEV - 1) + k],
                    recv_sem=recv_sems.at[t * (N_DEV - 1) + k], device_id=to, device_id_type=pl.DeviceIdType.MESH)

            mine = tr.src_at(srcs[t], me)
            first.append([copy(0, mine, me, (x, y, 1 - c))] + [copy(1 + j, mine, me, (px, py, c))
                                                                for j, (px, py) in enumerate(chips)])
            passed.append([copy(4 + j, tr.dst_at(dsts[t], 4 * px + 2 * py + c), 4 * px + 2 * py + c, (x, y, 1 - c))
                           for j, (px, py) in enumerate(chips)])
            own.append(pltpu.make_async_copy(mine, tr.dst_at(dsts[t], me), local_sems.at[t]))
        return first, passed, own

    def _copies(self, srcs, dsts, send_sems, recv_sems, local_sems):
        x, y, c = lax.axis_index("x"), lax.axis_index("y"), lax.axis_index("c")
        me = 4 * x + 2 * y + c
        copies = []
        for d in range(1, N_DEV):
            px = 1 - x if d & 4 else x
            py = 1 - y if d & 2 else y
            pc = 1 - c if d & 1 else c
            for t, tr in enumerate(self.transfers):
                if tr.same_core and d & 1:
                    continue
                peer, mine = (2 * px + py, 2 * x + y) if tr.same_core else (4 * px + 2 * py + pc, me)
                k = t * (N_DEV - 1) + d - 1
                copies.append(pltpu.make_async_remote_copy(
                    src_ref=tr.src_at(srcs[t], peer), dst_ref=tr.dst_at(dsts[t], mine),
                    send_sem=send_sems.at[k], recv_sem=recv_sems.at[k],
                    device_id=(px, py, pc), device_id_type=pl.DeviceIdType.MESH))
        own = []
        for t, tr in enumerate(self.transfers):
            mine = 2 * x + y if tr.same_core else me
            own.append(pltpu.make_async_copy(tr.src_at(srcs[t], mine), tr.dst_at(dsts[t], mine), local_sems.at[t]))
        return copies, own

    def start(self, srcs, dsts, *sems):
        if self.relay:
            first, _, own = self._relay_copies(srcs, dsts, *sems)
            for cp in own + [cp for per_t in first for cp in per_t]:
                cp.start()
            return
        copies, own = self._copies(srcs, dsts, *sems)
        for cp in own + copies:
            cp.start()

    def pass_on(self, srcs, dsts, *sems):
        if self.relay:
            first, passed, _ = self._relay_copies(srcs, dsts, *sems)
            for j in range(3):
                for t in range(self.n):
                    first[t][1 + j].wait_recv()
                    passed[t][j].start()

    def finish(self, srcs, dsts, *sems):
        if self.relay:
            first, passed, own = self._relay_copies(srcs, dsts, *sems)
            for t in range(self.n):
                first[t][0].wait_recv()
                for cp in passed[t]:
                    cp.wait_recv()
            for t in range(self.n):
                for cp in first[t] + passed[t]:
                    cp.wait_send()
                own[t].wait()
            return
        copies, own = self._copies(srcs, dsts, *sems)
        for cp in copies + own:
            cp.wait()


def _pcall(body, *, name, grid, in_specs, out_specs, out_shape, scratch_shapes, sem, args, comm=None):
    in_specs, out_specs, out_shape = list(in_specs), list(out_specs), list(out_shape)
    if comm is None:
        outs = pl.pallas_call(body, name=name, grid=grid, in_specs=in_specs, out_specs=out_specs, out_shape=out_shape,
                              scratch_shapes=list(scratch_shapes), compiler_params=_cparams(sem))(*args)
        return list(outs), []
    n_in, n_out, n_scr, k = len(in_specs), len(out_specs), len(scratch_shapes), comm.n

    def carrier(*refs):
        ins, cin = refs[:n_in], refs[n_in:n_in + k]
        outs, cout = refs[n_in + k:n_in + k + n_out], refs[n_in + k + n_out:n_in + 2 * k + n_out]
        scr, sems = refs[n_in + 2 * k + n_out:n_in + 2 * k + n_out + n_scr], refs[n_in + 2 * k + n_out + n_scr:]
        ids = [pl.program_id(d) for d in range(len(grid))]
        first = functools.reduce(jnp.logical_and, [i == 0 for i in ids])
        last = functools.reduce(jnp.logical_and, [i == g - 1 for i, g in zip(ids, grid)])

        @pl.when(first)
        def _():
            comm.start(cin, cout, *sems)

        body(*ins, *outs, *scr)

        early_pass = comm.relay and len(grid) > 1 and grid[0] > 1
        if early_pass:
            @pl.when(functools.reduce(jnp.logical_and, [ids[0] == grid[0] - 1] + [i == 0 for i in ids[1:]]))
            def _():
                comm.pass_on(cin, cout, *sems)

        @pl.when(last)
        def _():
            if not early_pass:
                comm.pass_on(cin, cout, *sems)
            comm.finish(cin, cout, *sems)

    hbm = pl.BlockSpec(memory_space=pl.ANY)
    outs = pl.pallas_call(
        carrier, name=name, grid=grid, in_specs=in_specs + [hbm] * k, out_specs=out_specs + [hbm] * k,
        out_shape=out_shape + comm.out_shape, scratch_shapes=list(scratch_shapes) + comm.scratch,
        compiler_params=_cparams(tuple("arbitrary" for _ in grid)),
    )(*args, *comm.arrays)
    return list(outs[:n_out]), list(outs[n_out:])


def _exchange(transfers, name, relay=False):
    comm = _Comm(transfers, relay)

    def body(*refs):
        k = comm.n
        comm.start(refs[:k], refs[k:2 * k], *refs[2 * k:])
        comm.pass_on(refs[:k], refs[k:2 * k], *refs[2 * k:])
        comm.finish(refs[:k], refs[k:2 * k], *refs[2 * k:])

    hbm = pl.BlockSpec(memory_space=pl.ANY)
    return pl.pallas_call(body, name=name, out_shape=comm.out_shape, in_specs=[hbm] * comm.n, out_specs=[hbm] * comm.n,
                          scratch_shapes=comm.scratch)(*comm.arrays)


def _pair_reduce(chunks, name, tr=320):
    _, r, c = chunks.shape
    tr = min(tr, r)
    assert r % tr == 0

    def swap(src_ref, dst_ref, send_sems, recv_sems):
        x, y, core = lax.axis_index("x"), lax.axis_index("y"), lax.axis_index("c")
        copies = [pltpu.make_async_remote_copy(
            src_ref=src_ref.at[2 * k + 1 - core], dst_ref=dst_ref.at[k], send_sem=send_sems.at[k],
            recv_sem=recv_sems.at[k], device_id=(x, y, 1 - core), device_id_type=pl.DeviceIdType.MESH) for k in range(4)]
        for cp in copies:
            cp.start()
        for cp in copies:
            cp.wait()

    hbm = pl.BlockSpec(memory_space=pl.ANY)
    theirs = pl.pallas_call(swap, name=name + "_swap", out_shape=jax.ShapeDtypeStruct((4, r, c), chunks.dtype),
                            in_specs=[hbm], out_specs=hbm,
                            scratch_shapes=[pltpu.SemaphoreType.DMA((4,)), pltpu.SemaphoreType.DMA((4,))])(chunks)

    def add(mine_ref, theirs_ref, out_ref):
        core = lax.axis_index("c")
        both = mine_ref[...].astype(F32)
        out_ref[...] = (jnp.where(core == 0, both[0], both[1]) + theirs_ref[...].astype(F32)).astype(out_ref.dtype)

    return pl.pallas_call(
        add, name=name + "_add", grid=(4, r // tr), out_shape=jax.ShapeDtypeStruct((4, r, c), chunks.dtype),
        in_specs=[pl.BlockSpec((None, 2, tr, c), lambda k, i: (k, 0, i, 0)), pl.BlockSpec((None, tr, c), lambda k, i: (k, i, 0))],
        out_specs=pl.BlockSpec((None, tr, c), lambda k, i: (k, i, 0)),
        compiler_params=_cparams(("parallel", "parallel")),
    )(chunks.reshape(4, 2, r, c), theirs)


def _chip_scatter_transfer(pairs):
    return _Transfer(pairs, pairs.shape, lambda ref, p: ref.at[p], lambda ref, s: ref.at[s], same_core=True)


def _matmul(a, b, mode, *, name, tm, tn, tk, epi=None, extra=(), out_dtype=F32, chunks=None, comm=None):
    pieces = list(a) if isinstance(a, (list, tuple)) else [a]
    rows_a, cols_a = pieces[0].shape[0], sum(p.shape[1] for p in pieces)
    if mode == "nn":
        (M, K), N = (rows_a, cols_a), b.shape[1]
    elif mode == "nt":
        (M, K), N = (rows_a, cols_a), b.shape[0]
    else:
        (K, M), N = (rows_a, cols_a), b.shape[1]
    tm, tn, tk = min(tm, M), min(tn, N), min(tk, K)
    assert M % tm == 0 and N % tn == 0 and K % tk == 0 and (epi != "ln" or tn == N), (name, M, N, K)
    nk = K // tk
    tile_cols, axis = (tm, 0) if mode == "tn" else (tk, 2)
    assert all(p.shape[1] % tile_cols == 0 for p in pieces)
    counts = [p.shape[1] // tile_cols for p in pieces]
    starts = [sum(counts[:q]) for q in range(len(pieces))]

    def a_spec_of(q):
        at = lambda t: jnp.clip(t - starts[q], 0, counts[q] - 1) if len(pieces) > 1 else t
        return {"nn": pl.BlockSpec((tm, tk), lambda i, j, k: (i, at(k))),
                "nt": pl.BlockSpec((tm, tk), lambda i, j, k: (i, at(k))),
                "tn": pl.BlockSpec((tk, tm), lambda i, j, k: (k, at(i)))}[mode]

    n_a = len(pieces)
    b_mode = pl.Buffered(1) if (nk == 1 and tn == N and n_a > 1) else None
    b_spec = {"nn": pl.BlockSpec((tk, tn), lambda i, j, k: (k, j), pipeline_mode=b_mode),
              "nt": pl.BlockSpec((tn, tk), lambda i, j, k: (j, k), pipeline_mode=b_mode),
              "tn": pl.BlockSpec((tk, tn), lambda i, j, k: (k, j), pipeline_mode=b_mode)}[mode]
    ca, cb = {"nn": (1, 0), "nt": (1, 1), "tn": (0, 0)}[mode]
    tile = pl.BlockSpec((tm, tn), lambda i, j, k: (i, j))
    row = pl.BlockSpec((1, tn), lambda i, j, k: (0, j))
    n_extra = {None: 0, "add": 1, "relu2": 0, "drelu2": 1, "ln": 3}[epi]
    assert len(extra) == n_extra
    extra_specs = {None: [], "add": [tile], "relu2": [], "drelu2": [tile], "ln": [tile, row, row]}[epi]
    split = 0
    if epi == "relu2":
        out_shape, out_specs = (jax.ShapeDtypeStruct((M, N), BF16),), (tile,)
    elif epi == "ln":
        out_shape = (jax.ShapeDtypeStruct((M, N), F32), jax.ShapeDtypeStruct((M, N), F32),
                     jax.ShapeDtypeStruct((M, N), BF16))
        out_specs = (tile, tile, tile)
    elif chunks == "cols":
        c = N // N_DEV
        out_shape = (jax.ShapeDtypeStruct((N_DEV, M, c), out_dtype),)
        if tn == N:
            split = c
            out_specs = (pl.BlockSpec((N_DEV, tm, c), lambda i, j, k: (0, i, 0)),)
        else:
            assert c % tn == 0
            out_specs = (pl.BlockSpec((None, tm, tn), lambda i, j, k: (j // (c // tn), i, j % (c // tn))),)
    else:
        out_shape, out_specs = (jax.ShapeDtypeStruct((M, N), out_dtype),), (tile,)
    n_out = len(out_shape)

    def body(*refs):
        a_refs, b_ref = refs[:n_a], refs[n_a]
        ex = refs[n_a + 1:n_a + 1 + n_extra]
        outs = refs[n_a + 1 + n_extra:n_a + 1 + n_extra + n_out]
        acc_ref = refs[-1]
        k = pl.program_id(2)

        def finish(acc):
            if epi == "add":
                outs[0][...] = (acc + ALPHA * ex[0][...]).astype(out_dtype)
            elif epi == "relu2":
                r = jnp.maximum(acc, 0.0)
                outs[0][...] = _bf(r * r)
            elif epi == "drelu2":
                outs[0][...] = (acc * (2.0 * jnp.sqrt(ex[0][...].astype(F32)))).astype(out_dtype)
            elif epi == "ln":
                u = ALPHA * ex[0][...] + acc
                xh, _ = _norm_stats(u)
                y = xh * ex[1][...] + ex[2][...]
                outs[0][...] = u
                outs[1][...] = y
                outs[2][...] = _bf(y)
            elif split:
                for p in range(N_DEV):
                    outs[0][p] = acc[:, p * split:(p + 1) * split].astype(out_dtype)
            else:
                outs[0][...] = acc.astype(out_dtype)

        def step(a_ref, first, middle, last):
            part = _dg(_bf(a_ref[...]), _bf(b_ref[...]), ca, cb)
            if nk == 1:
                finish(part)
                return
            if first:
                @pl.when(k == 0)
                def _():
                    acc_ref[...] = part

            if middle:
                @pl.when(jnp.logical_and(k > 0, k < nk - 1))
                def _():
                    acc_ref[...] += part

            if last:
                @pl.when(k == nk - 1)
                def _():
                    finish(acc_ref[...] + part)

        if n_a == 1:
            step(a_refs[0], True, True, True)
        else:
            t = pl.program_id(axis)
            for q in range(n_a):
                along_k = axis == 2
                first = not along_k or starts[q] == 0
                last = not along_k or starts[q] + counts[q] == nk
                middle = not along_k or counts[q] > int(first) + int(last)

                @pl.when(jnp.logical_and(t >= starts[q], t < starts[q] + counts[q]))
                def _(q=q, first=first, middle=middle, last=last):
                    step(a_refs[q], first, middle, last)

    outs, landed = _pcall(
        body, name=name, out_shape=out_shape, grid=(M // tm, N // tn, nk),
        in_specs=[a_spec_of(q) for q in range(n_a)] + [b_spec] + extra_specs, out_specs=out_specs,
        scratch_shapes=[pltpu.VMEM((tm, tn) if nk > 1 else (8, 128), F32)], sem=("parallel", "parallel", "arbitrary"),
        args=(*pieces, b, *extra), comm=comm)
    res = outs[0] if n_out == 1 else tuple(outs)
    if chunks == "rows":
        res = res.reshape(N_DEV, M // N_DEV, N)
    return res if comm is None else (res, landed)


def _matmul_rows_of(pieces, b, res, *, name, tm, comm=None):
    M, (K, N) = pieces[0].shape[0], b.shape
    tm = min(tm, M)
    subs, start = [], 0
    for q, p in enumerate(pieces):
        w = p.shape[1]
        step = w if start % w == 0 else 512
        assert w % step == 0 and start % step == 0
        subs += [(q, off, step, start + off) for off in range(0, w, step)]
        start += w
    assert start == K
    n_p, n_s = len(pieces), len(subs)

    def body(*refs):
        a_refs, b_refs, res_ref, out_ref = refs[:n_p], refs[n_p:n_p + n_s], refs[n_p + n_s], refs[n_p + n_s + 1]
        acc = None
        for (q, off, w, _), b_ref in zip(subs, b_refs):
            part = _dg(_bf(a_refs[q][:, off:off + w]), _bf(b_ref[...]), 1, 0)
            acc = part if acc is None else acc + part
        out_ref[...] = acc + ALPHA * res_ref[...]

    tile = pl.BlockSpec((tm, N), lambda i: (i, 0))
    outs, landed = _pcall(
        body, name=name, grid=(M // tm,), out_shape=[jax.ShapeDtypeStruct((M, N), F32)],
        in_specs=[pl.BlockSpec((tm, p.shape[1]), lambda i: (i, 0)) for p in pieces] +
                 [pl.BlockSpec((w, N), lambda i, r=row // w: (r, 0), pipeline_mode=pl.Buffered(1))
                  for _, _, w, row in subs] + [tile],
        out_specs=[tile], scratch_shapes=[], sem=("parallel",), args=(*pieces, *([b] * n_s), res), comm=comm)
    return outs[0] if comm is None else (outs[0], landed)


def _ret_tables(S):
    half = 64
    inv_freq = ROPE_BASE ** (-jnp.arange(half, dtype=F32) / half)
    ang = jnp.arange(S, dtype=jnp.int32).astype(F32)[:, None] * inv_freq[None, :]
    cos, sin = jnp.cos(ang), jnp.sin(ang)
    cosf = jnp.concatenate([cos, cos], axis=1)
    sinf = jnp.concatenate([-sin, sin], axis=1)
    log_g = jnp.log(1.0 - 2.0 ** (-5.0 - jnp.arange(4, dtype=F32)))
    idx = jnp.arange(CHUNK, dtype=F32)
    diff = idx[:, None] - idx[None, :]
    md = jnp.where(diff[None] >= 0, jnp.exp(log_g[:, None, None] * diff[None]), 0.0)
    kd = jnp.exp(log_g[:, None] * (CHUNK - 1 - idx)[None, :])
    qd = jnp.exp(log_g[:, None] * (idx + 1.0)[None, :])
    cd = jnp.exp(log_g * CHUNK)
    bc = lambda t: jnp.broadcast_to(t[:, :, None], (4, CHUNK, CHUNK))
    return cosf, sinf, md, bc(qd), bc(kd), jnp.broadcast_to(cd[:, None, None], (4, 8, CHUNK))


def _rot(x, cosf, sinf):
    return x * cosf + pltpu.roll(x, 64, 1) * sinf


def _rot_t(dx, cosf, sinf):
    return dx * cosf - pltpu.roll(dx, 64, 1) * sinf


RET_CHUNKS = 2


def _ret_specs(rev, S):
    R = min(RET_CHUNKS, S // CHUNK)
    rows, steps = R * CHUNK, S // (R * CHUNK)
    rn = (lambda n: steps - 1 - n) if rev else (lambda n: n)
    col = lambda c: pl.BlockSpec((rows, 512), lambda n, c=c: (rn(n), c))
    tab = pl.BlockSpec((rows, CHUNK), lambda n: (rn(n), 0))
    dec = pl.BlockSpec((4, CHUNK, CHUNK), lambda n: (0, 0, 0))
    cdec = pl.BlockSpec((4, 8, CHUNK), lambda n: (0, 0, 0))
    vec = pl.BlockSpec((1, 512), lambda n: (0, 0))
    st = pl.BlockSpec((R, 4, CHUNK, CHUNK), lambda n: (rn(n), 0, 0, 0))
    return R, steps, rn, col, tab, dec, cdec, vec, st


def _ret_fwd(proj, tables, gn_g, gn_b):
    S = proj.shape[0]
    R, steps, _, col, tab, dec, cdec, vec, st = _ret_specs(False, S)

    def body(q_ref, k_ref, v_ref, g_ref, cos_ref, sin_ref, md_ref, qd_ref, kd_ref, cd_ref, gng_ref, gnb_ref,
             out_ref, st_ref, state):
        @pl.when(pl.program_id(0) == 0)
        def _():
            state[...] = jnp.zeros_like(state)

        tiles = [(c, h) for c in range(R) for h in range(4)]
        rs = lambda c: slice(c * CHUNK, (c + 1) * CHUNK)
        sl = lambda h: slice(h * 128, (h + 1) * 128)
        qr = [_rot(q_ref[rs(c), sl(h)].astype(F32), cos_ref[rs(c), :], sin_ref[rs(c), :]) for c, h in tiles]
        kr = [_rot(k_ref[rs(c), sl(h)].astype(F32), cos_ref[rs(c), :], sin_ref[rs(c), :]) * (128 ** -0.5)
              for c, h in tiles]
        vb = [_bf(v_ref[rs(c), sl(h)]) for c, h in tiles]
        kv = [_dg(_bf(k * kd_ref[h]), v, 0, 0) for k, v, (c, h) in zip(kr, vb, tiles)]
        before = {}
        for h in range(4):
            s_h = state[h]
            for c in range(R):
                st_ref[c, h] = s_h
                before[(c, h)] = s_h
                s_h = s_h * cd_ref[h, 0:1, :] + kv[c * 4 + h]
            state[h] = s_h
        sc = [_dg(_bf(q), _bf(k), 1, 1) * md_ref[h] for q, k, (c, h) in zip(qr, kr, tiles)]
        r = [_dg(_bf(x), v, 1, 0) + _dg(_bf(q * qd_ref[h]), _bf(before[(c, h)]), 1, 0)
             for x, v, q, (c, h) in zip(sc, vb, qr, tiles)]
        for x, (c, h) in zip(r, tiles):
            y, _ = _norm_stats(x)
            rg = g_ref[rs(c), sl(h)].astype(F32)
            out_ref[rs(c), sl(h)] = rg * _sigmoid(rg) * (y * gng_ref[:, sl(h)] + gnb_ref[:, sl(h)])

    return pl.pallas_call(
        body, name="ret_fwd", grid=(steps,),
        out_shape=(jax.ShapeDtypeStruct((S, RET_W), F32), jax.ShapeDtypeStruct((S // CHUNK, 4, CHUNK, CHUNK), F32)),
        in_specs=[col(0), col(1), col(2), col(3), tab, tab, dec, dec, dec, cdec, vec, vec],
        out_specs=(pl.BlockSpec((R * CHUNK, 512), lambda n: (n, 0)), st),
        scratch_shapes=[pltpu.VMEM((4, CHUNK, CHUNK), F32)],
        compiler_params=_cparams(("arbitrary",)),
    )(proj, proj, proj, proj, *tables, gn_g, gn_b)


def _ret_bwd(proj, tables, gn_g, gn_b, states, d_out):
    S = proj.shape[0]
    R, steps, rn, col, tab, dec, cdec, vec, st = _ret_specs(True, S)

    def kernel_body(q_ref, k_ref, v_ref, g_ref, cos_ref, sin_ref, md_ref, qd_ref, kd_ref, cd_ref, gng_ref, gnb_ref,
                    st_ref, do_ref, dp_ref, dg_ref, db_ref, gstate):
        @pl.when(pl.program_id(0) == 0)
        def _():
            gstate[...] = jnp.zeros_like(gstate)
            dg_ref[...] = jnp.zeros_like(dg_ref)
            db_ref[...] = jnp.zeros_like(db_ref)

        tiles = [(c, h) for c in range(R) for h in range(4)]
        rs = lambda c: slice(c * CHUNK, (c + 1) * CHUNK)
        sl = lambda h: slice(h * 128, (h + 1) * 128)
        rot = lambda ref, c, h: _rot(ref[rs(c), sl(h)].astype(F32), cos_ref[rs(c), :], sin_ref[rs(c), :])
        qr = [rot(q_ref, c, h) for c, h in tiles]
        kr = [rot(k_ref, c, h) * (128 ** -0.5) for c, h in tiles]
        qb, kb = [_bf(x) for x in qr], [_bf(x) for x in kr]
        vb = [_bf(v_ref[rs(c), sl(h)]) for c, h in tiles]
        s0b = [_bf(st_ref[c, h]) for c, h in tiles]
        scb = [_bf(_dg(q, k, 1, 1) * md_ref[h]) for q, k, (c, h) in zip(qb, kb, tiles)]
        qdb = [_bf(q * qd_ref[h]) for q, (c, h) in zip(qr, tiles)]
        kdb = [_bf(k * kd_ref[h]) for k, (c, h) in zip(kr, tiles)]
        r = [_dg(x, v, 1, 0) + _dg(q, s, 1, 0) for x, v, q, s in zip(scb, vb, qdb, s0b)]
        drb, d_rg = [], []
        for x, (c, h) in zip(r, tiles):
            y, rstd = _norm_stats(x)
            gng = gng_ref[:, sl(h)]
            rg = g_ref[rs(c), sl(h)].astype(F32)
            sg = _sigmoid(rg)
            d_o = do_ref[rs(c), sl(h)]
            d_gn = d_o * (rg * sg)
            dg_ref[:, sl(h)] += jnp.sum(d_gn * y, axis=0, keepdims=True)
            db_ref[:, sl(h)] += jnp.sum(d_gn, axis=0, keepdims=True)
            drb.append(_bf(_norm_bwd(d_gn * gng, y, rstd)))
            d_rg.append(_bf(d_o * (y * gng + gnb_ref[:, sl(h)]) * (sg * (1.0 + rg * (1.0 - sg)))))
        grow = [_dg(q, d, 0, 0) for q, d in zip(qdb, drb)]
        after = {}
        for h in range(4):
            g_h = gstate[h]
            for c in reversed(range(R)):
                after[(c, h)] = _bf(g_h)
                g_h = g_h * cd_ref[h, 0:1, :] + grow[c * 4 + h]
            gstate[h] = g_h
        dscb = [_bf(_dg(d, v, 1, 1) * md_ref[h]) for d, v, (c, h) in zip(drb, vb, tiles)]
        for t, (c, h) in enumerate(tiles):
            gb = after[(c, h)]
            dqr = _dg(dscb[t], kb[t], 1, 0) + _dg(drb[t], s0b[t], 1, 1) * qd_ref[h]
            dkr = _dg(dscb[t], qb[t], 0, 0) + _dg(vb[t], gb, 1, 1) * kd_ref[h]
            dv = _dg(scb[t], drb[t], 0, 0) + _dg(kdb[t], gb, 1, 0)
            cosf, sinf = cos_ref[rs(c), :], sin_ref[rs(c), :]
            dp_ref[rs(c), 0 * 512 + h * 128:0 * 512 + (h + 1) * 128] = _bf(_rot_t(dqr, cosf, sinf))
            dp_ref[rs(c), 1 * 512 + h * 128:1 * 512 + (h + 1) * 128] = _bf(_rot_t(dkr, cosf, sinf) * (128 ** -0.5))
            dp_ref[rs(c), 2 * 512 + h * 128:2 * 512 + (h + 1) * 128] = _bf(dv)
            dp_ref[rs(c), 3 * 512 + h * 128:3 * 512 + (h + 1) * 128] = d_rg[t]

    acc = pl.BlockSpec((1, 512), lambda n: (0, 0))
    return pl.pallas_call(
        kernel_body, name="ret_bwd", grid=(steps,),
        out_shape=(jax.ShapeDtypeStruct((S, 2048), BF16), jax.ShapeDtypeStruct((1, 512), F32),
                   jax.ShapeDtypeStruct((1, 512), F32)),
        in_specs=[col(0), col(1), col(2), col(3), tab, tab, dec, dec, dec, cdec, vec, vec, st,
                  pl.BlockSpec((R * CHUNK, 512), lambda n: (rn(n), 0))],
        out_specs=(pl.BlockSpec((R * CHUNK, 2048), lambda n: (rn(n), 0)), acc, acc),
        scratch_shapes=[pltpu.VMEM((4, CHUNK, CHUNK), F32)],
        compiler_params=_cparams(("arbitrary",)),
    )(proj, proj, proj, proj, *tables, gn_g, gn_b, states, d_out)


SB_T = 256
SB_SCALE = 64 ** -0.5
SB_Q_COL, SB_K_COL, SB_V_COL = 2048 // 128, 2560 // 128, 3072 // 128


def _head_masks():
    lane = lax.broadcasted_iota(jnp.int32, (1, 128), 1)
    m0 = (lane < 64).astype(F32)
    return m0, 1.0 - m0


def _tri(n, cmp):
    r = lax.broadcasted_iota(jnp.int32, (n, n), 0)
    c = lax.broadcasted_iota(jnp.int32, (n, n), 1)
    return cmp(r, c)


def _tri_sum(x, tri):
    hi = _bf(x)
    lo = _bf(x - hi.astype(F32))
    return _dg(hi, tri, 1, 0) + _dg(lo, tri, 1, 0)


def _sb_weights(qms, kblks, upper, carry, causal):
    tiles = [(b, h) for b in range(len(kblks)) for h in range(2)]
    zs = [_dg(qms[h], kblks[b], 1, 1) for b, h in tiles]
    lgs = [-(jnp.maximum(z, 0.0) + jnp.log(1.0 + jnp.exp(-jnp.abs(z)))) for z in zs]
    if causal is not None:
        lgs = [jnp.where(causal, lg, 0.0) if b == 0 else lg for lg, (b, h) in zip(lgs, tiles)]
    carries = list(carry)
    for t in range(len(tiles) - 2):
        carries.append(carries[t] + jnp.sum(lgs[t], axis=1, keepdims=True))
    his = [_bf(lg) for lg in lgs]
    los = [_bf(lg - hi.astype(F32)) for lg, hi in zip(lgs, his)]
    later = [_dg(hi, upper, 1, 0) for hi in his]
    later = [r + _dg(lo, upper, 1, 0) for r, lo in zip(later, los)]
    a = [jnp.exp(lg + z + (r + c)) for lg, z, r, c in zip(lgs, zs, later, carries)]
    if causal is not None:
        a = [jnp.where(causal, x, 0.0) if b == 0 else x for x, (b, h) in zip(a, tiles)]
    out = tuple(carries[t] + jnp.sum(lgs[t], axis=1, keepdims=True) for t in (len(tiles) - 2, len(tiles) - 1))
    return [a[2 * b:2 * b + 2] for b in range(len(kblks))], out


def _sb_fwd(proj, comm=None):
    S = proj.shape[0]
    T = min(SB_T, S)
    nq = S // T

    def body(q_ref, k_ref, v_ref, o_ref, a_ref, kb_ref, vm_ref, acc_ref):
        i = pl.program_id(1)
        m0, m1 = _head_masks()

        @pl.when(i == 0)
        def _():
            v = v_ref[...]
            kb_ref[...] = _bf(k_ref[...])
            vm_ref[0] = _bf(v * m0)
            vm_ref[1] = _bf(v * m1)

        q = q_ref[...]
        qm = (_bf(q * (m0 * SB_SCALE)), _bf(q * (m1 * SB_SCALE)))
        upper = _tri(T, lambda r, c: r > c).astype(BF16)
        causal = _tri(T, lambda r, c: c < r)

        def tiles(js, carry, mask, first):
            ks = [pl.multiple_of(j * T, T) for j in js]
            a, out = _sb_weights(qm, [kb_ref[pl.ds(k, T), :] for k in ks], upper, carry, mask)
            a = [[_bf(t) for t in per_block] for per_block in a]
            for b, j in enumerate(js):
                for h in range(2):
                    a_ref[h, j] = a[b][h]
            parts = [_dg(a[b][h], vm_ref[h, pl.ds(k, T), :], 1, 0) for b, k in enumerate(ks) for h in range(2)]
            part = functools.reduce(lambda u, w: u + w, parts)
            if first:
                acc_ref[...] = part
            else:
                acc_ref[...] += part
            return out

        zero = jnp.zeros((T, 1), F32)
        carry = lax.cond(i == 0, lambda: tiles([i], (zero, zero), causal, True),
                         lambda: tiles([i, i - 1], (zero, zero), causal, True))
        n = jnp.maximum(i - 1, 0)
        carry = lax.fori_loop(0, n % 2, lambda _, c: tiles([n - 1], c, None, False), carry)
        top = n - 1 - n % 2
        carry = lax.fori_loop(0, (n // 2) % 2, lambda _, c: tiles([top, top - 1], c, None, False), carry)
        top = top - 2 * ((n // 2) % 2)
        lax.fori_loop(0, n // 4, lambda jj, c: tiles([top - 4 * jj - b for b in range(4)], c, None, False), carry)
        o_ref[...] = acc_ref[...]

    full = lambda c: pl.BlockSpec((S, 128), lambda p, i, c=c: (0, c + p))
    outs, landed = _pcall(
        body, name="sb_fwd", grid=(4, nq),
        out_shape=[jax.ShapeDtypeStruct((S, SB_W), F32), jax.ShapeDtypeStruct((4, 2, nq, nq, T, T), BF16)],
        in_specs=[pl.BlockSpec((T, 128), lambda p, i: (i, SB_Q_COL + p)), full(SB_K_COL), full(SB_V_COL)],
        out_specs=[pl.BlockSpec((T, 128), lambda p, i: (i, p)),
                   pl.BlockSpec((None, 2, None, nq, T, T), lambda p, i: (p, 0, i, 0, 0, 0))],
        scratch_shapes=[pltpu.VMEM((S, 128), BF16), pltpu.VMEM((2, S, 128), BF16), pltpu.VMEM((T, 128), F32)],
        sem=("arbitrary", "arbitrary"), args=(proj, proj, proj), comm=comm)
    return tuple(outs) if comm is None else (tuple(outs), landed)


def _sb_bwd(proj, a_saved, d_o, comm=None):
    S = proj.shape[0]
    T = min(SB_T, S)
    nq = S // T

    def body(q_ref, k_ref, v_ref, do_ref, a_ref, dq_ref, dk_ref, dv_ref, kb_ref, kbm_ref, vb_ref, dq_acc, dk_acc, dv_acc):
        i = pl.program_id(1)
        m0, m1 = _head_masks()

        @pl.when(i == 0)
        def _():
            k = k_ref[...]
            kb_ref[...] = _bf(k)
            kbm_ref[0] = _bf(k * m0)
            kbm_ref[1] = _bf(k * m1)
            vb_ref[...] = _bf(v_ref[...])
            dk_acc[...] = jnp.zeros_like(dk_acc)
            dv_acc[...] = jnp.zeros_like(dv_acc)

        q, d_out = q_ref[...], do_ref[...]
        qm = (_bf(q * (m0 * SB_SCALE)), _bf(q * (m1 * SB_SCALE)))
        dom = (_bf(d_out * m0), _bf(d_out * m1))
        lower = _tri(T, lambda r, c: r < c).astype(BF16)
        causal = _tri(T, lambda r, c: c < r)

        def up(js, carry, mask):
            ks = [pl.multiple_of(j * T, T) for j in js]
            tiles = [(b, h) for b in range(len(js)) for h in range(2)]
            zs = [_dg(qm[h], kb_ref[pl.ds(ks[b], T), :], 1, 1) for b, h in tiles]
            a = [a_ref[h, js[b]] for b, h in tiles]
            es = [w.astype(F32) * _dg(dom[h], vb_ref[pl.ds(ks[b], T), :], 1, 1) for w, (b, h) in zip(a, tiles)]
            carries = list(carry)
            for t in range(len(tiles)):
                carries.append(carries[t] + jnp.sum(es[t], axis=1, keepdims=True))
            d_lg = [_dg(_bf(e), lower, 1, 0) + c for e, c in zip(es, carries)]
            ens = [jnp.exp(-jnp.abs(z)) for z in zs]
            invs = [1.0 / (1.0 + en) for en in ens]
            betas = [jnp.where(z >= 0.0, inv, en * inv) for z, en, inv in zip(zs, ens, invs)]
            dzs = [e * (1.0 - b) - d * b for e, b, d in zip(es, betas, d_lg)]
            if mask is not None:
                dzs = [jnp.where(mask, dz, 0.0) if b == len(js) - 1 else dz for dz, (b, h) in zip(dzs, tiles)]
            dzs = [_bf(dz) for dz in dzs]
            parts = [_dg(dzs[t], kbm_ref[h, pl.ds(ks[b], T), :], 1, 0) for t, (b, h) in enumerate(tiles)]
            dq_acc[...] += functools.reduce(lambda u, w: u + w, parts)
            for b, k in enumerate(ks):
                dk_acc[pl.ds(k, T), :] += _dg(dzs[2 * b], qm[0], 0, 0) + _dg(dzs[2 * b + 1], qm[1], 0, 0)
                dv_acc[pl.ds(k, T), :] += _dg(a[2 * b], dom[0], 0, 0) + _dg(a[2 * b + 1], dom[1], 0, 0)
            return tuple(carries[-2:])

        zero = jnp.zeros((T, 1), F32)
        dq_acc[...] = jnp.zeros_like(dq_acc)
        n = jnp.maximum(i - 1, 0)
        carry = lax.fori_loop(0, n // 4, lambda jj, c: up([4 * jj + b for b in range(4)], c, None), (zero, zero))
        done = 4 * (n // 4)
        carry = lax.fori_loop(0, (n // 2) % 2, lambda _, c: up([done, done + 1], c, None), carry)
        carry = lax.fori_loop(0, n % 2, lambda _, c: up([n - 1], c, None), carry)

        @pl.when(i == 0)
        def _():
            up([i], carry, causal)

        @pl.when(i > 0)
        def _():
            up([i - 1, i], carry, causal)

        dq_ref[...] = _bf(dq_acc[...] * SB_SCALE)

        @pl.when(i == nq - 1)
        def _():
            dk_ref[...] = _bf(dk_acc[...])
            dv_ref[...] = _bf(dv_acc[...])

    full = lambda c: pl.BlockSpec((S, 128), lambda p, i, c=c: (0, c + p))
    tile = pl.BlockSpec((T, 128), lambda p, i: (i, p))
    acc = pl.BlockSpec((S, 128), lambda p, i: (0, p))
    out = jax.ShapeDtypeStruct((S, SB_W), BF16)
    outs, landed = _pcall(
        body, name="sb_bwd", grid=(4, nq), out_shape=[out, out, out],
        in_specs=[pl.BlockSpec((T, 128), lambda p, i: (i, SB_Q_COL + p)), full(SB_K_COL), full(SB_V_COL), tile,
                  pl.BlockSpec((None, 2, None, nq, T, T), lambda p, i: (p, 0, i, 0, 0, 0))],
        out_specs=[tile, acc, acc],
        scratch_shapes=[pltpu.VMEM((S, 128), BF16), pltpu.VMEM((2, S, 128), BF16), pltpu.VMEM((S, 128), BF16),
                        pltpu.VMEM((T, 128), F32), pltpu.VMEM((S, 128), F32), pltpu.VMEM((S, 128), F32)],
        sem=("arbitrary", "arbitrary"), args=(proj, proj, proj, d_o, a_saved), comm=comm)
    return tuple(outs) if comm is None else (tuple(outs), landed)


SGU_U_COL, SGU_V_COL = 3584 // 512, 4096 // 512


def _causal(w):
    r = lax.broadcasted_iota(jnp.int32, (CHUNK, CHUNK), 0)
    c = lax.broadcasted_iota(jnp.int32, (CHUNK, CHUNK), 1)
    return jnp.where(r >= c, w, 0.0)


SGU_CHUNKS = 4


def _sgu_fwd(proj, ln_g, ln_b, w, b):
    S = proj.shape[0]
    R = min(SGU_CHUNKS, S // CHUNK)
    rows = R * CHUNK

    def body(u_ref, v_ref, g_ref, b_ref, w_ref, bias_ref, out_ref):
        wc = [_bf(_causal(w_ref[g])) for g in range(4)]
        for r in range(R):
            rs = slice(r * CHUNK, (r + 1) * CHUNK)
            u = _gelu(u_ref[rs, :].astype(F32))
            xh, _ = _norm_stats(_gelu(v_ref[rs, :].astype(F32)))
            vn = _bf(xh * g_ref[...] + b_ref[...])
            for g in range(4):
                sl = slice(g * 128, (g + 1) * 128)
                out_ref[rs, sl] = u[:, sl] * (_dg(wc[g], vn[:, sl], 1, 0) + bias_ref[g])

    vec = pl.BlockSpec((1, 512), lambda n: (0, 0))
    return pl.pallas_call(
        body, name="sgu_fwd", grid=(S // rows,),
        out_shape=jax.ShapeDtypeStruct((S, SGU_W), F32),
        in_specs=[pl.BlockSpec((rows, 512), lambda n: (n, SGU_U_COL)),
                  pl.BlockSpec((rows, 512), lambda n: (n, SGU_V_COL)), vec, vec,
                  pl.BlockSpec((4, CHUNK, CHUNK), lambda n: (0, 0, 0)), pl.BlockSpec((4, CHUNK, 1), lambda n: (0, 0, 0))],
        out_specs=pl.BlockSpec((rows, 512), lambda n: (n, 0)),
        compiler_params=_cparams(("parallel",)),
    )(proj, proj, ln_g, ln_b, w, b)


def _sgu_bwd(proj, ln_g, ln_b, w, b, d_out):
    S = proj.shape[0]
    R = min(SGU_CHUNKS, S // CHUNK)
    rows = R * CHUNK

    def body(u_ref, v_ref, g_ref, b_ref, w_ref, bias_ref, do_ref, dp_ref, dg_ref, db_ref, dw_ref, dbias_ref):
        @pl.when(pl.program_id(0) == 0)
        def _():
            dg_ref[...] = jnp.zeros_like(dg_ref)
            db_ref[...] = jnp.zeros_like(db_ref)
            dw_ref[...] = jnp.zeros_like(dw_ref)
            dbias_ref[...] = jnp.zeros_like(dbias_ref)

        ln_gain = g_ref[...]
        wc = [_bf(_causal(w_ref[g])) for g in range(4)]
        for r in range(R):
            rs = slice(r * CHUNK, (r + 1) * CHUNK)
            gu, gv = u_ref[rs, :].astype(F32), v_ref[rs, :].astype(F32)
            u = _gelu(gu)
            xh, rstd = _norm_stats(_gelu(gv))
            vn = _bf(xh * ln_gain + b_ref[...])
            d_o = do_ref[rs, :]
            d_vn = []
            for g in range(4):
                sl = slice(g * 128, (g + 1) * 128)
                sv = _dg(wc[g], vn[:, sl], 1, 0) + bias_ref[g]
                dp_ref[rs, sl] = _bf(d_o[:, sl] * sv * _gelu_grad(gu[:, sl]))
                d_sv = d_o[:, sl] * u[:, sl]
                dbias_ref[g] += jnp.sum(d_sv, axis=1, keepdims=True)
                d_svb = _bf(d_sv)
                dw_ref[g] += _causal(_dg(d_svb, vn[:, sl], 1, 1))
                d_vn.append(_dg(wc[g], d_svb, 0, 0))
            d_vn = jnp.concatenate(d_vn, axis=1)
            dg_ref[...] += jnp.sum(d_vn * xh, axis=0, keepdims=True)
            db_ref[...] += jnp.sum(d_vn, axis=0, keepdims=True)
            dp_ref[rs, 512:1024] = _bf(_norm_bwd(d_vn * ln_gain, xh, rstd) * _gelu_grad(gv))

    vec = pl.BlockSpec((1, 512), lambda n: (0, 0))
    wspec = pl.BlockSpec((4, CHUNK, CHUNK), lambda n: (0, 0, 0))
    bspec = pl.BlockSpec((4, CHUNK, 1), lambda n: (0, 0, 0))
    return pl.pallas_call(
        body, name="sgu_bwd", grid=(S // rows,),
        out_shape=(jax.ShapeDtypeStruct((S, 1024), BF16), jax.ShapeDtypeStruct((1, 512), F32),
                   jax.ShapeDtypeStruct((1, 512), F32), jax.ShapeDtypeStruct((4, CHUNK, CHUNK), F32),
                   jax.ShapeDtypeStruct((4, CHUNK, 1), F32)),
        in_specs=[pl.BlockSpec((rows, 512), lambda n: (n, SGU_U_COL)),
                  pl.BlockSpec((rows, 512), lambda n: (n, SGU_V_COL)), vec, vec, wspec, bspec,
                  pl.BlockSpec((rows, 512), lambda n: (n, 0))],
        out_specs=(pl.BlockSpec((rows, 1024), lambda n: (n, 0)), vec, vec, wspec, bspec),
        compiler_params=_cparams(("arbitrary",)),
    )(proj, proj, ln_g, ln_b, w, b, d_out)


GATE_COL = 4608 // 512


def _merge_fwd(proj, branches, p_list, tm=512):
    S = proj.shape[0]
    tm = min(tm, S)

    def body(r_ref, s_ref, g_ref, pr_ref, ps_ref, pg_ref, gr_ref, gs_ref, gg_ref, m_ref, br_ref):
        acc = None
        for k, (x_ref, p_ref, gate_ref) in enumerate(((r_ref, pr_ref, gr_ref), (s_ref, ps_ref, gs_ref),
                                                      (g_ref, pg_ref, gg_ref))):
            br = _dg(_bf(x_ref[...]), _bf(p_ref[...]), 1, 0)
            br_ref[k] = _bf(br)
            term = _sigmoid(gate_ref[...].astype(F32)) * br
            acc = term if acc is None else acc + term
        m_ref[...] = _bf(acc)

    xs = pl.BlockSpec((tm, 512), lambda i, n: (i, 0))
    ps = pl.BlockSpec((512, 512), lambda i, n: (0, n))
    gate = lambda k: pl.BlockSpec((tm, 512), lambda i, n, k=k: (i, GATE_COL + 2 * k + n))
    return pl.pallas_call(
        body, name="merge_fwd", grid=(S // tm, 2),
        out_shape=(jax.ShapeDtypeStruct((S, D_MODEL), BF16), jax.ShapeDtypeStruct((3, S, D_MODEL), BF16)),
        in_specs=[xs, xs, xs, ps, ps, ps, gate(0), gate(1), gate(2)],
        out_specs=(pl.BlockSpec((tm, 512), lambda i, n: (i, n)), pl.BlockSpec((3, tm, 512), lambda i, n: (0, i, n))),
        compiler_params=_cparams(("parallel", "parallel")),
    )(*branches, *p_list, proj, proj, proj)


def _gate_bwd(proj, br, d_merged, tm=512):
    S = proj.shape[0]
    tm = min(tm, S)

    def body(dm_ref, br_ref, gr_ref, gs_ref, gg_ref, *out_refs):
        dm = dm_ref[...]
        for k, gate_ref in enumerate((gr_ref, gs_ref, gg_ref)):
            s = _sigmoid(gate_ref[...].astype(F32))
            out_refs[k][...] = _bf(dm * s)
            out_refs[3 + k][...] = _bf(dm * br_ref[k].astype(F32) * (s * (1.0 - s)))

    gate = lambda k: pl.BlockSpec((tm, 512), lambda i, n, k=k: (i, GATE_COL + 2 * k + n))
    three = pl.BlockSpec((3, tm, 512), lambda i, n: (0, i, n))
    tile = pl.BlockSpec((tm, 512), lambda i, n: (i, n))
    outs = pl.pallas_call(
        body, name="gate_bwd", grid=(S // tm, 2),
        out_shape=[jax.ShapeDtypeStruct((S, D_MODEL), BF16)] * 6,
        in_specs=[tile, three, gate(0), gate(1), gate(2)], out_specs=[tile] * 6,
        compiler_params=_cparams(("parallel", "parallel")),
    )(d_merged, br, proj, proj, proj)
    return outs[:3], outs[3:]


def _ln_bwd(dy, u, g, target=None, tm=256):
    S, D = u.shape
    tm = min(tm, S)
    loss = target is not None

    def body(*refs):
        dy_ref, u_ref, g_ref = refs[:3]
        du_ref, dub_ref, dg_ref, db_ref = refs[3 + loss:7 + loss]

        @pl.when(pl.program_id(0) == 0)
        def _():
            for acc_ref in refs[5 + loss:]:
                acc_ref[...] = jnp.zeros_like(acc_ref)

        dy_t = dy_ref[...]
        if loss:
            err = dy_t - refs[3][...]
            refs[-1][...] += jnp.sum(err * err, axis=0, keepdims=True)
            dy_t = err * (1.0 / D)
        xh, rstd = _norm_stats(u_ref[...])
        dg_ref[...] += jnp.sum(dy_t * xh, axis=0, keepdims=True)
        db_ref[...] += jnp.sum(dy_t, axis=0, keepdims=True)
        du = _norm_bwd(dy_t * g_ref[...], xh, rstd)
        du_ref[...] = du
        dub_ref[...] = _bf(du)

    tile = pl.BlockSpec((tm, D), lambda i: (i, 0))
    vec = pl.BlockSpec((1, D), lambda i: (0, 0))
    row = jax.ShapeDtypeStruct((1, D), F32)
    return pl.pallas_call(
        body, name="ln_bwd", grid=(S // tm,),
        out_shape=[jax.ShapeDtypeStruct((S, D), F32), jax.ShapeDtypeStruct((S, D), BF16)] + [row] * (2 + loss),
        in_specs=[tile, tile, vec] + [tile] * loss, out_specs=[tile, tile] + [vec] * (2 + loss),
        compiler_params=_cparams(("arbitrary",)),
    )(dy, u, g, *([target] if loss else []))


def _layer_fwd(x, x_bf, W, tables, sb_comm=None):
    proj = _matmul(x_bf, W["w_in_t"], "nt", name="proj", tm=1024, tn=768, tk=1024)
    retg, states = _ret_fwd(proj, tables, W["ret_gn_g"], W["ret_gn_b"])
    if sb_comm is None:
        sb, sb_a = _sb_fwd(proj)
    else:
        (sb, sb_a), landed = _sb_fwd(proj, comm=sb_comm[0])
        sb_comm[1](landed)
    sg = _sgu_fwd(proj, W["sgu_ln_g"], W["sgu_ln_b"], W["sgu_w"], W["sgu_b"])
    merged, br = _merge_fwd(proj, (retg, sb, sg), (W["p_ret"], W["p_sb"], W["p_sgu"]))
    u1, x1, x1_bf = _matmul(merged, W["w_out"], "nn", name="out_ln", tm=512, tn=1024, tk=1024, epi="ln",
                            extra=(x, W["ln1_g"], W["ln1_b"]))
    act = _matmul(x1_bf, W["w_up"], "nn", name="up", tm=1024, tn=1024, tk=1024, epi="relu2")
    u2, x2, x2_bf = _matmul(act, W["w_down"], "nn", name="down_ln", tm=512, tn=1024, tk=4096, epi="ln",
                            extra=(x1, W["ln2_g"], W["ln2_b"]))
    saved = dict(x_bf=x_bf, proj=proj, retg=retg, states=states, sb=sb, sb_a=sb_a, sg=sg, merged=merged, br=br, u1=u1,
                 x1_bf=x1_bf, act=act, u2=u2)
    return x2, x2_bf, saved


def _layer_bwd(d_x2, W, tables, sv, chunk_dtype=None, sb_comm_fn=None, dwin_comm_fn=None, dx_comm_fn=None, target=None):
    dt = F32 if chunk_dtype is None else chunk_dtype
    rows, cols = (None, None) if chunk_dtype is None else ("rows", "cols")
    g, landed = {}, {}
    du2, du2_bf, g["ln2_g"], g["ln2_b"], *sq = _ln_bwd(d_x2, sv["u2"], W["ln2_g"], target=target)
    if sq:
        landed["sq"] = sq[0]
    d_hpre = _matmul(du2_bf, W["w_down"], "nt", name="d_act", tm=1024, tn=1024, tk=1024, epi="drelu2",
                     extra=(sv["act"],), out_dtype=BF16)
    g["w_down"] = _matmul(sv["act"], du2_bf, "tn", name="dw_down", tm=512, tn=1024, tk=4096, out_dtype=dt, chunks=rows)
    g["w_up"] = _matmul(sv["x1_bf"], d_hpre, "tn", name="dw_up", tm=1024, tn=512, tk=4096, out_dtype=dt, chunks=cols)
    d_x1 = _matmul(d_hpre, W["w_up"], "nt", name="d_x1", tm=512, tn=1024, tk=4096, epi="add", extra=(du2,))
    du1, du1_bf, g["ln1_g"], g["ln1_b"] = _ln_bwd(d_x1, sv["u1"], W["ln1_g"])
    d_merged = _matmul(du1_bf, W["w_out"], "nt", name="d_merged", tm=1024, tn=1024, tk=1024)
    g["w_out"] = _matmul(sv["merged"], du1_bf, "tn", name="dw_out", tm=1024, tn=512, tk=4096, out_dtype=dt, chunks=rows)
    d_br, d_gate = _gate_bwd(sv["proj"], sv["br"], d_merged)
    d_branch = []
    for k, (nm, act) in enumerate((("p_ret", sv["retg"]), ("p_sb", sv["sb"]), ("p_sgu", sv["sg"]))):
        d_branch.append(_matmul(d_br[k], W[nm], "nt", name="d_" + nm[2:], tm=1024, tn=512, tk=1024))
        g[nm] = _matmul(act, d_br[k], "tn", name="dw_" + nm[2:], tm=512, tn=1024, tk=2048, out_dtype=dt, chunks=cols)
    d_ret, g["ret_gn_g"], g["ret_gn_b"] = _ret_bwd(sv["proj"], tables, W["ret_gn_g"], W["ret_gn_b"], sv["states"],
                                                   d_branch[0])
    if sb_comm_fn is None:
        d_sq, d_sk, d_sv = _sb_bwd(sv["proj"], sv["sb_a"], d_branch[1])
    else:
        (d_sq, d_sk, d_sv), landed["sb"] = _sb_bwd(sv["proj"], sv["sb_a"], d_branch[1], comm=sb_comm_fn(g))
    d_sgu, g["sgu_ln_g"], g["sgu_ln_b"], g["sgu_w"], g["sgu_b"] = _sgu_bwd(
        sv["proj"], W["sgu_ln_g"], W["sgu_ln_b"], W["sgu_w"], W["sgu_b"], d_branch[2])
    d_proj = [d_ret, d_sq, d_sk, d_sv, d_sgu, d_gate[0], d_gate[1], d_gate[2]]
    g["w_in"] = _matmul(d_proj, sv["x_bf"], "tn", name="dw_in", tm=256, tn=1024, tk=4096, out_dtype=dt, chunks=rows,
                        comm=None if dwin_comm_fn is None else dwin_comm_fn(g))
    if dwin_comm_fn is not None:
        g["w_in"], landed["dwin"] = g["w_in"]
    if chunk_dtype is None:
        g["w_in"] = g["w_in"].T
    d_x = _matmul_rows_of(d_proj, W["w_in_t"], du1, name="d_x", tm=512,
                          comm=None if dx_comm_fn is None else dx_comm_fn(g))
    if dx_comm_fn is not None:
        d_x, landed["dx"] = d_x
    return d_x, g, landed


BIG = ("w_in", "p_ret", "p_sb", "p_sgu", "w_out", "w_up", "w_down")
SMALL = ("ret_gn_g", "ret_gn_b", "sgu_ln_g", "sgu_ln_b", "sgu_w", "sgu_b", "ln1_g", "ln1_b", "ln2_g", "ln2_b")
GATHER_KIND = {"w_in": "rows", "p_ret": "cols", "p_sb": "cols", "p_sgu": "cols", "w_out": "rows", "w_up": "cols",
               "w_down": "rows"}


def _small_weights(small, l):
    W = {}
    for n in SMALL:
        if n == "sgu_w":
            W[n] = small[n][l]
        elif n == "sgu_b":
            W[n] = small[n][l].reshape(4, CHUNK, 1)
        else:
            W[n] = small[n][l].reshape(1, -1)
    return W


def _local_step(x, target, full, small):
    tables = _ret_tables(x.shape[0])
    Ws = [{**{n: full[n][l] for n in BIG[1:]}, "w_in_t": full["w_in"][l].T, **_small_weights(small, l)}
          for l in range(DEPTH)]
    saved = []
    h, h_bf = x, _bf(x)
    for l in range(DEPTH):
        h, h_bf, sv = _layer_fwd(h, h_bf, Ws[l], tables)
        saved.append(sv)
    grads = [None] * DEPTH
    d_h, grads[-1], landed = _layer_bwd(h, Ws[-1], tables, saved[-1], target=target)
    for l in reversed(range(DEPTH - 1)):
        d_h, grads[l], _ = _layer_bwd(d_h, Ws[l], tables, saved[l])
    return landed["sq"], d_h, grads


def _adam(w, parts, m, v, name):
    L, R, C = w.shape
    tr = next(t for t in (320, 256, 128) if R % t == 0)
    assert len(parts) == L

    def body(*refs):
        w_ref, p_refs, (m_ref, v_ref, g_ref, d_ref, nm_ref, nv_ref) = refs[0], refs[1:1 + L], refs[1 + L:]
        layer = pl.program_id(0)
        g = None
        for li, p_ref in enumerate(p_refs):
            s = p_ref[0].astype(F32)
            for j in range(1, p_ref.shape[0]):
                s = s + p_ref[j].astype(F32)
            g = s if g is None else jnp.where(layer == li, s, g)
        g_ref[...] = g
        d_ref[...], nm_ref[...], nv_ref[...] = _adam_update(w_ref[...], g, m_ref[...], v_ref[...])

    tile = pl.BlockSpec((None, tr, C), lambda l, i: (l, i, 0))
    part = lambda li: pl.BlockSpec((parts[li].shape[0], tr, C), lambda l, i, li=li: (0, jnp.where(l == li, i, 0), 0))
    out = jax.ShapeDtypeStruct((L, R, C), F32)
    return pl.pallas_call(
        body, name=name, grid=(L, R // tr), out_shape=(out, out, out, out),
        in_specs=[tile] + [part(li) for li in range(L)] + [tile, tile],
        out_specs=(tile, tile, tile, tile),
        compiler_params=_cparams(("parallel", "parallel")),
    )(w, *parts, m, v)


def _adam_update(w, g, m, v):
    m2 = ADAM_B1 * m + (1.0 - ADAM_B1) * g
    v2 = ADAM_B2 * v + (1.0 - ADAM_B2) * (g * g)
    m_hat = m2 / (1.0 - ADAM_B1 ** ADAM_STEP)
    v_hat = v2 / (1.0 - ADAM_B2 ** ADAM_STEP)
    return -ADAM_LR * (m_hat / (jnp.sqrt(v_hat) + ADAM_EPS) + ADAM_WD * w), m2, v2


def _adam_small(w, m, v, parts):
    k = len(SMALL)

    def body(*refs):
        w_refs, m_refs, v_refs, p_refs, outs = refs[:k], refs[k:2 * k], refs[2 * k:3 * k], refs[3 * k:5 * k], refs[5 * k:]
        for i in range(k):
            vector = len(w_refs[i].shape) == 2
            for l in range(DEPTH):
                p_ref = p_refs[DEPTH * i + l]
                g = p_ref[0]
                for j in range(1, N_DEV):
                    g = g + p_ref[j]
                at = (slice(l, l + 1), slice(None)) if vector else (l,)
                delta, m2, v2 = _adam_update(w_refs[i][at], g, m_refs[i][at], v_refs[i][at])
                for o_ref, val in zip(outs[4 * i:4 * i + 4], (g, delta, m2, v2)):
                    o_ref[at] = val

    vmem = pl.BlockSpec(memory_space=pltpu.VMEM)
    args = [w[n] for n in SMALL] + [m[n] for n in SMALL] + [v[n] for n in SMALL] + \
           [parts[(n, l)] for n in SMALL for l in range(DEPTH)]
    out_shape = [jax.ShapeDtypeStruct(w[n].shape, F32) for n in SMALL for _ in range(4)]
    outs = pl.pallas_call(body, name="adam_small", out_shape=out_shape, in_specs=[vmem] * len(args),
                          out_specs=[vmem] * len(out_shape), compiler_params=_cparams())(*args)
    return {n: tuple(outs[4 * i:4 * i + 4]) for i, n in enumerate(SMALL)}


WEIGHTS = ("w_in", "ret_gn_g", "ret_gn_b", "sgu_ln_g", "sgu_ln_b", "sgu_w", "sgu_b", "p_ret", "p_sb", "p_sgu", "w_out",
           "ln1_g", "ln1_b", "w_up", "w_down", "ln2_g", "ln2_b")


def kernel(x, w_in, ret_gn_g, ret_gn_b, sgu_ln_g, sgu_ln_b, sgu_w, sgu_b, p_ret, p_sb, p_sgu, w_out, ln1_g, ln1_b, w_up, w_down, ln2_g, ln2_b, loss_target, m_w_in, m_ret_gn_g, m_ret_gn_b, m_sgu_ln_g, m_sgu_ln_b, m_sgu_w, m_sgu_b, m_p_ret, m_p_sb, m_p_sgu, m_w_out, m_ln1_g, m_ln1_b, m_w_up, m_w_down, m_ln2_g, m_ln2_b, v_w_in, v_ret_gn_g, v_ret_gn_b, v_sgu_ln_g, v_sgu_ln_b, v_sgu_w, v_sgu_b, v_p_ret, v_p_sb, v_p_sgu, v_w_out, v_ln1_g, v_ln1_b, v_w_up, v_w_down, v_ln2_g, v_ln2_b):
    w = dict(zip(WEIGHTS, (w_in, ret_gn_g, ret_gn_b, sgu_ln_g, sgu_ln_b, sgu_w, sgu_b, p_ret, p_sb, p_sgu, w_out,
                           ln1_g, ln1_b, w_up, w_down, ln2_g, ln2_b)))
    m = dict(zip(WEIGHTS, (m_w_in, m_ret_gn_g, m_ret_gn_b, m_sgu_ln_g, m_sgu_ln_b, m_sgu_w, m_sgu_b, m_p_ret, m_p_sb,
                           m_p_sgu, m_w_out, m_ln1_g, m_ln1_b, m_w_up, m_w_down, m_ln2_g, m_ln2_b)))
    v = dict(zip(WEIGHTS, (v_w_in, v_ret_gn_g, v_ret_gn_b, v_sgu_ln_g, v_sgu_ln_b, v_sgu_w, v_sgu_b, v_p_ret, v_p_sb,
                           v_p_sgu, v_w_out, v_ln1_g, v_ln1_b, v_w_up, v_w_down, v_ln2_g, v_ln2_b)))

    small = {n: w[n] for n in SMALL}
    shard = {n: _bf(w[n]) for n in BIG}
    shard["w_in"] = shard["w_in"].transpose(0, 2, 1)
    S = x.shape[1]
    x0, target = x.reshape(S, D_MODEL), loss_target.reshape(S, D_MODEL)
    tables = _ret_tables(S)
    Ws = [_small_weights(small, l) for l in range(DEPTH)]

    (Ws[0]["w_in_t"],) = _exchange([_gather_transfer(shard["w_in"], 0, "rows")], "gather_w_in0", relay=True)
    def gather_under_sb(keys):
        def landed_fn(landed):
            for (n, l), z in zip(keys, landed):
                Ws[l]["w_in_t" if n == "w_in" else n] = z

        return _Comm([_gather_transfer(shard[n], l, GATHER_KIND[n]) for n, l in keys], relay=True), landed_fn

    h, h_bf, saved0 = _layer_fwd(x0, _bf(x0), Ws[0], tables,
                                 sb_comm=gather_under_sb([(n, 0) for n in BIG[1:]] + [("w_in", 1)]))
    h, _, saved1 = _layer_fwd(h, h_bf, Ws[1], tables, sb_comm=gather_under_sb([(n, 1) for n in BIG[1:]]))
    d_h, g1, landed1 = _layer_bwd(h, Ws[1], tables, saved1, chunk_dtype=BF16, target=target,
                                  sb_comm_fn=lambda g: _Comm([_scatter_transfer(g[n]) for n in BIG[1:]]))
    loss = lax.psum(0.5 * jnp.sum(landed1["sq"]) / D_MODEL, ("x", "y", "c"))
    early = [("w_in", 1)] + [(n, 0) for n in BIG[1:]]

    def small_slabs(g):
        return [_slab_transfer(g[n].reshape(4, CHUNK) if n == "sgu_b" else g[n]) for n in SMALL]

    def early_scatter(g0):
        return _Comm([_scatter_transfer((g1 if l else g0)[n]) for n, l in early] + small_slabs(g1))

    def late_scatter(g0):
        pairs = _pair_reduce(g0["w_in"], "w_in0_pairs")
        return _Comm([_chip_scatter_transfer(pairs)])

    d_x, g0, landed = _layer_bwd(d_h, Ws[0], tables, saved0, chunk_dtype=BF16, sb_comm_fn=early_scatter,
                                 dwin_comm_fn=lambda g: _Comm(small_slabs(g)), dx_comm_fn=late_scatter)
    parts = {**dict(zip(early, landed["sb"])), **{(n, 1): z for n, z in zip(BIG[1:], landed1["sb"])}}
    parts[("w_in", 0)] = landed["dx"][0]
    small_parts = {**{(n, 1): z for n, z in zip(SMALL, landed["sb"][len(early):])},
                   **{(n, 0): z for n, z in zip(SMALL, landed["dwin"])}}

    grad, delta, new_m, new_v = {}, {}, {}, {}
    for n in BIG:
        view = (lambda a: a.transpose(0, 2, 1)) if n == "w_in" else (lambda a: a)
        res = _adam(view(w[n]), [parts[(n, l)] for l in range(DEPTH)], view(m[n]), view(v[n]), "adam_" + n)
        grad[n], delta[n], new_m[n], new_v[n] = (view(r) for r in res)
    for n, res in _adam_small(small, m, v, small_parts).items():
        grad[n], delta[n], new_m[n], new_v[n] = res

    return (loss, d_x.reshape(x.shape), *[grad[n] for n in WEIGHTS], *[delta[n] for n in WEIGHTS],
            *[new_m[n] for n in WEIGHTS], *[new_v[n] for n in WEIGHTS])
```

```python
import functools
import math

import numpy as np
import jax
import jax.numpy as jnp
from jax import lax
from jax.experimental import pallas as pl
from jax.experimental.pallas import tpu as pltpu

F32 = jnp.float32
BF16 = jnp.bfloat16

N_DEV = 8
DEPTH = 2
D_MODEL = 1024
CHUNK = 128
RET_W = 512
SB_W = 512
SGU_W = 512
N_IN = 7680
LN_EPS = 1e-5
ALPHA = (2 * DEPTH) ** 0.25
ROPE_BASE = 10000.0
ADAM_LR, ADAM_B1, ADAM_B2, ADAM_EPS, ADAM_WD, ADAM_STEP = 0.001, 0.9, 0.999, 1e-08, 0.01, 10
VMEM_LIMIT = 56 * 1024 * 1024

_GELU_K = math.sqrt(2.0 / math.pi)
_GELU_C = 0.044715


def _cparams(sem=None):
    return pltpu.CompilerParams(dimension_semantics=sem, vmem_limit_bytes=VMEM_LIMIT)


def _dg(a, b, ca, cb):
    return lax.dot_general(a, b, (((ca,), (cb,)), ((), ())), preferred_element_type=F32)


def _bf(x):
    return x.astype(BF16)


def _sigmoid(x):
    return 1.0 / (1.0 + jnp.exp(-x))


def _gelu(x):
    t = jnp.tanh(_GELU_K * (x + _GELU_C * (x * x * x)))
    return x * (0.5 * (1.0 + t))


def _gelu_grad(x):
    t = jnp.tanh(_GELU_K * (x + _GELU_C * (x * x * x)))
    return 0.5 * (1.0 + t) + 0.5 * x * (1.0 - t * t) * (_GELU_K * (1.0 + 3.0 * _GELU_C * x * x))


def _norm_stats(u):
    mu = jnp.mean(u, axis=-1, keepdims=True)
    d = u - mu
    var = jnp.mean(d * d, axis=-1, keepdims=True)
    rstd = lax.rsqrt(var + LN_EPS)
    return d * rstd, rstd


def _norm_bwd(dxh, xh, rstd):
    return rstd * (dxh - jnp.mean(dxh, axis=-1, keepdims=True) - xh * jnp.mean(dxh * xh, axis=-1, keepdims=True))


class _Transfer:
    def __init__(self, src, dst_shape, src_at, dst_at, same_core=False):
        self.src, self.dst_shape, self.src_at, self.dst_at = src, tuple(dst_shape), src_at, dst_at
        self.same_core = same_core


def _gather_transfer(shard, l, kind):
    _, r, c = shard.shape
    src_at = lambda ref, p: ref.at[l]
    if kind == "slab":
        return _Transfer(shard, (N_DEV, r, c), src_at, lambda ref, s: ref.at[s])
    if kind == "rows":
        return _Transfer(shard, (N_DEV * r, c), src_at, lambda ref, s: ref.at[pl.ds(pl.multiple_of(s * r, r), r), :])
    return _Transfer(shard, (r, N_DEV * c), src_at, lambda ref, s: ref.at[:, pl.ds(pl.multiple_of(s * c, c), c)])


def _scatter_transfer(chunks):
    return _Transfer(chunks, chunks.shape, lambda ref, p: ref.at[p], lambda ref, s: ref.at[s])


def _slab_transfer(arr):
    return _Transfer(arr, (N_DEV,) + arr.shape, lambda ref, p: ref, lambda ref, s: ref.at[s])


class _Comm:
    def __init__(self, transfers, relay=False):
        self.transfers = list(transfers)
        self.relay = relay
        self.n = len(self.transfers)
        self.arrays = [t.src for t in self.transfers]
        self.out_shape = [jax.ShapeDtypeStruct(t.dst_shape, t.src.dtype) for t in self.transfers]
        self.scratch = [pltpu.SemaphoreType.DMA((self.n * (N_DEV - 1),)), pltpu.SemaphoreType.DMA((self.n * (N_DEV - 1),)),
                        pltpu.SemaphoreType.DMA((self.n,))]

    def _relay_copies(self, srcs, dsts, send_sems, recv_sems, local_sems):
        x, y, c = lax.axis_index("x"), lax.axis_index("y"), lax.axis_index("c")
        me = 4 * x + 2 * y + c
        chips = [(1 - x, y), (x, 1 - y), (1 - x, 1 - y)]
        first, passed, own = [], [], []
        for t, tr in enumerate(self.transfers):
            def copy(k, src, sender, to, t=t, tr=tr):
                return pltpu.make_async_remote_copy(
                    src_ref=src, dst_ref=tr.dst_at(dsts[t], sender), send_sem=send_sems.at[t * (N_DEV - 1) + k],
                    recv_sem=recv_sems.at[t * (N_DEV - 1) + k], device_id=to, device_id_type=pl.DeviceIdType.MESH)

            mine = tr.src_at(srcs[t], me)
            first.append([copy(0, mine, me, (x, y, 1 - c))] + [copy(1 + j, mine, me, (px, py, c))
                                                                for j, (px, py) in enumerate(chips)])
            passed.append([copy(4 + j, tr.dst_at(dsts[t], 4 * px + 2 * py + c), 4 * px + 2 * py + c, (x, y, 1 - c))
                           for j, (px, py) in enumerate(chips)])
            own.append(pltpu.make_async_copy(mine, tr.dst_at(dsts[t], me), local_sems.at[t]))
        return first, passed, own

    def _copies(self, srcs, dsts, send_sems, recv_sems, local_sems):
        x, y, c = lax.axis_index("x"), lax.axis_index("y"), lax.axis_index("c")
        me = 4 * x + 2 * y + c
        copies = []
        for d in range(1, N_DEV):
            px = 1 - x if d & 4 else x
            py = 1 - y if d & 2 else y
            pc = 1 - c if d & 1 else c
            for t, tr in enumerate(self.transfers):
                if tr.same_core and d & 1:
                    continue
                peer, mine = (2 * px + py, 2 * x + y) if tr.same_core else (4 * px + 2 * py + pc, me)
                k = t * (N_DEV - 1) + d - 1
                copies.append(pltpu.make_async_remote_copy(
                    src_ref=tr.src_at(srcs[t], peer), dst_ref=tr.dst_at(dsts[t], mine),
                    send_sem=send_sems.at[k], recv_sem=recv_sems.at[k],
                    device_id=(px, py, pc), device_id_type=pl.DeviceIdType.MESH))
        own = []
        for t, tr in enumerate(self.transfers):
            mine = 2 * x + y if tr.same_core else me
            own.append(pltpu.make_async_copy(tr.src_at(srcs[t], mine), tr.dst_at(dsts[t], mine), local_sems.at[t]))
        return copies, own

    def start(self, srcs, dsts, *sems):
        if self.relay:
            first, _, own = self._relay_copies(srcs, dsts, *sems)
            for cp in own + [cp for per_t in first for cp in per_t]:
                cp.start()
            return
        copies, own = self._copies(srcs, dsts, *sems)
        for cp in own + copies:
            cp.start()

    def pass_on(self, srcs, dsts, *sems):
        if self.relay:
            first, passed, _ = self._relay_copies(srcs, dsts, *sems)
            for j in range(3):
                for t in range(self.n):
                    first[t][1 + j].wait_recv()
                    passed[t][j].start()

    def finish(self, srcs, dsts, *sems):
        if self.relay:
            first, passed, own = self._relay_copies(srcs, dsts, *sems)
            for t in range(self.n):
                first[t][0].wait_recv()
                for cp in passed[t]:
                    cp.wait_recv()
            for t in range(self.n):
                for cp in first[t] + passed[t]:
                    cp.wait_send()
                own[t].wait()
            return
        copies, own = self._copies(srcs, dsts, *sems)
        for cp in copies + own:
            cp.wait()


def _pcall(body, *, name, grid, in_specs, out_specs, out_shape, scratch_shapes, sem, args, comm=None):
    in_specs, out_specs, out_shape = list(in_specs), list(out_specs), list(out_shape)
    if comm is None:
        outs = pl.pallas_call(body, name=name, grid=grid, in_specs=in_specs, out_specs=out_specs, out_shape=out_shape,
                              scratch_shapes=list(scratch_shapes), compiler_params=_cparams(sem))(*args)
        return list(outs), []
    n_in, n_out, n_scr, k = len(in_specs), len(out_specs), len(scratch_shapes), comm.n

    def carrier(*refs):
        ins, cin = refs[:n_in], refs[n_in:n_in + k]
        outs, cout = refs[n_in + k:n_in + k + n_out], refs[n_in + k + n_out:n_in + 2 * k + n_out]
        scr, sems = refs[n_in + 2 * k + n_out:n_in + 2 * k + n_out + n_scr], refs[n_in + 2 * k + n_out + n_scr:]
        ids = [pl.program_id(d) for d in range(len(grid))]
        first = functools.reduce(jnp.logical_and, [i == 0 for i in ids])
        last = functools.reduce(jnp.logical_and, [i == g - 1 for i, g in zip(ids, grid)])

        @pl.when(first)
        def _():
            comm.start(cin, cout, *sems)

        body(*ins, *outs, *scr)

        early_pass = comm.relay and len(grid) > 1 and grid[0] > 1
        if early_pass:
            @pl.when(functools.reduce(jnp.logical_and, [ids[0] == grid[0] - 1] + [i == 0 for i in ids[1:]]))
            def _():
                comm.pass_on(cin, cout, *sems)

        @pl.when(last)
        def _():
            if not early_pass:
                comm.pass_on(cin, cout, *sems)
            comm.finish(cin, cout, *sems)

    hbm = pl.BlockSpec(memory_space=pl.ANY)
    outs = pl.pallas_call(
        carrier, name=name, grid=grid, in_specs=in_specs + [hbm] * k, out_specs=out_specs + [hbm] * k,
        out_shape=out_shape + comm.out_shape, scratch_shapes=list(scratch_shapes) + comm.scratch,
        compiler_params=_cparams(tuple("arbitrary" for _ in grid)),
    )(*args, *comm.arrays)
    return list(outs[:n_out]), list(outs[n_out:])


def _exchange(transfers, name, relay=False):
    comm = _Comm(transfers, relay)

    def body(*refs):
        k = comm.n
        comm.start(refs[:k], refs[k:2 * k], *refs[2 * k:])
        comm.pass_on(refs[:k], refs[k:2 * k], *refs[2 * k:])
        comm.finish(refs[:k], refs[k:2 * k], *refs[2 * k:])

    hbm = pl.BlockSpec(memory_space=pl.ANY)
    return pl.pallas_call(body, name=name, out_shape=comm.out_shape, in_specs=[hbm] * comm.n, out_specs=[hbm] * comm.n,
                          scratch_shapes=comm.scratch)(*comm.arrays)


def _pair_reduce(chunks, name, tr=320):
    _, r, c = chunks.shape
    tr = min(tr, r)
    assert r % tr == 0

    def swap(src_ref, dst_ref, send_sems, recv_sems):
        x, y, core = lax.axis_index("x"), lax.axis_index("y"), lax.axis_index("c")
        copies = [pltpu.make_async_remote_copy(
            src_ref=src_ref.at[2 * k + 1 - core], dst_ref=dst_ref.at[k], send_sem=send_sems.at[k],
            recv_sem=recv_sems.at[k], device_id=(x, y, 1 - core), device_id_type=pl.DeviceIdType.MESH) for k in range(4)]
        for cp in copies:
            cp.start()
        for cp in copies:
            cp.wait()

    hbm = pl.BlockSpec(memory_space=pl.ANY)
    theirs = pl.pallas_call(swap, name=name + "_swap", out_shape=jax.ShapeDtypeStruct((4, r, c), chunks.dtype),
                            in_specs=[hbm], out_specs=hbm,
                            scratch_shapes=[pltpu.SemaphoreType.DMA((4,)), pltpu.SemaphoreType.DMA((4,))])(chunks)

    def add(mine_ref, theirs_ref, out_ref):
        core = lax.axis_index("c")
        both = mine_ref[...].astype(F32)
        out_ref[...] = (jnp.where(core == 0, both[0], both[1]) + theirs_ref[...].astype(F32)).astype(out_ref.dtype)

    return pl.pallas_call(
        add, name=name + "_add", grid=(4, r // tr), out_shape=jax.ShapeDtypeStruct((4, r, c), chunks.dtype),
        in_specs=[pl.BlockSpec((None, 2, tr, c), lambda k, i: (k, 0, i, 0)), pl.BlockSpec((None, tr, c), lambda k, i: (k, i, 0))],
        out_specs=pl.BlockSpec((None, tr, c), lambda k, i: (k, i, 0)),
        compiler_params=_cparams(("parallel", "parallel")),
    )(chunks.reshape(4, 2, r, c), theirs)


def _chip_scatter_transfer(pairs):
    return _Transfer(pairs, pairs.shape, lambda ref, p: ref.at[p], lambda ref, s: ref.at[s], same_core=True)


def _matmul(a, b, mode, *, name, tm, tn, tk, epi=None, extra=(), out_dtype=F32, chunks=None, comm=None):
    pieces = list(a) if isinstance(a, (list, tuple)) else [a]
    rows_a, cols_a = pieces[0].shape[0], sum(p.shape[1] for p in pieces)
    if mode == "nn":
        (M, K), N = (rows_a, cols_a), b.shape[1]
    elif mode == "nt":
        (M, K), N = (rows_a, cols_a), b.shape[0]
    else:
        (K, M), N = (rows_a, cols_a), b.shape[1]
    tm, tn, tk = min(tm, M), min(tn, N), min(tk, K)
    assert M % tm == 0 and N % tn == 0 and K % tk == 0 and (epi != "ln" or tn == N), (name, M, N, K)
    nk = K // tk
    tile_cols, axis = (tm, 0) if mode == "tn" else (tk, 2)
    assert all(p.shape[1] % tile_cols == 0 for p in pieces)
    counts = [p.shape[1] // tile_cols for p in pieces]
    starts = [sum(counts[:q]) for q in range(len(pieces))]

    def a_spec_of(q):
        at = lambda t: jnp.clip(t - starts[q], 0, counts[q] - 1) if len(pieces) > 1 else t
        return {"nn": pl.BlockSpec((tm, tk), lambda i, j, k: (i, at(k))),
                "nt": pl.BlockSpec((tm, tk), lambda i, j, k: (i, at(k))),
                "tn": pl.BlockSpec((tk, tm), lambda i, j, k: (k, at(i)))}[mode]

    n_a = len(pieces)
    b_mode = pl.Buffered(1) if (nk == 1 and tn == N and n_a > 1) else None
    b_spec = {"nn": pl.BlockSpec((tk, tn), lambda i, j, k: (k, j), pipeline_mode=b_mode),
              "nt": pl.BlockSpec((tn, tk), lambda i, j, k: (j, k), pipeline_mode=b_mode),
              "tn": pl.BlockSpec((tk, tn), lambda i, j, k: (k, j), pipeline_mode=b_mode)}[mode]
    ca, cb = {"nn": (1, 0), "nt": (1, 1), "tn": (0, 0)}[mode]
    tile = pl.BlockSpec((tm, tn), lambda i, j, k: (i, j))
    row = pl.BlockSpec((1, tn), lambda i, j, k: (0, j))
    n_extra = {None: 0, "add": 1, "relu2": 0, "drelu2": 1, "ln": 3}[epi]
    assert len(extra) == n_extra
    extra_specs = {None: [], "add": [tile], "relu2": [], "drelu2": [tile], "ln": [tile, row, row]}[epi]
    split = 0
    if epi == "relu2":
        out_shape, out_specs = (jax.ShapeDtypeStruct((M, N), BF16),), (tile,)
    elif epi == "ln":
        out_shape = (jax.ShapeDtypeStruct((M, N), F32), jax.ShapeDtypeStruct((M, N), F32),
                     jax.ShapeDtypeStruct((M, N), BF16))
        out_specs = (tile, tile, tile)
    elif chunks == "cols":
        c = N // N_DEV
        out_shape = (jax.ShapeDtypeStruct((N_DEV, M, c), out_dtype),)
        if tn == N:
            split = c
            out_specs = (pl.BlockSpec((N_DEV, tm, c), lambda i, j, k: (0, i, 0)),)
        else:
            assert c % tn == 0
            out_specs = (pl.BlockSpec((None, tm, tn), lambda i, j, k: (j // (c // tn), i, j % (c // tn))),)
    else:
        out_shape, out_specs = (jax.ShapeDtypeStruct((M, N), out_dtype),), (tile,)
    n_out = len(out_shape)

    def body(*refs):
        a_refs, b_ref = refs[:n_a], refs[n_a]
        ex = refs[n_a + 1:n_a + 1 + n_extra]
        outs = refs[n_a + 1 + n_extra:n_a + 1 + n_extra + n_out]
        acc_ref = refs[-1]
        k = pl.program_id(2)

        def finish(acc):
            if epi == "add":
                outs[0][...] = (acc + ALPHA * ex[0][...]).astype(out_dtype)
            elif epi == "relu2":
                r = jnp.maximum(acc, 0.0)
                outs[0][...] = _bf(r * r)
            elif epi == "drelu2":
                outs[0][...] = (acc * (2.0 * jnp.sqrt(ex[0][...].astype(F32)))).astype(out_dtype)
            elif epi == "ln":
                u = ALPHA * ex[0][...] + acc
                xh, _ = _norm_stats(u)
                y = xh * ex[1][...] + ex[2][...]
                outs[0][...] = u
                outs[1][...] = y
                outs[2][...] = _bf(y)
            elif split:
                for p in range(N_DEV):
                    outs[0][p] = acc[:, p * split:(p + 1) * split].astype(out_dtype)
            else:
                outs[0][...] = acc.astype(out_dtype)

        def step(a_ref, first, middle, last):
            part = _dg(_bf(a_ref[...]), _bf(b_ref[...]), ca, cb)
            if nk == 1:
                finish(part)
                return
            if first:
                @pl.when(k == 0)
                def _():
                    acc_ref[...] = part

            if middle:
                @pl.when(jnp.logical_and(k > 0, k < nk - 1))
                def _():
                    acc_ref[...] += part

            if last:
                @pl.when(k == nk - 1)
                def _():
                    finish(acc_ref[...] + part)

        if n_a == 1:
            step(a_refs[0], True, True, True)
        else:
            t = pl.program_id(axis)
            for q in range(n_a):
                along_k = axis == 2
                first = not along_k or starts[q] == 0
                last = not along_k or starts[q] + counts[q] == nk
                middle = not along_k or counts[q] > int(first) + int(last)

                @pl.when(jnp.logical_and(t >= starts[q], t < starts[q] + counts[q]))
                def _(q=q, first=first, middle=middle, last=last):
                    step(a_refs[q], first, middle, last)

    outs, landed = _pcall(
        body, name=name, out_shape=out_shape, grid=(M // tm, N // tn, nk),
        in_specs=[a_spec_of(q) for q in range(n_a)] + [b_spec] + extra_specs, out_specs=out_specs,
        scratch_shapes=[pltpu.VMEM((tm, tn) if nk > 1 else (8, 128), F32)], sem=("parallel", "parallel", "arbitrary"),
        args=(*pieces, b, *extra), comm=comm)
    res = outs[0] if n_out == 1 else tuple(outs)
    if chunks == "rows":
        res = res.reshape(N_DEV, M // N_DEV, N)
    return res if comm is None else (res, landed)


def _matmul_rows_of(pieces, b, res, *, name, tm, comm=None):
    M, (K, N) = pieces[0].shape[0], b.shape
    tm = min(tm, M)
    subs, start = [], 0
    for q, p in enumerate(pieces):
        w = p.shape[1]
        step = w if start % w == 0 else 512
        assert w % step == 0 and start % step == 0
        subs += [(q, off, step, start + off) for off in range(0, w, step)]
        start += w
    assert start == K
    n_p, n_s = len(pieces), len(subs)

    def body(*refs):
        a_refs, b_refs, res_ref, out_ref = refs[:n_p], refs[n_p:n_p + n_s], refs[n_p + n_s], refs[n_p + n_s + 1]
        acc = None
        for (q, off, w, _), b_ref in zip(subs, b_refs):
            part = _dg(_bf(a_refs[q][:, off:off + w]), _bf(b_ref[...]), 1, 0)
            acc = part if acc is None else acc + part
        out_ref[...] = acc + ALPHA * res_ref[...]

    tile = pl.BlockSpec((tm, N), lambda i: (i, 0))
    outs, landed = _pcall(
        body, name=name, grid=(M // tm,), out_shape=[jax.ShapeDtypeStruct((M, N), F32)],
        in_specs=[pl.BlockSpec((tm, p.shape[1]), lambda i: (i, 0)) for p in pieces] +
                 [pl.BlockSpec((w, N), lambda i, r=row // w: (r, 0), pipeline_mode=pl.Buffered(1))
                  for _, _, w, row in subs] + [tile],
        out_specs=[tile], scratch_shapes=[], sem=("parallel",), args=(*pieces, *([b] * n_s), res), comm=comm)
    return outs[0] if comm is None else (outs[0], landed)


def _ret_tables(S):
    half = 64
    inv_freq = ROPE_BASE ** (-jnp.arange(half, dtype=F32) / half)
    ang = jnp.arange(S, dtype=jnp.int32).astype(F32)[:, None] * inv_freq[None, :]
    cos, sin = jnp.cos(ang), jnp.sin(ang)
    cosf = jnp.concatenate([cos, cos], axis=1)
    sinf = jnp.concatenate([-sin, sin], axis=1)
    log_g = jnp.log(1.0 - 2.0 ** (-5.0 - jnp.arange(4, dtype=F32)))
    idx = jnp.arange(CHUNK, dtype=F32)
    diff = idx[:, None] - idx[None, :]
    md = jnp.where(diff[None] >= 0, jnp.exp(log_g[:, None, None] * diff[None]), 0.0)
    kd = jnp.exp(log_g[:, None] * (CHUNK - 1 - idx)[None, :])
    qd = jnp.exp(log_g[:, None] * (idx + 1.0)[None, :])
    cd = jnp.exp(log_g * CHUNK)
    bc = lambda t: jnp.broadcast_to(t[:, :, None], (4, CHUNK, CHUNK))
    return cosf, sinf, md, bc(qd), bc(kd), jnp.broadcast_to(cd[:, None, None], (4, 8, CHUNK))


def _rot(x, cosf, sinf):
    return x * cosf + pltpu.roll(x, 64, 1) * sinf


def _rot_t(dx, cosf, sinf):
    return dx * cosf - pltpu.roll(dx, 64, 1) * sinf


RET_CHUNKS = 2


def _ret_specs(rev, S):
    R = min(RET_CHUNKS, S // CHUNK)
    rows, steps = R * CHUNK, S // (R * CHUNK)
    rn = (lambda n: steps - 1 - n) if rev else (lambda n: n)
    col = lambda c: pl.BlockSpec((rows, 512), lambda n, c=c: (rn(n), c))
    tab = pl.BlockSpec((rows, CHUNK), lambda n: (rn(n), 0))
    dec = pl.BlockSpec((4, CHUNK, CHUNK), lambda n: (0, 0, 0))
    cdec = pl.BlockSpec((4, 8, CHUNK), lambda n: (0, 0, 0))
    vec = pl.BlockSpec((1, 512), lambda n: (0, 0))
    st = pl.BlockSpec((R, 4, CHUNK, CHUNK), lambda n: (rn(n), 0, 0, 0))
    return R, steps, rn, col, tab, dec, cdec, vec, st


def _ret_fwd(proj, tables, gn_g, gn_b):
    S = proj.shape[0]
    R, steps, _, col, tab, dec, cdec, vec, st = _ret_specs(False, S)

    def body(q_ref, k_ref, v_ref, g_ref, cos_ref, sin_ref, md_ref, qd_ref, kd_ref, cd_ref, gng_ref, gnb_ref,
             out_ref, st_ref, state):
        @pl.when(pl.program_id(0) == 0)
        def _():
            state[...] = jnp.zeros_like(state)

        tiles = [(c, h) for c in range(R) for h in range(4)]
        rs = lambda c: slice(c * CHUNK, (c + 1) * CHUNK)
        sl = lambda h: slice(h * 128, (h + 1) * 128)
        qr = [_rot(q_ref[rs(c), sl(h)].astype(F32), cos_ref[rs(c), :], sin_ref[rs(c), :]) for c, h in tiles]
        kr = [_rot(k_ref[rs(c), sl(h)].astype(F32), cos_ref[rs(c), :], sin_ref[rs(c), :]) * (128 ** -0.5)
              for c, h in tiles]
        vb = [_bf(v_ref[rs(c), sl(h)]) for c, h in tiles]
        kv = [_dg(_bf(k * kd_ref[h]), v, 0, 0) for k, v, (c, h) in zip(kr, vb, tiles)]
        before = {}
        for h in range(4):
            s_h = state[h]
            for c in range(R):
                st_ref[c, h] = s_h
                before[(c, h)] = s_h
                s_h = s_h * cd_ref[h, 0:1, :] + kv[c * 4 + h]
            state[h] = s_h
        sc = [_dg(_bf(q), _bf(k), 1, 1) * md_ref[h] for q, k, (c, h) in zip(qr, kr, tiles)]
        r = [_dg(_bf(x), v, 1, 0) + _dg(_bf(q * qd_ref[h]), _bf(before[(c, h)]), 1, 0)
             for x, v, q, (c, h) in zip(sc, vb, qr, tiles)]
        for x, (c, h) in zip(r, tiles):
            y, _ = _norm_stats(x)
            rg = g_ref[rs(c), sl(h)].astype(F32)
            out_ref[rs(c), sl(h)] = rg * _sigmoid(rg) * (y * gng_ref[:, sl(h)] + gnb_ref[:, sl(h)])

    return pl.pallas_call(
        body, name="ret_fwd", grid=(steps,),
        out_shape=(jax.ShapeDtypeStruct((S, RET_W), F32), jax.ShapeDtypeStruct((S // CHUNK, 4, CHUNK, CHUNK), F32)),
        in_specs=[col(0), col(1), col(2), col(3), tab, tab, dec, dec, dec, cdec, vec, vec],
        out_specs=(pl.BlockSpec((R * CHUNK, 512), lambda n: (n, 0)), st),
        scratch_shapes=[pltpu.VMEM((4, CHUNK, CHUNK), F32)],
        compiler_params=_cparams(("arbitrary",)),
    )(proj, proj, proj, proj, *tables, gn_g, gn_b)


def _ret_bwd(proj, tables, gn_g, gn_b, states, d_out):
    S = proj.shape[0]
    R, steps, rn, col, tab, dec, cdec, vec, st = _ret_specs(True, S)

    def kernel_body(q_ref, k_ref, v_ref, g_ref, cos_ref, sin_ref, md_ref, qd_ref, kd_ref, cd_ref, gng_ref, gnb_ref,
                    st_ref, do_ref, dp_ref, dg_ref, db_ref, gstate):
        @pl.when(pl.program_id(0) == 0)
        def _():
            gstate[...] = jnp.zeros_like(gstate)
            dg_ref[...] = jnp.zeros_like(dg_ref)
            db_ref[...] = jnp.zeros_like(db_ref)

        tiles = [(c, h) for c in range(R) for h in range(4)]
        rs = lambda c: slice(c * CHUNK, (c + 1) * CHUNK)
        sl = lambda h: slice(h * 128, (h + 1) * 128)
        rot = lambda ref, c, h: _rot(ref[rs(c), sl(h)].astype(F32), cos_ref[rs(c), :], sin_ref[rs(c), :])
        qr = [rot(q_ref, c, h) for c, h in tiles]
        kr = [rot(k_ref, c, h) * (128 ** -0.5) for c, h in tiles]
        qb, kb = [_bf(x) for x in qr], [_bf(x) for x in kr]
        vb = [_bf(v_ref[rs(c), sl(h)]) for c, h in tiles]
        s0b = [_bf(st_ref[c, h]) for c, h in tiles]
        scb = [_bf(_dg(q, k, 1, 1) * md_ref[h]) for q, k, (c, h) in zip(qb, kb, tiles)]
        qdb = [_bf(q * qd_ref[h]) for q, (c, h) in zip(qr, tiles)]
        kdb = [_bf(k * kd_ref[h]) for k, (c, h) in zip(kr, tiles)]
        r = [_dg(x, v, 1, 0) + _dg(q, s, 1, 0) for x, v, q, s in zip(scb, vb, qdb, s0b)]
        drb, d_rg = [], []
        for x, (c, h) in zip(r, tiles):
            y, rstd = _norm_stats(x)
            gng = gng_ref[:, sl(h)]
            rg = g_ref[rs(c), sl(h)].astype(F32)
            sg = _sigmoid(rg)
            d_o = do_ref[rs(c), sl(h)]
            d_gn = d_o * (rg * sg)
            dg_ref[:, sl(h)] += jnp.sum(d_gn * y, axis=0, keepdims=True)
            db_ref[:, sl(h)] += jnp.sum(d_gn, axis=0, keepdims=True)
            drb.append(_bf(_norm_bwd(d_gn * gng, y, rstd)))
            d_rg.append(_bf(d_o * (y * gng + gnb_ref[:, sl(h)]) * (sg * (1.0 + rg * (1.0 - sg)))))
        grow = [_dg(q, d, 0, 0) for q, d in zip(qdb, drb)]
        after = {}
        for h in range(4):
            g_h = gstate[h]
            for c in reversed(range(R)):
                after[(c, h)] = _bf(g_h)
                g_h = g_h * cd_ref[h, 0:1, :] + grow[c * 4 + h]
            gstate[h] = g_h
        dscb = [_bf(_dg(d, v, 1, 1) * md_ref[h]) for d, v, (c, h) in zip(drb, vb, tiles)]
        for t, (c, h) in enumerate(tiles):
            gb = after[(c, h)]
            dqr = _dg(dscb[t], kb[t], 1, 0) + _dg(drb[t], s0b[t], 1, 1) * qd_ref[h]
            dkr = _dg(dscb[t], qb[t], 0, 0) + _dg(vb[t], gb, 1, 1) * kd_ref[h]
            dv = _dg(scb[t], drb[t], 0, 0) + _dg(kdb[t], gb, 1, 0)
            cosf, sinf = cos_ref[rs(c), :], sin_ref[rs(c), :]
            dp_ref[rs(c), 0 * 512 + h * 128:0 * 512 + (h + 1) * 128] = _bf(_rot_t(dqr, cosf, sinf))
            dp_ref[rs(c), 1 * 512 + h * 128:1 * 512 + (h + 1) * 128] = _bf(_rot_t(dkr, cosf, sinf) * (128 ** -0.5))
            dp_ref[rs(c), 2 * 512 + h * 128:2 * 512 + (h + 1) * 128] = _bf(dv)
            dp_ref[rs(c), 3 * 512 + h * 128:3 * 512 + (h + 1) * 128] = d_rg[t]

    acc = pl.BlockSpec((1, 512), lambda n: (0, 0))
    return pl.pallas_call(
        kernel_body, name="ret_bwd", grid=(steps,),
        out_shape=(jax.ShapeDtypeStruct((S, 2048), BF16), jax.ShapeDtypeStruct((1, 512), F32),
                   jax.ShapeDtypeStruct((1, 512), F32)),
        in_specs=[col(0), col(1), col(2), col(3), tab, tab, dec, dec, dec, cdec, vec, vec, st,
                  pl.BlockSpec((R * CHUNK, 512), lambda n: (rn(n), 0))],
        out_specs=(pl.BlockSpec((R * CHUNK, 2048), lambda n: (rn(n), 0)), acc, acc),
        scratch_shapes=[pltpu.VMEM((4, CHUNK, CHUNK), F32)],
        compiler_params=_cparams(("arbitrary",)),
    )(proj, proj, proj, proj, *tables, gn_g, gn_b, states, d_out)


SB_T = 256
SB_SCALE = 64 ** -0.5
SB_Q_COL, SB_K_COL, SB_V_COL = 2048 // 128, 2560 // 128, 3072 // 128


def _head_masks():
    lane = lax.broadcasted_iota(jnp.int32, (1, 128), 1)
    m0 = (lane < 64).astype(F32)
    return m0, 1.0 - m0


def _tri(n, cmp):
    r = lax.broadcasted_iota(jnp.int32, (n, n), 0)
    c = lax.broadcasted_iota(jnp.int32, (n, n), 1)
    return cmp(r, c)


def _tri_sum(x, tri):
    hi = _bf(x)
    lo = _bf(x - hi.astype(F32))
    return _dg(hi, tri, 1, 0) + _dg(lo, tri, 1, 0)


def _sb_weights(qms, kblks, upper, carry, causal):
    tiles = [(b, h) for b in range(len(kblks)) for h in range(2)]
    zs = [_dg(qms[h], kblks[b], 1, 1) for b, h in tiles]
    lgs = [-(jnp.maximum(z, 0.0) + jnp.log(1.0 + jnp.exp(-jnp.abs(z)))) for z in zs]
    if causal is not None:
        lgs = [jnp.where(causal, lg, 0.0) if b == 0 else lg for lg, (b, h) in zip(lgs, tiles)]
    carries = list(carry)
    for t in range(len(tiles) - 2):
        carries.append(carries[t] + jnp.sum(lgs[t], axis=1, keepdims=True))
    his = [_bf(lg) for lg in lgs]
    los = [_bf(lg - hi.astype(F32)) for lg, hi in zip(lgs, his)]
    later = [_dg(hi, upper, 1, 0) for hi in his]
    later = [r + _dg(lo, upper, 1, 0) for r, lo in zip(later, los)]
    a = [jnp.exp(lg + z + (r + c)) for lg, z, r, c in zip(lgs, zs, later, carries)]
    if causal is not None:
        a = [jnp.where(causal, x, 0.0) if b == 0 else x for x, (b, h) in zip(a, tiles)]
    out = tuple(carries[t] + jnp.sum(lgs[t], axis=1, keepdims=True) for t in (len(tiles) - 2, len(tiles) - 1))
    return [a[2 * b:2 * b + 2] for b in range(len(kblks))], out


def _sb_fwd(proj, comm=None):
    S = proj.shape[0]
    T = min(SB_T, S)
    nq = S // T

    def body(q_ref, k_ref, v_ref, o_ref, a_ref, kb_ref, vm_ref, acc_ref):
        i = pl.program_id(1)
        m0, m1 = _head_masks()

        @pl.when(i == 0)
        def _():
            v = v_ref[...]
            kb_ref[...] = _bf(k_ref[...])
            vm_ref[0] = _bf(v * m0)
            vm_ref[1] = _bf(v * m1)

        q = q_ref[...]
        qm = (_bf(q * (m0 * SB_SCALE)), _bf(q * (m1 * SB_SCALE)))
        upper = _tri(T, lambda r, c: r > c).astype(BF16)
        causal = _tri(T, lambda r, c: c < r)

        def tiles(js, carry, mask, first):
            ks = [pl.multiple_of(j * T, T) for j in js]
            a, out = _sb_weights(qm, [kb_ref[pl.ds(k, T), :] for k in ks], upper, carry, mask)
            a = [[_bf(t) for t in per_block] for per_block in a]
            for b, j in enumerate(js):
                for h in range(2):
                    a_ref[h, j] = a[b][h]
            parts = [_dg(a[b][h], vm_ref[h, pl.ds(k, T), :], 1, 0) for b, k in enumerate(ks) for h in range(2)]
            part = functools.reduce(lambda u, w: u + w, parts)
            if first:
                acc_ref[...] = part
            else:
                acc_ref[...] += part
            return out

        zero = jnp.zeros((T, 1), F32)
        carry = lax.cond(i == 0, lambda: tiles([i], (zero, zero), causal, True),
                         lambda: tiles([i, i - 1], (zero, zero), causal, True))
        n = jnp.maximum(i - 1, 0)
        carry = lax.fori_loop(0, n % 2, lambda _, c: tiles([n - 1], c, None, False), carry)
        top = n - 1 - n % 2
        carry = lax.fori_loop(0, (n // 2) % 2, lambda _, c: tiles([top, top - 1], c, None, False), carry)
        top = top - 2 * ((n // 2) % 2)
        lax.fori_loop(0, n // 4, lambda jj, c: tiles([top - 4 * jj - b for b in range(4)], c, None, False), carry)
        o_ref[...] = acc_ref[...]

    full = lambda c: pl.BlockSpec((S, 128), lambda p, i, c=c: (0, c + p))
    outs, landed = _pcall(
        body, name="sb_fwd", grid=(4, nq),
        out_shape=[jax.ShapeDtypeStruct((S, SB_W), F32), jax.ShapeDtypeStruct((4, 2, nq, nq, T, T), BF16)],
        in_specs=[pl.BlockSpec((T, 128), lambda p, i: (i, SB_Q_COL + p)), full(SB_K_COL), full(SB_V_COL)],
        out_specs=[pl.BlockSpec((T, 128), lambda p, i: (i, p)),
                   pl.BlockSpec((None, 2, None, nq, T, T), lambda p, i: (p, 0, i, 0, 0, 0))],
        scratch_shapes=[pltpu.VMEM((S, 128), BF16), pltpu.VMEM((2, S, 128), BF16), pltpu.VMEM((T, 128), F32)],
        sem=("arbitrary", "arbitrary"), args=(proj, proj, proj), comm=comm)
    return tuple(outs) if comm is None else (tuple(outs), landed)


def _sb_bwd(proj, a_saved, d_o, comm=None):
    S = proj.shape[0]
    T = min(SB_T, S)
    nq = S // T

    def body(q_ref, k_ref, v_ref, do_ref, a_ref, dq_ref, dk_ref, dv_ref, kb_ref, kbm_ref, vb_ref, dq_acc, dk_acc, dv_acc):
        i = pl.program_id(1)
        m0, m1 = _head_masks()

        @pl.when(i == 0)
        def _():
            k = k_ref[...]
            kb_ref[...] = _bf(k)
            kbm_ref[0] = _bf(k * m0)
            kbm_ref[1] = _bf(k * m1)
            vb_ref[...] = _bf(v_ref[...])
            dk_acc[...] = jnp.zeros_like(dk_acc)
            dv_acc[...] = jnp.zeros_like(dv_acc)

        q, d_out = q_ref[...], do_ref[...]
        qm = (_bf(q * (m0 * SB_SCALE)), _bf(q * (m1 * SB_SCALE)))
        dom = (_bf(d_out * m0), _bf(d_out * m1))
        qm_t = tuple(_bf((q * (m * SB_SCALE)).T) for m in (m0, m1))
        dom_t = tuple(_bf((d_out * m).T) for m in (m0, m1))
        lower = _tri(T, lambda r, c: r < c).astype(BF16)
        causal = _tri(T, lambda r, c: c < r)

        def up(js, carry, mask):
            ks = [pl.multiple_of(j * T, T) for j in js]
            tiles = [(b, h) for b in range(len(js)) for h in range(2)]
            zs = [_dg(qm[h], kb_ref[pl.ds(ks[b], T), :], 1, 1) for b, h in tiles]
            a = [a_ref[h, js[b]] for b, h in tiles]
            es = [w.astype(F32) * _dg(dom[h], vb_ref[pl.ds(ks[b], T), :], 1, 1) for w, (b, h) in zip(a, tiles)]
            carries = list(carry)
            for t in range(len(tiles)):
                carries.append(carries[t] + jnp.sum(es[t], axis=1, keepdims=True))
            d_lg = [_dg(_bf(e), lower, 1, 0) + c for e, c in zip(es, carries)]
            ens = [jnp.exp(-jnp.abs(z)) for z in zs]
            invs = [1.0 / (1.0 + en) for en in ens]
            betas = [jnp.where(z >= 0.0, inv, en * inv) for z, en, inv in zip(zs, ens, invs)]
            dzs = [e * (1.0 - b) - d * b for e, b, d in zip(es, betas, d_lg)]
            if mask is not None:
                dzs = [jnp.where(mask, dz, 0.0) if b == len(js) - 1 else dz for dz, (b, h) in zip(dzs, tiles)]
            dzs = [_bf(dz) for dz in dzs]
            parts = [_dg(dzs[t], kbm_ref[h, pl.ds(ks[b], T), :], 1, 0) for t, (b, h) in enumerate(tiles)]
            dq_acc[...] += functools.reduce(lambda u, w: u + w, parts)
            for b, j in enumerate(js):
                dk_acc[j] += _dg(qm_t[0], dzs[2 * b], 1, 0) + _dg(qm_t[1], dzs[2 * b + 1], 1, 0)
                dv_acc[j] += _dg(dom_t[0], a[2 * b], 1, 0) + _dg(dom_t[1], a[2 * b + 1], 1, 0)
            return tuple(carries[-2:])

        zero = jnp.zeros((T, 1), F32)
        dq_acc[...] = jnp.zeros_like(dq_acc)
        n = jnp.maximum(i - 1, 0)
        carry = lax.fori_loop(0, n // 4, lambda jj, c: up([4 * jj + b for b in range(4)], c, None), (zero, zero))
        done = 4 * (n // 4)
        carry = lax.fori_loop(0, (n // 2) % 2, lambda _, c: up([done, done + 1], c, None), carry)
        carry = lax.fori_loop(0, n % 2, lambda _, c: up([n - 1], c, None), carry)

        @pl.when(i == 0)
        def _():
            up([i], carry, causal)

        @pl.when(i > 0)
        def _():
            up([i - 1, i], carry, causal)

        dq_ref[...] = _bf(dq_acc[...] * SB_SCALE)

        @pl.when(i == nq - 1)
        def _():
            for j in range(nq):
                dk_ref[j * T:(j + 1) * T, :] = _bf(dk_acc[j].T)
                dv_ref[j * T:(j + 1) * T, :] = _bf(dv_acc[j].T)

    full = lambda c: pl.BlockSpec((S, 128), lambda p, i, c=c: (0, c + p))
    tile = pl.BlockSpec((T, 128), lambda p, i: (i, p))
    acc = pl.BlockSpec((S, 128), lambda p, i: (0, p))
    out = jax.ShapeDtypeStruct((S, SB_W), BF16)
    outs, landed = _pcall(
        body, name="sb_bwd", grid=(4, nq), out_shape=[out, out, out],
        in_specs=[pl.BlockSpec((T, 128), lambda p, i: (i, SB_Q_COL + p)), full(SB_K_COL), full(SB_V_COL), tile,
                  pl.BlockSpec((None, 2, None, nq, T, T), lambda p, i: (p, 0, i, 0, 0, 0))],
        out_specs=[tile, acc, acc],
        scratch_shapes=[pltpu.VMEM((S, 128), BF16), pltpu.VMEM((2, S, 128), BF16), pltpu.VMEM((S, 128), BF16),
                        pltpu.VMEM((T, 128), F32), pltpu.VMEM((nq, 128, T), F32), pltpu.VMEM((nq, 128, T), F32)],
        sem=("arbitrary", "arbitrary"), args=(proj, proj, proj, d_o, a_saved), comm=comm)
    return tuple(outs) if comm is None else (tuple(outs), landed)


SGU_U_COL, SGU_V_COL = 3584 // 512, 4096 // 512


def _causal(w):
    r = lax.broadcasted_iota(jnp.int32, (CHUNK, CHUNK), 0)
    c = lax.broadcasted_iota(jnp.int32, (CHUNK, CHUNK), 1)
    return jnp.where(r >= c, w, 0.0)


SGU_CHUNKS = 4


def _sgu_fwd(proj, ln_g, ln_b, w, b):
    S = proj.shape[0]
    R = min(SGU_CHUNKS, S // CHUNK)
    rows = R * CHUNK

    def body(u_ref, v_ref, g_ref, b_ref, w_ref, bias_ref, out_ref):
        wc = [_bf(_causal(w_ref[g])) for g in range(4)]
        for r in range(R):
            rs = slice(r * CHUNK, (r + 1) * CHUNK)
            u = _gelu(u_ref[rs, :].astype(F32))
            xh, _ = _norm_stats(_gelu(v_ref[rs, :].astype(F32)))
            vn = _bf(xh * g_ref[...] + b_ref[...])
            for g in range(4):
                sl = slice(g * 128, (g + 1) * 128)
                out_ref[rs, sl] = u[:, sl] * (_dg(wc[g], vn[:, sl], 1, 0) + bias_ref[g])

    vec = pl.BlockSpec((1, 512), lambda n: (0, 0))
    return pl.pallas_call(
        body, name="sgu_fwd", grid=(S // rows,),
        out_shape=jax.ShapeDtypeStruct((S, SGU_W), F32),
        in_specs=[pl.BlockSpec((rows, 512), lambda n: (n, SGU_U_COL)),
                  pl.BlockSpec((rows, 512), lambda n: (n, SGU_V_COL)), vec, vec,
                  pl.BlockSpec((4, CHUNK, CHUNK), lambda n: (0, 0, 0)), pl.BlockSpec((4, CHUNK, 1), lambda n: (0, 0, 0))],
        out_specs=pl.BlockSpec((rows, 512), lambda n: (n, 0)),
        compiler_params=_cparams(("parallel",)),
    )(proj, proj, ln_g, ln_b, w, b)


def _sgu_bwd(proj, ln_g, ln_b, w, b, d_out):
    S = proj.shape[0]
    R = min(SGU_CHUNKS, S // CHUNK)
    rows = R * CHUNK

    def body(u_ref, v_ref, g_ref, b_ref, w_ref, bias_ref, do_ref, dp_ref, dg_ref, db_ref, dw_ref, dbias_ref):
        @pl.when(pl.program_id(0) == 0)
        def _():
            dg_ref[...] = jnp.zeros_like(dg_ref)
            db_ref[...] = jnp.zeros_like(db_ref)
            dw_ref[...] = jnp.zeros_like(dw_ref)
            dbias_ref[...] = jnp.zeros_like(dbias_ref)

        ln_gain = g_ref[...]
        wc = [_bf(_causal(w_ref[g])) for g in range(4)]
        for r in range(R):
            rs = slice(r * CHUNK, (r + 1) * CHUNK)
            gu, gv = u_ref[rs, :].astype(F32), v_ref[rs, :].astype(F32)
            u = _gelu(gu)
            xh, rstd = _norm_stats(_gelu(gv))
            vn = _bf(xh * ln_gain + b_ref[...])
            d_o = do_ref[rs, :]
            d_vn = []
            for g in range(4):
                sl = slice(g * 128, (g + 1) * 128)
                sv = _dg(wc[g], vn[:, sl], 1, 0) + bias_ref[g]
                dp_ref[rs, sl] = _bf(d_o[:, sl] * sv * _gelu_grad(gu[:, sl]))
                d_sv = d_o[:, sl] * u[:, sl]
                dbias_ref[g] += jnp.sum(d_sv, axis=1, keepdims=True)
                d_svb = _bf(d_sv)
                dw_ref[g] += _causal(_dg(d_svb, vn[:, sl], 1, 1))
                d_vn.append(_dg(wc[g], d_svb, 0, 0))
            d_vn = jnp.concatenate(d_vn, axis=1)
            dg_ref[...] += jnp.sum(d_vn * xh, axis=0, keepdims=True)
            db_ref[...] += jnp.sum(d_vn, axis=0, keepdims=True)
            dp_ref[rs, 512:1024] = _bf(_norm_bwd(d_vn * ln_gain, xh, rstd) * _gelu_grad(gv))

    vec = pl.BlockSpec((1, 512), lambda n: (0, 0))
    wspec = pl.BlockSpec((4, CHUNK, CHUNK), lambda n: (0, 0, 0))
    bspec = pl.BlockSpec((4, CHUNK, 1), lambda n: (0, 0, 0))
    return pl.pallas_call(
        body, name="sgu_bwd", grid=(S // rows,),
        out_shape=(jax.ShapeDtypeStruct((S, 1024), BF16), jax.ShapeDtypeStruct((1, 512), F32),
                   jax.ShapeDtypeStruct((1, 512), F32), jax.ShapeDtypeStruct((4, CHUNK, CHUNK), F32),
                   jax.ShapeDtypeStruct((4, CHUNK, 1), F32)),
        in_specs=[pl.BlockSpec((rows, 512), lambda n: (n, SGU_U_COL)),
                  pl.BlockSpec((rows, 512), lambda n: (n, SGU_V_COL)), vec, vec, wspec, bspec,
                  pl.BlockSpec((rows, 512), lambda n: (n, 0))],
        out_specs=(pl.BlockSpec((rows, 1024), lambda n: (n, 0)), vec, vec, wspec, bspec),
        compiler_params=_cparams(("arbitrary",)),
    )(proj, proj, ln_g, ln_b, w, b, d_out)


GATE_COL = 4608 // 512


def _merge_fwd(proj, branches, p_list, tm=512):
    S = proj.shape[0]
    tm = min(tm, S)

    def body(r_ref, s_ref, g_ref, pr_ref, ps_ref, pg_ref, gr_ref, gs_ref, gg_ref, m_ref, br_ref):
        acc = None
        for k, (x_ref, p_ref, gate_ref) in enumerate(((r_ref, pr_ref, gr_ref), (s_ref, ps_ref, gs_ref),
                                                      (g_ref, pg_ref, gg_ref))):
            br = _dg(_bf(x_ref[...]), _bf(p_ref[...]), 1, 0)
            br_ref[k] = _bf(br)
            term = _sigmoid(gate_ref[...].astype(F32)) * br
            acc = term if acc is None else acc + term
        m_ref[...] = _bf(acc)

    xs = pl.BlockSpec((tm, 512), lambda i, n: (i, 0))
    ps = pl.BlockSpec((512, 512), lambda i, n: (0, n))
    gate = lambda k: pl.BlockSpec((tm, 512), lambda i, n, k=k: (i, GATE_COL + 2 * k + n))
    return pl.pallas_call(
        body, name="merge_fwd", grid=(S // tm, 2),
        out_shape=(jax.ShapeDtypeStruct((S, D_MODEL), BF16), jax.ShapeDtypeStruct((3, S, D_MODEL), BF16)),
        in_specs=[xs, xs, xs, ps, ps, ps, gate(0), gate(1), gate(2)],
        out_specs=(pl.BlockSpec((tm, 512), lambda i, n: (i, n)), pl.BlockSpec((3, tm, 512), lambda i, n: (0, i, n))),
        compiler_params=_cparams(("parallel", "parallel")),
    )(*branches, *p_list, proj, proj, proj)


def _gate_bwd(proj, br, d_merged, tm=512):
    S = proj.shape[0]
    tm = min(tm, S)

    def body(dm_ref, br_ref, gr_ref, gs_ref, gg_ref, *out_refs):
        dm = dm_ref[...]
        for k, gate_ref in enumerate((gr_ref, gs_ref, gg_ref)):
            s = _sigmoid(gate_ref[...].astype(F32))
            out_refs[k][...] = _bf(dm * s)
            out_refs[3 + k][...] = _bf(dm * br_ref[k].astype(F32) * (s * (1.0 - s)))

    gate = lambda k: pl.BlockSpec((tm, 512), lambda i, n, k=k: (i, GATE_COL + 2 * k + n))
    three = pl.BlockSpec((3, tm, 512), lambda i, n: (0, i, n))
    tile = pl.BlockSpec((tm, 512), lambda i, n: (i, n))
    outs = pl.pallas_call(
        body, name="gate_bwd", grid=(S // tm, 2),
        out_shape=[jax.ShapeDtypeStruct((S, D_MODEL), BF16)] * 6,
        in_specs=[tile, three, gate(0), gate(1), gate(2)], out_specs=[tile] * 6,
        compiler_params=_cparams(("parallel", "parallel")),
    )(d_merged, br, proj, proj, proj)
    return outs[:3], outs[3:]


def _ln_bwd(dy, u, g, target=None, tm=256):
    S, D = u.shape
    tm = min(tm, S)
    loss = target is not None

    def body(*refs):
        dy_ref, u_ref, g_ref = refs[:3]
        du_ref, dub_ref, dg_ref, db_ref = refs[3 + loss:7 + loss]

        @pl.when(pl.program_id(0) == 0)
        def _():
            for acc_ref in refs[5 + loss:]:
                acc_ref[...] = jnp.zeros_like(acc_ref)

        dy_t = dy_ref[...]
        if loss:
            err = dy_t - refs[3][...]
            refs[-1][...] += jnp.sum(err * err, axis=0, keepdims=True)
            dy_t = err * (1.0 / D)
        xh, rstd = _norm_stats(u_ref[...])
        dg_ref[...] += jnp.sum(dy_t * xh, axis=0, keepdims=True)
        db_ref[...] += jnp.sum(dy_t, axis=0, keepdims=True)
        du = _norm_bwd(dy_t * g_ref[...], xh, rstd)
        du_ref[...] = du
        dub_ref[...] = _bf(du)

    tile = pl.BlockSpec((tm, D), lambda i: (i, 0))
    vec = pl.BlockSpec((1, D), lambda i: (0, 0))
    row = jax.ShapeDtypeStruct((1, D), F32)
    return pl.pallas_call(
        body, name="ln_bwd", grid=(S // tm,),
        out_shape=[jax.ShapeDtypeStruct((S, D), F32), jax.ShapeDtypeStruct((S, D), BF16)] + [row] * (2 + loss),
        in_specs=[tile, tile, vec] + [tile] * loss, out_specs=[tile, tile] + [vec] * (2 + loss),
        compiler_params=_cparams(("arbitrary",)),
    )(dy, u, g, *([target] if loss else []))


def _layer_fwd(x, x_bf, W, tables, sb_comm=None):
    proj = _matmul(x_bf, W["w_in_t"], "nt", name="proj", tm=1024, tn=768, tk=1024)
    retg, states = _ret_fwd(proj, tables, W["ret_gn_g"], W["ret_gn_b"])
    if sb_comm is None:
        sb, sb_a = _sb_fwd(proj)
    else:
        (sb, sb_a), landed = _sb_fwd(proj, comm=sb_comm[0])
        sb_comm[1](landed)
    sg = _sgu_fwd(proj, W["sgu_ln_g"], W["sgu_ln_b"], W["sgu_w"], W["sgu_b"])
    merged, br = _merge_fwd(proj, (retg, sb, sg), (W["p_ret"], W["p_sb"], W["p_sgu"]))
    u1, x1, x1_bf = _matmul(merged, W["w_out"], "nn", name="out_ln", tm=512, tn=1024, tk=1024, epi="ln",
                            extra=(x, W["ln1_g"], W["ln1_b"]))
    act = _matmul(x1_bf, W["w_up"], "nn", name="up", tm=1024, tn=1024, tk=1024, epi="relu2")
    u2, x2, x2_bf = _matmul(act, W["w_down"], "nn", name="down_ln", tm=512, tn=1024, tk=4096, epi="ln",
                            extra=(x1, W["ln2_g"], W["ln2_b"]))
    saved = dict(x_bf=x_bf, proj=proj, retg=retg, states=states, sb=sb, sb_a=sb_a, sg=sg, merged=merged, br=br, u1=u1,
                 x1_bf=x1_bf, act=act, u2=u2)
    return x2, x2_bf, saved


def _layer_bwd(d_x2, W, tables, sv, chunk_dtype=None, sb_comm_fn=None, dwin_comm_fn=None, dx_comm_fn=None, target=None):
    dt = F32 if chunk_dtype is None else chunk_dtype
    rows, cols = (None, None) if chunk_dtype is None else ("rows", "cols")
    g, landed = {}, {}
    du2, du2_bf, g["ln2_g"], g["ln2_b"], *sq = _ln_bwd(d_x2, sv["u2"], W["ln2_g"], target=target)
    if sq:
        landed["sq"] = sq[0]
    d_hpre = _matmul(du2_bf, W["w_down"], "nt", name="d_act", tm=1024, tn=1024, tk=1024, epi="drelu2",
                     extra=(sv["act"],), out_dtype=BF16)
    g["w_down"] = _matmul(sv["act"], du2_bf, "tn", name="dw_down", tm=512, tn=1024, tk=4096, out_dtype=dt, chunks=rows)
    g["w_up"] = _matmul(sv["x1_bf"], d_hpre, "tn", name="dw_up", tm=1024, tn=512, tk=4096, out_dtype=dt, chunks=cols)
    d_x1 = _matmul(d_hpre, W["w_up"], "nt", name="d_x1", tm=512, tn=1024, tk=4096, epi="add", extra=(du2,))
    du1, du1_bf, g["ln1_g"], g["ln1_b"] = _ln_bwd(d_x1, sv["u1"], W["ln1_g"])
    d_merged = _matmul(du1_bf, W["w_out"], "nt", name="d_merged", tm=1024, tn=1024, tk=1024)
    g["w_out"] = _matmul(sv["merged"], du1_bf, "tn", name="dw_out", tm=1024, tn=512, tk=4096, out_dtype=dt, chunks=rows)
    d_br, d_gate = _gate_bwd(sv["proj"], sv["br"], d_merged)
    d_branch = []
    for k, (nm, act) in enumerate((("p_ret", sv["retg"]), ("p_sb", sv["sb"]), ("p_sgu", sv["sg"]))):
        d_branch.append(_matmul(d_br[k], W[nm], "nt", name="d_" + nm[2:], tm=1024, tn=512, tk=1024))
        g[nm] = _matmul(act, d_br[k], "tn", name="dw_" + nm[2:], tm=512, tn=1024, tk=2048, out_dtype=dt, chunks=cols)
    d_ret, g["ret_gn_g"], g["ret_gn_b"] = _ret_bwd(sv["proj"], tables, W["ret_gn_g"], W["ret_gn_b"], sv["states"],
                                                   d_branch[0])
    if sb_comm_fn is None:
        d_sq, d_sk, d_sv = _sb_bwd(sv["proj"], sv["sb_a"], d_branch[1])
    else:
        (d_sq, d_sk, d_sv), landed["sb"] = _sb_bwd(sv["proj"], sv["sb_a"], d_branch[1], comm=sb_comm_fn(g))
    d_sgu, g["sgu_ln_g"], g["sgu_ln_b"], g["sgu_w"], g["sgu_b"] = _sgu_bwd(
        sv["proj"], W["sgu_ln_g"], W["sgu_ln_b"], W["sgu_w"], W["sgu_b"], d_branch[2])
    d_proj = [d_ret, d_sq, d_sk, d_sv, d_sgu, d_gate[0], d_gate[1], d_gate[2]]
    g["w_in"] = _matmul(d_proj, sv["x_bf"], "tn", name="dw_in", tm=256, tn=1024, tk=4096, out_dtype=dt, chunks=rows,
                        comm=None if dwin_comm_fn is None else dwin_comm_fn(g))
    if dwin_comm_fn is not None:
        g["w_in"], landed["dwin"] = g["w_in"]
    if chunk_dtype is None:
        g["w_in"] = g["w_in"].T
    d_x = _matmul_rows_of(d_proj, W["w_in_t"], du1, name="d_x", tm=512,
                          comm=None if dx_comm_fn is None else dx_comm_fn(g))
    if dx_comm_fn is not None:
        d_x, landed["dx"] = d_x
    return d_x, g, landed


BIG = ("w_in", "p_ret", "p_sb", "p_sgu", "w_out", "w_up", "w_down")
SMALL = ("ret_gn_g", "ret_gn_b", "sgu_ln_g", "sgu_ln_b", "sgu_w", "sgu_b", "ln1_g", "ln1_b", "ln2_g", "ln2_b")
GATHER_KIND = {"w_in": "rows", "p_ret": "cols", "p_sb": "cols", "p_sgu": "cols", "w_out": "rows", "w_up": "cols",
               "w_down": "rows"}


def _small_weights(small, l):
    W = {}
    for n in SMALL:
        if n == "sgu_w":
            W[n] = small[n][l]
        elif n == "sgu_b":
            W[n] = small[n][l].reshape(4, CHUNK, 1)
        else:
            W[n] = small[n][l].reshape(1, -1)
    return W


def _local_step(x, target, full, small):
    tables = _ret_tables(x.shape[0])
    Ws = [{**{n: full[n][l] for n in BIG[1:]}, "w_in_t": full["w_in"][l].T, **_small_weights(small, l)}
          for l in range(DEPTH)]
    saved = []
    h, h_bf = x, _bf(x)
    for l in range(DEPTH):
        h, h_bf, sv = _layer_fwd(h, h_bf, Ws[l], tables)
        saved.append(sv)
    grads = [None] * DEPTH
    d_h, grads[-1], landed = _layer_bwd(h, Ws[-1], tables, saved[-1], target=target)
    for l in reversed(range(DEPTH - 1)):
        d_h, grads[l], _ = _layer_bwd(d_h, Ws[l], tables, saved[l])
    return landed["sq"], d_h, grads


def _adam(w, parts, m, v, name):
    L, R, C = w.shape
    tr = next(t for t in (320, 256, 128) if R % t == 0)
    assert len(parts) == L

    def body(*refs):
        w_ref, p_refs, (m_ref, v_ref, g_ref, d_ref, nm_ref, nv_ref) = refs[0], refs[1:1 + L], refs[1 + L:]
        layer = pl.program_id(0)
        g = None
        for li, p_ref in enumerate(p_refs):
            s = p_ref[0].astype(F32)
            for j in range(1, p_ref.shape[0]):
                s = s + p_ref[j].astype(F32)
            g = s if g is None else jnp.where(layer == li, s, g)
        g_ref[...] = g
        d_ref[...], nm_ref[...], nv_ref[...] = _adam_update(w_ref[...], g, m_ref[...], v_ref[...])

    tile = pl.BlockSpec((None, tr, C), lambda l, i: (l, i, 0))
    part = lambda li: pl.BlockSpec((parts[li].shape[0], tr, C), lambda l, i, li=li: (0, jnp.where(l == li, i, 0), 0))
    out = jax.ShapeDtypeStruct((L, R, C), F32)
    return pl.pallas_call(
        body, name=name, grid=(L, R // tr), out_shape=(out, out, out, out),
        in_specs=[tile] + [part(li) for li in range(L)] + [tile, tile],
        out_specs=(tile, tile, tile, tile),
        compiler_params=_cparams(("parallel", "parallel")),
    )(w, *parts, m, v)


def _adam_update(w, g, m, v):
    m2 = ADAM_B1 * m + (1.0 - ADAM_B1) * g
    v2 = ADAM_B2 * v + (1.0 - ADAM_B2) * (g * g)
    m_hat = m2 / (1.0 - ADAM_B1 ** ADAM_STEP)
    v_hat = v2 / (1.0 - ADAM_B2 ** ADAM_STEP)
    return -ADAM_LR * (m_hat / (jnp.sqrt(v_hat) + ADAM_EPS) + ADAM_WD * w), m2, v2


def _adam_small(w, m, v, parts):
    k = len(SMALL)

    def body(*refs):
        w_refs, m_refs, v_refs, p_refs, outs = refs[:k], refs[k:2 * k], refs[2 * k:3 * k], refs[3 * k:5 * k], refs[5 * k:]
        for i in range(k):
            vector = len(w_refs[i].shape) == 2
            for l in range(DEPTH):
                p_ref = p_refs[DEPTH * i + l]
                g = p_ref[0]
                for j in range(1, N_DEV):
                    g = g + p_ref[j]
                at = (slice(l, l + 1), slice(None)) if vector else (l,)
                delta, m2, v2 = _adam_update(w_refs[i][at], g, m_refs[i][at], v_refs[i][at])
                for o_ref, val in zip(outs[4 * i:4 * i + 4], (g, delta, m2, v2)):
                    o_ref[at] = val

    vmem = pl.BlockSpec(memory_space=pltpu.VMEM)
    args = [w[n] for n in SMALL] + [m[n] for n in SMALL] + [v[n] for n in SMALL] + \
           [parts[(n, l)] for n in SMALL for l in range(DEPTH)]
    out_shape = [jax.ShapeDtypeStruct(w[n].shape, F32) for n in SMALL for _ in range(4)]
    outs = pl.pallas_call(body, name="adam_small", out_shape=out_shape, in_specs=[vmem] * len(args),
                          out_specs=[vmem] * len(out_shape), compiler_params=_cparams())(*args)
    return {n: tuple(outs[4 * i:4 * i + 4]) for i, n in enumerate(SMALL)}


WEIGHTS = ("w_in", "ret_gn_g", "ret_gn_b", "sgu_ln_g", "sgu_ln_b", "sgu_w", "sgu_b", "p_ret", "p_sb", "p_sgu", "w_out",
           "ln1_g", "ln1_b", "w_up", "w_down", "ln2_g", "ln2_b")


def kernel(x, w_in, ret_gn_g, ret_gn_b, sgu_ln_g, sgu_ln_b, sgu_w, sgu_b, p_ret, p_sb, p_sgu, w_out, ln1_g, ln1_b, w_up, w_down, ln2_g, ln2_b, loss_target, m_w_in, m_ret_gn_g, m_ret_gn_b, m_sgu_ln_g, m_sgu_ln_b, m_sgu_w, m_sgu_b, m_p_ret, m_p_sb, m_p_sgu, m_w_out, m_ln1_g, m_ln1_b, m_w_up, m_w_down, m_ln2_g, m_ln2_b, v_w_in, v_ret_gn_g, v_ret_gn_b, v_sgu_ln_g, v_sgu_ln_b, v_sgu_w, v_sgu_b, v_p_ret, v_p_sb, v_p_sgu, v_w_out, v_ln1_g, v_ln1_b, v_w_up, v_w_down, v_ln2_g, v_ln2_b):
    w = dict(zip(WEIGHTS, (w_in, ret_gn_g, ret_gn_b, sgu_ln_g, sgu_ln_b, sgu_w, sgu_b, p_ret, p_sb, p_sgu, w_out,
                           ln1_g, ln1_b, w_up, w_down, ln2_g, ln2_b)))
    m = dict(zip(WEIGHTS, (m_w_in, m_ret_gn_g, m_ret_gn_b, m_sgu_ln_g, m_sgu_ln_b, m_sgu_w, m_sgu_b, m_p_ret, m_p_sb,
                           m_p_sgu, m_w_out, m_ln1_g, m_ln1_b, m_w_up, m_w_down, m_ln2_g, m_ln2_b)))
    v = dict(zip(WEIGHTS, (v_w_in, v_ret_gn_g, v_ret_gn_b, v_sgu_ln_g, v_sgu_ln_b, v_sgu_w, v_sgu_b, v_p_ret, v_p_sb,
                           v_p_sgu, v_w_out, v_ln1_g, v_ln1_b, v_w_up, v_w_down, v_ln2_g, v_ln2_b)))

    small = {n: w[n] for n in SMALL}
    shard = {n: _bf(w[n]) for n in BIG}
    shard["w_in"] = shard["w_in"].transpose(0, 2, 1)
    S = x.shape[1]
    x0, target = x.reshape(S, D_MODEL), loss_target.reshape(S, D_MODEL)
    tables = _ret_tables(S)
    Ws = [_small_weights(small, l) for l in range(DEPTH)]

    (Ws[0]["w_in_t"],) = _exchange([_gather_transfer(shard["w_in"], 0, "rows")], "gather_w_in0", relay=True)
    def gather_under_sb(keys):
        def landed_fn(landed):
            for (n, l), z in zip(keys, landed):
                Ws[l]["w_in_t" if n == "w_in" else n] = z

        return _Comm([_gather_transfer(shard[n], l, GATHER_KIND[n]) for n, l in keys], relay=True), landed_fn

    h, h_bf, saved0 = _layer_fwd(x0, _bf(x0), Ws[0], tables,
                                 sb_comm=gather_under_sb([(n, 0) for n in BIG[1:]] + [("w_in", 1)]))
    h, _, saved1 = _layer_fwd(h, h_bf, Ws[1], tables, sb_comm=gather_under_sb([(n, 1) for n in BIG[1:]]))
    d_h, g1, landed1 = _layer_bwd(h, Ws[1], tables, saved1, chunk_dtype=BF16, target=target,
                                  sb_comm_fn=lambda g: _Comm([_scatter_transfer(g[n]) for n in BIG[1:]]))
    loss = lax.psum(0.5 * jnp.sum(landed1["sq"]) / D_MODEL, ("x", "y", "c"))
    early = [("w_in", 1)] + [(n, 0) for n in BIG[1:]]

    def small_slabs(g):
        return [_slab_transfer(g[n].reshape(4, CHUNK) if n == "sgu_b" else g[n]) for n in SMALL]

    def early_scatter(g0):
        return _Comm([_scatter_transfer((g1 if l else g0)[n]) for n, l in early] + small_slabs(g1))

    def late_scatter(g0):
        pairs = _pair_reduce(g0["w_in"], "w_in0_pairs")
        return _Comm([_chip_scatter_transfer(pairs)])

    d_x, g0, landed = _layer_bwd(d_h, Ws[0], tables, saved0, chunk_dtype=BF16, sb_comm_fn=early_scatter,
                                 dwin_comm_fn=lambda g: _Comm(small_slabs(g)), dx_comm_fn=late_scatter)
    parts = {**dict(zip(early, landed["sb"])), **{(n, 1): z for n, z in zip(BIG[1:], landed1["sb"])}}
    parts[("w_in", 0)] = landed["dx"][0]
    small_parts = {**{(n, 1): z for n, z in zip(SMALL, landed["sb"][len(early):])},
                   **{(n, 0): z for n, z in zip(SMALL, landed["dwin"])}}

    grad, delta, new_m, new_v = {}, {}, {}, {}
    for n in BIG:
        view = (lambda a: a.transpose(0, 2, 1)) if n == "w_in" else (lambda a: a)
        res = _adam(view(w[n]), [parts[(n, l)] for l in range(DEPTH)], view(m[n]), view(v[n]), "adam_" + n)
        grad[n], delta[n], new_m[n], new_v[n] = (view(r) for r in res)
    for n, res in _adam_small(small, m, v, small_parts).items():
        grad[n], delta[n], new_m[n], new_v[n] = res

    return (loss, d_x.reshape(x.shape), *[grad[n] for n in WEIGHTS], *[delta[n] for n in WEIGHTS],
            *[new_m[n] for n in WEIGHTS], *[new_v[n] for n in WEIGHTS])
```

```python
import functools
import math

import numpy as np
import jax
import jax.numpy as jnp
from jax import lax
from jax.experimental import pallas as pl
from jax.experimental.pallas import tpu as pltpu

F32 = jnp.float32
BF16 = jnp.bfloat16

N_DEV = 8
DEPTH = 2
D_MODEL = 1024
CHUNK = 128
RET_W = 512
SB_W = 512
SGU_W = 512
N_IN = 7680
LN_EPS = 1e-5
ALPHA = (2 * DEPTH) ** 0.25
ROPE_BASE = 10000.0
ADAM_LR, ADAM_B1, ADAM_B2, ADAM_EPS, ADAM_WD, ADAM_STEP = 0.001, 0.9, 0.999, 1e-08, 0.01, 10
VMEM_LIMIT = 56 * 1024 * 1024

_GELU_K = math.sqrt(2.0 / math.pi)
_GELU_C = 0.044715


def _cparams(sem=None):
    return pltpu.CompilerParams(dimension_semantics=sem, vmem_limit_bytes=VMEM_LIMIT)


def _dg(a, b, ca, cb):
    return lax.dot_general(a, b, (((ca,), (cb,)), ((), ())), preferred_element_type=F32)


def _bf(x):
    return x.astype(BF16)


def _sigmoid(x):
    return 1.0 / (1.0 + jnp.exp(-x))


def _gelu(x):
    t = jnp.tanh(_GELU_K * (x + _GELU_C * (x * x * x)))
    return x * (0.5 * (1.0 + t))


def _gelu_grad(x):
    t = jnp.tanh(_GELU_K * (x + _GELU_C * (x * x * x)))
    return 0.5 * (1.0 + t) + 0.5 * x * (1.0 - t * t) * (_GELU_K * (1.0 + 3.0 * _GELU_C * x * x))


def _norm_stats(u):
    mu = jnp.mean(u, axis=-1, keepdims=True)
    d = u - mu
    var = jnp.mean(d * d, axis=-1, keepdims=True)
    rstd = lax.rsqrt(var + LN_EPS)
    return d * rstd, rstd


def _norm_bwd(dxh, xh, rstd):
    return rstd * (dxh - jnp.mean(dxh, axis=-1, keepdims=True) - xh * jnp.mean(dxh * xh, axis=-1, keepdims=True))


class _Transfer:
    def __init__(self, src, dst_shape, src_at, dst_at, same_core=False):
        self.src, self.dst_shape, self.src_at, self.dst_at = src, tuple(dst_shape), src_at, dst_at
        self.same_core = same_core


def _gather_transfer(shard, l, kind):
    _, r, c = shard.shape
    src_at = lambda ref, p: ref.at[l]
    if kind == "slab":
        return _Transfer(shard, (N_DEV, r, c), src_at, lambda ref, s: ref.at[s])
    if kind == "rows":
        return _Transfer(shard, (N_DEV * r, c), src_at, lambda ref, s: ref.at[pl.ds(pl.multiple_of(s * r, r), r), :])
    return _Transfer(shard, (r, N_DEV * c), src_at, lambda ref, s: ref.at[:, pl.ds(pl.multiple_of(s * c, c), c)])


def _scatter_transfer(chunks):
    return _Transfer(chunks, chunks.shape, lambda ref, p: ref.at[p], lambda ref, s: ref.at[s])


def _slab_transfer(arr):
    return _Transfer(arr, (N_DEV,) + arr.shape, lambda ref, p: ref, lambda ref, s: ref.at[s])


class _Comm:
    def __init__(self, transfers, relay=False):
        self.transfers = list(transfers)
        self.relay = relay
        self.n = len(self.transfers)
        self.arrays = [t.src for t in self.transfers]
        self.out_shape = [jax.ShapeDtypeStruct(t.dst_shape, t.src.dtype) for t in self.transfers]
        self.scratch = [pltpu.SemaphoreType.DMA((self.n * (N_DEV - 1),)), pltpu.SemaphoreType.DMA((self.n * (N_DEV - 1),)),
                        pltpu.SemaphoreType.DMA((self.n,))]

    def _relay_copies(self, srcs, dsts, send_sems, recv_sems, local_sems):
        x, y, c = lax.axis_index("x"), lax.axis_index("y"), lax.axis_index("c")
        me = 4 * x + 2 * y + c
        chips = [(1 - x, y), (x, 1 - y), (1 - x, 1 - y)]
        first, passed, own = [], [], []
        for t, tr in enumerate(self.transfers):
            def copy(k, src, sender, to, t=t, tr=tr):
                return pltpu.make_async_remote_copy(
                    src_ref=src, dst_ref=tr.dst_at(dsts[t], sender), send_sem=send_sems.at[t * (N_DEV - 1) + k],
                    recv_sem=recv_sems.at[t * (N_DEV - 1) + k], device_id=to, device_id_type=pl.DeviceIdType.MESH)

            mine = tr.src_at(srcs[t], me)
            first.append([copy(0, mine, me, (x, y, 1 - c))] + [copy(1 + j, mine, me, (px, py, c))
                                                                for j, (px, py) in enumerate(chips)])
            passed.append([copy(4 + j, tr.dst_at(dsts[t], 4 * px + 2 * py + c), 4 * px + 2 * py + c, (x, y, 1 - c))
                           for j, (px, py) in enumerate(chips)])
            own.append(pltpu.make_async_copy(mine, tr.dst_at(dsts[t], me), local_sems.at[t]))
        return first, passed, own

    def _copies(self, srcs, dsts, send_sems, recv_sems, local_sems):
        x, y, c = lax.axis_index("x"), lax.axis_index("y"), lax.axis_index("c")
        me = 4 * x + 2 * y + c
        copies = []
        for d in range(1, N_DEV):
            px = 1 - x if d & 4 else x
            py = 1 - y if d & 2 else y
            pc = 1 - c if d & 1 else c
            for t, tr in enumerate(self.transfers):
                if tr.same_core and d & 1:
                    continue
                peer, mine = (2 * px + py, 2 * x + y) if tr.same_core else (4 * px + 2 * py + pc, me)
                k = t * (N_DEV - 1) + d - 1
                copies.append(pltpu.make_async_remote_copy(
                    src_ref=tr.src_at(srcs[t], peer), dst_ref=tr.dst_at(dsts[t], mine),
                    send_sem=send_sems.at[k], recv_sem=recv_sems.at[k],
                    device_id=(px, py, pc), device_id_type=pl.DeviceIdType.MESH))
        own = []
        for t, tr in enumerate(self.transfers):
            mine = 2 * x + y if tr.same_core else me
            own.append(pltpu.make_async_copy(tr.src_at(srcs[t], mine), tr.dst_at(dsts[t], mine), local_sems.at[t]))
        return copies, own

    def start(self, srcs, dsts, *sems):
        if self.relay:
            first, _, own = self._relay_copies(srcs, dsts, *sems)
            for cp in own + [cp for per_t in first for cp in per_t]:
                cp.start()
            return
        copies, own = self._copies(srcs, dsts, *sems)
        for cp in own + copies:
            cp.start()

    def pass_on(self, srcs, dsts, *sems):
        if self.relay:
            first, passed, _ = self._relay_copies(srcs, dsts, *sems)
            for j in range(3):
                for t in range(self.n):
                    first[t][1 + j].wait_recv()
                    passed[t][j].start()

    def finish(self, srcs, dsts, *sems):
        if self.relay:
            first, passed, own = self._relay_copies(srcs, dsts, *sems)
            for t in range(self.n):
                first[t][0].wait_recv()
                for cp in passed[t]:
                    cp.wait_recv()
            for t in range(self.n):
                for cp in first[t] + passed[t]:
                    cp.wait_send()
                own[t].wait()
            return
        copies, own = self._copies(srcs, dsts, *sems)
        for cp in copies + own:
            cp.wait()


def _pcall(body, *, name, grid, in_specs, out_specs, out_shape, scratch_shapes, sem, args, comm=None):
    in_specs, out_specs, out_shape = list(in_specs), list(out_specs), list(out_shape)
    if comm is None:
        outs = pl.pallas_call(body, name=name, grid=grid, in_specs=in_specs, out_specs=out_specs, out_shape=out_shape,
                              scratch_shapes=list(scratch_shapes), compiler_params=_cparams(sem))(*args)
        return list(outs), []
    n_in, n_out, n_scr, k = len(in_specs), len(out_specs), len(scratch_shapes), comm.n

    def carrier(*refs):
        ins, cin = refs[:n_in], refs[n_in:n_in + k]
        outs, cout = refs[n_in + k:n_in + k + n_out], refs[n_in + k + n_out:n_in + 2 * k + n_out]
        scr, sems = refs[n_in + 2 * k + n_out:n_in + 2 * k + n_out + n_scr], refs[n_in + 2 * k + n_out + n_scr:]
        ids = [pl.program_id(d) for d in range(len(grid))]
        first = functools.reduce(jnp.logical_and, [i == 0 for i in ids])
        last = functools.reduce(jnp.logical_and, [i == g - 1 for i, g in zip(ids, grid)])

        @pl.when(first)
        def _():
            comm.start(cin, cout, *sems)

        body(*ins, *outs, *scr)

        early_pass = comm.relay and len(grid) > 1 and grid[0] > 1
        if early_pass:
            @pl.when(functools.reduce(jnp.logical_and, [ids[0] == grid[0] - 1] + [i == 0 for i in ids[1:]]))
            def _():
                comm.pass_on(cin, cout, *sems)

        @pl.when(last)
        def _():
            if not early_pass:
                comm.pass_on(cin, cout, *sems)
            comm.finish(cin, cout, *sems)

    hbm = pl.BlockSpec(memory_space=pl.ANY)
    outs = pl.pallas_call(
        carrier, name=name, grid=grid, in_specs=in_specs + [hbm] * k, out_specs=out_specs + [hbm] * k,
        out_shape=out_shape + comm.out_shape, scratch_shapes=list(scratch_shapes) + comm.scratch,
        compiler_params=_cparams(tuple("arbitrary" for _ in grid)),
    )(*args, *comm.arrays)
    return list(outs[:n_out]), list(outs[n_out:])


def _exchange(transfers, name, relay=False):
    comm = _Comm(transfers, relay)

    def body(*refs):
        k = comm.n
        comm.start(refs[:k], refs[k:2 * k], *refs[2 * k:])
        comm.pass_on(refs[:k], refs[k:2 * k], *refs[2 * k:])
        comm.finish(refs[:k], refs[k:2 * k], *refs[2 * k:])

    hbm = pl.BlockSpec(memory_space=pl.ANY)
    return pl.pallas_call(body, name=name, out_shape=comm.out_shape, in_specs=[hbm] * comm.n, out_specs=[hbm] * comm.n,
                          scratch_shapes=comm.scratch)(*comm.arrays)


def _pair_reduce(chunks, name, tr=320):
    _, r, c = chunks.shape
    tr = min(tr, r)
    assert r % tr == 0

    def swap(src_ref, dst_ref, send_sems, recv_sems):
        x, y, core = lax.axis_index("x"), lax.axis_index("y"), lax.axis_index("c")
        copies = [pltpu.make_async_remote_copy(
            src_ref=src_ref.at[2 * k + 1 - core], dst_ref=dst_ref.at[k], send_sem=send_sems.at[k],
            recv_sem=recv_sems.at[k], device_id=(x, y, 1 - core), device_id_type=pl.DeviceIdType.MESH) for k in range(4)]
        for cp in copies:
            cp.start()
        for cp in copies:
            cp.wait()

    hbm = pl.BlockSpec(memory_space=pl.ANY)
    theirs = pl.pallas_call(swap, name=name + "_swap", out_shape=jax.ShapeDtypeStruct((4, r, c), chunks.dtype),
                            in_specs=[hbm], out_specs=hbm,
                            scratch_shapes=[pltpu.SemaphoreType.DMA((4,)), pltpu.SemaphoreType.DMA((4,))])(chunks)

    def add(mine_ref, theirs_ref, out_ref):
        core = lax.axis_index("c")
        both = mine_ref[...].astype(F32)
        out_ref[...] = (jnp.where(core == 0, both[0], both[1]) + theirs_ref[...].astype(F32)).astype(out_ref.dtype)

    return pl.pallas_call(
        add, name=name + "_add", grid=(4, r // tr), out_shape=jax.ShapeDtypeStruct((4, r, c), chunks.dtype),
        in_specs=[pl.BlockSpec((None, 2, tr, c), lambda k, i: (k, 0, i, 0)), pl.BlockSpec((None, tr, c), lambda k, i: (k, i, 0))],
        out_specs=pl.BlockSpec((None, tr, c), lambda k, i: (k, i, 0)),
        compiler_params=_cparams(("parallel", "parallel")),
    )(chunks.reshape(4, 2, r, c), theirs)


def _chip_scatter_transfer(pairs):
    return _Transfer(pairs, pairs.shape, lambda ref, p: ref.at[p], lambda ref, s: ref.at[s], same_core=True)


def _matmul(a, b, mode, *, name, tm, tn, tk, epi=None, extra=(), out_dtype=F32, chunks=None, comm=None):
    pieces = list(a) if isinstance(a, (list, tuple)) else [a]
    rows_a, cols_a = pieces[0].shape[0], sum(p.shape[1] for p in pieces)
    if mode == "nn":
        (M, K), N = (rows_a, cols_a), b.shape[1]
    elif mode == "nt":
        (M, K), N = (rows_a, cols_a), b.shape[0]
    else:
        (K, M), N = (rows_a, cols_a), b.shape[1]
    tm, tn, tk = min(tm, M), min(tn, N), min(tk, K)
    assert M % tm == 0 and N % tn == 0 and K % tk == 0 and (epi != "ln" or tn == N), (name, M, N, K)
    nk = K // tk
    tile_cols, axis = (tm, 0) if mode == "tn" else (tk, 2)
    assert all(p.shape[1] % tile_cols == 0 for p in pieces)
    counts = [p.shape[1] // tile_cols for p in pieces]
    starts = [sum(counts[:q]) for q in range(len(pieces))]

    def a_spec_of(q):
        at = lambda t: jnp.clip(t - starts[q], 0, counts[q] - 1) if len(pieces) > 1 else t
        return {"nn": pl.BlockSpec((tm, tk), lambda i, j, k: (i, at(k))),
                "nt": pl.BlockSpec((tm, tk), lambda i, j, k: (i, at(k))),
                "tn": pl.BlockSpec((tk, tm), lambda i, j, k: (k, at(i)))}[mode]

    n_a = len(pieces)
    b_mode = pl.Buffered(1) if (nk == 1 and tn == N and n_a > 1) else None
    b_spec = {"nn": pl.BlockSpec((tk, tn), lambda i, j, k: (k, j), pipeline_mode=b_mode),
              "nt": pl.BlockSpec((tn, tk), lambda i, j, k: (j, k), pipeline_mode=b_mode),
              "tn": pl.BlockSpec((tk, tn), lambda i, j, k: (k, j), pipeline_mode=b_mode)}[mode]
    ca, cb = {"nn": (1, 0), "nt": (1, 1), "tn": (0, 0)}[mode]
    tile = pl.BlockSpec((tm, tn), lambda i, j, k: (i, j))
    row = pl.BlockSpec((1, tn), lambda i, j, k: (0, j))
    n_extra = {None: 0, "add": 1, "relu2": 0, "drelu2": 1, "ln": 3}[epi]
    assert len(extra) == n_extra
    extra_specs = {None: [], "add": [tile], "relu2": [], "drelu2": [tile], "ln": [tile, row, row]}[epi]
    split = 0
    if epi == "relu2":
        out_shape, out_specs = (jax.ShapeDtypeStruct((M, N), BF16),), (tile,)
    elif epi == "ln":
        out_shape = (jax.ShapeDtypeStruct((M, N), F32), jax.ShapeDtypeStruct((M, N), F32),
                     jax.ShapeDtypeStruct((M, N), BF16))
        out_specs = (tile, tile, tile)
    elif chunks == "cols":
        c = N // N_DEV
        out_shape = (jax.ShapeDtypeStruct((N_DEV, M, c), out_dtype),)
        if tn == N:
            split = c
            out_specs = (pl.BlockSpec((N_DEV, tm, c), lambda i, j, k: (0, i, 0)),)
        else:
            assert c % tn == 0
            out_specs = (pl.BlockSpec((None, tm, tn), lambda i, j, k: (j // (c // tn), i, j % (c // tn))),)
    else:
        out_shape, out_specs = (jax.ShapeDtypeStruct((M, N), out_dtype),), (tile,)
    n_out = len(out_shape)

    def body(*refs):
        a_refs, b_ref = refs[:n_a], refs[n_a]
        ex = refs[n_a + 1:n_a + 1 + n_extra]
        outs = refs[n_a + 1 + n_extra:n_a + 1 + n_extra + n_out]
        acc_ref = refs[-1]
        k = pl.program_id(2)

        def finish(acc):
            if epi == "add":
                outs[0][...] = (acc + ALPHA * ex[0][...]).astype(out_dtype)
            elif epi == "relu2":
                r = jnp.maximum(acc, 0.0)
                outs[0][...] = _bf(r * r)
            elif epi == "drelu2":
                outs[0][...] = (acc * (2.0 * jnp.sqrt(ex[0][...].astype(F32)))).astype(out_dtype)
            elif epi == "ln":
                u = ALPHA * ex[0][...] + acc
                xh, _ = _norm_stats(u)
                y = xh * ex[1][...] + ex[2][...]
                outs[0][...] = u
                outs[1][...] = y
                outs[2][...] = _bf(y)
            elif split:
                for p in range(N_DEV):
                    outs[0][p] = acc[:, p * split:(p + 1) * split].astype(out_dtype)
            else:
                outs[0][...] = acc.astype(out_dtype)

        def step(a_ref, first, middle, last):
            part = _dg(_bf(a_ref[...]), _bf(b_ref[...]), ca, cb)
            if nk == 1:
                finish(part)
                return
            if first:
                @pl.when(k == 0)
                def _():
                    acc_ref[...] = part

            if middle:
                @pl.when(jnp.logical_and(k > 0, k < nk - 1))
                def _():
                    acc_ref[...] += part

            if last:
                @pl.when(k == nk - 1)
                def _():
                    finish(acc_ref[...] + part)

        if n_a == 1:
            step(a_refs[0], True, True, True)
        else:
            t = pl.program_id(axis)
            for q in range(n_a):
                along_k = axis == 2
                first = not along_k or starts[q] == 0
                last = not along_k or starts[q] + counts[q] == nk
                middle = not along_k or counts[q] > int(first) + int(last)

                @pl.when(jnp.logical_and(t >= starts[q], t < starts[q] + counts[q]))
                def _(q=q, first=first, middle=middle, last=last):
                    step(a_refs[q], first, middle, last)

    outs, landed = _pcall(
        body, name=name, out_shape=out_shape, grid=(M // tm, N // tn, nk),
        in_specs=[a_spec_of(q) for q in range(n_a)] + [b_spec] + extra_specs, out_specs=out_specs,
        scratch_shapes=[pltpu.VMEM((tm, tn) if nk > 1 else (8, 128), F32)], sem=("parallel", "parallel", "arbitrary"),
        args=(*pieces, b, *extra), comm=comm)
    res = outs[0] if n_out == 1 else tuple(outs)
    if chunks == "rows":
        res = res.reshape(N_DEV, M // N_DEV, N)
    return res if comm is None else (res, landed)


def _matmul_rows_of(pieces, b, res, *, name, tm, comm=None):
    M, (K, N) = pieces[0].shape[0], b.shape
    tm = min(tm, M)
    subs, start = [], 0
    for q, p in enumerate(pieces):
        w = p.shape[1]
        step = w if start % w == 0 else 512
        assert w % step == 0 and start % step == 0
        subs += [(q, off, step, start + off) for off in range(0, w, step)]
        start += w
    assert start == K
    n_p, n_s = len(pieces), len(subs)

    def body(*refs):
        a_refs, b_refs, res_ref, out_ref = refs[:n_p], refs[n_p:n_p + n_s], refs[n_p + n_s], refs[n_p + n_s + 1]
        acc = None
        for (q, off, w, _), b_ref in zip(subs, b_refs):
            part = _dg(_bf(a_refs[q][:, off:off + w]), _bf(b_ref[...]), 1, 0)
            acc = part if acc is None else acc + part
        out_ref[...] = acc + ALPHA * res_ref[...]

    tile = pl.BlockSpec((tm, N), lambda i: (i, 0))
    outs, landed = _pcall(
        body, name=name, grid=(M // tm,), out_shape=[jax.ShapeDtypeStruct((M, N), F32)],
        in_specs=[pl.BlockSpec((tm, p.shape[1]), lambda i: (i, 0)) for p in pieces] +
                 [pl.BlockSpec((w, N), lambda i, r=row // w: (r, 0), pipeline_mode=pl.Buffered(1))
                  for _, _, w, row in subs] + [tile],
        out_specs=[tile], scratch_shapes=[], sem=("parallel",), args=(*pieces, *([b] * n_s), res), comm=comm)
    return outs[0] if comm is None else (outs[0], landed)


def _ret_tables(S):
    half = 64
    inv_freq = ROPE_BASE ** (-jnp.arange(half, dtype=F32) / half)
    ang = jnp.arange(S, dtype=jnp.int32).astype(F32)[:, None] * inv_freq[None, :]
    cos, sin = jnp.cos(ang), jnp.sin(ang)
    cosf = jnp.concatenate([cos, cos], axis=1)
    sinf = jnp.concatenate([-sin, sin], axis=1)
    log_g = jnp.log(1.0 - 2.0 ** (-5.0 - jnp.arange(4, dtype=F32)))
    idx = jnp.arange(CHUNK, dtype=F32)
    diff = idx[:, None] - idx[None, :]
    md = jnp.where(diff[None] >= 0, jnp.exp(log_g[:, None, None] * diff[None]), 0.0)
    kd = jnp.exp(log_g[:, None] * (CHUNK - 1 - idx)[None, :])
    qd = jnp.exp(log_g[:, None] * (idx + 1.0)[None, :])
    cd = jnp.exp(log_g * CHUNK)
    bc = lambda t: jnp.broadcast_to(t[:, :, None], (4, CHUNK, CHUNK))
    return cosf, sinf, md, bc(qd), bc(kd), jnp.broadcast_to(cd[:, None, None], (4, 8, CHUNK))


def _rot(x, cosf, sinf):
    return x * cosf + pltpu.roll(x, 64, 1) * sinf


def _rot_t(dx, cosf, sinf):
    return dx * cosf - pltpu.roll(dx, 64, 1) * sinf


RET_CHUNKS = 2


def _ret_specs(rev, S):
    R = min(RET_CHUNKS, S // CHUNK)
    rows, steps = R * CHUNK, S // (R * CHUNK)
    rn = (lambda n: steps - 1 - n) if rev else (lambda n: n)
    col = lambda c: pl.BlockSpec((rows, 512), lambda n, c=c: (rn(n), c))
    tab = pl.BlockSpec((rows, CHUNK), lambda n: (rn(n), 0))
    dec = pl.BlockSpec((4, CHUNK, CHUNK), lambda n: (0, 0, 0))
    cdec = pl.BlockSpec((4, 8, CHUNK), lambda n: (0, 0, 0))
    vec = pl.BlockSpec((1, 512), lambda n: (0, 0))
    st = pl.BlockSpec((R, 4, CHUNK, CHUNK), lambda n: (rn(n), 0, 0, 0))
    return R, steps, rn, col, tab, dec, cdec, vec, st


def _ret_fwd(proj, tables, gn_g, gn_b):
    S = proj.shape[0]
    R, steps, _, col, tab, dec, cdec, vec, st = _ret_specs(False, S)

    def body(q_ref, k_ref, v_ref, g_ref, cos_ref, sin_ref, md_ref, qd_ref, kd_ref, cd_ref, gng_ref, gnb_ref,
             out_ref, st_ref, state):
        @pl.when(pl.program_id(0) == 0)
        def _():
            state[...] = jnp.zeros_like(state)

        tiles = [(c, h) for c in range(R) for h in range(4)]
        rs = lambda c: slice(c * CHUNK, (c + 1) * CHUNK)
        sl = lambda h: slice(h * 128, (h + 1) * 128)
        qr = [_rot(q_ref[rs(c), sl(h)].astype(F32), cos_ref[rs(c), :], sin_ref[rs(c), :]) for c, h in tiles]
        kr = [_rot(k_ref[rs(c), sl(h)].astype(F32), cos_ref[rs(c), :], sin_ref[rs(c), :]) * (128 ** -0.5)
              for c, h in tiles]
        vb = [_bf(v_ref[rs(c), sl(h)]) for c, h in tiles]
        kv = [_dg(_bf(k * kd_ref[h]), v, 0, 0) for k, v, (c, h) in zip(kr, vb, tiles)]
        before = {}
        for h in range(4):
            s_h = state[h]
            for c in range(R):
                st_ref[c, h] = s_h
                before[(c, h)] = s_h
                s_h = s_h * cd_ref[h, 0:1, :] + kv[c * 4 + h]
            state[h] = s_h
        sc = [_dg(_bf(q), _bf(k), 1, 1) * md_ref[h] for q, k, (c, h) in zip(qr, kr, tiles)]
        r = [_dg(_bf(x), v, 1, 0) + _dg(_bf(q * qd_ref[h]), _bf(before[(c, h)]), 1, 0)
             for x, v, q, (c, h) in zip(sc, vb, qr, tiles)]
        for x, (c, h) in zip(r, tiles):
            y, _ = _norm_stats(x)
            rg = g_ref[rs(c), sl(h)].astype(F32)
            out_ref[rs(c), sl(h)] = rg * _sigmoid(rg) * (y * gng_ref[:, sl(h)] + gnb_ref[:, sl(h)])

    return pl.pallas_call(
        body, name="ret_fwd", grid=(steps,),
        out_shape=(jax.ShapeDtypeStruct((S, RET_W), F32), jax.ShapeDtypeStruct((S // CHUNK, 4, CHUNK, CHUNK), F32)),
        in_specs=[col(0), col(1), col(2), col(3), tab, tab, dec, dec, dec, cdec, vec, vec],
        out_specs=(pl.BlockSpec((R * CHUNK, 512), lambda n: (n, 0)), st),
        scratch_shapes=[pltpu.VMEM((4, CHUNK, CHUNK), F32)],
        compiler_params=_cparams(("arbitrary",)),
    )(proj, proj, proj, proj, *tables, gn_g, gn_b)


def _ret_bwd(proj, tables, gn_g, gn_b, states, d_out):
    S = proj.shape[0]
    R, steps, rn, col, tab, dec, cdec, vec, st = _ret_specs(True, S)

    def kernel_body(q_ref, k_ref, v_ref, g_ref, cos_ref, sin_ref, md_ref, qd_ref, kd_ref, cd_ref, gng_ref, gnb_ref,
                    st_ref, do_ref, dp_ref, dg_ref, db_ref, gstate):
        @pl.when(pl.program_id(0) == 0)
        def _():
            gstate[...] = jnp.zeros_like(gstate)
            dg_ref[...] = jnp.zeros_like(dg_ref)
            db_ref[...] = jnp.zeros_like(db_ref)

        tiles = [(c, h) for c in range(R) for h in range(4)]
        rs = lambda c: slice(c * CHUNK, (c + 1) * CHUNK)
        sl = lambda h: slice(h * 128, (h + 1) * 128)
        rot = lambda ref, c, h: _rot(ref[rs(c), sl(h)].astype(F32), cos_ref[rs(c), :], sin_ref[rs(c), :])
        qr = [rot(q_ref, c, h) for c, h in tiles]
        kr = [rot(k_ref, c, h) * (128 ** -0.5) for c, h in tiles]
        qb, kb = [_bf(x) for x in qr], [_bf(x) for x in kr]
        vb = [_bf(v_ref[rs(c), sl(h)]) for c, h in tiles]
        s0b = [_bf(st_ref[c, h]) for c, h in tiles]
        scb = [_bf(_dg(q, k, 1, 1) * md_ref[h]) for q, k, (c, h) in zip(qb, kb, tiles)]
        qdb = [_bf(q * qd_ref[h]) for q, (c, h) in zip(qr, tiles)]
        kdb = [_bf(k * kd_ref[h]) for k, (c, h) in zip(kr, tiles)]
        r = [_dg(x, v, 1, 0) + _dg(q, s, 1, 0) for x, v, q, s in zip(scb, vb, qdb, s0b)]
        drb, d_rg = [], []
        for x, (c, h) in zip(r, tiles):
            y, rstd = _norm_stats(x)
            gng = gng_ref[:, sl(h)]
            rg = g_ref[rs(c), sl(h)].astype(F32)
            sg = _sigmoid(rg)
            d_o = do_ref[rs(c), sl(h)]
            d_gn = d_o * (rg * sg)
            dg_ref[:, sl(h)] += jnp.sum(d_gn * y, axis=0, keepdims=True)
            db_ref[:, sl(h)] += jnp.sum(d_gn, axis=0, keepdims=True)
            drb.append(_bf(_norm_bwd(d_gn * gng, y, rstd)))
            d_rg.append(_bf(d_o * (y * gng + gnb_ref[:, sl(h)]) * (sg * (1.0 + rg * (1.0 - sg)))))
        grow = [_dg(q, d, 0, 0) for q, d in zip(qdb, drb)]
        after = {}
        for h in range(4):
            g_h = gstate[h]
            for c in reversed(range(R)):
                after[(c, h)] = _bf(g_h)
                g_h = g_h * cd_ref[h, 0:1, :] + grow[c * 4 + h]
            gstate[h] = g_h
        dscb = [_bf(_dg(d, v, 1, 1) * md_ref[h]) for d, v, (c, h) in zip(drb, vb, tiles)]
        for t, (c, h) in enumerate(tiles):
            gb = after[(c, h)]
            dqr = _dg(dscb[t], kb[t], 1, 0) + _dg(drb[t], s0b[t], 1, 1) * qd_ref[h]
            dkr = _dg(dscb[t], qb[t], 0, 0) + _dg(vb[t], gb, 1, 1) * kd_ref[h]
            dv = _dg(scb[t], drb[t], 0, 0) + _dg(kdb[t], gb, 1, 0)
            cosf, sinf = cos_ref[rs(c), :], sin_ref[rs(c), :]
            dp_ref[rs(c), 0 * 512 + h * 128:0 * 512 + (h + 1) * 128] = _bf(_rot_t(dqr, cosf, sinf))
            dp_ref[rs(c), 1 * 512 + h * 128:1 * 512 + (h + 1) * 128] = _bf(_rot_t(dkr, cosf, sinf) * (128 ** -0.5))
            dp_ref[rs(c), 2 * 512 + h * 128:2 * 512 + (h + 1) * 128] = _bf(dv)
            dp_ref[rs(c), 3 * 512 + h * 128:3 * 512 + (h + 1) * 128] = d_rg[t]

    acc = pl.BlockSpec((1, 512), lambda n: (0, 0))
    return pl.pallas_call(
        kernel_body, name="ret_bwd", grid=(steps,),
        out_shape=(jax.ShapeDtypeStruct((S, 2048), BF16), jax.ShapeDtypeStruct((1, 512), F32),
                   jax.ShapeDtypeStruct((1, 512), F32)),
        in_specs=[col(0), col(1), col(2), col(3), tab, tab, dec, dec, dec, cdec, vec, vec, st,
                  pl.BlockSpec((R * CHUNK, 512), lambda n: (rn(n), 0))],
        out_specs=(pl.BlockSpec((R * CHUNK, 2048), lambda n: (rn(n), 0)), acc, acc),
        scratch_shapes=[pltpu.VMEM((4, CHUNK, CHUNK), F32)],
        compiler_params=_cparams(("arbitrary",)),
    )(proj, proj, proj, proj, *tables, gn_g, gn_b, states, d_out)


SB_T = 256
SB_SCALE = 64 ** -0.5
SB_Q_COL, SB_K_COL, SB_V_COL = 2048 // 128, 2560 // 128, 3072 // 128


def _head_masks():
    lane = lax.broadcasted_iota(jnp.int32, (1, 128), 1)
    m0 = (lane < 64).astype(F32)
    return m0, 1.0 - m0


def _tri(n, cmp):
    r = lax.broadcasted_iota(jnp.int32, (n, n), 0)
    c = lax.broadcasted_iota(jnp.int32, (n, n), 1)
    return cmp(r, c)


def _tri_sum(x, tri):
    hi = _bf(x)
    lo = _bf(x - hi.astype(F32))
    return _dg(hi, tri, 1, 0) + _dg(lo, tri, 1, 0)


def _sb_weights(qms, kblks, upper, carry, causal):
    tiles = [(b, h) for b in range(len(kblks)) for h in range(2)]
    zs = [_dg(qms[h], kblks[b], 1, 1) for b, h in tiles]
    lgs = [-(jnp.maximum(z, 0.0) + jnp.log(1.0 + jnp.exp(-jnp.abs(z)))) for z in zs]
    if causal is not None:
        lgs = [jnp.where(causal, lg, 0.0) if b == 0 else lg for lg, (b, h) in zip(lgs, tiles)]
    carries = list(carry)
    for t in range(len(tiles) - 2):
        carries.append(carries[t] + jnp.sum(lgs[t], axis=1, keepdims=True))
    his = [_bf(lg) for lg in lgs]
    los = [_bf(lg - hi.astype(F32)) for lg, hi in zip(lgs, his)]
    later = [_dg(hi, upper, 1, 0) for hi in his]
    later = [r + _dg(lo, upper, 1, 0) for r, lo in zip(later, los)]
    a = [jnp.exp(lg + z + (r + c)) for lg, z, r, c in zip(lgs, zs, later, carries)]
    if causal is not None:
        a = [jnp.where(causal, x, 0.0) if b == 0 else x for x, (b, h) in zip(a, tiles)]
    out = tuple(carries[t] + jnp.sum(lgs[t], axis=1, keepdims=True) for t in (len(tiles) - 2, len(tiles) - 1))
    return [a[2 * b:2 * b + 2] for b in range(len(kblks))], out


def _sb_fwd(proj, comm=None):
    S = proj.shape[0]
    T = min(SB_T, S)
    nq = S // T

    def body(q_ref, k_ref, v_ref, o_ref, a_ref, kb_ref, vm_ref, acc_ref):
        i = pl.program_id(1)
        m0, m1 = _head_masks()

        @pl.when(i == 0)
        def _():
            v = v_ref[...]
            kb_ref[...] = _bf(k_ref[...])
            vm_ref[0] = _bf(v * m0)
            vm_ref[1] = _bf(v * m1)

        q = q_ref[...]
        qm = (_bf(q * (m0 * SB_SCALE)), _bf(q * (m1 * SB_SCALE)))
        upper = _tri(T, lambda r, c: r > c).astype(BF16)
        causal = _tri(T, lambda r, c: c < r)

        def tiles(js, carry, mask, first):
            ks = [pl.multiple_of(j * T, T) for j in js]
            a, out = _sb_weights(qm, [kb_ref[pl.ds(k, T), :] for k in ks], upper, carry, mask)
            a = [[_bf(t) for t in per_block] for per_block in a]
            for b, j in enumerate(js):
                for h in range(2):
                    a_ref[h, j] = a[b][h]
            parts = [_dg(a[b][h], vm_ref[h, pl.ds(k, T), :], 1, 0) for b, k in enumerate(ks) for h in range(2)]
            part = functools.reduce(lambda u, w: u + w, parts)
            if first:
                acc_ref[...] = part
            else:
                acc_ref[...] += part
            return out

        zero = jnp.zeros((T, 1), F32)
        carry = lax.cond(i == 0, lambda: tiles([i], (zero, zero), causal, True),
                         lambda: tiles([i, i - 1], (zero, zero), causal, True))
        n = jnp.maximum(i - 1, 0)
        carry = lax.fori_loop(0, n % 2, lambda _, c: tiles([n - 1], c, None, False), carry)
        top = n - 1 - n % 2
        carry = lax.fori_loop(0, (n // 2) % 2, lambda _, c: tiles([top, top - 1], c, None, False), carry)
        top = top - 2 * ((n // 2) % 2)
        lax.fori_loop(0, n // 4, lambda jj, c: tiles([top - 4 * jj - b for b in range(4)], c, None, False), carry)
        o_ref[...] = acc_ref[...]

    full = lambda c: pl.BlockSpec((S, 128), lambda p, i, c=c: (0, c + p))
    outs, landed = _pcall(
        body, name="sb_fwd", grid=(4, nq),
        out_shape=[jax.ShapeDtypeStruct((S, SB_W), F32), jax.ShapeDtypeStruct((4, 2, nq, nq, T, T), BF16)],
        in_specs=[pl.BlockSpec((T, 128), lambda p, i: (i, SB_Q_COL + p)), full(SB_K_COL), full(SB_V_COL)],
        out_specs=[pl.BlockSpec((T, 128), lambda p, i: (i, p)),
                   pl.BlockSpec((None, 2, None, nq, T, T), lambda p, i: (p, 0, i, 0, 0, 0))],
        scratch_shapes=[pltpu.VMEM((S, 128), BF16), pltpu.VMEM((2, S, 128), BF16), pltpu.VMEM((T, 128), F32)],
        sem=("arbitrary", "arbitrary"), args=(proj, proj, proj), comm=comm)
    return tuple(outs) if comm is None else (tuple(outs), landed)


def _sb_bwd(proj, a_saved, d_o, comm=None):
    S = proj.shape[0]
    T = min(SB_T, S)
    nq = S // T

    def body(q_ref, k_ref, v_ref, do_ref, a_ref, dq_ref, dk_ref, dv_ref, kb_ref, kbm_ref, vb_ref, dq_acc, dk_acc, dv_acc):
        i = pl.program_id(1)
        m0, m1 = _head_masks()

        @pl.when(i == 0)
        def _():
            k = k_ref[...]
            kb_ref[...] = _bf(k)
            kbm_ref[0] = _bf(k * m0)
            kbm_ref[1] = _bf(k * m1)
            vb_ref[...] = _bf(v_ref[...])
            dk_acc[...] = jnp.zeros_like(dk_acc)
            dv_acc[...] = jnp.zeros_like(dv_acc)

        q, d_out = q_ref[...], do_ref[...]
        qm = (_bf(q * (m0 * SB_SCALE)), _bf(q * (m1 * SB_SCALE)))
        dom = (_bf(d_out * m0), _bf(d_out * m1))
        qm_t = tuple(_bf((q * (m * SB_SCALE)).T) for m in (m0, m1))
        dom_t = tuple(_bf((d_out * m).T) for m in (m0, m1))
        lower = _tri(T, lambda r, c: r < c).astype(BF16)
        causal = _tri(T, lambda r, c: c < r)

        def up(js, carry, mask):
            ks = [pl.multiple_of(j * T, T) for j in js]
            tiles = [(b, h) for b in range(len(js)) for h in range(2)]
            zs = [_dg(qm[h], kb_ref[pl.ds(ks[b], T), :], 1, 1) for b, h in tiles]
            a = [a_ref[h, js[b]] for b, h in tiles]
            es = [w.astype(F32) * _dg(dom[h], vb_ref[pl.ds(ks[b], T), :], 1, 1) for w, (b, h) in zip(a, tiles)]
            carries = list(carry)
            for t in range(len(tiles)):
                carries.append(carries[t] + jnp.sum(es[t], axis=1, keepdims=True))
            d_lg = [_dg(_bf(e), lower, 1, 0) + c for e, c in zip(es, carries)]
            ens = [jnp.exp(-jnp.abs(z)) for z in zs]
            invs = [1.0 / (1.0 + en) for en in ens]
            betas = [jnp.where(z >= 0.0, inv, en * inv) for z, en, inv in zip(zs, ens, invs)]
            dzs = [e * (1.0 - b) - d * b for e, b, d in zip(es, betas, d_lg)]
            if mask is not None:
                dzs = [jnp.where(mask, dz, 0.0) if b == len(js) - 1 else dz for dz, (b, h) in zip(dzs, tiles)]
            dzs = [_bf(dz) for dz in dzs]
            parts = [_dg(dzs[t], kbm_ref[h, pl.ds(ks[b], T), :], 1, 0) for t, (b, h) in enumerate(tiles)]
            dq_acc[...] += functools.reduce(lambda u, w: u + w, parts)
            for b, j in enumerate(js):
                dk_acc[j] += _dg(qm_t[0], dzs[2 * b], 1, 0) + _dg(qm_t[1], dzs[2 * b + 1], 1, 0)
                dv_acc[j] += _dg(dom_t[0], a[2 * b], 1, 0) + _dg(dom_t[1], a[2 * b + 1], 1, 0)
            return tuple(carries[-2:])

        zero = jnp.zeros((T, 1), F32)
        dq_acc[...] = jnp.zeros_like(dq_acc)
        n = jnp.maximum(i - 1, 0)
        carry = lax.fori_loop(0, n // 4, lambda jj, c: up([4 * jj + b for b in range(4)], c, None), (zero, zero))
        done = 4 * (n // 4)
        carry = lax.fori_loop(0, (n // 2) % 2, lambda _, c: up([done, done + 1], c, None), carry)
        carry = lax.fori_loop(0, n % 2, lambda _, c: up([n - 1], c, None), carry)

        @pl.when(i == 0)
        def _():
            up([i], carry, causal)

        @pl.when(i > 0)
        def _():
            up([i - 1, i], carry, causal)

        dq_ref[...] = _bf(dq_acc[...] * SB_SCALE)

        @pl.when(i == nq - 1)
        def _():
            for j in range(nq):
                dk_ref[j * T:(j + 1) * T, :] = _bf(dk_acc[j].T)
                dv_ref[j * T:(j + 1) * T, :] = _bf(dv_acc[j].T)

    full = lambda c: pl.BlockSpec((S, 128), lambda p, i, c=c: (0, c + p))
    tile = pl.BlockSpec((T, 128), lambda p, i: (i, p))
    acc = pl.BlockSpec((S, 128), lambda p, i: (0, p))
    out = jax.ShapeDtypeStruct((S, SB_W), BF16)
    outs, landed = _pcall(
        body, name="sb_bwd", grid=(4, nq), out_shape=[out, out, out],
        in_specs=[pl.BlockSpec((T, 128), lambda p, i: (i, SB_Q_COL + p)), full(SB_K_COL), full(SB_V_COL), tile,
                  pl.BlockSpec((None, 2, None, nq, T, T), lambda p, i: (p, 0, i, 0, 0, 0))],
        out_specs=[tile, acc, acc],
        scratch_shapes=[pltpu.VMEM((S, 128), BF16), pltpu.VMEM((2, S, 128), BF16), pltpu.VMEM((S, 128), BF16),
                        pltpu.VMEM((T, 128), F32), pltpu.VMEM((nq, 128, T), F32), pltpu.VMEM((nq, 128, T), F32)],
        sem=("arbitrary", "arbitrary"), args=(proj, proj, proj, d_o, a_saved), comm=comm)
    return tuple(outs) if comm is None else (tuple(outs), landed)


SGU_U_COL, SGU_V_COL = 3584 // 512, 4096 // 512


def _causal(w):
    r = lax.broadcasted_iota(jnp.int32, (CHUNK, CHUNK), 0)
    c = lax.broadcasted_iota(jnp.int32, (CHUNK, CHUNK), 1)
    return jnp.where(r >= c, w, 0.0)


SGU_CHUNKS = 4


def _sgu_fwd(proj, ln_g, ln_b, w, b):
    S = proj.shape[0]
    R = min(SGU_CHUNKS, S // CHUNK)
    rows = R * CHUNK

    def body(u_ref, v_ref, g_ref, b_ref, w_ref, bias_ref, out_ref):
        wc = [_bf(_causal(w_ref[g])) for g in range(4)]
        for r in range(R):
            rs = slice(r * CHUNK, (r + 1) * CHUNK)
            u = _gelu(u_ref[rs, :].astype(F32))
            xh, _ = _norm_stats(_gelu(v_ref[rs, :].astype(F32)))
            vn = _bf(xh * g_ref[...] + b_ref[...])
            for g in range(4):
                sl = slice(g * 128, (g + 1) * 128)
                out_ref[rs, sl] = u[:, sl] * (_dg(wc[g], vn[:, sl], 1, 0) + bias_ref[g])

    vec = pl.BlockSpec((1, 512), lambda n: (0, 0))
    return pl.pallas_call(
        body, name="sgu_fwd", grid=(S // rows,),
        out_shape=jax.ShapeDtypeStruct((S, SGU_W), F32),
        in_specs=[pl.BlockSpec((rows, 512), lambda n: (n, SGU_U_COL)),
                  pl.BlockSpec((rows, 512), lambda n: (n, SGU_V_COL)), vec, vec,
                  pl.BlockSpec((4, CHUNK, CHUNK), lambda n: (0, 0, 0)), pl.BlockSpec((4, CHUNK, 1), lambda n: (0, 0, 0))],
        out_specs=pl.BlockSpec((rows, 512), lambda n: (n, 0)),
        compiler_params=_cparams(("parallel",)),
    )(proj, proj, ln_g, ln_b, w, b)


def _sgu_bwd(proj, ln_g, ln_b, w, b, d_out):
    S = proj.shape[0]
    R = min(SGU_CHUNKS, S // CHUNK)
    rows = R * CHUNK

    def body(u_ref, v_ref, g_ref, b_ref, w_ref, bias_ref, do_ref, dp_ref, dg_ref, db_ref, dw_ref, dbias_ref):
        @pl.when(pl.program_id(0) == 0)
        def _():
            dg_ref[...] = jnp.zeros_like(dg_ref)
            db_ref[...] = jnp.zeros_like(db_ref)
            dw_ref[...] = jnp.zeros_like(dw_ref)
            dbias_ref[...] = jnp.zeros_like(dbias_ref)

        ln_gain = g_ref[...]
        wc = [_bf(_causal(w_ref[g])) for g in range(4)]
        for r in range(R):
            rs = slice(r * CHUNK, (r + 1) * CHUNK)
            gu, gv = u_ref[rs, :].astype(F32), v_ref[rs, :].astype(F32)
            u = _gelu(gu)
            xh, rstd = _norm_stats(_gelu(gv))
            vn = _bf(xh * ln_gain + b_ref[...])
            d_o = do_ref[rs, :]
            d_vn = []
            for g in range(4):
                sl = slice(g * 128, (g + 1) * 128)
                sv = _dg(wc[g], vn[:, sl], 1, 0) + bias_ref[g]
                dp_ref[rs, sl] = _bf(d_o[:, sl] * sv * _gelu_grad(gu[:, sl]))
                d_sv = d_o[:, sl] * u[:, sl]
                dbias_ref[g] += jnp.sum(d_sv, axis=1, keepdims=True)
                d_svb = _bf(d_sv)
                dw_ref[g] += _causal(_dg(d_svb, vn[:, sl], 1, 1))
                d_vn.append(_dg(wc[g], d_svb, 0, 0))
            d_vn = jnp.concatenate(d_vn, axis=1)
            dg_ref[...] += jnp.sum(d_vn * xh, axis=0, keepdims=True)
            db_ref[...] += jnp.sum(d_vn, axis=0, keepdims=True)
            dp_ref[rs, 512:1024] = _bf(_norm_bwd(d_vn * ln_gain, xh, rstd) * _gelu_grad(gv))

    vec = pl.BlockSpec((1, 512), lambda n: (0, 0))
    wspec = pl.BlockSpec((4, CHUNK, CHUNK), lambda n: (0, 0, 0))
    bspec = pl.BlockSpec((4, CHUNK, 1), lambda n: (0, 0, 0))
    return pl.pallas_call(
        body, name="sgu_bwd", grid=(S // rows,),
        out_shape=(jax.ShapeDtypeStruct((S, 1024), BF16), jax.ShapeDtypeStruct((1, 512), F32),
                   jax.ShapeDtypeStruct((1, 512), F32), jax.ShapeDtypeStruct((4, CHUNK, CHUNK), F32),
                   jax.ShapeDtypeStruct((4, CHUNK, 1), F32)),
        in_specs=[pl.BlockSpec((rows, 512), lambda n: (n, SGU_U_COL)),
                  pl.BlockSpec((rows, 512), lambda n: (n, SGU_V_COL)), vec, vec, wspec, bspec,
                  pl.BlockSpec((rows, 512), lambda n: (n, 0))],
        out_specs=(pl.BlockSpec((rows, 1024), lambda n: (n, 0)), vec, vec, wspec, bspec),
        compiler_params=_cparams(("arbitrary",)),
    )(proj, proj, ln_g, ln_b, w, b, d_out)


GATE_COL = 4608 // 512


def _merge_fwd(proj, branches, p_list, tm=512):
    S = proj.shape[0]
    tm = min(tm, S)

    def body(r_ref, s_ref, g_ref, pr_ref, ps_ref, pg_ref, gr_ref, gs_ref, gg_ref, m_ref, br_ref):
        acc = None
        for k, (x_ref, p_ref, gate_ref) in enumerate(((r_ref, pr_ref, gr_ref), (s_ref, ps_ref, gs_ref),
                                                      (g_ref, pg_ref, gg_ref))):
            br = _dg(_bf(x_ref[...]), _bf(p_ref[...]), 1, 0)
            br_ref[k] = _bf(br)
            term = _sigmoid(gate_ref[...].astype(F32)) * br
            acc = term if acc is None else acc + term
        m_ref[...] = _bf(acc)

    xs = pl.BlockSpec((tm, 512), lambda i, n: (i, 0))
    ps = pl.BlockSpec((512, 512), lambda i, n: (0, n))
    gate = lambda k: pl.BlockSpec((tm, 512), lambda i, n, k=k: (i, GATE_COL + 2 * k + n))
    return pl.pallas_call(
        body, name="merge_fwd", grid=(S // tm, 2),
        out_shape=(jax.ShapeDtypeStruct((S, D_MODEL), BF16), jax.ShapeDtypeStruct((3, S, D_MODEL), BF16)),
        in_specs=[xs, xs, xs, ps, ps, ps, gate(0), gate(1), gate(2)],
        out_specs=(pl.BlockSpec((tm, 512), lambda i, n: (i, n)), pl.BlockSpec((3, tm, 512), lambda i, n: (0, i, n))),
        compiler_params=_cparams(("parallel", "parallel")),
    )(*branches, *p_list, proj, proj, proj)


def _gate_bwd(proj, br, d_merged, tm=512):
    S = proj.shape[0]
    tm = min(tm, S)

    def body(dm_ref, br_ref, gr_ref, gs_ref, gg_ref, *out_refs):
        dm = dm_ref[...]
        for k, gate_ref in enumerate((gr_ref, gs_ref, gg_ref)):
            s = _sigmoid(gate_ref[...].astype(F32))
            out_refs[k][...] = _bf(dm * s)
            out_refs[3 + k][...] = _bf(dm * br_ref[k].astype(F32) * (s * (1.0 - s)))

    gate = lambda k: pl.BlockSpec((tm, 512), lambda i, n, k=k: (i, GATE_COL + 2 * k + n))
    three = pl.BlockSpec((3, tm, 512), lambda i, n: (0, i, n))
    tile = pl.BlockSpec((tm, 512), lambda i, n: (i, n))
    outs = pl.pallas_call(
        body, name="gate_bwd", grid=(S // tm, 2),
        out_shape=[jax.ShapeDtypeStruct((S, D_MODEL), BF16)] * 6,
        in_specs=[tile, three, gate(0), gate(1), gate(2)], out_specs=[tile] * 6,
        compiler_params=_cparams(("parallel", "parallel")),
    )(d_merged, br, proj, proj, proj)
    return outs[:3], outs[3:]


def _ln_bwd(dy, u, g, target=None, tm=256):
    S, D = u.shape
    tm = min(tm, S)
    loss = target is not None

    def body(*refs):
        dy_ref, u_ref, g_ref = refs[:3]
        du_ref, dub_ref, dg_ref, db_ref = refs[3 + loss:7 + loss]

        @pl.when(pl.program_id(0) == 0)
        def _():
            for acc_ref in refs[5 + loss:]:
                acc_ref[...] = jnp.zeros_like(acc_ref)

        dy_t = dy_ref[...]
        if loss:
            err = dy_t - refs[3][...]
            refs[-1][...] += jnp.sum(err * err, axis=0, keepdims=True)
            dy_t = err * (1.0 / D)
        xh, rstd = _norm_stats(u_ref[...])
        dg_ref[...] += jnp.sum(dy_t * xh, axis=0, keepdims=True)
        db_ref[...] += jnp.sum(dy_t, axis=0, keepdims=True)
        du = _norm_bwd(dy_t * g_ref[...], xh, rstd)
        du_ref[...] = du
        dub_ref[...] = _bf(du)

    tile = pl.BlockSpec((tm, D), lambda i: (i, 0))
    vec = pl.BlockSpec((1, D), lambda i: (0, 0))
    row = jax.ShapeDtypeStruct((1, D), F32)
    return pl.pallas_call(
        body, name="ln_bwd", grid=(S // tm,),
        out_shape=[jax.ShapeDtypeStruct((S, D), F32), jax.ShapeDtypeStruct((S, D), BF16)] + [row] * (2 + loss),
        in_specs=[tile, tile, vec] + [tile] * loss, out_specs=[tile, tile] + [vec] * (2 + loss),
        compiler_params=_cparams(("arbitrary",)),
    )(dy, u, g, *([target] if loss else []))


def _layer_fwd(x, x_bf, W, tables, sb_comm=None):
    proj = _matmul(x_bf, W["w_in_t"], "nt", name="proj", tm=1024, tn=1536, tk=1024)
    retg, states = _ret_fwd(proj, tables, W["ret_gn_g"], W["ret_gn_b"])
    if sb_comm is None:
        sb, sb_a = _sb_fwd(proj)
    else:
        (sb, sb_a), landed = _sb_fwd(proj, comm=sb_comm[0])
        sb_comm[1](landed)
    sg = _sgu_fwd(proj, W["sgu_ln_g"], W["sgu_ln_b"], W["sgu_w"], W["sgu_b"])
    merged, br = _merge_fwd(proj, (retg, sb, sg), (W["p_ret"], W["p_sb"], W["p_sgu"]))
    u1, x1, x1_bf = _matmul(merged, W["w_out"], "nn", name="out_ln", tm=512, tn=1024, tk=1024, epi="ln",
                            extra=(x, W["ln1_g"], W["ln1_b"]))
    act = _matmul(x1_bf, W["w_up"], "nn", name="up", tm=1024, tn=2048, tk=1024, epi="relu2")
    u2, x2, x2_bf = _matmul(act, W["w_down"], "nn", name="down_ln", tm=512, tn=1024, tk=4096, epi="ln",
                            extra=(x1, W["ln2_g"], W["ln2_b"]))
    saved = dict(x_bf=x_bf, proj=proj, retg=retg, states=states, sb=sb, sb_a=sb_a, sg=sg, merged=merged, br=br, u1=u1,
                 x1_bf=x1_bf, act=act, u2=u2)
    return x2, x2_bf, saved


def _layer_bwd(d_x2, W, tables, sv, chunk_dtype=None, sb_comm_fn=None, dwin_comm_fn=None, dx_comm_fn=None, target=None):
    dt = F32 if chunk_dtype is None else chunk_dtype
    rows, cols = (None, None) if chunk_dtype is None else ("rows", "cols")
    g, landed = {}, {}
    du2, du2_bf, g["ln2_g"], g["ln2_b"], *sq = _ln_bwd(d_x2, sv["u2"], W["ln2_g"], target=target)
    if sq:
        landed["sq"] = sq[0]
    d_hpre = _matmul(du2_bf, W["w_down"], "nt", name="d_act", tm=1024, tn=2048, tk=1024, epi="drelu2",
                     extra=(sv["act"],), out_dtype=BF16)
    g["w_down"] = _matmul(sv["act"], du2_bf, "tn", name="dw_down", tm=512, tn=1024, tk=4096, out_dtype=dt, chunks=rows)
    g["w_up"] = _matmul(sv["x1_bf"], d_hpre, "tn", name="dw_up", tm=1024, tn=512, tk=4096, out_dtype=dt, chunks=cols)
    d_x1 = _matmul(d_hpre, W["w_up"], "nt", name="d_x1", tm=512, tn=1024, tk=4096, epi="add", extra=(du2,))
    du1, du1_bf, g["ln1_g"], g["ln1_b"] = _ln_bwd(d_x1, sv["u1"], W["ln1_g"])
    d_merged = _matmul(du1_bf, W["w_out"], "nt", name="d_merged", tm=1024, tn=1024, tk=1024)
    g["w_out"] = _matmul(sv["merged"], du1_bf, "tn", name="dw_out", tm=1024, tn=512, tk=4096, out_dtype=dt, chunks=rows)
    d_br, d_gate = _gate_bwd(sv["proj"], sv["br"], d_merged)
    d_branch = []
    for k, (nm, act) in enumerate((("p_ret", sv["retg"]), ("p_sb", sv["sb"]), ("p_sgu", sv["sg"]))):
        d_branch.append(_matmul(d_br[k], W[nm], "nt", name="d_" + nm[2:], tm=1024, tn=512, tk=1024))
        g[nm] = _matmul(act, d_br[k], "tn", name="dw_" + nm[2:], tm=512, tn=1024, tk=2048, out_dtype=dt, chunks=cols)
    d_ret, g["ret_gn_g"], g["ret_gn_b"] = _ret_bwd(sv["proj"], tables, W["ret_gn_g"], W["ret_gn_b"], sv["states"],
                                                   d_branch[0])
    if sb_comm_fn is None:
        d_sq, d_sk, d_sv = _sb_bwd(sv["proj"], sv["sb_a"], d_branch[1])
    else:
        (d_sq, d_sk, d_sv), landed["sb"] = _sb_bwd(sv["proj"], sv["sb_a"], d_branch[1], comm=sb_comm_fn(g))
    d_sgu, g["sgu_ln_g"], g["sgu_ln_b"], g["sgu_w"], g["sgu_b"] = _sgu_bwd(
        sv["proj"], W["sgu_ln_g"], W["sgu_ln_b"], W["sgu_w"], W["sgu_b"], d_branch[2])
    d_proj = [d_ret, d_sq, d_sk, d_sv, d_sgu, d_gate[0], d_gate[1], d_gate[2]]
    g["w_in"] = _matmul(d_proj, sv["x_bf"], "tn", name="dw_in", tm=256, tn=1024, tk=4096, out_dtype=dt, chunks=rows,
                        comm=None if dwin_comm_fn is None else dwin_comm_fn(g))
    if dwin_comm_fn is not None:
        g["w_in"], landed["dwin"] = g["w_in"]
    if chunk_dtype is None:
        g["w_in"] = g["w_in"].T
    d_x = _matmul_rows_of(d_proj, W["w_in_t"], du1, name="d_x", tm=512,
                          comm=None if dx_comm_fn is None else dx_comm_fn(g))
    if dx_comm_fn is not None:
        d_x, landed["dx"] = d_x
    return d_x, g, landed


BIG = ("w_in", "p_ret", "p_sb", "p_sgu", "w_out", "w_up", "w_down")
SMALL = ("ret_gn_g", "ret_gn_b", "sgu_ln_g", "sgu_ln_b", "sgu_w", "sgu_b", "ln1_g", "ln1_b", "ln2_g", "ln2_b")
GATHER_KIND = {"w_in": "rows", "p_ret": "cols", "p_sb": "cols", "p_sgu": "cols", "w_out": "rows", "w_up": "cols",
               "w_down": "rows"}


def _small_weights(small, l):
    W = {}
    for n in SMALL:
        if n == "sgu_w":
            W[n] = small[n][l]
        elif n == "sgu_b":
            W[n] = small[n][l].reshape(4, CHUNK, 1)
        else:
            W[n] = small[n][l].reshape(1, -1)
    return W


def _local_step(x, target, full, small):
    tables = _ret_tables(x.shape[0])
    Ws = [{**{n: full[n][l] for n in BIG[1:]}, "w_in_t": full["w_in"][l].T, **_small_weights(small, l)}
          for l in range(DEPTH)]
    saved = []
    h, h_bf = x, _bf(x)
    for l in range(DEPTH):
        h, h_bf, sv = _layer_fwd(h, h_bf, Ws[l], tables)
        saved.append(sv)
    grads = [None] * DEPTH
    d_h, grads[-1], landed = _layer_bwd(h, Ws[-1], tables, saved[-1], target=target)
    for l in reversed(range(DEPTH - 1)):
        d_h, grads[l], _ = _layer_bwd(d_h, Ws[l], tables, saved[l])
    return landed["sq"], d_h, grads


def _adam(w, parts, m, v, name):
    L, R, C = w.shape
    tr = next(t for t in (320, 256, 128) if R % t == 0)
    assert len(parts) == L

    def body(*refs):
        w_ref, p_refs, (m_ref, v_ref, g_ref, d_ref, nm_ref, nv_ref) = refs[0], refs[1:1 + L], refs[1 + L:]
        layer = pl.program_id(0)
        g = None
        for li, p_ref in enumerate(p_refs):
            s = p_ref[0].astype(F32)
            for j in range(1, p_ref.shape[0]):
                s = s + p_ref[j].astype(F32)
            g = s if g is None else jnp.where(layer == li, s, g)
        g_ref[...] = g
        d_ref[...], nm_ref[...], nv_ref[...] = _adam_update(w_ref[...], g, m_ref[...], v_ref[...])

    tile = pl.BlockSpec((None, tr, C), lambda l, i: (l, i, 0))
    part = lambda li: pl.BlockSpec((parts[li].shape[0], tr, C), lambda l, i, li=li: (0, jnp.where(l == li, i, 0), 0))
    out = jax.ShapeDtypeStruct((L, R, C), F32)
    return pl.pallas_call(
        body, name=name, grid=(L, R // tr), out_shape=(out, out, out, out),
        in_specs=[tile] + [part(li) for li in range(L)] + [tile, tile],
        out_specs=(tile, tile, tile, tile),
        compiler_params=_cparams(("parallel", "parallel")),
    )(w, *parts, m, v)


def _adam_update(w, g, m, v):
    m2 = ADAM_B1 * m + (1.0 - ADAM_B1) * g
    v2 = ADAM_B2 * v + (1.0 - ADAM_B2) * (g * g)
    m_hat = m2 / (1.0 - ADAM_B1 ** ADAM_STEP)
    v_hat = v2 / (1.0 - ADAM_B2 ** ADAM_STEP)
    return -ADAM_LR * (m_hat / (jnp.sqrt(v_hat) + ADAM_EPS) + ADAM_WD * w), m2, v2


def _adam_small(w, m, v, parts):
    k = len(SMALL)

    def body(*refs):
        w_refs, m_refs, v_refs, p_refs, outs = refs[:k], refs[k:2 * k], refs[2 * k:3 * k], refs[3 * k:5 * k], refs[5 * k:]
        for i in range(k):
            vector = len(w_refs[i].shape) == 2
            for l in range(DEPTH):
                p_ref = p_refs[DEPTH * i + l]
                g = p_ref[0]
                for j in range(1, N_DEV):
                    g = g + p_ref[j]
                at = (slice(l, l + 1), slice(None)) if vector else (l,)
                delta, m2, v2 = _adam_update(w_refs[i][at], g, m_refs[i][at], v_refs[i][at])
                for o_ref, val in zip(outs[4 * i:4 * i + 4], (g, delta, m2, v2)):
                    o_ref[at] = val

    vmem = pl.BlockSpec(memory_space=pltpu.VMEM)
    args = [w[n] for n in SMALL] + [m[n] for n in SMALL] + [v[n] for n in SMALL] + \
           [parts[(n, l)] for n in SMALL for l in range(DEPTH)]
    out_shape = [jax.ShapeDtypeStruct(w[n].shape, F32) for n in SMALL for _ in range(4)]
    outs = pl.pallas_call(body, name="adam_small", out_shape=out_shape, in_specs=[vmem] * len(args),
                          out_specs=[vmem] * len(out_shape), compiler_params=_cparams())(*args)
    return {n: tuple(outs[4 * i:4 * i + 4]) for i, n in enumerate(SMALL)}


WEIGHTS = ("w_in", "ret_gn_g", "ret_gn_b", "sgu_ln_g", "sgu_ln_b", "sgu_w", "sgu_b", "p_ret", "p_sb", "p_sgu", "w_out",
           "ln1_g", "ln1_b", "w_up", "w_down", "ln2_g", "ln2_b")


def kernel(x, w_in, ret_gn_g, ret_gn_b, sgu_ln_g, sgu_ln_b, sgu_w, sgu_b, p_ret, p_sb, p_sgu, w_out, ln1_g, ln1_b, w_up, w_down, ln2_g, ln2_b, loss_target, m_w_in, m_ret_gn_g, m_ret_gn_b, m_sgu_ln_g, m_sgu_ln_b, m_sgu_w, m_sgu_b, m_p_ret, m_p_sb, m_p_sgu, m_w_out, m_ln1_g, m_ln1_b, m_w_up, m_w_down, m_ln2_g, m_ln2_b, v_w_in, v_ret_gn_g, v_ret_gn_b, v_sgu_ln_g, v_sgu_ln_b, v_sgu_w, v_sgu_b, v_p_ret, v_p_sb, v_p_sgu, v_w_out, v_ln1_g, v_ln1_b, v_w_up, v_w_down, v_ln2_g, v_ln2_b):
    w = dict(zip(WEIGHTS, (w_in, ret_gn_g, ret_gn_b, sgu_ln_g, sgu_ln_b, sgu_w, sgu_b, p_ret, p_sb, p_sgu, w_out,
                           ln1_g, ln1_b, w_up, w_down, ln2_g, ln2_b)))
    m = dict(zip(WEIGHTS, (m_w_in, m_ret_gn_g, m_ret_gn_b, m_sgu_ln_g, m_sgu_ln_b, m_sgu_w, m_sgu_b, m_p_ret, m_p_sb,
                           m_p_sgu, m_w_out, m_ln1_g, m_ln1_b, m_w_up, m_w_down, m_ln2_g, m_ln2_b)))
    v = dict(zip(WEIGHTS, (v_w_in, v_ret_gn_g, v_ret_gn_b, v_sgu_ln_g, v_sgu_ln_b, v_sgu_w, v_sgu_b, v_p_ret, v_p_sb,
                           v_p_sgu, v_w_out, v_ln1_g, v_ln1_b, v_w_up, v_w_down, v_ln2_g, v_ln2_b)))

    small = {n: w[n] for n in SMALL}
    shard = {n: _bf(w[n]) for n in BIG}
    shard["w_in"] = shard["w_in"].transpose(0, 2, 1)
    S = x.shape[1]
    x0, target = x.reshape(S, D_MODEL), loss_target.reshape(S, D_MODEL)
    tables = _ret_tables(S)
    Ws = [_small_weights(small, l) for l in range(DEPTH)]

    (Ws[0]["w_in_t"],) = _exchange([_gather_transfer(shard["w_in"], 0, "rows")], "gather_w_in0", relay=True)
    def gather_under_sb(keys):
        def landed_fn(landed):
            for (n, l), z in zip(keys, landed):
                Ws[l]["w_in_t" if n == "w_in" else n] = z

        return _Comm([_gather_transfer(shard[n], l, GATHER_KIND[n]) for n, l in keys], relay=True), landed_fn

    h, h_bf, saved0 = _layer_fwd(x0, _bf(x0), Ws[0], tables,
                                 sb_comm=gather_under_sb([(n, 0) for n in BIG[1:]] + [("w_in", 1)]))
    h, _, saved1 = _layer_fwd(h, h_bf, Ws[1], tables, sb_comm=gather_under_sb([(n, 1) for n in BIG[1:]]))
    d_h, g1, landed1 = _layer_bwd(h, Ws[1], tables, saved1, chunk_dtype=BF16, target=target,
                                  sb_comm_fn=lambda g: _Comm([_scatter_transfer(g[n]) for n in BIG[1:]]))
    loss = lax.psum(0.5 * jnp.sum(landed1["sq"]) / D_MODEL, ("x", "y", "c"))
    early = [("w_in", 1)] + [(n, 0) for n in BIG[1:]]

    def small_slabs(g):
        return [_slab_transfer(g[n].reshape(4, CHUNK) if n == "sgu_b" else g[n]) for n in SMALL]

    def early_scatter(g0):
        return _Comm([_scatter_transfer((g1 if l else g0)[n]) for n, l in early] + small_slabs(g1))

    def late_scatter(g0):
        pairs = _pair_reduce(g0["w_in"], "w_in0_pairs")
        return _Comm([_chip_scatter_transfer(pairs)])

    d_x, g0, landed = _layer_bwd(d_h, Ws[0], tables, saved0, chunk_dtype=BF16, sb_comm_fn=early_scatter,
                                 dwin_comm_fn=lambda g: _Comm(small_slabs(g)), dx_comm_fn=late_scatter)
    parts = {**dict(zip(early, landed["sb"])), **{(n, 1): z for n, z in zip(BIG[1:], landed1["sb"])}}
    parts[("w_in", 0)] = landed["dx"][0]
    small_parts = {**{(n, 1): z for n, z in zip(SMALL, landed["sb"][len(early):])},
                   **{(n, 0): z for n, z in zip(SMALL, landed["dwin"])}}

    grad, delta, new_m, new_v = {}, {}, {}, {}
    for n in BIG:
        view = (lambda a: a.transpose(0, 2, 1)) if n == "w_in" else (lambda a: a)
        res = _adam(view(w[n]), [parts[(n, l)] for l in range(DEPTH)], view(m[n]), view(v[n]), "adam_" + n)
        grad[n], delta[n], new_m[n], new_v[n] = (view(r) for r in res)
    for n, res in _adam_small(small, m, v, small_parts).items():
        grad[n], delta[n], new_m[n], new_v[n] = res

    return (loss, d_x.reshape(x.shape), *[grad[n] for n in WEIGHTS], *[delta[n] for n in WEIGHTS],
            *[new_m[n] for n in WEIGHTS], *[new_v[n] for n in WEIGHTS])
```

```python
import functools
import math

import numpy as np
import jax
import jax.numpy as jnp
from jax import lax
from jax.experimental import pallas as pl
from jax.experimental.pallas import tpu as pltpu

F32 = jnp.float32
BF16 = jnp.bfloat16

N_DEV = 8
DEPTH = 2
D_MODEL = 1024
CHUNK = 128
RET_W = 512
SB_W = 512
SGU_W = 512
N_IN = 7680
LN_EPS = 1e-5
ALPHA = (2 * DEPTH) ** 0.25
ROPE_BASE = 10000.0
ADAM_LR, ADAM_B1, ADAM_B2, ADAM_EPS, ADAM_WD, ADAM_STEP = 0.001, 0.9, 0.999, 1e-08, 0.01, 10
VMEM_LIMIT = 56 * 1024 * 1024

_GELU_K = math.sqrt(2.0 / math.pi)
_GELU_C = 0.044715


def _cparams(sem=None):
    return pltpu.CompilerParams(dimension_semantics=sem, vmem_limit_bytes=VMEM_LIMIT)


def _dg(a, b, ca, cb):
    return lax.dot_general(a, b, (((ca,), (cb,)), ((), ())), preferred_element_type=F32)


def _bf(x):
    return x.astype(BF16)


def _sigmoid(x):
    return 1.0 / (1.0 + jnp.exp(-x))


def _gelu(x):
    t = jnp.tanh(_GELU_K * (x + _GELU_C * (x * x * x)))
    return x * (0.5 * (1.0 + t))


def _gelu_grad(x):
    t = jnp.tanh(_GELU_K * (x + _GELU_C * (x * x * x)))
    return 0.5 * (1.0 + t) + 0.5 * x * (1.0 - t * t) * (_GELU_K * (1.0 + 3.0 * _GELU_C * x * x))


def _norm_stats(u):
    mu = jnp.mean(u, axis=-1, keepdims=True)
    d = u - mu
    var = jnp.mean(d * d, axis=-1, keepdims=True)
    rstd = lax.rsqrt(var + LN_EPS)
    return d * rstd, rstd


def _norm_bwd(dxh, xh, rstd):
    return rstd * (dxh - jnp.mean(dxh, axis=-1, keepdims=True) - xh * jnp.mean(dxh * xh, axis=-1, keepdims=True))


class _Transfer:
    def __init__(self, src, dst_shape, src_at, dst_at, same_core=False):
        self.src, self.dst_shape, self.src_at, self.dst_at = src, tuple(dst_shape), src_at, dst_at
        self.same_core = same_core


def _gather_transfer(shard, l, kind):
    _, r, c = shard.shape
    src_at = lambda ref, p: ref.at[l]
    if kind == "slab":
        return _Transfer(shard, (N_DEV, r, c), src_at, lambda ref, s: ref.at[s])
    if kind == "rows":
        return _Transfer(shard, (N_DEV * r, c), src_at, lambda ref, s: ref.at[pl.ds(pl.multiple_of(s * r, r), r), :])
    return _Transfer(shard, (r, N_DEV * c), src_at, lambda ref, s: ref.at[:, pl.ds(pl.multiple_of(s * c, c), c)])


def _scatter_transfer(chunks):
    return _Transfer(chunks, chunks.shape, lambda ref, p: ref.at[p], lambda ref, s: ref.at[s])


def _slab_transfer(arr):
    return _Transfer(arr, (N_DEV,) + arr.shape, lambda ref, p: ref, lambda ref, s: ref.at[s])


class _Comm:
    def __init__(self, transfers, relay=False):
        self.transfers = list(transfers)
        self.relay = relay
        self.n = len(self.transfers)
        self.arrays = [t.src for t in self.transfers]
        self.out_shape = [jax.ShapeDtypeStruct(t.dst_shape, t.src.dtype) for t in self.transfers]
        self.scratch = [pltpu.SemaphoreType.DMA((self.n * (N_DEV - 1),)), pltpu.SemaphoreType.DMA((self.n * (N_DEV - 1),)),
                        pltpu.SemaphoreType.DMA((self.n,))]

    def _relay_copies(self, srcs, dsts, send_sems, recv_sems, local_sems):
        x, y, c = lax.axis_index("x"), lax.axis_index("y"), lax.axis_index("c")
        me = 4 * x + 2 * y + c
        chips = [(1 - x, y), (x, 1 - y), (1 - x, 1 - y)]
        first, passed, own = [], [], []
        for t, tr in enumerate(self.transfers):
            def copy(k, src, sender, to, t=t, tr=tr):
                return pltpu.make_async_remote_copy(
                    src_ref=src, dst_ref=tr.dst_at(dsts[t], sender), send_sem=send_sems.at[t * (N_DEV - 1) + k],
                    recv_sem=recv_sems.at[t * (N_DEV - 1) + k], device_id=to, device_id_type=pl.DeviceIdType.MESH)

            mine = tr.src_at(srcs[t], me)
            first.append([copy(0, mine, me, (x, y, 1 - c))] + [copy(1 + j, mine, me, (px, py, c))
                                                                for j, (px, py) in enumerate(chips)])
            passed.append([copy(4 + j, tr.dst_at(dsts[t], 4 * px + 2 * py + c), 4 * px + 2 * py + c, (x, y, 1 - c))
                           for j, (px, py) in enumerate(chips)])
            own.append(pltpu.make_async_copy(mine, tr.dst_at(dsts[t], me), local_sems.at[t]))
        return first, passed, own

    def _copies(self, srcs, dsts, send_sems, recv_sems, local_sems):
        x, y, c = lax.axis_index("x"), lax.axis_index("y"), lax.axis_index("c")
        me = 4 * x + 2 * y + c
        copies = []
        for d in range(1, N_DEV):
            px = 1 - x if d & 4 else x
            py = 1 - y if d & 2 else y
            pc = 1 - c if d & 1 else c
            for t, tr in enumerate(self.transfers):
                if tr.same_core and d & 1:
                    continue
                peer, mine = (2 * px + py, 2 * x + y) if tr.same_core else (4 * px + 2 * py + pc, me)
                k = t * (N_DEV - 1) + d - 1
                copies.append(pltpu.make_async_remote_copy(
                    src_ref=tr.src_at(srcs[t], peer), dst_ref=tr.dst_at(dsts[t], mine),
                    send_sem=send_sems.at[k], recv_sem=recv_sems.at[k],
                    device_id=(px, py, pc), device_id_type=pl.DeviceIdType.MESH))
        own = []
        for t, tr in enumerate(self.transfers):
            mine = 2 * x + y if tr.same_core else me
            own.append(pltpu.make_async_copy(tr.src_at(srcs[t], mine), tr.dst_at(dsts[t], mine), local_sems.at[t]))
        return copies, own

    def start(self, srcs, dsts, *sems):
        if self.relay:
            first, _, own = self._relay_copies(srcs, dsts, *sems)
            for cp in own + [cp for per_t in first for cp in per_t]:
                cp.start()
            return
        copies, own = self._copies(srcs, dsts, *sems)
        for cp in own + copies:
            cp.start()

    def pass_on(self, srcs, dsts, *sems):
        if self.relay:
            first, passed, _ = self._relay_copies(srcs, dsts, *sems)
            for j in range(3):
                for t in range(self.n):
                    first[t][1 + j].wait_recv()
                    passed[t][j].start()

    def finish(self, srcs, dsts, *sems):
        if self.relay:
            first, passed, own = self._relay_copies(srcs, dsts, *sems)
            for t in range(self.n):
                first[t][0].wait_recv()
                for cp in passed[t]:
                    cp.wait_recv()
            for t in range(self.n):
                for cp in first[t] + passed[t]:
                    cp.wait_send()
                own[t].wait()
            return
        copies, own = self._copies(srcs, dsts, *sems)
        for cp in copies + own:
            cp.wait()


def _pcall(body, *, name, grid, in_specs, out_specs, out_shape, scratch_shapes, sem, args, comm=None):
    in_specs, out_specs, out_shape = list(in_specs), list(out_specs), list(out_shape)
    if comm is None:
        outs = pl.pallas_call(body, name=name, grid=grid, in_specs=in_specs, out_specs=out_specs, out_shape=out_shape,
                              scratch_shapes=list(scratch_shapes), compiler_params=_cparams(sem))(*args)
        return list(outs), []
    n_in, n_out, n_scr, k = len(in_specs), len(out_specs), len(scratch_shapes), comm.n

    def carrier(*refs):
        ins, cin = refs[:n_in], refs[n_in:n_in + k]
        outs, cout = refs[n_in + k:n_in + k + n_out], refs[n_in + k + n_out:n_in + 2 * k + n_out]
        scr, sems = refs[n_in + 2 * k + n_out:n_in + 2 * k + n_out + n_scr], refs[n_in + 2 * k + n_out + n_scr:]
        ids = [pl.program_id(d) for d in range(len(grid))]
        first = functools.reduce(jnp.logical_and, [i == 0 for i in ids])
        last = functools.reduce(jnp.logical_and, [i == g - 1 for i, g in zip(ids, grid)])

        @pl.when(first)
        def _():
            comm.start(cin, cout, *sems)

        body(*ins, *outs, *scr)

        early_pass = comm.relay and len(grid) > 1 and grid[0] > 1
        if early_pass:
            @pl.when(functools.reduce(jnp.logical_and, [ids[0] == grid[0] - 1] + [i == 0 for i in ids[1:]]))
            def _():
                comm.pass_on(cin, cout, *sems)

        @pl.when(last)
        def _():
            if not early_pass:
                comm.pass_on(cin, cout, *sems)
            comm.finish(cin, cout, *sems)

    hbm = pl.BlockSpec(memory_space=pl.ANY)
    outs = pl.pallas_call(
        carrier, name=name, grid=grid, in_specs=in_specs + [hbm] * k, out_specs=out_specs + [hbm] * k,
        out_shape=out_shape + comm.out_shape, scratch_shapes=list(scratch_shapes) + comm.scratch,
        compiler_params=_cparams(tuple("arbitrary" for _ in grid)),
    )(*args, *comm.arrays)
    return list(outs[:n_out]), list(outs[n_out:])


def _exchange(transfers, name, relay=False):
    comm = _Comm(transfers, relay)

    def body(*refs):
        k = comm.n
        comm.start(refs[:k], refs[k:2 * k], *refs[2 * k:])
        comm.pass_on(refs[:k], refs[k:2 * k], *refs[2 * k:])
        comm.finish(refs[:k], refs[k:2 * k], *refs[2 * k:])

    hbm = pl.BlockSpec(memory_space=pl.ANY)
    return pl.pallas_call(body, name=name, out_shape=comm.out_shape, in_specs=[hbm] * comm.n, out_specs=[hbm] * comm.n,
                          scratch_shapes=comm.scratch)(*comm.arrays)


def _pair_reduce(chunks, name, tr=320):
    _, r, c = chunks.shape
    tr = min(tr, r)
    assert r % tr == 0

    def swap(src_ref, dst_ref, send_sems, recv_sems):
        x, y, core = lax.axis_index("x"), lax.axis_index("y"), lax.axis_index("c")
        copies = [pltpu.make_async_remote_copy(
            src_ref=src_ref.at[2 * k + 1 - core], dst_ref=dst_ref.at[k], send_sem=send_sems.at[k],
            recv_sem=recv_sems.at[k], device_id=(x, y, 1 - core), device_id_type=pl.DeviceIdType.MESH) for k in range(4)]
        for cp in copies:
            cp.start()
        for cp in copies:
            cp.wait()

    hbm = pl.BlockSpec(memory_space=pl.ANY)
    theirs = pl.pallas_call(swap, name=name + "_swap", out_shape=jax.ShapeDtypeStruct((4, r, c), chunks.dtype),
                            in_specs=[hbm], out_specs=hbm,
                            scratch_shapes=[pltpu.SemaphoreType.DMA((4,)), pltpu.SemaphoreType.DMA((4,))])(chunks)

    def add(mine_ref, theirs_ref, out_ref):
        core = lax.axis_index("c")
        both = mine_ref[...].astype(F32)
        out_ref[...] = (jnp.where(core == 0, both[0], both[1]) + theirs_ref[...].astype(F32)).astype(out_ref.dtype)

    return pl.pallas_call(
        add, name=name + "_add", grid=(4, r // tr), out_shape=jax.ShapeDtypeStruct((4, r, c), chunks.dtype),
        in_specs=[pl.BlockSpec((None, 2, tr, c), lambda k, i: (k, 0, i, 0)), pl.BlockSpec((None, tr, c), lambda k, i: (k, i, 0))],
        out_specs=pl.BlockSpec((None, tr, c), lambda k, i: (k, i, 0)),
        compiler_params=_cparams(("parallel", "parallel")),
    )(chunks.reshape(4, 2, r, c), theirs)


def _chip_scatter_transfer(pairs):
    return _Transfer(pairs, pairs.shape, lambda ref, p: ref.at[p], lambda ref, s: ref.at[s], same_core=True)


def _matmul(a, b, mode, *, name, tm, tn, tk, epi=None, extra=(), out_dtype=F32, chunks=None, comm=None):
    pieces = list(a) if isinstance(a, (list, tuple)) else [a]
    rows_a, cols_a = pieces[0].shape[0], sum(p.shape[1] for p in pieces)
    if mode == "nn":
        (M, K), N = (rows_a, cols_a), b.shape[1]
    elif mode == "nt":
        (M, K), N = (rows_a, cols_a), b.shape[0]
    else:
        (K, M), N = (rows_a, cols_a), b.shape[1]
    tm, tn, tk = min(tm, M), min(tn, N), min(tk, K)
    assert M % tm == 0 and N % tn == 0 and K % tk == 0 and (epi != "ln" or tn == N), (name, M, N, K)
    nk = K // tk
    tile_cols, axis = (tm, 0) if mode == "tn" else (tk, 2)
    assert all(p.shape[1] % tile_cols == 0 for p in pieces)
    counts = [p.shape[1] // tile_cols for p in pieces]
    starts = [sum(counts[:q]) for q in range(len(pieces))]

    def a_spec_of(q):
        at = lambda t: jnp.clip(t - starts[q], 0, counts[q] - 1) if len(pieces) > 1 else t
        return {"nn": pl.BlockSpec((tm, tk), lambda i, j, k: (i, at(k))),
                "nt": pl.BlockSpec((tm, tk), lambda i, j, k: (i, at(k))),
                "tn": pl.BlockSpec((tk, tm), lambda i, j, k: (k, at(i)))}[mode]

    n_a = len(pieces)
    b_mode = pl.Buffered(1) if (nk == 1 and tn == N and n_a > 1) else None
    b_spec = {"nn": pl.BlockSpec((tk, tn), lambda i, j, k: (k, j), pipeline_mode=b_mode),
              "nt": pl.BlockSpec((tn, tk), lambda i, j, k: (j, k), pipeline_mode=b_mode),
              "tn": pl.BlockSpec((tk, tn), lambda i, j, k: (k, j), pipeline_mode=b_mode)}[mode]
    ca, cb = {"nn": (1, 0), "nt": (1, 1), "tn": (0, 0)}[mode]
    tile = pl.BlockSpec((tm, tn), lambda i, j, k: (i, j))
    row = pl.BlockSpec((1, tn), lambda i, j, k: (0, j))
    n_extra = {None: 0, "add": 1, "relu2": 0, "drelu2": 1, "ln": 3}[epi]
    assert len(extra) == n_extra
    extra_specs = {None: [], "add": [tile], "relu2": [], "drelu2": [tile], "ln": [tile, row, row]}[epi]
    split = 0
    if epi == "relu2":
        out_shape, out_specs = (jax.ShapeDtypeStruct((M, N), BF16),), (tile,)
    elif epi == "ln":
        out_shape = (jax.ShapeDtypeStruct((M, N), F32), jax.ShapeDtypeStruct((M, N), F32),
                     jax.ShapeDtypeStruct((M, N), BF16))
        out_specs = (tile, tile, tile)
    elif chunks == "cols":
        c = N // N_DEV
        out_shape = (jax.ShapeDtypeStruct((N_DEV, M, c), out_dtype),)
        if tn == N:
            split = c
            out_specs = (pl.BlockSpec((N_DEV, tm, c), lambda i, j, k: (0, i, 0)),)
        else:
            assert c % tn == 0
            out_specs = (pl.BlockSpec((None, tm, tn), lambda i, j, k: (j // (c // tn), i, j % (c // tn))),)
    else:
        out_shape, out_specs = (jax.ShapeDtypeStruct((M, N), out_dtype),), (tile,)
    n_out = len(out_shape)

    def body(*refs):
        a_refs, b_ref = refs[:n_a], refs[n_a]
        ex = refs[n_a + 1:n_a + 1 + n_extra]
        outs = refs[n_a + 1 + n_extra:n_a + 1 + n_extra + n_out]
        acc_ref = refs[-1]
        k = pl.program_id(2)

        def finish(acc):
            if epi == "add":
                outs[0][...] = (acc + ALPHA * ex[0][...]).astype(out_dtype)
            elif epi == "relu2":
                r = jnp.maximum(acc, 0.0)
                outs[0][...] = _bf(r * r)
            elif epi == "drelu2":
                outs[0][...] = (acc * (2.0 * jnp.sqrt(ex[0][...].astype(F32)))).astype(out_dtype)
            elif epi == "ln":
                u = ALPHA * ex[0][...] + acc
                xh, _ = _norm_stats(u)
                y = xh * ex[1][...] + ex[2][...]
                outs[0][...] = u
                outs[1][...] = y
                outs[2][...] = _bf(y)
            elif split:
                for p in range(N_DEV):
                    outs[0][p] = acc[:, p * split:(p + 1) * split].astype(out_dtype)
            else:
                outs[0][...] = acc.astype(out_dtype)

        def step(a_ref, first, middle, last):
            part = _dg(_bf(a_ref[...]), _bf(b_ref[...]), ca, cb)
            if nk == 1:
                finish(part)
                return
            if first:
                @pl.when(k == 0)
                def _():
                    acc_ref[...] = part

            if middle:
                @pl.when(jnp.logical_and(k > 0, k < nk - 1))
                def _():
                    acc_ref[...] += part

            if last:
                @pl.when(k == nk - 1)
                def _():
                    finish(acc_ref[...] + part)

        if n_a == 1:
            step(a_refs[0], True, True, True)
        else:
            t = pl.program_id(axis)
            for q in range(n_a):
                along_k = axis == 2
                first = not along_k or starts[q] == 0
                last = not along_k or starts[q] + counts[q] == nk
                middle = not along_k or counts[q] > int(first) + int(last)

                @pl.when(jnp.logical_and(t >= starts[q], t < starts[q] + counts[q]))
                def _(q=q, first=first, middle=middle, last=last):
                    step(a_refs[q], first, middle, last)

    outs, landed = _pcall(
        body, name=name, out_shape=out_shape, grid=(M // tm, N // tn, nk),
        in_specs=[a_spec_of(q) for q in range(n_a)] + [b_spec] + extra_specs, out_specs=out_specs,
        scratch_shapes=[pltpu.VMEM((tm, tn) if nk > 1 else (8, 128), F32)], sem=("parallel", "parallel", "arbitrary"),
        args=(*pieces, b, *extra), comm=comm)
    res = outs[0] if n_out == 1 else tuple(outs)
    if chunks == "rows":
        res = res.reshape(N_DEV, M // N_DEV, N)
    return res if comm is None else (res, landed)


def _matmul_rows_of(pieces, b, res, *, name, tm, comm=None):
    M, (K, N) = pieces[0].shape[0], b.shape
    tm = min(tm, M)
    subs, start = [], 0
    for q, p in enumerate(pieces):
        w = p.shape[1]
        step = w if start % w == 0 else 512
        assert w % step == 0 and start % step == 0
        subs += [(q, off, step, start + off) for off in range(0, w, step)]
        start += w
    assert start == K
    n_p, n_s = len(pieces), len(subs)

    def body(*refs):
        a_refs, b_refs, res_ref, out_ref = refs[:n_p], refs[n_p:n_p + n_s], refs[n_p + n_s], refs[n_p + n_s + 1]
        acc = None
        for (q, off, w, _), b_ref in zip(subs, b_refs):
            part = _dg(_bf(a_refs[q][:, off:off + w]), _bf(b_ref[...]), 1, 0)
            acc = part if acc is None else acc + part
        out_ref[...] = acc + ALPHA * res_ref[...]

    tile = pl.BlockSpec((tm, N), lambda i: (i, 0))
    outs, landed = _pcall(
        body, name=name, grid=(M // tm,), out_shape=[jax.ShapeDtypeStruct((M, N), F32)],
        in_specs=[pl.BlockSpec((tm, p.shape[1]), lambda i: (i, 0)) for p in pieces] +
                 [pl.BlockSpec((w, N), lambda i, r=row // w: (r, 0), pipeline_mode=pl.Buffered(1))
                  for _, _, w, row in subs] + [tile],
        out_specs=[tile], scratch_shapes=[], sem=("parallel",), args=(*pieces, *([b] * n_s), res), comm=comm)
    return outs[0] if comm is None else (outs[0], landed)


def _ret_tables(S):
    half = 64
    inv_freq = ROPE_BASE ** (-jnp.arange(half, dtype=F32) / half)
    ang = jnp.arange(S, dtype=jnp.int32).astype(F32)[:, None] * inv_freq[None, :]
    cos, sin = jnp.cos(ang), jnp.sin(ang)
    cosf = jnp.concatenate([cos, cos], axis=1)
    sinf = jnp.concatenate([-sin, sin], axis=1)
    log_g = jnp.log(1.0 - 2.0 ** (-5.0 - jnp.arange(4, dtype=F32)))
    idx = jnp.arange(CHUNK, dtype=F32)
    diff = idx[:, None] - idx[None, :]
    md = jnp.where(diff[None] >= 0, jnp.exp(log_g[:, None, None] * diff[None]), 0.0)
    kd = jnp.exp(log_g[:, None] * (CHUNK - 1 - idx)[None, :])
    qd = jnp.exp(log_g[:, None] * (idx + 1.0)[None, :])
    cd = jnp.exp(log_g * CHUNK)
    bc = lambda t: jnp.broadcast_to(t[:, :, None], (4, CHUNK, CHUNK))
    return cosf, sinf, md, bc(qd), bc(kd), jnp.broadcast_to(cd[:, None, None], (4, 8, CHUNK))


def _rot(x, cosf, sinf):
    return x * cosf + pltpu.roll(x, 64, 1) * sinf


def _rot_t(dx, cosf, sinf):
    return dx * cosf - pltpu.roll(dx, 64, 1) * sinf


RET_CHUNKS = 2


def _ret_specs(rev, S):
    R = min(RET_CHUNKS, S // CHUNK)
    rows, steps = R * CHUNK, S // (R * CHUNK)
    rn = (lambda n: steps - 1 - n) if rev else (lambda n: n)
    col = lambda c: pl.BlockSpec((rows, 512), lambda n, c=c: (rn(n), c))
    tab = pl.BlockSpec((rows, CHUNK), lambda n: (rn(n), 0))
    dec = pl.BlockSpec((4, CHUNK, CHUNK), lambda n: (0, 0, 0))
    cdec = pl.BlockSpec((4, 8, CHUNK), lambda n: (0, 0, 0))
    vec = pl.BlockSpec((1, 512), lambda n: (0, 0))
    st = pl.BlockSpec((R, 4, CHUNK, CHUNK), lambda n: (rn(n), 0, 0, 0))
    return R, steps, rn, col, tab, dec, cdec, vec, st


def _ret_fwd(proj, tables, gn_g, gn_b):
    S = proj.shape[0]
    R, steps, _, col, tab, dec, cdec, vec, st = _ret_specs(False, S)

    def body(q_ref, k_ref, v_ref, g_ref, cos_ref, sin_ref, md_ref, qd_ref, kd_ref, cd_ref, gng_ref, gnb_ref,
             out_ref, st_ref, state):
        @pl.when(pl.program_id(0) == 0)
        def _():
            state[...] = jnp.zeros_like(state)

        tiles = [(c, h) for c in range(R) for h in range(4)]
        rs = lambda c: slice(c * CHUNK, (c + 1) * CHUNK)
        sl = lambda h: slice(h * 128, (h + 1) * 128)
        qr = [_rot(q_ref[rs(c), sl(h)].astype(F32), cos_ref[rs(c), :], sin_ref[rs(c), :]) for c, h in tiles]
        kr = [_rot(k_ref[rs(c), sl(h)].astype(F32), cos_ref[rs(c), :], sin_ref[rs(c), :]) * (128 ** -0.5)
              for c, h in tiles]
        vb = [_bf(v_ref[rs(c), sl(h)]) for c, h in tiles]
        kv = [_dg(_bf(k * kd_ref[h]), v, 0, 0) for k, v, (c, h) in zip(kr, vb, tiles)]
        before = {}
        for h in range(4):
            s_h = state[h]
            for c in range(R):
                st_ref[c, h] = s_h
                before[(c, h)] = s_h
                s_h = s_h * cd_ref[h, 0:1, :] + kv[c * 4 + h]
            state[h] = s_h
        sc = [_dg(_bf(q), _bf(k), 1, 1) * md_ref[h] for q, k, (c, h) in zip(qr, kr, tiles)]
        r = [_dg(_bf(x), v, 1, 0) + _dg(_bf(q * qd_ref[h]), _bf(before[(c, h)]), 1, 0)
             for x, v, q, (c, h) in zip(sc, vb, qr, tiles)]
        for x, (c, h) in zip(r, tiles):
            y, _ = _norm_stats(x)
            rg = g_ref[rs(c), sl(h)].astype(F32)
            out_ref[rs(c), sl(h)] = rg * _sigmoid(rg) * (y * gng_ref[:, sl(h)] + gnb_ref[:, sl(h)])

    return pl.pallas_call(
        body, name="ret_fwd", grid=(steps,),
        out_shape=(jax.ShapeDtypeStruct((S, RET_W), F32), jax.ShapeDtypeStruct((S // CHUNK, 4, CHUNK, CHUNK), F32)),
        in_specs=[col(0), col(1), col(2), col(3), tab, tab, dec, dec, dec, cdec, vec, vec],
        out_specs=(pl.BlockSpec((R * CHUNK, 512), lambda n: (n, 0)), st),
        scratch_shapes=[pltpu.VMEM((4, CHUNK, CHUNK), F32)],
        compiler_params=_cparams(("arbitrary",)),
    )(proj, proj, proj, proj, *tables, gn_g, gn_b)


def _ret_bwd(proj, tables, gn_g, gn_b, states, d_out):
    S = proj.shape[0]
    R, steps, rn, col, tab, dec, cdec, vec, st = _ret_specs(True, S)

    def kernel_body(q_ref, k_ref, v_ref, g_ref, cos_ref, sin_ref, md_ref, qd_ref, kd_ref, cd_ref, gng_ref, gnb_ref,
                    st_ref, do_ref, dp_ref, dg_ref, db_ref, gstate):
        @pl.when(pl.program_id(0) == 0)
        def _():
            gstate[...] = jnp.zeros_like(gstate)
            dg_ref[...] = jnp.zeros_like(dg_ref)
            db_ref[...] = jnp.zeros_like(db_ref)

        tiles = [(c, h) for c in range(R) for h in range(4)]
        rs = lambda c: slice(c * CHUNK, (c + 1) * CHUNK)
        sl = lambda h: slice(h * 128, (h + 1) * 128)
        rot = lambda ref, c, h: _rot(ref[rs(c), sl(h)].astype(F32), cos_ref[rs(c), :], sin_ref[rs(c), :])
        qr = [rot(q_ref, c, h) for c, h in tiles]
        kr = [rot(k_ref, c, h) * (128 ** -0.5) for c, h in tiles]
        qb, kb = [_bf(x) for x in qr], [_bf(x) for x in kr]
        vb = [_bf(v_ref[rs(c), sl(h)]) for c, h in tiles]
        s0b = [_bf(st_ref[c, h]) for c, h in tiles]
        scb = [_bf(_dg(q, k, 1, 1) * md_ref[h]) for q, k, (c, h) in zip(qb, kb, tiles)]
        qdb = [_bf(q * qd_ref[h]) for q, (c, h) in zip(qr, tiles)]
        kdb = [_bf(k * kd_ref[h]) for k, (c, h) in zip(kr, tiles)]
        r = [_dg(x, v, 1, 0) + _dg(q, s, 1, 0) for x, v, q, s in zip(scb, vb, qdb, s0b)]
        drb, d_rg = [], []
        for x, (c, h) in zip(r, tiles):
            y, rstd = _norm_stats(x)
            gng = gng_ref[:, sl(h)]
            rg = g_ref[rs(c), sl(h)].astype(F32)
            sg = _sigmoid(rg)
            d_o = do_ref[rs(c), sl(h)]
            d_gn = d_o * (rg * sg)
            dg_ref[:, sl(h)] += jnp.sum(d_gn * y, axis=0, keepdims=True)
            db_ref[:, sl(h)] += jnp.sum(d_gn, axis=0, keepdims=True)
            drb.append(_bf(_norm_bwd(d_gn * gng, y, rstd)))
            d_rg.append(_bf(d_o * (y * gng + gnb_ref[:, sl(h)]) * (sg * (1.0 + rg * (1.0 - sg)))))
        grow = [_dg(q, d, 0, 0) for q, d in zip(qdb, drb)]
        after = {}
        for h in range(4):
            g_h = gstate[h]
            for c in reversed(range(R)):
                after[(c, h)] = _bf(g_h)
                g_h = g_h * cd_ref[h, 0:1, :] + grow[c * 4 + h]
            gstate[h] = g_h
        dscb = [_bf(_dg(d, v, 1, 1) * md_ref[h]) for d, v, (c, h) in zip(drb, vb, tiles)]
        for t, (c, h) in enumerate(tiles):
            gb = after[(c, h)]
            dqr = _dg(dscb[t], kb[t], 1, 0) + _dg(drb[t], s0b[t], 1, 1) * qd_ref[h]
            dkr = _dg(dscb[t], qb[t], 0, 0) + _dg(vb[t], gb, 1, 1) * kd_ref[h]
            dv = _dg(scb[t], drb[t], 0, 0) + _dg(kdb[t], gb, 1, 0)
            cosf, sinf = cos_ref[rs(c), :], sin_ref[rs(c), :]
            dp_ref[rs(c), 0 * 512 + h * 128:0 * 512 + (h + 1) * 128] = _bf(_rot_t(dqr, cosf, sinf))
            dp_ref[rs(c), 1 * 512 + h * 128:1 * 512 + (h + 1) * 128] = _bf(_rot_t(dkr, cosf, sinf) * (128 ** -0.5))
            dp_ref[rs(c), 2 * 512 + h * 128:2 * 512 + (h + 1) * 128] = _bf(dv)
            dp_ref[rs(c), 3 * 512 + h * 128:3 * 512 + (h + 1) * 128] = d_rg[t]

    acc = pl.BlockSpec((1, 512), lambda n: (0, 0))
    return pl.pallas_call(
        kernel_body, name="ret_bwd", grid=(steps,),
        out_shape=(jax.ShapeDtypeStruct((S, 2048), BF16), jax.ShapeDtypeStruct((1, 512), F32),
                   jax.ShapeDtypeStruct((1, 512), F32)),
        in_specs=[col(0), col(1), col(2), col(3), tab, tab, dec, dec, dec, cdec, vec, vec, st,
                  pl.BlockSpec((R * CHUNK, 512), lambda n: (rn(n), 0))],
        out_specs=(pl.BlockSpec((R * CHUNK, 2048), lambda n: (rn(n), 0)), acc, acc),
        scratch_shapes=[pltpu.VMEM((4, CHUNK, CHUNK), F32)],
        compiler_params=_cparams(("arbitrary",)),
    )(proj, proj, proj, proj, *tables, gn_g, gn_b, states, d_out)


SB_T = 256
SB_SCALE = 64 ** -0.5
SB_Q_COL, SB_K_COL, SB_V_COL = 2048 // 128, 2560 // 128, 3072 // 128


def _head_masks():
    lane = lax.broadcasted_iota(jnp.int32, (1, 128), 1)
    m0 = (lane < 64).astype(F32)
    return m0, 1.0 - m0


def _tri(n, cmp):
    r = lax.broadcasted_iota(jnp.int32, (n, n), 0)
    c = lax.broadcasted_iota(jnp.int32, (n, n), 1)
    return cmp(r, c)


def _tri_sum(x, tri):
    hi = _bf(x)
    lo = _bf(x - hi.astype(F32))
    return _dg(hi, tri, 1, 0) + _dg(lo, tri, 1, 0)


def _sb_weights(qms, kblks, upper, carry, causal):
    tiles = [(b, h) for b in range(len(kblks)) for h in range(2)]
    zs = [_dg(qms[h], kblks[b], 1, 1) for b, h in tiles]
    lgs = [-(jnp.maximum(z, 0.0) + jnp.log(1.0 + jnp.exp(-jnp.abs(z)))) for z in zs]
    if causal is not None:
        lgs = [jnp.where(causal, lg, 0.0) if b == 0 else lg for lg, (b, h) in zip(lgs, tiles)]
    carries = list(carry)
    for t in range(len(tiles) - 2):
        carries.append(carries[t] + jnp.sum(lgs[t], axis=1, keepdims=True))
    his = [_bf(lg) for lg in lgs]
    los = [_bf(lg - hi.astype(F32)) for lg, hi in zip(lgs, his)]
    later = [_dg(hi, upper, 1, 0) for hi in his]
    later = [r + _dg(lo, upper, 1, 0) for r, lo in zip(later, los)]
    a = [jnp.exp(lg + z + (r + c)) for lg, z, r, c in zip(lgs, zs, later, carries)]
    if causal is not None:
        a = [jnp.where(causal, x, 0.0) if b == 0 else x for x, (b, h) in zip(a, tiles)]
    out = tuple(carries[t] + jnp.sum(lgs[t], axis=1, keepdims=True) for t in (len(tiles) - 2, len(tiles) - 1))
    return [a[2 * b:2 * b + 2] for b in range(len(kblks))], out


def _sb_fwd(proj, comm=None):
    S = proj.shape[0]
    T = min(SB_T, S)
    nq = S // T

    def body(q_ref, k_ref, v_ref, o_ref, a_ref, kb_ref, vm_ref, acc_ref):
        i = pl.program_id(1)
        m0, m1 = _head_masks()

        @pl.when(i == 0)
        def _():
            v = v_ref[...]
            kb_ref[...] = _bf(k_ref[...])
            vm_ref[0] = _bf(v * m0)
            vm_ref[1] = _bf(v * m1)

        q = q_ref[...]
        qm = (_bf(q * (m0 * SB_SCALE)), _bf(q * (m1 * SB_SCALE)))
        upper = _tri(T, lambda r, c: r > c).astype(BF16)
        causal = _tri(T, lambda r, c: c < r)

        def tiles(js, carry, mask, first):
            ks = [pl.multiple_of(j * T, T) for j in js]
            a, out = _sb_weights(qm, [kb_ref[pl.ds(k, T), :] for k in ks], upper, carry, mask)
            a = [[_bf(t) for t in per_block] for per_block in a]
            for b, j in enumerate(js):
                for h in range(2):
                    a_ref[h, j] = a[b][h]
            parts = [_dg(a[b][h], vm_ref[h, pl.ds(k, T), :], 1, 0) for b, k in enumerate(ks) for h in range(2)]
            part = functools.reduce(lambda u, w: u + w, parts)
            if first:
                acc_ref[...] = part
            else:
                acc_ref[...] += part
            return out

        zero = jnp.zeros((T, 1), F32)
        carry = lax.cond(i == 0, lambda: tiles([i], (zero, zero), causal, True),
                         lambda: tiles([i, i - 1], (zero, zero), causal, True))
        n = jnp.maximum(i - 1, 0)
        carry = lax.fori_loop(0, n % 2, lambda _, c: tiles([n - 1], c, None, False), carry)
        top = n - 1 - n % 2
        carry = lax.fori_loop(0, (n // 2) % 2, lambda _, c: tiles([top, top - 1], c, None, False), carry)
        top = top - 2 * ((n // 2) % 2)
        lax.fori_loop(0, n // 4, lambda jj, c: tiles([top - 4 * jj - b for b in range(4)], c, None, False), carry)
        o_ref[...] = acc_ref[...]

    full = lambda c: pl.BlockSpec((S, 128), lambda p, i, c=c: (0, c + p))
    outs, landed = _pcall(
        body, name="sb_fwd", grid=(4, nq),
        out_shape=[jax.ShapeDtypeStruct((S, SB_W), F32), jax.ShapeDtypeStruct((4, 2, nq, nq, T, T), BF16)],
        in_specs=[pl.BlockSpec((T, 128), lambda p, i: (i, SB_Q_COL + p)), full(SB_K_COL), full(SB_V_COL)],
        out_specs=[pl.BlockSpec((T, 128), lambda p, i: (i, p)),
                   pl.BlockSpec((None, 2, None, nq, T, T), lambda p, i: (p, 0, i, 0, 0, 0))],
        scratch_shapes=[pltpu.VMEM((S, 128), BF16), pltpu.VMEM((2, S, 128), BF16), pltpu.VMEM((T, 128), F32)],
        sem=("arbitrary", "arbitrary"), args=(proj, proj, proj), comm=comm)
    return tuple(outs) if comm is None else (tuple(outs), landed)


def _sb_bwd(proj, a_saved, d_o, comm=None):
    S = proj.shape[0]
    T = min(SB_T, S)
    nq = S // T

    def body(q_ref, k_ref, v_ref, do_ref, a_ref, dq_ref, dk_ref, dv_ref, kb_ref, kbm_ref, vb_ref, dq_acc, dk_acc, dv_acc):
        i = pl.program_id(1)
        m0, m1 = _head_masks()

        @pl.when(i == 0)
        def _():
            k = k_ref[...]
            kb_ref[...] = _bf(k)
            kbm_ref[0] = _bf(k * m0)
            kbm_ref[1] = _bf(k * m1)
            vb_ref[...] = _bf(v_ref[...])
            dk_acc[...] = jnp.zeros_like(dk_acc)
            dv_acc[...] = jnp.zeros_like(dv_acc)

        q, d_out = q_ref[...], do_ref[...]
        qm = (_bf(q * (m0 * SB_SCALE)), _bf(q * (m1 * SB_SCALE)))
        dom = (_bf(d_out * m0), _bf(d_out * m1))
        qm_t = tuple(_bf((q * (m * SB_SCALE)).T) for m in (m0, m1))
        dom_t = tuple(_bf((d_out * m).T) for m in (m0, m1))
        lower = _tri(T, lambda r, c: r < c).astype(BF16)
        causal = _tri(T, lambda r, c: c < r)

        def up(js, carry, mask):
            ks = [pl.multiple_of(j * T, T) for j in js]
            tiles = [(b, h) for b in range(len(js)) for h in range(2)]
            zs = [_dg(qm[h], kb_ref[pl.ds(ks[b], T), :], 1, 1) for b, h in tiles]
            a = [a_ref[h, js[b]] for b, h in tiles]
            es = [w.astype(F32) * _dg(dom[h], vb_ref[pl.ds(ks[b], T), :], 1, 1) for w, (b, h) in zip(a, tiles)]
            carries = list(carry)
            for t in range(len(tiles)):
                carries.append(carries[t] + jnp.sum(es[t], axis=1, keepdims=True))
            d_lg = [_dg(_bf(e), lower, 1, 0) + c for e, c in zip(es, carries)]
            ens = [jnp.exp(-jnp.abs(z)) for z in zs]
            invs = [1.0 / (1.0 + en) for en in ens]
            betas = [jnp.where(z >= 0.0, inv, en * inv) for z, en, inv in zip(zs, ens, invs)]
            dzs = [e * (1.0 - b) - d * b for e, b, d in zip(es, betas, d_lg)]
            if mask is not None:
                dzs = [jnp.where(mask, dz, 0.0) if b == len(js) - 1 else dz for dz, (b, h) in zip(dzs, tiles)]
            dzs = [_bf(dz) for dz in dzs]
            parts = [_dg(dzs[t], kbm_ref[h, pl.ds(ks[b], T), :], 1, 0) for t, (b, h) in enumerate(tiles)]
            dq_acc[...] += functools.reduce(lambda u, w: u + w, parts)
            for b, j in enumerate(js):
                dk_acc[j] += _dg(qm_t[0], dzs[2 * b], 1, 0) + _dg(qm_t[1], dzs[2 * b + 1], 1, 0)
                dv_acc[j] += _dg(dom_t[0], a[2 * b], 1, 0) + _dg(dom_t[1], a[2 * b + 1], 1, 0)
            return tuple(carries[-2:])

        zero = jnp.zeros((T, 1), F32)
        dq_acc[...] = jnp.zeros_like(dq_acc)
        n = jnp.maximum(i - 1, 0)
        carry = lax.fori_loop(0, n // 4, lambda jj, c: up([4 * jj + b for b in range(4)], c, None), (zero, zero))
        done = 4 * (n // 4)
        carry = lax.fori_loop(0, (n // 2) % 2, lambda _, c: up([done, done + 1], c, None), carry)
        carry = lax.fori_loop(0, n % 2, lambda _, c: up([n - 1], c, None), carry)

        @pl.when(i == 0)
        def _():
            up([i], carry, causal)

        @pl.when(i > 0)
        def _():
            up([i - 1, i], carry, causal)

        dq_ref[...] = _bf(dq_acc[...] * SB_SCALE)

        @pl.when(i == nq - 1)
        def _():
            for j in range(nq):
                dk_ref[j * T:(j + 1) * T, :] = _bf(dk_acc[j].T)
                dv_ref[j * T:(j + 1) * T, :] = _bf(dv_acc[j].T)

    full = lambda c: pl.BlockSpec((S, 128), lambda p, i, c=c: (0, c + p))
    tile = pl.BlockSpec((T, 128), lambda p, i: (i, p))
    acc = pl.BlockSpec((S, 128), lambda p, i: (0, p))
    out = jax.ShapeDtypeStruct((S, SB_W), BF16)
    outs, landed = _pcall(
        body, name="sb_bwd", grid=(4, nq), out_shape=[out, out, out],
        in_specs=[pl.BlockSpec((T, 128), lambda p, i: (i, SB_Q_COL + p)), full(SB_K_COL), full(SB_V_COL), tile,
                  pl.BlockSpec((None, 2, None, nq, T, T), lambda p, i: (p, 0, i, 0, 0, 0))],
        out_specs=[tile, acc, acc],
        scratch_shapes=[pltpu.VMEM((S, 128), BF16), pltpu.VMEM((2, S, 128), BF16), pltpu.VMEM((S, 128), BF16),
                        pltpu.VMEM((T, 128), F32), pltpu.VMEM((nq, 128, T), F32), pltpu.VMEM((nq, 128, T), F32)],
        sem=("arbitrary", "arbitrary"), args=(proj, proj, proj, d_o, a_saved), comm=comm)
    return tuple(outs) if comm is None else (tuple(outs), landed)


SGU_U_COL, SGU_V_COL = 3584 // 512, 4096 // 512


def _causal(w):
    r = lax.broadcasted_iota(jnp.int32, (CHUNK, CHUNK), 0)
    c = lax.broadcasted_iota(jnp.int32, (CHUNK, CHUNK), 1)
    return jnp.where(r >= c, w, 0.0)


SGU_CHUNKS = 4


def _sgu_fwd(proj, ln_g, ln_b, w, b):
    S = proj.shape[0]
    R = min(SGU_CHUNKS, S // CHUNK)
    rows = R * CHUNK

    def body(u_ref, v_ref, g_ref, b_ref, w_ref, bias_ref, out_ref):
        wc = [_bf(_causal(w_ref[g])) for g in range(4)]
        for r in range(R):
            rs = slice(r * CHUNK, (r + 1) * CHUNK)
            u = _gelu(u_ref[rs, :].astype(F32))
            xh, _ = _norm_stats(_gelu(v_ref[rs, :].astype(F32)))
            vn = _bf(xh * g_ref[...] + b_ref[...])
            for g in range(4):
                sl = slice(g * 128, (g + 1) * 128)
                out_ref[rs, sl] = u[:, sl] * (_dg(wc[g], vn[:, sl], 1, 0) + bias_ref[g])

    vec = pl.BlockSpec((1, 512), lambda n: (0, 0))
    return pl.pallas_call(
        body, name="sgu_fwd", grid=(S // rows,),
        out_shape=jax.ShapeDtypeStruct((S, SGU_W), F32),
        in_specs=[pl.BlockSpec((rows, 512), lambda n: (n, SGU_U_COL)),
                  pl.BlockSpec((rows, 512), lambda n: (n, SGU_V_COL)), vec, vec,
                  pl.BlockSpec((4, CHUNK, CHUNK), lambda n: (0, 0, 0)), pl.BlockSpec((4, CHUNK, 1), lambda n: (0, 0, 0))],
        out_specs=pl.BlockSpec((rows, 512), lambda n: (n, 0)),
        compiler_params=_cparams(("parallel",)),
    )(proj, proj, ln_g, ln_b, w, b)


def _sgu_bwd(proj, ln_g, ln_b, w, b, d_out):
    S = proj.shape[0]
    R = min(SGU_CHUNKS, S // CHUNK)
    rows = R * CHUNK

    def body(u_ref, v_ref, g_ref, b_ref, w_ref, bias_ref, do_ref, dp_ref, dg_ref, db_ref, dw_ref, dbias_ref):
        @pl.when(pl.program_id(0) == 0)
        def _():
            dg_ref[...] = jnp.zeros_like(dg_ref)
            db_ref[...] = jnp.zeros_like(db_ref)
            dw_ref[...] = jnp.zeros_like(dw_ref)
            dbias_ref[...] = jnp.zeros_like(dbias_ref)

        ln_gain = g_ref[...]
        wc = [_bf(_causal(w_ref[g])) for g in range(4)]
        for r in range(R):
            rs = slice(r * CHUNK, (r + 1) * CHUNK)
            gu, gv = u_ref[rs, :].astype(F32), v_ref[rs, :].astype(F32)
            u = _gelu(gu)
            xh, rstd = _norm_stats(_gelu(gv))
            vn = _bf(xh * ln_gain + b_ref[...])
            d_o = do_ref[rs, :]
            d_vn = []
            for g in range(4):
                sl = slice(g * 128, (g + 1) * 128)
                sv = _dg(wc[g], vn[:, sl], 1, 0) + bias_ref[g]
                dp_ref[rs, sl] = _bf(d_o[:, sl] * sv * _gelu_grad(gu[:, sl]))
                d_sv = d_o[:, sl] * u[:, sl]
                dbias_ref[g] += jnp.sum(d_sv, axis=1, keepdims=True)
                d_svb = _bf(d_sv)
                dw_ref[g] += _causal(_dg(d_svb, vn[:, sl], 1, 1))
                d_vn.append(_dg(wc[g], d_svb, 0, 0))
            d_vn = jnp.concatenate(d_vn, axis=1)
            dg_ref[...] += jnp.sum(d_vn * xh, axis=0, keepdims=True)
            db_ref[...] += jnp.sum(d_vn, axis=0, keepdims=True)
            dp_ref[rs, 512:1024] = _bf(_norm_bwd(d_vn * ln_gain, xh, rstd) * _gelu_grad(gv))

    vec = pl.BlockSpec((1, 512), lambda n: (0, 0))
    wspec = pl.BlockSpec((4, CHUNK, CHUNK), lambda n: (0, 0, 0))
    bspec = pl.BlockSpec((4, CHUNK, 1), lambda n: (0, 0, 0))
    return pl.pallas_call(
        body, name="sgu_bwd", grid=(S // rows,),
        out_shape=(jax.ShapeDtypeStruct((S, 1024), BF16), jax.ShapeDtypeStruct((1, 512), F32),
                   jax.ShapeDtypeStruct((1, 512), F32), jax.ShapeDtypeStruct((4, CHUNK, CHUNK), F32),
                   jax.ShapeDtypeStruct((4, CHUNK, 1), F32)),
        in_specs=[pl.BlockSpec((rows, 512), lambda n: (n, SGU_U_COL)),
                  pl.BlockSpec((rows, 512), lambda n: (n, SGU_V_COL)), vec, vec, wspec, bspec,
                  pl.BlockSpec((rows, 512), lambda n: (n, 0))],
        out_specs=(pl.BlockSpec((rows, 1024), lambda n: (n, 0)), vec, vec, wspec, bspec),
        compiler_params=_cparams(("arbitrary",)),
    )(proj, proj, ln_g, ln_b, w, b, d_out)


GATE_COL = 4608 // 512


def _merge_fwd(proj, branches, p_list, tm=1024):
    S = proj.shape[0]
    tm = min(tm, S)

    def body(r_ref, s_ref, g_ref, pr_ref, ps_ref, pg_ref, gr_ref, gs_ref, gg_ref, m_ref, br_ref):
        acc = None
        for k, (x_ref, p_ref, gate_ref) in enumerate(((r_ref, pr_ref, gr_ref), (s_ref, ps_ref, gs_ref),
                                                      (g_ref, pg_ref, gg_ref))):
            br = _dg(_bf(x_ref[...]), _bf(p_ref[...]), 1, 0)
            br_ref[k] = _bf(br)
            term = _sigmoid(gate_ref[...].astype(F32)) * br
            acc = term if acc is None else acc + term
        m_ref[...] = _bf(acc)

    xs = pl.BlockSpec((tm, 512), lambda i, n: (i, 0))
    ps = pl.BlockSpec((512, 512), lambda i, n: (0, n))
    gate = lambda k: pl.BlockSpec((tm, 512), lambda i, n, k=k: (i, GATE_COL + 2 * k + n))
    return pl.pallas_call(
        body, name="merge_fwd", grid=(S // tm, 2),
        out_shape=(jax.ShapeDtypeStruct((S, D_MODEL), BF16), jax.ShapeDtypeStruct((3, S, D_MODEL), BF16)),
        in_specs=[xs, xs, xs, ps, ps, ps, gate(0), gate(1), gate(2)],
        out_specs=(pl.BlockSpec((tm, 512), lambda i, n: (i, n)), pl.BlockSpec((3, tm, 512), lambda i, n: (0, i, n))),
        compiler_params=_cparams(("parallel", "parallel")),
    )(*branches, *p_list, proj, proj, proj)


def _gate_bwd(proj, br, d_merged, tm=1024):
    S = proj.shape[0]
    tm = min(tm, S)

    def body(dm_ref, br_ref, gr_ref, gs_ref, gg_ref, *out_refs):
        dm = dm_ref[...]
        for k, gate_ref in enumerate((gr_ref, gs_ref, gg_ref)):
            s = _sigmoid(gate_ref[...].astype(F32))
            out_refs[k][...] = _bf(dm * s)
            out_refs[3 + k][...] = _bf(dm * br_ref[k].astype(F32) * (s * (1.0 - s)))

    gate = lambda k: pl.BlockSpec((tm, 512), lambda i, n, k=k: (i, GATE_COL + 2 * k + n))
    three = pl.BlockSpec((3, tm, 512), lambda i, n: (0, i, n))
    tile = pl.BlockSpec((tm, 512), lambda i, n: (i, n))
    outs = pl.pallas_call(
        body, name="gate_bwd", grid=(S // tm, 2),
        out_shape=[jax.ShapeDtypeStruct((S, D_MODEL), BF16)] * 6,
        in_specs=[tile, three, gate(0), gate(1), gate(2)], out_specs=[tile] * 6,
        compiler_params=_cparams(("parallel", "parallel")),
    )(d_merged, br, proj, proj, proj)
    return outs[:3], outs[3:]


def _ln_bwd(dy, u, g, target=None, tm=512):
    S, D = u.shape
    tm = min(tm, S)
    loss = target is not None

    def body(*refs):
        dy_ref, u_ref, g_ref = refs[:3]
        du_ref, dub_ref, dg_ref, db_ref = refs[3 + loss:7 + loss]

        @pl.when(pl.program_id(0) == 0)
        def _():
            for acc_ref in refs[5 + loss:]:
                acc_ref[...] = jnp.zeros_like(acc_ref)

        dy_t = dy_ref[...]
        if loss:
            err = dy_t - refs[3][...]
            refs[-1][...] += jnp.sum(err * err, axis=0, keepdims=True)
            dy_t = err * (1.0 / D)
        xh, rstd = _norm_stats(u_ref[...])
        dg_ref[...] += jnp.sum(dy_t * xh, axis=0, keepdims=True)
        db_ref[...] += jnp.sum(dy_t, axis=0, keepdims=True)
        du = _norm_bwd(dy_t * g_ref[...], xh, rstd)
        du_ref[...] = du
        dub_ref[...] = _bf(du)

    tile = pl.BlockSpec((tm, D), lambda i: (i, 0))
    vec = pl.BlockSpec((1, D), lambda i: (0, 0))
    row = jax.ShapeDtypeStruct((1, D), F32)
    return pl.pallas_call(
        body, name="ln_bwd", grid=(S // tm,),
        out_shape=[jax.ShapeDtypeStruct((S, D), F32), jax.ShapeDtypeStruct((S, D), BF16)] + [row] * (2 + loss),
        in_specs=[tile, tile, vec] + [tile] * loss, out_specs=[tile, tile] + [vec] * (2 + loss),
        compiler_params=_cparams(("arbitrary",)),
    )(dy, u, g, *([target] if loss else []))


def _layer_fwd(x, x_bf, W, tables, sb_comm=None):
    proj = _matmul(x_bf, W["w_in_t"], "nt", name="proj", tm=1024, tn=2560, tk=1024)
    retg, states = _ret_fwd(proj, tables, W["ret_gn_g"], W["ret_gn_b"])
    if sb_comm is None:
        sb, sb_a = _sb_fwd(proj)
    else:
        (sb, sb_a), landed = _sb_fwd(proj, comm=sb_comm[0])
        sb_comm[1](landed)
    sg = _sgu_fwd(proj, W["sgu_ln_g"], W["sgu_ln_b"], W["sgu_w"], W["sgu_b"])
    merged, br = _merge_fwd(proj, (retg, sb, sg), (W["p_ret"], W["p_sb"], W["p_sgu"]))
    u1, x1, x1_bf = _matmul(merged, W["w_out"], "nn", name="out_ln", tm=512, tn=1024, tk=1024, epi="ln",
                            extra=(x, W["ln1_g"], W["ln1_b"]))
    act = _matmul(x1_bf, W["w_up"], "nn", name="up", tm=1024, tn=4096, tk=1024, epi="relu2")
    u2, x2, x2_bf = _matmul(act, W["w_down"], "nn", name="down_ln", tm=512, tn=1024, tk=4096, epi="ln",
                            extra=(x1, W["ln2_g"], W["ln2_b"]))
    saved = dict(x_bf=x_bf, proj=proj, retg=retg, states=states, sb=sb, sb_a=sb_a, sg=sg, merged=merged, br=br, u1=u1,
                 x1_bf=x1_bf, act=act, u2=u2)
    return x2, x2_bf, saved


def _layer_bwd(d_x2, W, tables, sv, chunk_dtype=None, sb_comm_fn=None, dwin_comm_fn=None, dx_comm_fn=None, target=None):
    dt = F32 if chunk_dtype is None else chunk_dtype
    rows, cols = (None, None) if chunk_dtype is None else ("rows", "cols")
    g, landed = {}, {}
    du2, du2_bf, g["ln2_g"], g["ln2_b"], *sq = _ln_bwd(d_x2, sv["u2"], W["ln2_g"], target=target)
    if sq:
        landed["sq"] = sq[0]
    d_hpre = _matmul(du2_bf, W["w_down"], "nt", name="d_act", tm=1024, tn=2048, tk=1024, epi="drelu2",
                     extra=(sv["act"],), out_dtype=BF16)
    g["w_down"] = _matmul(sv["act"], du2_bf, "tn", name="dw_down", tm=512, tn=1024, tk=4096, out_dtype=dt, chunks=rows)
    g["w_up"] = _matmul(sv["x1_bf"], d_hpre, "tn", name="dw_up", tm=1024, tn=512, tk=4096, out_dtype=dt, chunks=cols)
    d_x1 = _matmul(d_hpre, W["w_up"], "nt", name="d_x1", tm=512, tn=1024, tk=4096, epi="add", extra=(du2,))
    du1, du1_bf, g["ln1_g"], g["ln1_b"] = _ln_bwd(d_x1, sv["u1"], W["ln1_g"])
    d_merged = _matmul(du1_bf, W["w_out"], "nt", name="d_merged", tm=1024, tn=1024, tk=1024)
    g["w_out"] = _matmul(sv["merged"], du1_bf, "tn", name="dw_out", tm=1024, tn=512, tk=4096, out_dtype=dt, chunks=rows)
    d_br, d_gate = _gate_bwd(sv["proj"], sv["br"], d_merged)
    d_branch = []
    for k, (nm, act) in enumerate((("p_ret", sv["retg"]), ("p_sb", sv["sb"]), ("p_sgu", sv["sg"]))):
        d_branch.append(_matmul(d_br[k], W[nm], "nt", name="d_" + nm[2:], tm=1024, tn=512, tk=1024))
        g[nm] = _matmul(act, d_br[k], "tn", name="dw_" + nm[2:], tm=512, tn=1024, tk=2048, out_dtype=dt, chunks=cols)
    d_ret, g["ret_gn_g"], g["ret_gn_b"] = _ret_bwd(sv["proj"], tables, W["ret_gn_g"], W["ret_gn_b"], sv["states"],
                                                   d_branch[0])
    if sb_comm_fn is None:
        d_sq, d_sk, d_sv = _sb_bwd(sv["proj"], sv["sb_a"], d_branch[1])
    else:
        (d_sq, d_sk, d_sv), landed["sb"] = _sb_bwd(sv["proj"], sv["sb_a"], d_branch[1], comm=sb_comm_fn(g))
    d_sgu, g["sgu_ln_g"], g["sgu_ln_b"], g["sgu_w"], g["sgu_b"] = _sgu_bwd(
        sv["proj"], W["sgu_ln_g"], W["sgu_ln_b"], W["sgu_w"], W["sgu_b"], d_branch[2])
    d_proj = [d_ret, d_sq, d_sk, d_sv, d_sgu, d_gate[0], d_gate[1], d_gate[2]]
    g["w_in"] = _matmul(d_proj, sv["x_bf"], "tn", name="dw_in", tm=256, tn=1024, tk=4096, out_dtype=dt, chunks=rows,
                        comm=None if dwin_comm_fn is None else dwin_comm_fn(g))
    if dwin_comm_fn is not None:
        g["w_in"], landed["dwin"] = g["w_in"]
    if chunk_dtype is None:
        g["w_in"] = g["w_in"].T
    d_x = _matmul_rows_of(d_proj, W["w_in_t"], du1, name="d_x", tm=512,
                          comm=None if dx_comm_fn is None else dx_comm_fn(g))
    if dx_comm_fn is not None:
        d_x, landed["dx"] = d_x
    return d_x, g, landed


BIG = ("w_in", "p_ret", "p_sb", "p_sgu", "w_out", "w_up", "w_down")
SMALL = ("ret_gn_g", "ret_gn_b", "sgu_ln_g", "sgu_ln_b", "sgu_w", "sgu_b", "ln1_g", "ln1_b", "ln2_g", "ln2_b")
GATHER_KIND = {"w_in": "rows", "p_ret": "cols", "p_sb": "cols", "p_sgu": "cols", "w_out": "rows", "w_up": "cols",
               "w_down": "rows"}


def _small_weights(small, l):
    W = {}
    for n in SMALL:
        if n == "sgu_w":
            W[n] = small[n][l]
        elif n == "sgu_b":
            W[n] = small[n][l].reshape(4, CHUNK, 1)
        else:
            W[n] = small[n][l].reshape(1, -1)
    return W


def _local_step(x, target, full, small):
    tables = _ret_tables(x.shape[0])
    Ws = [{**{n: full[n][l] for n in BIG[1:]}, "w_in_t": full["w_in"][l].T, **_small_weights(small, l)}
          for l in range(DEPTH)]
    saved = []
    h, h_bf = x, _bf(x)
    for l in range(DEPTH):
        h, h_bf, sv = _layer_fwd(h, h_bf, Ws[l], tables)
        saved.append(sv)
    grads = [None] * DEPTH
    d_h, grads[-1], landed = _layer_bwd(h, Ws[-1], tables, saved[-1], target=target)
    for l in reversed(range(DEPTH - 1)):
        d_h, grads[l], _ = _layer_bwd(d_h, Ws[l], tables, saved[l])
    return landed["sq"], d_h, grads


def _adam(w, parts, m, v, name):
    L, R, C = w.shape
    tr = next(t for t in (320, 256, 128) if R % t == 0)
    assert len(parts) == L

    def body(*refs):
        w_ref, p_refs, (m_ref, v_ref, g_ref, d_ref, nm_ref, nv_ref) = refs[0], refs[1:1 + L], refs[1 + L:]
        layer = pl.program_id(0)
        g = None
        for li, p_ref in enumerate(p_refs):
            s = p_ref[0].astype(F32)
            for j in range(1, p_ref.shape[0]):
                s = s + p_ref[j].astype(F32)
            g = s if g is None else jnp.where(layer == li, s, g)
        g_ref[...] = g
        d_ref[...], nm_ref[...], nv_ref[...] = _adam_update(w_ref[...], g, m_ref[...], v_ref[...])

    tile = pl.BlockSpec((None, tr, C), lambda l, i: (l, i, 0))
    part = lambda li: pl.BlockSpec((parts[li].shape[0], tr, C), lambda l, i, li=li: (0, jnp.where(l == li, i, 0), 0))
    out = jax.ShapeDtypeStruct((L, R, C), F32)
    return pl.pallas_call(
        body, name=name, grid=(L, R // tr), out_shape=(out, out, out, out),
        in_specs=[tile] + [part(li) for li in range(L)] + [tile, tile],
        out_specs=(tile, tile, tile, tile),
        compiler_params=_cparams(("parallel", "parallel")),
    )(w, *parts, m, v)


def _adam_update(w, g, m, v):
    m2 = ADAM_B1 * m + (1.0 - ADAM_B1) * g
    v2 = ADAM_B2 * v + (1.0 - ADAM_B2) * (g * g)
    m_hat = m2 / (1.0 - ADAM_B1 ** ADAM_STEP)
    v_hat = v2 / (1.0 - ADAM_B2 ** ADAM_STEP)
    return -ADAM_LR * (m_hat / (jnp.sqrt(v_hat) + ADAM_EPS) + ADAM_WD * w), m2, v2


def _adam_small(w, m, v, parts):
    k = len(SMALL)

    def body(*refs):
        w_refs, m_refs, v_refs, p_refs, outs = refs[:k], refs[k:2 * k], refs[2 * k:3 * k], refs[3 * k:5 * k], refs[5 * k:]
        for i in range(k):
            vector = len(w_refs[i].shape) == 2
            for l in range(DEPTH):
                p_ref = p_refs[DEPTH * i + l]
                g = p_ref[0]
                for j in range(1, N_DEV):
                    g = g + p_ref[j]
                at = (slice(l, l + 1), slice(None)) if vector else (l,)
                delta, m2, v2 = _adam_update(w_refs[i][at], g, m_refs[i][at], v_refs[i][at])
                for o_ref, val in zip(outs[4 * i:4 * i + 4], (g, delta, m2, v2)):
                    o_ref[at] = val

    vmem = pl.BlockSpec(memory_space=pltpu.VMEM)
    args = [w[n] for n in SMALL] + [m[n] for n in SMALL] + [v[n] for n in SMALL] + \
           [parts[(n, l)] for n in SMALL for l in range(DEPTH)]
    out_shape = [jax.ShapeDtypeStruct(w[n].shape, F32) for n in SMALL for _ in range(4)]
    outs = pl.pallas_call(body, name="adam_small", out_shape=out_shape, in_specs=[vmem] * len(args),
                          out_specs=[vmem] * len(out_shape), compiler_params=_cparams())(*args)
    return {n: tuple(outs[4 * i:4 * i + 4]) for i, n in enumerate(SMALL)}


WEIGHTS = ("w_in", "ret_gn_g", "ret_gn_b", "sgu_ln_g", "sgu_ln_b", "sgu_w", "sgu_b", "p_ret", "p_sb", "p_sgu", "w_out",
           "ln1_g", "ln1_b", "w_up", "w_down", "ln2_g", "ln2_b")


def kernel(x, w_in, ret_gn_g, ret_gn_b, sgu_ln_g, sgu_ln_b, sgu_w, sgu_b, p_ret, p_sb, p_sgu, w_out, ln1_g, ln1_b, w_up, w_down, ln2_g, ln2_b, loss_target, m_w_in, m_ret_gn_g, m_ret_gn_b, m_sgu_ln_g, m_sgu_ln_b, m_sgu_w, m_sgu_b, m_p_ret, m_p_sb, m_p_sgu, m_w_out, m_ln1_g, m_ln1_b, m_w_up, m_w_down, m_ln2_g, m_ln2_b, v_w_in, v_ret_gn_g, v_ret_gn_b, v_sgu_ln_g, v_sgu_ln_b, v_sgu_w, v_sgu_b, v_p_ret, v_p_sb, v_p_sgu, v_w_out, v_ln1_g, v_ln1_b, v_w_up, v_w_down, v_ln2_g, v_ln2_b):
    w = dict(zip(WEIGHTS, (w_in, ret_gn_g, ret_gn_b, sgu_ln_g, sgu_ln_b, sgu_w, sgu_b, p_ret, p_sb, p_sgu, w_out,
                           ln1_g, ln1_b, w_up, w_down, ln2_g, ln2_b)))
    m = dict(zip(WEIGHTS, (m_w_in, m_ret_gn_g, m_ret_gn_b, m_sgu_ln_g, m_sgu_ln_b, m_sgu_w, m_sgu_b, m_p_ret, m_p_sb,
                           m_p_sgu, m_w_out, m_ln1_g, m_ln1_b, m_w_up, m_w_down, m_ln2_g, m_ln2_b)))
    v = dict(zip(WEIGHTS, (v_w_in, v_ret_gn_g, v_ret_gn_b, v_sgu_ln_g, v_sgu_ln_b, v_sgu_w, v_sgu_b, v_p_ret, v_p_sb,
                           v_p_sgu, v_w_out, v_ln1_g, v_ln1_b, v_w_up, v_w_down, v_ln2_g, v_ln2_b)))

    small = {n: w[n] for n in SMALL}
    shard = {n: _bf(w[n]) for n in BIG}
    shard["w_in"] = shard["w_in"].transpose(0, 2, 1)
    S = x.shape[1]
    x0, target = x.reshape(S, D_MODEL), loss_target.reshape(S, D_MODEL)
    tables = _ret_tables(S)
    Ws = [_small_weights(small, l) for l in range(DEPTH)]

    (Ws[0]["w_in_t"],) = _exchange([_gather_transfer(shard["w_in"], 0, "rows")], "gather_w_in0", relay=True)
    def gather_under_sb(keys):
        def landed_fn(landed):
            for (n, l), z in zip(keys, landed):
                Ws[l]["w_in_t" if n == "w_in" else n] = z

        return _Comm([_gather_transfer(shard[n], l, GATHER_KIND[n]) for n, l in keys], relay=True), landed_fn

    h, h_bf, saved0 = _layer_fwd(x0, _bf(x0), Ws[0], tables,
                                 sb_comm=gather_under_sb([(n, 0) for n in BIG[1:]] + [("w_in", 1)]))
    h, _, saved1 = _layer_fwd(h, h_bf, Ws[1], tables, sb_comm=gather_under_sb([(n, 1) for n in BIG[1:]]))
    d_h, g1, landed1 = _layer_bwd(h, Ws[1], tables, saved1, chunk_dtype=BF16, target=target,
                                  sb_comm_fn=lambda g: _Comm([_scatter_transfer(g[n]) for n in BIG[1:]]))
    loss = lax.psum(0.5 * jnp.sum(landed1["sq"]) / D_MODEL, ("x", "y", "c"))
    early = [("w_in", 1)] + [(n, 0) for n in BIG[1:]]

    def small_slabs(g):
        return [_slab_transfer(g[n].reshape(4, CHUNK) if n == "sgu_b" else g[n]) for n in SMALL]

    def early_scatter(g0):
        return _Comm([_scatter_transfer((g1 if l else g0)[n]) for n, l in early] + small_slabs(g1))

    def late_scatter(g0):
        pairs = _pair_reduce(g0["w_in"], "w_in0_pairs")
        return _Comm([_chip_scatter_transfer(pairs)])

    d_x, g0, landed = _layer_bwd(d_h, Ws[0], tables, saved0, chunk_dtype=BF16, sb_comm_fn=early_scatter,
                                 dwin_comm_fn=lambda g: _Comm(small_slabs(g)), dx_comm_fn=late_scatter)
    parts = {**dict(zip(early, landed["sb"])), **{(n, 1): z for n, z in zip(BIG[1:], landed1["sb"])}}
    parts[("w_in", 0)] = landed["dx"][0]
    small_parts = {**{(n, 1): z for n, z in zip(SMALL, landed["sb"][len(early):])},
                   **{(n, 0): z for n, z in zip(SMALL, landed["dwin"])}}

    grad, delta, new_m, new_v = {}, {}, {}, {}
    for n in BIG:
        view = (lambda a: a.transpose(0, 2, 1)) if n == "w_in" else (lambda a: a)
        res = _adam(view(w[n]), [parts[(n, l)] for l in range(DEPTH)], view(m[n]), view(v[n]), "adam_" + n)
        grad[n], delta[n], new_m[n], new_v[n] = (view(r) for r in res)
    for n, res in _adam_small(small, m, v, small_parts).items():
        grad[n], delta[n], new_m[n], new_v[n] = res

    return (loss, d_x.reshape(x.shape), *[grad[n] for n in WEIGHTS], *[delta[n] for n in WEIGHTS],
            *[new_m[n] for n in WEIGHTS], *[new_v[n] for n in WEIGHTS])
```

```python
import functools
import math

import numpy as np
import jax
import jax.numpy as jnp
from jax import lax
from jax.experimental import pallas as pl
from jax.experimental.pallas import tpu as pltpu

F32 = jnp.float32
BF16 = jnp.bfloat16

N_DEV = 8
DEPTH = 2
D_MODEL = 1024
CHUNK = 128
RET_W = 512
SB_W = 512
SGU_W = 512
N_IN = 7680
LN_EPS = 1e-5
ALPHA = (2 * DEPTH) ** 0.25
ROPE_BASE = 10000.0
ADAM_LR, ADAM_B1, ADAM_B2, ADAM_EPS, ADAM_WD, ADAM_STEP = 0.001, 0.9, 0.999, 1e-08, 0.01, 10
VMEM_LIMIT = 56 * 1024 * 1024

_GELU_K = math.sqrt(2.0 / math.pi)
_GELU_C = 0.044715


def _cparams(sem=None):
    return pltpu.CompilerParams(dimension_semantics=sem, vmem_limit_bytes=VMEM_LIMIT)


def _dg(a, b, ca, cb):
    return lax.dot_general(a, b, (((ca,), (cb,)), ((), ())), preferred_element_type=F32)


def _bf(x):
    return x.astype(BF16)


def _sigmoid(x):
    return 1.0 / (1.0 + jnp.exp(-x))


def _gelu(x):
    t = jnp.tanh(_GELU_K * (x + _GELU_C * (x * x * x)))
    return x * (0.5 * (1.0 + t))


def _gelu_grad(x):
    t = jnp.tanh(_GELU_K * (x + _GELU_C * (x * x * x)))
    return 0.5 * (1.0 + t) + 0.5 * x * (1.0 - t * t) * (_GELU_K * (1.0 + 3.0 * _GELU_C * x * x))


def _norm_stats(u):
    mu = jnp.mean(u, axis=-1, keepdims=True)
    d = u - mu
    var = jnp.mean(d * d, axis=-1, keepdims=True)
    rstd = lax.rsqrt(var + LN_EPS)
    return d * rstd, rstd


def _norm_bwd(dxh, xh, rstd):
    return rstd * (dxh - jnp.mean(dxh, axis=-1, keepdims=True) - xh * jnp.mean(dxh * xh, axis=-1, keepdims=True))


class _Transfer:
    def __init__(self, src, dst_shape, src_at, dst_at, same_core=False):
        self.src, self.dst_shape, self.src_at, self.dst_at = src, tuple(dst_shape), src_at, dst_at
        self.same_core = same_core


def _gather_transfer(shard, l, kind):
    _, r, c = shard.shape
    src_at = lambda ref, p: ref.at[l]
    if kind == "slab":
        return _Transfer(shard, (N_DEV, r, c), src_at, lambda ref, s: ref.at[s])
    if kind == "rows":
        return _Transfer(shard, (N_DEV * r, c), src_at, lambda ref, s: ref.at[pl.ds(pl.multiple_of(s * r, r), r), :])
    return _Transfer(shard, (r, N_DEV * c), src_at, lambda ref, s: ref.at[:, pl.ds(pl.multiple_of(s * c, c), c)])


def _scatter_transfer(chunks):
    return _Transfer(chunks, chunks.shape, lambda ref, p: ref.at[p], lambda ref, s: ref.at[s])


def _slab_transfer(arr):
    return _Transfer(arr, (N_DEV,) + arr.shape, lambda ref, p: ref, lambda ref, s: ref.at[s])


class _Comm:
    def __init__(self, transfers, relay=False):
        self.transfers = list(transfers)
        self.relay = relay
        self.n = len(self.transfers)
        self.arrays = [t.src for t in self.transfers]
        self.out_shape = [jax.ShapeDtypeStruct(t.dst_shape, t.src.dtype) for t in self.transfers]
        self.scratch = [pltpu.SemaphoreType.DMA((self.n * (N_DEV - 1),)), pltpu.SemaphoreType.DMA((self.n * (N_DEV - 1),)),
                        pltpu.SemaphoreType.DMA((self.n,))]

    def _relay_copies(self, srcs, dsts, send_sems, recv_sems, local_sems):
        x, y, c = lax.axis_index("x"), lax.axis_index("y"), lax.axis_index("c")
        me = 4 * x + 2 * y + c
        chips = [(1 - x, y), (x, 1 - y), (1 - x, 1 - y)]
        first, passed, own = [], [], []
        for t, tr in enumerate(self.transfers):
            def copy(k, src, sender, to, t=t, tr=tr):
                return pltpu.make_async_remote_copy(
                    src_ref=src, dst_ref=tr.dst_at(dsts[t], sender), send_sem=send_sems.at[t * (N_DEV - 1) + k],
                    recv_sem=recv_sems.at[t * (N_DEV - 1) + k], device_id=to, device_id_type=pl.DeviceIdType.MESH)

            mine = tr.src_at(srcs[t], me)
            first.append([copy(0, mine, me, (x, y, 1 - c))] + [copy(1 + j, mine, me, (px, py, c))
                                                                for j, (px, py) in enumerate(chips)])
            passed.append([copy(4 + j, tr.dst_at(dsts[t], 4 * px + 2 * py + c), 4 * px + 2 * py + c, (x, y, 1 - c))
                           for j, (px, py) in enumerate(chips)])
            own.append(pltpu.make_async_copy(mine, tr.dst_at(dsts[t], me), local_sems.at[t]))
        return first, passed, own

    def _copies(self, srcs, dsts, send_sems, recv_sems, local_sems):
        x, y, c = lax.axis_index("x"), lax.axis_index("y"), lax.axis_index("c")
        me = 4 * x + 2 * y + c
        copies = []
        for d in range(1, N_DEV):
            px = 1 - x if d & 4 else x
            py = 1 - y if d & 2 else y
            pc = 1 - c if d & 1 else c
            for t, tr in enumerate(self.transfers):
                if tr.same_core and d & 1:
                    continue
                peer, mine = (2 * px + py, 2 * x + y) if tr.same_core else (4 * px + 2 * py + pc, me)
                k = t * (N_DEV - 1) + d - 1
                copies.append(pltpu.make_async_remote_copy(
                    src_ref=tr.src_at(srcs[t], peer), dst_ref=tr.dst_at(dsts[t], mine),
                    send_sem=send_sems.at[k], recv_sem=recv_sems.at[k],
                    device_id=(px, py, pc), device_id_type=pl.DeviceIdType.MESH))
        own = []
        for t, tr in enumerate(self.transfers):
            mine = 2 * x + y if tr.same_core else me
            own.append(pltpu.make_async_copy(tr.src_at(srcs[t], mine), tr.dst_at(dsts[t], mine), local_sems.at[t]))
        return copies, own

    def start(self, srcs, dsts, *sems):
        if self.relay:
            first, _, own = self._relay_copies(srcs, dsts, *sems)
            for cp in own + [cp for per_t in first for cp in per_t]:
                cp.start()
            return
        copies, own = self._copies(srcs, dsts, *sems)
        for cp in own + copies:
            cp.start()

    def pass_on(self, srcs, dsts, *sems):
        if self.relay:
            first, passed, _ = self._relay_copies(srcs, dsts, *sems)
            for j in range(3):
                for t in range(self.n):
                    first[t][1 + j].wait_recv()
                    passed[t][j].start()

    def finish(self, srcs, dsts, *sems):
        if self.relay:
            first, passed, own = self._relay_copies(srcs, dsts, *sems)
            for t in range(self.n):
                first[t][0].wait_recv()
                for cp in passed[t]:
                    cp.wait_recv()
            for t in range(self.n):
                for cp in first[t] + passed[t]:
                    cp.wait_send()
                own[t].wait()
            return
        copies, own = self._copies(srcs, dsts, *sems)
        for cp in copies + own:
            cp.wait()


def _pcall(body, *, name, grid, in_specs, out_specs, out_shape, scratch_shapes, sem, args, comm=None):
    in_specs, out_specs, out_shape = list(in_specs), list(out_specs), list(out_shape)
    if comm is None:
        outs = pl.pallas_call(body, name=name, grid=grid, in_specs=in_specs, out_specs=out_specs, out_shape=out_shape,
                              scratch_shapes=list(scratch_shapes), compiler_params=_cparams(sem))(*args)
        return list(outs), []
    n_in, n_out, n_scr, k = len(in_specs), len(out_specs), len(scratch_shapes), comm.n

    def carrier(*refs):
        ins, cin = refs[:n_in], refs[n_in:n_in + k]
        outs, cout = refs[n_in + k:n_in + k + n_out], refs[n_in + k + n_out:n_in + 2 * k + n_out]
        scr, sems = refs[n_in + 2 * k + n_out:n_in + 2 * k + n_out + n_scr], refs[n_in + 2 * k + n_out + n_scr:]
        ids = [pl.program_id(d) for d in range(len(grid))]
        first = functools.reduce(jnp.logical_and, [i == 0 for i in ids])
        last = functools.reduce(jnp.logical_and, [i == g - 1 for i, g in zip(ids, grid)])

        @pl.when(first)
        def _():
            comm.start(cin, cout, *sems)

        body(*ins, *outs, *scr)

        early_pass = comm.relay and len(grid) > 1 and grid[0] > 1
        if early_pass:
            @pl.when(functools.reduce(jnp.logical_and, [ids[0] == grid[0] - 1] + [i == 0 for i in ids[1:]]))
            def _():
                comm.pass_on(cin, cout, *sems)

        @pl.when(last)
        def _():
            if not early_pass:
                comm.pass_on(cin, cout, *sems)
            comm.finish(cin, cout, *sems)

    hbm = pl.BlockSpec(memory_space=pl.ANY)
    outs = pl.pallas_call(
        carrier, name=name, grid=grid, in_specs=in_specs + [hbm] * k, out_specs=out_specs + [hbm] * k,
        out_shape=out_shape + comm.out_shape, scratch_shapes=list(scratch_shapes) + comm.scratch,
        compiler_params=_cparams(tuple("arbitrary" for _ in grid)),
    )(*args, *comm.arrays)
    return list(outs[:n_out]), list(outs[n_out:])


def _exchange(transfers, name, relay=False):
    comm = _Comm(transfers, relay)

    def body(*refs):
        k = comm.n
        comm.start(refs[:k], refs[k:2 * k], *refs[2 * k:])
        comm.pass_on(refs[:k], refs[k:2 * k], *refs[2 * k:])
        comm.finish(refs[:k], refs[k:2 * k], *refs[2 * k:])

    hbm = pl.BlockSpec(memory_space=pl.ANY)
    return pl.pallas_call(body, name=name, out_shape=comm.out_shape, in_specs=[hbm] * comm.n, out_specs=[hbm] * comm.n,
                          scratch_shapes=comm.scratch)(*comm.arrays)


def _pair_reduce(chunks, name, tr=320):
    _, r, c = chunks.shape
    tr = min(tr, r)
    assert r % tr == 0

    def swap(src_ref, dst_ref, send_sems, recv_sems):
        x, y, core = lax.axis_index("x"), lax.axis_index("y"), lax.axis_index("c")
        copies = [pltpu.make_async_remote_copy(
            src_ref=src_ref.at[2 * k + 1 - core], dst_ref=dst_ref.at[k], send_sem=send_sems.at[k],
            recv_sem=recv_sems.at[k], device_id=(x, y, 1 - core), device_id_type=pl.DeviceIdType.MESH) for k in range(4)]
        for cp in copies:
            cp.start()
        for cp in copies:
            cp.wait()

    hbm = pl.BlockSpec(memory_space=pl.ANY)
    theirs = pl.pallas_call(swap, name=name + "_swap", out_shape=jax.ShapeDtypeStruct((4, r, c), chunks.dtype),
                            in_specs=[hbm], out_specs=hbm,
                            scratch_shapes=[pltpu.SemaphoreType.DMA((4,)), pltpu.SemaphoreType.DMA((4,))])(chunks)

    def add(mine_ref, theirs_ref, out_ref):
        core = lax.axis_index("c")
        both = mine_ref[...].astype(F32)
        out_ref[...] = (jnp.where(core == 0, both[0], both[1]) + theirs_ref[...].astype(F32)).astype(out_ref.dtype)

    return pl.pallas_call(
        add, name=name + "_add", grid=(4, r // tr), out_shape=jax.ShapeDtypeStruct((4, r, c), chunks.dtype),
        in_specs=[pl.BlockSpec((None, 2, tr, c), lambda k, i: (k, 0, i, 0)), pl.BlockSpec((None, tr, c), lambda k, i: (k, i, 0))],
        out_specs=pl.BlockSpec((None, tr, c), lambda k, i: (k, i, 0)),
        compiler_params=_cparams(("parallel", "parallel")),
    )(chunks.reshape(4, 2, r, c), theirs)


def _chip_scatter_transfer(pairs):
    return _Transfer(pairs, pairs.shape, lambda ref, p: ref.at[p], lambda ref, s: ref.at[s], same_core=True)


def _matmul(a, b, mode, *, name, tm, tn, tk, epi=None, extra=(), out_dtype=F32, chunks=None, comm=None):
    pieces = list(a) if isinstance(a, (list, tuple)) else [a]
    rows_a, cols_a = pieces[0].shape[0], sum(p.shape[1] for p in pieces)
    if mode == "nn":
        (M, K), N = (rows_a, cols_a), b.shape[1]
    elif mode == "nt":
        (M, K), N = (rows_a, cols_a), b.shape[0]
    else:
        (K, M), N = (rows_a, cols_a), b.shape[1]
    tm, tn, tk = min(tm, M), min(tn, N), min(tk, K)
    assert M % tm == 0 and N % tn == 0 and K % tk == 0 and (epi != "ln" or tn == N), (name, M, N, K)
    nk = K // tk
    tile_cols, axis = (tm, 0) if mode == "tn" else (tk, 2)
    assert all(p.shape[1] % tile_cols == 0 for p in pieces)
    counts = [p.shape[1] // tile_cols for p in pieces]
    starts = [sum(counts[:q]) for q in range(len(pieces))]

    def a_spec_of(q):
        at = lambda t: jnp.clip(t - starts[q], 0, counts[q] - 1) if len(pieces) > 1 else t
        return {"nn": pl.BlockSpec((tm, tk), lambda i, j, k: (i, at(k))),
                "nt": pl.BlockSpec((tm, tk), lambda i, j, k: (i, at(k))),
                "tn": pl.BlockSpec((tk, tm), lambda i, j, k: (k, at(i)))}[mode]

    n_a = len(pieces)
    b_mode = pl.Buffered(1) if (nk == 1 and tn == N and n_a > 1) else None
    b_spec = {"nn": pl.BlockSpec((tk, tn), lambda i, j, k: (k, j), pipeline_mode=b_mode),
              "nt": pl.BlockSpec((tn, tk), lambda i, j, k: (j, k), pipeline_mode=b_mode),
              "tn": pl.BlockSpec((tk, tn), lambda i, j, k: (k, j), pipeline_mode=b_mode)}[mode]
    ca, cb = {"nn": (1, 0), "nt": (1, 1), "tn": (0, 0)}[mode]
    tile = pl.BlockSpec((tm, tn), lambda i, j, k: (i, j))
    row = pl.BlockSpec((1, tn), lambda i, j, k: (0, j))
    n_extra = {None: 0, "add": 1, "relu2": 0, "drelu2": 1, "ln": 3}[epi]
    assert len(extra) == n_extra
    extra_specs = {None: [], "add": [tile], "relu2": [], "drelu2": [tile], "ln": [tile, row, row]}[epi]
    split = 0
    if epi == "relu2":
        out_shape, out_specs = (jax.ShapeDtypeStruct((M, N), BF16),), (tile,)
    elif epi == "ln":
        out_shape = (jax.ShapeDtypeStruct((M, N), F32), jax.ShapeDtypeStruct((M, N), F32),
                     jax.ShapeDtypeStruct((M, N), BF16))
        out_specs = (tile, tile, tile)
    elif chunks == "cols":
        c = N // N_DEV
        out_shape = (jax.ShapeDtypeStruct((N_DEV, M, c), out_dtype),)
        if tn == N:
            split = c
            out_specs = (pl.BlockSpec((N_DEV, tm, c), lambda i, j, k: (0, i, 0)),)
        else:
            assert c % tn == 0
            out_specs = (pl.BlockSpec((None, tm, tn), lambda i, j, k: (j // (c // tn), i, j % (c // tn))),)
    else:
        out_shape, out_specs = (jax.ShapeDtypeStruct((M, N), out_dtype),), (tile,)
    n_out = len(out_shape)

    def body(*refs):
        a_refs, b_ref = refs[:n_a], refs[n_a]
        ex = refs[n_a + 1:n_a + 1 + n_extra]
        outs = refs[n_a + 1 + n_extra:n_a + 1 + n_extra + n_out]
        acc_ref = refs[-1]
        k = pl.program_id(2)

        def finish(acc):
            if epi == "add":
                outs[0][...] = (acc + ALPHA * ex[0][...]).astype(out_dtype)
            elif epi == "relu2":
                r = jnp.maximum(acc, 0.0)
                outs[0][...] = _bf(r * r)
            elif epi == "drelu2":
                outs[0][...] = (acc * (2.0 * jnp.sqrt(ex[0][...].astype(F32)))).astype(out_dtype)
            elif epi == "ln":
                u = ALPHA * ex[0][...] + acc
                xh, _ = _norm_stats(u)
                y = xh * ex[1][...] + ex[2][...]
                outs[0][...] = u
                outs[1][...] = y
                outs[2][...] = _bf(y)
            elif split:
                for p in range(N_DEV):
                    outs[0][p] = acc[:, p * split:(p + 1) * split].astype(out_dtype)
            else:
                outs[0][...] = acc.astype(out_dtype)

        def step(a_ref, first, middle, last):
            part = _dg(_bf(a_ref[...]), _bf(b_ref[...]), ca, cb)
            if nk == 1:
                finish(part)
                return
            if first:
                @pl.when(k == 0)
                def _():
                    acc_ref[...] = part

            if middle:
                @pl.when(jnp.logical_and(k > 0, k < nk - 1))
                def _():
                    acc_ref[...] += part

            if last:
                @pl.when(k == nk - 1)
                def _():
                    finish(acc_ref[...] + part)

        if n_a == 1:
            step(a_refs[0], True, True, True)
        else:
            t = pl.program_id(axis)
            for q in range(n_a):
                along_k = axis == 2
                first = not along_k or starts[q] == 0
                last = not along_k or starts[q] + counts[q] == nk
                middle = not along_k or counts[q] > int(first) + int(last)

                @pl.when(jnp.logical_and(t >= starts[q], t < starts[q] + counts[q]))
                def _(q=q, first=first, middle=middle, last=last):
                    step(a_refs[q], first, middle, last)

    outs, landed = _pcall(
        body, name=name, out_shape=out_shape, grid=(M // tm, N // tn, nk),
        in_specs=[a_spec_of(q) for q in range(n_a)] + [b_spec] + extra_specs, out_specs=out_specs,
        scratch_shapes=[pltpu.VMEM((tm, tn) if nk > 1 else (8, 128), F32)], sem=("parallel", "parallel", "arbitrary"),
        args=(*pieces, b, *extra), comm=comm)
    res = outs[0] if n_out == 1 else tuple(outs)
    if chunks == "rows":
        res = res.reshape(N_DEV, M // N_DEV, N)
    return res if comm is None else (res, landed)


def _matmul_rows_of(pieces, b, res, *, name, tm, comm=None):
    M, (K, N) = pieces[0].shape[0], b.shape
    tm = min(tm, M)
    subs, start = [], 0
    for q, p in enumerate(pieces):
        w = p.shape[1]
        step = w if start % w == 0 else 512
        assert w % step == 0 and start % step == 0
        subs += [(q, off, step, start + off) for off in range(0, w, step)]
        start += w
    assert start == K
    n_p, n_s = len(pieces), len(subs)

    def body(*refs):
        a_refs, b_refs, res_ref, out_ref = refs[:n_p], refs[n_p:n_p + n_s], refs[n_p + n_s], refs[n_p + n_s + 1]
        acc = None
        for (q, off, w, _), b_ref in zip(subs, b_refs):
            part = _dg(_bf(a_refs[q][:, off:off + w]), _bf(b_ref[...]), 1, 0)
            acc = part if acc is None else acc + part
        out_ref[...] = acc + ALPHA * res_ref[...]

    tile = pl.BlockSpec((tm, N), lambda i: (i, 0))
    outs, landed = _pcall(
        body, name=name, grid=(M // tm,), out_shape=[jax.ShapeDtypeStruct((M, N), F32)],
        in_specs=[pl.BlockSpec((tm, p.shape[1]), lambda i: (i, 0)) for p in pieces] +
                 [pl.BlockSpec((w, N), lambda i, r=row // w: (r, 0), pipeline_mode=pl.Buffered(1))
                  for _, _, w, row in subs] + [tile],
        out_specs=[tile], scratch_shapes=[], sem=("parallel",), args=(*pieces, *([b] * n_s), res), comm=comm)
    return outs[0] if comm is None else (outs[0], landed)


def _ret_tables(S):
    half = 64
    inv_freq = ROPE_BASE ** (-jnp.arange(half, dtype=F32) / half)
    ang = jnp.arange(S, dtype=jnp.int32).astype(F32)[:, None] * inv_freq[None, :]
    cos, sin = jnp.cos(ang), jnp.sin(ang)
    cosf = jnp.concatenate([cos, cos], axis=1)
    sinf = jnp.concatenate([-sin, sin], axis=1)
    log_g = jnp.log(1.0 - 2.0 ** (-5.0 - jnp.arange(4, dtype=F32)))
    idx = jnp.arange(CHUNK, dtype=F32)
    diff = idx[:, None] - idx[None, :]
    md = jnp.where(diff[None] >= 0, jnp.exp(log_g[:, None, None] * diff[None]), 0.0)
    kd = jnp.exp(log_g[:, None] * (CHUNK - 1 - idx)[None, :])
    qd = jnp.exp(log_g[:, None] * (idx + 1.0)[None, :])
    cd = jnp.exp(log_g * CHUNK)
    bc = lambda t: jnp.broadcast_to(t[:, :, None], (4, CHUNK, CHUNK))
    return cosf, sinf, md, bc(qd), bc(kd), jnp.broadcast_to(cd[:, None, None], (4, 8, CHUNK))


def _rot(x, cosf, sinf):
    return x * cosf + pltpu.roll(x, 64, 1) * sinf


def _rot_t(dx, cosf, sinf):
    return dx * cosf - pltpu.roll(dx, 64, 1) * sinf


RET_CHUNKS = 2


def _ret_specs(rev, S):
    R = min(RET_CHUNKS, S // CHUNK)
    rows, steps = R * CHUNK, S // (R * CHUNK)
    rn = (lambda n: steps - 1 - n) if rev else (lambda n: n)
    col = lambda c: pl.BlockSpec((rows, 512), lambda n, c=c: (rn(n), c))
    tab = pl.BlockSpec((rows, CHUNK), lambda n: (rn(n), 0))
    dec = pl.BlockSpec((4, CHUNK, CHUNK), lambda n: (0, 0, 0))
    cdec = pl.BlockSpec((4, 8, CHUNK), lambda n: (0, 0, 0))
    vec = pl.BlockSpec((1, 512), lambda n: (0, 0))
    st = pl.BlockSpec((R, 4, CHUNK, CHUNK), lambda n: (rn(n), 0, 0, 0))
    return R, steps, rn, col, tab, dec, cdec, vec, st


def _ret_fwd(proj, tables, gn_g, gn_b):
    S = proj.shape[0]
    R, steps, _, col, tab, dec, cdec, vec, st = _ret_specs(False, S)

    def body(q_ref, k_ref, v_ref, g_ref, cos_ref, sin_ref, md_ref, qd_ref, kd_ref, cd_ref, gng_ref, gnb_ref,
             out_ref, st_ref, state):
        @pl.when(pl.program_id(0) == 0)
        def _():
            state[...] = jnp.zeros_like(state)

        tiles = [(c, h) for c in range(R) for h in range(4)]
        rs = lambda c: slice(c * CHUNK, (c + 1) * CHUNK)
        sl = lambda h: slice(h * 128, (h + 1) * 128)
        qr = [_rot(q_ref[rs(c), sl(h)].astype(F32), cos_ref[rs(c), :], sin_ref[rs(c), :]) for c, h in tiles]
        kr = [_rot(k_ref[rs(c), sl(h)].astype(F32), cos_ref[rs(c), :], sin_ref[rs(c), :]) * (128 ** -0.5)
              for c, h in tiles]
        vb = [_bf(v_ref[rs(c), sl(h)]) for c, h in tiles]
        kv = [_dg(_bf(k * kd_ref[h]), v, 0, 0) for k, v, (c, h) in zip(kr, vb, tiles)]
        before = {}
        for h in range(4):
            s_h = state[h]
            for c in range(R):
                st_ref[c, h] = s_h
                before[(c, h)] = s_h
                s_h = s_h * cd_ref[h, 0:1, :] + kv[c * 4 + h]
            state[h] = s_h
        sc = [_dg(_bf(q), _bf(k), 1, 1) * md_ref[h] for q, k, (c, h) in zip(qr, kr, tiles)]
        r = [_dg(_bf(x), v, 1, 0) + _dg(_bf(q * qd_ref[h]), _bf(before[(c, h)]), 1, 0)
             for x, v, q, (c, h) in zip(sc, vb, qr, tiles)]
        for x, (c, h) in zip(r, tiles):
            y, _ = _norm_stats(x)
            rg = g_ref[rs(c), sl(h)].astype(F32)
            out_ref[rs(c), sl(h)] = rg * _sigmoid(rg) * (y * gng_ref[:, sl(h)] + gnb_ref[:, sl(h)])

    return pl.pallas_call(
        body, name="ret_fwd", grid=(steps,),
        out_shape=(jax.ShapeDtypeStruct((S, RET_W), F32), jax.ShapeDtypeStruct((S // CHUNK, 4, CHUNK, CHUNK), F32)),
        in_specs=[col(0), col(1), col(2), col(3), tab, tab, dec, dec, dec, cdec, vec, vec],
        out_specs=(pl.BlockSpec((R * CHUNK, 512), lambda n: (n, 0)), st),
        scratch_shapes=[pltpu.VMEM((4, CHUNK, CHUNK), F32)],
        compiler_params=_cparams(("arbitrary",)),
    )(proj, proj, proj, proj, *tables, gn_g, gn_b)


def _ret_bwd(proj, tables, gn_g, gn_b, states, d_out):
    S = proj.shape[0]
    R, steps, rn, col, tab, dec, cdec, vec, st = _ret_specs(True, S)

    def kernel_body(q_ref, k_ref, v_ref, g_ref, cos_ref, sin_ref, md_ref, qd_ref, kd_ref, cd_ref, gng_ref, gnb_ref,
                    st_ref, do_ref, dp_ref, dg_ref, db_ref, gstate):
        @pl.when(pl.program_id(0) == 0)
        def _():
            gstate[...] = jnp.zeros_like(gstate)
            dg_ref[...] = jnp.zeros_like(dg_ref)
            db_ref[...] = jnp.zeros_like(db_ref)

        tiles = [(c, h) for c in range(R) for h in range(4)]
        rs = lambda c: slice(c * CHUNK, (c + 1) * CHUNK)
        sl = lambda h: slice(h * 128, (h + 1) * 128)
        rot = lambda ref, c, h: _rot(ref[rs(c), sl(h)].astype(F32), cos_ref[rs(c), :], sin_ref[rs(c), :])
        qr = [rot(q_ref, c, h) for c, h in tiles]
        kr = [rot(k_ref, c, h) * (128 ** -0.5) for c, h in tiles]
        qb, kb = [_bf(x) for x in qr], [_bf(x) for x in kr]
        vb = [_bf(v_ref[rs(c), sl(h)]) for c, h in tiles]
        s0b = [_bf(st_ref[c, h]) for c, h in tiles]
        scb = [_bf(_dg(q, k, 1, 1) * md_ref[h]) for q, k, (c, h) in zip(qb, kb, tiles)]
        qdb = [_bf(q * qd_ref[h]) for q, (c, h) in zip(qr, tiles)]
        kdb = [_bf(k * kd_ref[h]) for k, (c, h) in zip(kr, tiles)]
        r = [_dg(x, v, 1, 0) + _dg(q, s, 1, 0) for x, v, q, s in zip(scb, vb, qdb, s0b)]
        drb, d_rg = [], []
        for x, (c, h) in zip(r, tiles):
            y, rstd = _norm_stats(x)
            gng = gng_ref[:, sl(h)]
            rg = g_ref[rs(c), sl(h)].astype(F32)
            sg = _sigmoid(rg)
            d_o = do_ref[rs(c), sl(h)]
            d_gn = d_o * (rg * sg)
            dg_ref[:, sl(h)] += jnp.sum(d_gn * y, axis=0, keepdims=True)
            db_ref[:, sl(h)] += jnp.sum(d_gn, axis=0, keepdims=True)
            drb.append(_bf(_norm_bwd(d_gn * gng, y, rstd)))
            d_rg.append(_bf(d_o * (y * gng + gnb_ref[:, sl(h)]) * (sg * (1.0 + rg * (1.0 - sg)))))
        grow = [_dg(q, d, 0, 0) for q, d in zip(qdb, drb)]
        after = {}
        for h in range(4):
            g_h = gstate[h]
            for c in reversed(range(R)):
                after[(c, h)] = _bf(g_h)
                g_h = g_h * cd_ref[h, 0:1, :] + grow[c * 4 + h]
            gstate[h] = g_h
        dscb = [_bf(_dg(d, v, 1, 1) * md_ref[h]) for d, v, (c, h) in zip(drb, vb, tiles)]
        for t, (c, h) in enumerate(tiles):
            gb = after[(c, h)]
            dqr = _dg(dscb[t], kb[t], 1, 0) + _dg(drb[t], s0b[t], 1, 1) * qd_ref[h]
            dkr = _dg(dscb[t], qb[t], 0, 0) + _dg(vb[t], gb, 1, 1) * kd_ref[h]
            dv = _dg(scb[t], drb[t], 0, 0) + _dg(kdb[t], gb, 1, 0)
            cosf, sinf = cos_ref[rs(c), :], sin_ref[rs(c), :]
            dp_ref[rs(c), 0 * 512 + h * 128:0 * 512 + (h + 1) * 128] = _bf(_rot_t(dqr, cosf, sinf))
            dp_ref[rs(c), 1 * 512 + h * 128:1 * 512 + (h + 1) * 128] = _bf(_rot_t(dkr, cosf, sinf) * (128 ** -0.5))
            dp_ref[rs(c), 2 * 512 + h * 128:2 * 512 + (h + 1) * 128] = _bf(dv)
            dp_ref[rs(c), 3 * 512 + h * 128:3 * 512 + (h + 1) * 128] = d_rg[t]

    acc = pl.BlockSpec((1, 512), lambda n: (0, 0))
    return pl.pallas_call(
        kernel_body, name="ret_bwd", grid=(steps,),
        out_shape=(jax.ShapeDtypeStruct((S, 2048), BF16), jax.ShapeDtypeStruct((1, 512), F32),
                   jax.ShapeDtypeStruct((1, 512), F32)),
        in_specs=[col(0), col(1), col(2), col(3), tab, tab, dec, dec, dec, cdec, vec, vec, st,
                  pl.BlockSpec((R * CHUNK, 512), lambda n: (rn(n), 0))],
        out_specs=(pl.BlockSpec((R * CHUNK, 2048), lambda n: (rn(n), 0)), acc, acc),
        scratch_shapes=[pltpu.VMEM((4, CHUNK, CHUNK), F32)],
        compiler_params=_cparams(("arbitrary",)),
    )(proj, proj, proj, proj, *tables, gn_g, gn_b, states, d_out)


SB_T = 256
SB_SCALE = 64 ** -0.5
SB_Q_COL, SB_K_COL, SB_V_COL = 2048 // 128, 2560 // 128, 3072 // 128


def _head_masks():
    lane = lax.broadcasted_iota(jnp.int32, (1, 128), 1)
    m0 = (lane < 64).astype(F32)
    return m0, 1.0 - m0


def _tri(n, cmp):
    r = lax.broadcasted_iota(jnp.int32, (n, n), 0)
    c = lax.broadcasted_iota(jnp.int32, (n, n), 1)
    return cmp(r, c)


def _tri_sum(x, tri):
    hi = _bf(x)
    lo = _bf(x - hi.astype(F32))
    return _dg(hi, tri, 1, 0) + _dg(lo, tri, 1, 0)


def _sb_weights(qms, kblks, upper, carry, causal):
    tiles = [(b, h) for b in range(len(kblks)) for h in range(2)]
    zs = [_dg(qms[h], kblks[b], 1, 1) for b, h in tiles]
    lgs = [-(jnp.maximum(z, 0.0) + jnp.log(1.0 + jnp.exp(-jnp.abs(z)))) for z in zs]
    if causal is not None:
        lgs = [jnp.where(causal, lg, 0.0) if b == 0 else lg for lg, (b, h) in zip(lgs, tiles)]
    carries = list(carry)
    for t in range(len(tiles) - 2):
        carries.append(carries[t] + jnp.sum(lgs[t], axis=1, keepdims=True))
    his = [_bf(lg) for lg in lgs]
    los = [_bf(lg - hi.astype(F32)) for lg, hi in zip(lgs, his)]
    later = [_dg(hi, upper, 1, 0) for hi in his]
    later = [r + _dg(lo, upper, 1, 0) for r, lo in zip(later, los)]
    a = [jnp.exp(lg + z + (r + c)) for lg, z, r, c in zip(lgs, zs, later, carries)]
    if causal is not None:
        a = [jnp.where(causal, x, 0.0) if b == 0 else x for x, (b, h) in zip(a, tiles)]
    out = tuple(carries[t] + jnp.sum(lgs[t], axis=1, keepdims=True) for t in (len(tiles) - 2, len(tiles) - 1))
    return [a[2 * b:2 * b + 2] for b in range(len(kblks))], out


def _sb_fwd(proj, comm=None):
    S = proj.shape[0]
    T = min(SB_T, S)
    nq = S // T

    def body(q_ref, k_ref, v_ref, o_ref, a_ref, kb_ref, vm_ref, acc_ref):
        i = pl.program_id(1)
        m0, m1 = _head_masks()

        @pl.when(i == 0)
        def _():
            v = v_ref[...]
            kb_ref[...] = _bf(k_ref[...])
            vm_ref[0] = _bf(v * m0)
            vm_ref[1] = _bf(v * m1)

        q = q_ref[...]
        qm = (_bf(q * (m0 * SB_SCALE)), _bf(q * (m1 * SB_SCALE)))
        upper = _tri(T, lambda r, c: r > c).astype(BF16)
        causal = _tri(T, lambda r, c: c < r)

        def tiles(js, carry, mask, first):
            ks = [pl.multiple_of(j * T, T) for j in js]
            a, out = _sb_weights(qm, [kb_ref[pl.ds(k, T), :] for k in ks], upper, carry, mask)
            a = [[_bf(t) for t in per_block] for per_block in a]
            for b, j in enumerate(js):
                for h in range(2):
                    a_ref[h, j] = a[b][h]
            parts = [_dg(a[b][h], vm_ref[h, pl.ds(k, T), :], 1, 0) for b, k in enumerate(ks) for h in range(2)]
            part = functools.reduce(lambda u, w: u + w, parts)
            if first:
                acc_ref[...] = part
            else:
                acc_ref[...] += part
            return out

        zero = jnp.zeros((T, 1), F32)
        carry = lax.cond(i == 0, lambda: tiles([i], (zero, zero), causal, True),
                         lambda: tiles([i, i - 1], (zero, zero), causal, True))
        n = jnp.maximum(i - 1, 0)
        carry = lax.fori_loop(0, n % 2, lambda _, c: tiles([n - 1], c, None, False), carry)
        top = n - 1 - n % 2
        carry = lax.fori_loop(0, (n // 2) % 2, lambda _, c: tiles([top, top - 1], c, None, False), carry)
        top = top - 2 * ((n // 2) % 2)
        lax.fori_loop(0, n // 4, lambda jj, c: tiles([top - 4 * jj - b for b in range(4)], c, None, False), carry)
        o_ref[...] = acc_ref[...]

    full = lambda c: pl.BlockSpec((S, 128), lambda p, i, c=c: (0, c + p))
    outs, landed = _pcall(
        body, name="sb_fwd", grid=(4, nq),
        out_shape=[jax.ShapeDtypeStruct((S, SB_W), F32), jax.ShapeDtypeStruct((4, 2, nq, nq, T, T), BF16)],
        in_specs=[pl.BlockSpec((T, 128), lambda p, i: (i, SB_Q_COL + p)), full(SB_K_COL), full(SB_V_COL)],
        out_specs=[pl.BlockSpec((T, 128), lambda p, i: (i, p)),
                   pl.BlockSpec((None, 2, None, nq, T, T), lambda p, i: (p, 0, i, 0, 0, 0))],
        scratch_shapes=[pltpu.VMEM((S, 128), BF16), pltpu.VMEM((2, S, 128), BF16), pltpu.VMEM((T, 128), F32)],
        sem=("arbitrary", "arbitrary"), args=(proj, proj, proj), comm=comm)
    return tuple(outs) if comm is None else (tuple(outs), landed)


def _sb_bwd(proj, a_saved, d_o, comm=None):
    S = proj.shape[0]
    T = min(SB_T, S)
    nq = S // T

    def body(q_ref, k_ref, v_ref, do_ref, a_ref, dq_ref, dk_ref, dv_ref, kb_ref, kbm_ref, vb_ref, dq_acc, dk_acc, dv_acc):
        i = pl.program_id(1)
        m0, m1 = _head_masks()

        @pl.when(i == 0)
        def _():
            k = k_ref[...]
            kb_ref[...] = _bf(k)
            kbm_ref[0] = _bf(k * m0)
            kbm_ref[1] = _bf(k * m1)
            vb_ref[...] = _bf(v_ref[...])
            dk_acc[...] = jnp.zeros_like(dk_acc)
            dv_acc[...] = jnp.zeros_like(dv_acc)

        q, d_out = q_ref[...], do_ref[...]
        qm = (_bf(q * (m0 * SB_SCALE)), _bf(q * (m1 * SB_SCALE)))
        dom = (_bf(d_out * m0), _bf(d_out * m1))
        qm_t = tuple(_bf((q * (m * SB_SCALE)).T) for m in (m0, m1))
        dom_t = tuple(_bf((d_out * m).T) for m in (m0, m1))
        lower = _tri(T, lambda r, c: r < c).astype(BF16)
        causal = _tri(T, lambda r, c: c < r)

        def up(js, carry, mask):
            ks = [pl.multiple_of(j * T, T) for j in js]
            tiles = [(b, h) for b in range(len(js)) for h in range(2)]
            zs = [_dg(qm[h], kb_ref[pl.ds(ks[b], T), :], 1, 1) for b, h in tiles]
            a = [a_ref[h, js[b]] for b, h in tiles]
            es = [w.astype(F32) * _dg(dom[h], vb_ref[pl.ds(ks[b], T), :], 1, 1) for w, (b, h) in zip(a, tiles)]
            carries = list(carry)
            for t in range(len(tiles)):
                carries.append(carries[t] + jnp.sum(es[t], axis=1, keepdims=True))
            d_lg = [_dg(_bf(e), lower, 1, 0) + c for e, c in zip(es, carries)]
            ens = [jnp.exp(-jnp.abs(z)) for z in zs]
            invs = [1.0 / (1.0 + en) for en in ens]
            betas = [jnp.where(z >= 0.0, inv, en * inv) for z, en, inv in zip(zs, ens, invs)]
            dzs = [e * (1.0 - b) - d * b for e, b, d in zip(es, betas, d_lg)]
            if mask is not None:
                dzs = [jnp.where(mask, dz, 0.0) if b == len(js) - 1 else dz for dz, (b, h) in zip(dzs, tiles)]
            dzs = [_bf(dz) for dz in dzs]
            parts = [_dg(dzs[t], kbm_ref[h, pl.ds(ks[b], T), :], 1, 0) for t, (b, h) in enumerate(tiles)]
            dq_acc[...] += functools.reduce(lambda u, w: u + w, parts)
            for b, j in enumerate(js):
                dk_acc[j] += _dg(qm_t[0], dzs[2 * b], 1, 0) + _dg(qm_t[1], dzs[2 * b + 1], 1, 0)
                dv_acc[j] += _dg(dom_t[0], a[2 * b], 1, 0) + _dg(dom_t[1], a[2 * b + 1], 1, 0)
            return tuple(carries[-2:])

        zero = jnp.zeros((T, 1), F32)
        dq_acc[...] = jnp.zeros_like(dq_acc)
        n = jnp.maximum(i - 1, 0)
        carry = lax.fori_loop(0, n // 4, lambda jj, c: up([4 * jj + b for b in range(4)], c, None), (zero, zero))
        done = 4 * (n // 4)
        carry = lax.fori_loop(0, (n // 2) % 2, lambda _, c: up([done, done + 1], c, None), carry)
        carry = lax.fori_loop(0, n % 2, lambda _, c: up([n - 1], c, None), carry)

        @pl.when(i == 0)
        def _():
            up([i], carry, causal)

        @pl.when(i > 0)
        def _():
            up([i - 1, i], carry, causal)

        dq_ref[...] = _bf(dq_acc[...] * SB_SCALE)

        @pl.when(i == nq - 1)
        def _():
            for j in range(nq):
                dk_ref[j * T:(j + 1) * T, :] = _bf(dk_acc[j].T)
                dv_ref[j * T:(j + 1) * T, :] = _bf(dv_acc[j].T)

    full = lambda c: pl.BlockSpec((S, 128), lambda p, i, c=c: (0, c + p))
    tile = pl.BlockSpec((T, 128), lambda p, i: (i, p))
    acc = pl.BlockSpec((S, 128), lambda p, i: (0, p))
    out = jax.ShapeDtypeStruct((S, SB_W), BF16)
    outs, landed = _pcall(
        body, name="sb_bwd", grid=(4, nq), out_shape=[out, out, out],
        in_specs=[pl.BlockSpec((T, 128), lambda p, i: (i, SB_Q_COL + p)), full(SB_K_COL), full(SB_V_COL), tile,
                  pl.BlockSpec((None, 2, None, nq, T, T), lambda p, i: (p, 0, i, 0, 0, 0))],
        out_specs=[tile, acc, acc],
        scratch_shapes=[pltpu.VMEM((S, 128), BF16), pltpu.VMEM((2, S, 128), BF16), pltpu.VMEM((S, 128), BF16),
                        pltpu.VMEM((T, 128), F32), pltpu.VMEM((nq, 128, T), F32), pltpu.VMEM((nq, 128, T), F32)],
        sem=("arbitrary", "arbitrary"), args=(proj, proj, proj, d_o, a_saved), comm=comm)
    return tuple(outs) if comm is None else (tuple(outs), landed)


SGU_U_COL, SGU_V_COL = 3584 // 512, 4096 // 512


def _causal(w):
    r = lax.broadcasted_iota(jnp.int32, (CHUNK, CHUNK), 0)
    c = lax.broadcasted_iota(jnp.int32, (CHUNK, CHUNK), 1)
    return jnp.where(r >= c, w, 0.0)


SGU_CHUNKS = 4


def _sgu_fwd(proj, ln_g, ln_b, w, b):
    S = proj.shape[0]
    R = min(SGU_CHUNKS, S // CHUNK)
    rows = R * CHUNK

    def body(u_ref, v_ref, g_ref, b_ref, w_ref, bias_ref, out_ref):
        wc = [_bf(_causal(w_ref[g])) for g in range(4)]
        for r in range(R):
            rs = slice(r * CHUNK, (r + 1) * CHUNK)
            u = _gelu(u_ref[rs, :].astype(F32))
            xh, _ = _norm_stats(_gelu(v_ref[rs, :].astype(F32)))
            vn = _bf(xh * g_ref[...] + b_ref[...])
            for g in range(4):
                sl = slice(g * 128, (g + 1) * 128)
                out_ref[rs, sl] = u[:, sl] * (_dg(wc[g], vn[:, sl], 1, 0) + bias_ref[g])

    vec = pl.BlockSpec((1, 512), lambda n: (0, 0))
    return pl.pallas_call(
        body, name="sgu_fwd", grid=(S // rows,),
        out_shape=jax.ShapeDtypeStruct((S, SGU_W), F32),
        in_specs=[pl.BlockSpec((rows, 512), lambda n: (n, SGU_U_COL)),
                  pl.BlockSpec((rows, 512), lambda n: (n, SGU_V_COL)), vec, vec,
                  pl.BlockSpec((4, CHUNK, CHUNK), lambda n: (0, 0, 0)), pl.BlockSpec((4, CHUNK, 1), lambda n: (0, 0, 0))],
        out_specs=pl.BlockSpec((rows, 512), lambda n: (n, 0)),
        compiler_params=_cparams(("parallel",)),
    )(proj, proj, ln_g, ln_b, w, b)


def _sgu_bwd(proj, ln_g, ln_b, w, b, d_out):
    S = proj.shape[0]
    R = min(SGU_CHUNKS, S // CHUNK)
    rows = R * CHUNK

    def body(u_ref, v_ref, g_ref, b_ref, w_ref, bias_ref, do_ref, dp_ref, dg_ref, db_ref, dw_ref, dbias_ref):
        @pl.when(pl.program_id(0) == 0)
        def _():
            dg_ref[...] = jnp.zeros_like(dg_ref)
            db_ref[...] = jnp.zeros_like(db_ref)
            dw_ref[...] = jnp.zeros_like(dw_ref)
            dbias_ref[...] = jnp.zeros_like(dbias_ref)

        ln_gain = g_ref[...]
        wc = [_bf(_causal(w_ref[g])) for g in range(4)]
        for r in range(R):
            rs = slice(r * CHUNK, (r + 1) * CHUNK)
            gu, gv = u_ref[rs, :].astype(F32), v_ref[rs, :].astype(F32)
            u = _gelu(gu)
            xh, rstd = _norm_stats(_gelu(gv))
            vn = _bf(xh * ln_gain + b_ref[...])
            d_o = do_ref[rs, :]
            d_vn = []
            for g in range(4):
                sl = slice(g * 128, (g + 1) * 128)
                sv = _dg(wc[g], vn[:, sl], 1, 0) + bias_ref[g]
                dp_ref[rs, sl] = _bf(d_o[:, sl] * sv * _gelu_grad(gu[:, sl]))
                d_sv = d_o[:, sl] * u[:, sl]
                dbias_ref[g] += jnp.sum(d_sv, axis=1, keepdims=True)
                d_svb = _bf(d_sv)
                dw_ref[g] += _causal(_dg(d_svb, vn[:, sl], 1, 1))
                d_vn.append(_dg(wc[g], d_svb, 0, 0))
            d_vn = jnp.concatenate(d_vn, axis=1)
            dg_ref[...] += jnp.sum(d_vn * xh, axis=0, keepdims=True)
            db_ref[...] += jnp.sum(d_vn, axis=0, keepdims=True)
            dp_ref[rs, 512:1024] = _bf(_norm_bwd(d_vn * ln_gain, xh, rstd) * _gelu_grad(gv))

    vec = pl.BlockSpec((1, 512), lambda n: (0, 0))
    wspec = pl.BlockSpec((4, CHUNK, CHUNK), lambda n: (0, 0, 0))
    bspec = pl.BlockSpec((4, CHUNK, 1), lambda n: (0, 0, 0))
    return pl.pallas_call(
        body, name="sgu_bwd", grid=(S // rows,),
        out_shape=(jax.ShapeDtypeStruct((S, 1024), BF16), jax.ShapeDtypeStruct((1, 512), F32),
                   jax.ShapeDtypeStruct((1, 512), F32), jax.ShapeDtypeStruct((4, CHUNK, CHUNK), F32),
                   jax.ShapeDtypeStruct((4, CHUNK, 1), F32)),
        in_specs=[pl.BlockSpec((rows, 512), lambda n: (n, SGU_U_COL)),
                  pl.BlockSpec((rows, 512), lambda n: (n, SGU_V_COL)), vec, vec, wspec, bspec,
                  pl.BlockSpec((rows, 512), lambda n: (n, 0))],
        out_specs=(pl.BlockSpec((rows, 1024), lambda n: (n, 0)), vec, vec, wspec, bspec),
        compiler_params=_cparams(("arbitrary",)),
    )(proj, proj, ln_g, ln_b, w, b, d_out)


GATE_COL = 4608 // 512


def _merge_fwd(proj, branches, p_list, tm=1024):
    S = proj.shape[0]
    tm = min(tm, S)

    def body(r_ref, s_ref, g_ref, pr_ref, ps_ref, pg_ref, gr_ref, gs_ref, gg_ref, m_ref, br_ref):
        acc = None
        for k, (x_ref, p_ref, gate_ref) in enumerate(((r_ref, pr_ref, gr_ref), (s_ref, ps_ref, gs_ref),
                                                      (g_ref, pg_ref, gg_ref))):
            br = _dg(_bf(x_ref[...]), _bf(p_ref[...]), 1, 0)
            br_ref[k] = _bf(br)
            term = _sigmoid(gate_ref[...].astype(F32)) * br
            acc = term if acc is None else acc + term
        m_ref[...] = _bf(acc)

    xs = pl.BlockSpec((tm, 512), lambda i, n: (i, 0))
    ps = pl.BlockSpec((512, 512), lambda i, n: (0, n))
    gate = lambda k: pl.BlockSpec((tm, 512), lambda i, n, k=k: (i, GATE_COL + 2 * k + n))
    return pl.pallas_call(
        body, name="merge_fwd", grid=(S // tm, 2),
        out_shape=(jax.ShapeDtypeStruct((S, D_MODEL), BF16), jax.ShapeDtypeStruct((3, S, D_MODEL), BF16)),
        in_specs=[xs, xs, xs, ps, ps, ps, gate(0), gate(1), gate(2)],
        out_specs=(pl.BlockSpec((tm, 512), lambda i, n: (i, n)), pl.BlockSpec((3, tm, 512), lambda i, n: (0, i, n))),
        compiler_params=_cparams(("parallel", "parallel")),
    )(*branches, *p_list, proj, proj, proj)


def _gate_bwd(proj, br, d_merged, tm=1024):
    S = proj.shape[0]
    tm = min(tm, S)

    def body(dm_ref, br_ref, gr_ref, gs_ref, gg_ref, *out_refs):
        dm = dm_ref[...]
        for k, gate_ref in enumerate((gr_ref, gs_ref, gg_ref)):
            s = _sigmoid(gate_ref[...].astype(F32))
            out_refs[k][...] = _bf(dm * s)
            out_refs[3 + k][...] = _bf(dm * br_ref[k].astype(F32) * (s * (1.0 - s)))

    gate = lambda k: pl.BlockSpec((tm, 512), lambda i, n, k=k: (i, GATE_COL + 2 * k + n))
    three = pl.BlockSpec((3, tm, 512), lambda i, n: (0, i, n))
    tile = pl.BlockSpec((tm, 512), lambda i, n: (i, n))
    outs = pl.pallas_call(
        body, name="gate_bwd", grid=(S // tm, 2),
        out_shape=[jax.ShapeDtypeStruct((S, D_MODEL), BF16)] * 6,
        in_specs=[tile, three, gate(0), gate(1), gate(2)], out_specs=[tile] * 6,
        compiler_params=_cparams(("parallel", "parallel")),
    )(d_merged, br, proj, proj, proj)
    return outs[:3], outs[3:]


def _ln_bwd(dy, u, g, target=None, tm=512):
    S, D = u.shape
    tm = min(tm, S)
    loss = target is not None

    def body(*refs):
        dy_ref, u_ref, g_ref = refs[:3]
        du_ref, dub_ref, dg_ref, db_ref = refs[3 + loss:7 + loss]

        @pl.when(pl.program_id(0) == 0)
        def _():
            for acc_ref in refs[5 + loss:]:
                acc_ref[...] = jnp.zeros_like(acc_ref)

        dy_t = dy_ref[...]
        if loss:
            err = dy_t - refs[3][...]
            refs[-1][...] += jnp.sum(err * err, axis=0, keepdims=True)
            dy_t = err * (1.0 / D)
        xh, rstd = _norm_stats(u_ref[...])
        dg_ref[...] += jnp.sum(dy_t * xh, axis=0, keepdims=True)
        db_ref[...] += jnp.sum(dy_t, axis=0, keepdims=True)
        du = _norm_bwd(dy_t * g_ref[...], xh, rstd)
        du_ref[...] = du
        dub_ref[...] = _bf(du)

    tile = pl.BlockSpec((tm, D), lambda i: (i, 0))
    vec = pl.BlockSpec((1, D), lambda i: (0, 0))
    row = jax.ShapeDtypeStruct((1, D), F32)
    return pl.pallas_call(
        body, name="ln_bwd", grid=(S // tm,),
        out_shape=[jax.ShapeDtypeStruct((S, D), F32), jax.ShapeDtypeStruct((S, D), BF16)] + [row] * (2 + loss),
        in_specs=[tile, tile, vec] + [tile] * loss, out_specs=[tile, tile] + [vec] * (2 + loss),
        compiler_params=_cparams(("arbitrary",)),
    )(dy, u, g, *([target] if loss else []))


def _layer_fwd(x, x_bf, W, tables, sb_comm=None):
    proj = _matmul(x_bf, W["w_in_t"], "nt", name="proj", tm=1024, tn=2560, tk=1024)
    retg, states = _ret_fwd(proj, tables, W["ret_gn_g"], W["ret_gn_b"])
    if sb_comm is None:
        sb, sb_a = _sb_fwd(proj)
    else:
        (sb, sb_a), landed = _sb_fwd(proj, comm=sb_comm[0])
        sb_comm[1](landed)
    sg = _sgu_fwd(proj, W["sgu_ln_g"], W["sgu_ln_b"], W["sgu_w"], W["sgu_b"])
    merged, br = _merge_fwd(proj, (retg, sb, sg), (W["p_ret"], W["p_sb"], W["p_sgu"]))
    u1, x1, x1_bf = _matmul(merged, W["w_out"], "nn", name="out_ln", tm=1024, tn=1024, tk=1024, epi="ln",
                            extra=(x, W["ln1_g"], W["ln1_b"]))
    act = _matmul(x1_bf, W["w_up"], "nn", name="up", tm=1024, tn=4096, tk=1024, epi="relu2")
    u2, x2, x2_bf = _matmul(act, W["w_down"], "nn", name="down_ln", tm=512, tn=1024, tk=4096, epi="ln",
                            extra=(x1, W["ln2_g"], W["ln2_b"]))
    saved = dict(x_bf=x_bf, proj=proj, retg=retg, states=states, sb=sb, sb_a=sb_a, sg=sg, merged=merged, br=br, u1=u1,
                 x1_bf=x1_bf, act=act, u2=u2)
    return x2, x2_bf, saved


def _layer_bwd(d_x2, W, tables, sv, chunk_dtype=None, sb_comm_fn=None, dwin_comm_fn=None, dx_comm_fn=None, target=None):
    dt = F32 if chunk_dtype is None else chunk_dtype
    rows, cols = (None, None) if chunk_dtype is None else ("rows", "cols")
    g, landed = {}, {}
    du2, du2_bf, g["ln2_g"], g["ln2_b"], *sq = _ln_bwd(d_x2, sv["u2"], W["ln2_g"], target=target)
    if sq:
        landed["sq"] = sq[0]
    d_hpre = _matmul(du2_bf, W["w_down"], "nt", name="d_act", tm=1024, tn=2048, tk=1024, epi="drelu2",
                     extra=(sv["act"],), out_dtype=BF16)
    g["w_down"] = _matmul(sv["act"], du2_bf, "tn", name="dw_down", tm=1024, tn=1024, tk=4096, out_dtype=dt, chunks=rows)
    g["w_up"] = _matmul(sv["x1_bf"], d_hpre, "tn", name="dw_up", tm=1024, tn=512, tk=4096, out_dtype=dt, chunks=cols)
    d_x1 = _matmul(d_hpre, W["w_up"], "nt", name="d_x1", tm=512, tn=1024, tk=4096, epi="add", extra=(du2,))
    du1, du1_bf, g["ln1_g"], g["ln1_b"] = _ln_bwd(d_x1, sv["u1"], W["ln1_g"])
    d_merged = _matmul(du1_bf, W["w_out"], "nt", name="d_merged", tm=1024, tn=1024, tk=1024)
    g["w_out"] = _matmul(sv["merged"], du1_bf, "tn", name="dw_out", tm=1024, tn=512, tk=4096, out_dtype=dt, chunks=rows)
    d_br, d_gate = _gate_bwd(sv["proj"], sv["br"], d_merged)
    d_branch = []
    for k, (nm, act) in enumerate((("p_ret", sv["retg"]), ("p_sb", sv["sb"]), ("p_sgu", sv["sg"]))):
        d_branch.append(_matmul(d_br[k], W[nm], "nt", name="d_" + nm[2:], tm=1024, tn=512, tk=1024))
        g[nm] = _matmul(act, d_br[k], "tn", name="dw_" + nm[2:], tm=512, tn=1024, tk=2048, out_dtype=dt, chunks=cols)
    d_ret, g["ret_gn_g"], g["ret_gn_b"] = _ret_bwd(sv["proj"], tables, W["ret_gn_g"], W["ret_gn_b"], sv["states"],
                                                   d_branch[0])
    if sb_comm_fn is None:
        d_sq, d_sk, d_sv = _sb_bwd(sv["proj"], sv["sb_a"], d_branch[1])
    else:
        (d_sq, d_sk, d_sv), landed["sb"] = _sb_bwd(sv["proj"], sv["sb_a"], d_branch[1], comm=sb_comm_fn(g))
    d_sgu, g["sgu_ln_g"], g["sgu_ln_b"], g["sgu_w"], g["sgu_b"] = _sgu_bwd(
        sv["proj"], W["sgu_ln_g"], W["sgu_ln_b"], W["sgu_w"], W["sgu_b"], d_branch[2])
    d_proj = [d_ret, d_sq, d_sk, d_sv, d_sgu, d_gate[0], d_gate[1], d_gate[2]]
    g["w_in"] = _matmul(d_proj, sv["x_bf"], "tn", name="dw_in", tm=256, tn=1024, tk=4096, out_dtype=dt, chunks=rows,
                        comm=None if dwin_comm_fn is None else dwin_comm_fn(g))
    if dwin_comm_fn is not None:
        g["w_in"], landed["dwin"] = g["w_in"]
    if chunk_dtype is None:
        g["w_in"] = g["w_in"].T
    d_x = _matmul_rows_of(d_proj, W["w_in_t"], du1, name="d_x", tm=512,
                          comm=None if dx_comm_fn is None else dx_comm_fn(g))
    if dx_comm_fn is not None:
        d_x, landed["dx"] = d_x
    return d_x, g, landed


BIG = ("w_in", "p_ret", "p_sb", "p_sgu", "w_out", "w_up", "w_down")
SMALL = ("ret_gn_g", "ret_gn_b", "sgu_ln_g", "sgu_ln_b", "sgu_w", "sgu_b", "ln1_g", "ln1_b", "ln2_g", "ln2_b")
GATHER_KIND = {"w_in": "rows", "p_ret": "cols", "p_sb": "cols", "p_sgu": "cols", "w_out": "rows", "w_up": "cols",
               "w_down": "rows"}


def _small_weights(small, l):
    W = {}
    for n in SMALL:
        if n == "sgu_w":
            W[n] = small[n][l]
        elif n == "sgu_b":
            W[n] = small[n][l].reshape(4, CHUNK, 1)
        else:
            W[n] = small[n][l].reshape(1, -1)
    return W


def _local_step(x, target, full, small):
    tables = _ret_tables(x.shape[0])
    Ws = [{**{n: full[n][l] for n in BIG[1:]}, "w_in_t": full["w_in"][l].T, **_small_weights(small, l)}
          for l in range(DEPTH)]
    saved = []
    h, h_bf = x, _bf(x)
    for l in range(DEPTH):
        h, h_bf, sv = _layer_fwd(h, h_bf, Ws[l], tables)
        saved.append(sv)
    grads = [None] * DEPTH
    d_h, grads[-1], landed = _layer_bwd(h, Ws[-1], tables, saved[-1], target=target)
    for l in reversed(range(DEPTH - 1)):
        d_h, grads[l], _ = _layer_bwd(d_h, Ws[l], tables, saved[l])
    return landed["sq"], d_h, grads


def _adam(w, parts, m, v, name):
    L, R, C = w.shape
    tr = next(t for t in (320, 256, 128) if R % t == 0)
    assert len(parts) == L

    def body(*refs):
        w_ref, p_refs, (m_ref, v_ref, g_ref, d_ref, nm_ref, nv_ref) = refs[0], refs[1:1 + L], refs[1 + L:]
        layer = pl.program_id(0)
        g = None
        for li, p_ref in enumerate(p_refs):
            s = p_ref[0].astype(F32)
            for j in range(1, p_ref.shape[0]):
                s = s + p_ref[j].astype(F32)
            g = s if g is None else jnp.where(layer == li, s, g)
        g_ref[...] = g
        d_ref[...], nm_ref[...], nv_ref[...] = _adam_update(w_ref[...], g, m_ref[...], v_ref[...])

    tile = pl.BlockSpec((None, tr, C), lambda l, i: (l, i, 0))
    part = lambda li: pl.BlockSpec((parts[li].shape[0], tr, C), lambda l, i, li=li: (0, jnp.where(l == li, i, 0), 0))
    out = jax.ShapeDtypeStruct((L, R, C), F32)
    return pl.pallas_call(
        body, name=name, grid=(L, R // tr), out_shape=(out, out, out, out),
        in_specs=[tile] + [part(li) for li in range(L)] + [tile, tile],
        out_specs=(tile, tile, tile, tile),
        compiler_params=_cparams(("parallel", "parallel")),
    )(w, *parts, m, v)


def _adam_update(w, g, m, v):
    m2 = ADAM_B1 * m + (1.0 - ADAM_B1) * g
    v2 = ADAM_B2 * v + (1.0 - ADAM_B2) * (g * g)
    m_hat = m2 / (1.0 - ADAM_B1 ** ADAM_STEP)
    v_hat = v2 / (1.0 - ADAM_B2 ** ADAM_STEP)
    return -ADAM_LR * (m_hat / (jnp.sqrt(v_hat) + ADAM_EPS) + ADAM_WD * w), m2, v2


def _adam_small(w, m, v, parts):
    k = len(SMALL)

    def body(*refs):
        w_refs, m_refs, v_refs, p_refs, outs = refs[:k], refs[k:2 * k], refs[2 * k:3 * k], refs[3 * k:5 * k], refs[5 * k:]
        for i in range(k):
            vector = len(w_refs[i].shape) == 2
            for l in range(DEPTH):
                p_ref = p_refs[DEPTH * i + l]
                g = p_ref[0]
                for j in range(1, N_DEV):
                    g = g + p_ref[j]
                at = (slice(l, l + 1), slice(None)) if vector else (l,)
                delta, m2, v2 = _adam_update(w_refs[i][at], g, m_refs[i][at], v_refs[i][at])
                for o_ref, val in zip(outs[4 * i:4 * i + 4], (g, delta, m2, v2)):
                    o_ref[at] = val

    vmem = pl.BlockSpec(memory_space=pltpu.VMEM)
    args = [w[n] for n in SMALL] + [m[n] for n in SMALL] + [v[n] for n in SMALL] + \
           [parts[(n, l)] for n in SMALL for l in range(DEPTH)]
    out_shape = [jax.ShapeDtypeStruct(w[n].shape, F32) for n in SMALL for _ in range(4)]
    outs = pl.pallas_call(body, name="adam_small", out_shape=out_shape, in_specs=[vmem] * len(args),
                          out_specs=[vmem] * len(out_shape), compiler_params=_cparams())(*args)
    return {n: tuple(outs[4 * i:4 * i + 4]) for i, n in enumerate(SMALL)}


WEIGHTS = ("w_in", "ret_gn_g", "ret_gn_b", "sgu_ln_g", "sgu_ln_b", "sgu_w", "sgu_b", "p_ret", "p_sb", "p_sgu", "w_out",
           "ln1_g", "ln1_b", "w_up", "w_down", "ln2_g", "ln2_b")


def kernel(x, w_in, ret_gn_g, ret_gn_b, sgu_ln_g, sgu_ln_b, sgu_w, sgu_b, p_ret, p_sb, p_sgu, w_out, ln1_g, ln1_b, w_up, w_down, ln2_g, ln2_b, loss_target, m_w_in, m_ret_gn_g, m_ret_gn_b, m_sgu_ln_g, m_sgu_ln_b, m_sgu_w, m_sgu_b, m_p_ret, m_p_sb, m_p_sgu, m_w_out, m_ln1_g, m_ln1_b, m_w_up, m_w_down, m_ln2_g, m_ln2_b, v_w_in, v_ret_gn_g, v_ret_gn_b, v_sgu_ln_g, v_sgu_ln_b, v_sgu_w, v_sgu_b, v_p_ret, v_p_sb, v_p_sgu, v_w_out, v_ln1_g, v_ln1_b, v_w_up, v_w_down, v_ln2_g, v_ln2_b):
    w = dict(zip(WEIGHTS, (w_in, ret_gn_g, ret_gn_b, sgu_ln_g, sgu_ln_b, sgu_w, sgu_b, p_ret, p_sb, p_sgu, w_out,
                           ln1_g, ln1_b, w_up, w_down, ln2_g, ln2_b)))
    m = dict(zip(WEIGHTS, (m_w_in, m_ret_gn_g, m_ret_gn_b, m_sgu_ln_g, m_sgu_ln_b, m_sgu_w, m_sgu_b, m_p_ret, m_p_sb,
                           m_p_sgu, m_w_out, m_ln1_g, m_ln1_b, m_w_up, m_w_down, m_ln2_g, m_ln2_b)))
    v = dict(zip(WEIGHTS, (v_w_in, v_ret_gn_g, v_ret_gn_b, v_sgu_ln_g, v_sgu_ln_b, v_sgu_w, v_sgu_b, v_p_ret, v_p_sb,
                           v_p_sgu, v_w_out, v_ln1_g, v_ln1_b, v_w_up, v_w_down, v_ln2_g, v_ln2_b)))

    small = {n: w[n] for n in SMALL}
    shard = {n: _bf(w[n]) for n in BIG}
    shard["w_in"] = shard["w_in"].transpose(0, 2, 1)
    S = x.shape[1]
    x0, target = x.reshape(S, D_MODEL), loss_target.reshape(S, D_MODEL)
    tables = _ret_tables(S)
    Ws = [_small_weights(small, l) for l in range(DEPTH)]

    (Ws[0]["w_in_t"],) = _exchange([_gather_transfer(shard["w_in"], 0, "rows")], "gather_w_in0", relay=True)
    def gather_under_sb(keys):
        def landed_fn(landed):
            for (n, l), z in zip(keys, landed):
                Ws[l]["w_in_t" if n == "w_in" else n] = z

        return _Comm([_gather_transfer(shard[n], l, GATHER_KIND[n]) for n, l in keys], relay=True), landed_fn

    h, h_bf, saved0 = _layer_fwd(x0, _bf(x0), Ws[0], tables,
                                 sb_comm=gather_under_sb([(n, 0) for n in BIG[1:]] + [("w_in", 1)]))
    h, _, saved1 = _layer_fwd(h, h_bf, Ws[1], tables, sb_comm=gather_under_sb([(n, 1) for n in BIG[1:]]))
    d_h, g1, landed1 = _layer_bwd(h, Ws[1], tables, saved1, chunk_dtype=BF16, target=target,
                                  sb_comm_fn=lambda g: _Comm([_scatter_transfer(g[n]) for n in BIG[1:]]))
    loss = lax.psum(0.5 * jnp.sum(landed1["sq"]) / D_MODEL, ("x", "y", "c"))
    early = [("w_in", 1)] + [(n, 0) for n in BIG[1:]]

    def small_slabs(g):
        return [_slab_transfer(g[n].reshape(4, CHUNK) if n == "sgu_b" else g[n]) for n in SMALL]

    def early_scatter(g0):
        return _Comm([_scatter_transfer((g1 if l else g0)[n]) for n, l in early] + small_slabs(g1))

    def late_scatter(g0):
        pairs = _pair_reduce(g0["w_in"], "w_in0_pairs")
        return _Comm([_chip_scatter_transfer(pairs)])

    d_x, g0, landed = _layer_bwd(d_h, Ws[0], tables, saved0, chunk_dtype=BF16, sb_comm_fn=early_scatter,
                                 dwin_comm_fn=lambda g: _Comm(small_slabs(g)), dx_comm_fn=late_scatter)
    parts = {**dict(zip(early, landed["sb"])), **{(n, 1): z for n, z in zip(BIG[1:], landed1["sb"])}}
    parts[("w_in", 0)] = landed["dx"][0]
    small_parts = {**{(n, 1): z for n, z in zip(SMALL, landed["sb"][len(early):])},
                   **{(n, 0): z for n, z in zip(SMALL, landed["dwin"])}}

    grad, delta, new_m, new_v = {}, {}, {}, {}
    for n in BIG:
        view = (lambda a: a.transpose(0, 2, 1)) if n == "w_in" else (lambda a: a)
        res = _adam(view(w[n]), [parts[(n, l)] for l in range(DEPTH)], view(m[n]), view(v[n]), "adam_" + n)
        grad[n], delta[n], new_m[n], new_v[n] = (view(r) for r in res)
    for n, res in _adam_small(small, m, v, small_parts).items():
        grad[n], delta[n], new_m[n], new_v[n] = res

    return (loss, d_x.reshape(x.shape), *[grad[n] for n in WEIGHTS], *[delta[n] for n in WEIGHTS],
            *[new_m[n] for n in WEIGHTS], *[new_v[n] for n in WEIGHTS])
```

```python
import functools
import math

import numpy as np
import jax
import jax.numpy as jnp
from jax import lax
from jax.experimental import pallas as pl
from jax.experimental.pallas import tpu as pltpu

F32 = jnp.float32
BF16 = jnp.bfloat16

N_DEV = 8
DEPTH = 2
D_MODEL = 1024
CHUNK = 128
RET_W = 512
SB_W = 512
SGU_W = 512
N_IN = 7680
LN_EPS = 1e-5
ALPHA = (2 * DEPTH) ** 0.25
ROPE_BASE = 10000.0
ADAM_LR, ADAM_B1, ADAM_B2, ADAM_EPS, ADAM_WD, ADAM_STEP = 0.001, 0.9, 0.999, 1e-08, 0.01, 10
VMEM_LIMIT = 56 * 1024 * 1024

_GELU_K = math.sqrt(2.0 / math.pi)
_GELU_C = 0.044715


def _cparams(sem=None):
    return pltpu.CompilerParams(dimension_semantics=sem, vmem_limit_bytes=VMEM_LIMIT)


def _dg(a, b, ca, cb):
    return lax.dot_general(a, b, (((ca,), (cb,)), ((), ())), preferred_element_type=F32)


def _bf(x):
    return x.astype(BF16)


def _sigmoid(x):
    return 1.0 / (1.0 + jnp.exp(-x))


def _gelu(x):
    t = jnp.tanh(_GELU_K * (x + _GELU_C * (x * x * x)))
    return x * (0.5 * (1.0 + t))


def _gelu_grad(x):
    t = jnp.tanh(_GELU_K * (x + _GELU_C * (x * x * x)))
    return 0.5 * (1.0 + t) + 0.5 * x * (1.0 - t * t) * (_GELU_K * (1.0 + 3.0 * _GELU_C * x * x))


def _norm_stats(u):
    mu = jnp.mean(u, axis=-1, keepdims=True)
    d = u - mu
    var = jnp.mean(d * d, axis=-1, keepdims=True)
    rstd = lax.rsqrt(var + LN_EPS)
    return d * rstd, rstd


def _norm_bwd(dxh, xh, rstd):
    return rstd * (dxh - jnp.mean(dxh, axis=-1, keepdims=True) - xh * jnp.mean(dxh * xh, axis=-1, keepdims=True))


class _Transfer:
    def __init__(self, src, dst_shape, src_at, dst_at, same_core=False):
        self.src, self.dst_shape, self.src_at, self.dst_at = src, tuple(dst_shape), src_at, dst_at
        self.same_core = same_core


def _gather_transfer(shard, l, kind):
    _, r, c = shard.shape
    src_at = lambda ref, p: ref.at[l]
    if kind == "slab":
        return _Transfer(shard, (N_DEV, r, c), src_at, lambda ref, s: ref.at[s])
    if kind == "rows":
        return _Transfer(shard, (N_DEV * r, c), src_at, lambda ref, s: ref.at[pl.ds(pl.multiple_of(s * r, r), r), :])
    return _Transfer(shard, (r, N_DEV * c), src_at, lambda ref, s: ref.at[:, pl.ds(pl.multiple_of(s * c, c), c)])


def _scatter_transfer(chunks):
    return _Transfer(chunks, chunks.shape, lambda ref, p: ref.at[p], lambda ref, s: ref.at[s])


def _slab_transfer(arr):
    return _Transfer(arr, (N_DEV,) + arr.shape, lambda ref, p: ref, lambda ref, s: ref.at[s])


class _Comm:
    def __init__(self, transfers, relay=False):
        self.transfers = list(transfers)
        self.relay = relay
        self.n = len(self.transfers)
        self.arrays = [t.src for t in self.transfers]
        self.out_shape = [jax.ShapeDtypeStruct(t.dst_shape, t.src.dtype) for t in self.transfers]
        self.scratch = [pltpu.SemaphoreType.DMA((self.n * (N_DEV - 1),)), pltpu.SemaphoreType.DMA((self.n * (N_DEV - 1),)),
                        pltpu.SemaphoreType.DMA((self.n,))]

    def _relay_copies(self, srcs, dsts, send_sems, recv_sems, local_sems):
        x, y, c = lax.axis_index("x"), lax.axis_index("y"), lax.axis_index("c")
        me = 4 * x + 2 * y + c
        chips = [(1 - x, y), (x, 1 - y), (1 - x, 1 - y)]
        first, passed, own = [], [], []
        for t, tr in enumerate(self.transfers):
            def copy(k, src, sender, to, t=t, tr=tr):
                return pltpu.make_async_remote_copy(
                    src_ref=src, dst_ref=tr.dst_at(dsts[t], sender), send_sem=send_sems.at[t * (N_DEV - 1) + k],
                    recv_sem=recv_sems.at[t * (N_DEV - 1) + k], device_id=to, device_id_type=pl.DeviceIdType.MESH)

            mine = tr.src_at(srcs[t], me)
            first.append([copy(0, mine, me, (x, y, 1 - c))] + [copy(1 + j, mine, me, (px, py, c))
                                                                for j, (px, py) in enumerate(chips)])
            passed.append([copy(4 + j, tr.dst_at(dsts[t], 4 * px + 2 * py + c), 4 * px + 2 * py + c, (x, y, 1 - c))
                           for j, (px, py) in enumerate(chips)])
            own.append(pltpu.make_async_copy(mine, tr.dst_at(dsts[t], me), local_sems.at[t]))
        return first, passed, own

    def _copies(self, srcs, dsts, send_sems, recv_sems, local_sems):
        x, y, c = lax.axis_index("x"), lax.axis_index("y"), lax.axis_index("c")
        me = 4 * x + 2 * y + c
        copies = []
        for d in range(1, N_DEV):
            px = 1 - x if d & 4 else x
            py = 1 - y if d & 2 else y
            pc = 1 - c if d & 1 else c
            for t, tr in enumerate(self.transfers):
                if tr.same_core and d & 1:
                    continue
                peer, mine = (2 * px + py, 2 * x + y) if tr.same_core else (4 * px + 2 * py + pc, me)
                k = t * (N_DEV - 1) + d - 1
                copies.append(pltpu.make_async_remote_copy(
                    src_ref=tr.src_at(srcs[t], peer), dst_ref=tr.dst_at(dsts[t], mine),
                    send_sem=send_sems.at[k], recv_sem=recv_sems.at[k],
                    device_id=(px, py, pc), device_id_type=pl.DeviceIdType.MESH))
        own = []
        for t, tr in enumerate(self.transfers):
            mine = 2 * x + y if tr.same_core else me
            own.append(pltpu.make_async_copy(tr.src_at(srcs[t], mine), tr.dst_at(dsts[t], mine), local_sems.at[t]))
        return copies, own

    def start(self, srcs, dsts, *sems):
        if self.relay:
            first, _, own = self._relay_copies(srcs, dsts, *sems)
            for cp in own + [cp for per_t in first for cp in per_t]:
                cp.start()
            return
        copies, own = self._copies(srcs, dsts, *sems)
        for cp in own + copies:
            cp.start()

    def pass_on(self, srcs, dsts, *sems):
        if self.relay:
            first, passed, _ = self._relay_copies(srcs, dsts, *sems)
            for j in range(3):
                for t in range(self.n):
                    first[t][1 + j].wait_recv()
                    passed[t][j].start()

    def finish(self, srcs, dsts, *sems):
        if self.relay:
            first, passed, own = self._relay_copies(srcs, dsts, *sems)
            for t in range(self.n):
                first[t][0].wait_recv()
                for cp in passed[t]:
                    cp.wait_recv()
            for t in range(self.n):
                for cp in first[t] + passed[t]:
                    cp.wait_send()
                own[t].wait()
            return
        copies, own = self._copies(srcs, dsts, *sems)
        for cp in copies + own:
            cp.wait()


def _pcall(body, *, name, grid, in_specs, out_specs, out_shape, scratch_shapes, sem, args, comm=None):
    in_specs, out_specs, out_shape = list(in_specs), list(out_specs), list(out_shape)
    if comm is None:
        outs = pl.pallas_call(body, name=name, grid=grid, in_specs=in_specs, out_specs=out_specs, out_shape=out_shape,
                              scratch_shapes=list(scratch_shapes), compiler_params=_cparams(sem))(*args)
        return list(outs), []
    n_in, n_out, n_scr, k = len(in_specs), len(out_specs), len(scratch_shapes), comm.n

    def carrier(*refs):
        ins, cin = refs[:n_in], refs[n_in:n_in + k]
        outs, cout = refs[n_in + k:n_in + k + n_out], refs[n_in + k + n_out:n_in + 2 * k + n_out]
        scr, sems = refs[n_in + 2 * k + n_out:n_in + 2 * k + n_out + n_scr], refs[n_in + 2 * k + n_out + n_scr:]
        ids = [pl.program_id(d) for d in range(len(grid))]
        first = functools.reduce(jnp.logical_and, [i == 0 for i in ids])
        last = functools.reduce(jnp.logical_and, [i == g - 1 for i, g in zip(ids, grid)])

        @pl.when(first)
        def _():
            comm.start(cin, cout, *sems)

        body(*ins, *outs, *scr)

        early_pass = comm.relay and len(grid) > 1 and grid[0] > 1
        if early_pass:
            @pl.when(functools.reduce(jnp.logical_and, [ids[0] == grid[0] - 1] + [i == 0 for i in ids[1:]]))
            def _():
                comm.pass_on(cin, cout, *sems)

        @pl.when(last)
        def _():
            if not early_pass:
                comm.pass_on(cin, cout, *sems)
            comm.finish(cin, cout, *sems)

    hbm = pl.BlockSpec(memory_space=pl.ANY)
    outs = pl.pallas_call(
        carrier, name=name, grid=grid, in_specs=in_specs + [hbm] * k, out_specs=out_specs + [hbm] * k,
        out_shape=out_shape + comm.out_shape, scratch_shapes=list(scratch_shapes) + comm.scratch,
        compiler_params=_cparams(tuple("arbitrary" for _ in grid)),
    )(*args, *comm.arrays)
    return list(outs[:n_out]), list(outs[n_out:])


def _exchange(transfers, name, relay=False):
    comm = _Comm(transfers, relay)

    def body(*refs):
        k = comm.n
        comm.start(refs[:k], refs[k:2 * k], *refs[2 * k:])
        comm.pass_on(refs[:k], refs[k:2 * k], *refs[2 * k:])
        comm.finish(refs[:k], refs[k:2 * k], *refs[2 * k:])

    hbm = pl.BlockSpec(memory_space=pl.ANY)
    return pl.pallas_call(body, name=name, out_shape=comm.out_shape, in_specs=[hbm] * comm.n, out_specs=[hbm] * comm.n,
                          scratch_shapes=comm.scratch)(*comm.arrays)


def _pair_reduce(chunks, name, tr=320):
    _, r, c = chunks.shape
    tr = min(tr, r)
    assert r % tr == 0

    def swap(src_ref, dst_ref, send_sems, recv_sems):
        x, y, core = lax.axis_index("x"), lax.axis_index("y"), lax.axis_index("c")
        copies = [pltpu.make_async_remote_copy(
            src_ref=src_ref.at[2 * k + 1 - core], dst_ref=dst_ref.at[k], send_sem=send_sems.at[k],
            recv_sem=recv_sems.at[k], device_id=(x, y, 1 - core), device_id_type=pl.DeviceIdType.MESH) for k in range(4)]
        for cp in copies:
            cp.start()
        for cp in copies:
            cp.wait()

    hbm = pl.BlockSpec(memory_space=pl.ANY)
    theirs = pl.pallas_call(swap, name=name + "_swap", out_shape=jax.ShapeDtypeStruct((4, r, c), chunks.dtype),
                            in_specs=[hbm], out_specs=hbm,
                            scratch_shapes=[pltpu.SemaphoreType.DMA((4,)), pltpu.SemaphoreType.DMA((4,))])(chunks)

    def add(mine_ref, theirs_ref, out_ref):
        core = lax.axis_index("c")
        both = mine_ref[...].astype(F32)
        out_ref[...] = (jnp.where(core == 0, both[0], both[1]) + theirs_ref[...].astype(F32)).astype(out_ref.dtype)

    return pl.pallas_call(
        add, name=name + "_add", grid=(4, r // tr), out_shape=jax.ShapeDtypeStruct((4, r, c), chunks.dtype),
        in_specs=[pl.BlockSpec((None, 2, tr, c), lambda k, i: (k, 0, i, 0)), pl.BlockSpec((None, tr, c), lambda k, i: (k, i, 0))],
        out_specs=pl.BlockSpec((None, tr, c), lambda k, i: (k, i, 0)),
        compiler_params=_cparams(("parallel", "parallel")),
    )(chunks.reshape(4, 2, r, c), theirs)


def _chip_scatter_transfer(pairs):
    return _Transfer(pairs, pairs.shape, lambda ref, p: ref.at[p], lambda ref, s: ref.at[s], same_core=True)


def _matmul(a, b, mode, *, name, tm, tn, tk, epi=None, extra=(), out_dtype=F32, chunks=None, comm=None,
            b_resident=False):
    pieces = list(a) if isinstance(a, (list, tuple)) else [a]
    rows_a, cols_a = pieces[0].shape[0], sum(p.shape[1] for p in pieces)
    if mode == "nn":
        (M, K), N = (rows_a, cols_a), b.shape[1]
    elif mode == "nt":
        (M, K), N = (rows_a, cols_a), b.shape[0]
    else:
        (K, M), N = (rows_a, cols_a), b.shape[1]
    tm, tn, tk = min(tm, M), min(tn, N), min(tk, K)
    assert M % tm == 0 and N % tn == 0 and K % tk == 0 and (epi != "ln" or tn == N), (name, M, N, K)
    nk = K // tk
    tile_cols, axis = (tm, 0) if mode == "tn" else (tk, 2)
    assert all(p.shape[1] % tile_cols == 0 for p in pieces)
    counts = [p.shape[1] // tile_cols for p in pieces]
    starts = [sum(counts[:q]) for q in range(len(pieces))]

    def a_spec_of(q):
        at = lambda t: jnp.clip(t - starts[q], 0, counts[q] - 1) if len(pieces) > 1 else t
        return {"nn": pl.BlockSpec((tm, tk), lambda i, j, k: (i, at(k))),
                "nt": pl.BlockSpec((tm, tk), lambda i, j, k: (i, at(k))),
                "tn": pl.BlockSpec((tk, tm), lambda i, j, k: (k, at(i)))}[mode]

    n_a = len(pieces)
    b_mode = pl.Buffered(1) if (nk == 1 and tn == N and n_a > 1) else None
    b_spec = {"nn": pl.BlockSpec((tk, tn), lambda i, j, k: (k, j), pipeline_mode=b_mode),
              "nt": pl.BlockSpec((tn, tk), lambda i, j, k: (j, k), pipeline_mode=b_mode),
              "tn": pl.BlockSpec((tk, tn), lambda i, j, k: (k, j), pipeline_mode=b_mode)}[mode]
    if b_resident:
        assert mode == "nt" and nk == 1 and n_a == 1
        b_spec = pl.BlockSpec((N, K), lambda i, j, k: (0, 0), pipeline_mode=pl.Buffered(1))
    ca, cb = {"nn": (1, 0), "nt": (1, 1), "tn": (0, 0)}[mode]
    tile = pl.BlockSpec((tm, tn), lambda i, j, k: (i, j))
    row = pl.BlockSpec((1, tn), lambda i, j, k: (0, j))
    n_extra = {None: 0, "add": 1, "relu2": 0, "drelu2": 1, "ln": 3}[epi]
    assert len(extra) == n_extra
    extra_specs = {None: [], "add": [tile], "relu2": [], "drelu2": [tile], "ln": [tile, row, row]}[epi]
    split = 0
    if epi == "relu2":
        out_shape, out_specs = (jax.ShapeDtypeStruct((M, N), BF16),), (tile,)
    elif epi == "ln":
        out_shape = (jax.ShapeDtypeStruct((M, N), F32), jax.ShapeDtypeStruct((M, N), F32),
                     jax.ShapeDtypeStruct((M, N), BF16))
        out_specs = (tile, tile, tile)
    elif chunks == "cols":
        c = N // N_DEV
        out_shape = (jax.ShapeDtypeStruct((N_DEV, M, c), out_dtype),)
        if tn == N:
            split = c
            out_specs = (pl.BlockSpec((N_DEV, tm, c), lambda i, j, k: (0, i, 0)),)
        else:
            assert c % tn == 0
            out_specs = (pl.BlockSpec((None, tm, tn), lambda i, j, k: (j // (c // tn), i, j % (c // tn))),)
    else:
        out_shape, out_specs = (jax.ShapeDtypeStruct((M, N), out_dtype),), (tile,)
    n_out = len(out_shape)

    def body(*refs):
        a_refs, b_ref = refs[:n_a], refs[n_a]
        ex = refs[n_a + 1:n_a + 1 + n_extra]
        outs = refs[n_a + 1 + n_extra:n_a + 1 + n_extra + n_out]
        acc_ref = refs[-1]
        k = pl.program_id(2)

        def finish(acc):
            if epi == "add":
                outs[0][...] = (acc + ALPHA * ex[0][...]).astype(out_dtype)
            elif epi == "relu2":
                r = jnp.maximum(acc, 0.0)
                outs[0][...] = _bf(r * r)
            elif epi == "drelu2":
                outs[0][...] = (acc * (2.0 * jnp.sqrt(ex[0][...].astype(F32)))).astype(out_dtype)
            elif epi == "ln":
                u = ALPHA * ex[0][...] + acc
                xh, _ = _norm_stats(u)
                y = xh * ex[1][...] + ex[2][...]
                outs[0][...] = u
                outs[1][...] = y
                outs[2][...] = _bf(y)
            elif split:
                for p in range(N_DEV):
                    outs[0][p] = acc[:, p * split:(p + 1) * split].astype(out_dtype)
            else:
                outs[0][...] = acc.astype(out_dtype)

        def step(a_ref, first, middle, last):
            b_tile = b_ref[pl.ds(pl.multiple_of(pl.program_id(1) * tn, tn), tn), :] if b_resident else b_ref[...]
            part = _dg(_bf(a_ref[...]), _bf(b_tile), ca, cb)
            if nk == 1:
                finish(part)
                return
            if first:
                @pl.when(k == 0)
                def _():
                    acc_ref[...] = part

            if middle:
                @pl.when(jnp.logical_and(k > 0, k < nk - 1))
                def _():
                    acc_ref[...] += part

            if last:
                @pl.when(k == nk - 1)
                def _():
                    finish(acc_ref[...] + part)

        if n_a == 1:
            step(a_refs[0], True, True, True)
        else:
            t = pl.program_id(axis)
            for q in range(n_a):
                along_k = axis == 2
                first = not along_k or starts[q] == 0
                last = not along_k or starts[q] + counts[q] == nk
                middle = not along_k or counts[q] > int(first) + int(last)

                @pl.when(jnp.logical_and(t >= starts[q], t < starts[q] + counts[q]))
                def _(q=q, first=first, middle=middle, last=last):
                    step(a_refs[q], first, middle, last)

    outs, landed = _pcall(
        body, name=name, out_shape=out_shape, grid=(M // tm, N // tn, nk),
        in_specs=[a_spec_of(q) for q in range(n_a)] + [b_spec] + extra_specs, out_specs=out_specs,
        scratch_shapes=[pltpu.VMEM((tm, tn) if nk > 1 else (8, 128), F32)], sem=("parallel", "parallel", "arbitrary"),
        args=(*pieces, b, *extra), comm=comm)
    res = outs[0] if n_out == 1 else tuple(outs)
    if chunks == "rows":
        res = res.reshape(N_DEV, M // N_DEV, N)
    return res if comm is None else (res, landed)


def _matmul_rows_of(pieces, b, res, *, name, tm, comm=None):
    M, (K, N) = pieces[0].shape[0], b.shape
    tm = min(tm, M)
    subs, start = [], 0
    for q, p in enumerate(pieces):
        w = p.shape[1]
        step = w if start % w == 0 else 512
        assert w % step == 0 and start % step == 0
        subs += [(q, off, step, start + off) for off in range(0, w, step)]
        start += w
    assert start == K
    n_p, n_s = len(pieces), len(subs)

    def body(*refs):
        a_refs, b_refs, res_ref, out_ref = refs[:n_p], refs[n_p:n_p + n_s], refs[n_p + n_s], refs[n_p + n_s + 1]
        acc = None
        for (q, off, w, _), b_ref in zip(subs, b_refs):
            part = _dg(_bf(a_refs[q][:, off:off + w]), _bf(b_ref[...]), 1, 0)
            acc = part if acc is None else acc + part
        out_ref[...] = acc + ALPHA * res_ref[...]

    tile = pl.BlockSpec((tm, N), lambda i: (i, 0))
    outs, landed = _pcall(
        body, name=name, grid=(M // tm,), out_shape=[jax.ShapeDtypeStruct((M, N), F32)],
        in_specs=[pl.BlockSpec((tm, p.shape[1]), lambda i: (i, 0)) for p in pieces] +
                 [pl.BlockSpec((w, N), lambda i, r=row // w: (r, 0), pipeline_mode=pl.Buffered(1))
                  for _, _, w, row in subs] + [tile],
        out_specs=[tile], scratch_shapes=[], sem=("parallel",), args=(*pieces, *([b] * n_s), res), comm=comm)
    return outs[0] if comm is None else (outs[0], landed)


def _ret_tables(S):
    half = 64
    inv_freq = ROPE_BASE ** (-jnp.arange(half, dtype=F32) / half)
    ang = jnp.arange(S, dtype=jnp.int32).astype(F32)[:, None] * inv_freq[None, :]
    cos, sin = jnp.cos(ang), jnp.sin(ang)
    cosf = jnp.concatenate([cos, cos], axis=1)
    sinf = jnp.concatenate([-sin, sin], axis=1)
    log_g = jnp.log(1.0 - 2.0 ** (-5.0 - jnp.arange(4, dtype=F32)))
    idx = jnp.arange(CHUNK, dtype=F32)
    diff = idx[:, None] - idx[None, :]
    md = jnp.where(diff[None] >= 0, jnp.exp(log_g[:, None, None] * diff[None]), 0.0)
    kd = jnp.exp(log_g[:, None] * (CHUNK - 1 - idx)[None, :])
    qd = jnp.exp(log_g[:, None] * (idx + 1.0)[None, :])
    cd = jnp.exp(log_g * CHUNK)
    bc = lambda t: jnp.broadcast_to(t[:, :, None], (4, CHUNK, CHUNK))
    return cosf, sinf, md, bc(qd), bc(kd), jnp.broadcast_to(cd[:, None, None], (4, 8, CHUNK))


def _rot(x, cosf, sinf):
    return x * cosf + pltpu.roll(x, 64, 1) * sinf


def _rot_t(dx, cosf, sinf):
    return dx * cosf - pltpu.roll(dx, 64, 1) * sinf


RET_CHUNKS = 2


def _ret_specs(rev, S):
    R = min(RET_CHUNKS, S // CHUNK)
    rows, steps = R * CHUNK, S // (R * CHUNK)
    rn = (lambda n: steps - 1 - n) if rev else (lambda n: n)
    col = lambda c: pl.BlockSpec((rows, 512), lambda n, c=c: (rn(n), c))
    tab = pl.BlockSpec((rows, CHUNK), lambda n: (rn(n), 0))
    dec = pl.BlockSpec((4, CHUNK, CHUNK), lambda n: (0, 0, 0))
    cdec = pl.BlockSpec((4, 8, CHUNK), lambda n: (0, 0, 0))
    vec = pl.BlockSpec((1, 512), lambda n: (0, 0))
    st = pl.BlockSpec((R, 4, CHUNK, CHUNK), lambda n: (rn(n), 0, 0, 0))
    return R, steps, rn, col, tab, dec, cdec, vec, st


def _ret_fwd(proj, tables, gn_g, gn_b):
    S = proj.shape[0]
    R, steps, _, col, tab, dec, cdec, vec, st = _ret_specs(False, S)

    def body(q_ref, k_ref, v_ref, g_ref, cos_ref, sin_ref, md_ref, qd_ref, kd_ref, cd_ref, gng_ref, gnb_ref,
             out_ref, st_ref, state):
        @pl.when(pl.program_id(0) == 0)
        def _():
            state[...] = jnp.zeros_like(state)

        tiles = [(c, h) for c in range(R) for h in range(4)]
        rs = lambda c: slice(c * CHUNK, (c + 1) * CHUNK)
        sl = lambda h: slice(h * 128, (h + 1) * 128)
        qr = [_rot(q_ref[rs(c), sl(h)].astype(F32), cos_ref[rs(c), :], sin_ref[rs(c), :]) for c, h in tiles]
        kr = [_rot(k_ref[rs(c), sl(h)].astype(F32), cos_ref[rs(c), :], sin_ref[rs(c), :]) * (128 ** -0.5)
              for c, h in tiles]
        vb = [_bf(v_ref[rs(c), sl(h)]) for c, h in tiles]
        kv = [_dg(_bf(k * kd_ref[h]), v, 0, 0) for k, v, (c, h) in zip(kr, vb, tiles)]
        before = {}
        for h in range(4):
            s_h = state[h]
            for c in range(R):
                st_ref[c, h] = s_h
                before[(c, h)] = s_h
                s_h = s_h * cd_ref[h, 0:1, :] + kv[c * 4 + h]
            state[h] = s_h
        sc = [_dg(_bf(q), _bf(k), 1, 1) * md_ref[h] for q, k, (c, h) in zip(qr, kr, tiles)]
        r = [_dg(_bf(x), v, 1, 0) + _dg(_bf(q * qd_ref[h]), _bf(before[(c, h)]), 1, 0)
             for x, v, q, (c, h) in zip(sc, vb, qr, tiles)]
        for x, (c, h) in zip(r, tiles):
            y, _ = _norm_stats(x)
            rg = g_ref[rs(c), sl(h)].astype(F32)
            out_ref[rs(c), sl(h)] = rg * _sigmoid(rg) * (y * gng_ref[:, sl(h)] + gnb_ref[:, sl(h)])

    return pl.pallas_call(
        body, name="ret_fwd", grid=(steps,),
        out_shape=(jax.ShapeDtypeStruct((S, RET_W), F32), jax.ShapeDtypeStruct((S // CHUNK, 4, CHUNK, CHUNK), F32)),
        in_specs=[col(0), col(1), col(2), col(3), tab, tab, dec, dec, dec, cdec, vec, vec],
        out_specs=(pl.BlockSpec((R * CHUNK, 512), lambda n: (n, 0)), st),
        scratch_shapes=[pltpu.VMEM((4, CHUNK, CHUNK), F32)],
        compiler_params=_cparams(("arbitrary",)),
    )(proj, proj, proj, proj, *tables, gn_g, gn_b)


def _ret_bwd(proj, tables, gn_g, gn_b, states, d_out):
    S = proj.shape[0]
    R, steps, rn, col, tab, dec, cdec, vec, st = _ret_specs(True, S)

    def kernel_body(q_ref, k_ref, v_ref, g_ref, cos_ref, sin_ref, md_ref, qd_ref, kd_ref, cd_ref, gng_ref, gnb_ref,
                    st_ref, do_ref, dp_ref, dg_ref, db_ref, gstate):
        @pl.when(pl.program_id(0) == 0)
        def _():
            gstate[...] = jnp.zeros_like(gstate)
            dg_ref[...] = jnp.zeros_like(dg_ref)
            db_ref[...] = jnp.zeros_like(db_ref)

        tiles = [(c, h) for c in range(R) for h in range(4)]
        rs = lambda c: slice(c * CHUNK, (c + 1) * CHUNK)
        sl = lambda h: slice(h * 128, (h + 1) * 128)
        rot = lambda ref, c, h: _rot(ref[rs(c), sl(h)].astype(F32), cos_ref[rs(c), :], sin_ref[rs(c), :])
        qr = [rot(q_ref, c, h) for c, h in tiles]
        kr = [rot(k_ref, c, h) * (128 ** -0.5) for c, h in tiles]
        qb, kb = [_bf(x) for x in qr], [_bf(x) for x in kr]
        vb = [_bf(v_ref[rs(c), sl(h)]) for c, h in tiles]
        s0b = [_bf(st_ref[c, h]) for c, h in tiles]
        scb = [_bf(_dg(q, k, 1, 1) * md_ref[h]) for q, k, (c, h) in zip(qb, kb, tiles)]
        qdb = [_bf(q * qd_ref[h]) for q, (c, h) in zip(qr, tiles)]
        kdb = [_bf(k * kd_ref[h]) for k, (c, h) in zip(kr, tiles)]
        r = [_dg(x, v, 1, 0) + _dg(q, s, 1, 0) for x, v, q, s in zip(scb, vb, qdb, s0b)]
        drb, d_rg = [], []
        for x, (c, h) in zip(r, tiles):
            y, rstd = _norm_stats(x)
            gng = gng_ref[:, sl(h)]
            rg = g_ref[rs(c), sl(h)].astype(F32)
            sg = _sigmoid(rg)
            d_o = do_ref[rs(c), sl(h)]
            d_gn = d_o * (rg * sg)
            dg_ref[:, sl(h)] += jnp.sum(d_gn * y, axis=0, keepdims=True)
            db_ref[:, sl(h)] += jnp.sum(d_gn, axis=0, keepdims=True)
            drb.append(_bf(_norm_bwd(d_gn * gng, y, rstd)))
            d_rg.append(_bf(d_o * (y * gng + gnb_ref[:, sl(h)]) * (sg * (1.0 + rg * (1.0 - sg)))))
        grow = [_dg(q, d, 0, 0) for q, d in zip(qdb, drb)]
        after = {}
        for h in range(4):
            g_h = gstate[h]
            for c in reversed(range(R)):
                after[(c, h)] = _bf(g_h)
                g_h = g_h * cd_ref[h, 0:1, :] + grow[c * 4 + h]
            gstate[h] = g_h
        dscb = [_bf(_dg(d, v, 1, 1) * md_ref[h]) for d, v, (c, h) in zip(drb, vb, tiles)]
        for t, (c, h) in enumerate(tiles):
            gb = after[(c, h)]
            dqr = _dg(dscb[t], kb[t], 1, 0) + _dg(drb[t], s0b[t], 1, 1) * qd_ref[h]
            dkr = _dg(dscb[t], qb[t], 0, 0) + _dg(vb[t], gb, 1, 1) * kd_ref[h]
            dv = _dg(scb[t], drb[t], 0, 0) + _dg(kdb[t], gb, 1, 0)
            cosf, sinf = cos_ref[rs(c), :], sin_ref[rs(c), :]
            dp_ref[rs(c), 0 * 512 + h * 128:0 * 512 + (h + 1) * 128] = _bf(_rot_t(dqr, cosf, sinf))
            dp_ref[rs(c), 1 * 512 + h * 128:1 * 512 + (h + 1) * 128] = _bf(_rot_t(dkr, cosf, sinf) * (128 ** -0.5))
            dp_ref[rs(c), 2 * 512 + h * 128:2 * 512 + (h + 1) * 128] = _bf(dv)
            dp_ref[rs(c), 3 * 512 + h * 128:3 * 512 + (h + 1) * 128] = d_rg[t]

    acc = pl.BlockSpec((1, 512), lambda n: (0, 0))
    return pl.pallas_call(
        kernel_body, name="ret_bwd", grid=(steps,),
        out_shape=(jax.ShapeDtypeStruct((S, 2048), BF16), jax.ShapeDtypeStruct((1, 512), F32),
                   jax.ShapeDtypeStruct((1, 512), F32)),
        in_specs=[col(0), col(1), col(2), col(3), tab, tab, dec, dec, dec, cdec, vec, vec, st,
                  pl.BlockSpec((R * CHUNK, 512), lambda n: (rn(n), 0))],
        out_specs=(pl.BlockSpec((R * CHUNK, 2048), lambda n: (rn(n), 0)), acc, acc),
        scratch_shapes=[pltpu.VMEM((4, CHUNK, CHUNK), F32)],
        compiler_params=_cparams(("arbitrary",)),
    )(proj, proj, proj, proj, *tables, gn_g, gn_b, states, d_out)


SB_T = 256
SB_SCALE = 64 ** -0.5
SB_Q_COL, SB_K_COL, SB_V_COL = 2048 // 128, 2560 // 128, 3072 // 128


def _head_masks():
    lane = lax.broadcasted_iota(jnp.int32, (1, 128), 1)
    m0 = (lane < 64).astype(F32)
    return m0, 1.0 - m0


def _tri(n, cmp):
    r = lax.broadcasted_iota(jnp.int32, (n, n), 0)
    c = lax.broadcasted_iota(jnp.int32, (n, n), 1)
    return cmp(r, c)


def _tri_sum(x, tri):
    hi = _bf(x)
    lo = _bf(x - hi.astype(F32))
    return _dg(hi, tri, 1, 0) + _dg(lo, tri, 1, 0)


def _sb_weights(qms, kblks, upper, carry, causal):
    tiles = [(b, h) for b in range(len(kblks)) for h in range(2)]
    zs = [_dg(qms[h], kblks[b], 1, 1) for b, h in tiles]
    lgs = [-(jnp.maximum(z, 0.0) + jnp.log(1.0 + jnp.exp(-jnp.abs(z)))) for z in zs]
    if causal is not None:
        lgs = [jnp.where(causal, lg, 0.0) if b == 0 else lg for lg, (b, h) in zip(lgs, tiles)]
    carries = list(carry)
    for t in range(len(tiles) - 2):
        carries.append(carries[t] + jnp.sum(lgs[t], axis=1, keepdims=True))
    his = [_bf(lg) for lg in lgs]
    los = [_bf(lg - hi.astype(F32)) for lg, hi in zip(lgs, his)]
    later = [_dg(hi, upper, 1, 0) for hi in his]
    later = [r + _dg(lo, upper, 1, 0) for r, lo in zip(later, los)]
    a = [jnp.exp(lg + z + (r + c)) for lg, z, r, c in zip(lgs, zs, later, carries)]
    if causal is not None:
        a = [jnp.where(causal, x, 0.0) if b == 0 else x for x, (b, h) in zip(a, tiles)]
    out = tuple(carries[t] + jnp.sum(lgs[t], axis=1, keepdims=True) for t in (len(tiles) - 2, len(tiles) - 1))
    return [a[2 * b:2 * b + 2] for b in range(len(kblks))], out


def _sb_fwd(proj, comm=None):
    S = proj.shape[0]
    T = min(SB_T, S)
    nq = S // T

    def body(q_ref, k_ref, v_ref, o_ref, a_ref, kb_ref, vm_ref, acc_ref):
        i = pl.program_id(1)
        m0, m1 = _head_masks()

        @pl.when(i == 0)
        def _():
            v = v_ref[...]
            kb_ref[...] = _bf(k_ref[...])
            vm_ref[0] = _bf(v * m0)
            vm_ref[1] = _bf(v * m1)

        q = q_ref[...]
        qm = (_bf(q * (m0 * SB_SCALE)), _bf(q * (m1 * SB_SCALE)))
        upper = _tri(T, lambda r, c: r > c).astype(BF16)
        causal = _tri(T, lambda r, c: c < r)

        def tiles(js, carry, mask, first):
            ks = [pl.multiple_of(j * T, T) for j in js]
            a, out = _sb_weights(qm, [kb_ref[pl.ds(k, T), :] for k in ks], upper, carry, mask)
            a = [[_bf(t) for t in per_block] for per_block in a]
            for b, j in enumerate(js):
                for h in range(2):
                    a_ref[h, j] = a[b][h]
            parts = [_dg(a[b][h], vm_ref[h, pl.ds(k, T), :], 1, 0) for b, k in enumerate(ks) for h in range(2)]
            part = functools.reduce(lambda u, w: u + w, parts)
            if first:
                acc_ref[...] = part
            else:
                acc_ref[...] += part
            return out

        zero = jnp.zeros((T, 1), F32)
        carry = lax.cond(i == 0, lambda: tiles([i], (zero, zero), causal, True),
                         lambda: tiles([i, i - 1], (zero, zero), causal, True))
        n = jnp.maximum(i - 1, 0)
        carry = lax.fori_loop(0, n % 2, lambda _, c: tiles([n - 1], c, None, False), carry)
        top = n - 1 - n % 2
        carry = lax.fori_loop(0, (n // 2) % 2, lambda _, c: tiles([top, top - 1], c, None, False), carry)
        top = top - 2 * ((n // 2) % 2)
        lax.fori_loop(0, n // 4, lambda jj, c: tiles([top - 4 * jj - b for b in range(4)], c, None, False), carry)
        o_ref[...] = acc_ref[...]

    full = lambda c: pl.BlockSpec((S, 128), lambda p, i, c=c: (0, c + p))
    outs, landed = _pcall(
        body, name="sb_fwd", grid=(4, nq),
        out_shape=[jax.ShapeDtypeStruct((S, SB_W), F32), jax.ShapeDtypeStruct((4, 2, nq, nq, T, T), BF16)],
        in_specs=[pl.BlockSpec((T, 128), lambda p, i: (i, SB_Q_COL + p)), full(SB_K_COL), full(SB_V_COL)],
        out_specs=[pl.BlockSpec((T, 128), lambda p, i: (i, p)),
                   pl.BlockSpec((None, 2, None, nq, T, T), lambda p, i: (p, 0, i, 0, 0, 0))],
        scratch_shapes=[pltpu.VMEM((S, 128), BF16), pltpu.VMEM((2, S, 128), BF16), pltpu.VMEM((T, 128), F32)],
        sem=("arbitrary", "arbitrary"), args=(proj, proj, proj), comm=comm)
    return tuple(outs) if comm is None else (tuple(outs), landed)


def _sb_bwd(proj, a_saved, d_o, comm=None):
    S = proj.shape[0]
    T = min(SB_T, S)
    nq = S // T

    def body(q_ref, k_ref, v_ref, do_ref, a_ref, dq_ref, dk_ref, dv_ref, kb_ref, kbm_ref, vb_ref, dq_acc, dk_acc, dv_acc):
        i = pl.program_id(1)
        m0, m1 = _head_masks()

        @pl.when(i == 0)
        def _():
            k = k_ref[...]
            kb_ref[...] = _bf(k)
            kbm_ref[0] = _bf(k * m0)
            kbm_ref[1] = _bf(k * m1)
            vb_ref[...] = _bf(v_ref[...])
            dk_acc[...] = jnp.zeros_like(dk_acc)
            dv_acc[...] = jnp.zeros_like(dv_acc)

        q, d_out = q_ref[...], do_ref[...]
        qm = (_bf(q * (m0 * SB_SCALE)), _bf(q * (m1 * SB_SCALE)))
        dom = (_bf(d_out * m0), _bf(d_out * m1))
        qm_t = tuple(_bf((q * (m * SB_SCALE)).T) for m in (m0, m1))
        dom_t = tuple(_bf((d_out * m).T) for m in (m0, m1))
        lower = _tri(T, lambda r, c: r < c).astype(BF16)
        causal = _tri(T, lambda r, c: c < r)

        def up(js, carry, mask):
            ks = [pl.multiple_of(j * T, T) for j in js]
            tiles = [(b, h) for b in range(len(js)) for h in range(2)]
            zs = [_dg(qm[h], kb_ref[pl.ds(ks[b], T), :], 1, 1) for b, h in tiles]
            a = [a_ref[h, js[b]] for b, h in tiles]
            es = [w.astype(F32) * _dg(dom[h], vb_ref[pl.ds(ks[b], T), :], 1, 1) for w, (b, h) in zip(a, tiles)]
            carries = list(carry)
            for t in range(len(tiles)):
                carries.append(carries[t] + jnp.sum(es[t], axis=1, keepdims=True))
            d_lg = [_dg(_bf(e), lower, 1, 0) + c for e, c in zip(es, carries)]
            ens = [jnp.exp(-jnp.abs(z)) for z in zs]
            invs = [1.0 / (1.0 + en) for en in ens]
            betas = [jnp.where(z >= 0.0, inv, en * inv) for z, en, inv in zip(zs, ens, invs)]
            dzs = [e * (1.0 - b) - d * b for e, b, d in zip(es, betas, d_lg)]
            if mask is not None:
                dzs = [jnp.where(mask, dz, 0.0) if b == len(js) - 1 else dz for dz, (b, h) in zip(dzs, tiles)]
            dzs = [_bf(dz) for dz in dzs]
            parts = [_dg(dzs[t], kbm_ref[h, pl.ds(ks[b], T), :], 1, 0) for t, (b, h) in enumerate(tiles)]
            dq_acc[...] += functools.reduce(lambda u, w: u + w, parts)
            for b, j in enumerate(js):
                dk_acc[j] += _dg(qm_t[0], dzs[2 * b], 1, 0) + _dg(qm_t[1], dzs[2 * b + 1], 1, 0)
                dv_acc[j] += _dg(dom_t[0], a[2 * b], 1, 0) + _dg(dom_t[1], a[2 * b + 1], 1, 0)
            return tuple(carries[-2:])

        zero = jnp.zeros((T, 1), F32)
        dq_acc[...] = jnp.zeros_like(dq_acc)
        n = jnp.maximum(i - 1, 0)
        carry = lax.fori_loop(0, n // 4, lambda jj, c: up([4 * jj + b for b in range(4)], c, None), (zero, zero))
        done = 4 * (n // 4)
        carry = lax.fori_loop(0, (n // 2) % 2, lambda _, c: up([done, done + 1], c, None), carry)
        carry = lax.fori_loop(0, n % 2, lambda _, c: up([n - 1], c, None), carry)

        @pl.when(i == 0)
        def _():
            up([i], carry, causal)

        @pl.when(i > 0)
        def _():
            up([i - 1, i], carry, causal)

        dq_ref[...] = _bf(dq_acc[...] * SB_SCALE)

        @pl.when(i == nq - 1)
        def _():
            for j in range(nq):
                dk_ref[j * T:(j + 1) * T, :] = _bf(dk_acc[j].T)
                dv_ref[j * T:(j + 1) * T, :] = _bf(dv_acc[j].T)

    full = lambda c: pl.BlockSpec((S, 128), lambda p, i, c=c: (0, c + p))
    tile = pl.BlockSpec((T, 128), lambda p, i: (i, p))
    acc = pl.BlockSpec((S, 128), lambda p, i: (0, p))
    out = jax.ShapeDtypeStruct((S, SB_W), BF16)
    outs, landed = _pcall(
        body, name="sb_bwd", grid=(4, nq), out_shape=[out, out, out],
        in_specs=[pl.BlockSpec((T, 128), lambda p, i: (i, SB_Q_COL + p)), full(SB_K_COL), full(SB_V_COL), tile,
                  pl.BlockSpec((None, 2, None, nq, T, T), lambda p, i: (p, 0, i, 0, 0, 0))],
        out_specs=[tile, acc, acc],
        scratch_shapes=[pltpu.VMEM((S, 128), BF16), pltpu.VMEM((2, S, 128), BF16), pltpu.VMEM((S, 128), BF16),
                        pltpu.VMEM((T, 128), F32), pltpu.VMEM((nq, 128, T), F32), pltpu.VMEM((nq, 128, T), F32)],
        sem=("arbitrary", "arbitrary"), args=(proj, proj, proj, d_o, a_saved), comm=comm)
    return tuple(outs) if comm is None else (tuple(outs), landed)


SGU_U_COL, SGU_V_COL = 3584 // 512, 4096 // 512


def _causal(w):
    r = lax.broadcasted_iota(jnp.int32, (CHUNK, CHUNK), 0)
    c = lax.broadcasted_iota(jnp.int32, (CHUNK, CHUNK), 1)
    return jnp.where(r >= c, w, 0.0)


SGU_CHUNKS = 4


def _sgu_fwd(proj, ln_g, ln_b, w, b):
    S = proj.shape[0]
    R = min(SGU_CHUNKS, S // CHUNK)
    rows = R * CHUNK

    def body(u_ref, v_ref, g_ref, b_ref, w_ref, bias_ref, out_ref):
        wc = [_bf(_causal(w_ref[g])) for g in range(4)]
        for r in range(R):
            rs = slice(r * CHUNK, (r + 1) * CHUNK)
            u = _gelu(u_ref[rs, :].astype(F32))
            xh, _ = _norm_stats(_gelu(v_ref[rs, :].astype(F32)))
            vn = _bf(xh * g_ref[...] + b_ref[...])
            for g in range(4):
                sl = slice(g * 128, (g + 1) * 128)
                out_ref[rs, sl] = u[:, sl] * (_dg(wc[g], vn[:, sl], 1, 0) + bias_ref[g])

    vec = pl.BlockSpec((1, 512), lambda n: (0, 0))
    return pl.pallas_call(
        body, name="sgu_fwd", grid=(S // rows,),
        out_shape=jax.ShapeDtypeStruct((S, SGU_W), F32),
        in_specs=[pl.BlockSpec((rows, 512), lambda n: (n, SGU_U_COL)),
                  pl.BlockSpec((rows, 512), lambda n: (n, SGU_V_COL)), vec, vec,
                  pl.BlockSpec((4, CHUNK, CHUNK), lambda n: (0, 0, 0)), pl.BlockSpec((4, CHUNK, 1), lambda n: (0, 0, 0))],
        out_specs=pl.BlockSpec((rows, 512), lambda n: (n, 0)),
        compiler_params=_cparams(("parallel",)),
    )(proj, proj, ln_g, ln_b, w, b)


def _sgu_bwd(proj, ln_g, ln_b, w, b, d_out):
    S = proj.shape[0]
    R = min(SGU_CHUNKS, S // CHUNK)
    rows = R * CHUNK

    def body(u_ref, v_ref, g_ref, b_ref, w_ref, bias_ref, do_ref, dp_ref, dg_ref, db_ref, dw_ref, dbias_ref):
        @pl.when(pl.program_id(0) == 0)
        def _():
            dg_ref[...] = jnp.zeros_like(dg_ref)
            db_ref[...] = jnp.zeros_like(db_ref)
            dw_ref[...] = jnp.zeros_like(dw_ref)
            dbias_ref[...] = jnp.zeros_like(dbias_ref)

        ln_gain = g_ref[...]
        wc = [_bf(_causal(w_ref[g])) for g in range(4)]
        for r in range(R):
            rs = slice(r * CHUNK, (r + 1) * CHUNK)
            gu, gv = u_ref[rs, :].astype(F32), v_ref[rs, :].astype(F32)
            u = _gelu(gu)
            xh, rstd = _norm_stats(_gelu(gv))
            vn = _bf(xh * ln_gain + b_ref[...])
            d_o = do_ref[rs, :]
            d_vn = []
            for g in range(4):
                sl = slice(g * 128, (g + 1) * 128)
                sv = _dg(wc[g], vn[:, sl], 1, 0) + bias_ref[g]
                dp_ref[rs, sl] = _bf(d_o[:, sl] * sv * _gelu_grad(gu[:, sl]))
                d_sv = d_o[:, sl] * u[:, sl]
                dbias_ref[g] += jnp.sum(d_sv, axis=1, keepdims=True)
                d_svb = _bf(d_sv)
                dw_ref[g] += _causal(_dg(d_svb, vn[:, sl], 1, 1))
                d_vn.append(_dg(wc[g], d_svb, 0, 0))
            d_vn = jnp.concatenate(d_vn, axis=1)
            dg_ref[...] += jnp.sum(d_vn * xh, axis=0, keepdims=True)
            db_ref[...] += jnp.sum(d_vn, axis=0, keepdims=True)
            dp_ref[rs, 512:1024] = _bf(_norm_bwd(d_vn * ln_gain, xh, rstd) * _gelu_grad(gv))

    vec = pl.BlockSpec((1, 512), lambda n: (0, 0))
    wspec = pl.BlockSpec((4, CHUNK, CHUNK), lambda n: (0, 0, 0))
    bspec = pl.BlockSpec((4, CHUNK, 1), lambda n: (0, 0, 0))
    return pl.pallas_call(
        body, name="sgu_bwd", grid=(S // rows,),
        out_shape=(jax.ShapeDtypeStruct((S, 1024), BF16), jax.ShapeDtypeStruct((1, 512), F32),
                   jax.ShapeDtypeStruct((1, 512), F32), jax.ShapeDtypeStruct((4, CHUNK, CHUNK), F32),
                   jax.ShapeDtypeStruct((4, CHUNK, 1), F32)),
        in_specs=[pl.BlockSpec((rows, 512), lambda n: (n, SGU_U_COL)),
                  pl.BlockSpec((rows, 512), lambda n: (n, SGU_V_COL)), vec, vec, wspec, bspec,
                  pl.BlockSpec((rows, 512), lambda n: (n, 0))],
        out_specs=(pl.BlockSpec((rows, 1024), lambda n: (n, 0)), vec, vec, wspec, bspec),
        compiler_params=_cparams(("arbitrary",)),
    )(proj, proj, ln_g, ln_b, w, b, d_out)


GATE_COL = 4608 // 512


def _merge_fwd(proj, branches, p_list, tm=1024):
    S = proj.shape[0]
    tm = min(tm, S)

    def body(r_ref, s_ref, g_ref, pr_ref, ps_ref, pg_ref, gr_ref, gs_ref, gg_ref, m_ref, br_ref):
        acc = None
        for k, (x_ref, p_ref, gate_ref) in enumerate(((r_ref, pr_ref, gr_ref), (s_ref, ps_ref, gs_ref),
                                                      (g_ref, pg_ref, gg_ref))):
            br = _dg(_bf(x_ref[...]), _bf(p_ref[...]), 1, 0)
            br_ref[k] = _bf(br)
            term = _sigmoid(gate_ref[...].astype(F32)) * br
            acc = term if acc is None else acc + term
        m_ref[...] = _bf(acc)

    xs = pl.BlockSpec((tm, 512), lambda i, n: (i, 0))
    ps = pl.BlockSpec((512, 512), lambda i, n: (0, n))
    gate = lambda k: pl.BlockSpec((tm, 512), lambda i, n, k=k: (i, GATE_COL + 2 * k + n))
    return pl.pallas_call(
        body, name="merge_fwd", grid=(S // tm, 2),
        out_shape=(jax.ShapeDtypeStruct((S, D_MODEL), BF16), jax.ShapeDtypeStruct((3, S, D_MODEL), BF16)),
        in_specs=[xs, xs, xs, ps, ps, ps, gate(0), gate(1), gate(2)],
        out_specs=(pl.BlockSpec((tm, 512), lambda i, n: (i, n)), pl.BlockSpec((3, tm, 512), lambda i, n: (0, i, n))),
        compiler_params=_cparams(("parallel", "parallel")),
    )(*branches, *p_list, proj, proj, proj)


def _gate_bwd(proj, br, d_merged, tm=1024):
    S = proj.shape[0]
    tm = min(tm, S)

    def body(dm_ref, br_ref, gr_ref, gs_ref, gg_ref, *out_refs):
        dm = dm_ref[...]
        for k, gate_ref in enumerate((gr_ref, gs_ref, gg_ref)):
            s = _sigmoid(gate_ref[...].astype(F32))
            out_refs[k][...] = _bf(dm * s)
            out_refs[3 + k][...] = _bf(dm * br_ref[k].astype(F32) * (s * (1.0 - s)))

    gate = lambda k: pl.BlockSpec((tm, 512), lambda i, n, k=k: (i, GATE_COL + 2 * k + n))
    three = pl.BlockSpec((3, tm, 512), lambda i, n: (0, i, n))
    tile = pl.BlockSpec((tm, 512), lambda i, n: (i, n))
    outs = pl.pallas_call(
        body, name="gate_bwd", grid=(S // tm, 2),
        out_shape=[jax.ShapeDtypeStruct((S, D_MODEL), BF16)] * 6,
        in_specs=[tile, three, gate(0), gate(1), gate(2)], out_specs=[tile] * 6,
        compiler_params=_cparams(("parallel", "parallel")),
    )(d_merged, br, proj, proj, proj)
    return outs[:3], outs[3:]


def _ln_bwd(dy, u, g, target=None, tm=512):
    S, D = u.shape
    tm = min(tm, S)
    loss = target is not None

    def body(*refs):
        dy_ref, u_ref, g_ref = refs[:3]
        du_ref, dub_ref, dg_ref, db_ref = refs[3 + loss:7 + loss]

        @pl.when(pl.program_id(0) == 0)
        def _():
            for acc_ref in refs[5 + loss:]:
                acc_ref[...] = jnp.zeros_like(acc_ref)

        dy_t = dy_ref[...]
        if loss:
            err = dy_t - refs[3][...]
            refs[-1][...] += jnp.sum(err * err, axis=0, keepdims=True)
            dy_t = err * (1.0 / D)
        xh, rstd = _norm_stats(u_ref[...])
        dg_ref[...] += jnp.sum(dy_t * xh, axis=0, keepdims=True)
        db_ref[...] += jnp.sum(dy_t, axis=0, keepdims=True)
        du = _norm_bwd(dy_t * g_ref[...], xh, rstd)
        du_ref[...] = du
        dub_ref[...] = _bf(du)

    tile = pl.BlockSpec((tm, D), lambda i: (i, 0))
    vec = pl.BlockSpec((1, D), lambda i: (0, 0))
    row = jax.ShapeDtypeStruct((1, D), F32)
    return pl.pallas_call(
        body, name="ln_bwd", grid=(S // tm,),
        out_shape=[jax.ShapeDtypeStruct((S, D), F32), jax.ShapeDtypeStruct((S, D), BF16)] + [row] * (2 + loss),
        in_specs=[tile, tile, vec] + [tile] * loss, out_specs=[tile, tile] + [vec] * (2 + loss),
        compiler_params=_cparams(("arbitrary",)),
    )(dy, u, g, *([target] if loss else []))


def _layer_fwd(x, x_bf, W, tables, sb_comm=None):
    proj = _matmul(x_bf, W["w_in_t"], "nt", name="proj", tm=1024, tn=2560, tk=1024, b_resident=True)
    retg, states = _ret_fwd(proj, tables, W["ret_gn_g"], W["ret_gn_b"])
    if sb_comm is None:
        sb, sb_a = _sb_fwd(proj)
    else:
        (sb, sb_a), landed = _sb_fwd(proj, comm=sb_comm[0])
        sb_comm[1](landed)
    sg = _sgu_fwd(proj, W["sgu_ln_g"], W["sgu_ln_b"], W["sgu_w"], W["sgu_b"])
    merged, br = _merge_fwd(proj, (retg, sb, sg), (W["p_ret"], W["p_sb"], W["p_sgu"]))
    u1, x1, x1_bf = _matmul(merged, W["w_out"], "nn", name="out_ln", tm=1024, tn=1024, tk=1024, epi="ln",
                            extra=(x, W["ln1_g"], W["ln1_b"]))
    act = _matmul(x1_bf, W["w_up"], "nn", name="up", tm=1024, tn=4096, tk=1024, epi="relu2")
    u2, x2, x2_bf = _matmul(act, W["w_down"], "nn", name="down_ln", tm=512, tn=1024, tk=4096, epi="ln",
                            extra=(x1, W["ln2_g"], W["ln2_b"]))
    saved = dict(x_bf=x_bf, proj=proj, retg=retg, states=states, sb=sb, sb_a=sb_a, sg=sg, merged=merged, br=br, u1=u1,
                 x1_bf=x1_bf, act=act, u2=u2)
    return x2, x2_bf, saved


def _layer_bwd(d_x2, W, tables, sv, chunk_dtype=None, sb_comm_fn=None, dwin_comm_fn=None, dx_comm_fn=None, target=None):
    dt = F32 if chunk_dtype is None else chunk_dtype
    rows, cols = (None, None) if chunk_dtype is None else ("rows", "cols")
    g, landed = {}, {}
    du2, du2_bf, g["ln2_g"], g["ln2_b"], *sq = _ln_bwd(d_x2, sv["u2"], W["ln2_g"], target=target)
    if sq:
        landed["sq"] = sq[0]
    d_hpre = _matmul(du2_bf, W["w_down"], "nt", name="d_act", tm=1024, tn=2048, tk=1024, epi="drelu2",
                     extra=(sv["act"],), out_dtype=BF16, b_resident=True)
    g["w_down"] = _matmul(sv["act"], du2_bf, "tn", name="dw_down", tm=1024, tn=1024, tk=4096, out_dtype=dt, chunks=rows)
    g["w_up"] = _matmul(sv["x1_bf"], d_hpre, "tn", name="dw_up", tm=1024, tn=512, tk=4096, out_dtype=dt, chunks=cols)
    d_x1 = _matmul(d_hpre, W["w_up"], "nt", name="d_x1", tm=512, tn=1024, tk=4096, epi="add", extra=(du2,))
    du1, du1_bf, g["ln1_g"], g["ln1_b"] = _ln_bwd(d_x1, sv["u1"], W["ln1_g"])
    d_merged = _matmul(du1_bf, W["w_out"], "nt", name="d_merged", tm=1024, tn=1024, tk=1024)
    g["w_out"] = _matmul(sv["merged"], du1_bf, "tn", name="dw_out", tm=1024, tn=512, tk=4096, out_dtype=dt, chunks=rows)
    d_br, d_gate = _gate_bwd(sv["proj"], sv["br"], d_merged)
    d_branch = []
    for k, (nm, act) in enumerate((("p_ret", sv["retg"]), ("p_sb", sv["sb"]), ("p_sgu", sv["sg"]))):
        d_branch.append(_matmul(d_br[k], W[nm], "nt", name="d_" + nm[2:], tm=1024, tn=512, tk=1024))
        g[nm] = _matmul(act, d_br[k], "tn", name="dw_" + nm[2:], tm=512, tn=1024, tk=2048, out_dtype=dt, chunks=cols)
    d_ret, g["ret_gn_g"], g["ret_gn_b"] = _ret_bwd(sv["proj"], tables, W["ret_gn_g"], W["ret_gn_b"], sv["states"],
                                                   d_branch[0])
    if sb_comm_fn is None:
        d_sq, d_sk, d_sv = _sb_bwd(sv["proj"], sv["sb_a"], d_branch[1])
    else:
        (d_sq, d_sk, d_sv), landed["sb"] = _sb_bwd(sv["proj"], sv["sb_a"], d_branch[1], comm=sb_comm_fn(g))
    d_sgu, g["sgu_ln_g"], g["sgu_ln_b"], g["sgu_w"], g["sgu_b"] = _sgu_bwd(
        sv["proj"], W["sgu_ln_g"], W["sgu_ln_b"], W["sgu_w"], W["sgu_b"], d_branch[2])
    d_proj = [d_ret, d_sq, d_sk, d_sv, d_sgu, d_gate[0], d_gate[1], d_gate[2]]
    g["w_in"] = _matmul(d_proj, sv["x_bf"], "tn", name="dw_in", tm=256, tn=1024, tk=4096, out_dtype=dt, chunks=rows,
                        comm=None if dwin_comm_fn is None else dwin_comm_fn(g))
    if dwin_comm_fn is not None:
        g["w_in"], landed["dwin"] = g["w_in"]
    if chunk_dtype is None:
        g["w_in"] = g["w_in"].T
    d_x = _matmul_rows_of(d_proj, W["w_in_t"], du1, name="d_x", tm=512,
                          comm=None if dx_comm_fn is None else dx_comm_fn(g))
    if dx_comm_fn is not None:
        d_x, landed["dx"] = d_x
    return d_x, g, landed


BIG = ("w_in", "p_ret", "p_sb", "p_sgu", "w_out", "w_up", "w_down")
SMALL = ("ret_gn_g", "ret_gn_b", "sgu_ln_g", "sgu_ln_b", "sgu_w", "sgu_b", "ln1_g", "ln1_b", "ln2_g", "ln2_b")
GATHER_KIND = {"w_in": "rows", "p_ret": "cols", "p_sb": "cols", "p_sgu": "cols", "w_out": "rows", "w_up": "cols",
               "w_down": "rows"}


def _small_weights(small, l):
    W = {}
    for n in SMALL:
        if n == "sgu_w":
            W[n] = small[n][l]
        elif n == "sgu_b":
            W[n] = small[n][l].reshape(4, CHUNK, 1)
        else:
            W[n] = small[n][l].reshape(1, -1)
    return W


def _local_step(x, target, full, small):
    tables = _ret_tables(x.shape[0])
    Ws = [{**{n: full[n][l] for n in BIG[1:]}, "w_in_t": full["w_in"][l].T, **_small_weights(small, l)}
          for l in range(DEPTH)]
    saved = []
    h, h_bf = x, _bf(x)
    for l in range(DEPTH):
        h, h_bf, sv = _layer_fwd(h, h_bf, Ws[l], tables)
        saved.append(sv)
    grads = [None] * DEPTH
    d_h, grads[-1], landed = _layer_bwd(h, Ws[-1], tables, saved[-1], target=target)
    for l in reversed(range(DEPTH - 1)):
        d_h, grads[l], _ = _layer_bwd(d_h, Ws[l], tables, saved[l])
    return landed["sq"], d_h, grads


def _adam(w, parts, m, v, name):
    L, R, C = w.shape
    tr = next(t for t in (320, 256, 128) if R % t == 0)
    assert len(parts) == L

    def body(*refs):
        w_ref, p_refs, (m_ref, v_ref, g_ref, d_ref, nm_ref, nv_ref) = refs[0], refs[1:1 + L], refs[1 + L:]
        layer = pl.program_id(0)
        g = None
        for li, p_ref in enumerate(p_refs):
            s = p_ref[0].astype(F32)
            for j in range(1, p_ref.shape[0]):
                s = s + p_ref[j].astype(F32)
            g = s if g is None else jnp.where(layer == li, s, g)
        g_ref[...] = g
        d_ref[...], nm_ref[...], nv_ref[...] = _adam_update(w_ref[...], g, m_ref[...], v_ref[...])

    tile = pl.BlockSpec((None, tr, C), lambda l, i: (l, i, 0))
    part = lambda li: pl.BlockSpec((parts[li].shape[0], tr, C), lambda l, i, li=li: (0, jnp.where(l == li, i, 0), 0))
    out = jax.ShapeDtypeStruct((L, R, C), F32)
    return pl.pallas_call(
        body, name=name, grid=(L, R // tr), out_shape=(out, out, out, out),
        in_specs=[tile] + [part(li) for li in range(L)] + [tile, tile],
        out_specs=(tile, tile, tile, tile),
        compiler_params=_cparams(("parallel", "parallel")),
    )(w, *parts, m, v)


def _adam_update(w, g, m, v):
    m2 = ADAM_B1 * m + (1.0 - ADAM_B1) * g
    v2 = ADAM_B2 * v + (1.0 - ADAM_B2) * (g * g)
    m_hat = m2 / (1.0 - ADAM_B1 ** ADAM_STEP)
    v_hat = v2 / (1.0 - ADAM_B2 ** ADAM_STEP)
    return -ADAM_LR * (m_hat / (jnp.sqrt(v_hat) + ADAM_EPS) + ADAM_WD * w), m2, v2


def _adam_small(w, m, v, parts):
    k = len(SMALL)

    def body(*refs):
        w_refs, m_refs, v_refs, p_refs, outs = refs[:k], refs[k:2 * k], refs[2 * k:3 * k], refs[3 * k:5 * k], refs[5 * k:]
        for i in range(k):
            vector = len(w_refs[i].shape) == 2
            for l in range(DEPTH):
                p_ref = p_refs[DEPTH * i + l]
                g = p_ref[0]
                for j in range(1, N_DEV):
                    g = g + p_ref[j]
                at = (slice(l, l + 1), slice(None)) if vector else (l,)
                delta, m2, v2 = _adam_update(w_refs[i][at], g, m_refs[i][at], v_refs[i][at])
                for o_ref, val in zip(outs[4 * i:4 * i + 4], (g, delta, m2, v2)):
                    o_ref[at] = val

    vmem = pl.BlockSpec(memory_space=pltpu.VMEM)
    args = [w[n] for n in SMALL] + [m[n] for n in SMALL] + [v[n] for n in SMALL] + \
           [parts[(n, l)] for n in SMALL for l in range(DEPTH)]
    out_shape = [jax.ShapeDtypeStruct(w[n].shape, F32) for n in SMALL for _ in range(4)]
    outs = pl.pallas_call(body, name="adam_small", out_shape=out_shape, in_specs=[vmem] * len(args),
                          out_specs=[vmem] * len(out_shape), compiler_params=_cparams())(*args)
    return {n: tuple(outs[4 * i:4 * i + 4]) for i, n in enumerate(SMALL)}


WEIGHTS = ("w_in", "ret_gn_g", "ret_gn_b", "sgu_ln_g", "sgu_ln_b", "sgu_w", "sgu_b", "p_ret", "p_sb", "p_sgu", "w_out",
           "ln1_g", "ln1_b", "w_up", "w_down", "ln2_g", "ln2_b")


def kernel(x, w_in, ret_gn_g, ret_gn_b, sgu_ln_g, sgu_ln_b, sgu_w, sgu_b, p_ret, p_sb, p_sgu, w_out, ln1_g, ln1_b, w_up, w_down, ln2_g, ln2_b, loss_target, m_w_in, m_ret_gn_g, m_ret_gn_b, m_sgu_ln_g, m_sgu_ln_b, m_sgu_w, m_sgu_b, m_p_ret, m_p_sb, m_p_sgu, m_w_out, m_ln1_g, m_ln1_b, m_w_up, m_w_down, m_ln2_g, m_ln2_b, v_w_in, v_ret_gn_g, v_ret_gn_b, v_sgu_ln_g, v_sgu_ln_b, v_sgu_w, v_sgu_b, v_p_ret, v_p_sb, v_p_sgu, v_w_out, v_ln1_g, v_ln1_b, v_w_up, v_w_down, v_ln2_g, v_ln2_b):
    w = dict(zip(WEIGHTS, (w_in, ret_gn_g, ret_gn_b, sgu_ln_g, sgu_ln_b, sgu_w, sgu_b, p_ret, p_sb, p_sgu, w_out,
                           ln1_g, ln1_b, w_up, w_down, ln2_g, ln2_b)))
    m = dict(zip(WEIGHTS, (m_w_in, m_ret_gn_g, m_ret_gn_b, m_sgu_ln_g, m_sgu_ln_b, m_sgu_w, m_sgu_b, m_p_ret, m_p_sb,
                           m_p_sgu, m_w_out, m_ln1_g, m_ln1_b, m_w_up, m_w_down, m_ln2_g, m_ln2_b)))
    v = dict(zip(WEIGHTS, (v_w_in, v_ret_gn_g, v_ret_gn_b, v_sgu_ln_g, v_sgu_ln_b, v_sgu_w, v_sgu_b, v_p_ret, v_p_sb,
                           v_p_sgu, v_w_out, v_ln1_g, v_ln1_b, v_w_up, v_w_down, v_ln2_g, v_ln2_b)))

    small = {n: w[n] for n in SMALL}
    shard = {n: _bf(w[n]) for n in BIG}
    shard["w_in"] = shard["w_in"].transpose(0, 2, 1)
    S = x.shape[1]
    x0, target = x.reshape(S, D_MODEL), loss_target.reshape(S, D_MODEL)
    tables = _ret_tables(S)
    Ws = [_small_weights(small, l) for l in range(DEPTH)]

    (Ws[0]["w_in_t"],) = _exchange([_gather_transfer(shard["w_in"], 0, "rows")], "gather_w_in0", relay=True)
    def gather_under_sb(keys):
        def landed_fn(landed):
            for (n, l), z in zip(keys, landed):
                Ws[l]["w_in_t" if n == "w_in" else n] = z

        return _Comm([_gather_transfer(shard[n], l, GATHER_KIND[n]) for n, l in keys], relay=True), landed_fn

    h, h_bf, saved0 = _layer_fwd(x0, _bf(x0), Ws[0], tables,
                                 sb_comm=gather_under_sb([(n, 0) for n in BIG[1:]] + [("w_in", 1)]))
    h, _, saved1 = _layer_fwd(h, h_bf, Ws[1], tables, sb_comm=gather_under_sb([(n, 1) for n in BIG[1:]]))
    d_h, g1, landed1 = _layer_bwd(h, Ws[1], tables, saved1, chunk_dtype=BF16, target=target,
                                  sb_comm_fn=lambda g: _Comm([_scatter_transfer(g[n]) for n in BIG[1:]]))
    loss = lax.psum(0.5 * jnp.sum(landed1["sq"]) / D_MODEL, ("x", "y", "c"))
    early = [("w_in", 1)] + [(n, 0) for n in BIG[1:]]

    def small_slabs(g):
        return [_slab_transfer(g[n].reshape(4, CHUNK) if n == "sgu_b" else g[n]) for n in SMALL]

    def early_scatter(g0):
        return _Comm([_scatter_transfer((g1 if l else g0)[n]) for n, l in early] + small_slabs(g1))

    def late_scatter(g0):
        pairs = _pair_reduce(g0["w_in"], "w_in0_pairs")
        return _Comm([_chip_scatter_transfer(pairs)])

    d_x, g0, landed = _layer_bwd(d_h, Ws[0], tables, saved0, chunk_dtype=BF16, sb_comm_fn=early_scatter,
                                 dwin_comm_fn=lambda g: _Comm(small_slabs(g)), dx_comm_fn=late_scatter)
    parts = {**dict(zip(early, landed["sb"])), **{(n, 1): z for n, z in zip(BIG[1:], landed1["sb"])}}
    parts[("w_in", 0)] = landed["dx"][0]
    small_parts = {**{(n, 1): z for n, z in zip(SMALL, landed["sb"][len(early):])},
                   **{(n, 0): z for n, z in zip(SMALL, landed["dwin"])}}

    grad, delta, new_m, new_v = {}, {}, {}, {}
    for n in BIG:
        view = (lambda a: a.transpose(0, 2, 1)) if n == "w_in" else (lambda a: a)
        res = _adam(view(w[n]), [parts[(n, l)] for l in range(DEPTH)], view(m[n]), view(v[n]), "adam_" + n)
        grad[n], delta[n], new_m[n], new_v[n] = (view(r) for r in res)
    for n, res in _adam_small(small, m, v, small_parts).items():
        grad[n], delta[n], new_m[n], new_v[n] = res

    return (loss, d_x.reshape(x.shape), *[grad[n] for n in WEIGHTS], *[delta[n] for n in WEIGHTS],
            *[new_m[n] for n in WEIGHTS], *[new_v[n] for n in WEIGHTS])
```
